```python
import jax, jax.numpy as jnp
from jax import lax
import numpy as np

D_MODEL = 1024
BATCH = 4
SEQ = 4096
DEPTH = 2

GLA_HEADS = 4
GLA_KEY = D_MODEL // 4
GLA_VAL = D_MODEL // 2
GLA_DK = GLA_KEY // GLA_HEADS
GLA_DV = GLA_VAL // GLA_HEADS
GLA_RANK = 16
GLA_TAU = 16.0
GLA_CHUNK = 64
SGU_GROUPS = 4
SGU_WIDTH = D_MODEL // 4
SGU_GD = SGU_WIDTH // SGU_GROUPS
SGU_CHUNK = 128
POOL_WINDOWS = (2, 4, 8, 16)
POOL_WIDTH = D_MODEL // 4
POOL_GD = POOL_WIDTH // len(POOL_WINDOWS)
N_BRANCH = 3
IN_SIZES = (GLA_KEY, GLA_KEY, GLA_VAL, GLA_VAL, GLA_RANK, 2 * SGU_WIDTH, POOL_WIDTH, N_BRANCH * D_MODEL)
IN_SPLITS = tuple(sum(IN_SIZES[:i + 1]) for i in range(len(IN_SIZES) - 1))
N_IN = sum(IN_SIZES)
MEM_LEN = 256
XA_HEADS = 4
XA_DH = D_MODEL // XA_HEADS
N_EXPERTS = 32
TOP_K = 4
EXPERT_FF = D_MODEL
SWIGLU_LIMIT = 7.0
SWIGLU_ALPHA = 1.702
MOE_BLOCK = 128
DEEPNORM_ALPHA = (2 * DEPTH) ** 0.25
DEEPNORM_BETA = (8 * DEPTH) ** -0.25
LN_EPS = 1e-5

kernel_name = 'hybrid_gla_sgu_pool_moe_block'


def layer_norm(x, g, b):
    xf = x.astype(jnp.float32)
    mu = xf.mean(-1, keepdims=True)
    var = jnp.square(xf - mu).mean(-1, keepdims=True)
    return ((xf - mu) * lax.rsqrt(var + LN_EPS) * g + b).astype(x.dtype)


def rms_norm(x, g):
    xf = x.astype(jnp.float32)
    return xf * lax.rsqrt(jnp.square(xf).mean(-1, keepdims=True) + LN_EPS) * g


def gla_chunked(q, k, v, log_a):
    B, S, H, dk = q.shape
    dv = v.shape[-1]
    C = GLA_CHUNK
    N = S // C
    f32 = jnp.float32
    q = q.astype(f32).reshape(B, N, C, H, dk) * (dk ** -0.5)
    k = k.astype(f32).reshape(B, N, C, H, dk)
    v = v.astype(f32).reshape(B, N, C, H, dv)
    b = jnp.cumsum(log_a.astype(f32).reshape(B, N, C, H, dk), axis=2)
    b_last = b[:, :, -1:]
    q_dec = q * jnp.exp(b)
    k_dec = k * jnp.exp(-b)
    k_tail = k * jnp.exp(b_last - b)
    causal = jnp.tril(jnp.ones((C, C), bool))
    scores = jnp.einsum('bnthd,bnshd->bnhts', q_dec, k_dec)
    scores = jnp.where(causal, scores, 0.0)
    o_intra = jnp.einsum('bnhts,bnshe->bnthe', scores, v)
    chunk_kv = jnp.einsum('bnshd,bnshe->nbhde', k_tail, v)
    chunk_decay = jnp.exp(jnp.moveaxis(b_last[:, :, 0], 1, 0))

    def step(state, inp):
        dec, kv = inp
        return dec[..., None] * state + kv, state

    _, s_in = lax.scan(step, jnp.zeros((B, H, dk, dv), f32), (chunk_decay, chunk_kv))
    o_inter = jnp.einsum('bnthd,nbhde->bnthe', q_dec, s_in)
    return (o_intra + o_inter).reshape(B, S, H, dv)


def spatial_gating(uv_pre, ln_g, ln_b, ws, bs):
    B, S, _ = uv_pre.shape
    z = jax.nn.gelu(uv_pre, approximate=False)
    u, v = jnp.split(z, 2, axis=-1)
    v = layer_norm(v, ln_g, ln_b)
    N = S // SGU_CHUNK
    v = v.reshape(B, N, SGU_CHUNK, SGU_GROUPS, SGU_GD)
    ws_causal = jnp.tril(ws)
    s = jnp.einsum('gts,bnsgd->bntgd', ws_causal, v) + bs.T[:, :, None]
    return u * s.reshape(B, S, SGU_WIDTH)


def pool_mixer(xc, pool_w, pool_scale):
    B, S, _ = xc.shape
    G = len(POOL_WINDOWS)
    xg = xc.astype(jnp.float32).reshape(B, S, G, POOL_GD)
    csum = jnp.cumsum(xg, axis=1)
    t = jnp.arange(S)
    pooled = []
    for i, w in enumerate(POOL_WINDOWS):
        c_i = csum[:, :, i]
        c_prev = jnp.pad(c_i, ((0, 0), (w, 0), (0, 0)))[:, :S]
        count = jnp.minimum(t + 1, w).astype(jnp.float32)[None, :, None]
        pooled.append((c_i - c_prev) / count)
    y = jnp.stack(pooled, axis=2) - xg
    y = jnp.einsum('bsgc,gcd->bsgd', y, pool_w.astype(jnp.float32))
    return (y.reshape(B, S, POOL_WIDTH) * pool_scale).astype(xc.dtype)


def hybrid_mixer(x, w_in, b_in, gla_wg2, gla_bg, gla_norm_g, sgu_ln_g, sgu_ln_b, sgu_ws, sgu_bs,
                 pool_w, pool_scale, w_up_a, w_up_b, w_up_c, w_o):
    B, S, D = x.shape
    p = x @ w_in + b_in
    q, k, v, r, g_low, uv, xc, gate_pre = jnp.split(p, IN_SPLITS, axis=-1)
    log_a = jax.nn.log_sigmoid((g_low @ gla_wg2 + gla_bg).astype(jnp.float32)) / GLA_TAU
    o = gla_chunked(q.reshape(B, S, GLA_HEADS, GLA_DK), k.reshape(B, S, GLA_HEADS, GLA_DK),
                    v.reshape(B, S, GLA_HEADS, GLA_DV), log_a.reshape(B, S, GLA_HEADS, GLA_DK))
    o = rms_norm(o, gla_norm_g.reshape(GLA_HEADS, GLA_DV))
    y_a = o.reshape(B, S, GLA_VAL).astype(x.dtype) * jax.nn.silu(r)
    y_b = spatial_gating(uv, sgu_ln_g, sgu_ln_b, sgu_ws, sgu_bs)
    y_c = pool_mixer(xc, pool_w, pool_scale)
    gates = jax.nn.sigmoid(gate_pre.reshape(B, S, N_BRANCH, D))
    merged = (gates[:, :, 0] * (y_a @ w_up_a) + gates[:, :, 1] * (y_b @ w_up_b)
              + gates[:, :, 2] * (y_c @ w_up_c))
    return merged @ w_o


def memory_cross_attention(x, mem, wq, wk, wv, wo):
    B, S, D = x.shape
    M = mem.shape[1]
    q = (x @ wq).reshape(B, S, XA_HEADS, XA_DH)
    k = (mem @ wk).reshape(B, M, XA_HEADS, XA_DH)
    v = (mem @ wv).reshape(B, M, XA_HEADS, XA_DH)
    s = jnp.einsum('bshd,bmhd->bhsm', q, k).astype(jnp.float32) * (XA_DH ** -0.5)
    p = jax.nn.softmax(s, axis=-1).astype(v.dtype)
    o = jnp.einsum('bhsm,bmhd->bshd', p, v).reshape(B, S, D)
    return o @ wo


def moe_ffn(x, router_w, router_b, w_gu, b_gu, w_down, b_down):
    B, S, D = x.shape
    x2 = x.reshape(-1, D)
    N = x2.shape[0]
    logits = (x2 @ router_w).astype(jnp.float32) + router_b
    top_vals, top_idx = lax.top_k(logits, TOP_K)
    gate = jax.nn.softmax(top_vals, axis=-1)
    A = N * TOP_K
    flat_e = top_idx.reshape(-1)
    flat_tok = jnp.repeat(jnp.arange(N, dtype=jnp.int32), TOP_K)
    flat_g = gate.reshape(-1)
    order = jnp.argsort(flat_e)
    se, st, sg = flat_e[order], flat_tok[order], flat_g[order]
    counts = jnp.bincount(flat_e, length=N_EXPERTS)
    padded = ((counts + MOE_BLOCK - 1) // MOE_BLOCK) * MOE_BLOCK
    pad_end = jnp.cumsum(padded)
    pad_start = pad_end - padded
    start = jnp.cumsum(counts) - counts
    dest = pad_start[se] + (jnp.arange(A) - start[se])
    n_blocks = -(-A // MOE_BLOCK) + N_EXPERTS
    slot_tok = jnp.zeros((n_blocks * MOE_BLOCK,), jnp.int32).at[dest].set(st)
    slot_gate = jnp.zeros((n_blocks * MOE_BLOCK,), jnp.float32).at[dest].set(sg)
    block_expert = jnp.minimum(jnp.searchsorted(pad_end, jnp.arange(n_blocks) * MOE_BLOCK, side='right'),
                               N_EXPERTS - 1)

    def expert_block(args):
        e, tok, g = args
        h = x2[tok] @ w_gu[e] + b_gu[e]
        h_glu = jnp.minimum(h[:, :EXPERT_FF], SWIGLU_LIMIT)
        h_lin = jnp.clip(h[:, EXPERT_FF:], -SWIGLU_LIMIT, SWIGLU_LIMIT)
        a = h_glu * jax.nn.sigmoid(SWIGLU_ALPHA * h_glu) * (h_lin + 1.0)
        out = a @ w_down[e] + b_down[e]
        return out * g.astype(out.dtype)[:, None]

    ys = lax.map(expert_block, (block_expert, slot_tok.reshape(n_blocks, MOE_BLOCK),
                                slot_gate.reshape(n_blocks, MOE_BLOCK)))
    y = jnp.zeros_like(x2).at[slot_tok].add(ys.reshape(-1, D).astype(x2.dtype))
    return y.reshape(B, S, D)


def setup_inputs(seed: int = 0) -> dict:
    key = jax.random.key(seed)
    ks = iter(jax.random.split(key, 40))
    f32 = jnp.float32
    L, D = DEPTH, D_MODEL
    beta = DEEPNORM_BETA

    def nrm(shape, scale):
        return jax.random.normal(next(ks), shape, f32) * scale

    return {
        'x': nrm((BATCH, SEQ, D), 1.0),
        'mem': nrm((BATCH, MEM_LEN, D), 1.0),
        'w_in': nrm((L, D, N_IN), D ** -0.5),
        'b_in': nrm((L, N_IN), 0.02),
        'gla_wg2': nrm((L, GLA_RANK, GLA_KEY), GLA_RANK ** -0.5),
        'gla_bg': nrm((L, GLA_KEY), 0.1),
        'gla_norm_g': 1.0 + nrm((L, GLA_VAL), 0.02),
        'sgu_ln_g': 1.0 + nrm((L, SGU_WIDTH), 0.02),
        'sgu_ln_b': nrm((L, SGU_WIDTH), 0.02),
        'sgu_ws': nrm((L, SGU_GROUPS, SGU_CHUNK, SGU_CHUNK), SGU_CHUNK ** -0.5),
        'sgu_bs': 1.0 + nrm((L, SGU_GROUPS, SGU_CHUNK), 0.1),
        'pool_w': nrm((L, len(POOL_WINDOWS), POOL_GD, POOL_GD), POOL_GD ** -0.5),
        'pool_scale': 1.0 + nrm((L, POOL_WIDTH), 0.1),
        'w_up_a': nrm((L, GLA_VAL, D), beta * GLA_VAL ** -0.5),
        'w_up_b': nrm((L, SGU_WIDTH, D), beta * SGU_WIDTH ** -0.5),
        'w_up_c': nrm((L, POOL_WIDTH, D), beta * POOL_WIDTH ** -0.5),
        'w_o': nrm((L, D, D), beta * D ** -0.5),
        'ln1_g': 1.0 + nrm((L, D), 0.02),
        'ln1_b': nrm((L, D), 0.02),
        'xa_wq': nrm((L, D, D), D ** -0.5),
        'xa_wk': nrm((L, D, D), D ** -0.5),
        'xa_wv': nrm((L, D, D), beta * D ** -0.5),
        'xa_wo': nrm((L, D, D), beta * D ** -0.5),
        'ln2_g': 1.0 + nrm((L, D), 0.02),
        'ln2_b': nrm((L, D), 0.02),
        'router_w': nrm((L, D, N_EXPERTS), D ** -0.5),
        'router_b': nrm((L, N_EXPERTS), 0.01),
        'exp_w_gu': nrm((L, N_EXPERTS, D, 2 * EXPERT_FF), beta * D ** -0.5),
        'exp_b_gu': nrm((L, N_EXPERTS, 2 * EXPERT_FF), 0.02),
        'exp_w_down': nrm((L, N_EXPERTS, EXPERT_FF, D), beta * EXPERT_FF ** -0.5),
        'exp_b_down': nrm((L, N_EXPERTS, D), 0.02),
        'ln3_g': 1.0 + nrm((L, D), 0.02),
        'ln3_b': nrm((L, D), 0.02),
    }


def reference(x, mem, w_in, b_in, gla_wg2, gla_bg, gla_norm_g, sgu_ln_g, sgu_ln_b, sgu_ws, sgu_bs,
              pool_w, pool_scale, w_up_a, w_up_b, w_up_c, w_o, ln1_g, ln1_b,
              xa_wq, xa_wk, xa_wv, xa_wo, ln2_g, ln2_b,
              router_w, router_b, exp_w_gu, exp_b_gu, exp_w_down, exp_b_down, ln3_g, ln3_b):
    alpha = DEEPNORM_ALPHA
    for l in range(DEPTH):
        h = hybrid_mixer(x, w_in[l], b_in[l], gla_wg2[l], gla_bg[l], gla_norm_g[l], sgu_ln_g[l], sgu_ln_b[l],
                         sgu_ws[l], sgu_bs[l], pool_w[l], pool_scale[l], w_up_a[l], w_up_b[l], w_up_c[l], w_o[l])
        x = layer_norm(alpha * x + h, ln1_g[l], ln1_b[l])
        h = memory_cross_attention(x, mem, xa_wq[l], xa_wk[l], xa_wv[l], xa_wo[l])
        x = layer_norm(alpha * x + h, ln2_g[l], ln2_b[l])
        h = moe_ffn(x, router_w[l], router_b[l], exp_w_gu[l], exp_b_gu[l], exp_w_down[l], exp_b_down[l])
        x = layer_norm(alpha * x + h, ln3_g[l], ln3_b[l])
    return x
```

```python
import functools

import jax
import jax.numpy as jnp
from jax import lax
from jax.experimental import pallas as pl
from jax.experimental.pallas import tpu as pltpu

F32 = jnp.float32
BF16 = jnp.bfloat16

D_MODEL = 1024
DEPTH = 2
GLA_HEADS = 4
GLA_KEY = 256
GLA_VAL = 512
GLA_DK = 64
GLA_DV = 128
GLA_RANK = 16
GLA_TAU = 16.0
GLA_CHUNK = 64
SGU_GROUPS = 4
SGU_WIDTH = 256
SGU_GD = 64
SGU_CHUNK = 128
POOL_WINDOWS = (2, 4, 8, 16)
POOL_WIDTH = 256
POOL_GD = 64
POOL_CARRY = 32
MEM_LEN = 256
XA_HEADS = 4
XA_DH = 256
N_EXPERTS = 32
TOP_K = 4
EXPERT_FF = 1024
SWIGLU_LIMIT = 7.0
SWIGLU_ALPHA = 1.702
DEEPNORM_ALPHA = (2 * DEPTH) ** 0.25
LN_EPS = 1e-5
LANES = 128
VMEM_LIMIT = 56 * 1024 * 1024

MIX_TILE = 256
XA_TILE = 512
MOE_BLOCK = 256
CMB_TILE = 256


def _dot(a, b):
    return jnp.dot(a, b, preferred_element_type=F32)


def _dot_t0(a, b):
    return lax.dot_general(a, b, (((0,), (0,)), ((), ())), preferred_element_type=F32)


def _dot_t1(a, b):
    return lax.dot_general(a, b, (((1,), (1,)), ((), ())), preferred_element_type=F32)


def _split_bf16(x):
    hi = x.astype(BF16)
    lo = (x - hi.astype(F32)).astype(BF16)
    return hi, lo


def _layer_norm(x, g, b):
    mu = jnp.mean(x, axis=-1, keepdims=True)
    xc = x - mu
    var = jnp.mean(xc * xc, axis=-1, keepdims=True)
    return xc * lax.rsqrt(var + LN_EPS) * g + b


def _sigmoid(x):
    return 1.0 / (1.0 + jnp.exp(-x))


def _const_spec(shape):
    nd = len(shape)
    return pl.BlockSpec(shape, lambda *_: (0,) * nd, pipeline_mode=pl.Buffered(1))


def _mixer_kernel(x_ref, wqkvr_ref, bqkvr_ref, wglow_ref, bglow_ref, wg2_ref, bg_ref, gnorm_ref,
                  wuv_ref, buv_ref, slng_ref, slnb_ref, wtril_ref, sbias_ref,
                  wxc_ref, bxc_ref, poolw_ref, pscale_ref,
                  wgate_ref, bgate_ref, wupa_ref, wupb_ref, wupc_ref, wo_ref, ln1g_ref, ln1b_ref,
                  out_ref,
                  state_ref, qkvr_ref, la_ref, ya_ref, vln_ref, e_ref, s2_ref, s4_ref, s8_ref):
    T = MIX_TILE
    j = pl.program_id(1)
    x = x_ref[0]
    xb = x.astype(BF16)

    @pl.when(j == 0)
    def _():
        state_ref[...] = jnp.zeros_like(state_ref)
        e_ref[0:POOL_CARRY, :] = jnp.zeros((POOL_CARRY, POOL_WIDTH), F32)

    @pl.when(j > 0)
    def _():
        e_ref[0:POOL_CARRY, :] = e_ref[T:T + POOL_CARRY, :]

    qkvr_ref[...] = _dot(xb, wqkvr_ref[...]) + bqkvr_ref[...]
    glow = _dot(xb, wglow_ref[...]) + bglow_ref[...]
    z = _dot(glow.astype(BF16), wg2_ref[...]) + bg_ref[...]
    la_ref[...] = (jnp.minimum(z, 0.0) - jnp.log1p(jnp.exp(-jnp.abs(z)))) * (1.0 / GLA_TAU)

    C = GLA_CHUNK
    row = lax.broadcasted_iota(jnp.int32, (C, C), 0)
    col = lax.broadcasted_iota(jnp.int32, (C, C), 1)
    tril = row >= col
    tril_bf = jnp.where(tril, 1.0, 0.0).astype(BF16)
    row4 = lax.broadcasted_iota(jnp.int32, (GLA_HEADS * C, C), 0) & (C - 1)
    tril4 = row4 >= lax.broadcasted_iota(jnp.int32, (GLA_HEADS * C, C), 1)
    ones_bf = jnp.ones((C, GLA_DV), BF16)
    lane = lax.broadcasted_iota(jnp.int32, (C, GLA_KEY), 1)
    head_masks = [(lane >= h * GLA_DK) & (lane < (h + 1) * GLA_DK) for h in range(GLA_HEADS)]

    def gla_chunk(c, carry):
        r0 = pl.multiple_of(c * C, C)
        la = la_ref[pl.ds(r0, C), :]
        la_hi, la_lo = _split_bf16(la)
        b = _dot(tril_bf, la_hi) + _dot(tril_bf, la_lo)
        b_last = b[C - 1:C, :]
        q = qkvr_ref[pl.ds(r0, C), 0:GLA_KEY]
        k = qkvr_ref[pl.ds(r0, C), GLA_KEY:2 * GLA_KEY]
        v = qkvr_ref[pl.ds(r0, C), 2 * GLA_KEY:2 * GLA_KEY + GLA_VAL].astype(BF16)
        q_dec = q * (GLA_DK ** -0.5) * jnp.exp(b)
        k_dec = (k * jnp.exp(-b)).astype(BF16)
        k_tail = (k * jnp.exp(b_last - b)).astype(BF16)
        q_dec_bf = q_dec.astype(BF16)
        q_stack = jnp.concatenate([jnp.where(m, q_dec, 0.0) for m in head_masks], axis=0).astype(BF16)
        scores = _dot_t1(q_stack, k_dec)
        scores = jnp.where(tril4, scores, 0.0).astype(BF16)
        o_intra = _dot(scores, v)
        o_inter = _dot(q_dec_bf, state_ref[...].astype(BF16))
        kv = _dot_t0(k_tail, v)
        dec = jnp.exp(_dot_t0(la_hi, ones_bf) + _dot_t0(la_lo, ones_bf))
        for h in range(GLA_HEADS):
            rs = slice(h * GLA_DK, (h + 1) * GLA_DK)
            cs = slice(h * GLA_DV, (h + 1) * GLA_DV)
            o_h = o_intra[h * C:(h + 1) * C, cs] + o_inter[:, cs]
            ms = jnp.mean(o_h * o_h, axis=-1, keepdims=True)
            o_h = o_h * lax.rsqrt(ms + LN_EPS) * gnorm_ref[:, cs]
            r_h = qkvr_ref[pl.ds(r0, C), 2 * GLA_KEY + GLA_VAL + h * GLA_DV:2 * GLA_KEY + GLA_VAL + (h + 1) * GLA_DV]
            ya_ref[pl.ds(r0, C), cs] = (o_h * (r_h * _sigmoid(r_h))).astype(BF16)
            state_ref[rs, cs] = dec[rs, :] * state_ref[rs, cs] + kv[rs, cs]
        return carry

    lax.fori_loop(0, T // C, gla_chunk, 0)

    uv = _dot(xb, wuv_ref[...]) + buv_ref[...]
    zg = 0.5 * uv * (1.0 + lax.erf(uv * (2.0 ** -0.5)))
    u = zg[:, :SGU_WIDTH]
    vln_ref[...] = _layer_norm(zg[:, SGU_WIDTH:], slng_ref[...], slnb_ref[...])
    lane_s = lax.broadcasted_iota(jnp.int32, (SGU_CHUNK, SGU_WIDTH), 1)
    s_parts = []
    for n in range(T // SGU_CHUNK):
        vc = vln_ref[n * SGU_CHUNK:(n + 1) * SGU_CHUNK, :]
        s = sbias_ref[...]
        for g in range(SGU_GROUPS):
            vg = jnp.where((lane_s >= g * SGU_GD) & (lane_s < (g + 1) * SGU_GD), vc, 0.0).astype(BF16)
            s = s + _dot(wtril_ref[g], vg)
        s_parts.append(s)
    y_b = (u * jnp.concatenate(s_parts, axis=0)).astype(BF16)

    P = POOL_CARRY
    xc = _dot(xb, wxc_ref[...]) + bxc_ref[...]
    e_ref[P:P + T, :] = xc
    s2_ref[8:P + T, :] = e_ref[8:P + T, :] + e_ref[7:P + T - 1, :]
    s4_ref[16:P + T, :] = s2_ref[16:P + T, :] + s2_ref[14:P + T - 2, :]
    s8_ref[24:P + T, :] = s4_ref[24:P + T, :] + s4_ref[20:P + T - 4, :]
    s16 = s8_ref[P:P + T, :] + s8_ref[P - 8:P + T - 8, :]
    lane_p = lax.broadcasted_iota(jnp.int32, (T, POOL_WIDTH), 1)
    tpos = lax.broadcasted_iota(jnp.int32, (T, POOL_WIDTH), 0) + (j * T + 1)
    grp = lane_p >> 6
    win = jnp.where(grp == 0, POOL_WINDOWS[0], jnp.where(grp == 1, POOL_WINDOWS[1],
                    jnp.where(grp == 2, POOL_WINDOWS[2], POOL_WINDOWS[3])))
    wsum = jnp.where(grp == 0, s2_ref[P:P + T, :], jnp.where(grp == 1, s4_ref[P:P + T, :],
                     jnp.where(grp == 2, s8_ref[P:P + T, :], s16)))
    count = jnp.minimum(tpos, win).astype(F32)
    pooled = wsum / count - xc
    y_c = (_dot(pooled.astype(BF16), poolw_ref[...]) * pscale_ref[...]).astype(BF16)

    D = D_MODEL
    merged = _sigmoid(_dot(xb, wgate_ref[:, 0:D]) + bgate_ref[:, 0:D]) * _dot(ya_ref[...], wupa_ref[...])
    merged += _sigmoid(_dot(xb, wgate_ref[:, D:2 * D]) + bgate_ref[:, D:2 * D]) * _dot(y_b, wupb_ref[...])
    merged += _sigmoid(_dot(xb, wgate_ref[:, 2 * D:3 * D]) + bgate_ref[:, 2 * D:3 * D]) * _dot(y_c, wupc_ref[...])
    h = _dot(merged.astype(BF16), wo_ref[...])
    out_ref[0] = _layer_norm(DEEPNORM_ALPHA * x + h, ln1g_ref[...], ln1b_ref[...])


def _mixer(x, p):
    B, S, D = x.shape
    T = MIX_TILE
    weights = [p['wqkvr'], p['bqkvr'], p['wglow'], p['bglow'], p['wg2'], p['bg'], p['gnorm'],
               p['wuv'], p['buv'], p['slng'], p['slnb'], p['wtril'], p['sbias'],
               p['wxc'], p['bxc'], p['poolw'], p['pscale'],
               p['wgate'], p['bgate'], p['wupa'], p['wupb'], p['wupc'], p['wo'], p['ln1g'], p['ln1b']]
    return pl.pallas_call(
        _mixer_kernel,
        out_shape=jax.ShapeDtypeStruct((B, S, D), F32),
        grid=(B, S // T),
        in_specs=[pl.BlockSpec((1, T, D), lambda b, j: (b, j, 0))] + [_const_spec(w.shape) for w in weights],
        out_specs=pl.BlockSpec((1, T, D), lambda b, j: (b, j, 0)),
        scratch_shapes=[
            pltpu.VMEM((GLA_KEY, GLA_VAL), F32),
            pltpu.VMEM((T, 2 * GLA_KEY + 2 * GLA_VAL), F32),
            pltpu.VMEM((T, GLA_KEY), F32),
            pltpu.VMEM((T, GLA_VAL), BF16),
            pltpu.VMEM((T, SGU_WIDTH), F32),
            pltpu.VMEM((T + POOL_CARRY, POOL_WIDTH), F32),
            pltpu.VMEM((T + POOL_CARRY, POOL_WIDTH), F32),
            pltpu.VMEM((T + POOL_CARRY, POOL_WIDTH), F32),
            pltpu.VMEM((T + POOL_CARRY, POOL_WIDTH), F32),
        ],
        compiler_params=pltpu.CompilerParams(dimension_semantics=("arbitrary", "arbitrary"),
                                             vmem_limit_bytes=VMEM_LIMIT),
        name="mixer",
    )(x, *weights)


def _memkv_kernel(memt_ref, mem_ref, wkt_ref, wv_ref, kt_ref, v_ref):
    kt_ref[0] = _dot(wkt_ref[...], memt_ref[0].astype(BF16)).astype(BF16)
    v_ref[0] = _dot(mem_ref[0].astype(BF16), wv_ref[...]).astype(BF16)


def _memkv(mem, p):
    B, M, D = mem.shape
    memt = jnp.swapaxes(mem, 1, 2)
    return pl.pallas_call(
        _memkv_kernel,
        out_shape=(jax.ShapeDtypeStruct((B, D, M), BF16), jax.ShapeDtypeStruct((B, M, D), BF16)),
        grid=(B,),
        in_specs=[pl.BlockSpec((1, D, M), lambda b: (b, 0, 0)), pl.BlockSpec((1, M, D), lambda b: (b, 0, 0)),
                  _const_spec((D, D)), _const_spec((D, D))],
        out_specs=(pl.BlockSpec((1, D, M), lambda b: (b, 0, 0)), pl.BlockSpec((1, M, D), lambda b: (b, 0, 0))),
        compiler_params=pltpu.CompilerParams(dimension_semantics=("arbitrary",), vmem_limit_bytes=VMEM_LIMIT),
        name="memkv",
    )(memt, mem, p['wkt'], p['wv'])


def _xattn_kernel(x_ref, kt_ref, v_ref, wq_ref, wo_ref, ln2g_ref, ln2b_ref, rwhi_ref, rwlo_ref, rb_ref,
                  x2_ref, x2b_ref, logit_ref):
    x = x_ref[0]
    q = (_dot(x.astype(BF16), wq_ref[...]) * (XA_DH ** -0.5)).astype(BF16)
    h = jnp.zeros_like(x)
    for hd in range(XA_HEADS):
        cs = slice(hd * XA_DH, (hd + 1) * XA_DH)
        s = _dot(q[:, cs], kt_ref[0, cs, :])
        e = jnp.exp(s - jnp.max(s, axis=-1, keepdims=True))
        o = _dot(e.astype(BF16), v_ref[0, :, cs]) / jnp.sum(e, axis=-1, keepdims=True)
        h = h + _dot(o.astype(BF16), wo_ref[cs, :])
    x2 = _layer_norm(DEEPNORM_ALPHA * x + h, ln2g_ref[...], ln2b_ref[...])
    x2_ref[0] = x2
    x2b_ref[0] = x2.astype(BF16)
    hi, lo = _split_bf16(x2)
    logit_ref[0] = _dot(hi, rwhi_ref[...]) + (_dot(hi, rwlo_ref[...]) + _dot(lo, rwhi_ref[...])) + rb_ref[...]


def _xattn(x, kt, v, p):
    B, S, D = x.shape
    T = XA_TILE
    M = MEM_LEN
    return pl.pallas_call(
        _xattn_kernel,
        out_shape=(jax.ShapeDtypeStruct((B, S, D), F32), jax.ShapeDtypeStruct((B, S, D), BF16),
                   jax.ShapeDtypeStruct((B, S, LANES), F32)),
        grid=(B, S // T),
        in_specs=[pl.BlockSpec((1, T, D), lambda b, j: (b, j, 0)),
                  pl.BlockSpec((1, D, M), lambda b, j: (b, 0, 0)),
                  pl.BlockSpec((1, M, D), lambda b, j: (b, 0, 0)),
                  _const_spec((D, D)), _const_spec((D, D)), _const_spec((1, D)), _const_spec((1, D)),
                  _const_spec((D, LANES)), _const_spec((D, LANES)), _const_spec((1, LANES))],
        out_specs=(pl.BlockSpec((1, T, D), lambda b, j: (b, j, 0)), pl.BlockSpec((1, T, D), lambda b, j: (b, j, 0)),
                   pl.BlockSpec((1, T, LANES), lambda b, j: (b, j, 0))),
        compiler_params=pltpu.CompilerParams(dimension_semantics=("arbitrary", "arbitrary"),
                                             vmem_limit_bytes=VMEM_LIMIT),
        name="xattn",
    )(x, kt, v, p['wq'], p['wxo'], p['ln2g'], p['ln2b'], p['rwhi'], p['rwlo'], p['rb'])


def _expert_kernel(be_ref, nb_ref, xs_ref, wgu_ref, bgu_ref, wd_ref, bd_ref, out_ref, wgu_bf, wd_bf):
    i = pl.program_id(0)

    @pl.when(i < nb_ref[0])
    def _():
        e = be_ref[i]
        prev = be_ref[jnp.maximum(i - 1, 0)]

        @pl.when((i == 0) | (e != prev))
        def _():
            wgu_bf[...] = wgu_ref[0].astype(BF16)
            wd_bf[...] = wd_ref[0].astype(BF16)

        hh = _dot(xs_ref[...], wgu_bf[...]) + bgu_ref[0]
        h_glu = jnp.minimum(hh[:, :EXPERT_FF], SWIGLU_LIMIT)
        h_lin = jnp.clip(hh[:, EXPERT_FF:], -SWIGLU_LIMIT, SWIGLU_LIMIT)
        a = h_glu * _sigmoid(SWIGLU_ALPHA * h_glu) * (h_lin + 1.0)
        out_ref[...] = _dot(a.astype(BF16), wd_bf[...]) + bd_ref[0]


def _experts(block_expert, n_used, xs, w_gu, b_gu, w_down, b_down):
    P, D = xs.shape
    NB = P // MOE_BLOCK
    F2 = 2 * EXPERT_FF

    def row_map(i, be, nb):
        return (jnp.minimum(i, nb[0] - 1), 0)

    def exp_map(i, be, nb):
        return (be[jnp.minimum(i, nb[0] - 1)], 0, 0)

    grid_spec = pltpu.PrefetchScalarGridSpec(
        num_scalar_prefetch=2,
        grid=(NB,),
        in_specs=[pl.BlockSpec((MOE_BLOCK, D), row_map),
                  pl.BlockSpec((1, D, F2), exp_map),
                  pl.BlockSpec((1, 1, F2), exp_map),
                  pl.BlockSpec((1, EXPERT_FF, D), exp_map),
                  pl.BlockSpec((1, 1, D), exp_map)],
        out_specs=pl.BlockSpec((MOE_BLOCK, D), row_map),
        scratch_shapes=[pltpu.VMEM((D, F2), BF16), pltpu.VMEM((EXPERT_FF, D), BF16)],
    )
    return pl.pallas_call(
        _expert_kernel,
        out_shape=jax.ShapeDtypeStruct((P, D), F32),
        grid_spec=grid_spec,
        compiler_params=pltpu.CompilerParams(dimension_semantics=("arbitrary",), vmem_limit_bytes=VMEM_LIMIT),
        name="experts",
    )(block_expert, n_used, xs, w_gu, b_gu.reshape(N_EXPERTS, 1, F2), w_down, b_down.reshape(N_EXPERTS, 1, D))


def _combine_kernel(x_ref, yg_ref, gate_ref, ln3g_ref, ln3b_ref, out_ref):
    x = x_ref[...]
    g = gate_ref[...]
    y = jnp.zeros_like(x)
    for k in range(TOP_K):
        y = y + g[:, k:k + 1] * yg_ref[:, k * D_MODEL:(k + 1) * D_MODEL]
    out_ref[...] = _layer_norm(DEEPNORM_ALPHA * x + y, ln3g_ref[...], ln3b_ref[...])


def _combine(x2, yg, gate, ln3g, ln3b):
    N, D = x2.shape
    T = CMB_TILE
    return pl.pallas_call(
        _combine_kernel,
        out_shape=jax.ShapeDtypeStruct((N, D), F32),
        grid=(N // T,),
        in_specs=[pl.BlockSpec((T, D), lambda i: (i, 0)), pl.BlockSpec((T, TOP_K * D), lambda i: (i, 0)),
                  pl.BlockSpec((T, TOP_K), lambda i: (i, 0)), _const_spec((1, D)), _const_spec((1, D))],
        out_specs=pl.BlockSpec((T, D), lambda i: (i, 0)),
        compiler_params=pltpu.CompilerParams(dimension_semantics=("arbitrary",), vmem_limit_bytes=VMEM_LIMIT),
        name="combine",
    )(x2, yg, gate, ln3g, ln3b)


def _prep_layer(w_in, b_in, gla_wg2, gla_bg, gla_norm_g, sgu_ln_g, sgu_ln_b, sgu_ws, sgu_bs, pool_w, pool_scale,
                w_up_a, w_up_b, w_up_c, w_o, ln1_g, ln1_b, xa_wq, xa_wk, xa_wv, xa_wo, ln2_g, ln2_b,
                router_w, router_b, ln3_g, ln3_b):
    o_qkvr = 2 * GLA_KEY + 2 * GLA_VAL
    o_glow = o_qkvr + GLA_RANK
    o_uv = o_glow + 2 * SGU_WIDTH
    o_xc = o_uv + POOL_WIDTH
    row = lambda a: a.reshape(1, -1).astype(F32)
    pad_lanes = lambda a: jnp.pad(a, ((0, 0), (0, LANES - a.shape[1])))
    p = {}
    p['wqkvr'] = w_in[:, :o_qkvr].astype(BF16)
    p['bqkvr'] = row(b_in[:o_qkvr])
    p['wglow'] = pad_lanes(w_in[:, o_qkvr:o_glow]).astype(BF16)
    p['bglow'] = pad_lanes(row(b_in[o_qkvr:o_glow]))
    p['wg2'] = jnp.pad(gla_wg2, ((0, LANES - GLA_RANK), (0, 0))).astype(BF16)
    p['bg'] = row(gla_bg)
    p['gnorm'] = row(gla_norm_g)
    p['wuv'] = w_in[:, o_glow:o_uv].astype(BF16)
    p['buv'] = row(b_in[o_glow:o_uv])
    p['slng'] = row(sgu_ln_g)
    p['slnb'] = row(sgu_ln_b)
    p['wtril'] = jnp.tril(sgu_ws).astype(BF16)
    p['sbias'] = jnp.repeat(sgu_bs.T, SGU_GD, axis=1).astype(F32)
    p['wxc'] = w_in[:, o_uv:o_xc].astype(BF16)
    p['bxc'] = row(b_in[o_uv:o_xc])
    p['poolw'] = jax.scipy.linalg.block_diag(*[pool_w[g] for g in range(len(POOL_WINDOWS))]).astype(BF16)
    p['pscale'] = row(pool_scale)
    p['wgate'] = w_in[:, o_xc:].astype(BF16)
    p['bgate'] = row(b_in[o_xc:])
    p['wupa'] = w_up_a.astype(BF16)
    p['wupb'] = w_up_b.astype(BF16)
    p['wupc'] = w_up_c.astype(BF16)
    p['wo'] = w_o.astype(BF16)
    p['ln1g'], p['ln1b'] = row(ln1_g), row(ln1_b)
    p['wq'] = xa_wq.astype(BF16)
    p['wkt'] = xa_wk.T.astype(BF16)
    p['wv'] = xa_wv.astype(BF16)
    p['wxo'] = xa_wo.astype(BF16)
    p['ln2g'], p['ln2b'] = row(ln2_g), row(ln2_b)
    rw = pad_lanes(router_w)
    p['rwhi'] = rw.astype(BF16)
    p['rwlo'] = (rw - p['rwhi'].astype(F32)).astype(BF16)
    p['rb'] = pad_lanes(row(router_b))
    p['ln3g'], p['ln3b'] = row(ln3_g), row(ln3_b)
    return p


def _route(logits):
    N = logits.shape[0]
    A = N * TOP_K
    top_vals, top_idx = lax.top_k(logits, TOP_K)
    gate = jax.nn.softmax(top_vals, axis=-1)
    flat_e = top_idx.reshape(-1)
    onehot = (flat_e[:, None] == jnp.arange(N_EXPERTS, dtype=jnp.int32)[None, :]).astype(jnp.int32)
    csum = jnp.cumsum(onehot, axis=0)
    counts = csum[-1]
    rank = jnp.take_along_axis(csum, flat_e[:, None], axis=1)[:, 0] - 1
    padded = ((counts + MOE_BLOCK - 1) // MOE_BLOCK) * MOE_BLOCK
    pad_end = jnp.cumsum(padded)
    pad_start = pad_end - padded
    dest = pad_start[flat_e] + rank
    n_blocks = A // MOE_BLOCK + N_EXPERTS
    flat_tok = jnp.repeat(jnp.arange(N, dtype=jnp.int32), TOP_K)
    slot_tok = jnp.zeros((n_blocks * MOE_BLOCK,), jnp.int32).at[dest].set(flat_tok)
    block_expert = jnp.minimum(
        jnp.searchsorted(pad_end, jnp.arange(n_blocks, dtype=jnp.int32) * MOE_BLOCK, side='right'),
        N_EXPERTS - 1).astype(jnp.int32)
    n_used = (pad_end[-1] // MOE_BLOCK).astype(jnp.int32).reshape(1)
    return gate, dest, slot_tok, block_expert, n_used


def kernel(x, mem, w_in, b_in, gla_wg2, gla_bg, gla_norm_g, sgu_ln_g, sgu_ln_b, sgu_ws, sgu_bs, pool_w, pool_scale, w_up_a, w_up_b, w_up_c, w_o, ln1_g, ln1_b, xa_wq, xa_wk, xa_wv, xa_wo, ln2_g, ln2_b, router_w, router_b, exp_w_gu, exp_b_gu, exp_w_down, exp_b_down, ln3_g, ln3_b):
    B, S, D = x.shape
    N = B * S
    for l in range(DEPTH):
        p = _prep_layer(w_in[l], b_in[l], gla_wg2[l], gla_bg[l], gla_norm_g[l], sgu_ln_g[l], sgu_ln_b[l],
                        sgu_ws[l], sgu_bs[l], pool_w[l], pool_scale[l], w_up_a[l], w_up_b[l], w_up_c[l], w_o[l],
                        ln1_g[l], ln1_b[l], xa_wq[l], xa_wk[l], xa_wv[l], xa_wo[l], ln2_g[l], ln2_b[l],
                        router_w[l], router_b[l], ln3_g[l], ln3_b[l])
        x1 = _mixer(x, p)
        kt, v = _memkv(mem, p)
        x2, x2b, logits = _xattn(x1, kt, v, p)
        gate, dest, slot_tok, block_expert, n_used = _route(logits.reshape(N, LANES)[:, :N_EXPERTS])
        xs = x2b.reshape(N, D)[slot_tok]
        ys = _experts(block_expert, n_used, xs, exp_w_gu[l], exp_b_gu[l], exp_w_down[l], exp_b_down[l])
        yg = ys[dest].reshape(N, TOP_K * D)
        x = _combine(x2.reshape(N, D), yg, gate, p['ln3g'], p['ln3b']).reshape(B, S, D)
    return x
```

```python
import jax
import jax.numpy as jnp
from jax import lax
from jax.experimental import pallas as pl
from jax.experimental.pallas import tpu as pltpu

F32 = jnp.float32
BF16 = jnp.bfloat16

D_MODEL = 1024
DEPTH = 2
GLA_HEADS = 4
GLA_KEY = 256
GLA_VAL = 512
GLA_DK = 64
GLA_DV = 128
GLA_RANK = 16
GLA_TAU = 16.0
GLA_CHUNK = 64
SGU_GROUPS = 4
SGU_WIDTH = 256
SGU_GD = 64
SGU_CHUNK = 128
POOL_WINDOWS = (2, 4, 8, 16)
POOL_WIDTH = 256
POOL_GD = 64
POOL_CARRY = 32
MEM_LEN = 256
XA_HEADS = 4
XA_DH = 256
N_EXPERTS = 32
TOP_K = 4
EXPERT_FF = 1024
SWIGLU_LIMIT = 7.0
SWIGLU_ALPHA = 1.702
DEEPNORM_ALPHA = (2 * DEPTH) ** 0.25
LN_EPS = 1e-5
LANES = 128
VMEM_LIMIT = 56 * 1024 * 1024

MIX_TILE = 256
XA_TILE = 512
MOE_BLOCK = 256
CMB_TILE = 256

O_QKVR = 0
O_GLOW = 2 * GLA_KEY + 2 * GLA_VAL
O_UV = O_GLOW + LANES
O_XC = O_UV + 2 * SGU_WIDTH
O_GATE = O_XC + POOL_WIDTH
N_PACK = O_GATE + 3 * D_MODEL


def _dot(a, b):
    return jnp.dot(a, b, preferred_element_type=F32)


def _dot_t0(a, b):
    return lax.dot_general(a, b, (((0,), (0,)), ((), ())), preferred_element_type=F32)


def _dot_t1(a, b):
    return lax.dot_general(a, b, (((1,), (1,)), ((), ())), preferred_element_type=F32)


def _split_bf16(x):
    hi = x.astype(BF16)
    lo = (x - hi.astype(F32)).astype(BF16)
    return hi, lo


def _layer_norm(x, g, b):
    mu = jnp.mean(x, axis=-1, keepdims=True)
    xc = x - mu
    var = jnp.mean(xc * xc, axis=-1, keepdims=True)
    return xc * lax.rsqrt(var + LN_EPS) * g + b


def _sigmoid(x):
    return 1.0 / (1.0 + jnp.exp(-x))


def _layer_spec(arr, l):
    nd = arr.ndim - 1
    return pl.BlockSpec((None,) + arr.shape[1:], lambda *_: (l,) + (0,) * nd, pipeline_mode=pl.Buffered(1))


def _mixer_kernel(x_ref, wcat_ref, bcat_ref, wg2_ref, bg_ref, gnorm_ref,
                  slng_ref, slnb_ref, wtril_ref, sbias_ref, poolw_ref, pscale_ref,
                  wup_ref, wo_ref, ln1g_ref, ln1b_ref,
                  out_ref,
                  state_ref, qkvr_ref, la_ref, ya_ref, vln_ref, e_ref, s2_ref, s4_ref, s8_ref):
    T = MIX_TILE
    D = D_MODEL
    j = pl.program_id(1)
    x = x_ref[0]
    xb = x.astype(BF16)

    def proj(lo, hi):
        return _dot(xb, wcat_ref[:, lo:hi]) + bcat_ref[:, lo:hi]

    @pl.when(j == 0)
    def _():
        state_ref[...] = jnp.zeros_like(state_ref)
        e_ref[0:POOL_CARRY, :] = jnp.zeros((POOL_CARRY, POOL_WIDTH), F32)

    @pl.when(j > 0)
    def _():
        e_ref[0:POOL_CARRY, :] = e_ref[T:T + POOL_CARRY, :]

    qkvr_ref[...] = proj(O_QKVR, O_GLOW)
    glow = proj(O_GLOW, O_UV)
    z = _dot(glow.astype(BF16), wg2_ref[...]) + bg_ref[...]
    la_ref[...] = (jnp.minimum(z, 0.0) - jnp.log1p(jnp.exp(-jnp.abs(z)))) * (1.0 / GLA_TAU)

    C = GLA_CHUNK
    row = lax.broadcasted_iota(jnp.int32, (C, C), 0)
    col = lax.broadcasted_iota(jnp.int32, (C, C), 1)
    tril_bf = jnp.where(row >= col, 1.0, 0.0).astype(BF16)
    row4 = lax.broadcasted_iota(jnp.int32, (GLA_HEADS * C, C), 0) & (C - 1)
    tril4 = row4 >= lax.broadcasted_iota(jnp.int32, (GLA_HEADS * C, C), 1)
    ones_bf = jnp.ones((C, GLA_DV), BF16)
    lane = lax.broadcasted_iota(jnp.int32, (C, GLA_KEY), 1)
    head_masks = [(lane >= h * GLA_DK) & (lane < (h + 1) * GLA_DK) for h in range(GLA_HEADS)]

    def gla_chunk(c, carry):
        r0 = pl.multiple_of(c * C, C)
        la = la_ref[pl.ds(r0, C), :]
        la_hi, la_lo = _split_bf16(la)
        b = _dot(tril_bf, la_hi) + _dot(tril_bf, la_lo)
        b_last = b[C - 1:C, :]
        q = qkvr_ref[pl.ds(r0, C), 0:GLA_KEY]
        k = qkvr_ref[pl.ds(r0, C), GLA_KEY:2 * GLA_KEY]
        v = qkvr_ref[pl.ds(r0, C), 2 * GLA_KEY:2 * GLA_KEY + GLA_VAL].astype(BF16)
        q_dec = q * (GLA_DK ** -0.5) * jnp.exp(b)
        k_dec = (k * jnp.exp(-b)).astype(BF16)
        k_tail = (k * jnp.exp(b_last - b)).astype(BF16)
        q_dec_bf = q_dec.astype(BF16)
        q_stack = jnp.concatenate([jnp.where(m, q_dec, 0.0) for m in head_masks], axis=0).astype(BF16)
        scores = _dot_t1(q_stack, k_dec)
        scores = jnp.where(tril4, scores, 0.0).astype(BF16)
        o_intra = _dot(scores, v)
        o_inter = _dot(q_dec_bf, state_ref[...].astype(BF16))
        kv = _dot_t0(k_tail, v)
        dec = jnp.exp(_dot_t0(la_hi, ones_bf) + _dot_t0(la_lo, ones_bf))
        for h in range(GLA_HEADS):
            rs = slice(h * GLA_DK, (h + 1) * GLA_DK)
            cs = slice(h * GLA_DV, (h + 1) * GLA_DV)
            o_h = o_intra[h * C:(h + 1) * C, cs] + o_inter[:, cs]
            ms = jnp.mean(o_h * o_h, axis=-1, keepdims=True)
            o_h = o_h * lax.rsqrt(ms + LN_EPS) * gnorm_ref[:, cs]
            r_h = qkvr_ref[pl.ds(r0, C), 2 * GLA_KEY + GLA_VAL + h * GLA_DV:2 * GLA_KEY + GLA_VAL + (h + 1) * GLA_DV]
            ya_ref[pl.ds(r0, C), cs] = (o_h * (r_h * _sigmoid(r_h))).astype(BF16)
            state_ref[rs, cs] = dec[rs, :] * state_ref[rs, cs] + kv[rs, cs]
        return carry

    lax.fori_loop(0, T // C, gla_chunk, 0)

    uv = proj(O_UV, O_XC)
    zg = 0.5 * uv * (1.0 + lax.erf(uv * (2.0 ** -0.5)))
    u = zg[:, :SGU_WIDTH]
    vln_ref[...] = _layer_norm(zg[:, SGU_WIDTH:], slng_ref[...], slnb_ref[...])
    lane_s = lax.broadcasted_iota(jnp.int32, (SGU_CHUNK, SGU_WIDTH), 1)
    s_parts = []
    for n in range(T // SGU_CHUNK):
        vc = vln_ref[n * SGU_CHUNK:(n + 1) * SGU_CHUNK, :]
        s = sbias_ref[...]
        for g in range(SGU_GROUPS):
            vg = jnp.where((lane_s >= g * SGU_GD) & (lane_s < (g + 1) * SGU_GD), vc, 0.0).astype(BF16)
            s = s + _dot(wtril_ref[g], vg)
        s_parts.append(s)
    y_b = (u * jnp.concatenate(s_parts, axis=0)).astype(BF16)

    P = POOL_CARRY
    xc = proj(O_XC, O_GATE)
    e_ref[P:P + T, :] = xc
    s2_ref[8:P + T, :] = e_ref[8:P + T, :] + e_ref[7:P + T - 1, :]
    s4_ref[16:P + T, :] = s2_ref[16:P + T, :] + s2_ref[14:P + T - 2, :]
    s8_ref[24:P + T, :] = s4_ref[24:P + T, :] + s4_ref[20:P + T - 4, :]
    s16 = s8_ref[P:P + T, :] + s8_ref[P - 8:P + T - 8, :]
    lane_p = lax.broadcasted_iota(jnp.int32, (T, POOL_WIDTH), 1)
    tpos = lax.broadcasted_iota(jnp.int32, (T, POOL_WIDTH), 0) + (j * T + 1)
    grp = lane_p >> 6
    win = jnp.where(grp == 0, POOL_WINDOWS[0], jnp.where(grp == 1, POOL_WINDOWS[1],
                    jnp.where(grp == 2, POOL_WINDOWS[2], POOL_WINDOWS[3])))
    wsum = jnp.where(grp == 0, s2_ref[P:P + T, :], jnp.where(grp == 1, s4_ref[P:P + T, :],
                     jnp.where(grp == 2, s8_ref[P:P + T, :], s16)))
    count = jnp.minimum(tpos, win).astype(F32)
    pooled = wsum / count - xc
    y_c = (_dot(pooled.astype(BF16), poolw_ref[...]) * pscale_ref[...]).astype(BF16)

    ra, rb = GLA_VAL, GLA_VAL + SGU_WIDTH
    merged = _sigmoid(proj(O_GATE, O_GATE + D)) * _dot(ya_ref[...], wup_ref[0:ra, :])
    merged += _sigmoid(proj(O_GATE + D, O_GATE + 2 * D)) * _dot(y_b, wup_ref[ra:rb, :])
    merged += _sigmoid(proj(O_GATE + 2 * D, O_GATE + 3 * D)) * _dot(y_c, wup_ref[rb:, :])
    h = _dot(merged.astype(BF16), wo_ref[...])
    out_ref[0] = _layer_norm(DEEPNORM_ALPHA * x + h, ln1g_ref[...], ln1b_ref[...])


_MIXER_WEIGHTS = ('wcat', 'bcat', 'wg2', 'bg', 'gnorm', 'slng', 'slnb', 'wtril', 'sbias', 'poolw', 'pscale',
                  'wup', 'wo', 'ln1g', 'ln1b')


def _mixer(x, p, l):
    B, S, D = x.shape
    T = MIX_TILE
    weights = [p[n] for n in _MIXER_WEIGHTS]
    return pl.pallas_call(
        _mixer_kernel,
        out_shape=jax.ShapeDtypeStruct((B, S, D), F32),
        grid=(B, S // T),
        in_specs=[pl.BlockSpec((1, T, D), lambda b, j: (b, j, 0))] + [_layer_spec(w, l) for w in weights],
        out_specs=pl.BlockSpec((1, T, D), lambda b, j: (b, j, 0)),
        scratch_shapes=[
            pltpu.VMEM((GLA_KEY, GLA_VAL), F32),
            pltpu.VMEM((T, 2 * GLA_KEY + 2 * GLA_VAL), F32),
            pltpu.VMEM((T, GLA_KEY), F32),
            pltpu.VMEM((T, GLA_VAL), BF16),
            pltpu.VMEM((T, SGU_WIDTH), F32),
            pltpu.VMEM((T + POOL_CARRY, POOL_WIDTH), F32),
            pltpu.VMEM((T + POOL_CARRY, POOL_WIDTH), F32),
            pltpu.VMEM((T + POOL_CARRY, POOL_WIDTH), F32),
            pltpu.VMEM((T + POOL_CARRY, POOL_WIDTH), F32),
        ],
        compiler_params=pltpu.CompilerParams(dimension_semantics=("arbitrary", "arbitrary"),
                                             vmem_limit_bytes=VMEM_LIMIT),
        name="mixer",
    )(x, *weights)


def _memkv_kernel(memt_ref, mem_ref, wkt_ref, wv_ref, kt_ref, v_ref):
    kt_ref[0] = _dot(wkt_ref[...], memt_ref[0].astype(BF16)).astype(BF16)
    v_ref[0] = _dot(mem_ref[0].astype(BF16), wv_ref[...]).astype(BF16)


def _memkv(mem, memt, p, l):
    B, M, D = mem.shape
    return pl.pallas_call(
        _memkv_kernel,
        out_shape=(jax.ShapeDtypeStruct((B, D, M), BF16), jax.ShapeDtypeStruct((B, M, D), BF16)),
        grid=(B,),
        in_specs=[pl.BlockSpec((1, D, M), lambda b: (b, 0, 0)), pl.BlockSpec((1, M, D), lambda b: (b, 0, 0)),
                  _layer_spec(p['wkt'], l), _layer_spec(p['wv'], l)],
        out_specs=(pl.BlockSpec((1, D, M), lambda b: (b, 0, 0)), pl.BlockSpec((1, M, D), lambda b: (b, 0, 0))),
        compiler_params=pltpu.CompilerParams(dimension_semantics=("arbitrary",), vmem_limit_bytes=VMEM_LIMIT),
        name="memkv",
    )(memt, mem, p['wkt'], p['wv'])


def _xattn_kernel(x_ref, kt_ref, v_ref, wq_ref, wo_ref, ln2g_ref, ln2b_ref, rwhi_ref, rwlo_ref, rb_ref,
                  x2_ref, x2b_ref, logit_ref):
    x = x_ref[0]
    q = (_dot(x.astype(BF16), wq_ref[...]) * (XA_DH ** -0.5)).astype(BF16)
    h = jnp.zeros_like(x)
    for hd in range(XA_HEADS):
        cs = slice(hd * XA_DH, (hd + 1) * XA_DH)
        s = _dot(q[:, cs], kt_ref[0, cs, :])
        e = jnp.exp(s - jnp.max(s, axis=-1, keepdims=True))
        o = _dot(e.astype(BF16), v_ref[0, :, cs]) / jnp.sum(e, axis=-1, keepdims=True)
        h = h + _dot(o.astype(BF16), wo_ref[cs, :])
    x2 = _layer_norm(DEEPNORM_ALPHA * x + h, ln2g_ref[...], ln2b_ref[...])
    x2_ref[0] = x2
    x2b_ref[0] = x2.astype(BF16)
    hi, lo = _split_bf16(x2)
    logit_ref[0] = _dot(hi, rwhi_ref[...]) + (_dot(hi, rwlo_ref[...]) + _dot(lo, rwhi_ref[...])) + rb_ref[...]


def _xattn(x, kt, v, p, l):
    B, S, D = x.shape
    T = XA_TILE
    M = MEM_LEN
    weights = [p[n] for n in ('wq', 'wxo', 'ln2g', 'ln2b', 'rwhi', 'rwlo', 'rb')]
    return pl.pallas_call(
        _xattn_kernel,
        out_shape=(jax.ShapeDtypeStruct((B, S, D), F32), jax.ShapeDtypeStruct((B, S, D), BF16),
                   jax.ShapeDtypeStruct((B, S, LANES), F32)),
        grid=(B, S // T),
        in_specs=[pl.BlockSpec((1, T, D), lambda b, j: (b, j, 0)),
                  pl.BlockSpec((1, D, M), lambda b, j: (b, 0, 0)),
                  pl.BlockSpec((1, M, D), lambda b, j: (b, 0, 0))] + [_layer_spec(w, l) for w in weights],
        out_specs=(pl.BlockSpec((1, T, D), lambda b, j: (b, j, 0)), pl.BlockSpec((1, T, D), lambda b, j: (b, j, 0)),
                   pl.BlockSpec((1, T, LANES), lambda b, j: (b, j, 0))),
        compiler_params=pltpu.CompilerParams(dimension_semantics=("arbitrary", "arbitrary"),
                                             vmem_limit_bytes=VMEM_LIMIT),
        name="xattn",
    )(x, kt, v, *weights)


def _expert_kernel(be_ref, nb_ref, xs_ref, wgu_ref, bgu_ref, wd_ref, bd_ref, out_ref, wgu_bf, wd_bf):
    i = pl.program_id(0)

    @pl.when(i < nb_ref[0])
    def _():
        e = be_ref[i]
        prev = be_ref[jnp.maximum(i - 1, 0)]

        @pl.when((i == 0) | (e != prev))
        def _():
            wgu_bf[...] = wgu_ref[...].astype(BF16)
            wd_bf[...] = wd_ref[...].astype(BF16)

        hh = _dot(xs_ref[...], wgu_bf[...]) + bgu_ref[...]
        h_glu = jnp.minimum(hh[:, :EXPERT_FF], SWIGLU_LIMIT)
        h_lin = jnp.clip(hh[:, EXPERT_FF:], -SWIGLU_LIMIT, SWIGLU_LIMIT)
        a = h_glu * _sigmoid(SWIGLU_ALPHA * h_glu) * (h_lin + 1.0)
        out_ref[...] = _dot(a.astype(BF16), wd_bf[...]) + bd_ref[...]


def _experts(block_expert, n_used, xs, w_gu, b_gu, w_down, b_down, l):
    P, D = xs.shape
    NB = P // MOE_BLOCK
    F2 = 2 * EXPERT_FF

    def row_map(i, be, nb):
        return (jnp.minimum(i, nb[0] - 1), 0)

    def exp_map(i, be, nb):
        return (l, be[jnp.minimum(i, nb[0] - 1)], 0, 0)

    grid_spec = pltpu.PrefetchScalarGridSpec(
        num_scalar_prefetch=2,
        grid=(NB,),
        in_specs=[pl.BlockSpec((MOE_BLOCK, D), row_map),
                  pl.BlockSpec((None, None, D, F2), exp_map),
                  pl.BlockSpec((None, None, 1, F2), exp_map),
                  pl.BlockSpec((None, None, EXPERT_FF, D), exp_map),
                  pl.BlockSpec((None, None, 1, D), exp_map)],
        out_specs=pl.BlockSpec((MOE_BLOCK, D), row_map),
        scratch_shapes=[pltpu.VMEM((D, F2), BF16), pltpu.VMEM((EXPERT_FF, D), BF16)],
    )
    return pl.pallas_call(
        _expert_kernel,
        out_shape=jax.ShapeDtypeStruct((P, D), F32),
        grid_spec=grid_spec,
        compiler_params=pltpu.CompilerParams(dimension_semantics=("arbitrary",), vmem_limit_bytes=VMEM_LIMIT),
        name="experts",
    )(block_expert, n_used, xs, w_gu, b_gu.reshape(DEPTH, N_EXPERTS, 1, F2), w_down,
      b_down.reshape(DEPTH, N_EXPERTS, 1, D))


def _combine_kernel(x_ref, yg_ref, gate_ref, ln3g_ref, ln3b_ref, out_ref):
    x = x_ref[...]
    g = gate_ref[...]
    y = jnp.zeros_like(x)
    for k in range(TOP_K):
        y = y + g[:, k:k + 1] * yg_ref[:, k * D_MODEL:(k + 1) * D_MODEL]
    out_ref[...] = _layer_norm(DEEPNORM_ALPHA * x + y, ln3g_ref[...], ln3b_ref[...])


def _combine(x2, yg, gate, p, l):
    N, D = x2.shape
    T = CMB_TILE
    return pl.pallas_call(
        _combine_kernel,
        out_shape=jax.ShapeDtypeStruct((N, D), F32),
        grid=(N // T,),
        in_specs=[pl.BlockSpec((T, D), lambda i: (i, 0)), pl.BlockSpec((T, TOP_K * D), lambda i: (i, 0)),
                  pl.BlockSpec((T, TOP_K), lambda i: (i, 0)), _layer_spec(p['ln3g'], l), _layer_spec(p['ln3b'], l)],
        out_specs=pl.BlockSpec((T, D), lambda i: (i, 0)),
        compiler_params=pltpu.CompilerParams(dimension_semantics=("arbitrary",), vmem_limit_bytes=VMEM_LIMIT),
        name="combine",
    )(x2, yg, gate, p['ln3g'], p['ln3b'])


def _prep(w_in, b_in, gla_wg2, gla_bg, gla_norm_g, sgu_ln_g, sgu_ln_b, sgu_ws, sgu_bs, pool_w, pool_scale,
          w_up_a, w_up_b, w_up_c, w_o, ln1_g, ln1_b, xa_wq, xa_wk, xa_wv, xa_wo, ln2_g, ln2_b,
          router_w, router_b, ln3_g, ln3_b):
    L = w_in.shape[0]
    o_glow = O_GLOW
    o_uv = o_glow + GLA_RANK
    row = lambda a: a.reshape(L, 1, -1).astype(F32)
    pad_last = lambda a, n: jnp.pad(a, [(0, 0)] * (a.ndim - 1) + [(0, n - a.shape[-1])])
    p = {}
    p['wcat'] = jnp.concatenate([w_in[..., :o_glow], pad_last(w_in[..., o_glow:o_uv], LANES), w_in[..., o_uv:]],
                                axis=-1).astype(BF16)
    p['bcat'] = row(jnp.concatenate([b_in[..., :o_glow], pad_last(b_in[..., o_glow:o_uv], LANES), b_in[..., o_uv:]],
                                    axis=-1))
    p['wg2'] = jnp.pad(gla_wg2, ((0, 0), (0, LANES - GLA_RANK), (0, 0))).astype(BF16)
    p['bg'] = row(gla_bg)
    p['gnorm'] = row(gla_norm_g)
    p['slng'] = row(sgu_ln_g)
    p['slnb'] = row(sgu_ln_b)
    p['wtril'] = jnp.tril(sgu_ws).astype(BF16)
    p['sbias'] = jnp.repeat(jnp.swapaxes(sgu_bs, 1, 2), SGU_GD, axis=2).astype(F32)
    G = len(POOL_WINDOWS)
    eye = jnp.eye(G, dtype=F32)
    p['poolw'] = jnp.einsum('lgcd,gh->lgchd', pool_w, eye).reshape(L, POOL_WIDTH, POOL_WIDTH).astype(BF16)
    p['pscale'] = row(pool_scale)
    p['wup'] = jnp.concatenate([w_up_a, w_up_b, w_up_c], axis=1).astype(BF16)
    p['wo'] = w_o.astype(BF16)
    p['ln1g'], p['ln1b'] = row(ln1_g), row(ln1_b)
    p['wq'] = xa_wq.astype(BF16)
    p['wkt'] = jnp.swapaxes(xa_wk, 1, 2).astype(BF16)
    p['wv'] = xa_wv.astype(BF16)
    p['wxo'] = xa_wo.astype(BF16)
    p['ln2g'], p['ln2b'] = row(ln2_g), row(ln2_b)
    rw = pad_last(router_w, LANES)
    p['rwhi'] = rw.astype(BF16)
    p['rwlo'] = (rw - p['rwhi'].astype(F32)).astype(BF16)
    p['rb'] = row(pad_last(router_b, LANES))
    p['ln3g'], p['ln3b'] = row(ln3_g), row(ln3_b)
    return p


def _route(logits):
    N = logits.shape[0]
    A = N * TOP_K
    top_vals, top_idx = lax.top_k(logits, TOP_K)
    gate = jax.nn.softmax(top_vals, axis=-1)
    flat_e = top_idx.reshape(-1)
    onehot = (flat_e[:, None] == jnp.arange(N_EXPERTS, dtype=jnp.int32)[None, :]).astype(jnp.int32)
    csum = jnp.cumsum(onehot, axis=0)
    counts = csum[-1]
    padded = ((counts + MOE_BLOCK - 1) // MOE_BLOCK) * MOE_BLOCK
    pad_end = jnp.cumsum(padded)
    pad_start = pad_end - padded
    dest = jnp.sum(onehot * (csum - 1 + pad_start[None, :]), axis=1)
    n_blocks = A // MOE_BLOCK + N_EXPERTS
    flat_tok = jnp.repeat(jnp.arange(N, dtype=jnp.int32), TOP_K)
    slot_tok = jnp.zeros((n_blocks * MOE_BLOCK,), jnp.int32).at[dest].set(flat_tok)
    block_start = jnp.arange(n_blocks, dtype=jnp.int32) * MOE_BLOCK
    block_expert = jnp.minimum(jnp.sum((pad_end[None, :] <= block_start[:, None]).astype(jnp.int32), axis=1),
                               N_EXPERTS - 1)
    n_used = (pad_end[-1] // MOE_BLOCK).astype(jnp.int32).reshape(1)
    return gate, dest, slot_tok, block_expert, n_used


def kernel(x, mem, w_in, b_in, gla_wg2, gla_bg, gla_norm_g, sgu_ln_g, sgu_ln_b, sgu_ws, sgu_bs, pool_w, pool_scale, w_up_a, w_up_b, w_up_c, w_o, ln1_g, ln1_b, xa_wq, xa_wk, xa_wv, xa_wo, ln2_g, ln2_b, router_w, router_b, exp_w_gu, exp_b_gu, exp_w_down, exp_b_down, ln3_g, ln3_b):
    B, S, D = x.shape
    N = B * S
    p = _prep(w_in, b_in, gla_wg2, gla_bg, gla_norm_g, sgu_ln_g, sgu_ln_b, sgu_ws, sgu_bs, pool_w, pool_scale,
              w_up_a, w_up_b, w_up_c, w_o, ln1_g, ln1_b, xa_wq, xa_wk, xa_wv, xa_wo, ln2_g, ln2_b,
              router_w, router_b, ln3_g, ln3_b)
    memt = jnp.swapaxes(mem, 1, 2)
    for l in range(DEPTH):
        x1 = _mixer(x, p, l)
        kt, v = _memkv(mem, memt, p, l)
        x2, x2b, logits = _xattn(x1, kt, v, p, l)
        gate, dest, slot_tok, block_expert, n_used = _route(logits.reshape(N, LANES)[:, :N_EXPERTS])
        xs = x2b.reshape(N, D)[slot_tok]
        ys = _experts(block_expert, n_used, xs, exp_w_gu, exp_b_gu, exp_w_down, exp_b_down, l)
        yg = ys[dest].reshape(N, TOP_K * D)
        x = _combine(x2.reshape(N, D), yg, gate, p, l).reshape(B, S, D)
    return x
```

```python
import jax
import jax.numpy as jnp
from jax import lax
from jax.experimental import pallas as pl
from jax.experimental.pallas import tpu as pltpu
from jax.experimental.pallas import tpu_sc as plsc

F32 = jnp.float32
BF16 = jnp.bfloat16

D_MODEL = 1024
DEPTH = 2
GLA_HEADS = 4
GLA_KEY = 256
GLA_VAL = 512
GLA_DK = 64
GLA_DV = 128
GLA_RANK = 16
GLA_TAU = 16.0
GLA_CHUNK = 64
SGU_GROUPS = 4
SGU_WIDTH = 256
SGU_GD = 64
SGU_CHUNK = 128
POOL_WINDOWS = (2, 4, 8, 16)
POOL_WIDTH = 256
POOL_GD = 64
POOL_CARRY = 32
MEM_LEN = 256
XA_HEADS = 4
XA_DH = 256
N_EXPERTS = 32
TOP_K = 4
EXPERT_FF = 1024
SWIGLU_LIMIT = 7.0
SWIGLU_ALPHA = 1.702
DEEPNORM_ALPHA = (2 * DEPTH) ** 0.25
LN_EPS = 1e-5
LANES = 128
VMEM_LIMIT = 56 * 1024 * 1024

MIX_TILE = 256
XA_TILE = 512
MOE_BLOCK = 256
CMB_TILE = 256
SC_GATHER_WINDOW = 32

O_QKVR = 0
O_GLOW = 2 * GLA_KEY + 2 * GLA_VAL
O_UV = O_GLOW + LANES
O_XC = O_UV + 2 * SGU_WIDTH
O_GATE = O_XC + POOL_WIDTH
N_PACK = O_GATE + 3 * D_MODEL


def _dot(a, b):
    return jnp.dot(a, b, preferred_element_type=F32)


def _dot_t0(a, b):
    return lax.dot_general(a, b, (((0,), (0,)), ((), ())), preferred_element_type=F32)


def _dot_t1(a, b):
    return lax.dot_general(a, b, (((1,), (1,)), ((), ())), preferred_element_type=F32)


def _split_bf16(x):
    hi = x.astype(BF16)
    lo = (x - hi.astype(F32)).astype(BF16)
    return hi, lo


def _layer_norm(x, g, b):
    mu = jnp.mean(x, axis=-1, keepdims=True)
    xc = x - mu
    var = jnp.mean(xc * xc, axis=-1, keepdims=True)
    return xc * lax.rsqrt(var + LN_EPS) * g + b


def _sigmoid(x):
    return 1.0 / (1.0 + jnp.exp(-x))


def _layer_spec(arr, l):
    nd = arr.ndim - 1
    return pl.BlockSpec((None,) + arr.shape[1:], lambda *_: (l,) + (0,) * nd, pipeline_mode=pl.Buffered(1))


def _mixer_kernel(x_ref, wcat_ref, bcat_ref, wg2_ref, bg_ref, gnorm_ref,
                  slng_ref, slnb_ref, wtril_ref, sbias_ref, poolw_ref, pscale_ref,
                  wup_ref, wo_ref, ln1g_ref, ln1b_ref,
                  out_ref,
                  state_ref, qkvr_ref, la_ref, ya_ref, vln_ref, e_ref, s2_ref, s4_ref, s8_ref):
    T = MIX_TILE
    D = D_MODEL
    j = pl.program_id(1)
    x = x_ref[0]
    xb = x.astype(BF16)

    def proj(lo, hi):
        return _dot(xb, wcat_ref[:, lo:hi]) + bcat_ref[:, lo:hi]

    @pl.when(j == 0)
    def _():
        state_ref[...] = jnp.zeros_like(state_ref)
        e_ref[0:POOL_CARRY, :] = jnp.zeros((POOL_CARRY, POOL_WIDTH), F32)

    @pl.when(j > 0)
    def _():
        e_ref[0:POOL_CARRY, :] = e_ref[T:T + POOL_CARRY, :]

    qkvr_ref[...] = proj(O_QKVR, O_GLOW)
    glow = proj(O_GLOW, O_UV)
    z = _dot(glow.astype(BF16), wg2_ref[...]) + bg_ref[...]
    la_ref[...] = (jnp.minimum(z, 0.0) - jnp.log1p(jnp.exp(-jnp.abs(z)))) * (1.0 / GLA_TAU)

    C = GLA_CHUNK
    row = lax.broadcasted_iota(jnp.int32, (C, C), 0)
    col = lax.broadcasted_iota(jnp.int32, (C, C), 1)
    tril_bf = jnp.where(row >= col, 1.0, 0.0).astype(BF16)
    row4 = lax.broadcasted_iota(jnp.int32, (GLA_HEADS * C, C), 0) & (C - 1)
    tril4 = row4 >= lax.broadcasted_iota(jnp.int32, (GLA_HEADS * C, C), 1)
    ones_bf = jnp.ones((C, GLA_DV), BF16)
    lane = lax.broadcasted_iota(jnp.int32, (C, GLA_KEY), 1)
    head_masks = [(lane >= h * GLA_DK) & (lane < (h + 1) * GLA_DK) for h in range(GLA_HEADS)]

    def gla_chunk(c, carry):
        r0 = pl.multiple_of(c * C, C)
        la = la_ref[pl.ds(r0, C), :]
        la_hi, la_lo = _split_bf16(la)
        b = _dot(tril_bf, la_hi) + _dot(tril_bf, la_lo)
        b_last = b[C - 1:C, :]
        q = qkvr_ref[pl.ds(r0, C), 0:GLA_KEY]
        k = qkvr_ref[pl.ds(r0, C), GLA_KEY:2 * GLA_KEY]
        v = qkvr_ref[pl.ds(r0, C), 2 * GLA_KEY:2 * GLA_KEY + GLA_VAL].astype(BF16)
        q_dec = q * (GLA_DK ** -0.5) * jnp.exp(b)
        k_dec = (k * jnp.exp(-b)).astype(BF16)
        k_tail = (k * jnp.exp(b_last - b)).astype(BF16)
        q_dec_bf = q_dec.astype(BF16)
        q_stack = jnp.concatenate([jnp.where(m, q_dec, 0.0) for m in head_masks], axis=0).astype(BF16)
        scores = _dot_t1(q_stack, k_dec)
        scores = jnp.where(tril4, scores, 0.0).astype(BF16)
        o_intra = _dot(scores, v)
        o_inter = _dot(q_dec_bf, state_ref[...].astype(BF16))
        kv = _dot_t0(k_tail, v)
        dec = jnp.exp(_dot_t0(la_hi, ones_bf) + _dot_t0(la_lo, ones_bf))
        for h in range(GLA_HEADS):
            rs = slice(h * GLA_DK, (h + 1) * GLA_DK)
            cs = slice(h * GLA_DV, (h + 1) * GLA_DV)
            o_h = o_intra[h * C:(h + 1) * C, cs] + o_inter[:, cs]
            ms = jnp.mean(o_h * o_h, axis=-1, keepdims=True)
            o_h = o_h * lax.rsqrt(ms + LN_EPS) * gnorm_ref[:, cs]
            r_h = qkvr_ref[pl.ds(r0, C), 2 * GLA_KEY + GLA_VAL + h * GLA_DV:2 * GLA_KEY + GLA_VAL + (h + 1) * GLA_DV]
            ya_ref[pl.ds(r0, C), cs] = (o_h * (r_h * _sigmoid(r_h))).astype(BF16)
            state_ref[rs, cs] = dec[rs, :] * state_ref[rs, cs] + kv[rs, cs]
        return carry

    lax.fori_loop(0, T // C, gla_chunk, 0)

    uv = proj(O_UV, O_XC)
    zg = 0.5 * uv * (1.0 + lax.erf(uv * (2.0 ** -0.5)))
    u = zg[:, :SGU_WIDTH]
    vln_ref[...] = _layer_norm(zg[:, SGU_WIDTH:], slng_ref[...], slnb_ref[...])
    lane_s = lax.broadcasted_iota(jnp.int32, (SGU_CHUNK, SGU_WIDTH), 1)
    s_parts = []
    for n in range(T // SGU_CHUNK):
        vc = vln_ref[n * SGU_CHUNK:(n + 1) * SGU_CHUNK, :]
        s = sbias_ref[...]
        for g in range(SGU_GROUPS):
            vg = jnp.where((lane_s >= g * SGU_GD) & (lane_s < (g + 1) * SGU_GD), vc, 0.0).astype(BF16)
            s = s + _dot(wtril_ref[g], vg)
        s_parts.append(s)
    y_b = (u * jnp.concatenate(s_parts, axis=0)).astype(BF16)

    P = POOL_CARRY
    xc = proj(O_XC, O_GATE)
    e_ref[P:P + T, :] = xc
    s2_ref[8:P + T, :] = e_ref[8:P + T, :] + e_ref[7:P + T - 1, :]
    s4_ref[16:P + T, :] = s2_ref[16:P + T, :] + s2_ref[14:P + T - 2, :]
    s8_ref[24:P + T, :] = s4_ref[24:P + T, :] + s4_ref[20:P + T - 4, :]
    s16 = s8_ref[P:P + T, :] + s8_ref[P - 8:P + T - 8, :]
    lane_p = lax.broadcasted_iota(jnp.int32, (T, POOL_WIDTH), 1)
    tpos = lax.broadcasted_iota(jnp.int32, (T, POOL_WIDTH), 0) + (j * T + 1)
    grp = lane_p >> 6
    win = jnp.where(grp == 0, POOL_WINDOWS[0], jnp.where(grp == 1, POOL_WINDOWS[1],
                    jnp.where(grp == 2, POOL_WINDOWS[2], POOL_WINDOWS[3])))
    wsum = jnp.where(grp == 0, s2_ref[P:P + T, :], jnp.where(grp == 1, s4_ref[P:P + T, :],
                     jnp.where(grp == 2, s8_ref[P:P + T, :], s16)))
    count = jnp.minimum(tpos, win).astype(F32)
    pooled = wsum / count - xc
    y_c = (_dot(pooled.astype(BF16), poolw_ref[...]) * pscale_ref[...]).astype(BF16)

    ra, rb = GLA_VAL, GLA_VAL + SGU_WIDTH
    merged = _sigmoid(proj(O_GATE, O_GATE + D)) * _dot(ya_ref[...], wup_ref[0:ra, :])
    merged += _sigmoid(proj(O_GATE + D, O_GATE + 2 * D)) * _dot(y_b, wup_ref[ra:rb, :])
    merged += _sigmoid(proj(O_GATE + 2 * D, O_GATE + 3 * D)) * _dot(y_c, wup_ref[rb:, :])
    h = _dot(merged.astype(BF16), wo_ref[...])
    out_ref[0] = _layer_norm(DEEPNORM_ALPHA * x + h, ln1g_ref[...], ln1b_ref[...])


_MIXER_WEIGHTS = ('wcat', 'bcat', 'wg2', 'bg', 'gnorm', 'slng', 'slnb', 'wtril', 'sbias', 'poolw', 'pscale',
                  'wup', 'wo', 'ln1g', 'ln1b')


def _mixer(x, p, l):
    B, S, D = x.shape
    T = MIX_TILE
    weights = [p[n] for n in _MIXER_WEIGHTS]
    return pl.pallas_call(
        _mixer_kernel,
        out_shape=jax.ShapeDtypeStruct((B, S, D), F32),
        grid=(B, S // T),
        in_specs=[pl.BlockSpec((1, T, D), lambda b, j: (b, j, 0))] + [_layer_spec(w, l) for w in weights],
        out_specs=pl.BlockSpec((1, T, D), lambda b, j: (b, j, 0)),
        scratch_shapes=[
            pltpu.VMEM((GLA_KEY, GLA_VAL), F32),
            pltpu.VMEM((T, 2 * GLA_KEY + 2 * GLA_VAL), F32),
            pltpu.VMEM((T, GLA_KEY), F32),
            pltpu.VMEM((T, GLA_VAL), BF16),
            pltpu.VMEM((T, SGU_WIDTH), F32),
            pltpu.VMEM((T + POOL_CARRY, POOL_WIDTH), F32),
            pltpu.VMEM((T + POOL_CARRY, POOL_WIDTH), F32),
            pltpu.VMEM((T + POOL_CARRY, POOL_WIDTH), F32),
            pltpu.VMEM((T + POOL_CARRY, POOL_WIDTH), F32),
        ],
        compiler_params=pltpu.CompilerParams(dimension_semantics=("arbitrary", "arbitrary"),
                                             vmem_limit_bytes=VMEM_LIMIT),
        name="mixer",
    )(x, *weights)


def _memkv_kernel(memt_ref, mem_ref, wkt_ref, wv_ref, kt_ref, v_ref):
    kt_ref[0] = _dot(wkt_ref[...], memt_ref[0].astype(BF16)).astype(BF16)
    v_ref[0] = _dot(mem_ref[0].astype(BF16), wv_ref[...]).astype(BF16)


def _memkv(mem, memt, p, l):
    B, M, D = mem.shape
    return pl.pallas_call(
        _memkv_kernel,
        out_shape=(jax.ShapeDtypeStruct((B, D, M), BF16), jax.ShapeDtypeStruct((B, M, D), BF16)),
        grid=(B,),
        in_specs=[pl.BlockSpec((1, D, M), lambda b: (b, 0, 0)), pl.BlockSpec((1, M, D), lambda b: (b, 0, 0)),
                  _layer_spec(p['wkt'], l), _layer_spec(p['wv'], l)],
        out_specs=(pl.BlockSpec((1, D, M), lambda b: (b, 0, 0)), pl.BlockSpec((1, M, D), lambda b: (b, 0, 0))),
        compiler_params=pltpu.CompilerParams(dimension_semantics=("arbitrary",), vmem_limit_bytes=VMEM_LIMIT),
        name="memkv",
    )(memt, mem, p['wkt'], p['wv'])


def _xattn_kernel(x_ref, kt_ref, v_ref, wq_ref, wo_ref, ln2g_ref, ln2b_ref, rwhi_ref, rwlo_ref, rb_ref,
                  x2_ref, x2b_ref, logit_ref):
    x = x_ref[0]
    q = (_dot(x.astype(BF16), wq_ref[...]) * (XA_DH ** -0.5)).astype(BF16)
    h = jnp.zeros_like(x)
    for hd in range(XA_HEADS):
        cs = slice(hd * XA_DH, (hd + 1) * XA_DH)
        s = _dot(q[:, cs], kt_ref[0, cs, :])
        e = jnp.exp(s - jnp.max(s, axis=-1, keepdims=True))
        o = _dot(e.astype(BF16), v_ref[0, :, cs]) / jnp.sum(e, axis=-1, keepdims=True)
        h = h + _dot(o.astype(BF16), wo_ref[cs, :])
    x2 = _layer_norm(DEEPNORM_ALPHA * x + h, ln2g_ref[...], ln2b_ref[...])
    x2_ref[0] = x2
    x2b_ref[0] = x2.astype(BF16)
    hi, lo = _split_bf16(x2)
    logit_ref[0] = _dot(hi, rwhi_ref[...]) + (_dot(hi, rwlo_ref[...]) + _dot(lo, rwhi_ref[...])) + rb_ref[...]


def _xattn(x, kt, v, p, l):
    B, S, D = x.shape
    T = XA_TILE
    M = MEM_LEN
    weights = [p[n] for n in ('wq', 'wxo', 'ln2g', 'ln2b', 'rwhi', 'rwlo', 'rb')]
    return pl.pallas_call(
        _xattn_kernel,
        out_shape=(jax.ShapeDtypeStruct((B, S, D), F32), jax.ShapeDtypeStruct((B, S, D), BF16),
                   jax.ShapeDtypeStruct((B, S, LANES), F32)),
        grid=(B, S // T),
        in_specs=[pl.BlockSpec((1, T, D), lambda b, j: (b, j, 0)),
                  pl.BlockSpec((1, D, M), lambda b, j: (b, 0, 0)),
                  pl.BlockSpec((1, M, D), lambda b, j: (b, 0, 0))] + [_layer_spec(w, l) for w in weights],
        out_specs=(pl.BlockSpec((1, T, D), lambda b, j: (b, j, 0)), pl.BlockSpec((1, T, D), lambda b, j: (b, j, 0)),
                   pl.BlockSpec((1, T, LANES), lambda b, j: (b, j, 0))),
        compiler_params=pltpu.CompilerParams(dimension_semantics=("arbitrary", "arbitrary"),
                                             vmem_limit_bytes=VMEM_LIMIT),
        name="xattn",
    )(x, kt, v, *weights)


def _expert_kernel(be_ref, nb_ref, xs_ref, wgu_ref, bgu_ref, wd_ref, bd_ref, out_ref, wgu_bf, wd_bf):
    i = pl.program_id(0)

    @pl.when(i < nb_ref[0])
    def _():
        e = be_ref[i]
        prev = be_ref[jnp.maximum(i - 1, 0)]

        @pl.when((i == 0) | (e != prev))
        def _():
            wgu_bf[...] = wgu_ref[...].astype(BF16)
            wd_bf[...] = wd_ref[...].astype(BF16)

        hh = _dot(xs_ref[...], wgu_bf[...]) + bgu_ref[...]
        h_glu = jnp.minimum(hh[:, :EXPERT_FF], SWIGLU_LIMIT)
        h_lin = jnp.clip(hh[:, EXPERT_FF:], -SWIGLU_LIMIT, SWIGLU_LIMIT)
        a = h_glu * _sigmoid(SWIGLU_ALPHA * h_glu) * (h_lin + 1.0)
        out_ref[...] = _dot(a.astype(BF16), wd_bf[...]) + bd_ref[...]


def _experts(block_expert, n_used, xs, w_gu, b_gu, w_down, b_down, l):
    P, D = xs.shape
    NB = P // MOE_BLOCK
    F2 = 2 * EXPERT_FF

    def row_map(i, be, nb):
        return (jnp.minimum(i, nb[0] - 1), 0)

    def exp_map(i, be, nb):
        return (l, be[jnp.minimum(i, nb[0] - 1)], 0, 0)

    grid_spec = pltpu.PrefetchScalarGridSpec(
        num_scalar_prefetch=2,
        grid=(NB,),
        in_specs=[pl.BlockSpec((MOE_BLOCK, D), row_map),
                  pl.BlockSpec((None, None, D, F2), exp_map),
                  pl.BlockSpec((None, None, 1, F2), exp_map),
                  pl.BlockSpec((None, None, EXPERT_FF, D), exp_map),
                  pl.BlockSpec((None, None, 1, D), exp_map)],
        out_specs=pl.BlockSpec((MOE_BLOCK, D), row_map),
        scratch_shapes=[pltpu.VMEM((D, F2), BF16), pltpu.VMEM((EXPERT_FF, D), BF16)],
    )
    return pl.pallas_call(
        _expert_kernel,
        out_shape=jax.ShapeDtypeStruct((P, D), F32),
        grid_spec=grid_spec,
        compiler_params=pltpu.CompilerParams(dimension_semantics=("arbitrary",), vmem_limit_bytes=VMEM_LIMIT),
        name="experts",
    )(block_expert, n_used, xs, w_gu, b_gu.reshape(DEPTH, N_EXPERTS, 1, F2), w_down,
      b_down.reshape(DEPTH, N_EXPERTS, 1, D))


def _sc_gather_rows(x, idx):
    M = idx.shape[0]
    D = x.shape[1]
    W = SC_GATHER_WINDOW
    mesh = plsc.VectorSubcoreMesh(core_axis_name="core", subcore_axis_name="subcore")
    n_workers = mesh.num_cores * mesh.num_subcores
    rows_per = M // n_workers
    assert rows_per * n_workers == M and rows_per % W == 0

    @pl.kernel(out_type=jax.ShapeDtypeStruct((M, D), x.dtype), mesh=mesh, name="sc_gather_rows",
               scratch_types=[pltpu.VMEM((rows_per,), jnp.int32), pltpu.VMEM((W, D), x.dtype)])
    def gather_kernel(x_hbm, i_hbm, o_hbm, idx_vmem, buf):
        wid = lax.axis_index("core") * mesh.num_subcores + lax.axis_index("subcore")
        base = wid * rows_per
        pltpu.sync_copy(i_hbm.at[pl.ds(base, rows_per)], idx_vmem)

        @pl.loop(0, rows_per // W)
        def _(j):
            pltpu.sync_copy(x_hbm.at[idx_vmem.at[pl.ds(j * W, W)]], buf)
            pltpu.sync_copy(buf, o_hbm.at[pl.ds(base + j * W, W)])

    return gather_kernel(x, idx)


def _combine_kernel(x_ref, yg_ref, gate_ref, ln3g_ref, ln3b_ref, out_ref):
    x = x_ref[...]
    g = gate_ref[...]
    y = jnp.zeros_like(x)
    for k in range(TOP_K):
        y = y + g[:, k:k + 1] * yg_ref[k * CMB_TILE:(k + 1) * CMB_TILE, :]
    out_ref[...] = _layer_norm(DEEPNORM_ALPHA * x + y, ln3g_ref[...], ln3b_ref[...])


def _combine(x2, yg, gate, p, l):
    N, D = x2.shape
    T = CMB_TILE
    return pl.pallas_call(
        _combine_kernel,
        out_shape=jax.ShapeDtypeStruct((N, D), F32),
        grid=(N // T,),
        in_specs=[pl.BlockSpec((T, D), lambda i: (i, 0)), pl.BlockSpec((TOP_K * T, D), lambda i: (i, 0)),
                  pl.BlockSpec((T, TOP_K), lambda i: (i, 0)), _layer_spec(p['ln3g'], l), _layer_spec(p['ln3b'], l)],
        out_specs=pl.BlockSpec((T, D), lambda i: (i, 0)),
        compiler_params=pltpu.CompilerParams(dimension_semantics=("arbitrary",), vmem_limit_bytes=VMEM_LIMIT),
        name="combine",
    )(x2, yg, gate, p['ln3g'], p['ln3b'])


def _prep(w_in, b_in, gla_wg2, gla_bg, gla_norm_g, sgu_ln_g, sgu_ln_b, sgu_ws, sgu_bs, pool_w, pool_scale,
          w_up_a, w_up_b, w_up_c, w_o, ln1_g, ln1_b, xa_wq, xa_wk, xa_wv, xa_wo, ln2_g, ln2_b,
          router_w, router_b, ln3_g, ln3_b):
    L = w_in.shape[0]
    o_glow = O_GLOW
    o_uv = o_glow + GLA_RANK
    row = lambda a: a.reshape(L, 1, -1).astype(F32)
    pad_last = lambda a, n: jnp.pad(a, [(0, 0)] * (a.ndim - 1) + [(0, n - a.shape[-1])])
    p = {}
    p['wcat'] = jnp.concatenate([w_in[..., :o_glow], pad_last(w_in[..., o_glow:o_uv], LANES), w_in[..., o_uv:]],
                                axis=-1).astype(BF16)
    p['bcat'] = row(jnp.concatenate([b_in[..., :o_glow], pad_last(b_in[..., o_glow:o_uv], LANES), b_in[..., o_uv:]],
                                    axis=-1))
    p['wg2'] = jnp.pad(gla_wg2, ((0, 0), (0, LANES - GLA_RANK), (0, 0))).astype(BF16)
    p['bg'] = row(gla_bg)
    p['gnorm'] = row(gla_norm_g)
    p['slng'] = row(sgu_ln_g)
    p['slnb'] = row(sgu_ln_b)
    p['wtril'] = jnp.tril(sgu_ws).astype(BF16)
    p['sbias'] = jnp.repeat(jnp.swapaxes(sgu_bs, 1, 2), SGU_GD, axis=2).astype(F32)
    G = len(POOL_WINDOWS)
    eye = jnp.eye(G, dtype=F32)
    p['poolw'] = jnp.einsum('lgcd,gh->lgchd', pool_w, eye).reshape(L, POOL_WIDTH, POOL_WIDTH).astype(BF16)
    p['pscale'] = row(pool_scale)
    p['wup'] = jnp.concatenate([w_up_a, w_up_b, w_up_c], axis=1).astype(BF16)
    p['wo'] = w_o.astype(BF16)
    p['ln1g'], p['ln1b'] = row(ln1_g), row(ln1_b)
    p['wq'] = xa_wq.astype(BF16)
    p['wkt'] = jnp.swapaxes(xa_wk, 1, 2).astype(BF16)
    p['wv'] = xa_wv.astype(BF16)
    p['wxo'] = xa_wo.astype(BF16)
    p['ln2g'], p['ln2b'] = row(ln2_g), row(ln2_b)
    rw = pad_last(router_w, LANES)
    p['rwhi'] = rw.astype(BF16)
    p['rwlo'] = (rw - p['rwhi'].astype(F32)).astype(BF16)
    p['rb'] = row(pad_last(router_b, LANES))
    p['ln3g'], p['ln3b'] = row(ln3_g), row(ln3_b)
    return p


def _route(logits):
    N = logits.shape[0]
    A = N * TOP_K
    top_vals, top_idx = lax.top_k(logits, TOP_K)
    gate = jax.nn.softmax(top_vals, axis=-1)
    flat_e = top_idx.reshape(-1)
    onehot = (flat_e[:, None] == jnp.arange(N_EXPERTS, dtype=jnp.int32)[None, :]).astype(jnp.int32)
    csum = jnp.cumsum(onehot, axis=0)
    counts = csum[-1]
    padded = ((counts + MOE_BLOCK - 1) // MOE_BLOCK) * MOE_BLOCK
    pad_end = jnp.cumsum(padded)
    pad_start = pad_end - padded
    dest = jnp.sum(onehot * (csum - 1 + pad_start[None, :]), axis=1)
    n_blocks = A // MOE_BLOCK + N_EXPERTS
    flat_tok = jnp.repeat(jnp.arange(N, dtype=jnp.int32), TOP_K)
    slot_tok = jnp.zeros((n_blocks * MOE_BLOCK,), jnp.int32).at[dest].set(flat_tok)
    block_start = jnp.arange(n_blocks, dtype=jnp.int32) * MOE_BLOCK
    block_expert = jnp.minimum(jnp.sum((pad_end[None, :] <= block_start[:, None]).astype(jnp.int32), axis=1),
                               N_EXPERTS - 1)
    n_used = (pad_end[-1] // MOE_BLOCK).astype(jnp.int32).reshape(1)
    return gate, dest, slot_tok, block_expert, n_used


def kernel(x, mem, w_in, b_in, gla_wg2, gla_bg, gla_norm_g, sgu_ln_g, sgu_ln_b, sgu_ws, sgu_bs, pool_w, pool_scale, w_up_a, w_up_b, w_up_c, w_o, ln1_g, ln1_b, xa_wq, xa_wk, xa_wv, xa_wo, ln2_g, ln2_b, router_w, router_b, exp_w_gu, exp_b_gu, exp_w_down, exp_b_down, ln3_g, ln3_b):
    B, S, D = x.shape
    N = B * S
    p = _prep(w_in, b_in, gla_wg2, gla_bg, gla_norm_g, sgu_ln_g, sgu_ln_b, sgu_ws, sgu_bs, pool_w, pool_scale,
              w_up_a, w_up_b, w_up_c, w_o, ln1_g, ln1_b, xa_wq, xa_wk, xa_wv, xa_wo, ln2_g, ln2_b,
              router_w, router_b, ln3_g, ln3_b)
    memt = jnp.swapaxes(mem, 1, 2)
    for l in range(DEPTH):
        x1 = _mixer(x, p, l)
        kt, v = _memkv(mem, memt, p, l)
        x2, x2b, logits = _xattn(x1, kt, v, p, l)
        gate, dest, slot_tok, block_expert, n_used = _route(logits.reshape(N, LANES)[:, :N_EXPERTS])
        xs = x2b.reshape(N, D)[slot_tok]
        ys = _experts(block_expert, n_used, xs, exp_w_gu, exp_b_gu, exp_w_down, exp_b_down, l)
        dest_km = dest.reshape(N // CMB_TILE, CMB_TILE, TOP_K).transpose(0, 2, 1).reshape(-1)
        yg = _sc_gather_rows(ys, dest_km)
        x = _combine(x2.reshape(N, D), yg, gate, p, l).reshape(B, S, D)
    return x
```

```python
import jax
import jax.numpy as jnp
from jax import lax
from jax.experimental import pallas as pl
from jax.experimental.pallas import tpu as pltpu
from jax.experimental.pallas import tpu_sc as plsc

F32 = jnp.float32
BF16 = jnp.bfloat16

D_MODEL = 1024
DEPTH = 2
GLA_HEADS = 4
GLA_KEY = 256
GLA_VAL = 512
GLA_DK = 64
GLA_DV = 128
GLA_RANK = 16
GLA_TAU = 16.0
GLA_CHUNK = 64
SGU_GROUPS = 4
SGU_WIDTH = 256
SGU_GD = 64
SGU_CHUNK = 128
POOL_WINDOWS = (2, 4, 8, 16)
POOL_WIDTH = 256
POOL_GD = 64
POOL_CARRY = 32
MEM_LEN = 256
XA_HEADS = 4
XA_DH = 256
N_EXPERTS = 32
TOP_K = 4
EXPERT_FF = 1024
SWIGLU_LIMIT = 7.0
SWIGLU_ALPHA = 1.702
DEEPNORM_ALPHA = (2 * DEPTH) ** 0.25
LN_EPS = 1e-5
LANES = 128
VMEM_LIMIT = 56 * 1024 * 1024

MIX_TILE = 256
XA_TILE = 512
MOE_BLOCK = 256
CMB_TILE = 256
SC_GATHER_WINDOW = 32

O_QKVR = 0
O_GLOW = 2 * GLA_KEY + 2 * GLA_VAL
O_UV = O_GLOW + LANES
O_XC = O_UV + 2 * SGU_WIDTH
O_GATE = O_XC + POOL_WIDTH
N_PACK = O_GATE + 3 * D_MODEL


def _dot(a, b):
    return jnp.dot(a, b, preferred_element_type=F32)


def _dot_t0(a, b):
    return lax.dot_general(a, b, (((0,), (0,)), ((), ())), preferred_element_type=F32)


def _dot_t1(a, b):
    return lax.dot_general(a, b, (((1,), (1,)), ((), ())), preferred_element_type=F32)


def _split_bf16(x):
    hi = x.astype(BF16)
    lo = (x - hi.astype(F32)).astype(BF16)
    return hi, lo


def _layer_norm(x, g, b):
    mu = jnp.mean(x, axis=-1, keepdims=True)
    xc = x - mu
    var = jnp.mean(xc * xc, axis=-1, keepdims=True)
    return xc * lax.rsqrt(var + LN_EPS) * g + b


def _sigmoid(x):
    return 1.0 / (1.0 + jnp.exp(-x))


def _layer_spec(arr, l):
    nd = arr.ndim - 1
    return pl.BlockSpec((None,) + arr.shape[1:], lambda *_: (l,) + (0,) * nd, pipeline_mode=pl.Buffered(1))


def _mixer_kernel(x_ref, wcat_ref, bcat_ref, wg2_ref, bg_ref, gnorm_ref,
                  slng_ref, slnb_ref, wtril_ref, sbias_ref, poolw_ref, pscale_ref,
                  wup_ref, wo_ref, ln1g_ref, ln1b_ref,
                  out_ref,
                  state_ref, qkvr_ref, la_ref, ya_ref, vln_ref, e_ref, s2_ref, s4_ref, s8_ref):
    T = MIX_TILE
    D = D_MODEL
    j = pl.program_id(1)
    x = x_ref[0]
    xb = x.astype(BF16)

    def proj(lo, hi):
        return _dot(xb, wcat_ref[:, lo:hi]) + bcat_ref[:, lo:hi]

    @pl.when(j == 0)
    def _():
        state_ref[...] = jnp.zeros_like(state_ref)
        e_ref[0:POOL_CARRY, :] = jnp.zeros((POOL_CARRY, POOL_WIDTH), F32)

    @pl.when(j > 0)
    def _():
        e_ref[0:POOL_CARRY, :] = e_ref[T:T + POOL_CARRY, :]

    qkvr_ref[...] = proj(O_QKVR, O_GLOW)
    glow = proj(O_GLOW, O_UV)
    z = _dot(glow.astype(BF16), wg2_ref[...]) + bg_ref[...]
    la_ref[...] = (jnp.minimum(z, 0.0) - jnp.log1p(jnp.exp(-jnp.abs(z)))) * (1.0 / GLA_TAU)

    C = GLA_CHUNK
    row = lax.broadcasted_iota(jnp.int32, (C, C), 0)
    col = lax.broadcasted_iota(jnp.int32, (C, C), 1)
    tril_bf = jnp.where(row >= col, 1.0, 0.0).astype(BF16)
    row4 = lax.broadcasted_iota(jnp.int32, (GLA_HEADS * C, C), 0) & (C - 1)
    tril4 = row4 >= lax.broadcasted_iota(jnp.int32, (GLA_HEADS * C, C), 1)
    ones_bf = jnp.ones((C, GLA_DV), BF16)
    lane = lax.broadcasted_iota(jnp.int32, (C, GLA_KEY), 1)
    head_masks = [(lane >= h * GLA_DK) & (lane < (h + 1) * GLA_DK) for h in range(GLA_HEADS)]

    def gla_chunk(c, carry):
        r0 = pl.multiple_of(c * C, C)
        la = la_ref[pl.ds(r0, C), :]
        la_hi, la_lo = _split_bf16(la)
        b = _dot(tril_bf, la_hi) + _dot(tril_bf, la_lo)
        b_last = b[C - 1:C, :]
        q = qkvr_ref[pl.ds(r0, C), 0:GLA_KEY]
        k = qkvr_ref[pl.ds(r0, C), GLA_KEY:2 * GLA_KEY]
        v = qkvr_ref[pl.ds(r0, C), 2 * GLA_KEY:2 * GLA_KEY + GLA_VAL].astype(BF16)
        q_dec = q * (GLA_DK ** -0.5) * jnp.exp(b)
        k_dec = (k * jnp.exp(-b)).astype(BF16)
        k_tail = (k * jnp.exp(b_last - b)).astype(BF16)
        q_dec_bf = q_dec.astype(BF16)
        q_stack = jnp.concatenate([jnp.where(m, q_dec, 0.0) for m in head_masks], axis=0).astype(BF16)
        scores = _dot_t1(q_stack, k_dec)
        scores = jnp.where(tril4, scores, 0.0).astype(BF16)
        o_intra = _dot(scores, v)
        o_inter = _dot(q_dec_bf, state_ref[...].astype(BF16))
        kv = _dot_t0(k_tail, v)
        dec = jnp.exp(_dot_t0(la_hi, ones_bf) + _dot_t0(la_lo, ones_bf))
        for h in range(GLA_HEADS):
            rs = slice(h * GLA_DK, (h + 1) * GLA_DK)
            cs = slice(h * GLA_DV, (h + 1) * GLA_DV)
            o_h = o_intra[h * C:(h + 1) * C, cs] + o_inter[:, cs]
            ms = jnp.mean(o_h * o_h, axis=-1, keepdims=True)
            o_h = o_h * lax.rsqrt(ms + LN_EPS) * gnorm_ref[:, cs]
            r_h = qkvr_ref[pl.ds(r0, C), 2 * GLA_KEY + GLA_VAL + h * GLA_DV:2 * GLA_KEY + GLA_VAL + (h + 1) * GLA_DV]
            ya_ref[pl.ds(r0, C), cs] = (o_h * (r_h * _sigmoid(r_h))).astype(BF16)
            state_ref[rs, cs] = dec[rs, :] * state_ref[rs, cs] + kv[rs, cs]
        return carry

    lax.fori_loop(0, T // C, gla_chunk, 0)

    uv = proj(O_UV, O_XC)
    zg = 0.5 * uv * (1.0 + lax.erf(uv * (2.0 ** -0.5)))
    u = zg[:, :SGU_WIDTH]
    vln_ref[...] = _layer_norm(zg[:, SGU_WIDTH:], slng_ref[...], slnb_ref[...])
    lane_s = lax.broadcasted_iota(jnp.int32, (SGU_CHUNK, SGU_WIDTH), 1)
    s_parts = []
    for n in range(T // SGU_CHUNK):
        vc = vln_ref[n * SGU_CHUNK:(n + 1) * SGU_CHUNK, :]
        s = sbias_ref[...]
        for g in range(SGU_GROUPS):
            vg = jnp.where((lane_s >= g * SGU_GD) & (lane_s < (g + 1) * SGU_GD), vc, 0.0).astype(BF16)
            s = s + _dot(wtril_ref[g], vg)
        s_parts.append(s)
    y_b = (u * jnp.concatenate(s_parts, axis=0)).astype(BF16)

    P = POOL_CARRY
    xc = proj(O_XC, O_GATE)
    e_ref[P:P + T, :] = xc
    s2_ref[8:P + T, :] = e_ref[8:P + T, :] + e_ref[7:P + T - 1, :]
    s4_ref[16:P + T, :] = s2_ref[16:P + T, :] + s2_ref[14:P + T - 2, :]
    s8_ref[24:P + T, :] = s4_ref[24:P + T, :] + s4_ref[20:P + T - 4, :]
    s16 = s8_ref[P:P + T, :] + s8_ref[P - 8:P + T - 8, :]
    lane_p = lax.broadcasted_iota(jnp.int32, (T, POOL_WIDTH), 1)
    tpos = lax.broadcasted_iota(jnp.int32, (T, POOL_WIDTH), 0) + (j * T + 1)
    grp = lane_p >> 6
    win = jnp.where(grp == 0, POOL_WINDOWS[0], jnp.where(grp == 1, POOL_WINDOWS[1],
                    jnp.where(grp == 2, POOL_WINDOWS[2], POOL_WINDOWS[3])))
    wsum = jnp.where(grp == 0, s2_ref[P:P + T, :], jnp.where(grp == 1, s4_ref[P:P + T, :],
                     jnp.where(grp == 2, s8_ref[P:P + T, :], s16)))
    count = jnp.minimum(tpos, win).astype(F32)
    pooled = wsum / count - xc
    y_c = (_dot(pooled.astype(BF16), poolw_ref[...]) * pscale_ref[...]).astype(BF16)

    ra, rb = GLA_VAL, GLA_VAL + SGU_WIDTH
    merged = _sigmoid(proj(O_GATE, O_GATE + D)) * _dot(ya_ref[...], wup_ref[0:ra, :])
    merged += _sigmoid(proj(O_GATE + D, O_GATE + 2 * D)) * _dot(y_b, wup_ref[ra:rb, :])
    merged += _sigmoid(proj(O_GATE + 2 * D, O_GATE + 3 * D)) * _dot(y_c, wup_ref[rb:, :])
    h = _dot(merged.astype(BF16), wo_ref[...])
    out_ref[0] = _layer_norm(DEEPNORM_ALPHA * x + h, ln1g_ref[...], ln1b_ref[...])


_MIXER_WEIGHTS = ('wcat', 'bcat', 'wg2', 'bg', 'gnorm', 'slng', 'slnb', 'wtril', 'sbias', 'poolw', 'pscale',
                  'wup', 'wo', 'ln1g', 'ln1b')


def _mixer(x, p, l):
    B, S, D = x.shape
    T = MIX_TILE
    weights = [p[n] for n in _MIXER_WEIGHTS]
    return pl.pallas_call(
        _mixer_kernel,
        out_shape=jax.ShapeDtypeStruct((B, S, D), F32),
        grid=(B, S // T),
        in_specs=[pl.BlockSpec((1, T, D), lambda b, j: (b, j, 0))] + [_layer_spec(w, l) for w in weights],
        out_specs=pl.BlockSpec((1, T, D), lambda b, j: (b, j, 0)),
        scratch_shapes=[
            pltpu.VMEM((GLA_KEY, GLA_VAL), F32),
            pltpu.VMEM((T, 2 * GLA_KEY + 2 * GLA_VAL), F32),
            pltpu.VMEM((T, GLA_KEY), F32),
            pltpu.VMEM((T, GLA_VAL), BF16),
            pltpu.VMEM((T, SGU_WIDTH), F32),
            pltpu.VMEM((T + POOL_CARRY, POOL_WIDTH), F32),
            pltpu.VMEM((T + POOL_CARRY, POOL_WIDTH), F32),
            pltpu.VMEM((T + POOL_CARRY, POOL_WIDTH), F32),
            pltpu.VMEM((T + POOL_CARRY, POOL_WIDTH), F32),
        ],
        compiler_params=pltpu.CompilerParams(dimension_semantics=("arbitrary", "arbitrary"),
                                             vmem_limit_bytes=VMEM_LIMIT),
        name="mixer",
    )(x, *weights)


def _memkv_kernel(memt_ref, mem_ref, wkt_ref, wv_ref, kt_ref, v_ref):
    kt_ref[0] = _dot(wkt_ref[...], memt_ref[0].astype(BF16)).astype(BF16)
    v_ref[0] = _dot(mem_ref[0].astype(BF16), wv_ref[...]).astype(BF16)


def _memkv(mem, memt, p, l):
    B, M, D = mem.shape
    return pl.pallas_call(
        _memkv_kernel,
        out_shape=(jax.ShapeDtypeStruct((B, D, M), BF16), jax.ShapeDtypeStruct((B, M, D), BF16)),
        grid=(B,),
        in_specs=[pl.BlockSpec((1, D, M), lambda b: (b, 0, 0)), pl.BlockSpec((1, M, D), lambda b: (b, 0, 0)),
                  _layer_spec(p['wkt'], l), _layer_spec(p['wv'], l)],
        out_specs=(pl.BlockSpec((1, D, M), lambda b: (b, 0, 0)), pl.BlockSpec((1, M, D), lambda b: (b, 0, 0))),
        compiler_params=pltpu.CompilerParams(dimension_semantics=("arbitrary",), vmem_limit_bytes=VMEM_LIMIT),
        name="memkv",
    )(memt, mem, p['wkt'], p['wv'])


def _xattn_kernel(x_ref, kt_ref, v_ref, wq_ref, wo_ref, ln2g_ref, ln2b_ref, rwhi_ref, rwlo_ref, rb_ref,
                  x2_ref, logit_ref):
    x = x_ref[0]
    q = (_dot(x.astype(BF16), wq_ref[...]) * (XA_DH ** -0.5)).astype(BF16)
    h = jnp.zeros_like(x)
    for hd in range(XA_HEADS):
        cs = slice(hd * XA_DH, (hd + 1) * XA_DH)
        s = _dot(q[:, cs], kt_ref[0, cs, :])
        e = jnp.exp(s - jnp.max(s, axis=-1, keepdims=True))
        o = _dot(e.astype(BF16), v_ref[0, :, cs]) / jnp.sum(e, axis=-1, keepdims=True)
        h = h + _dot(o.astype(BF16), wo_ref[cs, :])
    x2 = _layer_norm(DEEPNORM_ALPHA * x + h, ln2g_ref[...], ln2b_ref[...])
    x2_ref[0] = x2
    hi, lo = _split_bf16(x2)
    logit_ref[0] = _dot(hi, rwhi_ref[...]) + (_dot(hi, rwlo_ref[...]) + _dot(lo, rwhi_ref[...])) + rb_ref[...]


def _xattn(x, kt, v, p, l):
    B, S, D = x.shape
    T = XA_TILE
    M = MEM_LEN
    weights = [p[n] for n in ('wq', 'wxo', 'ln2g', 'ln2b', 'rwhi', 'rwlo', 'rb')]
    return pl.pallas_call(
        _xattn_kernel,
        out_shape=(jax.ShapeDtypeStruct((B, S, D), F32), jax.ShapeDtypeStruct((B, S, LANES), F32)),
        grid=(B, S // T),
        in_specs=[pl.BlockSpec((1, T, D), lambda b, j: (b, j, 0)),
                  pl.BlockSpec((1, D, M), lambda b, j: (b, 0, 0)),
                  pl.BlockSpec((1, M, D), lambda b, j: (b, 0, 0))] + [_layer_spec(w, l) for w in weights],
        out_specs=(pl.BlockSpec((1, T, D), lambda b, j: (b, j, 0)),
                   pl.BlockSpec((1, T, LANES), lambda b, j: (b, j, 0))),
        compiler_params=pltpu.CompilerParams(dimension_semantics=("arbitrary", "arbitrary"),
                                             vmem_limit_bytes=VMEM_LIMIT),
        name="xattn",
    )(x, kt, v, *weights)


def _expert_kernel(be_ref, nb_ref, xs_ref, wgu_ref, bgu_ref, wd_ref, bd_ref, out_ref, wgu_bf, wd_bf):
    i = pl.program_id(0)

    @pl.when(i < nb_ref[0])
    def _():
        e = be_ref[i]
        prev = be_ref[jnp.maximum(i - 1, 0)]

        @pl.when((i == 0) | (e != prev))
        def _():
            wgu_bf[...] = wgu_ref[...].astype(BF16)
            wd_bf[...] = wd_ref[...].astype(BF16)

        hh = _dot(xs_ref[...].astype(BF16), wgu_bf[...]) + bgu_ref[...]
        h_glu = jnp.minimum(hh[:, :EXPERT_FF], SWIGLU_LIMIT)
        h_lin = jnp.clip(hh[:, EXPERT_FF:], -SWIGLU_LIMIT, SWIGLU_LIMIT)
        a = h_glu * _sigmoid(SWIGLU_ALPHA * h_glu) * (h_lin + 1.0)
        out_ref[...] = _dot(a.astype(BF16), wd_bf[...]) + bd_ref[...]


def _experts(block_expert, n_used, xs, w_gu, b_gu, w_down, b_down, l):
    P, D = xs.shape
    NB = P // MOE_BLOCK
    F2 = 2 * EXPERT_FF

    def row_map(i, be, nb):
        return (jnp.minimum(i, nb[0] - 1), 0)

    def exp_map(i, be, nb):
        return (l, be[jnp.minimum(i, nb[0] - 1)], 0, 0)

    grid_spec = pltpu.PrefetchScalarGridSpec(
        num_scalar_prefetch=2,
        grid=(NB,),
        in_specs=[pl.BlockSpec((MOE_BLOCK, D), row_map),
                  pl.BlockSpec((None, None, D, F2), exp_map),
                  pl.BlockSpec((None, None, 1, F2), exp_map),
                  pl.BlockSpec((None, None, EXPERT_FF, D), exp_map),
                  pl.BlockSpec((None, None, 1, D), exp_map)],
        out_specs=pl.BlockSpec((MOE_BLOCK, D), row_map),
        scratch_shapes=[pltpu.VMEM((D, F2), BF16), pltpu.VMEM((EXPERT_FF, D), BF16)],
    )
    return pl.pallas_call(
        _expert_kernel,
        out_shape=jax.ShapeDtypeStruct((P, D), F32),
        grid_spec=grid_spec,
        compiler_params=pltpu.CompilerParams(dimension_semantics=("arbitrary",), vmem_limit_bytes=VMEM_LIMIT),
        name="experts",
    )(block_expert, n_used, xs, w_gu, b_gu.reshape(DEPTH, N_EXPERTS, 1, F2), w_down,
      b_down.reshape(DEPTH, N_EXPERTS, 1, D))


def _sc_gather_rows(x, idx):
    M = idx.shape[0]
    D = x.shape[1]
    W = SC_GATHER_WINDOW
    mesh = plsc.VectorSubcoreMesh(core_axis_name="core", subcore_axis_name="subcore")
    n_workers = mesh.num_cores * mesh.num_subcores
    rows_per = M // n_workers
    assert rows_per * n_workers == M and rows_per % W == 0

    @pl.kernel(out_type=jax.ShapeDtypeStruct((M, D), x.dtype), mesh=mesh, name="sc_gather_rows",
               scratch_types=[pltpu.VMEM((rows_per,), jnp.int32), pltpu.VMEM((W, D), x.dtype)])
    def gather_kernel(x_hbm, i_hbm, o_hbm, idx_vmem, buf):
        wid = lax.axis_index("core") * mesh.num_subcores + lax.axis_index("subcore")
        base = wid * rows_per
        pltpu.sync_copy(i_hbm.at[pl.ds(base, rows_per)], idx_vmem)

        @pl.loop(0, rows_per // W)
        def _(j):
            pltpu.sync_copy(x_hbm.at[idx_vmem.at[pl.ds(j * W, W)]], buf)
            pltpu.sync_copy(buf, o_hbm.at[pl.ds(base + j * W, W)])

    return gather_kernel(x, idx)


def _sc_scatter_rows(x, idx, n_out):
    K, N = idx.shape
    D = x.shape[1]
    W = SC_GATHER_WINDOW
    mesh = plsc.VectorSubcoreMesh(core_axis_name="core", subcore_axis_name="subcore")
    n_workers = mesh.num_cores * mesh.num_subcores
    rows_per = N // n_workers
    assert rows_per * n_workers == N and rows_per % W == 0

    @pl.kernel(out_type=jax.ShapeDtypeStruct((n_out, D), x.dtype), mesh=mesh, name="sc_scatter_rows",
               scratch_types=[pltpu.VMEM((K * rows_per,), jnp.int32), pltpu.VMEM((W, D), x.dtype)])
    def scatter_kernel(x_hbm, i_hbm, o_hbm, idx_vmem, buf):
        wid = lax.axis_index("core") * mesh.num_subcores + lax.axis_index("subcore")
        base = wid * rows_per
        for k in range(K):
            pltpu.sync_copy(i_hbm.at[pl.ds(k * N + base, rows_per)], idx_vmem.at[pl.ds(k * rows_per, rows_per)])

        @pl.loop(0, rows_per // W)
        def _(j):
            pltpu.sync_copy(x_hbm.at[pl.ds(base + j * W, W)], buf)
            for k in range(K):
                pltpu.sync_copy(buf, o_hbm.at[idx_vmem.at[pl.ds(k * rows_per + j * W, W)]])

    return scatter_kernel(x, idx.reshape(K * N))


def _combine_kernel(x_ref, yg_ref, gate_ref, ln3g_ref, ln3b_ref, out_ref):
    x = x_ref[...]
    g = gate_ref[...]
    y = jnp.zeros_like(x)
    for k in range(TOP_K):
        y = y + g[:, k:k + 1] * yg_ref[k * CMB_TILE:(k + 1) * CMB_TILE, :]
    out_ref[...] = _layer_norm(DEEPNORM_ALPHA * x + y, ln3g_ref[...], ln3b_ref[...])


def _combine(x2, yg, gate, p, l):
    N, D = x2.shape
    T = CMB_TILE
    return pl.pallas_call(
        _combine_kernel,
        out_shape=jax.ShapeDtypeStruct((N, D), F32),
        grid=(N // T,),
        in_specs=[pl.BlockSpec((T, D), lambda i: (i, 0)), pl.BlockSpec((TOP_K * T, D), lambda i: (i, 0)),
                  pl.BlockSpec((T, TOP_K), lambda i: (i, 0)), _layer_spec(p['ln3g'], l), _layer_spec(p['ln3b'], l)],
        out_specs=pl.BlockSpec((T, D), lambda i: (i, 0)),
        compiler_params=pltpu.CompilerParams(dimension_semantics=("arbitrary",), vmem_limit_bytes=VMEM_LIMIT),
        name="combine",
    )(x2, yg, gate, p['ln3g'], p['ln3b'])


def _prep(w_in, b_in, gla_wg2, gla_bg, gla_norm_g, sgu_ln_g, sgu_ln_b, sgu_ws, sgu_bs, pool_w, pool_scale,
          w_up_a, w_up_b, w_up_c, w_o, ln1_g, ln1_b, xa_wq, xa_wk, xa_wv, xa_wo, ln2_g, ln2_b,
          router_w, router_b, ln3_g, ln3_b):
    L = w_in.shape[0]
    o_glow = O_GLOW
    o_uv = o_glow + GLA_RANK
    row = lambda a: a.reshape(L, 1, -1).astype(F32)
    pad_last = lambda a, n: jnp.pad(a, [(0, 0)] * (a.ndim - 1) + [(0, n - a.shape[-1])])
    p = {}
    p['wcat'] = jnp.concatenate([w_in[..., :o_glow], pad_last(w_in[..., o_glow:o_uv], LANES), w_in[..., o_uv:]],
                                axis=-1).astype(BF16)
    p['bcat'] = row(jnp.concatenate([b_in[..., :o_glow], pad_last(b_in[..., o_glow:o_uv], LANES), b_in[..., o_uv:]],
                                    axis=-1))
    p['wg2'] = jnp.pad(gla_wg2, ((0, 0), (0, LANES - GLA_RANK), (0, 0))).astype(BF16)
    p['bg'] = row(gla_bg)
    p['gnorm'] = row(gla_norm_g)
    p['slng'] = row(sgu_ln_g)
    p['slnb'] = row(sgu_ln_b)
    p['wtril'] = jnp.tril(sgu_ws).astype(BF16)
    p['sbias'] = jnp.repeat(jnp.swapaxes(sgu_bs, 1, 2), SGU_GD, axis=2).astype(F32)
    G = len(POOL_WINDOWS)
    eye = jnp.eye(G, dtype=F32)
    p['poolw'] = jnp.einsum('lgcd,gh->lgchd', pool_w, eye).reshape(L, POOL_WIDTH, POOL_WIDTH).astype(BF16)
    p['pscale'] = row(pool_scale)
    p['wup'] = jnp.concatenate([w_up_a, w_up_b, w_up_c], axis=1).astype(BF16)
    p['wo'] = w_o.astype(BF16)
    p['ln1g'], p['ln1b'] = row(ln1_g), row(ln1_b)
    p['wq'] = xa_wq.astype(BF16)
    p['wkt'] = jnp.swapaxes(xa_wk, 1, 2).astype(BF16)
    p['wv'] = xa_wv.astype(BF16)
    p['wxo'] = xa_wo.astype(BF16)
    p['ln2g'], p['ln2b'] = row(ln2_g), row(ln2_b)
    rw = pad_last(router_w, LANES)
    p['rwhi'] = rw.astype(BF16)
    p['rwlo'] = (rw - p['rwhi'].astype(F32)).astype(BF16)
    p['rb'] = row(pad_last(router_b, LANES))
    p['ln3g'], p['ln3b'] = row(ln3_g), row(ln3_b)
    return p


def _route(logits):
    N = logits.shape[0]
    A = N * TOP_K
    top_vals, top_idx = lax.top_k(logits, TOP_K)
    gate = jax.nn.softmax(top_vals, axis=-1)
    flat_e = top_idx.reshape(-1)
    onehot = (flat_e[:, None] == jnp.arange(N_EXPERTS, dtype=jnp.int32)[None, :]).astype(jnp.int32)
    csum = jnp.cumsum(onehot, axis=0)
    counts = csum[-1]
    padded = ((counts + MOE_BLOCK - 1) // MOE_BLOCK) * MOE_BLOCK
    pad_end = jnp.cumsum(padded)
    pad_start = pad_end - padded
    dest = jnp.sum(onehot * (csum - 1 + pad_start[None, :]), axis=1)
    n_blocks = A // MOE_BLOCK + N_EXPERTS
    block_start = jnp.arange(n_blocks, dtype=jnp.int32) * MOE_BLOCK
    block_expert = jnp.minimum(jnp.sum((pad_end[None, :] <= block_start[:, None]).astype(jnp.int32), axis=1),
                               N_EXPERTS - 1)
    n_used = (pad_end[-1] // MOE_BLOCK).astype(jnp.int32).reshape(1)
    return gate, dest, block_expert, n_used


def kernel(x, mem, w_in, b_in, gla_wg2, gla_bg, gla_norm_g, sgu_ln_g, sgu_ln_b, sgu_ws, sgu_bs, pool_w, pool_scale, w_up_a, w_up_b, w_up_c, w_o, ln1_g, ln1_b, xa_wq, xa_wk, xa_wv, xa_wo, ln2_g, ln2_b, router_w, router_b, exp_w_gu, exp_b_gu, exp_w_down, exp_b_down, ln3_g, ln3_b):
    B, S, D = x.shape
    N = B * S
    p = _prep(w_in, b_in, gla_wg2, gla_bg, gla_norm_g, sgu_ln_g, sgu_ln_b, sgu_ws, sgu_bs, pool_w, pool_scale,
              w_up_a, w_up_b, w_up_c, w_o, ln1_g, ln1_b, xa_wq, xa_wk, xa_wv, xa_wo, ln2_g, ln2_b,
              router_w, router_b, ln3_g, ln3_b)
    memt = jnp.swapaxes(mem, 1, 2)
    for l in range(DEPTH):
        x1 = _mixer(x, p, l)
        kt, v = _memkv(mem, memt, p, l)
        x2, logits = _xattn(x1, kt, v, p, l)
        gate, dest, block_expert, n_used = _route(logits.reshape(N, LANES)[:, :N_EXPERTS])
        n_slots = (N * TOP_K // MOE_BLOCK + N_EXPERTS) * MOE_BLOCK
        xs = _sc_scatter_rows(x2.reshape(N, D), dest.reshape(N, TOP_K).T, n_slots)
        ys = _experts(block_expert, n_used, xs, exp_w_gu, exp_b_gu, exp_w_down, exp_b_down, l)
        dest_km = dest.reshape(N // CMB_TILE, CMB_TILE, TOP_K).transpose(0, 2, 1).reshape(-1)
        yg = _sc_gather_rows(ys, dest_km)
        x = _combine(x2.reshape(N, D), yg, gate, p, l).reshape(B, S, D)
    return x
```

```python
import functools

import jax
import jax.numpy as jnp
from jax import lax
from jax.experimental import pallas as pl
from jax.experimental.pallas import tpu as pltpu
from jax.experimental.pallas import tpu_sc as plsc

F32 = jnp.float32
BF16 = jnp.bfloat16

D_MODEL = 1024
DEPTH = 2
GLA_HEADS = 4
GLA_KEY = 256
GLA_VAL = 512
GLA_DK = 64
GLA_DV = 128
GLA_RANK = 16
GLA_TAU = 16.0
GLA_CHUNK = 64
SGU_GROUPS = 4
SGU_WIDTH = 256
SGU_GD = 64
SGU_CHUNK = 128
POOL_WINDOWS = (2, 4, 8, 16)
POOL_WIDTH = 256
POOL_GD = 64
POOL_CARRY = 32
MEM_LEN = 256
XA_HEADS = 4
XA_DH = 256
N_EXPERTS = 32
TOP_K = 4
EXPERT_FF = 1024
SWIGLU_LIMIT = 7.0
SWIGLU_ALPHA = 1.702
DEEPNORM_ALPHA = (2 * DEPTH) ** 0.25
LN_EPS = 1e-5
LANES = 128
VMEM_LIMIT = 56 * 1024 * 1024

MIX_TILE = 256
XA_TILE = 512
MOE_BLOCK = 256
CMB_TILE = 256
SC_GATHER_WINDOW = 32

O_QKVR = 0
O_GLOW = 2 * GLA_KEY + 2 * GLA_VAL
O_UV = O_GLOW + LANES
O_XC = O_UV + 2 * SGU_WIDTH
O_GATE = O_XC + POOL_WIDTH
N_PACK = O_GATE + 3 * D_MODEL


def _dot(a, b):
    return jnp.dot(a, b, preferred_element_type=F32)


def _dot_t0(a, b):
    return lax.dot_general(a, b, (((0,), (0,)), ((), ())), preferred_element_type=F32)


def _dot_t1(a, b):
    return lax.dot_general(a, b, (((1,), (1,)), ((), ())), preferred_element_type=F32)


def _split_bf16(x):
    hi = x.astype(BF16)
    lo = (x - hi.astype(F32)).astype(BF16)
    return hi, lo


def _layer_norm(x, g, b):
    mu = jnp.mean(x, axis=-1, keepdims=True)
    xc = x - mu
    var = jnp.mean(xc * xc, axis=-1, keepdims=True)
    return xc * lax.rsqrt(var + LN_EPS) * g + b


def _sigmoid(x):
    return 1.0 / (1.0 + jnp.exp(-x))


def _layer_spec(arr, l):
    nd = arr.ndim - 1
    return pl.BlockSpec((None,) + arr.shape[1:], lambda *_: (l,) + (0,) * nd, pipeline_mode=pl.Buffered(1))


def _mixer_kernel(x_ref, wcat_ref, bcat_ref, wg2_ref, bg_ref, gnorm_ref,
                  slng_ref, slnb_ref, wtril_ref, sbias_ref, poolw_ref, pscale_ref,
                  wup_ref, wo_ref, ln1g_ref, ln1b_ref,
                  out_ref,
                  state_ref, qkvr_ref, la_ref, ya_ref, vln_ref, e_ref, s2_ref, s4_ref, s8_ref):
    T = MIX_TILE
    D = D_MODEL
    j = pl.program_id(1)
    x = x_ref[0]
    xb = x.astype(BF16)

    def proj(lo, hi):
        return _dot(xb, wcat_ref[:, lo:hi]) + bcat_ref[:, lo:hi]

    @pl.when(j == 0)
    def _():
        state_ref[...] = jnp.zeros_like(state_ref)
        e_ref[0:POOL_CARRY, :] = jnp.zeros((POOL_CARRY, POOL_WIDTH), F32)

    @pl.when(j > 0)
    def _():
        e_ref[0:POOL_CARRY, :] = e_ref[T:T + POOL_CARRY, :]

    qkvr_ref[...] = proj(O_QKVR, O_GLOW)
    glow = proj(O_GLOW, O_UV)
    z = _dot(glow.astype(BF16), wg2_ref[...]) + bg_ref[...]
    la_ref[...] = (jnp.minimum(z, 0.0) - jnp.log1p(jnp.exp(-jnp.abs(z)))) * (1.0 / GLA_TAU)

    C = GLA_CHUNK
    row = lax.broadcasted_iota(jnp.int32, (C, C), 0)
    col = lax.broadcasted_iota(jnp.int32, (C, C), 1)
    tril_bf = jnp.where(row >= col, 1.0, 0.0).astype(BF16)
    row4 = lax.broadcasted_iota(jnp.int32, (GLA_HEADS * C, C), 0) & (C - 1)
    tril4 = row4 >= lax.broadcasted_iota(jnp.int32, (GLA_HEADS * C, C), 1)
    ones_bf = jnp.ones((C, GLA_DV), BF16)
    lane = lax.broadcasted_iota(jnp.int32, (C, GLA_KEY), 1)
    head_masks = [(lane >= h * GLA_DK) & (lane < (h + 1) * GLA_DK) for h in range(GLA_HEADS)]

    def gla_chunk(c, carry):
        r0 = pl.multiple_of(c * C, C)
        la = la_ref[pl.ds(r0, C), :]
        la_hi, la_lo = _split_bf16(la)
        b = _dot(tril_bf, la_hi) + _dot(tril_bf, la_lo)
        b_last = b[C - 1:C, :]
        q = qkvr_ref[pl.ds(r0, C), 0:GLA_KEY]
        k = qkvr_ref[pl.ds(r0, C), GLA_KEY:2 * GLA_KEY]
        v = qkvr_ref[pl.ds(r0, C), 2 * GLA_KEY:2 * GLA_KEY + GLA_VAL].astype(BF16)
        q_dec = q * (GLA_DK ** -0.5) * jnp.exp(b)
        k_dec = (k * jnp.exp(-b)).astype(BF16)
        k_tail = (k * jnp.exp(b_last - b)).astype(BF16)
        q_dec_bf = q_dec.astype(BF16)
        q_stack = jnp.concatenate([jnp.where(m, q_dec, 0.0) for m in head_masks], axis=0).astype(BF16)
        scores = _dot_t1(q_stack, k_dec)
        scores = jnp.where(tril4, scores, 0.0).astype(BF16)
        o_intra = _dot(scores, v)
        o_inter = _dot(q_dec_bf, state_ref[...].astype(BF16))
        kv = _dot_t0(k_tail, v)
        dec = jnp.exp(_dot_t0(la_hi, ones_bf) + _dot_t0(la_lo, ones_bf))
        for h in range(GLA_HEADS):
            rs = slice(h * GLA_DK, (h + 1) * GLA_DK)
            cs = slice(h * GLA_DV, (h + 1) * GLA_DV)
            o_h = o_intra[h * C:(h + 1) * C, cs] + o_inter[:, cs]
            ms = jnp.mean(o_h * o_h, axis=-1, keepdims=True)
            o_h = o_h * lax.rsqrt(ms + LN_EPS) * gnorm_ref[:, cs]
            r_h = qkvr_ref[pl.ds(r0, C), 2 * GLA_KEY + GLA_VAL + h * GLA_DV:2 * GLA_KEY + GLA_VAL + (h + 1) * GLA_DV]
            ya_ref[pl.ds(r0, C), cs] = (o_h * (r_h * _sigmoid(r_h))).astype(BF16)
            state_ref[rs, cs] = dec[rs, :] * state_ref[rs, cs] + kv[rs, cs]
        return carry

    lax.fori_loop(0, T // C, gla_chunk, 0)

    uv = proj(O_UV, O_XC)
    zg = 0.5 * uv * (1.0 + lax.erf(uv * (2.0 ** -0.5)))
    u = zg[:, :SGU_WIDTH]
    vln_ref[...] = _layer_norm(zg[:, SGU_WIDTH:], slng_ref[...], slnb_ref[...])
    lane_s = lax.broadcasted_iota(jnp.int32, (SGU_CHUNK, SGU_WIDTH), 1)
    s_parts = []
    for n in range(T // SGU_CHUNK):
        vc = vln_ref[n * SGU_CHUNK:(n + 1) * SGU_CHUNK, :]
        s = sbias_ref[...]
        for g in range(SGU_GROUPS):
            vg = jnp.where((lane_s >= g * SGU_GD) & (lane_s < (g + 1) * SGU_GD), vc, 0.0).astype(BF16)
            s = s + _dot(wtril_ref[g], vg)
        s_parts.append(s)
    y_b = (u * jnp.concatenate(s_parts, axis=0)).astype(BF16)

    P = POOL_CARRY
    xc = proj(O_XC, O_GATE)
    e_ref[P:P + T, :] = xc
    s2_ref[8:P + T, :] = e_ref[8:P + T, :] + e_ref[7:P + T - 1, :]
    s4_ref[16:P + T, :] = s2_ref[16:P + T, :] + s2_ref[14:P + T - 2, :]
    s8_ref[24:P + T, :] = s4_ref[24:P + T, :] + s4_ref[20:P + T - 4, :]
    s16 = s8_ref[P:P + T, :] + s8_ref[P - 8:P + T - 8, :]
    lane_p = lax.broadcasted_iota(jnp.int32, (T, POOL_WIDTH), 1)
    tpos = lax.broadcasted_iota(jnp.int32, (T, POOL_WIDTH), 0) + (j * T + 1)
    grp = lane_p >> 6
    win = jnp.where(grp == 0, POOL_WINDOWS[0], jnp.where(grp == 1, POOL_WINDOWS[1],
                    jnp.where(grp == 2, POOL_WINDOWS[2], POOL_WINDOWS[3])))
    wsum = jnp.where(grp == 0, s2_ref[P:P + T, :], jnp.where(grp == 1, s4_ref[P:P + T, :],
                     jnp.where(grp == 2, s8_ref[P:P + T, :], s16)))
    count = jnp.minimum(tpos, win).astype(F32)
    pooled = wsum / count - xc
    y_c = (_dot(pooled.astype(BF16), poolw_ref[...]) * pscale_ref[...]).astype(BF16)

    ra, rb = GLA_VAL, GLA_VAL + SGU_WIDTH
    merged = _sigmoid(proj(O_GATE, O_GATE + D)) * _dot(ya_ref[...], wup_ref[0:ra, :])
    merged += _sigmoid(proj(O_GATE + D, O_GATE + 2 * D)) * _dot(y_b, wup_ref[ra:rb, :])
    merged += _sigmoid(proj(O_GATE + 2 * D, O_GATE + 3 * D)) * _dot(y_c, wup_ref[rb:, :])
    h = _dot(merged.astype(BF16), wo_ref[...])
    out_ref[0] = _layer_norm(DEEPNORM_ALPHA * x + h, ln1g_ref[...], ln1b_ref[...])


_MIXER_WEIGHTS = ('wcat', 'bcat', 'wg2', 'bg', 'gnorm', 'slng', 'slnb', 'wtril', 'sbias', 'poolw', 'pscale',
                  'wup', 'wo', 'ln1g', 'ln1b')


def _mixer(x, p, l):
    B, S, D = x.shape
    T = MIX_TILE
    weights = [p[n] for n in _MIXER_WEIGHTS]
    return pl.pallas_call(
        _mixer_kernel,
        out_shape=jax.ShapeDtypeStruct((B, S, D), F32),
        grid=(B, S // T),
        in_specs=[pl.BlockSpec((1, T, D), lambda b, j: (b, j, 0))] + [_layer_spec(w, l) for w in weights],
        out_specs=pl.BlockSpec((1, T, D), lambda b, j: (b, j, 0)),
        scratch_shapes=[
            pltpu.VMEM((GLA_KEY, GLA_VAL), F32),
            pltpu.VMEM((T, 2 * GLA_KEY + 2 * GLA_VAL), F32),
            pltpu.VMEM((T, GLA_KEY), F32),
            pltpu.VMEM((T, GLA_VAL), BF16),
            pltpu.VMEM((T, SGU_WIDTH), F32),
            pltpu.VMEM((T + POOL_CARRY, POOL_WIDTH), F32),
            pltpu.VMEM((T + POOL_CARRY, POOL_WIDTH), F32),
            pltpu.VMEM((T + POOL_CARRY, POOL_WIDTH), F32),
            pltpu.VMEM((T + POOL_CARRY, POOL_WIDTH), F32),
        ],
        compiler_params=pltpu.CompilerParams(dimension_semantics=("arbitrary", "arbitrary"),
                                             vmem_limit_bytes=VMEM_LIMIT),
        name="mixer",
    )(x, *weights)


def _memkv_kernel(memt_ref, mem_ref, wkt_ref, wv_ref, kt_ref, v_ref):
    kt_ref[0] = _dot(wkt_ref[...], memt_ref[0].astype(BF16)).astype(BF16)
    v_ref[0] = _dot(mem_ref[0].astype(BF16), wv_ref[...]).astype(BF16)


def _memkv(mem, memt, p, l):
    B, M, D = mem.shape
    return pl.pallas_call(
        _memkv_kernel,
        out_shape=(jax.ShapeDtypeStruct((B, D, M), BF16), jax.ShapeDtypeStruct((B, M, D), BF16)),
        grid=(B,),
        in_specs=[pl.BlockSpec((1, D, M), lambda b: (b, 0, 0)), pl.BlockSpec((1, M, D), lambda b: (b, 0, 0)),
                  _layer_spec(p['wkt'], l), _layer_spec(p['wv'], l)],
        out_specs=(pl.BlockSpec((1, D, M), lambda b: (b, 0, 0)), pl.BlockSpec((1, M, D), lambda b: (b, 0, 0))),
        compiler_params=pltpu.CompilerParams(dimension_semantics=("arbitrary",), vmem_limit_bytes=VMEM_LIMIT),
        name="memkv",
    )(memt, mem, p['wkt'], p['wv'])


def _xattn_kernel(x_ref, kt_ref, v_ref, wq_ref, wo_ref, ln2g_ref, ln2b_ref, rwhi_ref, rwlo_ref, rb_ref,
                  x2_ref, logit_ref):
    x = x_ref[0]
    q = (_dot(x.astype(BF16), wq_ref[...]) * (XA_DH ** -0.5)).astype(BF16)
    h = jnp.zeros_like(x)
    for hd in range(XA_HEADS):
        cs = slice(hd * XA_DH, (hd + 1) * XA_DH)
        s = _dot(q[:, cs], kt_ref[0, cs, :])
        e = jnp.exp(s - jnp.max(s, axis=-1, keepdims=True))
        o = _dot(e.astype(BF16), v_ref[0, :, cs]) / jnp.sum(e, axis=-1, keepdims=True)
        h = h + _dot(o.astype(BF16), wo_ref[cs, :])
    x2 = _layer_norm(DEEPNORM_ALPHA * x + h, ln2g_ref[...], ln2b_ref[...])
    x2_ref[0] = x2
    hi, lo = _split_bf16(x2)
    logit_ref[0] = _dot(hi, rwhi_ref[...]) + (_dot(hi, rwlo_ref[...]) + _dot(lo, rwhi_ref[...])) + rb_ref[...]


def _xattn(x, kt, v, p, l):
    B, S, D = x.shape
    T = XA_TILE
    M = MEM_LEN
    weights = [p[n] for n in ('wq', 'wxo', 'ln2g', 'ln2b', 'rwhi', 'rwlo', 'rb')]
    return pl.pallas_call(
        _xattn_kernel,
        out_shape=(jax.ShapeDtypeStruct((B, S, D), F32), jax.ShapeDtypeStruct((B, S, LANES), F32)),
        grid=(B, S // T),
        in_specs=[pl.BlockSpec((1, T, D), lambda b, j: (b, j, 0)),
                  pl.BlockSpec((1, D, M), lambda b, j: (b, 0, 0)),
                  pl.BlockSpec((1, M, D), lambda b, j: (b, 0, 0))] + [_layer_spec(w, l) for w in weights],
        out_specs=(pl.BlockSpec((1, T, D), lambda b, j: (b, j, 0)),
                   pl.BlockSpec((1, T, LANES), lambda b, j: (b, j, 0))),
        compiler_params=pltpu.CompilerParams(dimension_semantics=("arbitrary", "arbitrary"),
                                             vmem_limit_bytes=VMEM_LIMIT),
        name="xattn",
    )(x, kt, v, *weights)


def _expert_kernel(layer, be_ref, nb_ref, nxt_ref, xs_ref, wgu_hbm, bgu_ref, wd_hbm, bd_ref, out_ref,
                   wgu_st, wd_st, wgu_bf, wd_bf, sem):
    i = pl.program_id(0)
    F = EXPERT_FF

    def weight_copies(e):
        return (pltpu.make_async_copy(wgu_hbm.at[layer, e], wgu_st, sem.at[0]),
                pltpu.make_async_copy(wd_hbm.at[layer, e], wd_st, sem.at[1]))

    @pl.when(i < nb_ref[0])
    def _():
        e = be_ref[i]
        prev = be_ref[jnp.maximum(i - 1, 0)]

        @pl.when(i == 0)
        def _():
            for cp in weight_copies(e):
                cp.start()

        @pl.when((i == 0) | (e != prev))
        def _():
            for cp in weight_copies(e):
                cp.wait()
            wgu_bf[...] = wgu_st[...].astype(BF16)
            wd_bf[...] = wd_st[...].astype(BF16)
            nxt = nxt_ref[e]

            @pl.when(nxt != e)
            def _():
                for cp in weight_copies(nxt):
                    cp.start()

        hh = _dot(xs_ref[...].astype(BF16), wgu_bf[...]) + bgu_ref[...]
        h_glu = jnp.minimum(hh[:, :F], SWIGLU_LIMIT)
        h_lin = jnp.clip(hh[:, F:], -SWIGLU_LIMIT, SWIGLU_LIMIT)
        a = h_glu * _sigmoid(SWIGLU_ALPHA * h_glu) * (h_lin + 1.0)
        out_ref[...] = _dot(a.astype(BF16), wd_bf[...]) + bd_ref[...]


def _experts(block_expert, n_used, next_expert, xs, w_gu, b_gu, w_down, b_down, l):
    P, D = xs.shape
    NB = P // MOE_BLOCK
    F2 = 2 * EXPERT_FF

    def row_map(i, be, nb, nxt):
        return (jnp.minimum(i, nb[0] - 1), 0)

    def exp_map(i, be, nb, nxt):
        return (l, be[jnp.minimum(i, nb[0] - 1)], 0, 0)

    grid_spec = pltpu.PrefetchScalarGridSpec(
        num_scalar_prefetch=3,
        grid=(NB,),
        in_specs=[pl.BlockSpec((MOE_BLOCK, D), row_map),
                  pl.BlockSpec(memory_space=pl.ANY),
                  pl.BlockSpec((None, None, 1, F2), exp_map),
                  pl.BlockSpec(memory_space=pl.ANY),
                  pl.BlockSpec((None, None, 1, D), exp_map)],
        out_specs=pl.BlockSpec((MOE_BLOCK, D), row_map),
        scratch_shapes=[pltpu.VMEM((D, F2), F32), pltpu.VMEM((EXPERT_FF, D), F32),
                        pltpu.VMEM((D, F2), BF16), pltpu.VMEM((EXPERT_FF, D), BF16),
                        pltpu.SemaphoreType.DMA((2,))],
    )
    return pl.pallas_call(
        functools.partial(_expert_kernel, l),
        out_shape=jax.ShapeDtypeStruct((P, D), F32),
        grid_spec=grid_spec,
        compiler_params=pltpu.CompilerParams(dimension_semantics=("arbitrary",), vmem_limit_bytes=VMEM_LIMIT),
        name="experts",
    )(block_expert, n_used, next_expert, xs, w_gu, b_gu.reshape(DEPTH, N_EXPERTS, 1, F2), w_down,
      b_down.reshape(DEPTH, N_EXPERTS, 1, D))


def _sc_gather_rows(x, idx):
    M = idx.shape[0]
    D = x.shape[1]
    W = SC_GATHER_WINDOW
    mesh = plsc.VectorSubcoreMesh(core_axis_name="core", subcore_axis_name="subcore")
    n_workers = mesh.num_cores * mesh.num_subcores
    rows_per = M // n_workers
    assert rows_per * n_workers == M and rows_per % W == 0

    @pl.kernel(out_type=jax.ShapeDtypeStruct((M, D), x.dtype), mesh=mesh, name="sc_gather_rows",
               scratch_types=[pltpu.VMEM((rows_per,), jnp.int32), pltpu.VMEM((W, D), x.dtype)])
    def gather_kernel(x_hbm, i_hbm, o_hbm, idx_vmem, buf):
        wid = lax.axis_index("core") * mesh.num_subcores + lax.axis_index("subcore")
        base = wid * rows_per
        pltpu.sync_copy(i_hbm.at[pl.ds(base, rows_per)], idx_vmem)

        @pl.loop(0, rows_per // W)
        def _(j):
            pltpu.sync_copy(x_hbm.at[idx_vmem.at[pl.ds(j * W, W)]], buf)
            pltpu.sync_copy(buf, o_hbm.at[pl.ds(base + j * W, W)])

    return gather_kernel(x, idx)


def _sc_scatter_rows(x, idx, n_out):
    K, N = idx.shape
    D = x.shape[1]
    W = SC_GATHER_WINDOW
    mesh = plsc.VectorSubcoreMesh(core_axis_name="core", subcore_axis_name="subcore")
    n_workers = mesh.num_cores * mesh.num_subcores
    rows_per = N // n_workers
    assert rows_per * n_workers == N and rows_per % W == 0

    @pl.kernel(out_type=jax.ShapeDtypeStruct((n_out, D), x.dtype), mesh=mesh, name="sc_scatter_rows",
               scratch_types=[pltpu.VMEM((K * rows_per,), jnp.int32), pltpu.VMEM((W, D), x.dtype)])
    def scatter_kernel(x_hbm, i_hbm, o_hbm, idx_vmem, buf):
        wid = lax.axis_index("core") * mesh.num_subcores + lax.axis_index("subcore")
        base = wid * rows_per
        for k in range(K):
            pltpu.sync_copy(i_hbm.at[pl.ds(k * N + base, rows_per)], idx_vmem.at[pl.ds(k * rows_per, rows_per)])

        @pl.loop(0, rows_per // W)
        def _(j):
            pltpu.sync_copy(x_hbm.at[pl.ds(base + j * W, W)], buf)
            for k in range(K):
                pltpu.sync_copy(buf, o_hbm.at[idx_vmem.at[pl.ds(k * rows_per + j * W, W)]])

    return scatter_kernel(x, idx.reshape(K * N))


def _combine_kernel(x_ref, yg_ref, gate_ref, ln3g_ref, ln3b_ref, out_ref):
    x = x_ref[...]
    g = gate_ref[...]
    y = jnp.zeros_like(x)
    for k in range(TOP_K):
        y = y + g[:, k:k + 1] * yg_ref[k * CMB_TILE:(k + 1) * CMB_TILE, :]
    out_ref[...] = _layer_norm(DEEPNORM_ALPHA * x + y, ln3g_ref[...], ln3b_ref[...])


def _combine(x2, yg, gate, p, l):
    N, D = x2.shape
    T = CMB_TILE
    return pl.pallas_call(
        _combine_kernel,
        out_shape=jax.ShapeDtypeStruct((N, D), F32),
        grid=(N // T,),
        in_specs=[pl.BlockSpec((T, D), lambda i: (i, 0)), pl.BlockSpec((TOP_K * T, D), lambda i: (i, 0)),
                  pl.BlockSpec((T, TOP_K), lambda i: (i, 0)), _layer_spec(p['ln3g'], l), _layer_spec(p['ln3b'], l)],
        out_specs=pl.BlockSpec((T, D), lambda i: (i, 0)),
        compiler_params=pltpu.CompilerParams(dimension_semantics=("arbitrary",), vmem_limit_bytes=VMEM_LIMIT),
        name="combine",
    )(x2, yg, gate, p['ln3g'], p['ln3b'])


def _prep(w_in, b_in, gla_wg2, gla_bg, gla_norm_g, sgu_ln_g, sgu_ln_b, sgu_ws, sgu_bs, pool_w, pool_scale,
          w_up_a, w_up_b, w_up_c, w_o, ln1_g, ln1_b, xa_wq, xa_wk, xa_wv, xa_wo, ln2_g, ln2_b,
          router_w, router_b, ln3_g, ln3_b):
    L = w_in.shape[0]
    o_glow = O_GLOW
    o_uv = o_glow + GLA_RANK
    row = lambda a: a.reshape(L, 1, -1).astype(F32)
    pad_last = lambda a, n: jnp.pad(a, [(0, 0)] * (a.ndim - 1) + [(0, n - a.shape[-1])])
    p = {}
    p['wcat'] = jnp.concatenate([w_in[..., :o_glow], pad_last(w_in[..., o_glow:o_uv], LANES), w_in[..., o_uv:]],
                                axis=-1).astype(BF16)
    p['bcat'] = row(jnp.concatenate([b_in[..., :o_glow], pad_last(b_in[..., o_glow:o_uv], LANES), b_in[..., o_uv:]],
                                    axis=-1))
    p['wg2'] = jnp.pad(gla_wg2, ((0, 0), (0, LANES - GLA_RANK), (0, 0))).astype(BF16)
    p['bg'] = row(gla_bg)
    p['gnorm'] = row(gla_norm_g)
    p['slng'] = row(sgu_ln_g)
    p['slnb'] = row(sgu_ln_b)
    p['wtril'] = jnp.tril(sgu_ws).astype(BF16)
    p['sbias'] = jnp.repeat(jnp.swapaxes(sgu_bs, 1, 2), SGU_GD, axis=2).astype(F32)
    G = len(POOL_WINDOWS)
    eye = jnp.eye(G, dtype=F32)
    p['poolw'] = jnp.einsum('lgcd,gh->lgchd', pool_w, eye).reshape(L, POOL_WIDTH, POOL_WIDTH).astype(BF16)
    p['pscale'] = row(pool_scale)
    p['wup'] = jnp.concatenate([w_up_a, w_up_b, w_up_c], axis=1).astype(BF16)
    p['wo'] = w_o.astype(BF16)
    p['ln1g'], p['ln1b'] = row(ln1_g), row(ln1_b)
    p['wq'] = xa_wq.astype(BF16)
    p['wkt'] = jnp.swapaxes(xa_wk, 1, 2).astype(BF16)
    p['wv'] = xa_wv.astype(BF16)
    p['wxo'] = xa_wo.astype(BF16)
    p['ln2g'], p['ln2b'] = row(ln2_g), row(ln2_b)
    rw = pad_last(router_w, LANES)
    p['rwhi'] = rw.astype(BF16)
    p['rwlo'] = (rw - p['rwhi'].astype(F32)).astype(BF16)
    p['rb'] = row(pad_last(router_b, LANES))
    p['ln3g'], p['ln3b'] = row(ln3_g), row(ln3_b)
    return p


def _route(logits):
    N = logits.shape[0]
    A = N * TOP_K
    top_vals, top_idx = lax.top_k(logits, TOP_K)
    gate = jax.nn.softmax(top_vals, axis=-1)
    flat_e = top_idx.reshape(-1)
    onehot = (flat_e[:, None] == jnp.arange(N_EXPERTS, dtype=jnp.int32)[None, :]).astype(jnp.int32)
    csum = jnp.cumsum(onehot, axis=0)
    counts = csum[-1]
    padded = ((counts + MOE_BLOCK - 1) // MOE_BLOCK) * MOE_BLOCK
    pad_end = jnp.cumsum(padded)
    pad_start = pad_end - padded
    dest = jnp.sum(onehot * (csum - 1 + pad_start[None, :]), axis=1)
    n_blocks = A // MOE_BLOCK + N_EXPERTS
    block_start = jnp.arange(n_blocks, dtype=jnp.int32) * MOE_BLOCK
    block_expert = jnp.minimum(jnp.sum((pad_end[None, :] <= block_start[:, None]).astype(jnp.int32), axis=1),
                               N_EXPERTS - 1)
    n_used = (pad_end[-1] // MOE_BLOCK).astype(jnp.int32).reshape(1)
    ids = jnp.arange(N_EXPERTS, dtype=jnp.int32)
    later = jnp.where((ids[None, :] > ids[:, None]) & (counts[None, :] > 0), ids[None, :], N_EXPERTS)
    first_later = jnp.min(later, axis=1)
    next_expert = jnp.where(first_later < N_EXPERTS, first_later, ids).astype(jnp.int32)
    return gate, dest, block_expert, n_used, next_expert


def kernel(x, mem, w_in, b_in, gla_wg2, gla_bg, gla_norm_g, sgu_ln_g, sgu_ln_b, sgu_ws, sgu_bs, pool_w, pool_scale, w_up_a, w_up_b, w_up_c, w_o, ln1_g, ln1_b, xa_wq, xa_wk, xa_wv, xa_wo, ln2_g, ln2_b, router_w, router_b, exp_w_gu, exp_b_gu, exp_w_down, exp_b_down, ln3_g, ln3_b):
    B, S, D = x.shape
    N = B * S
    p = _prep(w_in, b_in, gla_wg2, gla_bg, gla_norm_g, sgu_ln_g, sgu_ln_b, sgu_ws, sgu_bs, pool_w, pool_scale,
              w_up_a, w_up_b, w_up_c, w_o, ln1_g, ln1_b, xa_wq, xa_wk, xa_wv, xa_wo, ln2_g, ln2_b,
              router_w, router_b, ln3_g, ln3_b)
    memt = jnp.swapaxes(mem, 1, 2)
    for l in range(DEPTH):
        x1 = _mixer(x, p, l)
        kt, v = _memkv(mem, memt, p, l)
        x2, logits = _xattn(x1, kt, v, p, l)
        gate, dest, block_expert, n_used, next_expert = _route(logits.reshape(N, LANES)[:, :N_EXPERTS])
        n_slots = (N * TOP_K // MOE_BLOCK + N_EXPERTS) * MOE_BLOCK
        xs = _sc_scatter_rows(x2.reshape(N, D), dest.reshape(N, TOP_K).T, n_slots)
        ys = _experts(block_expert, n_used, next_expert, xs, exp_w_gu, exp_b_gu, exp_w_down, exp_b_down, l)
        dest_km = dest.reshape(N // CMB_TILE, CMB_TILE, TOP_K).transpose(0, 2, 1).reshape(-1)
        yg = _sc_gather_rows(ys, dest_km)
        x = _combine(x2.reshape(N, D), yg, gate, p, l).reshape(B, S, D)
    return x
```

```python
import functools

import jax
import jax.numpy as jnp
from jax import lax
from jax.experimental import pallas as pl
from jax.experimental.pallas import tpu as pltpu
from jax.experimental.pallas import tpu_sc as plsc

F32 = jnp.float32
BF16 = jnp.bfloat16

D_MODEL = 1024
DEPTH = 2
GLA_HEADS = 4
GLA_KEY = 256
GLA_VAL = 512
GLA_DK = 64
GLA_DV = 128
GLA_RANK = 16
GLA_TAU = 16.0
GLA_CHUNK = 64
SGU_GROUPS = 4
SGU_WIDTH = 256
SGU_GD = 64
SGU_CHUNK = 128
POOL_WINDOWS = (2, 4, 8, 16)
POOL_WIDTH = 256
POOL_GD = 64
POOL_CARRY = 32
MEM_LEN = 256
XA_HEADS = 4
XA_DH = 256
N_EXPERTS = 32
TOP_K = 4
EXPERT_FF = 1024
SWIGLU_LIMIT = 7.0
SWIGLU_ALPHA = 1.702
DEEPNORM_ALPHA = (2 * DEPTH) ** 0.25
LN_EPS = 1e-5
LANES = 128
VMEM_LIMIT = 56 * 1024 * 1024

MIX_TILE = 256
XA_TILE = 512
MOE_BLOCK = 256
CMB_TILE = 256
SC_GATHER_WINDOW = 32

O_QKVR = 0
O_GLOW = 2 * GLA_KEY + 2 * GLA_VAL
O_UV = O_GLOW + LANES
O_XC = O_UV + 2 * SGU_WIDTH
O_GATE = O_XC + POOL_WIDTH
N_PACK = O_GATE + 3 * D_MODEL


def _dot(a, b):
    return jnp.dot(a, b, preferred_element_type=F32)


def _dot_t0(a, b):
    return lax.dot_general(a, b, (((0,), (0,)), ((), ())), preferred_element_type=F32)


def _dot_t1(a, b):
    return lax.dot_general(a, b, (((1,), (1,)), ((), ())), preferred_element_type=F32)


def _split_bf16(x):
    hi = x.astype(BF16)
    lo = (x - hi.astype(F32)).astype(BF16)
    return hi, lo


def _layer_norm(x, g, b):
    mu = jnp.mean(x, axis=-1, keepdims=True)
    xc = x - mu
    var = jnp.mean(xc * xc, axis=-1, keepdims=True)
    return xc * lax.rsqrt(var + LN_EPS) * g + b


def _sigmoid(x):
    return 1.0 / (1.0 + jnp.exp(-x))


def _layer_spec(arr, l):
    nd = arr.ndim - 1
    return pl.BlockSpec((None,) + arr.shape[1:], lambda *_: (l,) + (0,) * nd, pipeline_mode=pl.Buffered(1))


def _mixer_kernel(x_ref, wcat_ref, bcat_ref, wg2_ref, bg_ref, gnorm_ref,
                  slng_ref, slnb_ref, wtril_ref, sbias_ref, poolw_ref, pscale_ref,
                  wup_ref, wo_ref, ln1g_ref, ln1b_ref,
                  out_ref,
                  state_ref, qkvr_ref, ya_ref, vln_ref, e_ref, s2_ref, s4_ref, s8_ref):
    T = MIX_TILE
    D = D_MODEL
    j = pl.program_id(1)
    x = x_ref[0]
    xb = x.astype(BF16)

    def proj(lo, hi):
        return _dot(xb, wcat_ref[:, lo:hi]) + bcat_ref[:, lo:hi]

    @pl.when(j == 0)
    def _():
        state_ref[...] = jnp.zeros_like(state_ref)
        e_ref[0:POOL_CARRY, :] = jnp.zeros((POOL_CARRY, POOL_WIDTH), F32)

    @pl.when(j > 0)
    def _():
        e_ref[0:POOL_CARRY, :] = e_ref[T:T + POOL_CARRY, :]

    qkvr_ref[...] = proj(O_QKVR, O_GLOW)
    glow = proj(O_GLOW, O_UV)
    z = _dot(glow.astype(BF16), wg2_ref[...]) + bg_ref[...]
    la = (jnp.minimum(z, 0.0) - jnp.log1p(jnp.exp(-jnp.abs(z)))) * (1.0 / GLA_TAU)
    la_hi, la_lo = _split_bf16(la)

    C = GLA_CHUNK
    NC = T // C
    CSH = C.bit_length() - 1
    row = lax.broadcasted_iota(jnp.int32, (T, T), 0)
    col = lax.broadcasted_iota(jnp.int32, (T, T), 1)
    same_chunk = (row >> CSH) == (col >> CSH)
    causal = same_chunk & (row >= col)
    causal_bf = jnp.where(causal, 1.0, 0.0).astype(BF16)
    chunk_bf = jnp.where(same_chunk, 1.0, 0.0).astype(BF16)
    b = _dot(causal_bf, la_hi) + _dot(causal_bf, la_lo)
    b_end = _dot(chunk_bf, la_hi) + _dot(chunk_bf, la_lo)
    q = qkvr_ref[:, 0:GLA_KEY]
    k = qkvr_ref[:, GLA_KEY:2 * GLA_KEY]
    v = qkvr_ref[:, 2 * GLA_KEY:2 * GLA_KEY + GLA_VAL].astype(BF16)
    q_dec = q * (GLA_DK ** -0.5) * jnp.exp(b)
    k_dec = (k * jnp.exp(-b)).astype(BF16)
    k_tail = (k * jnp.exp(b_end - b)).astype(BF16)
    q_dec_bf = q_dec.astype(BF16)
    lane = lax.broadcasted_iota(jnp.int32, (T, GLA_KEY), 1)
    o_heads = []
    for h in range(GLA_HEADS):
        q_h = jnp.where((lane >= h * GLA_DK) & (lane < (h + 1) * GLA_DK), q_dec, 0.0).astype(BF16)
        scores = jnp.where(causal, _dot_t1(q_h, k_dec), 0.0).astype(BF16)
        o_heads.append(_dot(scores, v[:, h * GLA_DV:(h + 1) * GLA_DV]))
    o_intra = jnp.concatenate(o_heads, axis=1)
    ind = jnp.where((lax.broadcasted_iota(jnp.int32, (T, NC * LANES), 0) >> CSH)
                    == (lax.broadcasted_iota(jnp.int32, (T, NC * LANES), 1) >> 7), 1.0, 0.0).astype(BF16)
    dec_all = jnp.exp(_dot_t0(la_hi, ind) + _dot_t0(la_lo, ind))
    o_inter = []
    for c in range(NC):
        rows = slice(c * C, (c + 1) * C)
        o_inter.append(_dot(q_dec_bf[rows], state_ref[...].astype(BF16)))
        kv = _dot_t0(k_tail[rows], v[rows])
        for h in range(GLA_HEADS):
            rs = slice(h * GLA_DK, (h + 1) * GLA_DK)
            cs = slice(h * GLA_DV, (h + 1) * GLA_DV)
            state_ref[rs, cs] = dec_all[rs, c * LANES:(c + 1) * LANES] * state_ref[rs, cs] + kv[rs, cs]
    o = o_intra + jnp.concatenate(o_inter, axis=0)
    for h in range(GLA_HEADS):
        cs = slice(h * GLA_DV, (h + 1) * GLA_DV)
        o_h = o[:, cs]
        ms = jnp.mean(o_h * o_h, axis=-1, keepdims=True)
        o_h = o_h * lax.rsqrt(ms + LN_EPS) * gnorm_ref[:, cs]
        r_h = qkvr_ref[:, 2 * GLA_KEY + GLA_VAL + h * GLA_DV:2 * GLA_KEY + GLA_VAL + (h + 1) * GLA_DV]
        ya_ref[:, cs] = (o_h * (r_h * _sigmoid(r_h))).astype(BF16)

    uv = proj(O_UV, O_XC)
    zg = 0.5 * uv * (1.0 + lax.erf(uv * (2.0 ** -0.5)))
    u = zg[:, :SGU_WIDTH]
    vln_ref[...] = _layer_norm(zg[:, SGU_WIDTH:], slng_ref[...], slnb_ref[...])
    lane_s = lax.broadcasted_iota(jnp.int32, (SGU_CHUNK, SGU_WIDTH), 1)
    s_parts = []
    for n in range(T // SGU_CHUNK):
        vc = vln_ref[n * SGU_CHUNK:(n + 1) * SGU_CHUNK, :]
        s = sbias_ref[...]
        for g in range(SGU_GROUPS):
            vg = jnp.where((lane_s >= g * SGU_GD) & (lane_s < (g + 1) * SGU_GD), vc, 0.0).astype(BF16)
            s = s + _dot(wtril_ref[g], vg)
        s_parts.append(s)
    y_b = (u * jnp.concatenate(s_parts, axis=0)).astype(BF16)

    P = POOL_CARRY
    xc = proj(O_XC, O_GATE)
    e_ref[P:P + T, :] = xc
    s2_ref[8:P + T, :] = e_ref[8:P + T, :] + e_ref[7:P + T - 1, :]
    s4_ref[16:P + T, :] = s2_ref[16:P + T, :] + s2_ref[14:P + T - 2, :]
    s8_ref[24:P + T, :] = s4_ref[24:P + T, :] + s4_ref[20:P + T - 4, :]
    s16 = s8_ref[P:P + T, :] + s8_ref[P - 8:P + T - 8, :]
    lane_p = lax.broadcasted_iota(jnp.int32, (T, POOL_WIDTH), 1)
    tpos = lax.broadcasted_iota(jnp.int32, (T, POOL_WIDTH), 0) + (j * T + 1)
    grp = lane_p >> 6
    win = jnp.where(grp == 0, POOL_WINDOWS[0], jnp.where(grp == 1, POOL_WINDOWS[1],
                    jnp.where(grp == 2, POOL_WINDOWS[2], POOL_WINDOWS[3])))
    wsum = jnp.where(grp == 0, s2_ref[P:P + T, :], jnp.where(grp == 1, s4_ref[P:P + T, :],
                     jnp.where(grp == 2, s8_ref[P:P + T, :], s16)))
    count = jnp.minimum(tpos, win).astype(F32)
    pooled = wsum / count - xc
    y_c = (_dot(pooled.astype(BF16), poolw_ref[...]) * pscale_ref[...]).astype(BF16)

    ra, rb = GLA_VAL, GLA_VAL + SGU_WIDTH
    merged = _sigmoid(proj(O_GATE, O_GATE + D)) * _dot(ya_ref[...], wup_ref[0:ra, :])
    merged += _sigmoid(proj(O_GATE + D, O_GATE + 2 * D)) * _dot(y_b, wup_ref[ra:rb, :])
    merged += _sigmoid(proj(O_GATE + 2 * D, O_GATE + 3 * D)) * _dot(y_c, wup_ref[rb:, :])
    h = _dot(merged.astype(BF16), wo_ref[...])
    out_ref[0] = _layer_norm(DEEPNORM_ALPHA * x + h, ln1g_ref[...], ln1b_ref[...])


_MIXER_WEIGHTS = ('wcat', 'bcat', 'wg2', 'bg', 'gnorm', 'slng', 'slnb', 'wtril', 'sbias', 'poolw', 'pscale',
                  'wup', 'wo', 'ln1g', 'ln1b')


def _mixer(x, p, l):
    B, S, D = x.shape
    T = MIX_TILE
    weights = [p[n] for n in _MIXER_WEIGHTS]
    return pl.pallas_call(
        _mixer_kernel,
        out_shape=jax.ShapeDtypeStruct((B, S, D), F32),
        grid=(B, S // T),
        in_specs=[pl.BlockSpec((1, T, D), lambda b, j: (b, j, 0))] + [_layer_spec(w, l) for w in weights],
        out_specs=pl.BlockSpec((1, T, D), lambda b, j: (b, j, 0)),
        scratch_shapes=[
            pltpu.VMEM((GLA_KEY, GLA_VAL), F32),
            pltpu.VMEM((T, 2 * GLA_KEY + 2 * GLA_VAL), F32),
            pltpu.VMEM((T, GLA_VAL), BF16),
            pltpu.VMEM((T, SGU_WIDTH), F32),
            pltpu.VMEM((T + POOL_CARRY, POOL_WIDTH), F32),
            pltpu.VMEM((T + POOL_CARRY, POOL_WIDTH), F32),
            pltpu.VMEM((T + POOL_CARRY, POOL_WIDTH), F32),
            pltpu.VMEM((T + POOL_CARRY, POOL_WIDTH), F32),
        ],
        compiler_params=pltpu.CompilerParams(dimension_semantics=("arbitrary", "arbitrary"),
                                             vmem_limit_bytes=VMEM_LIMIT),
        name="mixer",
    )(x, *weights)


def _memkv_kernel(memt_ref, mem_ref, wkt_ref, wv_ref, kt_ref, v_ref):
    kt_ref[0] = _dot(wkt_ref[...], memt_ref[0].astype(BF16)).astype(BF16)
    v_ref[0] = _dot(mem_ref[0].astype(BF16), wv_ref[...]).astype(BF16)


def _memkv(mem, memt, p, l):
    B, M, D = mem.shape
    return pl.pallas_call(
        _memkv_kernel,
        out_shape=(jax.ShapeDtypeStruct((B, D, M), BF16), jax.ShapeDtypeStruct((B, M, D), BF16)),
        grid=(B,),
        in_specs=[pl.BlockSpec((1, D, M), lambda b: (b, 0, 0)), pl.BlockSpec((1, M, D), lambda b: (b, 0, 0)),
                  _layer_spec(p['wkt'], l), _layer_spec(p['wv'], l)],
        out_specs=(pl.BlockSpec((1, D, M), lambda b: (b, 0, 0)), pl.BlockSpec((1, M, D), lambda b: (b, 0, 0))),
        compiler_params=pltpu.CompilerParams(dimension_semantics=("arbitrary",), vmem_limit_bytes=VMEM_LIMIT),
        name="memkv",
    )(memt, mem, p['wkt'], p['wv'])


def _xattn_kernel(x_ref, kt_ref, v_ref, wq_ref, wo_ref, ln2g_ref, ln2b_ref, rwhi_ref, rwlo_ref, rb_ref,
                  x2_ref, logit_ref):
    x = x_ref[0]
    q = (_dot(x.astype(BF16), wq_ref[...]) * (XA_DH ** -0.5)).astype(BF16)
    h = jnp.zeros_like(x)
    for hd in range(XA_HEADS):
        cs = slice(hd * XA_DH, (hd + 1) * XA_DH)
        s = _dot(q[:, cs], kt_ref[0, cs, :])
        e = jnp.exp(s - jnp.max(s, axis=-1, keepdims=True))
        o = _dot(e.astype(BF16), v_ref[0, :, cs]) / jnp.sum(e, axis=-1, keepdims=True)
        h = h + _dot(o.astype(BF16), wo_ref[cs, :])
    x2 = _layer_norm(DEEPNORM_ALPHA * x + h, ln2g_ref[...], ln2b_ref[...])
    x2_ref[0] = x2
    hi, lo = _split_bf16(x2)
    logit_ref[0] = _dot(hi, rwhi_ref[...]) + (_dot(hi, rwlo_ref[...]) + _dot(lo, rwhi_ref[...])) + rb_ref[...]


def _xattn(x, kt, v, p, l):
    B, S, D = x.shape
    T = XA_TILE
    M = MEM_LEN
    weights = [p[n] for n in ('wq', 'wxo', 'ln2g', 'ln2b', 'rwhi', 'rwlo', 'rb')]
    return pl.pallas_call(
        _xattn_kernel,
        out_shape=(jax.ShapeDtypeStruct((B, S, D), F32), jax.ShapeDtypeStruct((B, S, LANES), F32)),
        grid=(B, S // T),
        in_specs=[pl.BlockSpec((1, T, D), lambda b, j: (b, j, 0)),
                  pl.BlockSpec((1, D, M), lambda b, j: (b, 0, 0)),
                  pl.BlockSpec((1, M, D), lambda b, j: (b, 0, 0))] + [_layer_spec(w, l) for w in weights],
        out_specs=(pl.BlockSpec((1, T, D), lambda b, j: (b, j, 0)),
                   pl.BlockSpec((1, T, LANES), lambda b, j: (b, j, 0))),
        compiler_params=pltpu.CompilerParams(dimension_semantics=("arbitrary", "arbitrary"),
                                             vmem_limit_bytes=VMEM_LIMIT),
        name="xattn",
    )(x, kt, v, *weights)


def _expert_kernel(layer, be_ref, nb_ref, nxt_ref, xs_ref, wgu_hbm, bgu_ref, wd_hbm, bd_ref, out_ref,
                   wgu_st, wd_st, wgu_bf, wd_bf, sem):
    i = pl.program_id(0)
    F = EXPERT_FF

    def weight_copies(e):
        return (pltpu.make_async_copy(wgu_hbm.at[layer, e], wgu_st, sem.at[0]),
                pltpu.make_async_copy(wd_hbm.at[layer, e], wd_st, sem.at[1]))

    @pl.when(i < nb_ref[0])
    def _():
        e = be_ref[i]
        prev = be_ref[jnp.maximum(i - 1, 0)]

        @pl.when(i == 0)
        def _():
            for cp in weight_copies(e):
                cp.start()

        @pl.when((i == 0) | (e != prev))
        def _():
            for cp in weight_copies(e):
                cp.wait()
            wgu_bf[...] = wgu_st[...].astype(BF16)
            wd_bf[...] = wd_st[...].astype(BF16)
            nxt = nxt_ref[e]

            @pl.when(nxt != e)
            def _():
                for cp in weight_copies(nxt):
                    cp.start()

        hh = _dot(xs_ref[...].astype(BF16), wgu_bf[...]) + bgu_ref[...]
        h_glu = jnp.minimum(hh[:, :F], SWIGLU_LIMIT)
        h_lin = jnp.clip(hh[:, F:], -SWIGLU_LIMIT, SWIGLU_LIMIT)
        a = h_glu * _sigmoid(SWIGLU_ALPHA * h_glu) * (h_lin + 1.0)
        out_ref[...] = _dot(a.astype(BF16), wd_bf[...]) + bd_ref[...]


def _experts(block_expert, n_used, next_expert, xs, w_gu, b_gu, w_down, b_down, l):
    P, D = xs.shape
    NB = P // MOE_BLOCK
    F2 = 2 * EXPERT_FF

    def row_map(i, be, nb, nxt):
        return (jnp.minimum(i, nb[0] - 1), 0)

    def exp_map(i, be, nb, nxt):
        return (l, be[jnp.minimum(i, nb[0] - 1)], 0, 0)

    grid_spec = pltpu.PrefetchScalarGridSpec(
        num_scalar_prefetch=3,
        grid=(NB,),
        in_specs=[pl.BlockSpec((MOE_BLOCK, D), row_map),
                  pl.BlockSpec(memory_space=pl.ANY),
                  pl.BlockSpec((None, None, 1, F2), exp_map),
                  pl.BlockSpec(memory_space=pl.ANY),
                  pl.BlockSpec((None, None, 1, D), exp_map)],
        out_specs=pl.BlockSpec((MOE_BLOCK, D), row_map),
        scratch_shapes=[pltpu.VMEM((D, F2), F32), pltpu.VMEM((EXPERT_FF, D), F32),
                        pltpu.VMEM((D, F2), BF16), pltpu.VMEM((EXPERT_FF, D), BF16),
                        pltpu.SemaphoreType.DMA((2,))],
    )
    return pl.pallas_call(
        functools.partial(_expert_kernel, l),
        out_shape=jax.ShapeDtypeStruct((P, D), F32),
        grid_spec=grid_spec,
        compiler_params=pltpu.CompilerParams(dimension_semantics=("arbitrary",), vmem_limit_bytes=VMEM_LIMIT),
        name="experts",
    )(block_expert, n_used, next_expert, xs, w_gu, b_gu.reshape(DEPTH, N_EXPERTS, 1, F2), w_down,
      b_down.reshape(DEPTH, N_EXPERTS, 1, D))


def _sc_gather_rows(x, idx):
    M = idx.shape[0]
    D = x.shape[1]
    W = SC_GATHER_WINDOW
    mesh = plsc.VectorSubcoreMesh(core_axis_name="core", subcore_axis_name="subcore")
    n_workers = mesh.num_cores * mesh.num_subcores
    rows_per = M // n_workers
    assert rows_per * n_workers == M and rows_per % W == 0

    @pl.kernel(out_type=jax.ShapeDtypeStruct((M, D), x.dtype), mesh=mesh, name="sc_gather_rows",
               scratch_types=[pltpu.VMEM((rows_per,), jnp.int32), pltpu.VMEM((W, D), x.dtype)])
    def gather_kernel(x_hbm, i_hbm, o_hbm, idx_vmem, buf):
        wid = lax.axis_index("core") * mesh.num_subcores + lax.axis_index("subcore")
        base = wid * rows_per
        pltpu.sync_copy(i_hbm.at[pl.ds(base, rows_per)], idx_vmem)

        @pl.loop(0, rows_per // W)
        def _(j):
            pltpu.sync_copy(x_hbm.at[idx_vmem.at[pl.ds(j * W, W)]], buf)
            pltpu.sync_copy(buf, o_hbm.at[pl.ds(base + j * W, W)])

    return gather_kernel(x, idx)


def _sc_scatter_rows(x, idx, n_out):
    K, N = idx.shape
    D = x.shape[1]
    W = SC_GATHER_WINDOW
    mesh = plsc.VectorSubcoreMesh(core_axis_name="core", subcore_axis_name="subcore")
    n_workers = mesh.num_cores * mesh.num_subcores
    rows_per = N // n_workers
    assert rows_per * n_workers == N and rows_per % W == 0

    @pl.kernel(out_type=jax.ShapeDtypeStruct((n_out, D), x.dtype), mesh=mesh, name="sc_scatter_rows",
               scratch_types=[pltpu.VMEM((K * rows_per,), jnp.int32), pltpu.VMEM((W, D), x.dtype)])
    def scatter_kernel(x_hbm, i_hbm, o_hbm, idx_vmem, buf):
        wid = lax.axis_index("core") * mesh.num_subcores + lax.axis_index("subcore")
        base = wid * rows_per
        for k in range(K):
            pltpu.sync_copy(i_hbm.at[pl.ds(k * N + base, rows_per)], idx_vmem.at[pl.ds(k * rows_per, rows_per)])

        @pl.loop(0, rows_per // W)
        def _(j):
            pltpu.sync_copy(x_hbm.at[pl.ds(base + j * W, W)], buf)
            for k in range(K):
                pltpu.sync_copy(buf, o_hbm.at[idx_vmem.at[pl.ds(k * rows_per + j * W, W)]])

    return scatter_kernel(x, idx.reshape(K * N))


def _combine_kernel(x_ref, yg_ref, gate_ref, ln3g_ref, ln3b_ref, out_ref):
    x = x_ref[...]
    g = gate_ref[...]
    y = jnp.zeros_like(x)
    for k in range(TOP_K):
        y = y + g[:, k:k + 1] * yg_ref[k * CMB_TILE:(k + 1) * CMB_TILE, :]
    out_ref[...] = _layer_norm(DEEPNORM_ALPHA * x + y, ln3g_ref[...], ln3b_ref[...])


def _combine(x2, yg, gate, p, l):
    N, D = x2.shape
    T = CMB_TILE
    return pl.pallas_call(
        _combine_kernel,
        out_shape=jax.ShapeDtypeStruct((N, D), F32),
        grid=(N // T,),
        in_specs=[pl.BlockSpec((T, D), lambda i: (i, 0)), pl.BlockSpec((TOP_K * T, D), lambda i: (i, 0)),
                  pl.BlockSpec((T, TOP_K), lambda i: (i, 0)), _layer_spec(p['ln3g'], l), _layer_spec(p['ln3b'], l)],
        out_specs=pl.BlockSpec((T, D), lambda i: (i, 0)),
        compiler_params=pltpu.CompilerParams(dimension_semantics=("arbitrary",), vmem_limit_bytes=VMEM_LIMIT),
        name="combine",
    )(x2, yg, gate, p['ln3g'], p['ln3b'])


def _prep(w_in, b_in, gla_wg2, gla_bg, gla_norm_g, sgu_ln_g, sgu_ln_b, sgu_ws, sgu_bs, pool_w, pool_scale,
          w_up_a, w_up_b, w_up_c, w_o, ln1_g, ln1_b, xa_wq, xa_wk, xa_wv, xa_wo, ln2_g, ln2_b,
          router_w, router_b, ln3_g, ln3_b):
    L = w_in.shape[0]
    o_glow = O_GLOW
    o_uv = o_glow + GLA_RANK
    row = lambda a: a.reshape(L, 1, -1).astype(F32)
    pad_last = lambda a, n: jnp.pad(a, [(0, 0)] * (a.ndim - 1) + [(0, n - a.shape[-1])])
    p = {}
    p['wcat'] = jnp.concatenate([w_in[..., :o_glow], pad_last(w_in[..., o_glow:o_uv], LANES), w_in[..., o_uv:]],
                                axis=-1).astype(BF16)
    p['bcat'] = row(jnp.concatenate([b_in[..., :o_glow], pad_last(b_in[..., o_glow:o_uv], LANES), b_in[..., o_uv:]],
                                    axis=-1))
    p['wg2'] = jnp.pad(gla_wg2, ((0, 0), (0, LANES - GLA_RANK), (0, 0))).astype(BF16)
    p['bg'] = row(gla_bg)
    p['gnorm'] = row(gla_norm_g)
    p['slng'] = row(sgu_ln_g)
    p['slnb'] = row(sgu_ln_b)
    p['wtril'] = jnp.tril(sgu_ws).astype(BF16)
    p['sbias'] = jnp.repeat(jnp.swapaxes(sgu_bs, 1, 2), SGU_GD, axis=2).astype(F32)
    G = len(POOL_WINDOWS)
    eye = jnp.eye(G, dtype=F32)
    p['poolw'] = jnp.einsum('lgcd,gh->lgchd', pool_w, eye).reshape(L, POOL_WIDTH, POOL_WIDTH).astype(BF16)
    p['pscale'] = row(pool_scale)
    p['wup'] = jnp.concatenate([w_up_a, w_up_b, w_up_c], axis=1).astype(BF16)
    p['wo'] = w_o.astype(BF16)
    p['ln1g'], p['ln1b'] = row(ln1_g), row(ln1_b)
    p['wq'] = xa_wq.astype(BF16)
    p['wkt'] = jnp.swapaxes(xa_wk, 1, 2).astype(BF16)
    p['wv'] = xa_wv.astype(BF16)
    p['wxo'] = xa_wo.astype(BF16)
    p['ln2g'], p['ln2b'] = row(ln2_g), row(ln2_b)
    rw = pad_last(router_w, LANES)
    p['rwhi'] = rw.astype(BF16)
    p['rwlo'] = (rw - p['rwhi'].astype(F32)).astype(BF16)
    p['rb'] = row(pad_last(router_b, LANES))
    p['ln3g'], p['ln3b'] = row(ln3_g), row(ln3_b)
    return p


def _route(logits):
    N = logits.shape[0]
    A = N * TOP_K
    top_vals, top_idx = lax.top_k(logits, TOP_K)
    gate = jax.nn.softmax(top_vals, axis=-1)
    flat_e = top_idx.reshape(-1)
    onehot = (flat_e[:, None] == jnp.arange(N_EXPERTS, dtype=jnp.int32)[None, :]).astype(jnp.int32)
    csum = jnp.cumsum(onehot, axis=0)
    counts = csum[-1]
    padded = ((counts + MOE_BLOCK - 1) // MOE_BLOCK) * MOE_BLOCK
    pad_end = jnp.cumsum(padded)
    pad_start = pad_end - padded
    dest = jnp.sum(onehot * (csum - 1 + pad_start[None, :]), axis=1)
    n_blocks = A // MOE_BLOCK + N_EXPERTS
    block_start = jnp.arange(n_blocks, dtype=jnp.int32) * MOE_BLOCK
    block_expert = jnp.minimum(jnp.sum((pad_end[None, :] <= block_start[:, None]).astype(jnp.int32), axis=1),
                               N_EXPERTS - 1)
    n_used = (pad_end[-1] // MOE_BLOCK).astype(jnp.int32).reshape(1)
    ids = jnp.arange(N_EXPERTS, dtype=jnp.int32)
    later = jnp.where((ids[None, :] > ids[:, None]) & (counts[None, :] > 0), ids[None, :], N_EXPERTS)
    first_later = jnp.min(later, axis=1)
    next_expert = jnp.where(first_later < N_EXPERTS, first_later, ids).astype(jnp.int32)
    return gate, dest, block_expert, n_used, next_expert


def kernel(x, mem, w_in, b_in, gla_wg2, gla_bg, gla_norm_g, sgu_ln_g, sgu_ln_b, sgu_ws, sgu_bs, pool_w, pool_scale, w_up_a, w_up_b, w_up_c, w_o, ln1_g, ln1_b, xa_wq, xa_wk, xa_wv, xa_wo, ln2_g, ln2_b, router_w, router_b, exp_w_gu, exp_b_gu, exp_w_down, exp_b_down, ln3_g, ln3_b):
    B, S, D = x.shape
    N = B * S
    p = _prep(w_in, b_in, gla_wg2, gla_bg, gla_norm_g, sgu_ln_g, sgu_ln_b, sgu_ws, sgu_bs, pool_w, pool_scale,
              w_up_a, w_up_b, w_up_c, w_o, ln1_g, ln1_b, xa_wq, xa_wk, xa_wv, xa_wo, ln2_g, ln2_b,
              router_w, router_b, ln3_g, ln3_b)
    memt = jnp.swapaxes(mem, 1, 2)
    for l in range(DEPTH):
        x1 = _mixer(x, p, l)
        kt, v = _memkv(mem, memt, p, l)
        x2, logits = _xattn(x1, kt, v, p, l)
        gate, dest, block_expert, n_used, next_expert = _route(logits.reshape(N, LANES)[:, :N_EXPERTS])
        n_slots = (N * TOP_K // MOE_BLOCK + N_EXPERTS) * MOE_BLOCK
        xs = _sc_scatter_rows(x2.reshape(N, D), dest.reshape(N, TOP_K).T, n_slots)
        ys = _experts(block_expert, n_used, next_expert, xs, exp_w_gu, exp_b_gu, exp_w_down, exp_b_down, l)
        dest_km = dest.reshape(N // CMB_TILE, CMB_TILE, TOP_K).transpose(0, 2, 1).reshape(-1)
        yg = _sc_gather_rows(ys, dest_km)
        x = _combine(x2.reshape(N, D), yg, gate, p, l).reshape(B, S, D)
    return x
```

```python
import functools

import jax
import jax.numpy as jnp
from jax import lax
from jax.experimental import pallas as pl
from jax.experimental.pallas import tpu as pltpu
from jax.experimental.pallas import tpu_sc as plsc

F32 = jnp.float32
BF16 = jnp.bfloat16

D_MODEL = 1024
DEPTH = 2
GLA_HEADS = 4
GLA_KEY = 256
GLA_VAL = 512
GLA_DK = 64
GLA_DV = 128
GLA_RANK = 16
GLA_TAU = 16.0
GLA_CHUNK = 64
SGU_GROUPS = 4
SGU_WIDTH = 256
SGU_GD = 64
SGU_CHUNK = 128
POOL_WINDOWS = (2, 4, 8, 16)
POOL_WIDTH = 256
POOL_GD = 64
POOL_CARRY = 32
MEM_LEN = 256
XA_HEADS = 4
XA_DH = 256
N_EXPERTS = 32
TOP_K = 4
EXPERT_FF = 1024
SWIGLU_LIMIT = 7.0
SWIGLU_ALPHA = 1.702
DEEPNORM_ALPHA = (2 * DEPTH) ** 0.25
LN_EPS = 1e-5
LANES = 128
VMEM_LIMIT = 56 * 1024 * 1024

MIX_TILE = 256
XA_TILE = 512
MOE_BLOCK = 256
CMB_TILE = 256
SC_GATHER_WINDOW = 32

O_QKVR = 0
O_GLOW = 2 * GLA_KEY + 2 * GLA_VAL
O_UV = O_GLOW + LANES
O_XC = O_UV + 2 * SGU_WIDTH
O_GATE = O_XC + POOL_WIDTH
N_PACK = O_GATE + 3 * D_MODEL


def _dot(a, b):
    return jnp.dot(a, b, preferred_element_type=F32)


def _dot_t0(a, b):
    return lax.dot_general(a, b, (((0,), (0,)), ((), ())), preferred_element_type=F32)


def _dot_t1(a, b):
    return lax.dot_general(a, b, (((1,), (1,)), ((), ())), preferred_element_type=F32)


def _split_bf16(x):
    hi = x.astype(BF16)
    lo = (x - hi.astype(F32)).astype(BF16)
    return hi, lo


def _layer_norm(x, g, b):
    mu = jnp.mean(x, axis=-1, keepdims=True)
    xc = x - mu
    var = jnp.mean(xc * xc, axis=-1, keepdims=True)
    return xc * lax.rsqrt(var + LN_EPS) * g + b


def _sigmoid(x):
    return 1.0 / (1.0 + jnp.exp(-x))


def _pack_bf16_pairs(x):
    H = x.shape[1] // 2
    bits = lax.bitcast_convert_type(x.astype(BF16).astype(F32), jnp.uint32)
    return (bits[:, :H] >> 16) | (bits[:, H:] & jnp.uint32(0xFFFF0000))


def _unpack_bf16_pairs(w):
    lo = lax.bitcast_convert_type(w << 16, F32)
    hi = lax.bitcast_convert_type(w & jnp.uint32(0xFFFF0000), F32)
    return jnp.concatenate([lo, hi], axis=1)


def _layer_spec(arr, l):
    nd = arr.ndim - 1
    return pl.BlockSpec((None,) + arr.shape[1:], lambda *_: (l,) + (0,) * nd, pipeline_mode=pl.Buffered(1))


def _mixer_kernel(x_ref, wcat_ref, bcat_ref, wg2_ref, bg_ref, gnorm_ref,
                  slng_ref, slnb_ref, wtril_ref, sbias_ref, poolw_ref, pscale_ref,
                  wup_ref, wo_ref, ln1g_ref, ln1b_ref,
                  out_ref,
                  state_ref, qkvr_ref, ya_ref, vln_ref, e_ref, s2_ref, s4_ref, s8_ref):
    T = MIX_TILE
    D = D_MODEL
    j = pl.program_id(1)
    x = x_ref[0]
    xb = x.astype(BF16)

    def proj(lo, hi):
        return _dot(xb, wcat_ref[:, lo:hi]) + bcat_ref[:, lo:hi]

    @pl.when(j == 0)
    def _():
        state_ref[...] = jnp.zeros_like(state_ref)
        e_ref[0:POOL_CARRY, :] = jnp.zeros((POOL_CARRY, POOL_WIDTH), F32)

    @pl.when(j > 0)
    def _():
        e_ref[0:POOL_CARRY, :] = e_ref[T:T + POOL_CARRY, :]

    qkvr_ref[...] = proj(O_QKVR, O_GLOW)
    glow = proj(O_GLOW, O_UV)
    z = _dot(glow.astype(BF16), wg2_ref[...]) + bg_ref[...]
    la = (jnp.minimum(z, 0.0) - jnp.log1p(jnp.exp(-jnp.abs(z)))) * (1.0 / GLA_TAU)
    la_hi, la_lo = _split_bf16(la)

    C = GLA_CHUNK
    NC = T // C
    CSH = C.bit_length() - 1
    row = lax.broadcasted_iota(jnp.int32, (T, T), 0)
    col = lax.broadcasted_iota(jnp.int32, (T, T), 1)
    same_chunk = (row >> CSH) == (col >> CSH)
    causal = same_chunk & (row >= col)
    causal_bf = jnp.where(causal, 1.0, 0.0).astype(BF16)
    chunk_bf = jnp.where(same_chunk, 1.0, 0.0).astype(BF16)
    b = _dot(causal_bf, la_hi) + _dot(causal_bf, la_lo)
    b_end = _dot(chunk_bf, la_hi) + _dot(chunk_bf, la_lo)
    q = qkvr_ref[:, 0:GLA_KEY]
    k = qkvr_ref[:, GLA_KEY:2 * GLA_KEY]
    v = qkvr_ref[:, 2 * GLA_KEY:2 * GLA_KEY + GLA_VAL].astype(BF16)
    q_dec = q * (GLA_DK ** -0.5) * jnp.exp(b)
    k_dec = (k * jnp.exp(-b)).astype(BF16)
    k_tail = (k * jnp.exp(b_end - b)).astype(BF16)
    q_dec_bf = q_dec.astype(BF16)
    lane = lax.broadcasted_iota(jnp.int32, (T, GLA_KEY), 1)
    o_heads = []
    for h in range(GLA_HEADS):
        q_h = jnp.where((lane >= h * GLA_DK) & (lane < (h + 1) * GLA_DK), q_dec, 0.0).astype(BF16)
        scores = jnp.where(causal, _dot_t1(q_h, k_dec), 0.0).astype(BF16)
        o_heads.append(_dot(scores, v[:, h * GLA_DV:(h + 1) * GLA_DV]))
    o_intra = jnp.concatenate(o_heads, axis=1)
    ind = jnp.where((lax.broadcasted_iota(jnp.int32, (T, NC * LANES), 0) >> CSH)
                    == (lax.broadcasted_iota(jnp.int32, (T, NC * LANES), 1) >> 7), 1.0, 0.0).astype(BF16)
    dec_all = jnp.exp(_dot_t0(la_hi, ind) + _dot_t0(la_lo, ind))
    o_inter = []
    for c in range(NC):
        rows = slice(c * C, (c + 1) * C)
        o_inter.append(_dot(q_dec_bf[rows], state_ref[...].astype(BF16)))
        kv = _dot_t0(k_tail[rows], v[rows])
        for h in range(GLA_HEADS):
            rs = slice(h * GLA_DK, (h + 1) * GLA_DK)
            cs = slice(h * GLA_DV, (h + 1) * GLA_DV)
            state_ref[rs, cs] = dec_all[rs, c * LANES:(c + 1) * LANES] * state_ref[rs, cs] + kv[rs, cs]
    o = o_intra + jnp.concatenate(o_inter, axis=0)
    for h in range(GLA_HEADS):
        cs = slice(h * GLA_DV, (h + 1) * GLA_DV)
        o_h = o[:, cs]
        ms = jnp.mean(o_h * o_h, axis=-1, keepdims=True)
        o_h = o_h * lax.rsqrt(ms + LN_EPS) * gnorm_ref[:, cs]
        r_h = qkvr_ref[:, 2 * GLA_KEY + GLA_VAL + h * GLA_DV:2 * GLA_KEY + GLA_VAL + (h + 1) * GLA_DV]
        ya_ref[:, cs] = (o_h * (r_h * _sigmoid(r_h))).astype(BF16)

    uv = proj(O_UV, O_XC)
    zg = 0.5 * uv * (1.0 + lax.erf(uv * (2.0 ** -0.5)))
    u = zg[:, :SGU_WIDTH]
    vln_ref[...] = _layer_norm(zg[:, SGU_WIDTH:], slng_ref[...], slnb_ref[...])
    lane_s = lax.broadcasted_iota(jnp.int32, (SGU_CHUNK, SGU_WIDTH), 1)
    s_parts = []
    for n in range(T // SGU_CHUNK):
        vc = vln_ref[n * SGU_CHUNK:(n + 1) * SGU_CHUNK, :]
        s = sbias_ref[...]
        for g in range(SGU_GROUPS):
            vg = jnp.where((lane_s >= g * SGU_GD) & (lane_s < (g + 1) * SGU_GD), vc, 0.0).astype(BF16)
            s = s + _dot(wtril_ref[g], vg)
        s_parts.append(s)
    y_b = (u * jnp.concatenate(s_parts, axis=0)).astype(BF16)

    P = POOL_CARRY
    xc = proj(O_XC, O_GATE)
    e_ref[P:P + T, :] = xc
    s2_ref[8:P + T, :] = e_ref[8:P + T, :] + e_ref[7:P + T - 1, :]
    s4_ref[16:P + T, :] = s2_ref[16:P + T, :] + s2_ref[14:P + T - 2, :]
    s8_ref[24:P + T, :] = s4_ref[24:P + T, :] + s4_ref[20:P + T - 4, :]
    s16 = s8_ref[P:P + T, :] + s8_ref[P - 8:P + T - 8, :]
    lane_p = lax.broadcasted_iota(jnp.int32, (T, POOL_WIDTH), 1)
    tpos = lax.broadcasted_iota(jnp.int32, (T, POOL_WIDTH), 0) + (j * T + 1)
    grp = lane_p >> 6
    win = jnp.where(grp == 0, POOL_WINDOWS[0], jnp.where(grp == 1, POOL_WINDOWS[1],
                    jnp.where(grp == 2, POOL_WINDOWS[2], POOL_WINDOWS[3])))
    wsum = jnp.where(grp == 0, s2_ref[P:P + T, :], jnp.where(grp == 1, s4_ref[P:P + T, :],
                     jnp.where(grp == 2, s8_ref[P:P + T, :], s16)))
    count = jnp.minimum(tpos, win).astype(F32)
    pooled = wsum / count - xc
    y_c = (_dot(pooled.astype(BF16), poolw_ref[...]) * pscale_ref[...]).astype(BF16)

    ra, rb = GLA_VAL, GLA_VAL + SGU_WIDTH
    merged = _sigmoid(proj(O_GATE, O_GATE + D)) * _dot(ya_ref[...], wup_ref[0:ra, :])
    merged += _sigmoid(proj(O_GATE + D, O_GATE + 2 * D)) * _dot(y_b, wup_ref[ra:rb, :])
    merged += _sigmoid(proj(O_GATE + 2 * D, O_GATE + 3 * D)) * _dot(y_c, wup_ref[rb:, :])
    h = _dot(merged.astype(BF16), wo_ref[...])
    out_ref[0] = _layer_norm(DEEPNORM_ALPHA * x + h, ln1g_ref[...], ln1b_ref[...])


_MIXER_WEIGHTS = ('wcat', 'bcat', 'wg2', 'bg', 'gnorm', 'slng', 'slnb', 'wtril', 'sbias', 'poolw', 'pscale',
                  'wup', 'wo', 'ln1g', 'ln1b')


def _mixer(x, p, l):
    B, S, D = x.shape
    T = MIX_TILE
    weights = [p[n] for n in _MIXER_WEIGHTS]
    return pl.pallas_call(
        _mixer_kernel,
        out_shape=jax.ShapeDtypeStruct((B, S, D), F32),
        grid=(B, S // T),
        in_specs=[pl.BlockSpec((1, T, D), lambda b, j: (b, j, 0))] + [_layer_spec(w, l) for w in weights],
        out_specs=pl.BlockSpec((1, T, D), lambda b, j: (b, j, 0)),
        scratch_shapes=[
            pltpu.VMEM((GLA_KEY, GLA_VAL), F32),
            pltpu.VMEM((T, 2 * GLA_KEY + 2 * GLA_VAL), F32),
            pltpu.VMEM((T, GLA_VAL), BF16),
            pltpu.VMEM((T, SGU_WIDTH), F32),
            pltpu.VMEM((T + POOL_CARRY, POOL_WIDTH), F32),
            pltpu.VMEM((T + POOL_CARRY, POOL_WIDTH), F32),
            pltpu.VMEM((T + POOL_CARRY, POOL_WIDTH), F32),
            pltpu.VMEM((T + POOL_CARRY, POOL_WIDTH), F32),
        ],
        compiler_params=pltpu.CompilerParams(dimension_semantics=("arbitrary", "arbitrary"),
                                             vmem_limit_bytes=VMEM_LIMIT),
        name="mixer",
    )(x, *weights)


def _memkv_kernel(memt_ref, mem_ref, wkt_ref, wv_ref, kt_ref, v_ref):
    kt_ref[0] = _dot(wkt_ref[...], memt_ref[0].astype(BF16)).astype(BF16)
    v_ref[0] = _dot(mem_ref[0].astype(BF16), wv_ref[...]).astype(BF16)


def _memkv(mem, memt, p, l):
    B, M, D = mem.shape
    return pl.pallas_call(
        _memkv_kernel,
        out_shape=(jax.ShapeDtypeStruct((B, D, M), BF16), jax.ShapeDtypeStruct((B, M, D), BF16)),
        grid=(B,),
        in_specs=[pl.BlockSpec((1, D, M), lambda b: (b, 0, 0)), pl.BlockSpec((1, M, D), lambda b: (b, 0, 0)),
                  _layer_spec(p['wkt'], l), _layer_spec(p['wv'], l)],
        out_specs=(pl.BlockSpec((1, D, M), lambda b: (b, 0, 0)), pl.BlockSpec((1, M, D), lambda b: (b, 0, 0))),
        compiler_params=pltpu.CompilerParams(dimension_semantics=("arbitrary",), vmem_limit_bytes=VMEM_LIMIT),
        name="memkv",
    )(memt, mem, p['wkt'], p['wv'])


def _xattn_kernel(x_ref, kt_ref, v_ref, wq_ref, wo_ref, ln2g_ref, ln2b_ref, rwhi_ref, rwlo_ref, rb_ref,
                  x2_ref, x2p_ref, logit_ref):
    x = x_ref[0]
    q = (_dot(x.astype(BF16), wq_ref[...]) * (XA_DH ** -0.5)).astype(BF16)
    h = jnp.zeros_like(x)
    for hd in range(XA_HEADS):
        cs = slice(hd * XA_DH, (hd + 1) * XA_DH)
        s = _dot(q[:, cs], kt_ref[0, cs, :])
        e = jnp.exp(s - jnp.max(s, axis=-1, keepdims=True))
        o = _dot(e.astype(BF16), v_ref[0, :, cs]) / jnp.sum(e, axis=-1, keepdims=True)
        h = h + _dot(o.astype(BF16), wo_ref[cs, :])
    x2 = _layer_norm(DEEPNORM_ALPHA * x + h, ln2g_ref[...], ln2b_ref[...])
    x2_ref[0] = x2
    x2p_ref[0] = _pack_bf16_pairs(x2)
    hi, lo = _split_bf16(x2)
    logit_ref[0] = _dot(hi, rwhi_ref[...]) + (_dot(hi, rwlo_ref[...]) + _dot(lo, rwhi_ref[...])) + rb_ref[...]


def _xattn(x, kt, v, p, l):
    B, S, D = x.shape
    T = XA_TILE
    M = MEM_LEN
    weights = [p[n] for n in ('wq', 'wxo', 'ln2g', 'ln2b', 'rwhi', 'rwlo', 'rb')]
    return pl.pallas_call(
        _xattn_kernel,
        out_shape=(jax.ShapeDtypeStruct((B, S, D), F32), jax.ShapeDtypeStruct((B, S, D // 2), jnp.uint32),
                   jax.ShapeDtypeStruct((B, S, LANES), F32)),
        grid=(B, S // T),
        in_specs=[pl.BlockSpec((1, T, D), lambda b, j: (b, j, 0)),
                  pl.BlockSpec((1, D, M), lambda b, j: (b, 0, 0)),
                  pl.BlockSpec((1, M, D), lambda b, j: (b, 0, 0))] + [_layer_spec(w, l) for w in weights],
        out_specs=(pl.BlockSpec((1, T, D), lambda b, j: (b, j, 0)),
                   pl.BlockSpec((1, T, D // 2), lambda b, j: (b, j, 0)),
                   pl.BlockSpec((1, T, LANES), lambda b, j: (b, j, 0))),
        compiler_params=pltpu.CompilerParams(dimension_semantics=("arbitrary", "arbitrary"),
                                             vmem_limit_bytes=VMEM_LIMIT),
        name="xattn",
    )(x, kt, v, *weights)


def _expert_kernel(layer, be_ref, nb_ref, nxt_ref, xs_ref, wgu_hbm, bgu_ref, wd_hbm, bd_ref, out_ref,
                   wgu_st, wd_st, wgu_bf, wd_bf, sem):
    i = pl.program_id(0)
    F = EXPERT_FF

    def weight_copies(e):
        return (pltpu.make_async_copy(wgu_hbm.at[layer, e], wgu_st, sem.at[0]),
                pltpu.make_async_copy(wd_hbm.at[layer, e], wd_st, sem.at[1]))

    @pl.when(i < nb_ref[0])
    def _():
        e = be_ref[i]
        prev = be_ref[jnp.maximum(i - 1, 0)]

        @pl.when(i == 0)
        def _():
            for cp in weight_copies(e):
                cp.start()

        @pl.when((i == 0) | (e != prev))
        def _():
            for cp in weight_copies(e):
                cp.wait()
            wgu_bf[...] = wgu_st[...].astype(BF16)
            wd_bf[...] = wd_st[...].astype(BF16)
            nxt = nxt_ref[e]

            @pl.when(nxt != e)
            def _():
                for cp in weight_copies(nxt):
                    cp.start()

        hh = _dot(_unpack_bf16_pairs(xs_ref[...]).astype(BF16), wgu_bf[...]) + bgu_ref[...]
        h_glu = jnp.minimum(hh[:, :F], SWIGLU_LIMIT)
        h_lin = jnp.clip(hh[:, F:], -SWIGLU_LIMIT, SWIGLU_LIMIT)
        a = h_glu * _sigmoid(SWIGLU_ALPHA * h_glu) * (h_lin + 1.0)
        out_ref[...] = _pack_bf16_pairs(_dot(a.astype(BF16), wd_bf[...]) + bd_ref[...])


def _experts(block_expert, n_used, next_expert, xs, w_gu, b_gu, w_down, b_down, l):
    P, DH = xs.shape
    D = 2 * DH
    NB = P // MOE_BLOCK
    F2 = 2 * EXPERT_FF

    def row_map(i, be, nb, nxt):
        return (jnp.minimum(i, nb[0] - 1), 0)

    def exp_map(i, be, nb, nxt):
        return (l, be[jnp.minimum(i, nb[0] - 1)], 0, 0)

    grid_spec = pltpu.PrefetchScalarGridSpec(
        num_scalar_prefetch=3,
        grid=(NB,),
        in_specs=[pl.BlockSpec((MOE_BLOCK, DH), row_map),
                  pl.BlockSpec(memory_space=pl.ANY),
                  pl.BlockSpec((None, None, 1, F2), exp_map),
                  pl.BlockSpec(memory_space=pl.ANY),
                  pl.BlockSpec((None, None, 1, D), exp_map)],
        out_specs=pl.BlockSpec((MOE_BLOCK, DH), row_map),
        scratch_shapes=[pltpu.VMEM((D, F2), F32), pltpu.VMEM((EXPERT_FF, D), F32),
                        pltpu.VMEM((D, F2), BF16), pltpu.VMEM((EXPERT_FF, D), BF16),
                        pltpu.SemaphoreType.DMA((2,))],
    )
    return pl.pallas_call(
        functools.partial(_expert_kernel, l),
        out_shape=jax.ShapeDtypeStruct((P, DH), jnp.uint32),
        grid_spec=grid_spec,
        compiler_params=pltpu.CompilerParams(dimension_semantics=("arbitrary",), vmem_limit_bytes=VMEM_LIMIT),
        name="experts",
    )(block_expert, n_used, next_expert, xs, w_gu, b_gu.reshape(DEPTH, N_EXPERTS, 1, F2), w_down,
      b_down.reshape(DEPTH, N_EXPERTS, 1, D))


def _sc_gather_rows(x, idx):
    M = idx.shape[0]
    D = x.shape[1]
    W = SC_GATHER_WINDOW
    mesh = plsc.VectorSubcoreMesh(core_axis_name="core", subcore_axis_name="subcore")
    n_workers = mesh.num_cores * mesh.num_subcores
    rows_per = M // n_workers
    assert rows_per * n_workers == M and rows_per % W == 0

    @pl.kernel(out_type=jax.ShapeDtypeStruct((M, D), x.dtype), mesh=mesh, name="sc_gather_rows",
               scratch_types=[pltpu.VMEM((rows_per,), jnp.int32), pltpu.VMEM((W, D), x.dtype)])
    def gather_kernel(x_hbm, i_hbm, o_hbm, idx_vmem, buf):
        wid = lax.axis_index("core") * mesh.num_subcores + lax.axis_index("subcore")
        base = wid * rows_per
        pltpu.sync_copy(i_hbm.at[pl.ds(base, rows_per)], idx_vmem)

        @pl.loop(0, rows_per // W)
        def _(j):
            pltpu.sync_copy(x_hbm.at[idx_vmem.at[pl.ds(j * W, W)]], buf)
            pltpu.sync_copy(buf, o_hbm.at[pl.ds(base + j * W, W)])

    return gather_kernel(x, idx)


def _sc_scatter_rows(x, idx, n_out):
    K, N = idx.shape
    D = x.shape[1]
    W = SC_GATHER_WINDOW
    mesh = plsc.VectorSubcoreMesh(core_axis_name="core", subcore_axis_name="subcore")
    n_workers = mesh.num_cores * mesh.num_subcores
    rows_per = N // n_workers
    assert rows_per * n_workers == N and rows_per % W == 0

    @pl.kernel(out_type=jax.ShapeDtypeStruct((n_out, D), x.dtype), mesh=mesh, name="sc_scatter_rows",
               scratch_types=[pltpu.VMEM((K * rows_per,), jnp.int32), pltpu.VMEM((W, D), x.dtype)])
    def scatter_kernel(x_hbm, i_hbm, o_hbm, idx_vmem, buf):
        wid = lax.axis_index("core") * mesh.num_subcores + lax.axis_index("subcore")
        base = wid * rows_per
        for k in range(K):
            pltpu.sync_copy(i_hbm.at[pl.ds(k * N + base, rows_per)], idx_vmem.at[pl.ds(k * rows_per, rows_per)])

        @pl.loop(0, rows_per // W)
        def _(j):
            pltpu.sync_copy(x_hbm.at[pl.ds(base + j * W, W)], buf)
            for k in range(K):
                pltpu.sync_copy(buf, o_hbm.at[idx_vmem.at[pl.ds(k * rows_per + j * W, W)]])

    return scatter_kernel(x, idx.reshape(K * N))


def _combine_kernel(x_ref, yg_ref, gate_ref, ln3g_ref, ln3b_ref, out_ref):
    x = x_ref[...]
    g = gate_ref[...]
    y = jnp.zeros_like(x)
    for k in range(TOP_K):
        y = y + g[:, k:k + 1] * _unpack_bf16_pairs(yg_ref[k * CMB_TILE:(k + 1) * CMB_TILE, :])
    out_ref[...] = _layer_norm(DEEPNORM_ALPHA * x + y, ln3g_ref[...], ln3b_ref[...])


def _combine(x2, yg, gate, p, l):
    N, D = x2.shape
    T = CMB_TILE
    return pl.pallas_call(
        _combine_kernel,
        out_shape=jax.ShapeDtypeStruct((N, D), F32),
        grid=(N // T,),
        in_specs=[pl.BlockSpec((T, D), lambda i: (i, 0)), pl.BlockSpec((TOP_K * T, D // 2), lambda i: (i, 0)),
                  pl.BlockSpec((T, TOP_K), lambda i: (i, 0)), _layer_spec(p['ln3g'], l), _layer_spec(p['ln3b'], l)],
        out_specs=pl.BlockSpec((T, D), lambda i: (i, 0)),
        compiler_params=pltpu.CompilerParams(dimension_semantics=("arbitrary",), vmem_limit_bytes=VMEM_LIMIT),
        name="combine",
    )(x2, yg, gate, p['ln3g'], p['ln3b'])


def _prep(w_in, b_in, gla_wg2, gla_bg, gla_norm_g, sgu_ln_g, sgu_ln_b, sgu_ws, sgu_bs, pool_w, pool_scale,
          w_up_a, w_up_b, w_up_c, w_o, ln1_g, ln1_b, xa_wq, xa_wk, xa_wv, xa_wo, ln2_g, ln2_b,
          router_w, router_b, ln3_g, ln3_b):
    L = w_in.shape[0]
    o_glow = O_GLOW
    o_uv = o_glow + GLA_RANK
    row = lambda a: a.reshape(L, 1, -1).astype(F32)
    pad_last = lambda a, n: jnp.pad(a, [(0, 0)] * (a.ndim - 1) + [(0, n - a.shape[-1])])
    p = {}
    p['wcat'] = jnp.concatenate([w_in[..., :o_glow], pad_last(w_in[..., o_glow:o_uv], LANES), w_in[..., o_uv:]],
                                axis=-1).astype(BF16)
    p['bcat'] = row(jnp.concatenate([b_in[..., :o_glow], pad_last(b_in[..., o_glow:o_uv], LANES), b_in[..., o_uv:]],
                                    axis=-1))
    p['wg2'] = jnp.pad(gla_wg2, ((0, 0), (0, LANES - GLA_RANK), (0, 0))).astype(BF16)
    p['bg'] = row(gla_bg)
    p['gnorm'] = row(gla_norm_g)
    p['slng'] = row(sgu_ln_g)
    p['slnb'] = row(sgu_ln_b)
    p['wtril'] = jnp.tril(sgu_ws).astype(BF16)
    p['sbias'] = jnp.repeat(jnp.swapaxes(sgu_bs, 1, 2), SGU_GD, axis=2).astype(F32)
    G = len(POOL_WINDOWS)
    eye = jnp.eye(G, dtype=F32)
    p['poolw'] = jnp.einsum('lgcd,gh->lgchd', pool_w, eye).reshape(L, POOL_WIDTH, POOL_WIDTH).astype(BF16)
    p['pscale'] = row(pool_scale)
    p['wup'] = jnp.concatenate([w_up_a, w_up_b, w_up_c], axis=1).astype(BF16)
    p['wo'] = w_o.astype(BF16)
    p['ln1g'], p['ln1b'] = row(ln1_g), row(ln1_b)
    p['wq'] = xa_wq.astype(BF16)
    p['wkt'] = jnp.swapaxes(xa_wk, 1, 2).astype(BF16)
    p['wv'] = xa_wv.astype(BF16)
    p['wxo'] = xa_wo.astype(BF16)
    p['ln2g'], p['ln2b'] = row(ln2_g), row(ln2_b)
    rw = pad_last(router_w, LANES)
    p['rwhi'] = rw.astype(BF16)
    p['rwlo'] = (rw - p['rwhi'].astype(F32)).astype(BF16)
    p['rb'] = row(pad_last(router_b, LANES))
    p['ln3g'], p['ln3b'] = row(ln3_g), row(ln3_b)
    return p


def _route(logits):
    N = logits.shape[0]
    A = N * TOP_K
    top_vals, top_idx = lax.top_k(logits, TOP_K)
    gate = jax.nn.softmax(top_vals, axis=-1)
    flat_e = top_idx.reshape(-1)
    onehot = (flat_e[:, None] == jnp.arange(N_EXPERTS, dtype=jnp.int32)[None, :]).astype(jnp.int32)
    csum = jnp.cumsum(onehot, axis=0)
    counts = csum[-1]
    padded = ((counts + MOE_BLOCK - 1) // MOE_BLOCK) * MOE_BLOCK
    pad_end = jnp.cumsum(padded)
    pad_start = pad_end - padded
    dest = jnp.sum(onehot * (csum - 1 + pad_start[None, :]), axis=1)
    n_blocks = A // MOE_BLOCK + N_EXPERTS
    block_start = jnp.arange(n_blocks, dtype=jnp.int32) * MOE_BLOCK
    block_expert = jnp.minimum(jnp.sum((pad_end[None, :] <= block_start[:, None]).astype(jnp.int32), axis=1),
                               N_EXPERTS - 1)
    n_used = (pad_end[-1] // MOE_BLOCK).astype(jnp.int32).reshape(1)
    ids = jnp.arange(N_EXPERTS, dtype=jnp.int32)
    later = jnp.where((ids[None, :] > ids[:, None]) & (counts[None, :] > 0), ids[None, :], N_EXPERTS)
    first_later = jnp.min(later, axis=1)
    next_expert = jnp.where(first_later < N_EXPERTS, first_later, ids).astype(jnp.int32)
    return gate, dest, block_expert, n_used, next_expert


def kernel(x, mem, w_in, b_in, gla_wg2, gla_bg, gla_norm_g, sgu_ln_g, sgu_ln_b, sgu_ws, sgu_bs, pool_w, pool_scale, w_up_a, w_up_b, w_up_c, w_o, ln1_g, ln1_b, xa_wq, xa_wk, xa_wv, xa_wo, ln2_g, ln2_b, router_w, router_b, exp_w_gu, exp_b_gu, exp_w_down, exp_b_down, ln3_g, ln3_b):
    B, S, D = x.shape
    N = B * S
    p = _prep(w_in, b_in, gla_wg2, gla_bg, gla_norm_g, sgu_ln_g, sgu_ln_b, sgu_ws, sgu_bs, pool_w, pool_scale,
              w_up_a, w_up_b, w_up_c, w_o, ln1_g, ln1_b, xa_wq, xa_wk, xa_wv, xa_wo, ln2_g, ln2_b,
              router_w, router_b, ln3_g, ln3_b)
    memt = jnp.swapaxes(mem, 1, 2)
    for l in range(DEPTH):
        x1 = _mixer(x, p, l)
        kt, v = _memkv(mem, memt, p, l)
        x2, x2p, logits = _xattn(x1, kt, v, p, l)
        gate, dest, block_expert, n_used, next_expert = _route(logits.reshape(N, LANES)[:, :N_EXPERTS])
        n_slots = (N * TOP_K // MOE_BLOCK + N_EXPERTS) * MOE_BLOCK
        xs = _sc_scatter_rows(x2p.reshape(N, D // 2), dest.reshape(N, TOP_K).T, n_slots)
        ys = _experts(block_expert, n_used, next_expert, xs, exp_w_gu, exp_b_gu, exp_w_down, exp_b_down, l)
        dest_km = dest.reshape(N // CMB_TILE, CMB_TILE, TOP_K).transpose(0, 2, 1).reshape(-1)
        yg = _sc_gather_rows(ys, dest_km)
        x = _combine(x2.reshape(N, D), yg, gate, p, l).reshape(B, S, D)
    return x
```

```python
import functools

import jax
import jax.numpy as jnp
from jax import lax
from jax.experimental import pallas as pl
from jax.experimental.pallas import tpu as pltpu
from jax.experimental.pallas import tpu_sc as plsc

F32 = jnp.float32
BF16 = jnp.bfloat16

D_MODEL = 1024
DEPTH = 2
GLA_HEADS = 4
GLA_KEY = 256
GLA_VAL = 512
GLA_DK = 64
GLA_DV = 128
GLA_RANK = 16
GLA_TAU = 16.0
GLA_CHUNK = 64
SGU_GROUPS = 4
SGU_WIDTH = 256
SGU_GD = 64
SGU_CHUNK = 128
POOL_WINDOWS = (2, 4, 8, 16)
POOL_WIDTH = 256
POOL_GD = 64
POOL_CARRY = 32
MEM_LEN = 256
XA_HEADS = 4
XA_DH = 256
N_EXPERTS = 32
TOP_K = 4
EXPERT_FF = 1024
SWIGLU_LIMIT = 7.0
SWIGLU_ALPHA = 1.702
DEEPNORM_ALPHA = (2 * DEPTH) ** 0.25
LN_EPS = 1e-5
LANES = 128
VMEM_LIMIT = 56 * 1024 * 1024

MIX_TILE = 256
XA_TILE = 512
MOE_BLOCK = 256
CMB_TILE = 256
SC_GATHER_WINDOW = 32

O_QKVR = 0
O_GLOW = 2 * GLA_KEY + 2 * GLA_VAL
O_UV = O_GLOW + LANES
O_XC = O_UV + 2 * SGU_WIDTH
O_GATE = O_XC + POOL_WIDTH
N_PACK = O_GATE + 3 * D_MODEL


def _dot(a, b):
    return jnp.dot(a, b, preferred_element_type=F32)


def _dot_t0(a, b):
    return lax.dot_general(a, b, (((0,), (0,)), ((), ())), preferred_element_type=F32)


def _dot_t1(a, b):
    return lax.dot_general(a, b, (((1,), (1,)), ((), ())), preferred_element_type=F32)


def _split_bf16(x):
    hi = x.astype(BF16)
    lo = (x - hi.astype(F32)).astype(BF16)
    return hi, lo


def _layer_norm(x, g, b):
    mu = jnp.mean(x, axis=-1, keepdims=True)
    xc = x - mu
    var = jnp.mean(xc * xc, axis=-1, keepdims=True)
    return xc * lax.rsqrt(var + LN_EPS) * g + b


def _sigmoid(x):
    return 1.0 / (1.0 + jnp.exp(-x))


def _pack_bf16_pairs(x):
    H = x.shape[1] // 2
    bits = lax.bitcast_convert_type(x.astype(BF16).astype(F32), jnp.uint32)
    return (bits[:, :H] >> 16) | (bits[:, H:] & jnp.uint32(0xFFFF0000))


def _unpack_bf16_pairs(w):
    lo = lax.bitcast_convert_type(w << 16, F32)
    hi = lax.bitcast_convert_type(w & jnp.uint32(0xFFFF0000), F32)
    return jnp.concatenate([lo, hi], axis=1)


def _layer_spec(arr, l):
    nd = arr.ndim - 1
    return pl.BlockSpec((None,) + arr.shape[1:], lambda *_: (l,) + (0,) * nd, pipeline_mode=pl.Buffered(1))


def _mixer_kernel(x_ref, wcat_ref, bcat_ref, wg2_ref, bg_ref, gnorm_ref,
                  slng_ref, slnb_ref, wtril_ref, sbias_ref, poolw_ref, pscale_ref,
                  wup_ref, wo_ref, ln1g_ref, ln1b_ref,
                  out_ref,
                  state_ref, qkvr_ref, ya_ref, vln_ref, e_ref, s2_ref, s4_ref, s8_ref):
    T = MIX_TILE
    D = D_MODEL
    j = pl.program_id(1)
    x = x_ref[0]
    xb = x.astype(BF16)

    def proj(lo, hi):
        return _dot(xb, wcat_ref[:, lo:hi]) + bcat_ref[:, lo:hi]

    @pl.when(j == 0)
    def _():
        state_ref[...] = jnp.zeros_like(state_ref)
        e_ref[0:POOL_CARRY, :] = jnp.zeros((POOL_CARRY, POOL_WIDTH), F32)

    @pl.when(j > 0)
    def _():
        e_ref[0:POOL_CARRY, :] = e_ref[T:T + POOL_CARRY, :]

    qkvr_ref[...] = proj(O_QKVR, O_GLOW)
    glow = proj(O_GLOW, O_UV)
    z = _dot(glow.astype(BF16), wg2_ref[...]) + bg_ref[...]
    la = (jnp.minimum(z, 0.0) - jnp.log1p(jnp.exp(-jnp.abs(z)))) * (1.0 / GLA_TAU)
    la_hi, la_lo = _split_bf16(la)

    C = GLA_CHUNK
    NC = T // C
    CSH = C.bit_length() - 1
    row = lax.broadcasted_iota(jnp.int32, (T, T), 0)
    col = lax.broadcasted_iota(jnp.int32, (T, T), 1)
    same_chunk = (row >> CSH) == (col >> CSH)
    causal = same_chunk & (row >= col)
    causal_bf = jnp.where(causal, 1.0, 0.0).astype(BF16)
    chunk_bf = jnp.where(same_chunk, 1.0, 0.0).astype(BF16)
    b = _dot(causal_bf, la_hi) + _dot(causal_bf, la_lo)
    b_end = _dot(chunk_bf, la_hi) + _dot(chunk_bf, la_lo)
    q = qkvr_ref[:, 0:GLA_KEY]
    k = qkvr_ref[:, GLA_KEY:2 * GLA_KEY]
    v = qkvr_ref[:, 2 * GLA_KEY:2 * GLA_KEY + GLA_VAL].astype(BF16)
    q_dec = q * (GLA_DK ** -0.5) * jnp.exp(b)
    k_dec = (k * jnp.exp(-b)).astype(BF16)
    k_tail = (k * jnp.exp(b_end - b)).astype(BF16)
    q_dec_bf = q_dec.astype(BF16)
    lane = lax.broadcasted_iota(jnp.int32, (T, GLA_KEY), 1)
    o_heads = []
    for h in range(GLA_HEADS):
        q_h = jnp.where((lane >= h * GLA_DK) & (lane < (h + 1) * GLA_DK), q_dec, 0.0).astype(BF16)
        scores = jnp.where(causal, _dot_t1(q_h, k_dec), 0.0).astype(BF16)
        o_heads.append(_dot(scores, v[:, h * GLA_DV:(h + 1) * GLA_DV]))
    o_intra = jnp.concatenate(o_heads, axis=1)
    ind = jnp.where((lax.broadcasted_iota(jnp.int32, (T, NC * LANES), 0) >> CSH)
                    == (lax.broadcasted_iota(jnp.int32, (T, NC * LANES), 1) >> 7), 1.0, 0.0).astype(BF16)
    dec_all = jnp.exp(_dot_t0(la_hi, ind) + _dot_t0(la_lo, ind))
    o_inter = []
    for c in range(NC):
        rows = slice(c * C, (c + 1) * C)
        o_inter.append(_dot(q_dec_bf[rows], state_ref[...].astype(BF16)))
        kv = _dot_t0(k_tail[rows], v[rows])
        for h in range(GLA_HEADS):
            rs = slice(h * GLA_DK, (h + 1) * GLA_DK)
            cs = slice(h * GLA_DV, (h + 1) * GLA_DV)
            state_ref[rs, cs] = dec_all[rs, c * LANES:(c + 1) * LANES] * state_ref[rs, cs] + kv[rs, cs]
    o = o_intra + jnp.concatenate(o_inter, axis=0)
    for h in range(GLA_HEADS):
        cs = slice(h * GLA_DV, (h + 1) * GLA_DV)
        o_h = o[:, cs]
        ms = jnp.mean(o_h * o_h, axis=-1, keepdims=True)
        o_h = o_h * lax.rsqrt(ms + LN_EPS) * gnorm_ref[:, cs]
        r_h = qkvr_ref[:, 2 * GLA_KEY + GLA_VAL + h * GLA_DV:2 * GLA_KEY + GLA_VAL + (h + 1) * GLA_DV]
        ya_ref[:, cs] = (o_h * (r_h * _sigmoid(r_h))).astype(BF16)

    uv = proj(O_UV, O_XC)
    zg = 0.5 * uv * (1.0 + lax.erf(uv * (2.0 ** -0.5)))
    u = zg[:, :SGU_WIDTH]
    vln_ref[...] = _layer_norm(zg[:, SGU_WIDTH:], slng_ref[...], slnb_ref[...])
    lane_s = lax.broadcasted_iota(jnp.int32, (SGU_CHUNK, SGU_WIDTH), 1)
    s_parts = []
    for n in range(T // SGU_CHUNK):
        vc = vln_ref[n * SGU_CHUNK:(n + 1) * SGU_CHUNK, :]
        s = sbias_ref[...]
        for g in range(SGU_GROUPS):
            vg = jnp.where((lane_s >= g * SGU_GD) & (lane_s < (g + 1) * SGU_GD), vc, 0.0).astype(BF16)
            s = s + _dot(wtril_ref[g], vg)
        s_parts.append(s)
    y_b = (u * jnp.concatenate(s_parts, axis=0)).astype(BF16)

    P = POOL_CARRY
    xc = proj(O_XC, O_GATE)
    e_ref[P:P + T, :] = xc
    s2_ref[8:P + T, :] = e_ref[8:P + T, :] + e_ref[7:P + T - 1, :]
    s4_ref[16:P + T, :] = s2_ref[16:P + T, :] + s2_ref[14:P + T - 2, :]
    s8_ref[24:P + T, :] = s4_ref[24:P + T, :] + s4_ref[20:P + T - 4, :]
    s16 = s8_ref[P:P + T, :] + s8_ref[P - 8:P + T - 8, :]
    lane_p = lax.broadcasted_iota(jnp.int32, (T, POOL_WIDTH), 1)
    tpos = lax.broadcasted_iota(jnp.int32, (T, POOL_WIDTH), 0) + (j * T + 1)
    grp = lane_p >> 6
    win = jnp.where(grp == 0, POOL_WINDOWS[0], jnp.where(grp == 1, POOL_WINDOWS[1],
                    jnp.where(grp == 2, POOL_WINDOWS[2], POOL_WINDOWS[3])))
    wsum = jnp.where(grp == 0, s2_ref[P:P + T, :], jnp.where(grp == 1, s4_ref[P:P + T, :],
                     jnp.where(grp == 2, s8_ref[P:P + T, :], s16)))
    count = jnp.minimum(tpos, win).astype(F32)
    pooled = wsum / count - xc
    y_c = (_dot(pooled.astype(BF16), poolw_ref[...]) * pscale_ref[...]).astype(BF16)

    ra, rb = GLA_VAL, GLA_VAL + SGU_WIDTH
    merged = _sigmoid(proj(O_GATE, O_GATE + D)) * _dot(ya_ref[...], wup_ref[0:ra, :])
    merged += _sigmoid(proj(O_GATE + D, O_GATE + 2 * D)) * _dot(y_b, wup_ref[ra:rb, :])
    merged += _sigmoid(proj(O_GATE + 2 * D, O_GATE + 3 * D)) * _dot(y_c, wup_ref[rb:, :])
    h = _dot(merged.astype(BF16), wo_ref[...])
    out_ref[0] = _layer_norm(DEEPNORM_ALPHA * x + h, ln1g_ref[...], ln1b_ref[...])


_MIXER_WEIGHTS = ('wcat', 'bcat', 'wg2', 'bg', 'gnorm', 'slng', 'slnb', 'wtril', 'sbias', 'poolw', 'pscale',
                  'wup', 'wo', 'ln1g', 'ln1b')


def _mixer(x, p, l):
    B, S, D = x.shape
    T = MIX_TILE
    weights = [p[n] for n in _MIXER_WEIGHTS]
    return pl.pallas_call(
        _mixer_kernel,
        out_shape=jax.ShapeDtypeStruct((B, S, D), F32),
        grid=(B, S // T),
        in_specs=[pl.BlockSpec((1, T, D), lambda b, j: (b, j, 0))] + [_layer_spec(w, l) for w in weights],
        out_specs=pl.BlockSpec((1, T, D), lambda b, j: (b, j, 0)),
        scratch_shapes=[
            pltpu.VMEM((GLA_KEY, GLA_VAL), F32),
            pltpu.VMEM((T, 2 * GLA_KEY + 2 * GLA_VAL), F32),
            pltpu.VMEM((T, GLA_VAL), BF16),
            pltpu.VMEM((T, SGU_WIDTH), F32),
            pltpu.VMEM((T + POOL_CARRY, POOL_WIDTH), F32),
            pltpu.VMEM((T + POOL_CARRY, POOL_WIDTH), F32),
            pltpu.VMEM((T + POOL_CARRY, POOL_WIDTH), F32),
            pltpu.VMEM((T + POOL_CARRY, POOL_WIDTH), F32),
        ],
        compiler_params=pltpu.CompilerParams(dimension_semantics=("arbitrary", "arbitrary"),
                                             vmem_limit_bytes=VMEM_LIMIT),
        name="mixer",
    )(x, *weights)


def _memkv_kernel(memt_ref, mem_ref, wkt_ref, wv_ref, kt_ref, v_ref):
    kt_ref[0] = _dot(wkt_ref[...], memt_ref[0].astype(BF16)).astype(BF16)
    v_ref[0] = _dot(mem_ref[0].astype(BF16), wv_ref[...]).astype(BF16)


def _memkv(mem, memt, p, l):
    B, M, D = mem.shape
    return pl.pallas_call(
        _memkv_kernel,
        out_shape=(jax.ShapeDtypeStruct((B, D, M), BF16), jax.ShapeDtypeStruct((B, M, D), BF16)),
        grid=(B,),
        in_specs=[pl.BlockSpec((1, D, M), lambda b: (b, 0, 0)), pl.BlockSpec((1, M, D), lambda b: (b, 0, 0)),
                  _layer_spec(p['wkt'], l), _layer_spec(p['wv'], l)],
        out_specs=(pl.BlockSpec((1, D, M), lambda b: (b, 0, 0)), pl.BlockSpec((1, M, D), lambda b: (b, 0, 0))),
        compiler_params=pltpu.CompilerParams(dimension_semantics=("arbitrary",), vmem_limit_bytes=VMEM_LIMIT),
        name="memkv",
    )(memt, mem, p['wkt'], p['wv'])


def _xattn_kernel(x_ref, kt_ref, v_ref, wq_ref, wo_ref, ln2g_ref, ln2b_ref, rwcat_ref, rb_ref,
                  x2_ref, x2p_ref, route_ref, counts_ref, carry_ref):
    T = XA_TILE

    @pl.when((pl.program_id(0) == 0) & (pl.program_id(1) == 0))
    def _():
        carry_ref[...] = jnp.zeros_like(carry_ref)

    x = x_ref[0]
    q = (_dot(x.astype(BF16), wq_ref[...]) * (XA_DH ** -0.5)).astype(BF16)
    h = jnp.zeros_like(x)
    for hd in range(XA_HEADS):
        cs = slice(hd * XA_DH, (hd + 1) * XA_DH)
        s = _dot(q[:, cs], kt_ref[0, cs, :])
        e = jnp.exp(s - jnp.max(s, axis=-1, keepdims=True))
        o = _dot(e.astype(BF16), v_ref[0, :, cs]) / jnp.sum(e, axis=-1, keepdims=True)
        h = h + _dot(o.astype(BF16), wo_ref[cs, :])
    x2 = _layer_norm(DEEPNORM_ALPHA * x + h, ln2g_ref[...], ln2b_ref[...])
    x2_ref[0] = x2
    x2p_ref[0] = _pack_bf16_pairs(x2)

    hi, lo = _split_bf16(x2)
    hw = _dot(hi, rwcat_ref[...])
    logits = hw[:, :LANES] + (hw[:, LANES:] + _dot(lo, rwcat_ref[:, :LANES])) + rb_ref[...]

    lane = lax.broadcasted_iota(jnp.int32, (T, LANES), 1)
    neg_inf = jnp.float32(-jnp.inf)
    rest = jnp.where(lane < N_EXPERTS, logits, neg_inf)
    tops, picks = [], []
    for _ in range(TOP_K):
        m = jnp.max(rest, axis=-1, keepdims=True)
        idx = jnp.min(jnp.where(rest == m, lane, LANES), axis=-1, keepdims=True)
        pick = lane == idx
        rest = jnp.where(pick, neg_inf, rest)
        tops.append((m, idx))
        picks.append(pick)
    exps = [jnp.exp(m - tops[0][0]) for m, _ in tops]
    denom = exps[0]
    for e in exps[1:]:
        denom = denom + e

    chosen = jnp.zeros((T, LANES), F32)
    for pick in picks:
        chosen = chosen + jnp.where(pick, 1.0, 0.0)
    strict_lower = (lax.broadcasted_iota(jnp.int32, (T, T), 0) > lax.broadcasted_iota(jnp.int32, (T, T), 1))
    before = _dot(jnp.where(strict_lower, 1.0, 0.0).astype(BF16), chosen.astype(BF16)) + carry_ref[...]
    carry_ref[...] = carry_ref[...] + jnp.sum(chosen, axis=0, keepdims=True)
    counts_ref[...] = carry_ref[...]

    route = jnp.zeros((T, LANES), F32)
    for k in range(TOP_K):
        rank = jnp.sum(jnp.where(picks[k], before, 0.0), axis=-1, keepdims=True)
        route = jnp.where(lane == k, tops[k][1].astype(F32), route)
        route = jnp.where(lane == TOP_K + k, exps[k] / denom, route)
        route = jnp.where(lane == 2 * TOP_K + k, rank, route)
    route_ref[0] = route


def _xattn(x, kt, v, p, l):
    B, S, D = x.shape
    T = XA_TILE
    M = MEM_LEN
    weights = [p[n] for n in ('wq', 'wxo', 'ln2g', 'ln2b', 'rwcat', 'rb')]
    return pl.pallas_call(
        _xattn_kernel,
        out_shape=(jax.ShapeDtypeStruct((B, S, D), F32), jax.ShapeDtypeStruct((B, S, D // 2), jnp.uint32),
                   jax.ShapeDtypeStruct((B, S, LANES), F32), jax.ShapeDtypeStruct((1, LANES), F32)),
        grid=(B, S // T),
        in_specs=[pl.BlockSpec((1, T, D), lambda b, j: (b, j, 0)),
                  pl.BlockSpec((1, D, M), lambda b, j: (b, 0, 0)),
                  pl.BlockSpec((1, M, D), lambda b, j: (b, 0, 0))] + [_layer_spec(w, l) for w in weights],
        out_specs=(pl.BlockSpec((1, T, D), lambda b, j: (b, j, 0)),
                   pl.BlockSpec((1, T, D // 2), lambda b, j: (b, j, 0)),
                   pl.BlockSpec((1, T, LANES), lambda b, j: (b, j, 0)),
                   pl.BlockSpec((1, LANES), lambda b, j: (0, 0))),
        scratch_shapes=[pltpu.VMEM((1, LANES), F32)],
        compiler_params=pltpu.CompilerParams(dimension_semantics=("arbitrary", "arbitrary"),
                                             vmem_limit_bytes=VMEM_LIMIT),
        name="xattn",
    )(x, kt, v, *weights)


def _expert_kernel(layer, be_ref, nb_ref, nxt_ref, xs_ref, wgu_hbm, bgu_ref, wd_hbm, bd_ref, out_ref,
                   wgu_st, wd_st, wgu_bf, wd_bf, sem):
    i = pl.program_id(0)
    F = EXPERT_FF

    def weight_copies(e):
        return (pltpu.make_async_copy(wgu_hbm.at[layer, e], wgu_st, sem.at[0]),
                pltpu.make_async_copy(wd_hbm.at[layer, e], wd_st, sem.at[1]))

    @pl.when(i < nb_ref[0])
    def _():
        e = be_ref[i]
        prev = be_ref[jnp.maximum(i - 1, 0)]

        @pl.when(i == 0)
        def _():
            for cp in weight_copies(e):
                cp.start()

        @pl.when((i == 0) | (e != prev))
        def _():
            for cp in weight_copies(e):
                cp.wait()
            wgu_bf[...] = wgu_st[...].astype(BF16)
            wd_bf[...] = wd_st[...].astype(BF16)
            nxt = nxt_ref[e]

            @pl.when(nxt != e)
            def _():
                for cp in weight_copies(nxt):
                    cp.start()

        hh = _dot(_unpack_bf16_pairs(xs_ref[...]).astype(BF16), wgu_bf[...]) + bgu_ref[...]
        h_glu = jnp.minimum(hh[:, :F], SWIGLU_LIMIT)
        h_lin = jnp.clip(hh[:, F:], -SWIGLU_LIMIT, SWIGLU_LIMIT)
        a = h_glu * _sigmoid(SWIGLU_ALPHA * h_glu) * (h_lin + 1.0)
        out_ref[...] = _pack_bf16_pairs(_dot(a.astype(BF16), wd_bf[...]) + bd_ref[...])


def _experts(block_expert, n_used, next_expert, xs, w_gu, b_gu, w_down, b_down, l):
    P, DH = xs.shape
    D = 2 * DH
    NB = P // MOE_BLOCK
    F2 = 2 * EXPERT_FF

    def row_map(i, be, nb, nxt):
        return (jnp.minimum(i, nb[0] - 1), 0)

    def exp_map(i, be, nb, nxt):
        return (l, be[jnp.minimum(i, nb[0] - 1)], 0, 0)

    grid_spec = pltpu.PrefetchScalarGridSpec(
        num_scalar_prefetch=3,
        grid=(NB,),
        in_specs=[pl.BlockSpec((MOE_BLOCK, DH), row_map),
                  pl.BlockSpec(memory_space=pl.ANY),
                  pl.BlockSpec((None, None, 1, F2), exp_map),
                  pl.BlockSpec(memory_space=pl.ANY),
                  pl.BlockSpec((None, None, 1, D), exp_map)],
        out_specs=pl.BlockSpec((MOE_BLOCK, DH), row_map),
        scratch_shapes=[pltpu.VMEM((D, F2), F32), pltpu.VMEM((EXPERT_FF, D), F32),
                        pltpu.VMEM((D, F2), BF16), pltpu.VMEM((EXPERT_FF, D), BF16),
                        pltpu.SemaphoreType.DMA((2,))],
    )
    return pl.pallas_call(
        functools.partial(_expert_kernel, l),
        out_shape=jax.ShapeDtypeStruct((P, DH), jnp.uint32),
        grid_spec=grid_spec,
        compiler_params=pltpu.CompilerParams(dimension_semantics=("arbitrary",), vmem_limit_bytes=VMEM_LIMIT),
        name="experts",
    )(block_expert, n_used, next_expert, xs, w_gu, b_gu.reshape(DEPTH, N_EXPERTS, 1, F2), w_down,
      b_down.reshape(DEPTH, N_EXPERTS, 1, D))


def _sc_gather_rows(x, idx):
    M = idx.shape[0]
    D = x.shape[1]
    W = SC_GATHER_WINDOW
    mesh = plsc.VectorSubcoreMesh(core_axis_name="core", subcore_axis_name="subcore")
    n_workers = mesh.num_cores * mesh.num_subcores
    rows_per = M // n_workers
    assert rows_per * n_workers == M and rows_per % W == 0

    @pl.kernel(out_type=jax.ShapeDtypeStruct((M, D), x.dtype), mesh=mesh, name="sc_gather_rows",
               scratch_types=[pltpu.VMEM((rows_per,), jnp.int32), pltpu.VMEM((W, D), x.dtype)])
    def gather_kernel(x_hbm, i_hbm, o_hbm, idx_vmem, buf):
        wid = lax.axis_index("core") * mesh.num_subcores + lax.axis_index("subcore")
        base = wid * rows_per
        pltpu.sync_copy(i_hbm.at[pl.ds(base, rows_per)], idx_vmem)

        @pl.loop(0, rows_per // W)
        def _(j):
            pltpu.sync_copy(x_hbm.at[idx_vmem.at[pl.ds(j * W, W)]], buf)
            pltpu.sync_copy(buf, o_hbm.at[pl.ds(base + j * W, W)])

    return gather_kernel(x, idx)


def _sc_scatter_rows(x, idx, n_out):
    K, N = idx.shape
    D = x.shape[1]
    W = SC_GATHER_WINDOW
    mesh = plsc.VectorSubcoreMesh(core_axis_name="core", subcore_axis_name="subcore")
    n_workers = mesh.num_cores * mesh.num_subcores
    rows_per = N // n_workers
    assert rows_per * n_workers == N and rows_per % W == 0

    @pl.kernel(out_type=jax.ShapeDtypeStruct((n_out, D), x.dtype), mesh=mesh, name="sc_scatter_rows",
               scratch_types=[pltpu.VMEM((K * rows_per,), jnp.int32), pltpu.VMEM((W, D), x.dtype)])
    def scatter_kernel(x_hbm, i_hbm, o_hbm, idx_vmem, buf):
        wid = lax.axis_index("core") * mesh.num_subcores + lax.axis_index("subcore")
        base = wid * rows_per
        for k in range(K):
            pltpu.sync_copy(i_hbm.at[pl.ds(k * N + base, rows_per)], idx_vmem.at[pl.ds(k * rows_per, rows_per)])

        @pl.loop(0, rows_per // W)
        def _(j):
            pltpu.sync_copy(x_hbm.at[pl.ds(base + j * W, W)], buf)
            for k in range(K):
                pltpu.sync_copy(buf, o_hbm.at[idx_vmem.at[pl.ds(k * rows_per + j * W, W)]])

    return scatter_kernel(x, idx.reshape(K * N))


def _combine_kernel(x_ref, yg_ref, gate_ref, ln3g_ref, ln3b_ref, out_ref):
    x = x_ref[...]
    g = gate_ref[...]
    y = jnp.zeros_like(x)
    for k in range(TOP_K):
        y = y + g[:, k:k + 1] * _unpack_bf16_pairs(yg_ref[k * CMB_TILE:(k + 1) * CMB_TILE, :])
    out_ref[...] = _layer_norm(DEEPNORM_ALPHA * x + y, ln3g_ref[...], ln3b_ref[...])


def _combine(x2, yg, gate, p, l):
    N, D = x2.shape
    T = CMB_TILE
    return pl.pallas_call(
        _combine_kernel,
        out_shape=jax.ShapeDtypeStruct((N, D), F32),
        grid=(N // T,),
        in_specs=[pl.BlockSpec((T, D), lambda i: (i, 0)), pl.BlockSpec((TOP_K * T, D // 2), lambda i: (i, 0)),
                  pl.BlockSpec((T, TOP_K), lambda i: (i, 0)), _layer_spec(p['ln3g'], l), _layer_spec(p['ln3b'], l)],
        out_specs=pl.BlockSpec((T, D), lambda i: (i, 0)),
        compiler_params=pltpu.CompilerParams(dimension_semantics=("arbitrary",), vmem_limit_bytes=VMEM_LIMIT),
        name="combine",
    )(x2, yg, gate, p['ln3g'], p['ln3b'])


def _prep(w_in, b_in, gla_wg2, gla_bg, gla_norm_g, sgu_ln_g, sgu_ln_b, sgu_ws, sgu_bs, pool_w, pool_scale,
          w_up_a, w_up_b, w_up_c, w_o, ln1_g, ln1_b, xa_wq, xa_wk, xa_wv, xa_wo, ln2_g, ln2_b,
          router_w, router_b, ln3_g, ln3_b):
    L = w_in.shape[0]
    o_glow = O_GLOW
    o_uv = o_glow + GLA_RANK
    row = lambda a: a.reshape(L, 1, -1).astype(F32)
    pad_last = lambda a, n: jnp.pad(a, [(0, 0)] * (a.ndim - 1) + [(0, n - a.shape[-1])])
    p = {}
    p['wcat'] = jnp.concatenate([w_in[..., :o_glow], pad_last(w_in[..., o_glow:o_uv], LANES), w_in[..., o_uv:]],
                                axis=-1).astype(BF16)
    p['bcat'] = row(jnp.concatenate([b_in[..., :o_glow], pad_last(b_in[..., o_glow:o_uv], LANES), b_in[..., o_uv:]],
                                    axis=-1))
    p['wg2'] = jnp.pad(gla_wg2, ((0, 0), (0, LANES - GLA_RANK), (0, 0))).astype(BF16)
    p['bg'] = row(gla_bg)
    p['gnorm'] = row(gla_norm_g)
    p['slng'] = row(sgu_ln_g)
    p['slnb'] = row(sgu_ln_b)
    p['wtril'] = jnp.tril(sgu_ws).astype(BF16)
    p['sbias'] = jnp.repeat(jnp.swapaxes(sgu_bs, 1, 2), SGU_GD, axis=2).astype(F32)
    G = len(POOL_WINDOWS)
    eye = jnp.eye(G, dtype=F32)
    p['poolw'] = jnp.einsum('lgcd,gh->lgchd', pool_w, eye).reshape(L, POOL_WIDTH, POOL_WIDTH).astype(BF16)
    p['pscale'] = row(pool_scale)
    p['wup'] = jnp.concatenate([w_up_a, w_up_b, w_up_c], axis=1).astype(BF16)
    p['wo'] = w_o.astype(BF16)
    p['ln1g'], p['ln1b'] = row(ln1_g), row(ln1_b)
    p['wq'] = xa_wq.astype(BF16)
    p['wkt'] = jnp.swapaxes(xa_wk, 1, 2).astype(BF16)
    p['wv'] = xa_wv.astype(BF16)
    p['wxo'] = xa_wo.astype(BF16)
    p['ln2g'], p['ln2b'] = row(ln2_g), row(ln2_b)
    rw = pad_last(router_w, LANES)
    rwhi = rw.astype(BF16)
    p['rwcat'] = jnp.concatenate([rwhi, (rw - rwhi.astype(F32)).astype(BF16)], axis=-1)
    p['rb'] = row(pad_last(router_b, LANES))
    p['ln3g'], p['ln3b'] = row(ln3_g), row(ln3_b)
    return p


def _route(route, counts):
    N = route.shape[0]
    top_idx = route[:, 0:TOP_K].astype(jnp.int32)
    gate = route[:, TOP_K:2 * TOP_K]
    rank = route[:, 2 * TOP_K:3 * TOP_K].astype(jnp.int32)
    counts = counts[0, :N_EXPERTS].astype(jnp.int32)
    padded = ((counts + MOE_BLOCK - 1) // MOE_BLOCK) * MOE_BLOCK
    pad_end = jnp.cumsum(padded)
    pad_start = pad_end - padded
    ids = jnp.arange(N_EXPERTS, dtype=jnp.int32)
    start_of = jnp.sum(jnp.where(top_idx[:, :, None] == ids[None, None, :], pad_start[None, None, :], 0), axis=-1)
    dest = start_of + rank
    n_blocks = N * TOP_K // MOE_BLOCK + N_EXPERTS
    block_start = jnp.arange(n_blocks, dtype=jnp.int32) * MOE_BLOCK
    block_expert = jnp.minimum(jnp.sum((pad_end[None, :] <= block_start[:, None]).astype(jnp.int32), axis=1),
                               N_EXPERTS - 1)
    n_used = (pad_end[-1] // MOE_BLOCK).astype(jnp.int32).reshape(1)
    later = jnp.where((ids[None, :] > ids[:, None]) & (counts[None, :] > 0), ids[None, :], N_EXPERTS)
    first_later = jnp.min(later, axis=1)
    next_expert = jnp.where(first_later < N_EXPERTS, first_later, ids).astype(jnp.int32)
    return gate, dest, block_expert, n_used, next_expert


def kernel(x, mem, w_in, b_in, gla_wg2, gla_bg, gla_norm_g, sgu_ln_g, sgu_ln_b, sgu_ws, sgu_bs, pool_w, pool_scale, w_up_a, w_up_b, w_up_c, w_o, ln1_g, ln1_b, xa_wq, xa_wk, xa_wv, xa_wo, ln2_g, ln2_b, router_w, router_b, exp_w_gu, exp_b_gu, exp_w_down, exp_b_down, ln3_g, ln3_b):
    B, S, D = x.shape
    N = B * S
    p = _prep(w_in, b_in, gla_wg2, gla_bg, gla_norm_g, sgu_ln_g, sgu_ln_b, sgu_ws, sgu_bs, pool_w, pool_scale,
              w_up_a, w_up_b, w_up_c, w_o, ln1_g, ln1_b, xa_wq, xa_wk, xa_wv, xa_wo, ln2_g, ln2_b,
              router_w, router_b, ln3_g, ln3_b)
    memt = jnp.swapaxes(mem, 1, 2)
    for l in range(DEPTH):
        x1 = _mixer(x, p, l)
        kt, v = _memkv(mem, memt, p, l)
        x2, x2p, route, counts = _xattn(x1, kt, v, p, l)
        gate, dest, block_expert, n_used, next_expert = _route(route.reshape(N, LANES), counts)
        n_slots = (N * TOP_K // MOE_BLOCK + N_EXPERTS) * MOE_BLOCK
        xs = _sc_scatter_rows(x2p.reshape(N, D // 2), dest.T, n_slots)
        ys = _experts(block_expert, n_used, next_expert, xs, exp_w_gu, exp_b_gu, exp_w_down, exp_b_down, l)
        dest_km = dest.reshape(N // CMB_TILE, CMB_TILE, TOP_K).transpose(0, 2, 1).reshape(-1)
        yg = _sc_gather_rows(ys, dest_km)
        x = _combine(x2.reshape(N, D), yg, gate, p, l).reshape(B, S, D)
    return x
```

```python
import functools

import jax
import jax.numpy as jnp
from jax import lax
from jax.experimental import pallas as pl
from jax.experimental.pallas import tpu as pltpu
from jax.experimental.pallas import tpu_sc as plsc

F32 = jnp.float32
BF16 = jnp.bfloat16

D_MODEL = 1024
DEPTH = 2
GLA_HEADS = 4
GLA_KEY = 256
GLA_VAL = 512
GLA_DK = 64
GLA_DV = 128
GLA_RANK = 16
GLA_TAU = 16.0
GLA_CHUNK = 64
SGU_GROUPS = 4
SGU_WIDTH = 256
SGU_GD = 64
SGU_CHUNK = 128
POOL_WINDOWS = (2, 4, 8, 16)
POOL_WIDTH = 256
POOL_GD = 64
POOL_CARRY = 32
MEM_LEN = 256
XA_HEADS = 4
XA_DH = 256
N_EXPERTS = 32
TOP_K = 4
EXPERT_FF = 1024
SWIGLU_LIMIT = 7.0
SWIGLU_ALPHA = 1.702
DEEPNORM_ALPHA = (2 * DEPTH) ** 0.25
LN_EPS = 1e-5
LANES = 128
VMEM_LIMIT = 56 * 1024 * 1024

MIX_TILE = 256
XA_TILE = 512
MOE_BLOCK = 256
CMB_TILE = 256
SC_GATHER_WINDOW = 128

O_QKVR = 0
O_GLOW = 2 * GLA_KEY + 2 * GLA_VAL
O_UV = O_GLOW + LANES
O_XC = O_UV + 2 * SGU_WIDTH
O_GATE = O_XC + POOL_WIDTH
N_PACK = O_GATE + 3 * D_MODEL


def _dot(a, b):
    return jnp.dot(a, b, preferred_element_type=F32)


def _dot_t0(a, b):
    return lax.dot_general(a, b, (((0,), (0,)), ((), ())), preferred_element_type=F32)


def _dot_t1(a, b):
    return lax.dot_general(a, b, (((1,), (1,)), ((), ())), preferred_element_type=F32)


def _split_bf16(x):
    hi = x.astype(BF16)
    lo = (x - hi.astype(F32)).astype(BF16)
    return hi, lo


def _layer_norm(x, g, b):
    mu = jnp.mean(x, axis=-1, keepdims=True)
    xc = x - mu
    var = jnp.mean(xc * xc, axis=-1, keepdims=True)
    return xc * lax.rsqrt(var + LN_EPS) * g + b


def _sigmoid(x):
    return 1.0 / (1.0 + jnp.exp(-x))


def _pack_bf16_pairs(x):
    H = x.shape[1] // 2
    bits = lax.bitcast_convert_type(x.astype(BF16).astype(F32), jnp.uint32)
    return (bits[:, :H] >> 16) | (bits[:, H:] & jnp.uint32(0xFFFF0000))


def _unpack_bf16_pairs(w):
    lo = lax.bitcast_convert_type(w << 16, F32)
    hi = lax.bitcast_convert_type(w & jnp.uint32(0xFFFF0000), F32)
    return jnp.concatenate([lo, hi], axis=1)


def _layer_spec(arr, l):
    nd = arr.ndim - 1
    return pl.BlockSpec((None,) + arr.shape[1:], lambda *_: (l,) + (0,) * nd, pipeline_mode=pl.Buffered(1))


def _mixer_kernel(x_ref, wcat_ref, bcat_ref, wg2_ref, bg_ref, gnorm_ref,
                  slng_ref, slnb_ref, wtril_ref, sbias_ref, poolw_ref, pscale_ref,
                  wup_ref, wo_ref, ln1g_ref, ln1b_ref,
                  out_ref,
                  state_ref, qkvr_ref, ya_ref, vln_ref, e_ref, s2_ref, s4_ref, s8_ref):
    T = MIX_TILE
    D = D_MODEL
    j = pl.program_id(1)
    x = x_ref[0]
    xb = x.astype(BF16)

    def proj(lo, hi):
        return _dot(xb, wcat_ref[:, lo:hi]) + bcat_ref[:, lo:hi]

    @pl.when(j == 0)
    def _():
        state_ref[...] = jnp.zeros_like(state_ref)
        e_ref[0:POOL_CARRY, :] = jnp.zeros((POOL_CARRY, POOL_WIDTH), F32)

    @pl.when(j > 0)
    def _():
        e_ref[0:POOL_CARRY, :] = e_ref[T:T + POOL_CARRY, :]

    qkvr_ref[...] = proj(O_QKVR, O_GLOW)
    glow = proj(O_GLOW, O_UV)
    z = _dot(glow.astype(BF16), wg2_ref[...]) + bg_ref[...]
    la = (jnp.minimum(z, 0.0) - jnp.log1p(jnp.exp(-jnp.abs(z)))) * (1.0 / GLA_TAU)
    la_hi, la_lo = _split_bf16(la)
    gate_a = _sigmoid(proj(O_GATE, O_GATE + D))

    C = GLA_CHUNK
    NC = T // C
    CSH = C.bit_length() - 1
    row = lax.broadcasted_iota(jnp.int32, (T, T), 0)
    col = lax.broadcasted_iota(jnp.int32, (T, T), 1)
    same_chunk = (row >> CSH) == (col >> CSH)
    causal = same_chunk & (row >= col)
    causal_bf = jnp.where(causal, 1.0, 0.0).astype(BF16)
    chunk_bf = jnp.where(same_chunk, 1.0, 0.0).astype(BF16)
    b = _dot(causal_bf, la_hi) + _dot(causal_bf, la_lo)
    b_end = _dot(chunk_bf, la_hi) + _dot(chunk_bf, la_lo)
    q = qkvr_ref[:, 0:GLA_KEY]
    k = qkvr_ref[:, GLA_KEY:2 * GLA_KEY]
    v = qkvr_ref[:, 2 * GLA_KEY:2 * GLA_KEY + GLA_VAL].astype(BF16)
    q_dec = q * (GLA_DK ** -0.5) * jnp.exp(b)
    k_dec = (k * jnp.exp(-b)).astype(BF16)
    k_tail = (k * jnp.exp(b_end - b)).astype(BF16)
    q_dec_bf = q_dec.astype(BF16)
    lane = lax.broadcasted_iota(jnp.int32, (T, GLA_KEY), 1)
    o_heads = []
    for h in range(GLA_HEADS):
        q_h = jnp.where((lane >= h * GLA_DK) & (lane < (h + 1) * GLA_DK), q_dec, 0.0).astype(BF16)
        scores = jnp.where(causal, _dot_t1(q_h, k_dec), 0.0).astype(BF16)
        o_heads.append(_dot(scores, v[:, h * GLA_DV:(h + 1) * GLA_DV]))
    o_intra = jnp.concatenate(o_heads, axis=1)
    gate_b = _sigmoid(proj(O_GATE + D, O_GATE + 2 * D))
    ind = jnp.where((lax.broadcasted_iota(jnp.int32, (T, NC * LANES), 0) >> CSH)
                    == (lax.broadcasted_iota(jnp.int32, (T, NC * LANES), 1) >> 7), 1.0, 0.0).astype(BF16)
    dec_all = jnp.exp(_dot_t0(la_hi, ind) + _dot_t0(la_lo, ind))
    o_inter = []
    for c in range(NC):
        rows = slice(c * C, (c + 1) * C)
        o_inter.append(_dot(q_dec_bf[rows], state_ref[...].astype(BF16)))
        kv = _dot_t0(k_tail[rows], v[rows])
        for h in range(GLA_HEADS):
            rs = slice(h * GLA_DK, (h + 1) * GLA_DK)
            cs = slice(h * GLA_DV, (h + 1) * GLA_DV)
            state_ref[rs, cs] = dec_all[rs, c * LANES:(c + 1) * LANES] * state_ref[rs, cs] + kv[rs, cs]
    o = o_intra + jnp.concatenate(o_inter, axis=0)
    for h in range(GLA_HEADS):
        cs = slice(h * GLA_DV, (h + 1) * GLA_DV)
        o_h = o[:, cs]
        ms = jnp.mean(o_h * o_h, axis=-1, keepdims=True)
        o_h = o_h * lax.rsqrt(ms + LN_EPS) * gnorm_ref[:, cs]
        r_h = qkvr_ref[:, 2 * GLA_KEY + GLA_VAL + h * GLA_DV:2 * GLA_KEY + GLA_VAL + (h + 1) * GLA_DV]
        ya_ref[:, cs] = (o_h * (r_h * _sigmoid(r_h))).astype(BF16)

    gate_c = _sigmoid(proj(O_GATE + 2 * D, O_GATE + 3 * D))
    uv = proj(O_UV, O_XC)
    zg = 0.5 * uv * (1.0 + lax.erf(uv * (2.0 ** -0.5)))
    u = zg[:, :SGU_WIDTH]
    vln_ref[...] = _layer_norm(zg[:, SGU_WIDTH:], slng_ref[...], slnb_ref[...])
    lane_s = lax.broadcasted_iota(jnp.int32, (SGU_CHUNK, SGU_WIDTH), 1)
    s_parts = []
    for n in range(T // SGU_CHUNK):
        vc = vln_ref[n * SGU_CHUNK:(n + 1) * SGU_CHUNK, :]
        s = sbias_ref[...]
        for g in range(SGU_GROUPS):
            vg = jnp.where((lane_s >= g * SGU_GD) & (lane_s < (g + 1) * SGU_GD), vc, 0.0).astype(BF16)
            s = s + _dot(wtril_ref[g], vg)
        s_parts.append(s)
    y_b = (u * jnp.concatenate(s_parts, axis=0)).astype(BF16)

    P = POOL_CARRY
    xc = proj(O_XC, O_GATE)
    e_ref[P:P + T, :] = xc
    s2_ref[8:P + T, :] = e_ref[8:P + T, :] + e_ref[7:P + T - 1, :]
    s4_ref[16:P + T, :] = s2_ref[16:P + T, :] + s2_ref[14:P + T - 2, :]
    s8_ref[24:P + T, :] = s4_ref[24:P + T, :] + s4_ref[20:P + T - 4, :]
    s16 = s8_ref[P:P + T, :] + s8_ref[P - 8:P + T - 8, :]
    lane_p = lax.broadcasted_iota(jnp.int32, (T, POOL_WIDTH), 1)
    tpos = lax.broadcasted_iota(jnp.int32, (T, POOL_WIDTH), 0) + (j * T + 1)
    grp = lane_p >> 6
    win = jnp.where(grp == 0, POOL_WINDOWS[0], jnp.where(grp == 1, POOL_WINDOWS[1],
                    jnp.where(grp == 2, POOL_WINDOWS[2], POOL_WINDOWS[3])))
    wsum = jnp.where(grp == 0, s2_ref[P:P + T, :], jnp.where(grp == 1, s4_ref[P:P + T, :],
                     jnp.where(grp == 2, s8_ref[P:P + T, :], s16)))
    count = jnp.minimum(tpos, win).astype(F32)
    pooled = wsum / count - xc
    y_c = (_dot(pooled.astype(BF16), poolw_ref[...]) * pscale_ref[...]).astype(BF16)

    ra, rb = GLA_VAL, GLA_VAL + SGU_WIDTH
    merged = gate_a * _dot(ya_ref[...], wup_ref[0:ra, :])
    merged += gate_b * _dot(y_b, wup_ref[ra:rb, :])
    merged += gate_c * _dot(y_c, wup_ref[rb:, :])
    h = _dot(merged.astype(BF16), wo_ref[...])
    out_ref[0] = _layer_norm(DEEPNORM_ALPHA * x + h, ln1g_ref[...], ln1b_ref[...])


_MIXER_WEIGHTS = ('wcat', 'bcat', 'wg2', 'bg', 'gnorm', 'slng', 'slnb', 'wtril', 'sbias', 'poolw', 'pscale',
                  'wup', 'wo', 'ln1g', 'ln1b')


def _mixer(x, p, l):
    B, S, D = x.shape
    T = MIX_TILE
    weights = [p[n] for n in _MIXER_WEIGHTS]
    return pl.pallas_call(
        _mixer_kernel,
        out_shape=jax.ShapeDtypeStruct((B, S, D), F32),
        grid=(B, S // T),
        in_specs=[pl.BlockSpec((1, T, D), lambda b, j: (b, j, 0))] + [_layer_spec(w, l) for w in weights],
        out_specs=pl.BlockSpec((1, T, D), lambda b, j: (b, j, 0)),
        scratch_shapes=[
            pltpu.VMEM((GLA_KEY, GLA_VAL), F32),
            pltpu.VMEM((T, 2 * GLA_KEY + 2 * GLA_VAL), F32),
            pltpu.VMEM((T, GLA_VAL), BF16),
            pltpu.VMEM((T, SGU_WIDTH), F32),
            pltpu.VMEM((T + POOL_CARRY, POOL_WIDTH), F32),
            pltpu.VMEM((T + POOL_CARRY, POOL_WIDTH), F32),
            pltpu.VMEM((T + POOL_CARRY, POOL_WIDTH), F32),
            pltpu.VMEM((T + POOL_CARRY, POOL_WIDTH), F32),
        ],
        compiler_params=pltpu.CompilerParams(dimension_semantics=("arbitrary", "arbitrary"),
                                             vmem_limit_bytes=VMEM_LIMIT),
        name="mixer",
    )(x, *weights)


def _memkv_kernel(memt_ref, mem_ref, wkt_ref, wv_ref, kt_ref, v_ref):
    kt_ref[0] = _dot(wkt_ref[...], memt_ref[0].astype(BF16)).astype(BF16)
    v_ref[0] = _dot(mem_ref[0].astype(BF16), wv_ref[...]).astype(BF16)


def _memkv(mem, memt, p, l):
    B, M, D = mem.shape
    return pl.pallas_call(
        _memkv_kernel,
        out_shape=(jax.ShapeDtypeStruct((B, D, M), BF16), jax.ShapeDtypeStruct((B, M, D), BF16)),
        grid=(B,),
        in_specs=[pl.BlockSpec((1, D, M), lambda b: (b, 0, 0)), pl.BlockSpec((1, M, D), lambda b: (b, 0, 0)),
                  _layer_spec(p['wkt'], l), _layer_spec(p['wv'], l)],
        out_specs=(pl.BlockSpec((1, D, M), lambda b: (b, 0, 0)), pl.BlockSpec((1, M, D), lambda b: (b, 0, 0))),
        compiler_params=pltpu.CompilerParams(dimension_semantics=("arbitrary",), vmem_limit_bytes=VMEM_LIMIT),
        name="memkv",
    )(memt, mem, p['wkt'], p['wv'])


def _xattn_kernel(x_ref, kt_ref, v_ref, wq_ref, wo_ref, ln2g_ref, ln2b_ref, rwcat_ref, rb_ref,
                  x2_ref, x2p_ref, route_ref, counts_ref, carry_ref):
    T = XA_TILE

    @pl.when((pl.program_id(0) == 0) & (pl.program_id(1) == 0))
    def _():
        carry_ref[...] = jnp.zeros_like(carry_ref)

    x = x_ref[0]
    q = (_dot(x.astype(BF16), wq_ref[...]) * (XA_DH ** -0.5)).astype(BF16)
    h = jnp.zeros_like(x)
    for hd in range(XA_HEADS):
        cs = slice(hd * XA_DH, (hd + 1) * XA_DH)
        s = _dot(q[:, cs], kt_ref[0, cs, :])
        e = jnp.exp(s - jnp.max(s, axis=-1, keepdims=True))
        o = _dot(e.astype(BF16), v_ref[0, :, cs]) / jnp.sum(e, axis=-1, keepdims=True)
        h = h + _dot(o.astype(BF16), wo_ref[cs, :])
    x2 = _layer_norm(DEEPNORM_ALPHA * x + h, ln2g_ref[...], ln2b_ref[...])
    x2_ref[0] = x2
    x2p_ref[0] = _pack_bf16_pairs(x2)

    hi, lo = _split_bf16(x2)
    hw = _dot(hi, rwcat_ref[...])
    logits = hw[:, :LANES] + (hw[:, LANES:] + _dot(lo, rwcat_ref[:, :LANES])) + rb_ref[...]

    lane = lax.broadcasted_iota(jnp.int32, (T, LANES), 1)
    neg_inf = jnp.float32(-jnp.inf)
    rest = jnp.where(lane < N_EXPERTS, logits, neg_inf)
    tops, picks = [], []
    for _ in range(TOP_K):
        m = jnp.max(rest, axis=-1, keepdims=True)
        idx = jnp.min(jnp.where(rest == m, lane, LANES), axis=-1, keepdims=True)
        pick = lane == idx
        rest = jnp.where(pick, neg_inf, rest)
        tops.append((m, idx))
        picks.append(pick)
    exps = [jnp.exp(m - tops[0][0]) for m, _ in tops]
    denom = exps[0]
    for e in exps[1:]:
        denom = denom + e

    chosen = jnp.zeros((T, LANES), F32)
    for pick in picks:
        chosen = chosen + jnp.where(pick, 1.0, 0.0)
    strict_lower = (lax.broadcasted_iota(jnp.int32, (T, T), 0) > lax.broadcasted_iota(jnp.int32, (T, T), 1))
    before = _dot(jnp.where(strict_lower, 1.0, 0.0).astype(BF16), chosen.astype(BF16)) + carry_ref[...]
    carry_ref[...] = carry_ref[...] + jnp.sum(chosen, axis=0, keepdims=True)
    counts_ref[...] = carry_ref[...]

    route = jnp.zeros((T, LANES), F32)
    for k in range(TOP_K):
        rank = jnp.sum(jnp.where(picks[k], before, 0.0), axis=-1, keepdims=True)
        route = jnp.where(lane == k, tops[k][1].astype(F32), route)
        route = jnp.where(lane == TOP_K + k, exps[k] / denom, route)
        route = jnp.where(lane == 2 * TOP_K + k, rank, route)
    route_ref[0] = route


def _xattn(x, kt, v, p, l):
    B, S, D = x.shape
    T = XA_TILE
    M = MEM_LEN
    weights = [p[n] for n in ('wq', 'wxo', 'ln2g', 'ln2b', 'rwcat', 'rb')]
    return pl.pallas_call(
        _xattn_kernel,
        out_shape=(jax.ShapeDtypeStruct((B, S, D), F32), jax.ShapeDtypeStruct((B, S, D // 2), jnp.uint32),
                   jax.ShapeDtypeStruct((B, S, LANES), F32), jax.ShapeDtypeStruct((1, LANES), F32)),
        grid=(B, S // T),
        in_specs=[pl.BlockSpec((1, T, D), lambda b, j: (b, j, 0)),
                  pl.BlockSpec((1, D, M), lambda b, j: (b, 0, 0)),
                  pl.BlockSpec((1, M, D), lambda b, j: (b, 0, 0))] + [_layer_spec(w, l) for w in weights],
        out_specs=(pl.BlockSpec((1, T, D), lambda b, j: (b, j, 0)),
                   pl.BlockSpec((1, T, D // 2), lambda b, j: (b, j, 0)),
                   pl.BlockSpec((1, T, LANES), lambda b, j: (b, j, 0)),
                   pl.BlockSpec((1, LANES), lambda b, j: (0, 0))),
        scratch_shapes=[pltpu.VMEM((1, LANES), F32)],
        compiler_params=pltpu.CompilerParams(dimension_semantics=("arbitrary", "arbitrary"),
                                             vmem_limit_bytes=VMEM_LIMIT),
        name="xattn",
    )(x, kt, v, *weights)


def _expert_kernel(layer, be_ref, nb_ref, nxt_ref, xs_ref, wgu_hbm, bgu_ref, wd_hbm, bd_ref, out_ref,
                   wgu_st, wd_st, wgu_bf, wd_bf, sem):
    i = pl.program_id(0)
    F = EXPERT_FF

    def weight_copies(e):
        return (pltpu.make_async_copy(wgu_hbm.at[layer, e], wgu_st, sem.at[0]),
                pltpu.make_async_copy(wd_hbm.at[layer, e], wd_st, sem.at[1]))

    @pl.when(i < nb_ref[0])
    def _():
        e = be_ref[i]
        prev = be_ref[jnp.maximum(i - 1, 0)]

        @pl.when(i == 0)
        def _():
            for cp in weight_copies(e):
                cp.start()

        @pl.when((i == 0) | (e != prev))
        def _():
            for cp in weight_copies(e):
                cp.wait()
            wgu_bf[...] = wgu_st[...].astype(BF16)
            wd_bf[...] = wd_st[...].astype(BF16)
            nxt = nxt_ref[e]

            @pl.when(nxt != e)
            def _():
                for cp in weight_copies(nxt):
                    cp.start()

        hh = _dot(_unpack_bf16_pairs(xs_ref[...]).astype(BF16), wgu_bf[...]) + bgu_ref[...]
        h_glu = jnp.minimum(hh[:, :F], SWIGLU_LIMIT)
        h_lin = jnp.clip(hh[:, F:], -SWIGLU_LIMIT, SWIGLU_LIMIT)
        a = h_glu * _sigmoid(SWIGLU_ALPHA * h_glu) * (h_lin + 1.0)
        out_ref[...] = _pack_bf16_pairs(_dot(a.astype(BF16), wd_bf[...]) + bd_ref[...])


def _experts(block_expert, n_used, next_expert, xs, w_gu, b_gu, w_down, b_down, l):
    P, DH = xs.shape
    D = 2 * DH
    NB = P // MOE_BLOCK
    F2 = 2 * EXPERT_FF

    def row_map(i, be, nb, nxt):
        return (jnp.minimum(i, nb[0] - 1), 0)

    def exp_map(i, be, nb, nxt):
        return (l, be[jnp.minimum(i, nb[0] - 1)], 0, 0)

    grid_spec = pltpu.PrefetchScalarGridSpec(
        num_scalar_prefetch=3,
        grid=(NB,),
        in_specs=[pl.BlockSpec((MOE_BLOCK, DH), row_map),
                  pl.BlockSpec(memory_space=pl.ANY),
                  pl.BlockSpec((None, None, 1, F2), exp_map),
                  pl.BlockSpec(memory_space=pl.ANY),
                  pl.BlockSpec((None, None, 1, D), exp_map)],
        out_specs=pl.BlockSpec((MOE_BLOCK, DH), row_map),
        scratch_shapes=[pltpu.VMEM((D, F2), F32), pltpu.VMEM((EXPERT_FF, D), F32),
                        pltpu.VMEM((D, F2), BF16), pltpu.VMEM((EXPERT_FF, D), BF16),
                        pltpu.SemaphoreType.DMA((2,))],
    )
    return pl.pallas_call(
        functools.partial(_expert_kernel, l),
        out_shape=jax.ShapeDtypeStruct((P, DH), jnp.uint32),
        grid_spec=grid_spec,
        compiler_params=pltpu.CompilerParams(dimension_semantics=("arbitrary",), vmem_limit_bytes=VMEM_LIMIT),
        name="experts",
    )(block_expert, n_used, next_expert, xs, w_gu, b_gu.reshape(DEPTH, N_EXPERTS, 1, F2), w_down,
      b_down.reshape(DEPTH, N_EXPERTS, 1, D))


def _sc_gather_rows(x, idx):
    M = idx.shape[0]
    D = x.shape[1]
    W = SC_GATHER_WINDOW
    mesh = plsc.VectorSubcoreMesh(core_axis_name="core", subcore_axis_name="subcore")
    n_workers = mesh.num_cores * mesh.num_subcores
    rows_per = M // n_workers
    assert rows_per * n_workers == M and rows_per % W == 0

    @pl.kernel(out_type=jax.ShapeDtypeStruct((M, D), x.dtype), mesh=mesh, name="sc_gather_rows",
               scratch_types=[pltpu.VMEM((rows_per,), jnp.int32), pltpu.VMEM((W, D), x.dtype)])
    def gather_kernel(x_hbm, i_hbm, o_hbm, idx_vmem, buf):
        wid = lax.axis_index("core") * mesh.num_subcores + lax.axis_index("subcore")
        base = wid * rows_per
        pltpu.sync_copy(i_hbm.at[pl.ds(base, rows_per)], idx_vmem)

        @pl.loop(0, rows_per // W)
        def _(j):
            pltpu.sync_copy(x_hbm.at[idx_vmem.at[pl.ds(j * W, W)]], buf)
            pltpu.sync_copy(buf, o_hbm.at[pl.ds(base + j * W, W)])

    return gather_kernel(x, idx)


def _sc_scatter_rows(x, idx, n_out):
    K, N = idx.shape
    D = x.shape[1]
    W = SC_GATHER_WINDOW
    mesh = plsc.VectorSubcoreMesh(core_axis_name="core", subcore_axis_name="subcore")
    n_workers = mesh.num_cores * mesh.num_subcores
    rows_per = N // n_workers
    assert rows_per * n_workers == N and rows_per % W == 0

    @pl.kernel(out_type=jax.ShapeDtypeStruct((n_out, D), x.dtype), mesh=mesh, name="sc_scatter_rows",
               scratch_types=[pltpu.VMEM((K * rows_per,), jnp.int32), pltpu.VMEM((W, D), x.dtype)])
    def scatter_kernel(x_hbm, i_hbm, o_hbm, idx_vmem, buf):
        wid = lax.axis_index("core") * mesh.num_subcores + lax.axis_index("subcore")
        base = wid * rows_per
        for k in range(K):
            pltpu.sync_copy(i_hbm.at[pl.ds(k * N + base, rows_per)], idx_vmem.at[pl.ds(k * rows_per, rows_per)])

        @pl.loop(0, rows_per // W)
        def _(j):
            pltpu.sync_copy(x_hbm.at[pl.ds(base + j * W, W)], buf)
            for k in range(K):
                pltpu.sync_copy(buf, o_hbm.at[idx_vmem.at[pl.ds(k * rows_per + j * W, W)]])

    return scatter_kernel(x, idx.reshape(K * N))


def _combine_kernel(x_ref, yg_ref, gate_ref, ln3g_ref, ln3b_ref, out_ref):
    x = x_ref[...]
    g = gate_ref[...]
    y = jnp.zeros_like(x)
    for k in range(TOP_K):
        y = y + g[:, k:k + 1] * _unpack_bf16_pairs(yg_ref[k * CMB_TILE:(k + 1) * CMB_TILE, :])
    out_ref[...] = _layer_norm(DEEPNORM_ALPHA * x + y, ln3g_ref[...], ln3b_ref[...])


def _combine(x2, yg, gate, p, l):
    N, D = x2.shape
    T = CMB_TILE
    return pl.pallas_call(
        _combine_kernel,
        out_shape=jax.ShapeDtypeStruct((N, D), F32),
        grid=(N // T,),
        in_specs=[pl.BlockSpec((T, D), lambda i: (i, 0)), pl.BlockSpec((TOP_K * T, D // 2), lambda i: (i, 0)),
                  pl.BlockSpec((T, TOP_K), lambda i: (i, 0)), _layer_spec(p['ln3g'], l), _layer_spec(p['ln3b'], l)],
        out_specs=pl.BlockSpec((T, D), lambda i: (i, 0)),
        compiler_params=pltpu.CompilerParams(dimension_semantics=("arbitrary",), vmem_limit_bytes=VMEM_LIMIT),
        name="combine",
    )(x2, yg, gate, p['ln3g'], p['ln3b'])


def _prep(w_in, b_in, gla_wg2, gla_bg, gla_norm_g, sgu_ln_g, sgu_ln_b, sgu_ws, sgu_bs, pool_w, pool_scale,
          w_up_a, w_up_b, w_up_c, w_o, ln1_g, ln1_b, xa_wq, xa_wk, xa_wv, xa_wo, ln2_g, ln2_b,
          router_w, router_b, ln3_g, ln3_b):
    L = w_in.shape[0]
    o_glow = O_GLOW
    o_uv = o_glow + GLA_RANK
    row = lambda a: a.reshape(L, 1, -1).astype(F32)
    pad_last = lambda a, n: jnp.pad(a, [(0, 0)] * (a.ndim - 1) + [(0, n - a.shape[-1])])
    p = {}
    p['wcat'] = jnp.concatenate([w_in[..., :o_glow], pad_last(w_in[..., o_glow:o_uv], LANES), w_in[..., o_uv:]],
                                axis=-1).astype(BF16)
    p['bcat'] = row(jnp.concatenate([b_in[..., :o_glow], pad_last(b_in[..., o_glow:o_uv], LANES), b_in[..., o_uv:]],
                                    axis=-1))
    p['wg2'] = jnp.pad(gla_wg2, ((0, 0), (0, LANES - GLA_RANK), (0, 0))).astype(BF16)
    p['bg'] = row(gla_bg)
    p['gnorm'] = row(gla_norm_g)
    p['slng'] = row(sgu_ln_g)
    p['slnb'] = row(sgu_ln_b)
    p['wtril'] = jnp.tril(sgu_ws).astype(BF16)
    p['sbias'] = jnp.repeat(jnp.swapaxes(sgu_bs, 1, 2), SGU_GD, axis=2).astype(F32)
    G = len(POOL_WINDOWS)
    eye = jnp.eye(G, dtype=F32)
    p['poolw'] = jnp.einsum('lgcd,gh->lgchd', pool_w, eye).reshape(L, POOL_WIDTH, POOL_WIDTH).astype(BF16)
    p['pscale'] = row(pool_scale)
    p['wup'] = jnp.concatenate([w_up_a, w_up_b, w_up_c], axis=1).astype(BF16)
    p['wo'] = w_o.astype(BF16)
    p['ln1g'], p['ln1b'] = row(ln1_g), row(ln1_b)
    p['wq'] = xa_wq.astype(BF16)
    p['wkt'] = jnp.swapaxes(xa_wk, 1, 2).astype(BF16)
    p['wv'] = xa_wv.astype(BF16)
    p['wxo'] = xa_wo.astype(BF16)
    p['ln2g'], p['ln2b'] = row(ln2_g), row(ln2_b)
    rw = pad_last(router_w, LANES)
    rwhi = rw.astype(BF16)
    p['rwcat'] = jnp.concatenate([rwhi, (rw - rwhi.astype(F32)).astype(BF16)], axis=-1)
    p['rb'] = row(pad_last(router_b, LANES))
    p['ln3g'], p['ln3b'] = row(ln3_g), row(ln3_b)
    return p


def _route(route, counts):
    N = route.shape[0]
    top_idx = route[:, 0:TOP_K].astype(jnp.int32)
    gate = route[:, TOP_K:2 * TOP_K]
    rank = route[:, 2 * TOP_K:3 * TOP_K].astype(jnp.int32)
    counts = counts[0, :N_EXPERTS].astype(jnp.int32)
    padded = ((counts + MOE_BLOCK - 1) // MOE_BLOCK) * MOE_BLOCK
    pad_end = jnp.cumsum(padded)
    pad_start = pad_end - padded
    ids = jnp.arange(N_EXPERTS, dtype=jnp.int32)
    start_of = jnp.sum(jnp.where(top_idx[:, :, None] == ids[None, None, :], pad_start[None, None, :], 0), axis=-1)
    dest = start_of + rank
    n_blocks = N * TOP_K // MOE_BLOCK + N_EXPERTS
    block_start = jnp.arange(n_blocks, dtype=jnp.int32) * MOE_BLOCK
    block_expert = jnp.minimum(jnp.sum((pad_end[None, :] <= block_start[:, None]).astype(jnp.int32), axis=1),
                               N_EXPERTS - 1)
    n_used = (pad_end[-1] // MOE_BLOCK).astype(jnp.int32).reshape(1)
    later = jnp.where((ids[None, :] > ids[:, None]) & (counts[None, :] > 0), ids[None, :], N_EXPERTS)
    first_later = jnp.min(later, axis=1)
    next_expert = jnp.where(first_later < N_EXPERTS, first_later, ids).astype(jnp.int32)
    return gate, dest, block_expert, n_used, next_expert


def kernel(x, mem, w_in, b_in, gla_wg2, gla_bg, gla_norm_g, sgu_ln_g, sgu_ln_b, sgu_ws, sgu_bs, pool_w, pool_scale, w_up_a, w_up_b, w_up_c, w_o, ln1_g, ln1_b, xa_wq, xa_wk, xa_wv, xa_wo, ln2_g, ln2_b, router_w, router_b, exp_w_gu, exp_b_gu, exp_w_down, exp_b_down, ln3_g, ln3_b):
    B, S, D = x.shape
    N = B * S
    p = _prep(w_in, b_in, gla_wg2, gla_bg, gla_norm_g, sgu_ln_g, sgu_ln_b, sgu_ws, sgu_bs, pool_w, pool_scale,
              w_up_a, w_up_b, w_up_c, w_o, ln1_g, ln1_b, xa_wq, xa_wk, xa_wv, xa_wo, ln2_g, ln2_b,
              router_w, router_b, ln3_g, ln3_b)
    memt = jnp.swapaxes(mem, 1, 2)
    for l in range(DEPTH):
        x1 = _mixer(x, p, l)
        kt, v = _memkv(mem, memt, p, l)
        x2, x2p, route, counts = _xattn(x1, kt, v, p, l)
        gate, dest, block_expert, n_used, next_expert = _route(route.reshape(N, LANES), counts)
        n_slots = (N * TOP_K // MOE_BLOCK + N_EXPERTS) * MOE_BLOCK
        xs = _sc_scatter_rows(x2p.reshape(N, D // 2), dest.T, n_slots)
        ys = _experts(block_expert, n_used, next_expert, xs, exp_w_gu, exp_b_gu, exp_w_down, exp_b_down, l)
        dest_km = dest.reshape(N // CMB_TILE, CMB_TILE, TOP_K).transpose(0, 2, 1).reshape(-1)
        yg = _sc_gather_rows(ys, dest_km)
        x = _combine(x2.reshape(N, D), yg, gate, p, l).reshape(B, S, D)
    return x
```

```python
import functools

import jax
import jax.numpy as jnp
from jax import lax
from jax.experimental import pallas as pl
from jax.experimental.pallas import tpu as pltpu
from jax.experimental.pallas import tpu_sc as plsc

F32 = jnp.float32
BF16 = jnp.bfloat16

D_MODEL = 1024
DEPTH = 2
GLA_HEADS = 4
GLA_KEY = 256
GLA_VAL = 512
GLA_DK = 64
GLA_DV = 128
GLA_RANK = 16
GLA_TAU = 16.0
GLA_CHUNK = 64
SGU_GROUPS = 4
SGU_WIDTH = 256
SGU_GD = 64
SGU_CHUNK = 128
POOL_WINDOWS = (2, 4, 8, 16)
POOL_WIDTH = 256
POOL_GD = 64
POOL_CARRY = 32
MEM_LEN = 256
XA_HEADS = 4
XA_DH = 256
N_EXPERTS = 32
TOP_K = 4
EXPERT_FF = 1024
SWIGLU_LIMIT = 7.0
SWIGLU_ALPHA = 1.702
DEEPNORM_ALPHA = (2 * DEPTH) ** 0.25
LN_EPS = 1e-5
LANES = 128
VMEM_LIMIT = 56 * 1024 * 1024

MIX_TILE = 256
XA_TILE = 512
MOE_BLOCK = 256
CMB_TILE = 256
ROUTE_ROWS = 16
SC_GATHER_WINDOW = 128

O_QKVR = 0
O_GLOW = 2 * GLA_KEY + 2 * GLA_VAL
O_UV = O_GLOW + LANES
O_XC = O_UV + 2 * SGU_WIDTH
O_GATE = O_XC + POOL_WIDTH
N_PACK = O_GATE + 3 * D_MODEL


def _dot(a, b):
    return jnp.dot(a, b, preferred_element_type=F32)


def _dot_t0(a, b):
    return lax.dot_general(a, b, (((0,), (0,)), ((), ())), preferred_element_type=F32)


def _dot_t1(a, b):
    return lax.dot_general(a, b, (((1,), (1,)), ((), ())), preferred_element_type=F32)


def _split_bf16(x):
    hi = x.astype(BF16)
    lo = (x - hi.astype(F32)).astype(BF16)
    return hi, lo


def _layer_norm(x, g, b):
    mu = jnp.mean(x, axis=-1, keepdims=True)
    xc = x - mu
    var = jnp.mean(xc * xc, axis=-1, keepdims=True)
    return xc * lax.rsqrt(var + LN_EPS) * g + b


def _sigmoid(x):
    return 1.0 / (1.0 + jnp.exp(-x))


def _pack_bf16_pairs(x):
    H = x.shape[1] // 2
    bits = lax.bitcast_convert_type(x.astype(BF16).astype(F32), jnp.uint32)
    return (bits[:, :H] >> 16) | (bits[:, H:] & jnp.uint32(0xFFFF0000))


def _unpack_bf16_pairs(w):
    lo = lax.bitcast_convert_type(w << 16, F32)
    hi = lax.bitcast_convert_type(w & jnp.uint32(0xFFFF0000), F32)
    return jnp.concatenate([lo, hi], axis=1)


def _layer_spec(arr, l):
    nd = arr.ndim - 1
    return pl.BlockSpec((None,) + arr.shape[1:], lambda *_: (l,) + (0,) * nd, pipeline_mode=pl.Buffered(1))


def _mixer_kernel(x_ref, wcat_ref, bcat_ref, wg2_ref, bg_ref, gnorm_ref,
                  slng_ref, slnb_ref, wtril_ref, sbias_ref, poolw_ref, pscale_ref,
                  wup_ref, wo_ref, ln1g_ref, ln1b_ref,
                  out_ref,
                  state_ref, qkvr_ref, ya_ref, vln_ref, e_ref, s2_ref, s4_ref, s8_ref):
    T = MIX_TILE
    D = D_MODEL
    j = pl.program_id(1)
    x = x_ref[0]
    xb = x.astype(BF16)

    def proj(lo, hi):
        return _dot(xb, wcat_ref[:, lo:hi]) + bcat_ref[:, lo:hi]

    @pl.when(j == 0)
    def _():
        state_ref[...] = jnp.zeros_like(state_ref)
        e_ref[0:POOL_CARRY, :] = jnp.zeros((POOL_CARRY, POOL_WIDTH), F32)

    @pl.when(j > 0)
    def _():
        e_ref[0:POOL_CARRY, :] = e_ref[T:T + POOL_CARRY, :]

    qkvr_ref[...] = proj(O_QKVR, O_GLOW)
    glow = proj(O_GLOW, O_UV)
    z = _dot(glow.astype(BF16), wg2_ref[...]) + bg_ref[...]
    la = (jnp.minimum(z, 0.0) - jnp.log1p(jnp.exp(-jnp.abs(z)))) * (1.0 / GLA_TAU)
    la_hi, la_lo = _split_bf16(la)
    gate_a = _sigmoid(proj(O_GATE, O_GATE + D))

    C = GLA_CHUNK
    NC = T // C
    CSH = C.bit_length() - 1
    row = lax.broadcasted_iota(jnp.int32, (T, T), 0)
    col = lax.broadcasted_iota(jnp.int32, (T, T), 1)
    same_chunk = (row >> CSH) == (col >> CSH)
    causal = same_chunk & (row >= col)
    causal_bf = jnp.where(causal, 1.0, 0.0).astype(BF16)
    chunk_bf = jnp.where(same_chunk, 1.0, 0.0).astype(BF16)
    b = _dot(causal_bf, la_hi) + _dot(causal_bf, la_lo)
    b_end = _dot(chunk_bf, la_hi) + _dot(chunk_bf, la_lo)
    q = qkvr_ref[:, 0:GLA_KEY]
    k = qkvr_ref[:, GLA_KEY:2 * GLA_KEY]
    v = qkvr_ref[:, 2 * GLA_KEY:2 * GLA_KEY + GLA_VAL].astype(BF16)
    q_dec = q * (GLA_DK ** -0.5) * jnp.exp(b)
    k_dec = (k * jnp.exp(-b)).astype(BF16)
    k_tail = (k * jnp.exp(b_end - b)).astype(BF16)
    q_dec_bf = q_dec.astype(BF16)
    lane = lax.broadcasted_iota(jnp.int32, (T, GLA_KEY), 1)
    o_heads = []
    for h in range(GLA_HEADS):
        q_h = jnp.where((lane >= h * GLA_DK) & (lane < (h + 1) * GLA_DK), q_dec, 0.0).astype(BF16)
        scores = jnp.where(causal, _dot_t1(q_h, k_dec), 0.0).astype(BF16)
        o_heads.append(_dot(scores, v[:, h * GLA_DV:(h + 1) * GLA_DV]))
    o_intra = jnp.concatenate(o_heads, axis=1)
    gate_b = _sigmoid(proj(O_GATE + D, O_GATE + 2 * D))
    ind = jnp.where((lax.broadcasted_iota(jnp.int32, (T, NC * LANES), 0) >> CSH)
                    == (lax.broadcasted_iota(jnp.int32, (T, NC * LANES), 1) >> 7), 1.0, 0.0).astype(BF16)
    dec_all = jnp.exp(_dot_t0(la_hi, ind) + _dot_t0(la_lo, ind))
    o_inter = []
    for c in range(NC):
        rows = slice(c * C, (c + 1) * C)
        o_inter.append(_dot(q_dec_bf[rows], state_ref[...].astype(BF16)))
        kv = _dot_t0(k_tail[rows], v[rows])
        for h in range(GLA_HEADS):
            rs = slice(h * GLA_DK, (h + 1) * GLA_DK)
            cs = slice(h * GLA_DV, (h + 1) * GLA_DV)
            state_ref[rs, cs] = dec_all[rs, c * LANES:(c + 1) * LANES] * state_ref[rs, cs] + kv[rs, cs]
    o = o_intra + jnp.concatenate(o_inter, axis=0)
    for h in range(GLA_HEADS):
        cs = slice(h * GLA_DV, (h + 1) * GLA_DV)
        o_h = o[:, cs]
        ms = jnp.mean(o_h * o_h, axis=-1, keepdims=True)
        o_h = o_h * lax.rsqrt(ms + LN_EPS) * gnorm_ref[:, cs]
        r_h = qkvr_ref[:, 2 * GLA_KEY + GLA_VAL + h * GLA_DV:2 * GLA_KEY + GLA_VAL + (h + 1) * GLA_DV]
        ya_ref[:, cs] = (o_h * (r_h * _sigmoid(r_h))).astype(BF16)

    gate_c = _sigmoid(proj(O_GATE + 2 * D, O_GATE + 3 * D))
    uv = proj(O_UV, O_XC)
    zg = 0.5 * uv * (1.0 + lax.erf(uv * (2.0 ** -0.5)))
    u = zg[:, :SGU_WIDTH]
    vln_ref[...] = _layer_norm(zg[:, SGU_WIDTH:], slng_ref[...], slnb_ref[...])
    lane_s = lax.broadcasted_iota(jnp.int32, (SGU_CHUNK, SGU_WIDTH), 1)
    s_parts = []
    for n in range(T // SGU_CHUNK):
        vc = vln_ref[n * SGU_CHUNK:(n + 1) * SGU_CHUNK, :]
        s = sbias_ref[...]
        for g in range(SGU_GROUPS):
            vg = jnp.where((lane_s >= g * SGU_GD) & (lane_s < (g + 1) * SGU_GD), vc, 0.0).astype(BF16)
            s = s + _dot(wtril_ref[g], vg)
        s_parts.append(s)
    y_b = (u * jnp.concatenate(s_parts, axis=0)).astype(BF16)

    P = POOL_CARRY
    xc = proj(O_XC, O_GATE)
    e_ref[P:P + T, :] = xc
    s2_ref[8:P + T, :] = e_ref[8:P + T, :] + e_ref[7:P + T - 1, :]
    s4_ref[16:P + T, :] = s2_ref[16:P + T, :] + s2_ref[14:P + T - 2, :]
    s8_ref[24:P + T, :] = s4_ref[24:P + T, :] + s4_ref[20:P + T - 4, :]
    s16 = s8_ref[P:P + T, :] + s8_ref[P - 8:P + T - 8, :]
    lane_p = lax.broadcasted_iota(jnp.int32, (T, POOL_WIDTH), 1)
    tpos = lax.broadcasted_iota(jnp.int32, (T, POOL_WIDTH), 0) + (j * T + 1)
    grp = lane_p >> 6
    win = jnp.where(grp == 0, POOL_WINDOWS[0], jnp.where(grp == 1, POOL_WINDOWS[1],
                    jnp.where(grp == 2, POOL_WINDOWS[2], POOL_WINDOWS[3])))
    wsum = jnp.where(grp == 0, s2_ref[P:P + T, :], jnp.where(grp == 1, s4_ref[P:P + T, :],
                     jnp.where(grp == 2, s8_ref[P:P + T, :], s16)))
    count = jnp.minimum(tpos, win).astype(F32)
    pooled = wsum / count - xc
    y_c = (_dot(pooled.astype(BF16), poolw_ref[...]) * pscale_ref[...]).astype(BF16)

    ra, rb = GLA_VAL, GLA_VAL + SGU_WIDTH
    merged = gate_a * _dot(ya_ref[...], wup_ref[0:ra, :])
    merged += gate_b * _dot(y_b, wup_ref[ra:rb, :])
    merged += gate_c * _dot(y_c, wup_ref[rb:, :])
    h = _dot(merged.astype(BF16), wo_ref[...])
    out_ref[0] = _layer_norm(DEEPNORM_ALPHA * x + h, ln1g_ref[...], ln1b_ref[...])


_MIXER_WEIGHTS = ('wcat', 'bcat', 'wg2', 'bg', 'gnorm', 'slng', 'slnb', 'wtril', 'sbias', 'poolw', 'pscale',
                  'wup', 'wo', 'ln1g', 'ln1b')


def _mixer(x, p, l):
    B, S, D = x.shape
    T = MIX_TILE
    weights = [p[n] for n in _MIXER_WEIGHTS]
    return pl.pallas_call(
        _mixer_kernel,
        out_shape=jax.ShapeDtypeStruct((B, S, D), F32),
        grid=(B, S // T),
        in_specs=[pl.BlockSpec((1, T, D), lambda b, j: (b, j, 0))] + [_layer_spec(w, l) for w in weights],
        out_specs=pl.BlockSpec((1, T, D), lambda b, j: (b, j, 0)),
        scratch_shapes=[
            pltpu.VMEM((GLA_KEY, GLA_VAL), F32),
            pltpu.VMEM((T, 2 * GLA_KEY + 2 * GLA_VAL), F32),
            pltpu.VMEM((T, GLA_VAL), BF16),
            pltpu.VMEM((T, SGU_WIDTH), F32),
            pltpu.VMEM((T + POOL_CARRY, POOL_WIDTH), F32),
            pltpu.VMEM((T + POOL_CARRY, POOL_WIDTH), F32),
            pltpu.VMEM((T + POOL_CARRY, POOL_WIDTH), F32),
            pltpu.VMEM((T + POOL_CARRY, POOL_WIDTH), F32),
        ],
        compiler_params=pltpu.CompilerParams(dimension_semantics=("arbitrary", "arbitrary"),
                                             vmem_limit_bytes=VMEM_LIMIT),
        name="mixer",
    )(x, *weights)


def _memkv_kernel(memt_ref, mem_ref, wkt_ref, wv_ref, kt_ref, v_ref):
    kt_ref[0] = _dot(wkt_ref[...], memt_ref[0].astype(BF16)).astype(BF16)
    v_ref[0] = _dot(mem_ref[0].astype(BF16), wv_ref[...]).astype(BF16)


def _memkv(mem, memt, p, l):
    B, M, D = mem.shape
    return pl.pallas_call(
        _memkv_kernel,
        out_shape=(jax.ShapeDtypeStruct((B, D, M), BF16), jax.ShapeDtypeStruct((B, M, D), BF16)),
        grid=(B,),
        in_specs=[pl.BlockSpec((1, D, M), lambda b: (b, 0, 0)), pl.BlockSpec((1, M, D), lambda b: (b, 0, 0)),
                  _layer_spec(p['wkt'], l), _layer_spec(p['wv'], l)],
        out_specs=(pl.BlockSpec((1, D, M), lambda b: (b, 0, 0)), pl.BlockSpec((1, M, D), lambda b: (b, 0, 0))),
        compiler_params=pltpu.CompilerParams(dimension_semantics=("arbitrary",), vmem_limit_bytes=VMEM_LIMIT),
        name="memkv",
    )(memt, mem, p['wkt'], p['wv'])


def _xattn_kernel(x_ref, kt_ref, v_ref, wq_ref, wo_ref, ln2g_ref, ln2b_ref, rwt_ref, rbt_ref,
                  x2_ref, x2p_ref, route_ref, counts_ref, carry_ref):
    T = XA_TILE

    @pl.when((pl.program_id(0) == 0) & (pl.program_id(1) == 0))
    def _():
        carry_ref[...] = jnp.zeros_like(carry_ref)

    x = x_ref[0]
    q = (_dot(x.astype(BF16), wq_ref[...]) * (XA_DH ** -0.5)).astype(BF16)
    h = jnp.zeros_like(x)
    for hd in range(XA_HEADS):
        cs = slice(hd * XA_DH, (hd + 1) * XA_DH)
        s = _dot(q[:, cs], kt_ref[0, cs, :])
        e = jnp.exp(s - jnp.max(s, axis=-1, keepdims=True))
        o = _dot(e.astype(BF16), v_ref[0, :, cs]) / jnp.sum(e, axis=-1, keepdims=True)
        h = h + _dot(o.astype(BF16), wo_ref[cs, :])
    x2 = _layer_norm(DEEPNORM_ALPHA * x + h, ln2g_ref[...], ln2b_ref[...])
    x2_ref[0] = x2
    x2p_ref[0] = _pack_bf16_pairs(x2)

    E = N_EXPERTS
    hi, lo = _split_bf16(x2)
    lt = _dot_t1(rwt_ref[...], hi)
    logits = lt[0:E] + (lt[E:2 * E] + _dot_t1(rwt_ref[0:E, :], lo)) + rbt_ref[...]

    eid = lax.broadcasted_iota(jnp.int32, (E, T), 0)
    neg_inf = jnp.float32(-jnp.inf)
    rest = logits
    tops, picks = [], []
    for _ in range(TOP_K):
        m = jnp.max(rest, axis=0, keepdims=True)
        idx = jnp.min(jnp.where(rest == m, eid, E), axis=0, keepdims=True)
        pick = eid == idx
        rest = jnp.where(pick, neg_inf, rest)
        tops.append((m, idx))
        picks.append(pick)
    exps = [jnp.exp(m - tops[0][0]) for m, _ in tops]
    denom = exps[0]
    for e in exps[1:]:
        denom = denom + e

    chosen = jnp.zeros((E, T), F32)
    for pick in picks:
        chosen = chosen + jnp.where(pick, 1.0, 0.0)
    chosen_bf = chosen.astype(BF16)
    earlier = (lax.broadcasted_iota(jnp.int32, (T, T), 0) < lax.broadcasted_iota(jnp.int32, (T, T), 1))
    carry = carry_ref[...]
    before = _dot(chosen_bf, jnp.where(earlier, 1.0, 0.0).astype(BF16)) + jnp.concatenate([carry] * (T // LANES), axis=1)
    carry = carry + _dot(chosen_bf, jnp.ones((T, LANES), BF16))
    carry_ref[...] = carry
    counts_ref[...] = carry

    rid = lax.broadcasted_iota(jnp.int32, (ROUTE_ROWS, T), 0)
    route = jnp.zeros((ROUTE_ROWS, T), F32)
    for k in range(TOP_K):
        rank = jnp.sum(jnp.where(picks[k], before, 0.0), axis=0, keepdims=True)
        route = jnp.where(rid == k, tops[k][1].astype(F32), route)
        route = jnp.where(rid == TOP_K + k, exps[k] / denom, route)
        route = jnp.where(rid == 2 * TOP_K + k, rank, route)
    route_ref[...] = route


def _xattn(x, kt, v, p, l):
    B, S, D = x.shape
    T = XA_TILE
    M = MEM_LEN
    weights = [p[n] for n in ('wq', 'wxo', 'ln2g', 'ln2b', 'rwt', 'rbt')]
    return pl.pallas_call(
        _xattn_kernel,
        out_shape=(jax.ShapeDtypeStruct((B, S, D), F32), jax.ShapeDtypeStruct((B, S, D // 2), jnp.uint32),
                   jax.ShapeDtypeStruct((ROUTE_ROWS, B * S), F32), jax.ShapeDtypeStruct((N_EXPERTS, LANES), F32)),
        grid=(B, S // T),
        in_specs=[pl.BlockSpec((1, T, D), lambda b, j: (b, j, 0)),
                  pl.BlockSpec((1, D, M), lambda b, j: (b, 0, 0)),
                  pl.BlockSpec((1, M, D), lambda b, j: (b, 0, 0))] + [_layer_spec(w, l) for w in weights],
        out_specs=(pl.BlockSpec((1, T, D), lambda b, j: (b, j, 0)),
                   pl.BlockSpec((1, T, D // 2), lambda b, j: (b, j, 0)),
                   pl.BlockSpec((ROUTE_ROWS, T), lambda b, j: (0, b * (S // T) + j)),
                   pl.BlockSpec((N_EXPERTS, LANES), lambda b, j: (0, 0))),
        scratch_shapes=[pltpu.VMEM((N_EXPERTS, LANES), F32)],
        compiler_params=pltpu.CompilerParams(dimension_semantics=("arbitrary", "arbitrary"),
                                             vmem_limit_bytes=VMEM_LIMIT),
        name="xattn",
    )(x, kt, v, *weights)


def _expert_kernel(layer, be_ref, nb_ref, nxt_ref, xs_ref, wgu_hbm, bgu_ref, wd_hbm, bd_ref, out_ref,
                   wgu_st, wd_st, wgu_bf, wd_bf, sem):
    i = pl.program_id(0)
    F = EXPERT_FF

    def weight_copies(e):
        return (pltpu.make_async_copy(wgu_hbm.at[layer, e], wgu_st, sem.at[0]),
                pltpu.make_async_copy(wd_hbm.at[layer, e], wd_st, sem.at[1]))

    @pl.when(i < nb_ref[0])
    def _():
        e = be_ref[i]
        prev = be_ref[jnp.maximum(i - 1, 0)]

        @pl.when(i == 0)
        def _():
            for cp in weight_copies(e):
                cp.start()

        @pl.when((i == 0) | (e != prev))
        def _():
            for cp in weight_copies(e):
                cp.wait()
            wgu_bf[...] = wgu_st[...].astype(BF16)
            wd_bf[...] = wd_st[...].astype(BF16)
            nxt = nxt_ref[e]

            @pl.when(nxt != e)
            def _():
                for cp in weight_copies(nxt):
                    cp.start()

        hh = _dot(_unpack_bf16_pairs(xs_ref[...]).astype(BF16), wgu_bf[...]) + bgu_ref[...]
        h_glu = jnp.minimum(hh[:, :F], SWIGLU_LIMIT)
        h_lin = jnp.clip(hh[:, F:], -SWIGLU_LIMIT, SWIGLU_LIMIT)
        a = h_glu * _sigmoid(SWIGLU_ALPHA * h_glu) * (h_lin + 1.0)
        out_ref[...] = _pack_bf16_pairs(_dot(a.astype(BF16), wd_bf[...]) + bd_ref[...])


def _experts(block_expert, n_used, next_expert, xs, w_gu, b_gu, w_down, b_down, l):
    P, DH = xs.shape
    D = 2 * DH
    NB = P // MOE_BLOCK
    F2 = 2 * EXPERT_FF

    def row_map(i, be, nb, nxt):
        return (jnp.minimum(i, nb[0] - 1), 0)

    def exp_map(i, be, nb, nxt):
        return (l, be[jnp.minimum(i, nb[0] - 1)], 0, 0)

    grid_spec = pltpu.PrefetchScalarGridSpec(
        num_scalar_prefetch=3,
        grid=(NB,),
        in_specs=[pl.BlockSpec((MOE_BLOCK, DH), row_map),
                  pl.BlockSpec(memory_space=pl.ANY),
                  pl.BlockSpec((None, None, 1, F2), exp_map),
                  pl.BlockSpec(memory_space=pl.ANY),
                  pl.BlockSpec((None, None, 1, D), exp_map)],
        out_specs=pl.BlockSpec((MOE_BLOCK, DH), row_map),
        scratch_shapes=[pltpu.VMEM((D, F2), F32), pltpu.VMEM((EXPERT_FF, D), F32),
                        pltpu.VMEM((D, F2), BF16), pltpu.VMEM((EXPERT_FF, D), BF16),
                        pltpu.SemaphoreType.DMA((2,))],
    )
    return pl.pallas_call(
        functools.partial(_expert_kernel, l),
        out_shape=jax.ShapeDtypeStruct((P, DH), jnp.uint32),
        grid_spec=grid_spec,
        compiler_params=pltpu.CompilerParams(dimension_semantics=("arbitrary",), vmem_limit_bytes=VMEM_LIMIT),
        name="experts",
    )(block_expert, n_used, next_expert, xs, w_gu, b_gu.reshape(DEPTH, N_EXPERTS, 1, F2), w_down,
      b_down.reshape(DEPTH, N_EXPERTS, 1, D))


def _sc_gather_rows(x, idx):
    M = idx.shape[0]
    D = x.shape[1]
    W = SC_GATHER_WINDOW
    mesh = plsc.VectorSubcoreMesh(core_axis_name="core", subcore_axis_name="subcore")
    n_workers = mesh.num_cores * mesh.num_subcores
    rows_per = M // n_workers
    assert rows_per * n_workers == M and rows_per % W == 0

    @pl.kernel(out_type=jax.ShapeDtypeStruct((M, D), x.dtype), mesh=mesh, name="sc_gather_rows",
               scratch_types=[pltpu.VMEM((rows_per,), jnp.int32), pltpu.VMEM((W, D), x.dtype)])
    def gather_kernel(x_hbm, i_hbm, o_hbm, idx_vmem, buf):
        wid = lax.axis_index("core") * mesh.num_subcores + lax.axis_index("subcore")
        base = wid * rows_per
        pltpu.sync_copy(i_hbm.at[pl.ds(base, rows_per)], idx_vmem)

        @pl.loop(0, rows_per // W)
        def _(j):
            pltpu.sync_copy(x_hbm.at[idx_vmem.at[pl.ds(j * W, W)]], buf)
            pltpu.sync_copy(buf, o_hbm.at[pl.ds(base + j * W, W)])

    return gather_kernel(x, idx)


def _sc_scatter_rows(x, idx, n_out):
    K, N = idx.shape
    D = x.shape[1]
    W = SC_GATHER_WINDOW
    mesh = plsc.VectorSubcoreMesh(core_axis_name="core", subcore_axis_name="subcore")
    n_workers = mesh.num_cores * mesh.num_subcores
    rows_per = N // n_workers
    assert rows_per * n_workers == N and rows_per % W == 0

    @pl.kernel(out_type=jax.ShapeDtypeStruct((n_out, D), x.dtype), mesh=mesh, name="sc_scatter_rows",
               scratch_types=[pltpu.VMEM((K * rows_per,), jnp.int32), pltpu.VMEM((W, D), x.dtype)])
    def scatter_kernel(x_hbm, i_hbm, o_hbm, idx_vmem, buf):
        wid = lax.axis_index("core") * mesh.num_subcores + lax.axis_index("subcore")
        base = wid * rows_per
        for k in range(K):
            pltpu.sync_copy(i_hbm.at[pl.ds(k * N + base, rows_per)], idx_vmem.at[pl.ds(k * rows_per, rows_per)])

        @pl.loop(0, rows_per // W)
        def _(j):
            pltpu.sync_copy(x_hbm.at[pl.ds(base + j * W, W)], buf)
            for k in range(K):
                pltpu.sync_copy(buf, o_hbm.at[idx_vmem.at[pl.ds(k * rows_per + j * W, W)]])

    return scatter_kernel(x, idx.reshape(K * N))


def _combine_kernel(x_ref, yg_ref, gate_ref, ln3g_ref, ln3b_ref, out_ref):
    x = x_ref[...]
    g = gate_ref[...]
    y = jnp.zeros_like(x)
    for k in range(TOP_K):
        y = y + g[:, k:k + 1] * _unpack_bf16_pairs(yg_ref[k * CMB_TILE:(k + 1) * CMB_TILE, :])
    out_ref[...] = _layer_norm(DEEPNORM_ALPHA * x + y, ln3g_ref[...], ln3b_ref[...])


def _combine(x2, yg, gate, p, l):
    N, D = x2.shape
    T = CMB_TILE
    return pl.pallas_call(
        _combine_kernel,
        out_shape=jax.ShapeDtypeStruct((N, D), F32),
        grid=(N // T,),
        in_specs=[pl.BlockSpec((T, D), lambda i: (i, 0)), pl.BlockSpec((TOP_K * T, D // 2), lambda i: (i, 0)),
                  pl.BlockSpec((T, TOP_K), lambda i: (i, 0)), _layer_spec(p['ln3g'], l), _layer_spec(p['ln3b'], l)],
        out_specs=pl.BlockSpec((T, D), lambda i: (i, 0)),
        compiler_params=pltpu.CompilerParams(dimension_semantics=("arbitrary",), vmem_limit_bytes=VMEM_LIMIT),
        name="combine",
    )(x2, yg, gate, p['ln3g'], p['ln3b'])


def _prep(w_in, b_in, gla_wg2, gla_bg, gla_norm_g, sgu_ln_g, sgu_ln_b, sgu_ws, sgu_bs, pool_w, pool_scale,
          w_up_a, w_up_b, w_up_c, w_o, ln1_g, ln1_b, xa_wq, xa_wk, xa_wv, xa_wo, ln2_g, ln2_b,
          router_w, router_b, ln3_g, ln3_b):
    L = w_in.shape[0]
    o_glow = O_GLOW
    o_uv = o_glow + GLA_RANK
    row = lambda a: a.reshape(L, 1, -1).astype(F32)
    pad_last = lambda a, n: jnp.pad(a, [(0, 0)] * (a.ndim - 1) + [(0, n - a.shape[-1])])
    p = {}
    p['wcat'] = jnp.concatenate([w_in[..., :o_glow], pad_last(w_in[..., o_glow:o_uv], LANES), w_in[..., o_uv:]],
                                axis=-1).astype(BF16)
    p['bcat'] = row(jnp.concatenate([b_in[..., :o_glow], pad_last(b_in[..., o_glow:o_uv], LANES), b_in[..., o_uv:]],
                                    axis=-1))
    p['wg2'] = jnp.pad(gla_wg2, ((0, 0), (0, LANES - GLA_RANK), (0, 0))).astype(BF16)
    p['bg'] = row(gla_bg)
    p['gnorm'] = row(gla_norm_g)
    p['slng'] = row(sgu_ln_g)
    p['slnb'] = row(sgu_ln_b)
    p['wtril'] = jnp.tril(sgu_ws).astype(BF16)
    p['sbias'] = jnp.repeat(jnp.swapaxes(sgu_bs, 1, 2), SGU_GD, axis=2).astype(F32)
    G = len(POOL_WINDOWS)
    eye = jnp.eye(G, dtype=F32)
    p['poolw'] = jnp.einsum('lgcd,gh->lgchd', pool_w, eye).reshape(L, POOL_WIDTH, POOL_WIDTH).astype(BF16)
    p['pscale'] = row(pool_scale)
    p['wup'] = jnp.concatenate([w_up_a, w_up_b, w_up_c], axis=1).astype(BF16)
    p['wo'] = w_o.astype(BF16)
    p['ln1g'], p['ln1b'] = row(ln1_g), row(ln1_b)
    p['wq'] = xa_wq.astype(BF16)
    p['wkt'] = jnp.swapaxes(xa_wk, 1, 2).astype(BF16)
    p['wv'] = xa_wv.astype(BF16)
    p['wxo'] = xa_wo.astype(BF16)
    p['ln2g'], p['ln2b'] = row(ln2_g), row(ln2_b)
    rwt = jnp.swapaxes(router_w, 1, 2)
    rwt_hi = rwt.astype(BF16)
    p['rwt'] = jnp.concatenate([rwt_hi, (rwt - rwt_hi.astype(F32)).astype(BF16)], axis=1)
    p['rbt'] = jnp.broadcast_to(router_b[:, :, None], router_b.shape + (XA_TILE,)).astype(F32)
    p['ln3g'], p['ln3b'] = row(ln3_g), row(ln3_b)
    return p


def _route(route, counts):
    N = route.shape[1]
    top_idx = route[0:TOP_K].astype(jnp.int32)
    gate = route[TOP_K:2 * TOP_K].T
    rank = route[2 * TOP_K:3 * TOP_K].astype(jnp.int32)
    counts = counts[:, 0].astype(jnp.int32)
    padded = ((counts + MOE_BLOCK - 1) // MOE_BLOCK) * MOE_BLOCK
    pad_end = jnp.cumsum(padded)
    pad_start = pad_end - padded
    ids = jnp.arange(N_EXPERTS, dtype=jnp.int32)
    start_of = jnp.sum(jnp.where(top_idx[:, :, None] == ids[None, None, :], pad_start[None, None, :], 0), axis=-1)
    dest = start_of + rank
    n_blocks = N * TOP_K // MOE_BLOCK + N_EXPERTS
    block_start = jnp.arange(n_blocks, dtype=jnp.int32) * MOE_BLOCK
    block_expert = jnp.minimum(jnp.sum((pad_end[None, :] <= block_start[:, None]).astype(jnp.int32), axis=1),
                               N_EXPERTS - 1)
    n_used = (pad_end[-1] // MOE_BLOCK).astype(jnp.int32).reshape(1)
    later = jnp.where((ids[None, :] > ids[:, None]) & (counts[None, :] > 0), ids[None, :], N_EXPERTS)
    first_later = jnp.min(later, axis=1)
    next_expert = jnp.where(first_later < N_EXPERTS, first_later, ids).astype(jnp.int32)
    return gate, dest, block_expert, n_used, next_expert


def kernel(x, mem, w_in, b_in, gla_wg2, gla_bg, gla_norm_g, sgu_ln_g, sgu_ln_b, sgu_ws, sgu_bs, pool_w, pool_scale, w_up_a, w_up_b, w_up_c, w_o, ln1_g, ln1_b, xa_wq, xa_wk, xa_wv, xa_wo, ln2_g, ln2_b, router_w, router_b, exp_w_gu, exp_b_gu, exp_w_down, exp_b_down, ln3_g, ln3_b):
    B, S, D = x.shape
    N = B * S
    p = _prep(w_in, b_in, gla_wg2, gla_bg, gla_norm_g, sgu_ln_g, sgu_ln_b, sgu_ws, sgu_bs, pool_w, pool_scale,
              w_up_a, w_up_b, w_up_c, w_o, ln1_g, ln1_b, xa_wq, xa_wk, xa_wv, xa_wo, ln2_g, ln2_b,
              router_w, router_b, ln3_g, ln3_b)
    memt = jnp.swapaxes(mem, 1, 2)
    for l in range(DEPTH):
        x1 = _mixer(x, p, l)
        kt, v = _memkv(mem, memt, p, l)
        x2, x2p, route, counts = _xattn(x1, kt, v, p, l)
        gate, dest, block_expert, n_used, next_expert = _route(route, counts)
        n_slots = (N * TOP_K // MOE_BLOCK + N_EXPERTS) * MOE_BLOCK
        xs = _sc_scatter_rows(x2p.reshape(N, D // 2), dest, n_slots)
        ys = _experts(block_expert, n_used, next_expert, xs, exp_w_gu, exp_b_gu, exp_w_down, exp_b_down, l)
        dest_km = dest.reshape(TOP_K, N // CMB_TILE, CMB_TILE).transpose(1, 0, 2).reshape(-1)
        yg = _sc_gather_rows(ys, dest_km)
        x = _combine(x2.reshape(N, D), yg, gate, p, l).reshape(B, S, D)
    return x
```

```python
import functools

import jax
import jax.numpy as jnp
from jax import lax
from jax.experimental import pallas as pl
from jax.experimental.pallas import tpu as pltpu
from jax.experimental.pallas import tpu_sc as plsc

F32 = jnp.float32
BF16 = jnp.bfloat16

D_MODEL = 1024
DEPTH = 2
GLA_HEADS = 4
GLA_KEY = 256
GLA_VAL = 512
GLA_DK = 64
GLA_DV = 128
GLA_RANK = 16
GLA_TAU = 16.0
GLA_CHUNK = 64
SGU_GROUPS = 4
SGU_WIDTH = 256
SGU_GD = 64
SGU_CHUNK = 128
POOL_WINDOWS = (2, 4, 8, 16)
POOL_WIDTH = 256
POOL_GD = 64
POOL_CARRY = 32
MEM_LEN = 256
XA_HEADS = 4
XA_DH = 256
N_EXPERTS = 32
TOP_K = 4
EXPERT_FF = 1024
SWIGLU_LIMIT = 7.0
SWIGLU_ALPHA = 1.702
DEEPNORM_ALPHA = (2 * DEPTH) ** 0.25
LN_EPS = 1e-5
LANES = 128
VMEM_LIMIT = 56 * 1024 * 1024

MIX_TILE = 256
XA_TILE = 512
MOE_BLOCK = 256
CMB_TILE = 256
COMBINE_PARTS = 4
ROUTE_ROWS = 16
SC_GATHER_WINDOW = 128

O_QKVR = 0
O_GLOW = 2 * GLA_KEY + 2 * GLA_VAL
O_UV = O_GLOW + LANES
O_XC = O_UV + 2 * SGU_WIDTH
O_GATE = O_XC + POOL_WIDTH
N_PACK = O_GATE + 3 * D_MODEL


def _dot(a, b):
    return jnp.dot(a, b, preferred_element_type=F32)


def _dot_t0(a, b):
    return lax.dot_general(a, b, (((0,), (0,)), ((), ())), preferred_element_type=F32)


def _dot_t1(a, b):
    return lax.dot_general(a, b, (((1,), (1,)), ((), ())), preferred_element_type=F32)


def _split_bf16(x):
    hi = x.astype(BF16)
    lo = (x - hi.astype(F32)).astype(BF16)
    return hi, lo


def _layer_norm(x, g, b):
    mu = jnp.mean(x, axis=-1, keepdims=True)
    xc = x - mu
    var = jnp.mean(xc * xc, axis=-1, keepdims=True)
    return xc * lax.rsqrt(var + LN_EPS) * g + b


def _sigmoid(x):
    return 1.0 / (1.0 + jnp.exp(-x))


def _pack_bf16_pairs(x):
    H = x.shape[1] // 2
    bits = lax.bitcast_convert_type(x.astype(BF16).astype(F32), jnp.uint32)
    return (bits[:, :H] >> 16) | (bits[:, H:] & jnp.uint32(0xFFFF0000))


def _unpack_bf16_pairs(w):
    lo = lax.bitcast_convert_type(w << 16, F32)
    hi = lax.bitcast_convert_type(w & jnp.uint32(0xFFFF0000), F32)
    return jnp.concatenate([lo, hi], axis=1)


def _layer_spec(arr, l):
    nd = arr.ndim - 1
    return pl.BlockSpec((None,) + arr.shape[1:], lambda *_: (l,) + (0,) * nd, pipeline_mode=pl.Buffered(1))


def _mixer_kernel(x_ref, wcat_ref, bcat_ref, wg2_ref, bg_ref, gnorm_ref,
                  slng_ref, slnb_ref, wtril_ref, sbias_ref, poolw_ref, pscale_ref,
                  wup_ref, wo_ref, ln1g_ref, ln1b_ref,
                  out_ref,
                  state_ref, qkvr_ref, ya_ref, vln_ref, e_ref, s2_ref, s4_ref, s8_ref):
    T = MIX_TILE
    D = D_MODEL
    j = pl.program_id(1)
    x = x_ref[0]
    xb = x.astype(BF16)

    def proj(lo, hi):
        return _dot(xb, wcat_ref[:, lo:hi]) + bcat_ref[:, lo:hi]

    @pl.when(j == 0)
    def _():
        state_ref[...] = jnp.zeros_like(state_ref)
        e_ref[0:POOL_CARRY, :] = jnp.zeros((POOL_CARRY, POOL_WIDTH), F32)

    @pl.when(j > 0)
    def _():
        e_ref[0:POOL_CARRY, :] = e_ref[T:T + POOL_CARRY, :]

    qkvr_ref[...] = proj(O_QKVR, O_GLOW)
    glow = proj(O_GLOW, O_UV)
    z = _dot(glow.astype(BF16), wg2_ref[...]) + bg_ref[...]
    la = (jnp.minimum(z, 0.0) - jnp.log1p(jnp.exp(-jnp.abs(z)))) * (1.0 / GLA_TAU)
    la_hi, la_lo = _split_bf16(la)
    gate_a = _sigmoid(proj(O_GATE, O_GATE + D))

    C = GLA_CHUNK
    NC = T // C
    CSH = C.bit_length() - 1
    row = lax.broadcasted_iota(jnp.int32, (T, T), 0)
    col = lax.broadcasted_iota(jnp.int32, (T, T), 1)
    same_chunk = (row >> CSH) == (col >> CSH)
    causal = same_chunk & (row >= col)
    causal_bf = jnp.where(causal, 1.0, 0.0).astype(BF16)
    chunk_bf = jnp.where(same_chunk, 1.0, 0.0).astype(BF16)
    b = _dot(causal_bf, la_hi) + _dot(causal_bf, la_lo)
    b_end = _dot(chunk_bf, la_hi) + _dot(chunk_bf, la_lo)
    q = qkvr_ref[:, 0:GLA_KEY]
    k = qkvr_ref[:, GLA_KEY:2 * GLA_KEY]
    v = qkvr_ref[:, 2 * GLA_KEY:2 * GLA_KEY + GLA_VAL].astype(BF16)
    q_dec = q * (GLA_DK ** -0.5) * jnp.exp(b)
    k_dec = (k * jnp.exp(-b)).astype(BF16)
    k_tail = (k * jnp.exp(b_end - b)).astype(BF16)
    q_dec_bf = q_dec.astype(BF16)
    lane = lax.broadcasted_iota(jnp.int32, (T, GLA_KEY), 1)
    o_heads = []
    for h in range(GLA_HEADS):
        q_h = jnp.where((lane >= h * GLA_DK) & (lane < (h + 1) * GLA_DK), q_dec, 0.0).astype(BF16)
        scores = jnp.where(causal, _dot_t1(q_h, k_dec), 0.0).astype(BF16)
        o_heads.append(_dot(scores, v[:, h * GLA_DV:(h + 1) * GLA_DV]))
    o_intra = jnp.concatenate(o_heads, axis=1)
    gate_b = _sigmoid(proj(O_GATE + D, O_GATE + 2 * D))
    ind = jnp.where((lax.broadcasted_iota(jnp.int32, (T, NC * LANES), 0) >> CSH)
                    == (lax.broadcasted_iota(jnp.int32, (T, NC * LANES), 1) >> 7), 1.0, 0.0).astype(BF16)
    dec_all = jnp.exp(_dot_t0(la_hi, ind) + _dot_t0(la_lo, ind))
    o_inter = []
    for c in range(NC):
        rows = slice(c * C, (c + 1) * C)
        o_inter.append(_dot(q_dec_bf[rows], state_ref[...].astype(BF16)))
        kv = _dot_t0(k_tail[rows], v[rows])
        for h in range(GLA_HEADS):
            rs = slice(h * GLA_DK, (h + 1) * GLA_DK)
            cs = slice(h * GLA_DV, (h + 1) * GLA_DV)
            state_ref[rs, cs] = dec_all[rs, c * LANES:(c + 1) * LANES] * state_ref[rs, cs] + kv[rs, cs]
    o = o_intra + jnp.concatenate(o_inter, axis=0)
    for h in range(GLA_HEADS):
        cs = slice(h * GLA_DV, (h + 1) * GLA_DV)
        o_h = o[:, cs]
        ms = jnp.mean(o_h * o_h, axis=-1, keepdims=True)
        o_h = o_h * lax.rsqrt(ms + LN_EPS) * gnorm_ref[:, cs]
        r_h = qkvr_ref[:, 2 * GLA_KEY + GLA_VAL + h * GLA_DV:2 * GLA_KEY + GLA_VAL + (h + 1) * GLA_DV]
        ya_ref[:, cs] = (o_h * (r_h * _sigmoid(r_h))).astype(BF16)

    gate_c = _sigmoid(proj(O_GATE + 2 * D, O_GATE + 3 * D))
    uv = proj(O_UV, O_XC)
    zg = 0.5 * uv * (1.0 + lax.erf(uv * (2.0 ** -0.5)))
    u = zg[:, :SGU_WIDTH]
    vln_ref[...] = _layer_norm(zg[:, SGU_WIDTH:], slng_ref[...], slnb_ref[...])
    lane_s = lax.broadcasted_iota(jnp.int32, (SGU_CHUNK, SGU_WIDTH), 1)
    s_parts = []
    for n in range(T // SGU_CHUNK):
        vc = vln_ref[n * SGU_CHUNK:(n + 1) * SGU_CHUNK, :]
        s = sbias_ref[...]
        for g in range(SGU_GROUPS):
            vg = jnp.where((lane_s >= g * SGU_GD) & (lane_s < (g + 1) * SGU_GD), vc, 0.0).astype(BF16)
            s = s + _dot(wtril_ref[g], vg)
        s_parts.append(s)
    y_b = (u * jnp.concatenate(s_parts, axis=0)).astype(BF16)

    P = POOL_CARRY
    xc = proj(O_XC, O_GATE)
    e_ref[P:P + T, :] = xc
    s2_ref[8:P + T, :] = e_ref[8:P + T, :] + e_ref[7:P + T - 1, :]
    s4_ref[16:P + T, :] = s2_ref[16:P + T, :] + s2_ref[14:P + T - 2, :]
    s8_ref[24:P + T, :] = s4_ref[24:P + T, :] + s4_ref[20:P + T - 4, :]
    s16 = s8_ref[P:P + T, :] + s8_ref[P - 8:P + T - 8, :]
    lane_p = lax.broadcasted_iota(jnp.int32, (T, POOL_WIDTH), 1)
    tpos = lax.broadcasted_iota(jnp.int32, (T, POOL_WIDTH), 0) + (j * T + 1)
    grp = lane_p >> 6
    win = jnp.where(grp == 0, POOL_WINDOWS[0], jnp.where(grp == 1, POOL_WINDOWS[1],
                    jnp.where(grp == 2, POOL_WINDOWS[2], POOL_WINDOWS[3])))
    wsum = jnp.where(grp == 0, s2_ref[P:P + T, :], jnp.where(grp == 1, s4_ref[P:P + T, :],
                     jnp.where(grp == 2, s8_ref[P:P + T, :], s16)))
    count = jnp.minimum(tpos, win).astype(F32)
    pooled = wsum / count - xc
    y_c = (_dot(pooled.astype(BF16), poolw_ref[...]) * pscale_ref[...]).astype(BF16)

    ra, rb = GLA_VAL, GLA_VAL + SGU_WIDTH
    merged = gate_a * _dot(ya_ref[...], wup_ref[0:ra, :])
    merged += gate_b * _dot(y_b, wup_ref[ra:rb, :])
    merged += gate_c * _dot(y_c, wup_ref[rb:, :])
    h = _dot(merged.astype(BF16), wo_ref[...])
    out_ref[0] = _layer_norm(DEEPNORM_ALPHA * x + h, ln1g_ref[...], ln1b_ref[...])


_MIXER_WEIGHTS = ('wcat', 'bcat', 'wg2', 'bg', 'gnorm', 'slng', 'slnb', 'wtril', 'sbias', 'poolw', 'pscale',
                  'wup', 'wo', 'ln1g', 'ln1b')


def _mixer(x, p, l):
    B, S, D = x.shape
    T = MIX_TILE
    weights = [p[n] for n in _MIXER_WEIGHTS]
    return pl.pallas_call(
        _mixer_kernel,
        out_shape=jax.ShapeDtypeStruct((B, S, D), F32),
        grid=(B, S // T),
        in_specs=[pl.BlockSpec((1, T, D), lambda b, j: (b, j, 0))] + [_layer_spec(w, l) for w in weights],
        out_specs=pl.BlockSpec((1, T, D), lambda b, j: (b, j, 0)),
        scratch_shapes=[
            pltpu.VMEM((GLA_KEY, GLA_VAL), F32),
            pltpu.VMEM((T, 2 * GLA_KEY + 2 * GLA_VAL), F32),
            pltpu.VMEM((T, GLA_VAL), BF16),
            pltpu.VMEM((T, SGU_WIDTH), F32),
            pltpu.VMEM((T + POOL_CARRY, POOL_WIDTH), F32),
            pltpu.VMEM((T + POOL_CARRY, POOL_WIDTH), F32),
            pltpu.VMEM((T + POOL_CARRY, POOL_WIDTH), F32),
            pltpu.VMEM((T + POOL_CARRY, POOL_WIDTH), F32),
        ],
        compiler_params=pltpu.CompilerParams(dimension_semantics=("arbitrary", "arbitrary"),
                                             vmem_limit_bytes=VMEM_LIMIT),
        name="mixer",
    )(x, *weights)


def _memkv_kernel(memt_ref, mem_ref, wkt_ref, wv_ref, kt_ref, v_ref):
    kt_ref[0] = _dot(wkt_ref[...], memt_ref[0].astype(BF16)).astype(BF16)
    v_ref[0] = _dot(mem_ref[0].astype(BF16), wv_ref[...]).astype(BF16)


def _memkv(mem, memt, p, l):
    B, M, D = mem.shape
    return pl.pallas_call(
        _memkv_kernel,
        out_shape=(jax.ShapeDtypeStruct((B, D, M), BF16), jax.ShapeDtypeStruct((B, M, D), BF16)),
        grid=(B,),
        in_specs=[pl.BlockSpec((1, D, M), lambda b: (b, 0, 0)), pl.BlockSpec((1, M, D), lambda b: (b, 0, 0)),
                  _layer_spec(p['wkt'], l), _layer_spec(p['wv'], l)],
        out_specs=(pl.BlockSpec((1, D, M), lambda b: (b, 0, 0)), pl.BlockSpec((1, M, D), lambda b: (b, 0, 0))),
        compiler_params=pltpu.CompilerParams(dimension_semantics=("arbitrary",), vmem_limit_bytes=VMEM_LIMIT),
        name="memkv",
    )(memt, mem, p['wkt'], p['wv'])


def _xattn_kernel(x_ref, kt_ref, v_ref, wq_ref, wo_ref, ln2g_ref, ln2b_ref, rwt_ref, rbt_ref,
                  x2_ref, x2p_ref, route_ref, counts_ref, carry_ref):
    T = XA_TILE

    @pl.when((pl.program_id(0) == 0) & (pl.program_id(1) == 0))
    def _():
        carry_ref[...] = jnp.zeros_like(carry_ref)

    x = x_ref[0]
    q = (_dot(x.astype(BF16), wq_ref[...]) * (XA_DH ** -0.5)).astype(BF16)
    h = jnp.zeros_like(x)
    for hd in range(XA_HEADS):
        cs = slice(hd * XA_DH, (hd + 1) * XA_DH)
        s = _dot(q[:, cs], kt_ref[0, cs, :])
        e = jnp.exp(s - jnp.max(s, axis=-1, keepdims=True))
        o = _dot(e.astype(BF16), v_ref[0, :, cs]) / jnp.sum(e, axis=-1, keepdims=True)
        h = h + _dot(o.astype(BF16), wo_ref[cs, :])
    x2 = _layer_norm(DEEPNORM_ALPHA * x + h, ln2g_ref[...], ln2b_ref[...])
    x2_ref[0] = x2
    x2p_ref[0] = _pack_bf16_pairs(x2)

    E = N_EXPERTS
    hi, lo = _split_bf16(x2)
    lt = _dot_t1(rwt_ref[...], hi)
    logits = lt[0:E] + (lt[E:2 * E] + _dot_t1(rwt_ref[0:E, :], lo)) + rbt_ref[...]

    eid = lax.broadcasted_iota(jnp.int32, (E, T), 0)
    neg_inf = jnp.float32(-jnp.inf)
    rest = logits
    tops, picks = [], []
    for _ in range(TOP_K):
        m = jnp.max(rest, axis=0, keepdims=True)
        idx = jnp.min(jnp.where(rest == m, eid, E), axis=0, keepdims=True)
        pick = eid == idx
        rest = jnp.where(pick, neg_inf, rest)
        tops.append((m, idx))
        picks.append(pick)
    exps = [jnp.exp(m - tops[0][0]) for m, _ in tops]
    denom = exps[0]
    for e in exps[1:]:
        denom = denom + e

    chosen = jnp.zeros((E, T), F32)
    for pick in picks:
        chosen = chosen + jnp.where(pick, 1.0, 0.0)
    chosen_bf = chosen.astype(BF16)
    earlier = (lax.broadcasted_iota(jnp.int32, (T, T), 0) < lax.broadcasted_iota(jnp.int32, (T, T), 1))
    carry = carry_ref[...]
    before = _dot(chosen_bf, jnp.where(earlier, 1.0, 0.0).astype(BF16)) + jnp.concatenate([carry] * (T // LANES), axis=1)
    carry = carry + _dot(chosen_bf, jnp.ones((T, LANES), BF16))
    carry_ref[...] = carry
    counts_ref[...] = carry

    rid = lax.broadcasted_iota(jnp.int32, (ROUTE_ROWS, T), 0)
    route = jnp.zeros((ROUTE_ROWS, T), F32)
    for k in range(TOP_K):
        rank = jnp.sum(jnp.where(picks[k], before, 0.0), axis=0, keepdims=True)
        route = jnp.where(rid == k, tops[k][1].astype(F32), route)
        route = jnp.where(rid == TOP_K + k, exps[k] / denom, route)
        route = jnp.where(rid == 2 * TOP_K + k, rank, route)
    route_ref[...] = route


def _xattn(x, kt, v, p, l):
    B, S, D = x.shape
    T = XA_TILE
    M = MEM_LEN
    weights = [p[n] for n in ('wq', 'wxo', 'ln2g', 'ln2b', 'rwt', 'rbt')]
    return pl.pallas_call(
        _xattn_kernel,
        out_shape=(jax.ShapeDtypeStruct((B, S, D), F32), jax.ShapeDtypeStruct((B, S, D // 2), jnp.uint32),
                   jax.ShapeDtypeStruct((ROUTE_ROWS, B * S), F32), jax.ShapeDtypeStruct((N_EXPERTS, LANES), F32)),
        grid=(B, S // T),
        in_specs=[pl.BlockSpec((1, T, D), lambda b, j: (b, j, 0)),
                  pl.BlockSpec((1, D, M), lambda b, j: (b, 0, 0)),
                  pl.BlockSpec((1, M, D), lambda b, j: (b, 0, 0))] + [_layer_spec(w, l) for w in weights],
        out_specs=(pl.BlockSpec((1, T, D), lambda b, j: (b, j, 0)),
                   pl.BlockSpec((1, T, D // 2), lambda b, j: (b, j, 0)),
                   pl.BlockSpec((ROUTE_ROWS, T), lambda b, j: (0, b * (S // T) + j)),
                   pl.BlockSpec((N_EXPERTS, LANES), lambda b, j: (0, 0))),
        scratch_shapes=[pltpu.VMEM((N_EXPERTS, LANES), F32)],
        compiler_params=pltpu.CompilerParams(dimension_semantics=("arbitrary", "arbitrary"),
                                             vmem_limit_bytes=VMEM_LIMIT),
        name="xattn",
    )(x, kt, v, *weights)


def _expert_kernel(layer, be_ref, nb_ref, nxt_ref, xs_ref, wgu_hbm, bgu_ref, wd_hbm, bd_ref, out_ref,
                   wgu_st, wd_st, wgu_bf, wd_bf, sem):
    i = pl.program_id(0)
    F = EXPERT_FF

    def weight_copies(e):
        return (pltpu.make_async_copy(wgu_hbm.at[layer, e], wgu_st, sem.at[0]),
                pltpu.make_async_copy(wd_hbm.at[layer, e], wd_st, sem.at[1]))

    @pl.when(i < nb_ref[0])
    def _():
        e = be_ref[i]
        prev = be_ref[jnp.maximum(i - 1, 0)]

        @pl.when(i == 0)
        def _():
            for cp in weight_copies(e):
                cp.start()

        @pl.when((i == 0) | (e != prev))
        def _():
            for cp in weight_copies(e):
                cp.wait()
            wgu_bf[...] = wgu_st[...].astype(BF16)
            wd_bf[...] = wd_st[...].astype(BF16)
            nxt = nxt_ref[e]

            @pl.when(nxt != e)
            def _():
                for cp in weight_copies(nxt):
                    cp.start()

        hh = _dot(_unpack_bf16_pairs(xs_ref[...]).astype(BF16), wgu_bf[...]) + bgu_ref[...]
        h_glu = jnp.minimum(hh[:, :F], SWIGLU_LIMIT)
        h_lin = jnp.clip(hh[:, F:], -SWIGLU_LIMIT, SWIGLU_LIMIT)
        a = h_glu * _sigmoid(SWIGLU_ALPHA * h_glu) * (h_lin + 1.0)
        out_ref[...] = _pack_bf16_pairs(_dot(a.astype(BF16), wd_bf[...]) + bd_ref[...])


def _experts(block_expert, n_used, next_expert, xs, w_gu, b_gu, w_down, b_down, l):
    P, DH = xs.shape
    D = 2 * DH
    NB = P // MOE_BLOCK
    F2 = 2 * EXPERT_FF

    def row_map(i, be, nb, nxt):
        return (jnp.minimum(i, nb[0] - 1), 0)

    def exp_map(i, be, nb, nxt):
        return (l, be[jnp.minimum(i, nb[0] - 1)], 0, 0)

    grid_spec = pltpu.PrefetchScalarGridSpec(
        num_scalar_prefetch=3,
        grid=(NB,),
        in_specs=[pl.BlockSpec((MOE_BLOCK, DH), row_map),
                  pl.BlockSpec(memory_space=pl.ANY),
                  pl.BlockSpec((None, None, 1, F2), exp_map),
                  pl.BlockSpec(memory_space=pl.ANY),
                  pl.BlockSpec((None, None, 1, D), exp_map)],
        out_specs=pl.BlockSpec((MOE_BLOCK, DH), row_map),
        scratch_shapes=[pltpu.VMEM((D, F2), F32), pltpu.VMEM((EXPERT_FF, D), F32),
                        pltpu.VMEM((D, F2), BF16), pltpu.VMEM((EXPERT_FF, D), BF16),
                        pltpu.SemaphoreType.DMA((2,))],
    )
    return pl.pallas_call(
        functools.partial(_expert_kernel, l),
        out_shape=jax.ShapeDtypeStruct((P, DH), jnp.uint32),
        grid_spec=grid_spec,
        compiler_params=pltpu.CompilerParams(dimension_semantics=("arbitrary",), vmem_limit_bytes=VMEM_LIMIT),
        name="experts",
    )(block_expert, n_used, next_expert, xs, w_gu, b_gu.reshape(DEPTH, N_EXPERTS, 1, F2), w_down,
      b_down.reshape(DEPTH, N_EXPERTS, 1, D))


def _sc_gather_rows(x, idx):
    M = idx.shape[0]
    D = x.shape[1]
    W = SC_GATHER_WINDOW
    mesh = plsc.VectorSubcoreMesh(core_axis_name="core", subcore_axis_name="subcore")
    n_workers = mesh.num_cores * mesh.num_subcores
    rows_per = M // n_workers
    assert rows_per * n_workers == M and rows_per % W == 0

    @pl.kernel(out_type=jax.ShapeDtypeStruct((M, D), x.dtype), mesh=mesh, name="sc_gather_rows",
               scratch_types=[pltpu.VMEM((rows_per,), jnp.int32), pltpu.VMEM((W, D), x.dtype)])
    def gather_kernel(x_hbm, i_hbm, o_hbm, idx_vmem, buf):
        wid = lax.axis_index("core") * mesh.num_subcores + lax.axis_index("subcore")
        base = wid * rows_per
        pltpu.sync_copy(i_hbm.at[pl.ds(base, rows_per)], idx_vmem)

        @pl.loop(0, rows_per // W)
        def _(j):
            pltpu.sync_copy(x_hbm.at[idx_vmem.at[pl.ds(j * W, W)]], buf)
            pltpu.sync_copy(buf, o_hbm.at[pl.ds(base + j * W, W)])

    return gather_kernel(x, idx)


def _sc_scatter_rows(x, idx, n_out):
    K, N = idx.shape
    D = x.shape[1]
    W = SC_GATHER_WINDOW
    mesh = plsc.VectorSubcoreMesh(core_axis_name="core", subcore_axis_name="subcore")
    n_workers = mesh.num_cores * mesh.num_subcores
    rows_per = N // n_workers
    assert rows_per * n_workers == N and rows_per % W == 0

    @pl.kernel(out_type=jax.ShapeDtypeStruct((n_out, D), x.dtype), mesh=mesh, name="sc_scatter_rows",
               scratch_types=[pltpu.VMEM((K * rows_per,), jnp.int32), pltpu.VMEM((W, D), x.dtype)])
    def scatter_kernel(x_hbm, i_hbm, o_hbm, idx_vmem, buf):
        wid = lax.axis_index("core") * mesh.num_subcores + lax.axis_index("subcore")
        base = wid * rows_per
        for k in range(K):
            pltpu.sync_copy(i_hbm.at[pl.ds(k * N + base, rows_per)], idx_vmem.at[pl.ds(k * rows_per, rows_per)])

        @pl.loop(0, rows_per // W)
        def _(j):
            pltpu.sync_copy(x_hbm.at[pl.ds(base + j * W, W)], buf)
            for k in range(K):
                pltpu.sync_copy(buf, o_hbm.at[idx_vmem.at[pl.ds(k * rows_per + j * W, W)]])

    return scatter_kernel(x, idx.reshape(K * N))


def _combine_kernel(x_ref, yg_ref, gate_ref, ln3g_ref, ln3b_ref, *rest):
    out_ref = rest[-1]
    x = x_ref[...]
    g = gate_ref[...]
    y = jnp.zeros_like(x)
    for k in range(TOP_K):
        y = y + g[:, k:k + 1] * _unpack_bf16_pairs(yg_ref[k * CMB_TILE:(k + 1) * CMB_TILE, :])
    out_ref[...] = _layer_norm(DEEPNORM_ALPHA * x + y, ln3g_ref[...], ln3b_ref[...])


def _combine(x2, yg, gate, p, l, part, acc):
    N, D = x2.shape
    T = CMB_TILE
    tiles = yg.shape[0] // (TOP_K * T)
    first = part * tiles
    in_specs = [pl.BlockSpec((T, D), lambda i: (first + i, 0)), pl.BlockSpec((TOP_K * T, D // 2), lambda i: (i, 0)),
                pl.BlockSpec((T, TOP_K), lambda i: (first + i, 0)), _layer_spec(p['ln3g'], l), _layer_spec(p['ln3b'], l)]
    args = [x2, yg, gate, p['ln3g'], p['ln3b']]
    aliases = {}
    if acc is not None:
        in_specs.append(pl.BlockSpec(memory_space=pl.ANY))
        args.append(acc)
        aliases = {len(args) - 1: 0}
    return pl.pallas_call(
        _combine_kernel,
        out_shape=jax.ShapeDtypeStruct((N, D), F32),
        grid=(tiles,),
        in_specs=in_specs,
        out_specs=pl.BlockSpec((T, D), lambda i: (first + i, 0)),
        input_output_aliases=aliases,
        compiler_params=pltpu.CompilerParams(dimension_semantics=("arbitrary",), vmem_limit_bytes=VMEM_LIMIT),
        name="combine",
    )(*args)


def _prep(w_in, b_in, gla_wg2, gla_bg, gla_norm_g, sgu_ln_g, sgu_ln_b, sgu_ws, sgu_bs, pool_w, pool_scale,
          w_up_a, w_up_b, w_up_c, w_o, ln1_g, ln1_b, xa_wq, xa_wk, xa_wv, xa_wo, ln2_g, ln2_b,
          router_w, router_b, ln3_g, ln3_b):
    L = w_in.shape[0]
    o_glow = O_GLOW
    o_uv = o_glow + GLA_RANK
    row = lambda a: a.reshape(L, 1, -1).astype(F32)
    pad_last = lambda a, n: jnp.pad(a, [(0, 0)] * (a.ndim - 1) + [(0, n - a.shape[-1])])
    p = {}
    p['wcat'] = jnp.concatenate([w_in[..., :o_glow], pad_last(w_in[..., o_glow:o_uv], LANES), w_in[..., o_uv:]],
                                axis=-1).astype(BF16)
    p['bcat'] = row(jnp.concatenate([b_in[..., :o_glow], pad_last(b_in[..., o_glow:o_uv], LANES), b_in[..., o_uv:]],
                                    axis=-1))
    p['wg2'] = jnp.pad(gla_wg2, ((0, 0), (0, LANES - GLA_RANK), (0, 0))).astype(BF16)
    p['bg'] = row(gla_bg)
    p['gnorm'] = row(gla_norm_g)
    p['slng'] = row(sgu_ln_g)
    p['slnb'] = row(sgu_ln_b)
    p['wtril'] = jnp.tril(sgu_ws).astype(BF16)
    p['sbias'] = jnp.repeat(jnp.swapaxes(sgu_bs, 1, 2), SGU_GD, axis=2).astype(F32)
    G = len(POOL_WINDOWS)
    eye = jnp.eye(G, dtype=F32)
    p['poolw'] = jnp.einsum('lgcd,gh->lgchd', pool_w, eye).reshape(L, POOL_WIDTH, POOL_WIDTH).astype(BF16)
    p['pscale'] = row(pool_scale)
    p['wup'] = jnp.concatenate([w_up_a, w_up_b, w_up_c], axis=1).astype(BF16)
    p['wo'] = w_o.astype(BF16)
    p['ln1g'], p['ln1b'] = row(ln1_g), row(ln1_b)
    p['wq'] = xa_wq.astype(BF16)
    p['wkt'] = jnp.swapaxes(xa_wk, 1, 2).astype(BF16)
    p['wv'] = xa_wv.astype(BF16)
    p['wxo'] = xa_wo.astype(BF16)
    p['ln2g'], p['ln2b'] = row(ln2_g), row(ln2_b)
    rwt = jnp.swapaxes(router_w, 1, 2)
    rwt_hi = rwt.astype(BF16)
    p['rwt'] = jnp.concatenate([rwt_hi, (rwt - rwt_hi.astype(F32)).astype(BF16)], axis=1)
    p['rbt'] = jnp.broadcast_to(router_b[:, :, None], router_b.shape + (XA_TILE,)).astype(F32)
    p['ln3g'], p['ln3b'] = row(ln3_g), row(ln3_b)
    return p


def _route(route, counts):
    N = route.shape[1]
    top_idx = route[0:TOP_K].astype(jnp.int32)
    gate = route[TOP_K:2 * TOP_K].T
    rank = route[2 * TOP_K:3 * TOP_K].astype(jnp.int32)
    counts = counts[:, 0].astype(jnp.int32)
    padded = ((counts + MOE_BLOCK - 1) // MOE_BLOCK) * MOE_BLOCK
    pad_end = jnp.cumsum(padded)
    pad_start = pad_end - padded
    ids = jnp.arange(N_EXPERTS, dtype=jnp.int32)
    start_of = jnp.sum(jnp.where(top_idx[:, :, None] == ids[None, None, :], pad_start[None, None, :], 0), axis=-1)
    dest = start_of + rank
    n_blocks = N * TOP_K // MOE_BLOCK + N_EXPERTS
    block_start = jnp.arange(n_blocks, dtype=jnp.int32) * MOE_BLOCK
    block_expert = jnp.minimum(jnp.sum((pad_end[None, :] <= block_start[:, None]).astype(jnp.int32), axis=1),
                               N_EXPERTS - 1)
    n_used = (pad_end[-1] // MOE_BLOCK).astype(jnp.int32).reshape(1)
    later = jnp.where((ids[None, :] > ids[:, None]) & (counts[None, :] > 0), ids[None, :], N_EXPERTS)
    first_later = jnp.min(later, axis=1)
    next_expert = jnp.where(first_later < N_EXPERTS, first_later, ids).astype(jnp.int32)
    return gate, dest, block_expert, n_used, next_expert


def kernel(x, mem, w_in, b_in, gla_wg2, gla_bg, gla_norm_g, sgu_ln_g, sgu_ln_b, sgu_ws, sgu_bs, pool_w, pool_scale, w_up_a, w_up_b, w_up_c, w_o, ln1_g, ln1_b, xa_wq, xa_wk, xa_wv, xa_wo, ln2_g, ln2_b, router_w, router_b, exp_w_gu, exp_b_gu, exp_w_down, exp_b_down, ln3_g, ln3_b):
    B, S, D = x.shape
    N = B * S
    p = _prep(w_in, b_in, gla_wg2, gla_bg, gla_norm_g, sgu_ln_g, sgu_ln_b, sgu_ws, sgu_bs, pool_w, pool_scale,
              w_up_a, w_up_b, w_up_c, w_o, ln1_g, ln1_b, xa_wq, xa_wk, xa_wv, xa_wo, ln2_g, ln2_b,
              router_w, router_b, ln3_g, ln3_b)
    memt = jnp.swapaxes(mem, 1, 2)
    for l in range(DEPTH):
        x1 = _mixer(x, p, l)
        kt, v = _memkv(mem, memt, p, l)
        x2, x2p, route, counts = _xattn(x1, kt, v, p, l)
        gate, dest, block_expert, n_used, next_expert = _route(route, counts)
        n_slots = (N * TOP_K // MOE_BLOCK + N_EXPERTS) * MOE_BLOCK
        xs = _sc_scatter_rows(x2p.reshape(N, D // 2), dest, n_slots)
        ys = _experts(block_expert, n_used, next_expert, xs, exp_w_gu, exp_b_gu, exp_w_down, exp_b_down, l)
        dest_km = dest.reshape(TOP_K, N // CMB_TILE, CMB_TILE).transpose(1, 0, 2).reshape(-1)
        rows = dest_km.shape[0] // COMBINE_PARTS
        acc = None
        for part in range(COMBINE_PARTS):
            yg = _sc_gather_rows(ys, dest_km[part * rows:(part + 1) * rows])
            acc = _combine(x2.reshape(N, D), yg, gate, p, l, part, acc)
        x = acc.reshape(B, S, D)
    return x
```

```python
import functools

import jax
import jax.numpy as jnp
from jax import lax
from jax.experimental import pallas as pl
from jax.experimental.pallas import tpu as pltpu
from jax.experimental.pallas import tpu_sc as plsc

F32 = jnp.float32
BF16 = jnp.bfloat16

D_MODEL = 1024
DEPTH = 2
GLA_HEADS = 4
GLA_KEY = 256
GLA_VAL = 512
GLA_DK = 64
GLA_DV = 128
GLA_RANK = 16
GLA_TAU = 16.0
GLA_CHUNK = 64
SGU_GROUPS = 4
SGU_WIDTH = 256
SGU_GD = 64
SGU_CHUNK = 128
POOL_WINDOWS = (2, 4, 8, 16)
POOL_WIDTH = 256
POOL_GD = 64
POOL_CARRY = 32
MEM_LEN = 256
XA_HEADS = 4
XA_DH = 256
N_EXPERTS = 32
TOP_K = 4
EXPERT_FF = 1024
SWIGLU_LIMIT = 7.0
SWIGLU_ALPHA = 1.702
DEEPNORM_ALPHA = (2 * DEPTH) ** 0.25
LN_EPS = 1e-5
LANES = 128
VMEM_LIMIT = 56 * 1024 * 1024

MIX_TILE = 256
XA_TILE = 512
MOE_BLOCK = 256
CMB_TILE = 256
COMBINE_PARTS = 4
ROUTE_ROWS = 16
SC_GATHER_WINDOW = 128

O_QKVR = 0
O_GLOW = 2 * GLA_KEY + 2 * GLA_VAL
O_UV = O_GLOW + LANES
O_XC = O_UV + 2 * SGU_WIDTH
O_GATE = O_XC + POOL_WIDTH
N_PACK = O_GATE + 3 * D_MODEL


def _dot(a, b):
    return jnp.dot(a, b, preferred_element_type=F32)


def _dot_t0(a, b):
    return lax.dot_general(a, b, (((0,), (0,)), ((), ())), preferred_element_type=F32)


def _dot_t1(a, b):
    return lax.dot_general(a, b, (((1,), (1,)), ((), ())), preferred_element_type=F32)


def _split_bf16(x):
    hi = x.astype(BF16)
    lo = (x - hi.astype(F32)).astype(BF16)
    return hi, lo


def _layer_norm(x, g, b):
    mu = jnp.mean(x, axis=-1, keepdims=True)
    xc = x - mu
    var = jnp.mean(xc * xc, axis=-1, keepdims=True)
    return xc * lax.rsqrt(var + LN_EPS) * g + b


def _sigmoid(x):
    return 1.0 / (1.0 + jnp.exp(-x))


def _pack_bf16_pairs(x):
    H = x.shape[1] // 2
    bits = lax.bitcast_convert_type(x.astype(BF16).astype(F32), jnp.uint32)
    return (bits[:, :H] >> 16) | (bits[:, H:] & jnp.uint32(0xFFFF0000))


def _unpack_bf16_pairs(w):
    lo = lax.bitcast_convert_type(w << 16, F32)
    hi = lax.bitcast_convert_type(w & jnp.uint32(0xFFFF0000), F32)
    return jnp.concatenate([lo, hi], axis=1)


def _layer_spec(arr, l):
    nd = arr.ndim - 1
    return pl.BlockSpec((None,) + arr.shape[1:], lambda *_: (l,) + (0,) * nd, pipeline_mode=pl.Buffered(1))


def _mixer_kernel(x_ref, wcat_ref, bcat_ref, wg2_ref, bg_ref, gnorm_ref,
                  slng_ref, slnb_ref, wtril_ref, sbias_ref, poolw_ref, pscale_ref,
                  wup_ref, wo_ref, ln1g_ref, ln1b_ref,
                  out_ref,
                  state_ref, qkvr_ref, ya_ref, vln_ref, e_ref, s2_ref, s4_ref, s8_ref):
    T = MIX_TILE
    D = D_MODEL
    j = pl.program_id(1)
    x = x_ref[0]
    xb = x.astype(BF16)

    def proj(lo, hi):
        return _dot(xb, wcat_ref[:, lo:hi]) + bcat_ref[:, lo:hi]

    @pl.when(j == 0)
    def _():
        state_ref[...] = jnp.zeros_like(state_ref)
        e_ref[0:POOL_CARRY, :] = jnp.zeros((POOL_CARRY, POOL_WIDTH), F32)

    @pl.when(j > 0)
    def _():
        e_ref[0:POOL_CARRY, :] = e_ref[T:T + POOL_CARRY, :]

    qkvr_ref[...] = proj(O_QKVR, O_GLOW)
    glow = proj(O_GLOW, O_UV)
    z = _dot(glow.astype(BF16), wg2_ref[...]) + bg_ref[...]
    la = (jnp.minimum(z, 0.0) - jnp.log1p(jnp.exp(-jnp.abs(z)))) * (1.0 / GLA_TAU)
    la_hi, la_lo = _split_bf16(la)
    gate_a = _sigmoid(proj(O_GATE, O_GATE + D))

    C = GLA_CHUNK
    NC = T // C
    CSH = C.bit_length() - 1
    row = lax.broadcasted_iota(jnp.int32, (T, T), 0)
    col = lax.broadcasted_iota(jnp.int32, (T, T), 1)
    same_chunk = (row >> CSH) == (col >> CSH)
    causal = same_chunk & (row >= col)
    causal_bf = jnp.where(causal, 1.0, 0.0).astype(BF16)
    chunk_bf = jnp.where(same_chunk, 1.0, 0.0).astype(BF16)
    b = _dot(causal_bf, la_hi) + _dot(causal_bf, la_lo)
    b_end = _dot(chunk_bf, la_hi) + _dot(chunk_bf, la_lo)
    q = qkvr_ref[:, 0:GLA_KEY]
    k = qkvr_ref[:, GLA_KEY:2 * GLA_KEY]
    v = qkvr_ref[:, 2 * GLA_KEY:2 * GLA_KEY + GLA_VAL].astype(BF16)
    q_dec = q * (GLA_DK ** -0.5) * jnp.exp(b)
    k_dec = (k * jnp.exp(-b)).astype(BF16)
    k_tail = (k * jnp.exp(b_end - b)).astype(BF16)
    q_dec_bf = q_dec.astype(BF16)
    lane = lax.broadcasted_iota(jnp.int32, (T, GLA_KEY), 1)
    o_heads = []
    for h in range(GLA_HEADS):
        q_h = jnp.where((lane >= h * GLA_DK) & (lane < (h + 1) * GLA_DK), q_dec, 0.0).astype(BF16)
        scores = jnp.where(causal, _dot_t1(q_h, k_dec), 0.0).astype(BF16)
        o_heads.append(_dot(scores, v[:, h * GLA_DV:(h + 1) * GLA_DV]))
    o_intra = jnp.concatenate(o_heads, axis=1)
    gate_b = _sigmoid(proj(O_GATE + D, O_GATE + 2 * D))
    ind = jnp.where((lax.broadcasted_iota(jnp.int32, (T, NC * LANES), 0) >> CSH)
                    == (lax.broadcasted_iota(jnp.int32, (T, NC * LANES), 1) >> 7), 1.0, 0.0).astype(BF16)
    dec_all = jnp.exp(_dot_t0(la_hi, ind) + _dot_t0(la_lo, ind))
    o_inter = []
    for c in range(NC):
        rows = slice(c * C, (c + 1) * C)
        o_inter.append(_dot(q_dec_bf[rows], state_ref[...].astype(BF16)))
        kv = _dot_t0(k_tail[rows], v[rows])
        for h in range(GLA_HEADS):
            rs = slice(h * GLA_DK, (h + 1) * GLA_DK)
            cs = slice(h * GLA_DV, (h + 1) * GLA_DV)
            state_ref[rs, cs] = dec_all[rs, c * LANES:(c + 1) * LANES] * state_ref[rs, cs] + kv[rs, cs]
    o = o_intra + jnp.concatenate(o_inter, axis=0)
    for h in range(GLA_HEADS):
        cs = slice(h * GLA_DV, (h + 1) * GLA_DV)
        o_h = o[:, cs]
        ms = jnp.mean(o_h * o_h, axis=-1, keepdims=True)
        o_h = o_h * lax.rsqrt(ms + LN_EPS) * gnorm_ref[:, cs]
        r_h = qkvr_ref[:, 2 * GLA_KEY + GLA_VAL + h * GLA_DV:2 * GLA_KEY + GLA_VAL + (h + 1) * GLA_DV]
        ya_ref[:, cs] = (o_h * (r_h * _sigmoid(r_h))).astype(BF16)

    gate_c = _sigmoid(proj(O_GATE + 2 * D, O_GATE + 3 * D))
    uv = proj(O_UV, O_XC)
    zg = 0.5 * uv * (1.0 + lax.erf(uv * (2.0 ** -0.5)))
    u = zg[:, :SGU_WIDTH]
    vln_ref[...] = _layer_norm(zg[:, SGU_WIDTH:], slng_ref[...], slnb_ref[...])
    lane_s = lax.broadcasted_iota(jnp.int32, (SGU_CHUNK, SGU_WIDTH), 1)
    s_parts = []
    for n in range(T // SGU_CHUNK):
        vc = vln_ref[n * SGU_CHUNK:(n + 1) * SGU_CHUNK, :]
        s = sbias_ref[...]
        for g in range(SGU_GROUPS):
            vg = jnp.where((lane_s >= g * SGU_GD) & (lane_s < (g + 1) * SGU_GD), vc, 0.0).astype(BF16)
            s = s + _dot(wtril_ref[g], vg)
        s_parts.append(s)
    y_b = (u * jnp.concatenate(s_parts, axis=0)).astype(BF16)

    P = POOL_CARRY
    xc = proj(O_XC, O_GATE)
    e_ref[P:P + T, :] = xc
    s2_ref[8:P + T, :] = e_ref[8:P + T, :] + e_ref[7:P + T - 1, :]
    s4_ref[16:P + T, :] = s2_ref[16:P + T, :] + s2_ref[14:P + T - 2, :]
    s8_ref[24:P + T, :] = s4_ref[24:P + T, :] + s4_ref[20:P + T - 4, :]
    s16 = s8_ref[P:P + T, :] + s8_ref[P - 8:P + T - 8, :]
    lane_p = lax.broadcasted_iota(jnp.int32, (T, POOL_WIDTH), 1)
    tpos = lax.broadcasted_iota(jnp.int32, (T, POOL_WIDTH), 0) + (j * T + 1)
    grp = lane_p >> 6
    win = jnp.where(grp == 0, POOL_WINDOWS[0], jnp.where(grp == 1, POOL_WINDOWS[1],
                    jnp.where(grp == 2, POOL_WINDOWS[2], POOL_WINDOWS[3])))
    wsum = jnp.where(grp == 0, s2_ref[P:P + T, :], jnp.where(grp == 1, s4_ref[P:P + T, :],
                     jnp.where(grp == 2, s8_ref[P:P + T, :], s16)))
    count = jnp.minimum(tpos, win).astype(F32)
    pooled = wsum / count - xc
    y_c = (_dot(pooled.astype(BF16), poolw_ref[...]) * pscale_ref[...]).astype(BF16)

    ra, rb = GLA_VAL, GLA_VAL + SGU_WIDTH
    merged = gate_a * _dot(ya_ref[...], wup_ref[0:ra, :])
    merged += gate_b * _dot(y_b, wup_ref[ra:rb, :])
    merged += gate_c * _dot(y_c, wup_ref[rb:, :])
    h = _dot(merged.astype(BF16), wo_ref[...])
    out_ref[0] = _layer_norm(DEEPNORM_ALPHA * x + h, ln1g_ref[...], ln1b_ref[...])


_MIXER_WEIGHTS = ('wcat', 'bcat', 'wg2', 'bg', 'gnorm', 'slng', 'slnb', 'wtril', 'sbias', 'poolw', 'pscale',
                  'wup', 'wo', 'ln1g', 'ln1b')


def _mixer(x, p, l):
    B, S, D = x.shape
    T = MIX_TILE
    weights = [p[n] for n in _MIXER_WEIGHTS]
    return pl.pallas_call(
        _mixer_kernel,
        out_shape=jax.ShapeDtypeStruct((B, S, D), F32),
        grid=(B, S // T),
        in_specs=[pl.BlockSpec((1, T, D), lambda b, j: (b, j, 0))] + [_layer_spec(w, l) for w in weights],
        out_specs=pl.BlockSpec((1, T, D), lambda b, j: (b, j, 0)),
        scratch_shapes=[
            pltpu.VMEM((GLA_KEY, GLA_VAL), F32),
            pltpu.VMEM((T, 2 * GLA_KEY + 2 * GLA_VAL), F32),
            pltpu.VMEM((T, GLA_VAL), BF16),
            pltpu.VMEM((T, SGU_WIDTH), F32),
            pltpu.VMEM((T + POOL_CARRY, POOL_WIDTH), F32),
            pltpu.VMEM((T + POOL_CARRY, POOL_WIDTH), F32),
            pltpu.VMEM((T + POOL_CARRY, POOL_WIDTH), F32),
            pltpu.VMEM((T + POOL_CARRY, POOL_WIDTH), F32),
        ],
        compiler_params=pltpu.CompilerParams(dimension_semantics=("arbitrary", "arbitrary"),
                                             vmem_limit_bytes=VMEM_LIMIT),
        name="mixer",
    )(x, *weights)


def _memkv_kernel(memt_ref, mem_ref, wkt_ref, wv_ref, kt_ref, v_ref):
    kt_ref[0] = _dot(wkt_ref[...], memt_ref[0].astype(BF16)).astype(BF16)
    v_ref[0] = _dot(mem_ref[0].astype(BF16), wv_ref[...]).astype(BF16)


def _memkv(mem, memt, p, l):
    B, M, D = mem.shape
    return pl.pallas_call(
        _memkv_kernel,
        out_shape=(jax.ShapeDtypeStruct((B, D, M), BF16), jax.ShapeDtypeStruct((B, M, D), BF16)),
        grid=(B,),
        in_specs=[pl.BlockSpec((1, D, M), lambda b: (b, 0, 0)), pl.BlockSpec((1, M, D), lambda b: (b, 0, 0)),
                  _layer_spec(p['wkt'], l), _layer_spec(p['wv'], l)],
        out_specs=(pl.BlockSpec((1, D, M), lambda b: (b, 0, 0)), pl.BlockSpec((1, M, D), lambda b: (b, 0, 0))),
        compiler_params=pltpu.CompilerParams(dimension_semantics=("arbitrary",), vmem_limit_bytes=VMEM_LIMIT),
        name="memkv",
    )(memt, mem, p['wkt'], p['wv'])


def _xattn_kernel(x_ref, kt_ref, v_ref, wq_ref, wo_ref, ln2g_ref, ln2b_ref, rwt_ref, rbt_ref,
                  x2_ref, x2p_ref, route_ref, counts_ref, carry_ref):
    T = XA_TILE

    @pl.when((pl.program_id(0) == 0) & (pl.program_id(1) == 0))
    def _():
        carry_ref[...] = jnp.zeros_like(carry_ref)

    x = x_ref[0]
    q = (_dot(x.astype(BF16), wq_ref[...]) * (XA_DH ** -0.5)).astype(BF16)
    h = jnp.zeros_like(x)
    for hd in range(XA_HEADS):
        cs = slice(hd * XA_DH, (hd + 1) * XA_DH)
        s = _dot(q[:, cs], kt_ref[0, cs, :])
        e = jnp.exp(s - jnp.max(s, axis=-1, keepdims=True))
        o = _dot(e.astype(BF16), v_ref[0, :, cs]) / jnp.sum(e, axis=-1, keepdims=True)
        h = h + _dot(o.astype(BF16), wo_ref[cs, :])
    x2 = _layer_norm(DEEPNORM_ALPHA * x + h, ln2g_ref[...], ln2b_ref[...])
    x2_ref[0] = x2
    x2p_ref[0] = _pack_bf16_pairs(x2)

    E = N_EXPERTS
    hi, lo = _split_bf16(x2)
    lt = _dot_t1(rwt_ref[...], hi)
    logits = lt[0:E] + (lt[E:2 * E] + _dot_t1(rwt_ref[0:E, :], lo)) + rbt_ref[...]

    eid = lax.broadcasted_iota(jnp.int32, (E, T), 0)
    neg_inf = jnp.float32(-jnp.inf)
    rest = logits
    tops, picks = [], []
    for _ in range(TOP_K):
        m = jnp.max(rest, axis=0, keepdims=True)
        idx = jnp.min(jnp.where(rest == m, eid, E), axis=0, keepdims=True)
        pick = eid == idx
        rest = jnp.where(pick, neg_inf, rest)
        tops.append((m, idx))
        picks.append(pick)
    exps = [jnp.exp(m - tops[0][0]) for m, _ in tops]
    denom = exps[0]
    for e in exps[1:]:
        denom = denom + e

    chosen = jnp.zeros((E, T), F32)
    for pick in picks:
        chosen = chosen + jnp.where(pick, 1.0, 0.0)
    chosen_bf = chosen.astype(BF16)
    earlier = (lax.broadcasted_iota(jnp.int32, (T, T), 0) < lax.broadcasted_iota(jnp.int32, (T, T), 1))
    carry = carry_ref[...]
    before = _dot(chosen_bf, jnp.where(earlier, 1.0, 0.0).astype(BF16)) + jnp.concatenate([carry] * (T // LANES), axis=1)
    carry = carry + _dot(chosen_bf, jnp.ones((T, LANES), BF16))
    carry_ref[...] = carry
    counts_ref[...] = carry

    rid = lax.broadcasted_iota(jnp.int32, (ROUTE_ROWS, T), 0)
    route = jnp.zeros((ROUTE_ROWS, T), F32)
    for k in range(TOP_K):
        rank = jnp.sum(jnp.where(picks[k], before, 0.0), axis=0, keepdims=True)
        route = jnp.where(rid == k, tops[k][1].astype(F32), route)
        route = jnp.where(rid == TOP_K + k, exps[k] / denom, route)
        route = jnp.where(rid == 2 * TOP_K + k, rank, route)
    route_ref[...] = route


def _xattn(x, kt, v, p, l):
    B, S, D = x.shape
    T = XA_TILE
    M = MEM_LEN
    weights = [p[n] for n in ('wq', 'wxo', 'ln2g', 'ln2b', 'rwt', 'rbt')]
    return pl.pallas_call(
        _xattn_kernel,
        out_shape=(jax.ShapeDtypeStruct((B, S, D), F32), jax.ShapeDtypeStruct((B, S, D // 2), jnp.uint32),
                   jax.ShapeDtypeStruct((ROUTE_ROWS, B * S), F32), jax.ShapeDtypeStruct((N_EXPERTS, LANES), F32)),
        grid=(B, S // T),
        in_specs=[pl.BlockSpec((1, T, D), lambda b, j: (b, j, 0)),
                  pl.BlockSpec((1, D, M), lambda b, j: (b, 0, 0)),
                  pl.BlockSpec((1, M, D), lambda b, j: (b, 0, 0))] + [_layer_spec(w, l) for w in weights],
        out_specs=(pl.BlockSpec((1, T, D), lambda b, j: (b, j, 0)),
                   pl.BlockSpec((1, T, D // 2), lambda b, j: (b, j, 0)),
                   pl.BlockSpec((ROUTE_ROWS, T), lambda b, j: (0, b * (S // T) + j)),
                   pl.BlockSpec((N_EXPERTS, LANES), lambda b, j: (0, 0))),
        scratch_shapes=[pltpu.VMEM((N_EXPERTS, LANES), F32)],
        compiler_params=pltpu.CompilerParams(dimension_semantics=("arbitrary", "arbitrary"),
                                             vmem_limit_bytes=VMEM_LIMIT),
        name="xattn",
    )(x, kt, v, *weights)


def _expert_kernel(layer, be_ref, nb_ref, nxt_ref, xs_ref, wgu_hbm, bgu_ref, wd_hbm, bd_ref, out_ref,
                   wgu_st, wd_st, wgu_bf, wd_bf, sem):
    i = pl.program_id(0)
    F = EXPERT_FF

    def weight_copies(e):
        return (pltpu.make_async_copy(wgu_hbm.at[layer, e], wgu_st, sem.at[0]),
                pltpu.make_async_copy(wd_hbm.at[layer, e], wd_st, sem.at[1]))

    @pl.when(i < nb_ref[0])
    def _():
        e = be_ref[i]
        prev = be_ref[jnp.maximum(i - 1, 0)]

        @pl.when(i == 0)
        def _():
            for cp in weight_copies(e):
                cp.start()

        @pl.when((i == 0) | (e != prev))
        def _():
            for cp in weight_copies(e):
                cp.wait()
            wgu_bf[...] = wgu_st[...].astype(BF16)
            wd_bf[...] = wd_st[...].astype(BF16)
            nxt = nxt_ref[e]

            @pl.when(nxt != e)
            def _():
                for cp in weight_copies(nxt):
                    cp.start()

        hh = _dot(_unpack_bf16_pairs(xs_ref[...]).astype(BF16), wgu_bf[...]) + bgu_ref[...]
        h_glu = jnp.minimum(hh[:, :F], SWIGLU_LIMIT)
        h_lin = jnp.clip(hh[:, F:], -SWIGLU_LIMIT, SWIGLU_LIMIT)
        a = h_glu * _sigmoid(SWIGLU_ALPHA * h_glu) * (h_lin + 1.0)
        out_ref[...] = _pack_bf16_pairs(_dot(a.astype(BF16), wd_bf[...]) + bd_ref[...])


def _experts(block_expert, n_used, next_expert, xs, w_gu, b_gu, w_down, b_down, l):
    P, DH = xs.shape
    D = 2 * DH
    NB = P // MOE_BLOCK
    F2 = 2 * EXPERT_FF

    def row_map(i, be, nb, nxt):
        return (jnp.minimum(i, nb[0] - 1), 0)

    def exp_map(i, be, nb, nxt):
        return (l, be[jnp.minimum(i, nb[0] - 1)], 0, 0)

    grid_spec = pltpu.PrefetchScalarGridSpec(
        num_scalar_prefetch=3,
        grid=(NB,),
        in_specs=[pl.BlockSpec((MOE_BLOCK, DH), row_map),
                  pl.BlockSpec(memory_space=pl.ANY),
                  pl.BlockSpec((None, None, 1, F2), exp_map),
                  pl.BlockSpec(memory_space=pl.ANY),
                  pl.BlockSpec((None, None, 1, D), exp_map)],
        out_specs=pl.BlockSpec((MOE_BLOCK, DH), row_map),
        scratch_shapes=[pltpu.VMEM((D, F2), F32), pltpu.VMEM((EXPERT_FF, D), F32),
                        pltpu.VMEM((D, F2), BF16), pltpu.VMEM((EXPERT_FF, D), BF16),
                        pltpu.SemaphoreType.DMA((2,))],
    )
    return pl.pallas_call(
        functools.partial(_expert_kernel, l),
        out_shape=jax.ShapeDtypeStruct((P, DH), jnp.uint32),
        grid_spec=grid_spec,
        compiler_params=pltpu.CompilerParams(dimension_semantics=("arbitrary",), vmem_limit_bytes=VMEM_LIMIT),
        name="experts",
    )(block_expert, n_used, next_expert, xs, w_gu, b_gu.reshape(DEPTH, N_EXPERTS, 1, F2), w_down,
      b_down.reshape(DEPTH, N_EXPERTS, 1, D))


def _sc_gather_rows(x, idx):
    M = idx.shape[0]
    D = x.shape[1]
    W = SC_GATHER_WINDOW
    mesh = plsc.VectorSubcoreMesh(core_axis_name="core", subcore_axis_name="subcore")
    n_workers = mesh.num_cores * mesh.num_subcores
    rows_per = M // n_workers
    assert rows_per * n_workers == M and rows_per % W == 0

    @pl.kernel(out_type=jax.ShapeDtypeStruct((M, D), x.dtype), mesh=mesh, name="sc_gather_rows",
               scratch_types=[pltpu.VMEM((rows_per,), jnp.int32), pltpu.VMEM((W, D), x.dtype)])
    def gather_kernel(x_hbm, i_hbm, o_hbm, idx_vmem, buf):
        wid = lax.axis_index("core") * mesh.num_subcores + lax.axis_index("subcore")
        base = wid * rows_per
        pltpu.sync_copy(i_hbm.at[pl.ds(base, rows_per)], idx_vmem)

        @pl.loop(0, rows_per // W)
        def _(j):
            pltpu.sync_copy(x_hbm.at[idx_vmem.at[pl.ds(j * W, W)]], buf)
            pltpu.sync_copy(buf, o_hbm.at[pl.ds(base + j * W, W)])

    return gather_kernel(x, idx)


def _sc_scatter_rows(x, idx, n_out):
    K, N = idx.shape
    D = x.shape[1]
    W = SC_GATHER_WINDOW
    mesh = plsc.VectorSubcoreMesh(core_axis_name="core", subcore_axis_name="subcore")
    n_workers = mesh.num_cores * mesh.num_subcores
    rows_per = N // n_workers
    assert rows_per * n_workers == N and rows_per % W == 0

    @pl.kernel(out_type=jax.ShapeDtypeStruct((n_out, D), x.dtype), mesh=mesh, name="sc_scatter_rows",
               scratch_types=[pltpu.VMEM((K * rows_per,), jnp.int32), pltpu.VMEM((W, D), x.dtype),
                              pltpu.SemaphoreType.DMA((K,))])
    def scatter_kernel(x_hbm, i_hbm, o_hbm, idx_vmem, buf, sem):
        wid = lax.axis_index("core") * mesh.num_subcores + lax.axis_index("subcore")
        base = wid * rows_per
        for k in range(K):
            pltpu.sync_copy(i_hbm.at[pl.ds(k * N + base, rows_per)], idx_vmem.at[pl.ds(k * rows_per, rows_per)])

        @pl.loop(0, rows_per // W)
        def _(j):
            pltpu.sync_copy(x_hbm.at[pl.ds(base + j * W, W)], buf)
            copies = [pltpu.make_async_copy(buf, o_hbm.at[idx_vmem.at[pl.ds(k * rows_per + j * W, W)]], sem.at[k])
                      for k in range(K)]
            for cp in copies:
                cp.start()
            for cp in copies:
                cp.wait()

    return scatter_kernel(x, idx.reshape(K * N))


def _combine_kernel(x_ref, yg_ref, gate_ref, ln3g_ref, ln3b_ref, *rest):
    out_ref = rest[-1]
    x = x_ref[...]
    g = gate_ref[...]
    y = jnp.zeros_like(x)
    for k in range(TOP_K):
        y = y + g[:, k:k + 1] * _unpack_bf16_pairs(yg_ref[k * CMB_TILE:(k + 1) * CMB_TILE, :])
    out_ref[...] = _layer_norm(DEEPNORM_ALPHA * x + y, ln3g_ref[...], ln3b_ref[...])


def _combine(x2, yg, gate, p, l, part, acc):
    N, D = x2.shape
    T = CMB_TILE
    tiles = yg.shape[0] // (TOP_K * T)
    first = part * tiles
    in_specs = [pl.BlockSpec((T, D), lambda i: (first + i, 0)), pl.BlockSpec((TOP_K * T, D // 2), lambda i: (i, 0)),
                pl.BlockSpec((T, TOP_K), lambda i: (first + i, 0)), _layer_spec(p['ln3g'], l), _layer_spec(p['ln3b'], l)]
    args = [x2, yg, gate, p['ln3g'], p['ln3b']]
    aliases = {}
    if acc is not None:
        in_specs.append(pl.BlockSpec(memory_space=pl.ANY))
        args.append(acc)
        aliases = {len(args) - 1: 0}
    return pl.pallas_call(
        _combine_kernel,
        out_shape=jax.ShapeDtypeStruct((N, D), F32),
        grid=(tiles,),
        in_specs=in_specs,
        out_specs=pl.BlockSpec((T, D), lambda i: (first + i, 0)),
        input_output_aliases=aliases,
        compiler_params=pltpu.CompilerParams(dimension_semantics=("arbitrary",), vmem_limit_bytes=VMEM_LIMIT),
        name="combine",
    )(*args)


def _prep(w_in, b_in, gla_wg2, gla_bg, gla_norm_g, sgu_ln_g, sgu_ln_b, sgu_ws, sgu_bs, pool_w, pool_scale,
          w_up_a, w_up_b, w_up_c, w_o, ln1_g, ln1_b, xa_wq, xa_wk, xa_wv, xa_wo, ln2_g, ln2_b,
          router_w, router_b, ln3_g, ln3_b):
    L = w_in.shape[0]
    o_glow = O_GLOW
    o_uv = o_glow + GLA_RANK
    row = lambda a: a.reshape(L, 1, -1).astype(F32)
    pad_last = lambda a, n: jnp.pad(a, [(0, 0)] * (a.ndim - 1) + [(0, n - a.shape[-1])])
    p = {}
    p['wcat'] = jnp.concatenate([w_in[..., :o_glow], pad_last(w_in[..., o_glow:o_uv], LANES), w_in[..., o_uv:]],
                                axis=-1).astype(BF16)
    p['bcat'] = row(jnp.concatenate([b_in[..., :o_glow], pad_last(b_in[..., o_glow:o_uv], LANES), b_in[..., o_uv:]],
                                    axis=-1))
    p['wg2'] = jnp.pad(gla_wg2, ((0, 0), (0, LANES - GLA_RANK), (0, 0))).astype(BF16)
    p['bg'] = row(gla_bg)
    p['gnorm'] = row(gla_norm_g)
    p['slng'] = row(sgu_ln_g)
    p['slnb'] = row(sgu_ln_b)
    p['wtril'] = jnp.tril(sgu_ws).astype(BF16)
    p['sbias'] = jnp.repeat(jnp.swapaxes(sgu_bs, 1, 2), SGU_GD, axis=2).astype(F32)
    G = len(POOL_WINDOWS)
    eye = jnp.eye(G, dtype=F32)
    p['poolw'] = jnp.einsum('lgcd,gh->lgchd', pool_w, eye).reshape(L, POOL_WIDTH, POOL_WIDTH).astype(BF16)
    p['pscale'] = row(pool_scale)
    p['wup'] = jnp.concatenate([w_up_a, w_up_b, w_up_c], axis=1).astype(BF16)
    p['wo'] = w_o.astype(BF16)
    p['ln1g'], p['ln1b'] = row(ln1_g), row(ln1_b)
    p['wq'] = xa_wq.astype(BF16)
    p['wkt'] = jnp.swapaxes(xa_wk, 1, 2).astype(BF16)
    p['wv'] = xa_wv.astype(BF16)
    p['wxo'] = xa_wo.astype(BF16)
    p['ln2g'], p['ln2b'] = row(ln2_g), row(ln2_b)
    rwt = jnp.swapaxes(router_w, 1, 2)
    rwt_hi = rwt.astype(BF16)
    p['rwt'] = jnp.concatenate([rwt_hi, (rwt - rwt_hi.astype(F32)).astype(BF16)], axis=1)
    p['rbt'] = jnp.broadcast_to(router_b[:, :, None], router_b.shape + (XA_TILE,)).astype(F32)
    p['ln3g'], p['ln3b'] = row(ln3_g), row(ln3_b)
    return p


def _route(route, counts):
    N = route.shape[1]
    top_idx = route[0:TOP_K].astype(jnp.int32)
    gate = route[TOP_K:2 * TOP_K].T
    rank = route[2 * TOP_K:3 * TOP_K].astype(jnp.int32)
    counts = counts[:, 0].astype(jnp.int32)
    padded = ((counts + MOE_BLOCK - 1) // MOE_BLOCK) * MOE_BLOCK
    pad_end = jnp.cumsum(padded)
    pad_start = pad_end - padded
    ids = jnp.arange(N_EXPERTS, dtype=jnp.int32)
    start_of = jnp.sum(jnp.where(top_idx[:, :, None] == ids[None, None, :], pad_start[None, None, :], 0), axis=-1)
    dest = start_of + rank
    n_blocks = N * TOP_K // MOE_BLOCK + N_EXPERTS
    block_start = jnp.arange(n_blocks, dtype=jnp.int32) * MOE_BLOCK
    block_expert = jnp.minimum(jnp.sum((pad_end[None, :] <= block_start[:, None]).astype(jnp.int32), axis=1),
                               N_EXPERTS - 1)
    n_used = (pad_end[-1] // MOE_BLOCK).astype(jnp.int32).reshape(1)
    later = jnp.where((ids[None, :] > ids[:, None]) & (counts[None, :] > 0), ids[None, :], N_EXPERTS)
    first_later = jnp.min(later, axis=1)
    next_expert = jnp.where(first_later < N_EXPERTS, first_later, ids).astype(jnp.int32)
    return gate, dest, block_expert, n_used, next_expert


def kernel(x, mem, w_in, b_in, gla_wg2, gla_bg, gla_norm_g, sgu_ln_g, sgu_ln_b, sgu_ws, sgu_bs, pool_w, pool_scale, w_up_a, w_up_b, w_up_c, w_o, ln1_g, ln1_b, xa_wq, xa_wk, xa_wv, xa_wo, ln2_g, ln2_b, router_w, router_b, exp_w_gu, exp_b_gu, exp_w_down, exp_b_down, ln3_g, ln3_b):
    B, S, D = x.shape
    N = B * S
    p = _prep(w_in, b_in, gla_wg2, gla_bg, gla_norm_g, sgu_ln_g, sgu_ln_b, sgu_ws, sgu_bs, pool_w, pool_scale,
              w_up_a, w_up_b, w_up_c, w_o, ln1_g, ln1_b, xa_wq, xa_wk, xa_wv, xa_wo, ln2_g, ln2_b,
              router_w, router_b, ln3_g, ln3_b)
    memt = jnp.swapaxes(mem, 1, 2)
    for l in range(DEPTH):
        x1 = _mixer(x, p, l)
        kt, v = _memkv(mem, memt, p, l)
        x2, x2p, route, counts = _xattn(x1, kt, v, p, l)
        gate, dest, block_expert, n_used, next_expert = _route(route, counts)
        n_slots = (N * TOP_K // MOE_BLOCK + N_EXPERTS) * MOE_BLOCK
        xs = _sc_scatter_rows(x2p.reshape(N, D // 2), dest, n_slots)
        ys = _experts(block_expert, n_used, next_expert, xs, exp_w_gu, exp_b_gu, exp_w_down, exp_b_down, l)
        dest_km = dest.reshape(TOP_K, N // CMB_TILE, CMB_TILE).transpose(1, 0, 2).reshape(-1)
        rows = dest_km.shape[0] // COMBINE_PARTS
        acc = None
        for part in range(COMBINE_PARTS):
            yg = _sc_gather_rows(ys, dest_km[part * rows:(part + 1) * rows])
            acc = _combine(x2.reshape(N, D), yg, gate, p, l, part, acc)
        x = acc.reshape(B, S, D)
    return x
```

```python
import functools

import jax
import jax.numpy as jnp
from jax import lax
from jax.experimental import pallas as pl
from jax.experimental.pallas import tpu as pltpu
from jax.experimental.pallas import tpu_sc as plsc

F32 = jnp.float32
BF16 = jnp.bfloat16

D_MODEL = 1024
DEPTH = 2
GLA_HEADS = 4
GLA_KEY = 256
GLA_VAL = 512
GLA_DK = 64
GLA_DV = 128
GLA_RANK = 16
GLA_TAU = 16.0
GLA_CHUNK = 64
SGU_GROUPS = 4
SGU_WIDTH = 256
SGU_GD = 64
SGU_CHUNK = 128
POOL_WINDOWS = (2, 4, 8, 16)
POOL_WIDTH = 256
POOL_GD = 64
POOL_CARRY = 32
MEM_LEN = 256
XA_HEADS = 4
XA_DH = 256
N_EXPERTS = 32
TOP_K = 4
EXPERT_FF = 1024
SWIGLU_LIMIT = 7.0
SWIGLU_ALPHA = 1.702
DEEPNORM_ALPHA = (2 * DEPTH) ** 0.25
LN_EPS = 1e-5
LANES = 128
VMEM_LIMIT = 56 * 1024 * 1024

MIX_TILE = 256
XA_TILE = 1024
MOE_BLOCK = 256
CMB_TILE = 256
COMBINE_PARTS = 4
ROUTE_ROWS = 16
SC_GATHER_WINDOW = 128

O_QKVR = 0
O_GLOW = 2 * GLA_KEY + 2 * GLA_VAL
O_UV = O_GLOW + LANES
O_XC = O_UV + 2 * SGU_WIDTH
O_GATE = O_XC + POOL_WIDTH
N_PACK = O_GATE + 3 * D_MODEL


def _dot(a, b):
    return jnp.dot(a, b, preferred_element_type=F32)


def _dot_t0(a, b):
    return lax.dot_general(a, b, (((0,), (0,)), ((), ())), preferred_element_type=F32)


def _dot_t1(a, b):
    return lax.dot_general(a, b, (((1,), (1,)), ((), ())), preferred_element_type=F32)


def _split_bf16(x):
    hi = x.astype(BF16)
    lo = (x - hi.astype(F32)).astype(BF16)
    return hi, lo


def _layer_norm(x, g, b):
    mu = jnp.mean(x, axis=-1, keepdims=True)
    xc = x - mu
    var = jnp.mean(xc * xc, axis=-1, keepdims=True)
    return xc * lax.rsqrt(var + LN_EPS) * g + b


def _sigmoid(x):
    return 1.0 / (1.0 + jnp.exp(-x))


def _pack_bf16_pairs(x):
    H = x.shape[1] // 2
    bits = lax.bitcast_convert_type(x.astype(BF16).astype(F32), jnp.uint32)
    return (bits[:, :H] >> 16) | (bits[:, H:] & jnp.uint32(0xFFFF0000))


def _unpack_bf16_pairs(w):
    lo = lax.bitcast_convert_type(w << 16, F32)
    hi = lax.bitcast_convert_type(w & jnp.uint32(0xFFFF0000), F32)
    return jnp.concatenate([lo, hi], axis=1)


def _layer_spec(arr, l):
    nd = arr.ndim - 1
    return pl.BlockSpec((None,) + arr.shape[1:], lambda *_: (l,) + (0,) * nd, pipeline_mode=pl.Buffered(1))


def _mixer_kernel(x_ref, wcat_ref, bcat_ref, wg2_ref, bg_ref, gnorm_ref,
                  slng_ref, slnb_ref, wtril_ref, sbias_ref, poolw_ref, pscale_ref,
                  wup_ref, wo_ref, ln1g_ref, ln1b_ref,
                  out_ref,
                  state_ref, qkvr_ref, ya_ref, vln_ref, e_ref, s2_ref, s4_ref, s8_ref):
    T = MIX_TILE
    D = D_MODEL
    j = pl.program_id(1)
    x = x_ref[0]
    xb = x.astype(BF16)

    def proj(lo, hi):
        return _dot(xb, wcat_ref[:, lo:hi]) + bcat_ref[:, lo:hi]

    @pl.when(j == 0)
    def _():
        state_ref[...] = jnp.zeros_like(state_ref)
        e_ref[0:POOL_CARRY, :] = jnp.zeros((POOL_CARRY, POOL_WIDTH), F32)

    @pl.when(j > 0)
    def _():
        e_ref[0:POOL_CARRY, :] = e_ref[T:T + POOL_CARRY, :]

    qkvr_ref[...] = proj(O_QKVR, O_GLOW)
    glow = proj(O_GLOW, O_UV)
    z = _dot(glow.astype(BF16), wg2_ref[...]) + bg_ref[...]
    la = (jnp.minimum(z, 0.0) - jnp.log1p(jnp.exp(-jnp.abs(z)))) * (1.0 / GLA_TAU)
    la_hi, la_lo = _split_bf16(la)
    gate_a = _sigmoid(proj(O_GATE, O_GATE + D))

    C = GLA_CHUNK
    NC = T // C
    CSH = C.bit_length() - 1
    row = lax.broadcasted_iota(jnp.int32, (T, T), 0)
    col = lax.broadcasted_iota(jnp.int32, (T, T), 1)
    same_chunk = (row >> CSH) == (col >> CSH)
    causal = same_chunk & (row >= col)
    causal_bf = jnp.where(causal, 1.0, 0.0).astype(BF16)
    chunk_bf = jnp.where(same_chunk, 1.0, 0.0).astype(BF16)
    b = _dot(causal_bf, la_hi) + _dot(causal_bf, la_lo)
    b_end = _dot(chunk_bf, la_hi) + _dot(chunk_bf, la_lo)
    q = qkvr_ref[:, 0:GLA_KEY]
    k = qkvr_ref[:, GLA_KEY:2 * GLA_KEY]
    v = qkvr_ref[:, 2 * GLA_KEY:2 * GLA_KEY + GLA_VAL].astype(BF16)
    q_dec = q * (GLA_DK ** -0.5) * jnp.exp(b)
    k_dec = (k * jnp.exp(-b)).astype(BF16)
    k_tail = (k * jnp.exp(b_end - b)).astype(BF16)
    q_dec_bf = q_dec.astype(BF16)
    lane = lax.broadcasted_iota(jnp.int32, (T, GLA_KEY), 1)
    o_heads = []
    for h in range(GLA_HEADS):
        q_h = jnp.where((lane >= h * GLA_DK) & (lane < (h + 1) * GLA_DK), q_dec, 0.0).astype(BF16)
        scores = jnp.where(causal, _dot_t1(q_h, k_dec), 0.0).astype(BF16)
        o_heads.append(_dot(scores, v[:, h * GLA_DV:(h + 1) * GLA_DV]))
    o_intra = jnp.concatenate(o_heads, axis=1)
    gate_b = _sigmoid(proj(O_GATE + D, O_GATE + 2 * D))
    ind = jnp.where((lax.broadcasted_iota(jnp.int32, (T, NC * LANES), 0) >> CSH)
                    == (lax.broadcasted_iota(jnp.int32, (T, NC * LANES), 1) >> 7), 1.0, 0.0).astype(BF16)
    dec_all = jnp.exp(_dot_t0(la_hi, ind) + _dot_t0(la_lo, ind))
    o_inter = []
    for c in range(NC):
        rows = slice(c * C, (c + 1) * C)
        o_inter.append(_dot(q_dec_bf[rows], state_ref[...].astype(BF16)))
        kv = _dot_t0(k_tail[rows], v[rows])
        for h in range(GLA_HEADS):
            rs = slice(h * GLA_DK, (h + 1) * GLA_DK)
            cs = slice(h * GLA_DV, (h + 1) * GLA_DV)
            state_ref[rs, cs] = dec_all[rs, c * LANES:(c + 1) * LANES] * state_ref[rs, cs] + kv[rs, cs]
    o = o_intra + jnp.concatenate(o_inter, axis=0)
    for h in range(GLA_HEADS):
        cs = slice(h * GLA_DV, (h + 1) * GLA_DV)
        o_h = o[:, cs]
        ms = jnp.mean(o_h * o_h, axis=-1, keepdims=True)
        o_h = o_h * lax.rsqrt(ms + LN_EPS) * gnorm_ref[:, cs]
        r_h = qkvr_ref[:, 2 * GLA_KEY + GLA_VAL + h * GLA_DV:2 * GLA_KEY + GLA_VAL + (h + 1) * GLA_DV]
        ya_ref[:, cs] = (o_h * (r_h * _sigmoid(r_h))).astype(BF16)

    gate_c = _sigmoid(proj(O_GATE + 2 * D, O_GATE + 3 * D))
    uv = proj(O_UV, O_XC)
    zg = 0.5 * uv * (1.0 + lax.erf(uv * (2.0 ** -0.5)))
    u = zg[:, :SGU_WIDTH]
    vln_ref[...] = _layer_norm(zg[:, SGU_WIDTH:], slng_ref[...], slnb_ref[...])
    lane_s = lax.broadcasted_iota(jnp.int32, (SGU_CHUNK, SGU_WIDTH), 1)
    s_parts = []
    for n in range(T // SGU_CHUNK):
        vc = vln_ref[n * SGU_CHUNK:(n + 1) * SGU_CHUNK, :]
        s = sbias_ref[...]
        for g in range(SGU_GROUPS):
            vg = jnp.where((lane_s >= g * SGU_GD) & (lane_s < (g + 1) * SGU_GD), vc, 0.0).astype(BF16)
            s = s + _dot(wtril_ref[g], vg)
        s_parts.append(s)
    y_b = (u * jnp.concatenate(s_parts, axis=0)).astype(BF16)

    P = POOL_CARRY
    xc = proj(O_XC, O_GATE)
    e_ref[P:P + T, :] = xc
    s2_ref[8:P + T, :] = e_ref[8:P + T, :] + e_ref[7:P + T - 1, :]
    s4_ref[16:P + T, :] = s2_ref[16:P + T, :] + s2_ref[14:P + T - 2, :]
    s8_ref[24:P + T, :] = s4_ref[24:P + T, :] + s4_ref[20:P + T - 4, :]
    s16 = s8_ref[P:P + T, :] + s8_ref[P - 8:P + T - 8, :]
    lane_p = lax.broadcasted_iota(jnp.int32, (T, POOL_WIDTH), 1)
    tpos = lax.broadcasted_iota(jnp.int32, (T, POOL_WIDTH), 0) + (j * T + 1)
    grp = lane_p >> 6
    win = jnp.where(grp == 0, POOL_WINDOWS[0], jnp.where(grp == 1, POOL_WINDOWS[1],
                    jnp.where(grp == 2, POOL_WINDOWS[2], POOL_WINDOWS[3])))
    wsum = jnp.where(grp == 0, s2_ref[P:P + T, :], jnp.where(grp == 1, s4_ref[P:P + T, :],
                     jnp.where(grp == 2, s8_ref[P:P + T, :], s16)))
    count = jnp.minimum(tpos, win).astype(F32)
    pooled = wsum / count - xc
    y_c = (_dot(pooled.astype(BF16), poolw_ref[...]) * pscale_ref[...]).astype(BF16)

    ra, rb = GLA_VAL, GLA_VAL + SGU_WIDTH
    merged = gate_a * _dot(ya_ref[...], wup_ref[0:ra, :])
    merged += gate_b * _dot(y_b, wup_ref[ra:rb, :])
    merged += gate_c * _dot(y_c, wup_ref[rb:, :])
    h = _dot(merged.astype(BF16), wo_ref[...])
    out_ref[0] = _layer_norm(DEEPNORM_ALPHA * x + h, ln1g_ref[...], ln1b_ref[...])


_MIXER_WEIGHTS = ('wcat', 'bcat', 'wg2', 'bg', 'gnorm', 'slng', 'slnb', 'wtril', 'sbias', 'poolw', 'pscale',
                  'wup', 'wo', 'ln1g', 'ln1b')


def _mixer(x, p, l):
    B, S, D = x.shape
    T = MIX_TILE
    weights = [p[n] for n in _MIXER_WEIGHTS]
    return pl.pallas_call(
        _mixer_kernel,
        out_shape=jax.ShapeDtypeStruct((B, S, D), F32),
        grid=(B, S // T),
        in_specs=[pl.BlockSpec((1, T, D), lambda b, j: (b, j, 0))] + [_layer_spec(w, l) for w in weights],
        out_specs=pl.BlockSpec((1, T, D), lambda b, j: (b, j, 0)),
        scratch_shapes=[
            pltpu.VMEM((GLA_KEY, GLA_VAL), F32),
            pltpu.VMEM((T, 2 * GLA_KEY + 2 * GLA_VAL), F32),
            pltpu.VMEM((T, GLA_VAL), BF16),
            pltpu.VMEM((T, SGU_WIDTH), F32),
            pltpu.VMEM((T + POOL_CARRY, POOL_WIDTH), F32),
            pltpu.VMEM((T + POOL_CARRY, POOL_WIDTH), F32),
            pltpu.VMEM((T + POOL_CARRY, POOL_WIDTH), F32),
            pltpu.VMEM((T + POOL_CARRY, POOL_WIDTH), F32),
        ],
        compiler_params=pltpu.CompilerParams(dimension_semantics=("arbitrary", "arbitrary"),
                                             vmem_limit_bytes=VMEM_LIMIT),
        name="mixer",
    )(x, *weights)


def _memkv_kernel(memt_ref, mem_ref, wkt_ref, wv_ref, kt_ref, v_ref):
    kt_ref[0] = _dot(wkt_ref[...], memt_ref[0].astype(BF16)).astype(BF16)
    v_ref[0] = _dot(mem_ref[0].astype(BF16), wv_ref[...]).astype(BF16)


def _memkv(mem, memt, p, l):
    B, M, D = mem.shape
    return pl.pallas_call(
        _memkv_kernel,
        out_shape=(jax.ShapeDtypeStruct((B, D, M), BF16), jax.ShapeDtypeStruct((B, M, D), BF16)),
        grid=(B,),
        in_specs=[pl.BlockSpec((1, D, M), lambda b: (b, 0, 0)), pl.BlockSpec((1, M, D), lambda b: (b, 0, 0)),
                  _layer_spec(p['wkt'], l), _layer_spec(p['wv'], l)],
        out_specs=(pl.BlockSpec((1, D, M), lambda b: (b, 0, 0)), pl.BlockSpec((1, M, D), lambda b: (b, 0, 0))),
        compiler_params=pltpu.CompilerParams(dimension_semantics=("arbitrary",), vmem_limit_bytes=VMEM_LIMIT),
        name="memkv",
    )(memt, mem, p['wkt'], p['wv'])


def _xattn_kernel(x_ref, kt_ref, v_ref, wq_ref, wo_ref, ln2g_ref, ln2b_ref, rwt_ref, rbt_ref,
                  x2_ref, x2p_ref, route_ref, counts_ref, carry_ref):
    T = XA_TILE

    @pl.when((pl.program_id(0) == 0) & (pl.program_id(1) == 0))
    def _():
        carry_ref[...] = jnp.zeros_like(carry_ref)

    x = x_ref[0]
    q = (_dot(x.astype(BF16), wq_ref[...]) * (XA_DH ** -0.5)).astype(BF16)
    h = jnp.zeros_like(x)
    for hd in range(XA_HEADS):
        cs = slice(hd * XA_DH, (hd + 1) * XA_DH)
        s = _dot(q[:, cs], kt_ref[0, cs, :])
        e = jnp.exp(s - jnp.max(s, axis=-1, keepdims=True))
        o = _dot(e.astype(BF16), v_ref[0, :, cs]) / jnp.sum(e, axis=-1, keepdims=True)
        h = h + _dot(o.astype(BF16), wo_ref[cs, :])
    x2 = _layer_norm(DEEPNORM_ALPHA * x + h, ln2g_ref[...], ln2b_ref[...])
    x2_ref[0] = x2
    x2p_ref[0] = _pack_bf16_pairs(x2)

    E = N_EXPERTS
    hi, lo = _split_bf16(x2)
    lt = _dot_t1(rwt_ref[...], hi)
    logits = lt[0:E] + (lt[E:2 * E] + _dot_t1(rwt_ref[0:E, :], lo)) + rbt_ref[...]

    eid = lax.broadcasted_iota(jnp.int32, (E, T), 0)
    neg_inf = jnp.float32(-jnp.inf)
    rest = logits
    tops, picks = [], []
    for _ in range(TOP_K):
        m = jnp.max(rest, axis=0, keepdims=True)
        idx = jnp.min(jnp.where(rest == m, eid, E), axis=0, keepdims=True)
        pick = eid == idx
        rest = jnp.where(pick, neg_inf, rest)
        tops.append((m, idx))
        picks.append(pick)
    exps = [jnp.exp(m - tops[0][0]) for m, _ in tops]
    denom = exps[0]
    for e in exps[1:]:
        denom = denom + e

    chosen = jnp.zeros((E, T), F32)
    for pick in picks:
        chosen = chosen + jnp.where(pick, 1.0, 0.0)
    chosen_bf = chosen.astype(BF16)
    earlier = (lax.broadcasted_iota(jnp.int32, (T, T), 0) < lax.broadcasted_iota(jnp.int32, (T, T), 1))
    carry = carry_ref[...]
    before = _dot(chosen_bf, jnp.where(earlier, 1.0, 0.0).astype(BF16)) + jnp.concatenate([carry] * (T // LANES), axis=1)
    carry = carry + _dot(chosen_bf, jnp.ones((T, LANES), BF16))
    carry_ref[...] = carry
    counts_ref[...] = carry

    rid = lax.broadcasted_iota(jnp.int32, (ROUTE_ROWS, T), 0)
    route = jnp.zeros((ROUTE_ROWS, T), F32)
    for k in range(TOP_K):
        rank = jnp.sum(jnp.where(picks[k], before, 0.0), axis=0, keepdims=True)
        route = jnp.where(rid == k, tops[k][1].astype(F32), route)
        route = jnp.where(rid == TOP_K + k, exps[k] / denom, route)
        route = jnp.where(rid == 2 * TOP_K + k, rank, route)
    route_ref[...] = route


def _xattn(x, kt, v, p, l):
    B, S, D = x.shape
    T = XA_TILE
    M = MEM_LEN
    weights = [p[n] for n in ('wq', 'wxo', 'ln2g', 'ln2b', 'rwt', 'rbt')]
    return pl.pallas_call(
        _xattn_kernel,
        out_shape=(jax.ShapeDtypeStruct((B, S, D), F32), jax.ShapeDtypeStruct((B, S, D // 2), jnp.uint32),
                   jax.ShapeDtypeStruct((ROUTE_ROWS, B * S), F32), jax.ShapeDtypeStruct((N_EXPERTS, LANES), F32)),
        grid=(B, S // T),
        in_specs=[pl.BlockSpec((1, T, D), lambda b, j: (b, j, 0)),
                  pl.BlockSpec((1, D, M), lambda b, j: (b, 0, 0)),
                  pl.BlockSpec((1, M, D), lambda b, j: (b, 0, 0))] + [_layer_spec(w, l) for w in weights],
        out_specs=(pl.BlockSpec((1, T, D), lambda b, j: (b, j, 0)),
                   pl.BlockSpec((1, T, D // 2), lambda b, j: (b, j, 0)),
                   pl.BlockSpec((ROUTE_ROWS, T), lambda b, j: (0, b * (S // T) + j)),
                   pl.BlockSpec((N_EXPERTS, LANES), lambda b, j: (0, 0))),
        scratch_shapes=[pltpu.VMEM((N_EXPERTS, LANES), F32)],
        compiler_params=pltpu.CompilerParams(dimension_semantics=("arbitrary", "arbitrary"),
                                             vmem_limit_bytes=VMEM_LIMIT),
        name="xattn",
    )(x, kt, v, *weights)


def _expert_kernel(layer, be_ref, nb_ref, nxt_ref, xs_ref, wgu_hbm, bgu_ref, wd_hbm, bd_ref, out_ref,
                   wgu_st, wd_st, wgu_bf, wd_bf, sem):
    i = pl.program_id(0)
    F = EXPERT_FF

    def weight_copies(e):
        return (pltpu.make_async_copy(wgu_hbm.at[layer, e], wgu_st, sem.at[0]),
                pltpu.make_async_copy(wd_hbm.at[layer, e], wd_st, sem.at[1]))

    @pl.when(i < nb_ref[0])
    def _():
        e = be_ref[i]
        prev = be_ref[jnp.maximum(i - 1, 0)]

        @pl.when(i == 0)
        def _():
            for cp in weight_copies(e):
                cp.start()

        @pl.when((i == 0) | (e != prev))
        def _():
            for cp in weight_copies(e):
                cp.wait()
            wgu_bf[...] = wgu_st[...].astype(BF16)
            wd_bf[...] = wd_st[...].astype(BF16)
            nxt = nxt_ref[e]

            @pl.when(nxt != e)
            def _():
                for cp in weight_copies(nxt):
                    cp.start()

        hh = _dot(_unpack_bf16_pairs(xs_ref[...]).astype(BF16), wgu_bf[...]) + bgu_ref[...]
        h_glu = jnp.minimum(hh[:, :F], SWIGLU_LIMIT)
        h_lin = jnp.clip(hh[:, F:], -SWIGLU_LIMIT, SWIGLU_LIMIT)
        a = h_glu * _sigmoid(SWIGLU_ALPHA * h_glu) * (h_lin + 1.0)
        out_ref[...] = _pack_bf16_pairs(_dot(a.astype(BF16), wd_bf[...]) + bd_ref[...])


def _experts(block_expert, n_used, next_expert, xs, w_gu, b_gu, w_down, b_down, l):
    P, DH = xs.shape
    D = 2 * DH
    NB = P // MOE_BLOCK
    F2 = 2 * EXPERT_FF

    def row_map(i, be, nb, nxt):
        return (jnp.minimum(i, nb[0] - 1), 0)

    def exp_map(i, be, nb, nxt):
        return (l, be[jnp.minimum(i, nb[0] - 1)], 0, 0)

    grid_spec = pltpu.PrefetchScalarGridSpec(
        num_scalar_prefetch=3,
        grid=(NB,),
        in_specs=[pl.BlockSpec((MOE_BLOCK, DH), row_map),
                  pl.BlockSpec(memory_space=pl.ANY),
                  pl.BlockSpec((None, None, 1, F2), exp_map),
                  pl.BlockSpec(memory_space=pl.ANY),
                  pl.BlockSpec((None, None, 1, D), exp_map)],
        out_specs=pl.BlockSpec((MOE_BLOCK, DH), row_map),
        scratch_shapes=[pltpu.VMEM((D, F2), F32), pltpu.VMEM((EXPERT_FF, D), F32),
                        pltpu.VMEM((D, F2), BF16), pltpu.VMEM((EXPERT_FF, D), BF16),
                        pltpu.SemaphoreType.DMA((2,))],
    )
    return pl.pallas_call(
        functools.partial(_expert_kernel, l),
        out_shape=jax.ShapeDtypeStruct((P, DH), jnp.uint32),
        grid_spec=grid_spec,
        compiler_params=pltpu.CompilerParams(dimension_semantics=("arbitrary",), vmem_limit_bytes=VMEM_LIMIT),
        name="experts",
    )(block_expert, n_used, next_expert, xs, w_gu, b_gu.reshape(DEPTH, N_EXPERTS, 1, F2), w_down,
      b_down.reshape(DEPTH, N_EXPERTS, 1, D))


def _sc_gather_rows(x, idx):
    M = idx.shape[0]
    D = x.shape[1]
    W = SC_GATHER_WINDOW
    mesh = plsc.VectorSubcoreMesh(core_axis_name="core", subcore_axis_name="subcore")
    n_workers = mesh.num_cores * mesh.num_subcores
    rows_per = M // n_workers
    assert rows_per * n_workers == M and rows_per % W == 0

    @pl.kernel(out_type=jax.ShapeDtypeStruct((M, D), x.dtype), mesh=mesh, name="sc_gather_rows",
               scratch_types=[pltpu.VMEM((rows_per,), jnp.int32), pltpu.VMEM((W, D), x.dtype)])
    def gather_kernel(x_hbm, i_hbm, o_hbm, idx_vmem, buf):
        wid = lax.axis_index("core") * mesh.num_subcores + lax.axis_index("subcore")
        base = wid * rows_per
        pltpu.sync_copy(i_hbm.at[pl.ds(base, rows_per)], idx_vmem)

        @pl.loop(0, rows_per // W)
        def _(j):
            pltpu.sync_copy(x_hbm.at[idx_vmem.at[pl.ds(j * W, W)]], buf)
            pltpu.sync_copy(buf, o_hbm.at[pl.ds(base + j * W, W)])

    return gather_kernel(x, idx)


def _sc_scatter_rows(x, idx, n_out):
    K, N = idx.shape
    D = x.shape[1]
    W = SC_GATHER_WINDOW
    mesh = plsc.VectorSubcoreMesh(core_axis_name="core", subcore_axis_name="subcore")
    n_workers = mesh.num_cores * mesh.num_subcores
    rows_per = N // n_workers
    assert rows_per * n_workers == N and rows_per % W == 0

    @pl.kernel(out_type=jax.ShapeDtypeStruct((n_out, D), x.dtype), mesh=mesh, name="sc_scatter_rows",
               scratch_types=[pltpu.VMEM((K * rows_per,), jnp.int32), pltpu.VMEM((W, D), x.dtype),
                              pltpu.SemaphoreType.DMA((K,))])
    def scatter_kernel(x_hbm, i_hbm, o_hbm, idx_vmem, buf, sem):
        wid = lax.axis_index("core") * mesh.num_subcores + lax.axis_index("subcore")
        base = wid * rows_per
        for k in range(K):
            pltpu.sync_copy(i_hbm.at[pl.ds(k * N + base, rows_per)], idx_vmem.at[pl.ds(k * rows_per, rows_per)])

        @pl.loop(0, rows_per // W)
        def _(j):
            pltpu.sync_copy(x_hbm.at[pl.ds(base + j * W, W)], buf)
            copies = [pltpu.make_async_copy(buf, o_hbm.at[idx_vmem.at[pl.ds(k * rows_per + j * W, W)]], sem.at[k])
                      for k in range(K)]
            for cp in copies:
                cp.start()
            for cp in copies:
                cp.wait()

    return scatter_kernel(x, idx.reshape(K * N))


def _combine_kernel(x_ref, yg_ref, gate_ref, ln3g_ref, ln3b_ref, *rest):
    out_ref = rest[-1]
    x = x_ref[...]
    g = gate_ref[...]
    y = jnp.zeros_like(x)
    for k in range(TOP_K):
        y = y + g[:, k:k + 1] * _unpack_bf16_pairs(yg_ref[k * CMB_TILE:(k + 1) * CMB_TILE, :])
    out_ref[...] = _layer_norm(DEEPNORM_ALPHA * x + y, ln3g_ref[...], ln3b_ref[...])


def _combine(x2, yg, gate, p, l, part, acc):
    N, D = x2.shape
    T = CMB_TILE
    tiles = yg.shape[0] // (TOP_K * T)
    first = part * tiles
    in_specs = [pl.BlockSpec((T, D), lambda i: (first + i, 0)), pl.BlockSpec((TOP_K * T, D // 2), lambda i: (i, 0)),
                pl.BlockSpec((T, TOP_K), lambda i: (first + i, 0)), _layer_spec(p['ln3g'], l), _layer_spec(p['ln3b'], l)]
    args = [x2, yg, gate, p['ln3g'], p['ln3b']]
    aliases = {}
    if acc is not None:
        in_specs.append(pl.BlockSpec(memory_space=pl.ANY))
        args.append(acc)
        aliases = {len(args) - 1: 0}
    return pl.pallas_call(
        _combine_kernel,
        out_shape=jax.ShapeDtypeStruct((N, D), F32),
        grid=(tiles,),
        in_specs=in_specs,
        out_specs=pl.BlockSpec((T, D), lambda i: (first + i, 0)),
        input_output_aliases=aliases,
        compiler_params=pltpu.CompilerParams(dimension_semantics=("arbitrary",), vmem_limit_bytes=VMEM_LIMIT),
        name="combine",
    )(*args)


def _prep(w_in, b_in, gla_wg2, gla_bg, gla_norm_g, sgu_ln_g, sgu_ln_b, sgu_ws, sgu_bs, pool_w, pool_scale,
          w_up_a, w_up_b, w_up_c, w_o, ln1_g, ln1_b, xa_wq, xa_wk, xa_wv, xa_wo, ln2_g, ln2_b,
          router_w, router_b, ln3_g, ln3_b):
    L = w_in.shape[0]
    o_glow = O_GLOW
    o_uv = o_glow + GLA_RANK
    row = lambda a: a.reshape(L, 1, -1).astype(F32)
    pad_last = lambda a, n: jnp.pad(a, [(0, 0)] * (a.ndim - 1) + [(0, n - a.shape[-1])])
    p = {}
    p['wcat'] = jnp.concatenate([w_in[..., :o_glow], pad_last(w_in[..., o_glow:o_uv], LANES), w_in[..., o_uv:]],
                                axis=-1).astype(BF16)
    p['bcat'] = row(jnp.concatenate([b_in[..., :o_glow], pad_last(b_in[..., o_glow:o_uv], LANES), b_in[..., o_uv:]],
                                    axis=-1))
    p['wg2'] = jnp.pad(gla_wg2, ((0, 0), (0, LANES - GLA_RANK), (0, 0))).astype(BF16)
    p['bg'] = row(gla_bg)
    p['gnorm'] = row(gla_norm_g)
    p['slng'] = row(sgu_ln_g)
    p['slnb'] = row(sgu_ln_b)
    p['wtril'] = jnp.tril(sgu_ws).astype(BF16)
    p['sbias'] = jnp.repeat(jnp.swapaxes(sgu_bs, 1, 2), SGU_GD, axis=2).astype(F32)
    G = len(POOL_WINDOWS)
    eye = jnp.eye(G, dtype=F32)
    p['poolw'] = jnp.einsum('lgcd,gh->lgchd', pool_w, eye).reshape(L, POOL_WIDTH, POOL_WIDTH).astype(BF16)
    p['pscale'] = row(pool_scale)
    p['wup'] = jnp.concatenate([w_up_a, w_up_b, w_up_c], axis=1).astype(BF16)
    p['wo'] = w_o.astype(BF16)
    p['ln1g'], p['ln1b'] = row(ln1_g), row(ln1_b)
    p['wq'] = xa_wq.astype(BF16)
    p['wkt'] = jnp.swapaxes(xa_wk, 1, 2).astype(BF16)
    p['wv'] = xa_wv.astype(BF16)
    p['wxo'] = xa_wo.astype(BF16)
    p['ln2g'], p['ln2b'] = row(ln2_g), row(ln2_b)
    rwt = jnp.swapaxes(router_w, 1, 2)
    rwt_hi = rwt.astype(BF16)
    p['rwt'] = jnp.concatenate([rwt_hi, (rwt - rwt_hi.astype(F32)).astype(BF16)], axis=1)
    p['rbt'] = jnp.broadcast_to(router_b[:, :, None], router_b.shape + (XA_TILE,)).astype(F32)
    p['ln3g'], p['ln3b'] = row(ln3_g), row(ln3_b)
    return p


def _route(route, counts):
    N = route.shape[1]
    top_idx = route[0:TOP_K].astype(jnp.int32)
    gate = route[TOP_K:2 * TOP_K].T
    rank = route[2 * TOP_K:3 * TOP_K].astype(jnp.int32)
    counts = counts[:, 0].astype(jnp.int32)
    padded = ((counts + MOE_BLOCK - 1) // MOE_BLOCK) * MOE_BLOCK
    pad_end = jnp.cumsum(padded)
    pad_start = pad_end - padded
    ids = jnp.arange(N_EXPERTS, dtype=jnp.int32)
    start_of = jnp.sum(jnp.where(top_idx[:, :, None] == ids[None, None, :], pad_start[None, None, :], 0), axis=-1)
    dest = start_of + rank
    n_blocks = N * TOP_K // MOE_BLOCK + N_EXPERTS
    block_start = jnp.arange(n_blocks, dtype=jnp.int32) * MOE_BLOCK
    block_expert = jnp.minimum(jnp.sum((pad_end[None, :] <= block_start[:, None]).astype(jnp.int32), axis=1),
                               N_EXPERTS - 1)
    n_used = (pad_end[-1] // MOE_BLOCK).astype(jnp.int32).reshape(1)
    later = jnp.where((ids[None, :] > ids[:, None]) & (counts[None, :] > 0), ids[None, :], N_EXPERTS)
    first_later = jnp.min(later, axis=1)
    next_expert = jnp.where(first_later < N_EXPERTS, first_later, ids).astype(jnp.int32)
    return gate, dest, block_expert, n_used, next_expert


def kernel(x, mem, w_in, b_in, gla_wg2, gla_bg, gla_norm_g, sgu_ln_g, sgu_ln_b, sgu_ws, sgu_bs, pool_w, pool_scale, w_up_a, w_up_b, w_up_c, w_o, ln1_g, ln1_b, xa_wq, xa_wk, xa_wv, xa_wo, ln2_g, ln2_b, router_w, router_b, exp_w_gu, exp_b_gu, exp_w_down, exp_b_down, ln3_g, ln3_b):
    B, S, D = x.shape
    N = B * S
    p = _prep(w_in, b_in, gla_wg2, gla_bg, gla_norm_g, sgu_ln_g, sgu_ln_b, sgu_ws, sgu_bs, pool_w, pool_scale,
              w_up_a, w_up_b, w_up_c, w_o, ln1_g, ln1_b, xa_wq, xa_wk, xa_wv, xa_wo, ln2_g, ln2_b,
              router_w, router_b, ln3_g, ln3_b)
    memt = jnp.swapaxes(mem, 1, 2)
    for l in range(DEPTH):
        x1 = _mixer(x, p, l)
        kt, v = _memkv(mem, memt, p, l)
        x2, x2p, route, counts = _xattn(x1, kt, v, p, l)
        gate, dest, block_expert, n_used, next_expert = _route(route, counts)
        n_slots = (N * TOP_K // MOE_BLOCK + N_EXPERTS) * MOE_BLOCK
        xs = _sc_scatter_rows(x2p.reshape(N, D // 2), dest, n_slots)
        ys = _experts(block_expert, n_used, next_expert, xs, exp_w_gu, exp_b_gu, exp_w_down, exp_b_down, l)
        dest_km = dest.reshape(TOP_K, N // CMB_TILE, CMB_TILE).transpose(1, 0, 2).reshape(-1)
        rows = dest_km.shape[0] // COMBINE_PARTS
        acc = None
        for part in range(COMBINE_PARTS):
            yg = _sc_gather_rows(ys, dest_km[part * rows:(part + 1) * rows])
            acc = _combine(x2.reshape(N, D), yg, gate, p, l, part, acc)
        x = acc.reshape(B, S, D)
    return x
```

```python
import functools

import jax
import jax.numpy as jnp
from jax import lax
from jax.experimental import pallas as pl
from jax.experimental.pallas import tpu as pltpu
from jax.experimental.pallas import tpu_sc as plsc

F32 = jnp.float32
BF16 = jnp.bfloat16

D_MODEL = 1024
DEPTH = 2
GLA_HEADS = 4
GLA_KEY = 256
GLA_VAL = 512
GLA_DK = 64
GLA_DV = 128
GLA_RANK = 16
GLA_TAU = 16.0
GLA_CHUNK = 64
SGU_GROUPS = 4
SGU_WIDTH = 256
SGU_GD = 64
SGU_CHUNK = 128
POOL_WINDOWS = (2, 4, 8, 16)
POOL_WIDTH = 256
POOL_GD = 64
POOL_CARRY = 32
MEM_LEN = 256
XA_HEADS = 4
XA_DH = 256
N_EXPERTS = 32
TOP_K = 4
EXPERT_FF = 1024
SWIGLU_LIMIT = 7.0
SWIGLU_ALPHA = 1.702
DEEPNORM_ALPHA = (2 * DEPTH) ** 0.25
LN_EPS = 1e-5
LANES = 128
VMEM_LIMIT = 56 * 1024 * 1024

MIX_TILE = 512
GLA_SUB = 256
XA_TILE = 1024
MOE_BLOCK = 256
CMB_TILE = 256
COMBINE_PARTS = 4
ROUTE_ROWS = 16
SC_GATHER_WINDOW = 128

O_QKVR = 0
O_GLOW = 2 * GLA_KEY + 2 * GLA_VAL
O_UV = O_GLOW + LANES
O_XC = O_UV + 2 * SGU_WIDTH
O_GATE = O_XC + POOL_WIDTH
N_PACK = O_GATE + 3 * D_MODEL


def _dot(a, b):
    return jnp.dot(a, b, preferred_element_type=F32)


def _dot_t0(a, b):
    return lax.dot_general(a, b, (((0,), (0,)), ((), ())), preferred_element_type=F32)


def _dot_t1(a, b):
    return lax.dot_general(a, b, (((1,), (1,)), ((), ())), preferred_element_type=F32)


def _split_bf16(x):
    hi = x.astype(BF16)
    lo = (x - hi.astype(F32)).astype(BF16)
    return hi, lo


def _layer_norm(x, g, b):
    mu = jnp.mean(x, axis=-1, keepdims=True)
    xc = x - mu
    var = jnp.mean(xc * xc, axis=-1, keepdims=True)
    return xc * lax.rsqrt(var + LN_EPS) * g + b


def _sigmoid(x):
    return 1.0 / (1.0 + jnp.exp(-x))


def _pack_bf16_pairs(x):
    H = x.shape[1] // 2
    bits = lax.bitcast_convert_type(x.astype(BF16).astype(F32), jnp.uint32)
    return (bits[:, :H] >> 16) | (bits[:, H:] & jnp.uint32(0xFFFF0000))


def _unpack_bf16_pairs(w):
    lo = lax.bitcast_convert_type(w << 16, F32)
    hi = lax.bitcast_convert_type(w & jnp.uint32(0xFFFF0000), F32)
    return jnp.concatenate([lo, hi], axis=1)


def _layer_spec(arr, l):
    nd = arr.ndim - 1
    return pl.BlockSpec((None,) + arr.shape[1:], lambda *_: (l,) + (0,) * nd, pipeline_mode=pl.Buffered(1))


def _mixer_kernel(x_ref, wcat_ref, bcat_ref, wg2_ref, bg_ref, gnorm_ref,
                  slng_ref, slnb_ref, wtril_ref, sbias_ref, poolw_ref, pscale_ref,
                  wup_ref, wo_ref, ln1g_ref, ln1b_ref,
                  out_ref,
                  state_ref, qkvr_ref, ya_ref, vln_ref, e_ref, s2_ref, s4_ref, s8_ref):
    T = MIX_TILE
    D = D_MODEL
    j = pl.program_id(1)
    x = x_ref[0]
    xb = x.astype(BF16)

    def proj(lo, hi):
        return _dot(xb, wcat_ref[:, lo:hi]) + bcat_ref[:, lo:hi]

    @pl.when(j == 0)
    def _():
        state_ref[...] = jnp.zeros_like(state_ref)
        e_ref[0:POOL_CARRY, :] = jnp.zeros((POOL_CARRY, POOL_WIDTH), F32)

    @pl.when(j > 0)
    def _():
        e_ref[0:POOL_CARRY, :] = e_ref[T:T + POOL_CARRY, :]

    qkvr_ref[...] = proj(O_QKVR, O_GLOW)
    glow = proj(O_GLOW, O_UV)
    z = _dot(glow.astype(BF16), wg2_ref[...]) + bg_ref[...]
    la = (jnp.minimum(z, 0.0) - jnp.log1p(jnp.exp(-jnp.abs(z)))) * (1.0 / GLA_TAU)
    la_hi, la_lo = _split_bf16(la)
    gate_a = _sigmoid(proj(O_GATE, O_GATE + D))

    C = GLA_CHUNK
    G = GLA_SUB
    NC = G // C
    CSH = C.bit_length() - 1
    row = lax.broadcasted_iota(jnp.int32, (G, G), 0)
    col = lax.broadcasted_iota(jnp.int32, (G, G), 1)
    same_chunk = (row >> CSH) == (col >> CSH)
    causal = same_chunk & (row >= col)
    causal_bf = jnp.where(causal, 1.0, 0.0).astype(BF16)
    chunk_bf = jnp.where(same_chunk, 1.0, 0.0).astype(BF16)
    lane = lax.broadcasted_iota(jnp.int32, (G, GLA_KEY), 1)
    ind = jnp.where((lax.broadcasted_iota(jnp.int32, (G, NC * LANES), 0) >> CSH)
                    == (lax.broadcasted_iota(jnp.int32, (G, NC * LANES), 1) >> 7), 1.0, 0.0).astype(BF16)
    gate_b = None
    for g0 in range(0, T, G):
        gr = slice(g0, g0 + G)
        lh, ll = la_hi[gr], la_lo[gr]
        b = _dot(causal_bf, lh) + _dot(causal_bf, ll)
        b_end = _dot(chunk_bf, lh) + _dot(chunk_bf, ll)
        q = qkvr_ref[gr, 0:GLA_KEY]
        k = qkvr_ref[gr, GLA_KEY:2 * GLA_KEY]
        v = qkvr_ref[gr, 2 * GLA_KEY:2 * GLA_KEY + GLA_VAL].astype(BF16)
        q_dec = q * (GLA_DK ** -0.5) * jnp.exp(b)
        k_dec = (k * jnp.exp(-b)).astype(BF16)
        k_tail = (k * jnp.exp(b_end - b)).astype(BF16)
        q_dec_bf = q_dec.astype(BF16)
        o_heads = []
        for h in range(GLA_HEADS):
            q_h = jnp.where((lane >= h * GLA_DK) & (lane < (h + 1) * GLA_DK), q_dec, 0.0).astype(BF16)
            scores = jnp.where(causal, _dot_t1(q_h, k_dec), 0.0).astype(BF16)
            o_heads.append(_dot(scores, v[:, h * GLA_DV:(h + 1) * GLA_DV]))
        o_intra = jnp.concatenate(o_heads, axis=1)
        if gate_b is None:
            gate_b = _sigmoid(proj(O_GATE + D, O_GATE + 2 * D))
        dec_all = jnp.exp(_dot_t0(lh, ind) + _dot_t0(ll, ind))
        o_inter = []
        for c in range(NC):
            rows = slice(c * C, (c + 1) * C)
            o_inter.append(_dot(q_dec_bf[rows], state_ref[...].astype(BF16)))
            kv = _dot_t0(k_tail[rows], v[rows])
            for h in range(GLA_HEADS):
                rs = slice(h * GLA_DK, (h + 1) * GLA_DK)
                cs = slice(h * GLA_DV, (h + 1) * GLA_DV)
                state_ref[rs, cs] = dec_all[rs, c * LANES:(c + 1) * LANES] * state_ref[rs, cs] + kv[rs, cs]
        o = o_intra + jnp.concatenate(o_inter, axis=0)
        for h in range(GLA_HEADS):
            cs = slice(h * GLA_DV, (h + 1) * GLA_DV)
            o_h = o[:, cs]
            ms = jnp.mean(o_h * o_h, axis=-1, keepdims=True)
            o_h = o_h * lax.rsqrt(ms + LN_EPS) * gnorm_ref[:, cs]
            r_h = qkvr_ref[gr, 2 * GLA_KEY + GLA_VAL + h * GLA_DV:2 * GLA_KEY + GLA_VAL + (h + 1) * GLA_DV]
            ya_ref[gr, cs] = (o_h * (r_h * _sigmoid(r_h))).astype(BF16)

    gate_c = _sigmoid(proj(O_GATE + 2 * D, O_GATE + 3 * D))
    uv = proj(O_UV, O_XC)
    zg = 0.5 * uv * (1.0 + lax.erf(uv * (2.0 ** -0.5)))
    u = zg[:, :SGU_WIDTH]
    vln_ref[...] = _layer_norm(zg[:, SGU_WIDTH:], slng_ref[...], slnb_ref[...])
    lane_s = lax.broadcasted_iota(jnp.int32, (SGU_CHUNK, SGU_WIDTH), 1)
    s_parts = []
    for n in range(T // SGU_CHUNK):
        vc = vln_ref[n * SGU_CHUNK:(n + 1) * SGU_CHUNK, :]
        s = sbias_ref[...]
        for g in range(SGU_GROUPS):
            vg = jnp.where((lane_s >= g * SGU_GD) & (lane_s < (g + 1) * SGU_GD), vc, 0.0).astype(BF16)
            s = s + _dot(wtril_ref[g], vg)
        s_parts.append(s)
    y_b = (u * jnp.concatenate(s_parts, axis=0)).astype(BF16)

    P = POOL_CARRY
    xc = proj(O_XC, O_GATE)
    e_ref[P:P + T, :] = xc
    s2_ref[8:P + T, :] = e_ref[8:P + T, :] + e_ref[7:P + T - 1, :]
    s4_ref[16:P + T, :] = s2_ref[16:P + T, :] + s2_ref[14:P + T - 2, :]
    s8_ref[24:P + T, :] = s4_ref[24:P + T, :] + s4_ref[20:P + T - 4, :]
    s16 = s8_ref[P:P + T, :] + s8_ref[P - 8:P + T - 8, :]
    lane_p = lax.broadcasted_iota(jnp.int32, (T, POOL_WIDTH), 1)
    tpos = lax.broadcasted_iota(jnp.int32, (T, POOL_WIDTH), 0) + (j * T + 1)
    grp = lane_p >> 6
    win = jnp.where(grp == 0, POOL_WINDOWS[0], jnp.where(grp == 1, POOL_WINDOWS[1],
                    jnp.where(grp == 2, POOL_WINDOWS[2], POOL_WINDOWS[3])))
    wsum = jnp.where(grp == 0, s2_ref[P:P + T, :], jnp.where(grp == 1, s4_ref[P:P + T, :],
                     jnp.where(grp == 2, s8_ref[P:P + T, :], s16)))
    count = jnp.minimum(tpos, win).astype(F32)
    pooled = wsum / count - xc
    y_c = (_dot(pooled.astype(BF16), poolw_ref[...]) * pscale_ref[...]).astype(BF16)

    ra, rb = GLA_VAL, GLA_VAL + SGU_WIDTH
    merged = gate_a * _dot(ya_ref[...], wup_ref[0:ra, :])
    merged += gate_b * _dot(y_b, wup_ref[ra:rb, :])
    merged += gate_c * _dot(y_c, wup_ref[rb:, :])
    h = _dot(merged.astype(BF16), wo_ref[...])
    out_ref[0] = _layer_norm(DEEPNORM_ALPHA * x + h, ln1g_ref[...], ln1b_ref[...])


_MIXER_WEIGHTS = ('wcat', 'bcat', 'wg2', 'bg', 'gnorm', 'slng', 'slnb', 'wtril', 'sbias', 'poolw', 'pscale',
                  'wup', 'wo', 'ln1g', 'ln1b')


def _mixer(x, p, l):
    B, S, D = x.shape
    T = MIX_TILE
    weights = [p[n] for n in _MIXER_WEIGHTS]
    return pl.pallas_call(
        _mixer_kernel,
        out_shape=jax.ShapeDtypeStruct((B, S, D), F32),
        grid=(B, S // T),
        in_specs=[pl.BlockSpec((1, T, D), lambda b, j: (b, j, 0))] + [_layer_spec(w, l) for w in weights],
        out_specs=pl.BlockSpec((1, T, D), lambda b, j: (b, j, 0)),
        scratch_shapes=[
            pltpu.VMEM((GLA_KEY, GLA_VAL), F32),
            pltpu.VMEM((T, 2 * GLA_KEY + 2 * GLA_VAL), F32),
            pltpu.VMEM((T, GLA_VAL), BF16),
            pltpu.VMEM((T, SGU_WIDTH), F32),
            pltpu.VMEM((T + POOL_CARRY, POOL_WIDTH), F32),
            pltpu.VMEM((T + POOL_CARRY, POOL_WIDTH), F32),
            pltpu.VMEM((T + POOL_CARRY, POOL_WIDTH), F32),
            pltpu.VMEM((T + POOL_CARRY, POOL_WIDTH), F32),
        ],
        compiler_params=pltpu.CompilerParams(dimension_semantics=("arbitrary", "arbitrary"),
                                             vmem_limit_bytes=VMEM_LIMIT),
        name="mixer",
    )(x, *weights)


def _memkv_kernel(memt_ref, mem_ref, wkt_ref, wv_ref, kt_ref, v_ref):
    kt_ref[0] = _dot(wkt_ref[...], memt_ref[0].astype(BF16)).astype(BF16)
    v_ref[0] = _dot(mem_ref[0].astype(BF16), wv_ref[...]).astype(BF16)


def _memkv(mem, memt, p, l):
    B, M, D = mem.shape
    return pl.pallas_call(
        _memkv_kernel,
        out_shape=(jax.ShapeDtypeStruct((B, D, M), BF16), jax.ShapeDtypeStruct((B, M, D), BF16)),
        grid=(B,),
        in_specs=[pl.BlockSpec((1, D, M), lambda b: (b, 0, 0)), pl.BlockSpec((1, M, D), lambda b: (b, 0, 0)),
                  _layer_spec(p['wkt'], l), _layer_spec(p['wv'], l)],
        out_specs=(pl.BlockSpec((1, D, M), lambda b: (b, 0, 0)), pl.BlockSpec((1, M, D), lambda b: (b, 0, 0))),
        compiler_params=pltpu.CompilerParams(dimension_semantics=("arbitrary",), vmem_limit_bytes=VMEM_LIMIT),
        name="memkv",
    )(memt, mem, p['wkt'], p['wv'])


def _xattn_kernel(x_ref, kt_ref, v_ref, wq_ref, wo_ref, ln2g_ref, ln2b_ref, rwt_ref, rbt_ref,
                  x2_ref, x2p_ref, route_ref, counts_ref, carry_ref):
    T = XA_TILE

    @pl.when((pl.program_id(0) == 0) & (pl.program_id(1) == 0))
    def _():
        carry_ref[...] = jnp.zeros_like(carry_ref)

    x = x_ref[0]
    q = (_dot(x.astype(BF16), wq_ref[...]) * (XA_DH ** -0.5)).astype(BF16)
    h = jnp.zeros_like(x)
    for hd in range(XA_HEADS):
        cs = slice(hd * XA_DH, (hd + 1) * XA_DH)
        s = _dot(q[:, cs], kt_ref[0, cs, :])
        e = jnp.exp(s - jnp.max(s, axis=-1, keepdims=True))
        o = _dot(e.astype(BF16), v_ref[0, :, cs]) / jnp.sum(e, axis=-1, keepdims=True)
        h = h + _dot(o.astype(BF16), wo_ref[cs, :])
    x2 = _layer_norm(DEEPNORM_ALPHA * x + h, ln2g_ref[...], ln2b_ref[...])
    x2_ref[0] = x2
    x2p_ref[0] = _pack_bf16_pairs(x2)

    E = N_EXPERTS
    hi, lo = _split_bf16(x2)
    lt = _dot_t1(rwt_ref[...], hi)
    logits = lt[0:E] + (lt[E:2 * E] + _dot_t1(rwt_ref[0:E, :], lo)) + rbt_ref[...]

    eid = lax.broadcasted_iota(jnp.int32, (E, T), 0)
    neg_inf = jnp.float32(-jnp.inf)
    rest = logits
    tops, picks = [], []
    for _ in range(TOP_K):
        m = jnp.max(rest, axis=0, keepdims=True)
        idx = jnp.min(jnp.where(rest == m, eid, E), axis=0, keepdims=True)
        pick = eid == idx
        rest = jnp.where(pick, neg_inf, rest)
        tops.append((m, idx))
        picks.append(pick)
    exps = [jnp.exp(m - tops[0][0]) for m, _ in tops]
    denom = exps[0]
    for e in exps[1:]:
        denom = denom + e

    chosen = jnp.zeros((E, T), F32)
    for pick in picks:
        chosen = chosen + jnp.where(pick, 1.0, 0.0)
    chosen_bf = chosen.astype(BF16)
    earlier = (lax.broadcasted_iota(jnp.int32, (T, T), 0) < lax.broadcasted_iota(jnp.int32, (T, T), 1))
    carry = carry_ref[...]
    before = _dot(chosen_bf, jnp.where(earlier, 1.0, 0.0).astype(BF16)) + jnp.concatenate([carry] * (T // LANES), axis=1)
    carry = carry + _dot(chosen_bf, jnp.ones((T, LANES), BF16))
    carry_ref[...] = carry
    counts_ref[...] = carry

    rid = lax.broadcasted_iota(jnp.int32, (ROUTE_ROWS, T), 0)
    route = jnp.zeros((ROUTE_ROWS, T), F32)
    for k in range(TOP_K):
        rank = jnp.sum(jnp.where(picks[k], before, 0.0), axis=0, keepdims=True)
        route = jnp.where(rid == k, tops[k][1].astype(F32), route)
        route = jnp.where(rid == TOP_K + k, exps[k] / denom, route)
        route = jnp.where(rid == 2 * TOP_K + k, rank, route)
    route_ref[...] = route


def _xattn(x, kt, v, p, l):
    B, S, D = x.shape
    T = XA_TILE
    M = MEM_LEN
    weights = [p[n] for n in ('wq', 'wxo', 'ln2g', 'ln2b', 'rwt', 'rbt')]
    return pl.pallas_call(
        _xattn_kernel,
        out_shape=(jax.ShapeDtypeStruct((B, S, D), F32), jax.ShapeDtypeStruct((B, S, D // 2), jnp.uint32),
                   jax.ShapeDtypeStruct((ROUTE_ROWS, B * S), F32), jax.ShapeDtypeStruct((N_EXPERTS, LANES), F32)),
        grid=(B, S // T),
        in_specs=[pl.BlockSpec((1, T, D), lambda b, j: (b, j, 0)),
                  pl.BlockSpec((1, D, M), lambda b, j: (b, 0, 0)),
                  pl.BlockSpec((1, M, D), lambda b, j: (b, 0, 0))] + [_layer_spec(w, l) for w in weights],
        out_specs=(pl.BlockSpec((1, T, D), lambda b, j: (b, j, 0)),
                   pl.BlockSpec((1, T, D // 2), lambda b, j: (b, j, 0)),
                   pl.BlockSpec((ROUTE_ROWS, T), lambda b, j: (0, b * (S // T) + j)),
                   pl.BlockSpec((N_EXPERTS, LANES), lambda b, j: (0, 0))),
        scratch_shapes=[pltpu.VMEM((N_EXPERTS, LANES), F32)],
        compiler_params=pltpu.CompilerParams(dimension_semantics=("arbitrary", "arbitrary"),
                                             vmem_limit_bytes=VMEM_LIMIT),
        name="xattn",
    )(x, kt, v, *weights)


def _expert_kernel(layer, be_ref, nb_ref, nxt_ref, xs_ref, wgu_hbm, bgu_ref, wd_hbm, bd_ref, out_ref,
                   wgu_st, wd_st, wgu_bf, wd_bf, sem):
    i = pl.program_id(0)
    F = EXPERT_FF

    def weight_copies(e):
        return (pltpu.make_async_copy(wgu_hbm.at[layer, e], wgu_st, sem.at[0]),
                pltpu.make_async_copy(wd_hbm.at[layer, e], wd_st, sem.at[1]))

    @pl.when(i < nb_ref[0])
    def _():
        e = be_ref[i]
        prev = be_ref[jnp.maximum(i - 1, 0)]

        @pl.when(i == 0)
        def _():
            for cp in weight_copies(e):
                cp.start()

        @pl.when((i == 0) | (e != prev))
        def _():
            for cp in weight_copies(e):
                cp.wait()
            wgu_bf[...] = wgu_st[...].astype(BF16)
            wd_bf[...] = wd_st[...].astype(BF16)
            nxt = nxt_ref[e]

            @pl.when(nxt != e)
            def _():
                for cp in weight_copies(nxt):
                    cp.start()

        hh = _dot(_unpack_bf16_pairs(xs_ref[...]).astype(BF16), wgu_bf[...]) + bgu_ref[...]
        h_glu = jnp.minimum(hh[:, :F], SWIGLU_LIMIT)
        h_lin = jnp.clip(hh[:, F:], -SWIGLU_LIMIT, SWIGLU_LIMIT)
        a = h_glu * _sigmoid(SWIGLU_ALPHA * h_glu) * (h_lin + 1.0)
        out_ref[...] = _pack_bf16_pairs(_dot(a.astype(BF16), wd_bf[...]) + bd_ref[...])


def _experts(block_expert, n_used, next_expert, xs, w_gu, b_gu, w_down, b_down, l):
    P, DH = xs.shape
    D = 2 * DH
    NB = P // MOE_BLOCK
    F2 = 2 * EXPERT_FF

    def row_map(i, be, nb, nxt):
        return (jnp.minimum(i, nb[0] - 1), 0)

    def exp_map(i, be, nb, nxt):
        return (l, be[jnp.minimum(i, nb[0] - 1)], 0, 0)

    grid_spec = pltpu.PrefetchScalarGridSpec(
        num_scalar_prefetch=3,
        grid=(NB,),
        in_specs=[pl.BlockSpec((MOE_BLOCK, DH), row_map),
                  pl.BlockSpec(memory_space=pl.ANY),
                  pl.BlockSpec((None, None, 1, F2), exp_map),
                  pl.BlockSpec(memory_space=pl.ANY),
                  pl.BlockSpec((None, None, 1, D), exp_map)],
        out_specs=pl.BlockSpec((MOE_BLOCK, DH), row_map),
        scratch_shapes=[pltpu.VMEM((D, F2), F32), pltpu.VMEM((EXPERT_FF, D), F32),
                        pltpu.VMEM((D, F2), BF16), pltpu.VMEM((EXPERT_FF, D), BF16),
                        pltpu.SemaphoreType.DMA((2,))],
    )
    return pl.pallas_call(
        functools.partial(_expert_kernel, l),
        out_shape=jax.ShapeDtypeStruct((P, DH), jnp.uint32),
        grid_spec=grid_spec,
        compiler_params=pltpu.CompilerParams(dimension_semantics=("arbitrary",), vmem_limit_bytes=VMEM_LIMIT),
        name="experts",
    )(block_expert, n_used, next_expert, xs, w_gu, b_gu.reshape(DEPTH, N_EXPERTS, 1, F2), w_down,
      b_down.reshape(DEPTH, N_EXPERTS, 1, D))


def _sc_gather_rows(x, idx):
    M = idx.shape[0]
    D = x.shape[1]
    W = SC_GATHER_WINDOW
    mesh = plsc.VectorSubcoreMesh(core_axis_name="core", subcore_axis_name="subcore")
    n_workers = mesh.num_cores * mesh.num_subcores
    rows_per = M // n_workers
    assert rows_per * n_workers == M and rows_per % W == 0

    @pl.kernel(out_type=jax.ShapeDtypeStruct((M, D), x.dtype), mesh=mesh, name="sc_gather_rows",
               scratch_types=[pltpu.VMEM((rows_per,), jnp.int32), pltpu.VMEM((W, D), x.dtype)])
    def gather_kernel(x_hbm, i_hbm, o_hbm, idx_vmem, buf):
        wid = lax.axis_index("core") * mesh.num_subcores + lax.axis_index("subcore")
        base = wid * rows_per
        pltpu.sync_copy(i_hbm.at[pl.ds(base, rows_per)], idx_vmem)

        @pl.loop(0, rows_per // W)
        def _(j):
            pltpu.sync_copy(x_hbm.at[idx_vmem.at[pl.ds(j * W, W)]], buf)
            pltpu.sync_copy(buf, o_hbm.at[pl.ds(base + j * W, W)])

    return gather_kernel(x, idx)


def _sc_scatter_rows(x, idx, n_out):
    K, N = idx.shape
    D = x.shape[1]
    W = SC_GATHER_WINDOW
    mesh = plsc.VectorSubcoreMesh(core_axis_name="core", subcore_axis_name="subcore")
    n_workers = mesh.num_cores * mesh.num_subcores
    rows_per = N // n_workers
    assert rows_per * n_workers == N and rows_per % W == 0

    @pl.kernel(out_type=jax.ShapeDtypeStruct((n_out, D), x.dtype), mesh=mesh, name="sc_scatter_rows",
               scratch_types=[pltpu.VMEM((K * rows_per,), jnp.int32), pltpu.VMEM((W, D), x.dtype),
                              pltpu.SemaphoreType.DMA((K,))])
    def scatter_kernel(x_hbm, i_hbm, o_hbm, idx_vmem, buf, sem):
        wid = lax.axis_index("core") * mesh.num_subcores + lax.axis_index("subcore")
        base = wid * rows_per
        for k in range(K):
            pltpu.sync_copy(i_hbm.at[pl.ds(k * N + base, rows_per)], idx_vmem.at[pl.ds(k * rows_per, rows_per)])

        @pl.loop(0, rows_per // W)
        def _(j):
            pltpu.sync_copy(x_hbm.at[pl.ds(base + j * W, W)], buf)
            copies = [pltpu.make_async_copy(buf, o_hbm.at[idx_vmem.at[pl.ds(k * rows_per + j * W, W)]], sem.at[k])
                      for k in range(K)]
            for cp in copies:
                cp.start()
            for cp in copies:
                cp.wait()

    return scatter_kernel(x, idx.reshape(K * N))


def _combine_kernel(x_ref, yg_ref, gate_ref, ln3g_ref, ln3b_ref, *rest):
    out_ref = rest[-1]
    x = x_ref[...]
    g = gate_ref[...]
    y = jnp.zeros_like(x)
    for k in range(TOP_K):
        y = y + g[:, k:k + 1] * _unpack_bf16_pairs(yg_ref[k * CMB_TILE:(k + 1) * CMB_TILE, :])
    out_ref[...] = _layer_norm(DEEPNORM_ALPHA * x + y, ln3g_ref[...], ln3b_ref[...])


def _combine(x2, yg, gate, p, l, part, acc):
    N, D = x2.shape
    T = CMB_TILE
    tiles = yg.shape[0] // (TOP_K * T)
    first = part * tiles
    in_specs = [pl.BlockSpec((T, D), lambda i: (first + i, 0)), pl.BlockSpec((TOP_K * T, D // 2), lambda i: (i, 0)),
                pl.BlockSpec((T, TOP_K), lambda i: (first + i, 0)), _layer_spec(p['ln3g'], l), _layer_spec(p['ln3b'], l)]
    args = [x2, yg, gate, p['ln3g'], p['ln3b']]
    aliases = {}
    if acc is not None:
        in_specs.append(pl.BlockSpec(memory_space=pl.ANY))
        args.append(acc)
        aliases = {len(args) - 1: 0}
    return pl.pallas_call(
        _combine_kernel,
        out_shape=jax.ShapeDtypeStruct((N, D), F32),
        grid=(tiles,),
        in_specs=in_specs,
        out_specs=pl.BlockSpec((T, D), lambda i: (first + i, 0)),
        input_output_aliases=aliases,
        compiler_params=pltpu.CompilerParams(dimension_semantics=("arbitrary",), vmem_limit_bytes=VMEM_LIMIT),
        name="combine",
    )(*args)


def _prep(w_in, b_in, gla_wg2, gla_bg, gla_norm_g, sgu_ln_g, sgu_ln_b, sgu_ws, sgu_bs, pool_w, pool_scale,
          w_up_a, w_up_b, w_up_c, w_o, ln1_g, ln1_b, xa_wq, xa_wk, xa_wv, xa_wo, ln2_g, ln2_b,
          router_w, router_b, ln3_g, ln3_b):
    L = w_in.shape[0]
    o_glow = O_GLOW
    o_uv = o_glow + GLA_RANK
    row = lambda a: a.reshape(L, 1, -1).astype(F32)
    pad_last = lambda a, n: jnp.pad(a, [(0, 0)] * (a.ndim - 1) + [(0, n - a.shape[-1])])
    p = {}
    p['wcat'] = jnp.concatenate([w_in[..., :o_glow], pad_last(w_in[..., o_glow:o_uv], LANES), w_in[..., o_uv:]],
                                axis=-1).astype(BF16)
    p['bcat'] = row(jnp.concatenate([b_in[..., :o_glow], pad_last(b_in[..., o_glow:o_uv], LANES), b_in[..., o_uv:]],
                                    axis=-1))
    p['wg2'] = jnp.pad(gla_wg2, ((0, 0), (0, LANES - GLA_RANK), (0, 0))).astype(BF16)
    p['bg'] = row(gla_bg)
    p['gnorm'] = row(gla_norm_g)
    p['slng'] = row(sgu_ln_g)
    p['slnb'] = row(sgu_ln_b)
    p['wtril'] = jnp.tril(sgu_ws).astype(BF16)
    p['sbias'] = jnp.repeat(jnp.swapaxes(sgu_bs, 1, 2), SGU_GD, axis=2).astype(F32)
    G = len(POOL_WINDOWS)
    eye = jnp.eye(G, dtype=F32)
    p['poolw'] = jnp.einsum('lgcd,gh->lgchd', pool_w, eye).reshape(L, POOL_WIDTH, POOL_WIDTH).astype(BF16)
    p['pscale'] = row(pool_scale)
    p['wup'] = jnp.concatenate([w_up_a, w_up_b, w_up_c], axis=1).astype(BF16)
    p['wo'] = w_o.astype(BF16)
    p['ln1g'], p['ln1b'] = row(ln1_g), row(ln1_b)
    p['wq'] = xa_wq.astype(BF16)
    p['wkt'] = jnp.swapaxes(xa_wk, 1, 2).astype(BF16)
    p['wv'] = xa_wv.astype(BF16)
    p['wxo'] = xa_wo.astype(BF16)
    p['ln2g'], p['ln2b'] = row(ln2_g), row(ln2_b)
    rwt = jnp.swapaxes(router_w, 1, 2)
    rwt_hi = rwt.astype(BF16)
    p['rwt'] = jnp.concatenate([rwt_hi, (rwt - rwt_hi.astype(F32)).astype(BF16)], axis=1)
    p['rbt'] = jnp.broadcast_to(router_b[:, :, None], router_b.shape + (XA_TILE,)).astype(F32)
    p['ln3g'], p['ln3b'] = row(ln3_g), row(ln3_b)
    return p


def _route(route, counts):
    N = route.shape[1]
    top_idx = route[0:TOP_K].astype(jnp.int32)
    gate = route[TOP_K:2 * TOP_K].T
    rank = route[2 * TOP_K:3 * TOP_K].astype(jnp.int32)
    counts = counts[:, 0].astype(jnp.int32)
    padded = ((counts + MOE_BLOCK - 1) // MOE_BLOCK) * MOE_BLOCK
    pad_end = jnp.cumsum(padded)
    pad_start = pad_end - padded
    ids = jnp.arange(N_EXPERTS, dtype=jnp.int32)
    start_of = jnp.sum(jnp.where(top_idx[:, :, None] == ids[None, None, :], pad_start[None, None, :], 0), axis=-1)
    dest = start_of + rank
    n_blocks = N * TOP_K // MOE_BLOCK + N_EXPERTS
    block_start = jnp.arange(n_blocks, dtype=jnp.int32) * MOE_BLOCK
    block_expert = jnp.minimum(jnp.sum((pad_end[None, :] <= block_start[:, None]).astype(jnp.int32), axis=1),
                               N_EXPERTS - 1)
    n_used = (pad_end[-1] // MOE_BLOCK).astype(jnp.int32).reshape(1)
    later = jnp.where((ids[None, :] > ids[:, None]) & (counts[None, :] > 0), ids[None, :], N_EXPERTS)
    first_later = jnp.min(later, axis=1)
    next_expert = jnp.where(first_later < N_EXPERTS, first_later, ids).astype(jnp.int32)
    return gate, dest, block_expert, n_used, next_expert


def kernel(x, mem, w_in, b_in, gla_wg2, gla_bg, gla_norm_g, sgu_ln_g, sgu_ln_b, sgu_ws, sgu_bs, pool_w, pool_scale, w_up_a, w_up_b, w_up_c, w_o, ln1_g, ln1_b, xa_wq, xa_wk, xa_wv, xa_wo, ln2_g, ln2_b, router_w, router_b, exp_w_gu, exp_b_gu, exp_w_down, exp_b_down, ln3_g, ln3_b):
    B, S, D = x.shape
    N = B * S
    p = _prep(w_in, b_in, gla_wg2, gla_bg, gla_norm_g, sgu_ln_g, sgu_ln_b, sgu_ws, sgu_bs, pool_w, pool_scale,
              w_up_a, w_up_b, w_up_c, w_o, ln1_g, ln1_b, xa_wq, xa_wk, xa_wv, xa_wo, ln2_g, ln2_b,
              router_w, router_b, ln3_g, ln3_b)
    memt = jnp.swapaxes(mem, 1, 2)
    for l in range(DEPTH):
        x1 = _mixer(x, p, l)
        kt, v = _memkv(mem, memt, p, l)
        x2, x2p, route, counts = _xattn(x1, kt, v, p, l)
        gate, dest, block_expert, n_used, next_expert = _route(route, counts)
        n_slots = (N * TOP_K // MOE_BLOCK + N_EXPERTS) * MOE_BLOCK
        xs = _sc_scatter_rows(x2p.reshape(N, D // 2), dest, n_slots)
        ys = _experts(block_expert, n_used, next_expert, xs, exp_w_gu, exp_b_gu, exp_w_down, exp_b_down, l)
        dest_km = dest.reshape(TOP_K, N // CMB_TILE, CMB_TILE).transpose(1, 0, 2).reshape(-1)
        rows = dest_km.shape[0] // COMBINE_PARTS
        acc = None
        for part in range(COMBINE_PARTS):
            yg = _sc_gather_rows(ys, dest_km[part * rows:(part + 1) * rows])
            acc = _combine(x2.reshape(N, D), yg, gate, p, l, part, acc)
        x = acc.reshape(B, S, D)
    return x
```

```python
import functools

import jax
import jax.numpy as jnp
from jax import lax
from jax.experimental import pallas as pl
from jax.experimental.pallas import tpu as pltpu
from jax.experimental.pallas import tpu_sc as plsc

F32 = jnp.float32
BF16 = jnp.bfloat16

D_MODEL = 1024
DEPTH = 2
GLA_HEADS = 4
GLA_KEY = 256
GLA_VAL = 512
GLA_DK = 64
GLA_DV = 128
GLA_RANK = 16
GLA_TAU = 16.0
GLA_CHUNK = 64
SGU_GROUPS = 4
SGU_WIDTH = 256
SGU_GD = 64
SGU_CHUNK = 128
POOL_WINDOWS = (2, 4, 8, 16)
POOL_WIDTH = 256
POOL_GD = 64
POOL_CARRY = 32
MEM_LEN = 256
XA_HEADS = 4
XA_DH = 256
N_EXPERTS = 32
TOP_K = 4
EXPERT_FF = 1024
SWIGLU_LIMIT = 7.0
SWIGLU_ALPHA = 1.702
DEEPNORM_ALPHA = (2 * DEPTH) ** 0.25
LN_EPS = 1e-5
LANES = 128
VMEM_LIMIT = 56 * 1024 * 1024

MIX_TILE = 512
GLA_SUB = 256
XA_TILE = 1024
MOE_BLOCK = 256
MOE_STEP_BLOCKS = 2
CMB_TILE = 256
COMBINE_PARTS = 4
ROUTE_ROWS = 16
SC_GATHER_WINDOW = 128

O_QKVR = 0
O_GLOW = 2 * GLA_KEY + 2 * GLA_VAL
O_UV = O_GLOW + LANES
O_XC = O_UV + 2 * SGU_WIDTH
O_GATE = O_XC + POOL_WIDTH
N_PACK = O_GATE + 3 * D_MODEL


def _dot(a, b):
    return jnp.dot(a, b, preferred_element_type=F32)


def _dot_t0(a, b):
    return lax.dot_general(a, b, (((0,), (0,)), ((), ())), preferred_element_type=F32)


def _dot_t1(a, b):
    return lax.dot_general(a, b, (((1,), (1,)), ((), ())), preferred_element_type=F32)


def _split_bf16(x):
    hi = x.astype(BF16)
    lo = (x - hi.astype(F32)).astype(BF16)
    return hi, lo


def _layer_norm(x, g, b):
    mu = jnp.mean(x, axis=-1, keepdims=True)
    xc = x - mu
    var = jnp.mean(xc * xc, axis=-1, keepdims=True)
    return xc * lax.rsqrt(var + LN_EPS) * g + b


def _sigmoid(x):
    return 1.0 / (1.0 + jnp.exp(-x))


def _pack_bf16_pairs(x):
    H = x.shape[1] // 2
    bits = lax.bitcast_convert_type(x.astype(BF16).astype(F32), jnp.uint32)
    return (bits[:, :H] >> 16) | (bits[:, H:] & jnp.uint32(0xFFFF0000))


def _unpack_bf16_pairs(w):
    lo = lax.bitcast_convert_type(w << 16, F32)
    hi = lax.bitcast_convert_type(w & jnp.uint32(0xFFFF0000), F32)
    return jnp.concatenate([lo, hi], axis=1)


def _layer_spec(arr, l):
    nd = arr.ndim - 1
    return pl.BlockSpec((None,) + arr.shape[1:], lambda *_: (l,) + (0,) * nd, pipeline_mode=pl.Buffered(1))


def _mixer_kernel(x_ref, wcat_ref, bcat_ref, wg2_ref, bg_ref, gnorm_ref,
                  slng_ref, slnb_ref, wtril_ref, sbias_ref, poolw_ref, pscale_ref,
                  wup_ref, wo_ref, ln1g_ref, ln1b_ref,
                  out_ref,
                  state_ref, qkvr_ref, ya_ref, vln_ref, e_ref, s2_ref, s4_ref, s8_ref):
    T = MIX_TILE
    D = D_MODEL
    j = pl.program_id(1)
    x = x_ref[0]
    xb = x.astype(BF16)

    def proj(lo, hi):
        return _dot(xb, wcat_ref[:, lo:hi]) + bcat_ref[:, lo:hi]

    @pl.when(j == 0)
    def _():
        state_ref[...] = jnp.zeros_like(state_ref)
        e_ref[0:POOL_CARRY, :] = jnp.zeros((POOL_CARRY, POOL_WIDTH), F32)

    @pl.when(j > 0)
    def _():
        e_ref[0:POOL_CARRY, :] = e_ref[T:T + POOL_CARRY, :]

    qkvr_ref[...] = proj(O_QKVR, O_GLOW)
    glow = proj(O_GLOW, O_UV)
    z = _dot(glow.astype(BF16), wg2_ref[...]) + bg_ref[...]
    la = (jnp.minimum(z, 0.0) - jnp.log1p(jnp.exp(-jnp.abs(z)))) * (1.0 / GLA_TAU)
    la_hi, la_lo = _split_bf16(la)
    gate_a = _sigmoid(proj(O_GATE, O_GATE + D))

    C = GLA_CHUNK
    G = GLA_SUB
    NC = G // C
    CSH = C.bit_length() - 1
    row = lax.broadcasted_iota(jnp.int32, (G, G), 0)
    col = lax.broadcasted_iota(jnp.int32, (G, G), 1)
    same_chunk = (row >> CSH) == (col >> CSH)
    causal = same_chunk & (row >= col)
    causal_bf = jnp.where(causal, 1.0, 0.0).astype(BF16)
    chunk_bf = jnp.where(same_chunk, 1.0, 0.0).astype(BF16)
    lane = lax.broadcasted_iota(jnp.int32, (G, GLA_KEY), 1)
    ind = jnp.where((lax.broadcasted_iota(jnp.int32, (G, NC * LANES), 0) >> CSH)
                    == (lax.broadcasted_iota(jnp.int32, (G, NC * LANES), 1) >> 7), 1.0, 0.0).astype(BF16)
    gate_b = None
    for g0 in range(0, T, G):
        gr = slice(g0, g0 + G)
        lh, ll = la_hi[gr], la_lo[gr]
        b = _dot(causal_bf, lh) + _dot(causal_bf, ll)
        b_end = _dot(chunk_bf, lh) + _dot(chunk_bf, ll)
        q = qkvr_ref[gr, 0:GLA_KEY]
        k = qkvr_ref[gr, GLA_KEY:2 * GLA_KEY]
        v = qkvr_ref[gr, 2 * GLA_KEY:2 * GLA_KEY + GLA_VAL].astype(BF16)
        q_dec = q * (GLA_DK ** -0.5) * jnp.exp(b)
        k_dec = (k * jnp.exp(-b)).astype(BF16)
        k_tail = (k * jnp.exp(b_end - b)).astype(BF16)
        q_dec_bf = q_dec.astype(BF16)
        o_heads = []
        for h in range(GLA_HEADS):
            q_h = jnp.where((lane >= h * GLA_DK) & (lane < (h + 1) * GLA_DK), q_dec, 0.0).astype(BF16)
            scores = jnp.where(causal, _dot_t1(q_h, k_dec), 0.0).astype(BF16)
            o_heads.append(_dot(scores, v[:, h * GLA_DV:(h + 1) * GLA_DV]))
        o_intra = jnp.concatenate(o_heads, axis=1)
        if gate_b is None:
            gate_b = _sigmoid(proj(O_GATE + D, O_GATE + 2 * D))
        dec_all = jnp.exp(_dot_t0(lh, ind) + _dot_t0(ll, ind))
        o_inter = []
        for c in range(NC):
            rows = slice(c * C, (c + 1) * C)
            o_inter.append(_dot(q_dec_bf[rows], state_ref[...].astype(BF16)))
            kv = _dot_t0(k_tail[rows], v[rows])
            for h in range(GLA_HEADS):
                rs = slice(h * GLA_DK, (h + 1) * GLA_DK)
                cs = slice(h * GLA_DV, (h + 1) * GLA_DV)
                state_ref[rs, cs] = dec_all[rs, c * LANES:(c + 1) * LANES] * state_ref[rs, cs] + kv[rs, cs]
        o = o_intra + jnp.concatenate(o_inter, axis=0)
        for h in range(GLA_HEADS):
            cs = slice(h * GLA_DV, (h + 1) * GLA_DV)
            o_h = o[:, cs]
            ms = jnp.mean(o_h * o_h, axis=-1, keepdims=True)
            o_h = o_h * lax.rsqrt(ms + LN_EPS) * gnorm_ref[:, cs]
            r_h = qkvr_ref[gr, 2 * GLA_KEY + GLA_VAL + h * GLA_DV:2 * GLA_KEY + GLA_VAL + (h + 1) * GLA_DV]
            ya_ref[gr, cs] = (o_h * (r_h * _sigmoid(r_h))).astype(BF16)

    gate_c = _sigmoid(proj(O_GATE + 2 * D, O_GATE + 3 * D))
    uv = proj(O_UV, O_XC)
    zg = 0.5 * uv * (1.0 + lax.erf(uv * (2.0 ** -0.5)))
    u = zg[:, :SGU_WIDTH]
    vln_ref[...] = _layer_norm(zg[:, SGU_WIDTH:], slng_ref[...], slnb_ref[...])
    lane_s = lax.broadcasted_iota(jnp.int32, (SGU_CHUNK, SGU_WIDTH), 1)
    s_parts = []
    for n in range(T // SGU_CHUNK):
        vc = vln_ref[n * SGU_CHUNK:(n + 1) * SGU_CHUNK, :]
        s = sbias_ref[...]
        for g in range(SGU_GROUPS):
            vg = jnp.where((lane_s >= g * SGU_GD) & (lane_s < (g + 1) * SGU_GD), vc, 0.0).astype(BF16)
            s = s + _dot(wtril_ref[g], vg)
        s_parts.append(s)
    y_b = (u * jnp.concatenate(s_parts, axis=0)).astype(BF16)

    P = POOL_CARRY
    xc = proj(O_XC, O_GATE)
    e_ref[P:P + T, :] = xc
    s2_ref[8:P + T, :] = e_ref[8:P + T, :] + e_ref[7:P + T - 1, :]
    s4_ref[16:P + T, :] = s2_ref[16:P + T, :] + s2_ref[14:P + T - 2, :]
    s8_ref[24:P + T, :] = s4_ref[24:P + T, :] + s4_ref[20:P + T - 4, :]
    s16 = s8_ref[P:P + T, :] + s8_ref[P - 8:P + T - 8, :]
    lane_p = lax.broadcasted_iota(jnp.int32, (T, POOL_WIDTH), 1)
    tpos = lax.broadcasted_iota(jnp.int32, (T, POOL_WIDTH), 0) + (j * T + 1)
    grp = lane_p >> 6
    win = jnp.where(grp == 0, POOL_WINDOWS[0], jnp.where(grp == 1, POOL_WINDOWS[1],
                    jnp.where(grp == 2, POOL_WINDOWS[2], POOL_WINDOWS[3])))
    wsum = jnp.where(grp == 0, s2_ref[P:P + T, :], jnp.where(grp == 1, s4_ref[P:P + T, :],
                     jnp.where(grp == 2, s8_ref[P:P + T, :], s16)))
    count = jnp.minimum(tpos, win).astype(F32)
    pooled = wsum / count - xc
    y_c = (_dot(pooled.astype(BF16), poolw_ref[...]) * pscale_ref[...]).astype(BF16)

    ra, rb = GLA_VAL, GLA_VAL + SGU_WIDTH
    merged = gate_a * _dot(ya_ref[...], wup_ref[0:ra, :])
    merged += gate_b * _dot(y_b, wup_ref[ra:rb, :])
    merged += gate_c * _dot(y_c, wup_ref[rb:, :])
    h = _dot(merged.astype(BF16), wo_ref[...])
    out_ref[0] = _layer_norm(DEEPNORM_ALPHA * x + h, ln1g_ref[...], ln1b_ref[...])


_MIXER_WEIGHTS = ('wcat', 'bcat', 'wg2', 'bg', 'gnorm', 'slng', 'slnb', 'wtril', 'sbias', 'poolw', 'pscale',
                  'wup', 'wo', 'ln1g', 'ln1b')


def _mixer(x, p, l):
    B, S, D = x.shape
    T = MIX_TILE
    weights = [p[n] for n in _MIXER_WEIGHTS]
    return pl.pallas_call(
        _mixer_kernel,
        out_shape=jax.ShapeDtypeStruct((B, S, D), F32),
        grid=(B, S // T),
        in_specs=[pl.BlockSpec((1, T, D), lambda b, j: (b, j, 0))] + [_layer_spec(w, l) for w in weights],
        out_specs=pl.BlockSpec((1, T, D), lambda b, j: (b, j, 0)),
        scratch_shapes=[
            pltpu.VMEM((GLA_KEY, GLA_VAL), F32),
            pltpu.VMEM((T, 2 * GLA_KEY + 2 * GLA_VAL), F32),
            pltpu.VMEM((T, GLA_VAL), BF16),
            pltpu.VMEM((T, SGU_WIDTH), F32),
            pltpu.VMEM((T + POOL_CARRY, POOL_WIDTH), F32),
            pltpu.VMEM((T + POOL_CARRY, POOL_WIDTH), F32),
            pltpu.VMEM((T + POOL_CARRY, POOL_WIDTH), F32),
            pltpu.VMEM((T + POOL_CARRY, POOL_WIDTH), F32),
        ],
        compiler_params=pltpu.CompilerParams(dimension_semantics=("arbitrary", "arbitrary"),
                                             vmem_limit_bytes=VMEM_LIMIT),
        name="mixer",
    )(x, *weights)


def _memkv_kernel(memt_ref, mem_ref, wkt_ref, wv_ref, kt_ref, v_ref):
    kt_ref[0] = _dot(wkt_ref[...], memt_ref[0].astype(BF16)).astype(BF16)
    v_ref[0] = _dot(mem_ref[0].astype(BF16), wv_ref[...]).astype(BF16)


def _memkv(mem, memt, p, l):
    B, M, D = mem.shape
    return pl.pallas_call(
        _memkv_kernel,
        out_shape=(jax.ShapeDtypeStruct((B, D, M), BF16), jax.ShapeDtypeStruct((B, M, D), BF16)),
        grid=(B,),
        in_specs=[pl.BlockSpec((1, D, M), lambda b: (b, 0, 0)), pl.BlockSpec((1, M, D), lambda b: (b, 0, 0)),
                  _layer_spec(p['wkt'], l), _layer_spec(p['wv'], l)],
        out_specs=(pl.BlockSpec((1, D, M), lambda b: (b, 0, 0)), pl.BlockSpec((1, M, D), lambda b: (b, 0, 0))),
        compiler_params=pltpu.CompilerParams(dimension_semantics=("arbitrary",), vmem_limit_bytes=VMEM_LIMIT),
        name="memkv",
    )(memt, mem, p['wkt'], p['wv'])


def _xattn_kernel(x_ref, kt_ref, v_ref, wq_ref, wo_ref, ln2g_ref, ln2b_ref, rwt_ref, rbt_ref,
                  x2_ref, x2p_ref, route_ref, counts_ref, carry_ref):
    T = XA_TILE

    @pl.when((pl.program_id(0) == 0) & (pl.program_id(1) == 0))
    def _():
        carry_ref[...] = jnp.zeros_like(carry_ref)

    x = x_ref[0]
    q = (_dot(x.astype(BF16), wq_ref[...]) * (XA_DH ** -0.5)).astype(BF16)
    h = jnp.zeros_like(x)
    for hd in range(XA_HEADS):
        cs = slice(hd * XA_DH, (hd + 1) * XA_DH)
        s = _dot(q[:, cs], kt_ref[0, cs, :])
        e = jnp.exp(s - jnp.max(s, axis=-1, keepdims=True))
        o = _dot(e.astype(BF16), v_ref[0, :, cs]) / jnp.sum(e, axis=-1, keepdims=True)
        h = h + _dot(o.astype(BF16), wo_ref[cs, :])
    x2 = _layer_norm(DEEPNORM_ALPHA * x + h, ln2g_ref[...], ln2b_ref[...])
    x2_ref[0] = x2
    x2p_ref[0] = _pack_bf16_pairs(x2)

    E = N_EXPERTS
    hi, lo = _split_bf16(x2)
    lt = _dot_t1(rwt_ref[...], hi)
    logits = lt[0:E] + (lt[E:2 * E] + _dot_t1(rwt_ref[0:E, :], lo)) + rbt_ref[...]

    eid = lax.broadcasted_iota(jnp.int32, (E, T), 0)
    neg_inf = jnp.float32(-jnp.inf)
    rest = logits
    tops, picks = [], []
    for _ in range(TOP_K):
        m = jnp.max(rest, axis=0, keepdims=True)
        idx = jnp.min(jnp.where(rest == m, eid, E), axis=0, keepdims=True)
        pick = eid == idx
        rest = jnp.where(pick, neg_inf, rest)
        tops.append((m, idx))
        picks.append(pick)
    exps = [jnp.exp(m - tops[0][0]) for m, _ in tops]
    denom = exps[0]
    for e in exps[1:]:
        denom = denom + e

    chosen = jnp.zeros((E, T), F32)
    for pick in picks:
        chosen = chosen + jnp.where(pick, 1.0, 0.0)
    chosen_bf = chosen.astype(BF16)
    earlier = (lax.broadcasted_iota(jnp.int32, (T, T), 0) < lax.broadcasted_iota(jnp.int32, (T, T), 1))
    carry = carry_ref[...]
    before = _dot(chosen_bf, jnp.where(earlier, 1.0, 0.0).astype(BF16)) + jnp.concatenate([carry] * (T // LANES), axis=1)
    carry = carry + _dot(chosen_bf, jnp.ones((T, LANES), BF16))
    carry_ref[...] = carry
    counts_ref[...] = carry

    rid = lax.broadcasted_iota(jnp.int32, (ROUTE_ROWS, T), 0)
    route = jnp.zeros((ROUTE_ROWS, T), F32)
    for k in range(TOP_K):
        rank = jnp.sum(jnp.where(picks[k], before, 0.0), axis=0, keepdims=True)
        route = jnp.where(rid == k, tops[k][1].astype(F32), route)
        route = jnp.where(rid == TOP_K + k, exps[k] / denom, route)
        route = jnp.where(rid == 2 * TOP_K + k, rank, route)
    route_ref[...] = route


def _xattn(x, kt, v, p, l):
    B, S, D = x.shape
    T = XA_TILE
    M = MEM_LEN
    weights = [p[n] for n in ('wq', 'wxo', 'ln2g', 'ln2b', 'rwt', 'rbt')]
    return pl.pallas_call(
        _xattn_kernel,
        out_shape=(jax.ShapeDtypeStruct((B, S, D), F32), jax.ShapeDtypeStruct((B, S, D // 2), jnp.uint32),
                   jax.ShapeDtypeStruct((ROUTE_ROWS, B * S), F32), jax.ShapeDtypeStruct((N_EXPERTS, LANES), F32)),
        grid=(B, S // T),
        in_specs=[pl.BlockSpec((1, T, D), lambda b, j: (b, j, 0)),
                  pl.BlockSpec((1, D, M), lambda b, j: (b, 0, 0)),
                  pl.BlockSpec((1, M, D), lambda b, j: (b, 0, 0))] + [_layer_spec(w, l) for w in weights],
        out_specs=(pl.BlockSpec((1, T, D), lambda b, j: (b, j, 0)),
                   pl.BlockSpec((1, T, D // 2), lambda b, j: (b, j, 0)),
                   pl.BlockSpec((ROUTE_ROWS, T), lambda b, j: (0, b * (S // T) + j)),
                   pl.BlockSpec((N_EXPERTS, LANES), lambda b, j: (0, 0))),
        scratch_shapes=[pltpu.VMEM((N_EXPERTS, LANES), F32)],
        compiler_params=pltpu.CompilerParams(dimension_semantics=("arbitrary", "arbitrary"),
                                             vmem_limit_bytes=VMEM_LIMIT),
        name="xattn",
    )(x, kt, v, *weights)


def _expert_kernel(layer, se_ref, sh_ref, nu_ref, nxt_ref, xs_ref, wgu_hbm, bgu_ref, wd_hbm, bd_ref, out_ref,
                   wgu_st, wd_st, wgu_bf, wd_bf, sem):
    i = pl.program_id(0)
    F = EXPERT_FF
    halves = sh_ref[i]

    def weight_copies(e):
        return (pltpu.make_async_copy(wgu_hbm.at[layer, e], wgu_st, sem.at[0]),
                pltpu.make_async_copy(wd_hbm.at[layer, e], wd_st, sem.at[1]))

    def ffn(rows):
        xsb = _unpack_bf16_pairs(xs_ref[rows, :]).astype(BF16)
        hh = _dot(xsb, wgu_bf[...]) + bgu_ref[...]
        h_glu = jnp.minimum(hh[:, :F], SWIGLU_LIMIT)
        h_lin = jnp.clip(hh[:, F:], -SWIGLU_LIMIT, SWIGLU_LIMIT)
        a = h_glu * _sigmoid(SWIGLU_ALPHA * h_glu) * (h_lin + 1.0)
        out_ref[rows, :] = _pack_bf16_pairs(_dot(a.astype(BF16), wd_bf[...]) + bd_ref[...])

    @pl.when(halves > 0)
    def _():
        e = se_ref[i]
        prev = se_ref[jnp.maximum(i - 1, 0)]

        @pl.when(i == 0)
        def _():
            for cp in weight_copies(e):
                cp.start()

        @pl.when((i == 0) | (e != prev))
        def _():
            for cp in weight_copies(e):
                cp.wait()
            wgu_bf[...] = wgu_st[...].astype(BF16)
            wd_bf[...] = wd_st[...].astype(BF16)
            nxt = nxt_ref[e]

            @pl.when(nxt != e)
            def _():
                for cp in weight_copies(nxt):
                    cp.start()

    @pl.when(halves == 2)
    def _():
        ffn(slice(0, MOE_STEP_BLOCKS * MOE_BLOCK))

    @pl.when(halves == 1)
    def _():
        ffn(slice(0, MOE_BLOCK))


def _experts(step_expert, step_halves, n_used, next_expert, xs, w_gu, b_gu, w_down, b_down, l):
    P, DH = xs.shape
    D = 2 * DH
    R = MOE_STEP_BLOCKS * MOE_BLOCK
    NS = P // R
    F2 = 2 * EXPERT_FF

    def row_map(i, se, sh, nu, nxt):
        return (jnp.minimum(i, nu[0] - 1), 0)

    def exp_map(i, se, sh, nu, nxt):
        return (l, se[jnp.minimum(i, nu[0] - 1)], 0, 0)

    grid_spec = pltpu.PrefetchScalarGridSpec(
        num_scalar_prefetch=4,
        grid=(NS,),
        in_specs=[pl.BlockSpec((R, DH), row_map),
                  pl.BlockSpec(memory_space=pl.ANY),
                  pl.BlockSpec((None, None, 1, F2), exp_map),
                  pl.BlockSpec(memory_space=pl.ANY),
                  pl.BlockSpec((None, None, 1, D), exp_map)],
        out_specs=pl.BlockSpec((R, DH), row_map),
        scratch_shapes=[pltpu.VMEM((D, F2), F32), pltpu.VMEM((EXPERT_FF, D), F32),
                        pltpu.VMEM((D, F2), BF16), pltpu.VMEM((EXPERT_FF, D), BF16),
                        pltpu.SemaphoreType.DMA((2,))],
    )
    return pl.pallas_call(
        functools.partial(_expert_kernel, l),
        out_shape=jax.ShapeDtypeStruct((P, DH), jnp.uint32),
        grid_spec=grid_spec,
        compiler_params=pltpu.CompilerParams(dimension_semantics=("arbitrary",), vmem_limit_bytes=VMEM_LIMIT),
        name="experts",
    )(step_expert, step_halves, n_used, next_expert, xs, w_gu, b_gu.reshape(DEPTH, N_EXPERTS, 1, F2), w_down,
      b_down.reshape(DEPTH, N_EXPERTS, 1, D))


def _sc_gather_rows(x, idx):
    M = idx.shape[0]
    D = x.shape[1]
    W = SC_GATHER_WINDOW
    mesh = plsc.VectorSubcoreMesh(core_axis_name="core", subcore_axis_name="subcore")
    n_workers = mesh.num_cores * mesh.num_subcores
    rows_per = M // n_workers
    assert rows_per * n_workers == M and rows_per % W == 0

    @pl.kernel(out_type=jax.ShapeDtypeStruct((M, D), x.dtype), mesh=mesh, name="sc_gather_rows",
               scratch_types=[pltpu.VMEM((rows_per,), jnp.int32), pltpu.VMEM((W, D), x.dtype)])
    def gather_kernel(x_hbm, i_hbm, o_hbm, idx_vmem, buf):
        wid = lax.axis_index("core") * mesh.num_subcores + lax.axis_index("subcore")
        base = wid * rows_per
        pltpu.sync_copy(i_hbm.at[pl.ds(base, rows_per)], idx_vmem)

        @pl.loop(0, rows_per // W)
        def _(j):
            pltpu.sync_copy(x_hbm.at[idx_vmem.at[pl.ds(j * W, W)]], buf)
            pltpu.sync_copy(buf, o_hbm.at[pl.ds(base + j * W, W)])

    return gather_kernel(x, idx)


def _sc_scatter_rows(x, idx, n_out):
    K, N = idx.shape
    D = x.shape[1]
    W = SC_GATHER_WINDOW
    mesh = plsc.VectorSubcoreMesh(core_axis_name="core", subcore_axis_name="subcore")
    n_workers = mesh.num_cores * mesh.num_subcores
    rows_per = N // n_workers
    assert rows_per * n_workers == N and rows_per % W == 0

    @pl.kernel(out_type=jax.ShapeDtypeStruct((n_out, D), x.dtype), mesh=mesh, name="sc_scatter_rows",
               scratch_types=[pltpu.VMEM((K * rows_per,), jnp.int32), pltpu.VMEM((W, D), x.dtype),
                              pltpu.SemaphoreType.DMA((K,))])
    def scatter_kernel(x_hbm, i_hbm, o_hbm, idx_vmem, buf, sem):
        wid = lax.axis_index("core") * mesh.num_subcores + lax.axis_index("subcore")
        base = wid * rows_per
        for k in range(K):
            pltpu.sync_copy(i_hbm.at[pl.ds(k * N + base, rows_per)], idx_vmem.at[pl.ds(k * rows_per, rows_per)])

        @pl.loop(0, rows_per // W)
        def _(j):
            pltpu.sync_copy(x_hbm.at[pl.ds(base + j * W, W)], buf)
            copies = [pltpu.make_async_copy(buf, o_hbm.at[idx_vmem.at[pl.ds(k * rows_per + j * W, W)]], sem.at[k])
                      for k in range(K)]
            for cp in copies:
                cp.start()
            for cp in copies:
                cp.wait()

    return scatter_kernel(x, idx.reshape(K * N))


def _combine_kernel(x_ref, yg_ref, gate_ref, ln3g_ref, ln3b_ref, *rest):
    out_ref = rest[-1]
    x = x_ref[...]
    g = gate_ref[...]
    y = jnp.zeros_like(x)
    for k in range(TOP_K):
        y = y + g[:, k:k + 1] * _unpack_bf16_pairs(yg_ref[k * CMB_TILE:(k + 1) * CMB_TILE, :])
    out_ref[...] = _layer_norm(DEEPNORM_ALPHA * x + y, ln3g_ref[...], ln3b_ref[...])


def _combine(x2, yg, gate, p, l, part, acc):
    N, D = x2.shape
    T = CMB_TILE
    tiles = yg.shape[0] // (TOP_K * T)
    first = part * tiles
    in_specs = [pl.BlockSpec((T, D), lambda i: (first + i, 0)), pl.BlockSpec((TOP_K * T, D // 2), lambda i: (i, 0)),
                pl.BlockSpec((T, TOP_K), lambda i: (first + i, 0)), _layer_spec(p['ln3g'], l), _layer_spec(p['ln3b'], l)]
    args = [x2, yg, gate, p['ln3g'], p['ln3b']]
    aliases = {}
    if acc is not None:
        in_specs.append(pl.BlockSpec(memory_space=pl.ANY))
        args.append(acc)
        aliases = {len(args) - 1: 0}
    return pl.pallas_call(
        _combine_kernel,
        out_shape=jax.ShapeDtypeStruct((N, D), F32),
        grid=(tiles,),
        in_specs=in_specs,
        out_specs=pl.BlockSpec((T, D), lambda i: (first + i, 0)),
        input_output_aliases=aliases,
        compiler_params=pltpu.CompilerParams(dimension_semantics=("arbitrary",), vmem_limit_bytes=VMEM_LIMIT),
        name="combine",
    )(*args)


def _prep(w_in, b_in, gla_wg2, gla_bg, gla_norm_g, sgu_ln_g, sgu_ln_b, sgu_ws, sgu_bs, pool_w, pool_scale,
          w_up_a, w_up_b, w_up_c, w_o, ln1_g, ln1_b, xa_wq, xa_wk, xa_wv, xa_wo, ln2_g, ln2_b,
          router_w, router_b, ln3_g, ln3_b):
    L = w_in.shape[0]
    o_glow = O_GLOW
    o_uv = o_glow + GLA_RANK
    row = lambda a: a.reshape(L, 1, -1).astype(F32)
    pad_last = lambda a, n: jnp.pad(a, [(0, 0)] * (a.ndim - 1) + [(0, n - a.shape[-1])])
    p = {}
    p['wcat'] = jnp.concatenate([w_in[..., :o_glow], pad_last(w_in[..., o_glow:o_uv], LANES), w_in[..., o_uv:]],
                                axis=-1).astype(BF16)
    p['bcat'] = row(jnp.concatenate([b_in[..., :o_glow], pad_last(b_in[..., o_glow:o_uv], LANES), b_in[..., o_uv:]],
                                    axis=-1))
    p['wg2'] = jnp.pad(gla_wg2, ((0, 0), (0, LANES - GLA_RANK), (0, 0))).astype(BF16)
    p['bg'] = row(gla_bg)
    p['gnorm'] = row(gla_norm_g)
    p['slng'] = row(sgu_ln_g)
    p['slnb'] = row(sgu_ln_b)
    p['wtril'] = jnp.tril(sgu_ws).astype(BF16)
    p['sbias'] = jnp.repeat(jnp.swapaxes(sgu_bs, 1, 2), SGU_GD, axis=2).astype(F32)
    G = len(POOL_WINDOWS)
    eye = jnp.eye(G, dtype=F32)
    p['poolw'] = jnp.einsum('lgcd,gh->lgchd', pool_w, eye).reshape(L, POOL_WIDTH, POOL_WIDTH).astype(BF16)
    p['pscale'] = row(pool_scale)
    p['wup'] = jnp.concatenate([w_up_a, w_up_b, w_up_c], axis=1).astype(BF16)
    p['wo'] = w_o.astype(BF16)
    p['ln1g'], p['ln1b'] = row(ln1_g), row(ln1_b)
    p['wq'] = xa_wq.astype(BF16)
    p['wkt'] = jnp.swapaxes(xa_wk, 1, 2).astype(BF16)
    p['wv'] = xa_wv.astype(BF16)
    p['wxo'] = xa_wo.astype(BF16)
    p['ln2g'], p['ln2b'] = row(ln2_g), row(ln2_b)
    rwt = jnp.swapaxes(router_w, 1, 2)
    rwt_hi = rwt.astype(BF16)
    p['rwt'] = jnp.concatenate([rwt_hi, (rwt - rwt_hi.astype(F32)).astype(BF16)], axis=1)
    p['rbt'] = jnp.broadcast_to(router_b[:, :, None], router_b.shape + (XA_TILE,)).astype(F32)
    p['ln3g'], p['ln3b'] = row(ln3_g), row(ln3_b)
    return p


def _route(route, counts):
    N = route.shape[1]
    top_idx = route[0:TOP_K].astype(jnp.int32)
    gate = route[TOP_K:2 * TOP_K].T
    rank = route[2 * TOP_K:3 * TOP_K].astype(jnp.int32)
    counts = counts[:, 0].astype(jnp.int32)
    R = MOE_STEP_BLOCKS * MOE_BLOCK
    blocks = (counts + MOE_BLOCK - 1) // MOE_BLOCK
    padded = ((counts + R - 1) // R) * R
    pad_end = jnp.cumsum(padded)
    pad_start = pad_end - padded
    ids = jnp.arange(N_EXPERTS, dtype=jnp.int32)
    start_of = jnp.sum(jnp.where(top_idx[:, :, None] == ids[None, None, :], pad_start[None, None, :], 0), axis=-1)
    dest = start_of + rank
    n_steps = N * TOP_K // R + N_EXPERTS
    step_start = jnp.arange(n_steps, dtype=jnp.int32) * R
    step_expert = jnp.minimum(jnp.sum((pad_end[None, :] <= step_start[:, None]).astype(jnp.int32), axis=1),
                              N_EXPERTS - 1)
    mine = step_expert[:, None] == ids[None, :]
    blocks_before = (step_start - jnp.sum(jnp.where(mine, pad_start[None, :], 0), axis=1)) // MOE_BLOCK
    step_halves = jnp.clip(jnp.sum(jnp.where(mine, blocks[None, :], 0), axis=1) - blocks_before, 0, MOE_STEP_BLOCKS)
    step_halves = jnp.where(step_start < pad_end[-1], step_halves, 0).astype(jnp.int32)
    n_used = (pad_end[-1] // R).astype(jnp.int32).reshape(1)
    later = jnp.where((ids[None, :] > ids[:, None]) & (counts[None, :] > 0), ids[None, :], N_EXPERTS)
    first_later = jnp.min(later, axis=1)
    next_expert = jnp.where(first_later < N_EXPERTS, first_later, ids).astype(jnp.int32)
    return gate, dest, step_expert, step_halves, n_used, next_expert


def kernel(x, mem, w_in, b_in, gla_wg2, gla_bg, gla_norm_g, sgu_ln_g, sgu_ln_b, sgu_ws, sgu_bs, pool_w, pool_scale, w_up_a, w_up_b, w_up_c, w_o, ln1_g, ln1_b, xa_wq, xa_wk, xa_wv, xa_wo, ln2_g, ln2_b, router_w, router_b, exp_w_gu, exp_b_gu, exp_w_down, exp_b_down, ln3_g, ln3_b):
    B, S, D = x.shape
    N = B * S
    p = _prep(w_in, b_in, gla_wg2, gla_bg, gla_norm_g, sgu_ln_g, sgu_ln_b, sgu_ws, sgu_bs, pool_w, pool_scale,
              w_up_a, w_up_b, w_up_c, w_o, ln1_g, ln1_b, xa_wq, xa_wk, xa_wv, xa_wo, ln2_g, ln2_b,
              router_w, router_b, ln3_g, ln3_b)
    memt = jnp.swapaxes(mem, 1, 2)
    for l in range(DEPTH):
        x1 = _mixer(x, p, l)
        kt, v = _memkv(mem, memt, p, l)
        x2, x2p, route, counts = _xattn(x1, kt, v, p, l)
        gate, dest, step_expert, step_halves, n_used, next_expert = _route(route, counts)
        n_slots = N * TOP_K + N_EXPERTS * MOE_STEP_BLOCKS * MOE_BLOCK
        xs = _sc_scatter_rows(x2p.reshape(N, D // 2), dest, n_slots)
        ys = _experts(step_expert, step_halves, n_used, next_expert, xs, exp_w_gu, exp_b_gu, exp_w_down, exp_b_down, l)
        dest_km = dest.reshape(TOP_K, N // CMB_TILE, CMB_TILE).transpose(1, 0, 2).reshape(-1)
        rows = dest_km.shape[0] // COMBINE_PARTS
        acc = None
        for part in range(COMBINE_PARTS):
            yg = _sc_gather_rows(ys, dest_km[part * rows:(part + 1) * rows])
            acc = _combine(x2.reshape(N, D), yg, gate, p, l, part, acc)
        x = acc.reshape(B, S, D)
    return x
```

```python
import functools

import jax
import jax.numpy as jnp
from jax import lax
from jax.experimental import pallas as pl
from jax.experimental.pallas import tpu as pltpu
from jax.experimental.pallas import tpu_sc as plsc

F32 = jnp.float32
BF16 = jnp.bfloat16

D_MODEL = 1024
DEPTH = 2
GLA_HEADS = 4
GLA_KEY = 256
GLA_VAL = 512
GLA_DK = 64
GLA_DV = 128
GLA_RANK = 16
GLA_TAU = 16.0
GLA_CHUNK = 64
SGU_GROUPS = 4
SGU_WIDTH = 256
SGU_GD = 64
SGU_CHUNK = 128
POOL_WINDOWS = (2, 4, 8, 16)
POOL_WIDTH = 256
POOL_GD = 64
POOL_CARRY = 32
MEM_LEN = 256
XA_HEADS = 4
XA_DH = 256
N_EXPERTS = 32
TOP_K = 4
EXPERT_FF = 1024
SWIGLU_LIMIT = 7.0
SWIGLU_ALPHA = 1.702
DEEPNORM_ALPHA = (2 * DEPTH) ** 0.25
LN_EPS = 1e-5
LANES = 128
VMEM_LIMIT = 56 * 1024 * 1024

MIX_TILE = 512
GLA_SUB = 256
XA_TILE = 1024
MOE_BLOCK = 256
MOE_STEP_BLOCKS = 4
CMB_TILE = 256
COMBINE_PARTS = 4
ROUTE_ROWS = 16
SC_GATHER_WINDOW = 128

O_QKVR = 0
O_GLOW = 2 * GLA_KEY + 2 * GLA_VAL
O_UV = O_GLOW + LANES
O_XC = O_UV + 2 * SGU_WIDTH
O_GATE = O_XC + POOL_WIDTH
N_PACK = O_GATE + 3 * D_MODEL


def _dot(a, b):
    return jnp.dot(a, b, preferred_element_type=F32)


def _dot_t0(a, b):
    return lax.dot_general(a, b, (((0,), (0,)), ((), ())), preferred_element_type=F32)


def _dot_t1(a, b):
    return lax.dot_general(a, b, (((1,), (1,)), ((), ())), preferred_element_type=F32)


def _split_bf16(x):
    hi = x.astype(BF16)
    lo = (x - hi.astype(F32)).astype(BF16)
    return hi, lo


def _layer_norm(x, g, b):
    mu = jnp.mean(x, axis=-1, keepdims=True)
    xc = x - mu
    var = jnp.mean(xc * xc, axis=-1, keepdims=True)
    return xc * lax.rsqrt(var + LN_EPS) * g + b


def _sigmoid(x):
    return 1.0 / (1.0 + jnp.exp(-x))


def _pack_bf16_pairs(x):
    H = x.shape[1] // 2
    bits = lax.bitcast_convert_type(x.astype(BF16).astype(F32), jnp.uint32)
    return (bits[:, :H] >> 16) | (bits[:, H:] & jnp.uint32(0xFFFF0000))


def _unpack_bf16_pairs(w):
    lo = lax.bitcast_convert_type(w << 16, F32)
    hi = lax.bitcast_convert_type(w & jnp.uint32(0xFFFF0000), F32)
    return jnp.concatenate([lo, hi], axis=1)


def _layer_spec(arr, l):
    nd = arr.ndim - 1
    return pl.BlockSpec((None,) + arr.shape[1:], lambda *_: (l,) + (0,) * nd, pipeline_mode=pl.Buffered(1))


def _mixer_kernel(x_ref, wcat_ref, bcat_ref, wg2_ref, bg_ref, gnorm_ref,
                  slng_ref, slnb_ref, wtril_ref, sbias_ref, poolw_ref, pscale_ref,
                  wup_ref, wo_ref, ln1g_ref, ln1b_ref,
                  out_ref,
                  state_ref, qkvr_ref, ya_ref, vln_ref, e_ref, s2_ref, s4_ref, s8_ref):
    T = MIX_TILE
    D = D_MODEL
    j = pl.program_id(1)
    x = x_ref[0]
    xb = x.astype(BF16)

    def proj(lo, hi):
        return _dot(xb, wcat_ref[:, lo:hi]) + bcat_ref[:, lo:hi]

    @pl.when(j == 0)
    def _():
        state_ref[...] = jnp.zeros_like(state_ref)
        e_ref[0:POOL_CARRY, :] = jnp.zeros((POOL_CARRY, POOL_WIDTH), F32)

    @pl.when(j > 0)
    def _():
        e_ref[0:POOL_CARRY, :] = e_ref[T:T + POOL_CARRY, :]

    qkvr_ref[...] = proj(O_QKVR, O_GLOW)
    glow = proj(O_GLOW, O_UV)
    z = _dot(glow.astype(BF16), wg2_ref[...]) + bg_ref[...]
    la = (jnp.minimum(z, 0.0) - jnp.log1p(jnp.exp(-jnp.abs(z)))) * (1.0 / GLA_TAU)
    la_hi, la_lo = _split_bf16(la)
    gate_a = _sigmoid(proj(O_GATE, O_GATE + D))

    C = GLA_CHUNK
    G = GLA_SUB
    NC = G // C
    CSH = C.bit_length() - 1
    row = lax.broadcasted_iota(jnp.int32, (G, G), 0)
    col = lax.broadcasted_iota(jnp.int32, (G, G), 1)
    same_chunk = (row >> CSH) == (col >> CSH)
    causal = same_chunk & (row >= col)
    causal_bf = jnp.where(causal, 1.0, 0.0).astype(BF16)
    chunk_bf = jnp.where(same_chunk, 1.0, 0.0).astype(BF16)
    lane = lax.broadcasted_iota(jnp.int32, (G, GLA_KEY), 1)
    ind = jnp.where((lax.broadcasted_iota(jnp.int32, (G, NC * LANES), 0) >> CSH)
                    == (lax.broadcasted_iota(jnp.int32, (G, NC * LANES), 1) >> 7), 1.0, 0.0).astype(BF16)
    gate_b = None
    for g0 in range(0, T, G):
        gr = slice(g0, g0 + G)
        lh, ll = la_hi[gr], la_lo[gr]
        b = _dot(causal_bf, lh) + _dot(causal_bf, ll)
        b_end = _dot(chunk_bf, lh) + _dot(chunk_bf, ll)
        q = qkvr_ref[gr, 0:GLA_KEY]
        k = qkvr_ref[gr, GLA_KEY:2 * GLA_KEY]
        v = qkvr_ref[gr, 2 * GLA_KEY:2 * GLA_KEY + GLA_VAL].astype(BF16)
        q_dec = q * (GLA_DK ** -0.5) * jnp.exp(b)
        k_dec = (k * jnp.exp(-b)).astype(BF16)
        k_tail = (k * jnp.exp(b_end - b)).astype(BF16)
        q_dec_bf = q_dec.astype(BF16)
        o_heads = []
        for h in range(GLA_HEADS):
            q_h = jnp.where((lane >= h * GLA_DK) & (lane < (h + 1) * GLA_DK), q_dec, 0.0).astype(BF16)
            scores = jnp.where(causal, _dot_t1(q_h, k_dec), 0.0).astype(BF16)
            o_heads.append(_dot(scores, v[:, h * GLA_DV:(h + 1) * GLA_DV]))
        o_intra = jnp.concatenate(o_heads, axis=1)
        if gate_b is None:
            gate_b = _sigmoid(proj(O_GATE + D, O_GATE + 2 * D))
        dec_all = jnp.exp(_dot_t0(lh, ind) + _dot_t0(ll, ind))
        o_inter = []
        for c in range(NC):
            rows = slice(c * C, (c + 1) * C)
            o_inter.append(_dot(q_dec_bf[rows], state_ref[...].astype(BF16)))
            kv = _dot_t0(k_tail[rows], v[rows])
            for h in range(GLA_HEADS):
                rs = slice(h * GLA_DK, (h + 1) * GLA_DK)
                cs = slice(h * GLA_DV, (h + 1) * GLA_DV)
                state_ref[rs, cs] = dec_all[rs, c * LANES:(c + 1) * LANES] * state_ref[rs, cs] + kv[rs, cs]
        o = o_intra + jnp.concatenate(o_inter, axis=0)
        for h in range(GLA_HEADS):
            cs = slice(h * GLA_DV, (h + 1) * GLA_DV)
            o_h = o[:, cs]
            ms = jnp.mean(o_h * o_h, axis=-1, keepdims=True)
            o_h = o_h * lax.rsqrt(ms + LN_EPS) * gnorm_ref[:, cs]
            r_h = qkvr_ref[gr, 2 * GLA_KEY + GLA_VAL + h * GLA_DV:2 * GLA_KEY + GLA_VAL + (h + 1) * GLA_DV]
            ya_ref[gr, cs] = (o_h * (r_h * _sigmoid(r_h))).astype(BF16)

    gate_c = _sigmoid(proj(O_GATE + 2 * D, O_GATE + 3 * D))
    uv = proj(O_UV, O_XC)
    zg = 0.5 * uv * (1.0 + lax.erf(uv * (2.0 ** -0.5)))
    u = zg[:, :SGU_WIDTH]
    vln_ref[...] = _layer_norm(zg[:, SGU_WIDTH:], slng_ref[...], slnb_ref[...])
    lane_s = lax.broadcasted_iota(jnp.int32, (SGU_CHUNK, SGU_WIDTH), 1)
    s_parts = []
    for n in range(T // SGU_CHUNK):
        vc = vln_ref[n * SGU_CHUNK:(n + 1) * SGU_CHUNK, :]
        s = sbias_ref[...]
        for g in range(SGU_GROUPS):
            vg = jnp.where((lane_s >= g * SGU_GD) & (lane_s < (g + 1) * SGU_GD), vc, 0.0).astype(BF16)
            s = s + _dot(wtril_ref[g], vg)
        s_parts.append(s)
    y_b = (u * jnp.concatenate(s_parts, axis=0)).astype(BF16)

    P = POOL_CARRY
    xc = proj(O_XC, O_GATE)
    e_ref[P:P + T, :] = xc
    s2_ref[8:P + T, :] = e_ref[8:P + T, :] + e_ref[7:P + T - 1, :]
    s4_ref[16:P + T, :] = s2_ref[16:P + T, :] + s2_ref[14:P + T - 2, :]
    s8_ref[24:P + T, :] = s4_ref[24:P + T, :] + s4_ref[20:P + T - 4, :]
    s16 = s8_ref[P:P + T, :] + s8_ref[P - 8:P + T - 8, :]
    lane_p = lax.broadcasted_iota(jnp.int32, (T, POOL_WIDTH), 1)
    tpos = lax.broadcasted_iota(jnp.int32, (T, POOL_WIDTH), 0) + (j * T + 1)
    grp = lane_p >> 6
    win = jnp.where(grp == 0, POOL_WINDOWS[0], jnp.where(grp == 1, POOL_WINDOWS[1],
                    jnp.where(grp == 2, POOL_WINDOWS[2], POOL_WINDOWS[3])))
    wsum = jnp.where(grp == 0, s2_ref[P:P + T, :], jnp.where(grp == 1, s4_ref[P:P + T, :],
                     jnp.where(grp == 2, s8_ref[P:P + T, :], s16)))
    count = jnp.minimum(tpos, win).astype(F32)
    pooled = wsum / count - xc
    y_c = (_dot(pooled.astype(BF16), poolw_ref[...]) * pscale_ref[...]).astype(BF16)

    ra, rb = GLA_VAL, GLA_VAL + SGU_WIDTH
    merged = gate_a * _dot(ya_ref[...], wup_ref[0:ra, :])
    merged += gate_b * _dot(y_b, wup_ref[ra:rb, :])
    merged += gate_c * _dot(y_c, wup_ref[rb:, :])
    h = _dot(merged.astype(BF16), wo_ref[...])
    out_ref[0] = _layer_norm(DEEPNORM_ALPHA * x + h, ln1g_ref[...], ln1b_ref[...])


_MIXER_WEIGHTS = ('wcat', 'bcat', 'wg2', 'bg', 'gnorm', 'slng', 'slnb', 'wtril', 'sbias', 'poolw', 'pscale',
                  'wup', 'wo', 'ln1g', 'ln1b')


def _mixer(x, p, l):
    B, S, D = x.shape
    T = MIX_TILE
    weights = [p[n] for n in _MIXER_WEIGHTS]
    return pl.pallas_call(
        _mixer_kernel,
        out_shape=jax.ShapeDtypeStruct((B, S, D), F32),
        grid=(B, S // T),
        in_specs=[pl.BlockSpec((1, T, D), lambda b, j: (b, j, 0))] + [_layer_spec(w, l) for w in weights],
        out_specs=pl.BlockSpec((1, T, D), lambda b, j: (b, j, 0)),
        scratch_shapes=[
            pltpu.VMEM((GLA_KEY, GLA_VAL), F32),
            pltpu.VMEM((T, 2 * GLA_KEY + 2 * GLA_VAL), F32),
            pltpu.VMEM((T, GLA_VAL), BF16),
            pltpu.VMEM((T, SGU_WIDTH), F32),
            pltpu.VMEM((T + POOL_CARRY, POOL_WIDTH), F32),
            pltpu.VMEM((T + POOL_CARRY, POOL_WIDTH), F32),
            pltpu.VMEM((T + POOL_CARRY, POOL_WIDTH), F32),
            pltpu.VMEM((T + POOL_CARRY, POOL_WIDTH), F32),
        ],
        compiler_params=pltpu.CompilerParams(dimension_semantics=("arbitrary", "arbitrary"),
                                             vmem_limit_bytes=VMEM_LIMIT),
        name="mixer",
    )(x, *weights)


def _memkv_kernel(memt_ref, mem_ref, wkt_ref, wv_ref, kt_ref, v_ref):
    kt_ref[0] = _dot(wkt_ref[...], memt_ref[0].astype(BF16)).astype(BF16)
    v_ref[0] = _dot(mem_ref[0].astype(BF16), wv_ref[...]).astype(BF16)


def _memkv(mem, memt, p, l):
    B, M, D = mem.shape
    return pl.pallas_call(
        _memkv_kernel,
        out_shape=(jax.ShapeDtypeStruct((B, D, M), BF16), jax.ShapeDtypeStruct((B, M, D), BF16)),
        grid=(B,),
        in_specs=[pl.BlockSpec((1, D, M), lambda b: (b, 0, 0)), pl.BlockSpec((1, M, D), lambda b: (b, 0, 0)),
                  _layer_spec(p['wkt'], l), _layer_spec(p['wv'], l)],
        out_specs=(pl.BlockSpec((1, D, M), lambda b: (b, 0, 0)), pl.BlockSpec((1, M, D), lambda b: (b, 0, 0))),
        compiler_params=pltpu.CompilerParams(dimension_semantics=("arbitrary",), vmem_limit_bytes=VMEM_LIMIT),
        name="memkv",
    )(memt, mem, p['wkt'], p['wv'])


def _xattn_kernel(x_ref, kt_ref, v_ref, wq_ref, wo_ref, ln2g_ref, ln2b_ref, rwt_ref, rbt_ref,
                  x2_ref, x2p_ref, route_ref, counts_ref, carry_ref):
    T = XA_TILE

    @pl.when((pl.program_id(0) == 0) & (pl.program_id(1) == 0))
    def _():
        carry_ref[...] = jnp.zeros_like(carry_ref)

    x = x_ref[0]
    q = (_dot(x.astype(BF16), wq_ref[...]) * (XA_DH ** -0.5)).astype(BF16)
    h = jnp.zeros_like(x)
    for hd in range(XA_HEADS):
        cs = slice(hd * XA_DH, (hd + 1) * XA_DH)
        s = _dot(q[:, cs], kt_ref[0, cs, :])
        e = jnp.exp(s - jnp.max(s, axis=-1, keepdims=True))
        o = _dot(e.astype(BF16), v_ref[0, :, cs]) / jnp.sum(e, axis=-1, keepdims=True)
        h = h + _dot(o.astype(BF16), wo_ref[cs, :])
    x2 = _layer_norm(DEEPNORM_ALPHA * x + h, ln2g_ref[...], ln2b_ref[...])
    x2_ref[0] = x2
    x2p_ref[0] = _pack_bf16_pairs(x2)

    E = N_EXPERTS
    hi, lo = _split_bf16(x2)
    lt = _dot_t1(rwt_ref[...], hi)
    logits = lt[0:E] + (lt[E:2 * E] + _dot_t1(rwt_ref[0:E, :], lo)) + rbt_ref[...]

    eid = lax.broadcasted_iota(jnp.int32, (E, T), 0)
    neg_inf = jnp.float32(-jnp.inf)
    rest = logits
    tops, picks = [], []
    for _ in range(TOP_K):
        m = jnp.max(rest, axis=0, keepdims=True)
        idx = jnp.min(jnp.where(rest == m, eid, E), axis=0, keepdims=True)
        pick = eid == idx
        rest = jnp.where(pick, neg_inf, rest)
        tops.append((m, idx))
        picks.append(pick)
    exps = [jnp.exp(m - tops[0][0]) for m, _ in tops]
    denom = exps[0]
    for e in exps[1:]:
        denom = denom + e

    chosen = jnp.zeros((E, T), F32)
    for pick in picks:
        chosen = chosen + jnp.where(pick, 1.0, 0.0)
    chosen_bf = chosen.astype(BF16)
    earlier = (lax.broadcasted_iota(jnp.int32, (T, T), 0) < lax.broadcasted_iota(jnp.int32, (T, T), 1))
    carry = carry_ref[...]
    before = _dot(chosen_bf, jnp.where(earlier, 1.0, 0.0).astype(BF16)) + jnp.concatenate([carry] * (T // LANES), axis=1)
    carry = carry + _dot(chosen_bf, jnp.ones((T, LANES), BF16))
    carry_ref[...] = carry
    counts_ref[...] = carry

    rid = lax.broadcasted_iota(jnp.int32, (ROUTE_ROWS, T), 0)
    route = jnp.zeros((ROUTE_ROWS, T), F32)
    for k in range(TOP_K):
        rank = jnp.sum(jnp.where(picks[k], before, 0.0), axis=0, keepdims=True)
        route = jnp.where(rid == k, tops[k][1].astype(F32), route)
        route = jnp.where(rid == TOP_K + k, exps[k] / denom, route)
        route = jnp.where(rid == 2 * TOP_K + k, rank, route)
    route_ref[...] = route


def _xattn(x, kt, v, p, l):
    B, S, D = x.shape
    T = XA_TILE
    M = MEM_LEN
    weights = [p[n] for n in ('wq', 'wxo', 'ln2g', 'ln2b', 'rwt', 'rbt')]
    return pl.pallas_call(
        _xattn_kernel,
        out_shape=(jax.ShapeDtypeStruct((B, S, D), F32), jax.ShapeDtypeStruct((B, S, D // 2), jnp.uint32),
                   jax.ShapeDtypeStruct((ROUTE_ROWS, B * S), F32), jax.ShapeDtypeStruct((N_EXPERTS, LANES), F32)),
        grid=(B, S // T),
        in_specs=[pl.BlockSpec((1, T, D), lambda b, j: (b, j, 0)),
                  pl.BlockSpec((1, D, M), lambda b, j: (b, 0, 0)),
                  pl.BlockSpec((1, M, D), lambda b, j: (b, 0, 0))] + [_layer_spec(w, l) for w in weights],
        out_specs=(pl.BlockSpec((1, T, D), lambda b, j: (b, j, 0)),
                   pl.BlockSpec((1, T, D // 2), lambda b, j: (b, j, 0)),
                   pl.BlockSpec((ROUTE_ROWS, T), lambda b, j: (0, b * (S // T) + j)),
                   pl.BlockSpec((N_EXPERTS, LANES), lambda b, j: (0, 0))),
        scratch_shapes=[pltpu.VMEM((N_EXPERTS, LANES), F32)],
        compiler_params=pltpu.CompilerParams(dimension_semantics=("arbitrary", "arbitrary"),
                                             vmem_limit_bytes=VMEM_LIMIT),
        name="xattn",
    )(x, kt, v, *weights)


def _expert_kernel(layer, se_ref, sh_ref, nu_ref, nxt_ref, xs_ref, wgu_hbm, bgu_ref, wd_hbm, bd_ref, out_ref,
                   wgu_st, wd_st, wgu_bf, wd_bf, sem):
    i = pl.program_id(0)
    F = EXPERT_FF
    halves = sh_ref[i]

    def weight_copies(e):
        return (pltpu.make_async_copy(wgu_hbm.at[layer, e], wgu_st, sem.at[0]),
                pltpu.make_async_copy(wd_hbm.at[layer, e], wd_st, sem.at[1]))

    def ffn(rows):
        xsb = _unpack_bf16_pairs(xs_ref[rows, :]).astype(BF16)
        hh = _dot(xsb, wgu_bf[...]) + bgu_ref[...]
        h_glu = jnp.minimum(hh[:, :F], SWIGLU_LIMIT)
        h_lin = jnp.clip(hh[:, F:], -SWIGLU_LIMIT, SWIGLU_LIMIT)
        a = h_glu * _sigmoid(SWIGLU_ALPHA * h_glu) * (h_lin + 1.0)
        out_ref[rows, :] = _pack_bf16_pairs(_dot(a.astype(BF16), wd_bf[...]) + bd_ref[...])

    @pl.when(halves > 0)
    def _():
        e = se_ref[i]
        prev = se_ref[jnp.maximum(i - 1, 0)]

        @pl.when(i == 0)
        def _():
            for cp in weight_copies(e):
                cp.start()

        @pl.when((i == 0) | (e != prev))
        def _():
            for cp in weight_copies(e):
                cp.wait()
            wgu_bf[...] = wgu_st[...].astype(BF16)
            wd_bf[...] = wd_st[...].astype(BF16)
            nxt = nxt_ref[e]

            @pl.when(nxt != e)
            def _():
                for cp in weight_copies(nxt):
                    cp.start()

    for n in range(1, MOE_STEP_BLOCKS + 1):
        @pl.when(halves == n)
        def _(n=n):
            ffn(slice(0, n * MOE_BLOCK))


def _experts(step_expert, step_halves, n_used, next_expert, xs, w_gu, b_gu, w_down, b_down, l):
    P, DH = xs.shape
    D = 2 * DH
    R = MOE_STEP_BLOCKS * MOE_BLOCK
    NS = P // R
    F2 = 2 * EXPERT_FF

    def row_map(i, se, sh, nu, nxt):
        return (jnp.minimum(i, nu[0] - 1), 0)

    def exp_map(i, se, sh, nu, nxt):
        return (l, se[jnp.minimum(i, nu[0] - 1)], 0, 0)

    grid_spec = pltpu.PrefetchScalarGridSpec(
        num_scalar_prefetch=4,
        grid=(NS,),
        in_specs=[pl.BlockSpec((R, DH), row_map),
                  pl.BlockSpec(memory_space=pl.ANY),
                  pl.BlockSpec((None, None, 1, F2), exp_map),
                  pl.BlockSpec(memory_space=pl.ANY),
                  pl.BlockSpec((None, None, 1, D), exp_map)],
        out_specs=pl.BlockSpec((R, DH), row_map),
        scratch_shapes=[pltpu.VMEM((D, F2), F32), pltpu.VMEM((EXPERT_FF, D), F32),
                        pltpu.VMEM((D, F2), BF16), pltpu.VMEM((EXPERT_FF, D), BF16),
                        pltpu.SemaphoreType.DMA((2,))],
    )
    return pl.pallas_call(
        functools.partial(_expert_kernel, l),
        out_shape=jax.ShapeDtypeStruct((P, DH), jnp.uint32),
        grid_spec=grid_spec,
        compiler_params=pltpu.CompilerParams(dimension_semantics=("arbitrary",), vmem_limit_bytes=VMEM_LIMIT),
        name="experts",
    )(step_expert, step_halves, n_used, next_expert, xs, w_gu, b_gu.reshape(DEPTH, N_EXPERTS, 1, F2), w_down,
      b_down.reshape(DEPTH, N_EXPERTS, 1, D))


def _sc_gather_rows(x, idx):
    M = idx.shape[0]
    D = x.shape[1]
    W = SC_GATHER_WINDOW
    mesh = plsc.VectorSubcoreMesh(core_axis_name="core", subcore_axis_name="subcore")
    n_workers = mesh.num_cores * mesh.num_subcores
    rows_per = M // n_workers
    assert rows_per * n_workers == M and rows_per % W == 0

    @pl.kernel(out_type=jax.ShapeDtypeStruct((M, D), x.dtype), mesh=mesh, name="sc_gather_rows",
               scratch_types=[pltpu.VMEM((rows_per,), jnp.int32), pltpu.VMEM((W, D), x.dtype)])
    def gather_kernel(x_hbm, i_hbm, o_hbm, idx_vmem, buf):
        wid = lax.axis_index("core") * mesh.num_subcores + lax.axis_index("subcore")
        base = wid * rows_per
        pltpu.sync_copy(i_hbm.at[pl.ds(base, rows_per)], idx_vmem)

        @pl.loop(0, rows_per // W)
        def _(j):
            pltpu.sync_copy(x_hbm.at[idx_vmem.at[pl.ds(j * W, W)]], buf)
            pltpu.sync_copy(buf, o_hbm.at[pl.ds(base + j * W, W)])

    return gather_kernel(x, idx)


def _sc_scatter_rows(x, idx, n_out):
    K, N = idx.shape
    D = x.shape[1]
    W = SC_GATHER_WINDOW
    mesh = plsc.VectorSubcoreMesh(core_axis_name="core", subcore_axis_name="subcore")
    n_workers = mesh.num_cores * mesh.num_subcores
    rows_per = N // n_workers
    assert rows_per * n_workers == N and rows_per % W == 0

    @pl.kernel(out_type=jax.ShapeDtypeStruct((n_out, D), x.dtype), mesh=mesh, name="sc_scatter_rows",
               scratch_types=[pltpu.VMEM((K * rows_per,), jnp.int32), pltpu.VMEM((W, D), x.dtype),
                              pltpu.SemaphoreType.DMA((K,))])
    def scatter_kernel(x_hbm, i_hbm, o_hbm, idx_vmem, buf, sem):
        wid = lax.axis_index("core") * mesh.num_subcores + lax.axis_index("subcore")
        base = wid * rows_per
        for k in range(K):
            pltpu.sync_copy(i_hbm.at[pl.ds(k * N + base, rows_per)], idx_vmem.at[pl.ds(k * rows_per, rows_per)])

        @pl.loop(0, rows_per // W)
        def _(j):
            pltpu.sync_copy(x_hbm.at[pl.ds(base + j * W, W)], buf)
            copies = [pltpu.make_async_copy(buf, o_hbm.at[idx_vmem.at[pl.ds(k * rows_per + j * W, W)]], sem.at[k])
                      for k in range(K)]
            for cp in copies:
                cp.start()
            for cp in copies:
                cp.wait()

    return scatter_kernel(x, idx.reshape(K * N))


def _combine_kernel(x_ref, yg_ref, gate_ref, ln3g_ref, ln3b_ref, *rest):
    out_ref = rest[-1]
    x = x_ref[...]
    g = gate_ref[...]
    y = jnp.zeros_like(x)
    for k in range(TOP_K):
        y = y + g[:, k:k + 1] * _unpack_bf16_pairs(yg_ref[k * CMB_TILE:(k + 1) * CMB_TILE, :])
    out_ref[...] = _layer_norm(DEEPNORM_ALPHA * x + y, ln3g_ref[...], ln3b_ref[...])


def _combine(x2, yg, gate, p, l, part, acc):
    N, D = x2.shape
    T = CMB_TILE
    tiles = yg.shape[0] // (TOP_K * T)
    first = part * tiles
    in_specs = [pl.BlockSpec((T, D), lambda i: (first + i, 0)), pl.BlockSpec((TOP_K * T, D // 2), lambda i: (i, 0)),
                pl.BlockSpec((T, TOP_K), lambda i: (first + i, 0)), _layer_spec(p['ln3g'], l), _layer_spec(p['ln3b'], l)]
    args = [x2, yg, gate, p['ln3g'], p['ln3b']]
    aliases = {}
    if acc is not None:
        in_specs.append(pl.BlockSpec(memory_space=pl.ANY))
        args.append(acc)
        aliases = {len(args) - 1: 0}
    return pl.pallas_call(
        _combine_kernel,
        out_shape=jax.ShapeDtypeStruct((N, D), F32),
        grid=(tiles,),
        in_specs=in_specs,
        out_specs=pl.BlockSpec((T, D), lambda i: (first + i, 0)),
        input_output_aliases=aliases,
        compiler_params=pltpu.CompilerParams(dimension_semantics=("arbitrary",), vmem_limit_bytes=VMEM_LIMIT),
        name="combine",
    )(*args)


def _prep(w_in, b_in, gla_wg2, gla_bg, gla_norm_g, sgu_ln_g, sgu_ln_b, sgu_ws, sgu_bs, pool_w, pool_scale,
          w_up_a, w_up_b, w_up_c, w_o, ln1_g, ln1_b, xa_wq, xa_wk, xa_wv, xa_wo, ln2_g, ln2_b,
          router_w, router_b, ln3_g, ln3_b):
    L = w_in.shape[0]
    o_glow = O_GLOW
    o_uv = o_glow + GLA_RANK
    row = lambda a: a.reshape(L, 1, -1).astype(F32)
    pad_last = lambda a, n: jnp.pad(a, [(0, 0)] * (a.ndim - 1) + [(0, n - a.shape[-1])])
    p = {}
    p['wcat'] = jnp.concatenate([w_in[..., :o_glow], pad_last(w_in[..., o_glow:o_uv], LANES), w_in[..., o_uv:]],
                                axis=-1).astype(BF16)
    p['bcat'] = row(jnp.concatenate([b_in[..., :o_glow], pad_last(b_in[..., o_glow:o_uv], LANES), b_in[..., o_uv:]],
                                    axis=-1))
    p['wg2'] = jnp.pad(gla_wg2, ((0, 0), (0, LANES - GLA_RANK), (0, 0))).astype(BF16)
    p['bg'] = row(gla_bg)
    p['gnorm'] = row(gla_norm_g)
    p['slng'] = row(sgu_ln_g)
    p['slnb'] = row(sgu_ln_b)
    p['wtril'] = jnp.tril(sgu_ws).astype(BF16)
    p['sbias'] = jnp.repeat(jnp.swapaxes(sgu_bs, 1, 2), SGU_GD, axis=2).astype(F32)
    G = len(POOL_WINDOWS)
    eye = jnp.eye(G, dtype=F32)
    p['poolw'] = jnp.einsum('lgcd,gh->lgchd', pool_w, eye).reshape(L, POOL_WIDTH, POOL_WIDTH).astype(BF16)
    p['pscale'] = row(pool_scale)
    p['wup'] = jnp.concatenate([w_up_a, w_up_b, w_up_c], axis=1).astype(BF16)
    p['wo'] = w_o.astype(BF16)
    p['ln1g'], p['ln1b'] = row(ln1_g), row(ln1_b)
    p['wq'] = xa_wq.astype(BF16)
    p['wkt'] = jnp.swapaxes(xa_wk, 1, 2).astype(BF16)
    p['wv'] = xa_wv.astype(BF16)
    p['wxo'] = xa_wo.astype(BF16)
    p['ln2g'], p['ln2b'] = row(ln2_g), row(ln2_b)
    rwt = jnp.swapaxes(router_w, 1, 2)
    rwt_hi = rwt.astype(BF16)
    p['rwt'] = jnp.concatenate([rwt_hi, (rwt - rwt_hi.astype(F32)).astype(BF16)], axis=1)
    p['rbt'] = jnp.broadcast_to(router_b[:, :, None], router_b.shape + (XA_TILE,)).astype(F32)
    p['ln3g'], p['ln3b'] = row(ln3_g), row(ln3_b)
    return p


def _route(route, counts):
    N = route.shape[1]
    top_idx = route[0:TOP_K].astype(jnp.int32)
    gate = route[TOP_K:2 * TOP_K].T
    rank = route[2 * TOP_K:3 * TOP_K].astype(jnp.int32)
    counts = counts[:, 0].astype(jnp.int32)
    R = MOE_STEP_BLOCKS * MOE_BLOCK
    blocks = (counts + MOE_BLOCK - 1) // MOE_BLOCK
    padded = ((counts + R - 1) // R) * R
    pad_end = jnp.cumsum(padded)
    pad_start = pad_end - padded
    ids = jnp.arange(N_EXPERTS, dtype=jnp.int32)
    start_of = jnp.sum(jnp.where(top_idx[:, :, None] == ids[None, None, :], pad_start[None, None, :], 0), axis=-1)
    dest = start_of + rank
    n_steps = N * TOP_K // R + N_EXPERTS
    step_start = jnp.arange(n_steps, dtype=jnp.int32) * R
    step_expert = jnp.minimum(jnp.sum((pad_end[None, :] <= step_start[:, None]).astype(jnp.int32), axis=1),
                              N_EXPERTS - 1)
    mine = step_expert[:, None] == ids[None, :]
    blocks_before = (step_start - jnp.sum(jnp.where(mine, pad_start[None, :], 0), axis=1)) // MOE_BLOCK
    step_halves = jnp.clip(jnp.sum(jnp.where(mine, blocks[None, :], 0), axis=1) - blocks_before, 0, MOE_STEP_BLOCKS)
    step_halves = jnp.where(step_start < pad_end[-1], step_halves, 0).astype(jnp.int32)
    n_used = (pad_end[-1] // R).astype(jnp.int32).reshape(1)
    later = jnp.where((ids[None, :] > ids[:, None]) & (counts[None, :] > 0), ids[None, :], N_EXPERTS)
    first_later = jnp.min(later, axis=1)
    next_expert = jnp.where(first_later < N_EXPERTS, first_later, ids).astype(jnp.int32)
    return gate, dest, step_expert, step_halves, n_used, next_expert


def kernel(x, mem, w_in, b_in, gla_wg2, gla_bg, gla_norm_g, sgu_ln_g, sgu_ln_b, sgu_ws, sgu_bs, pool_w, pool_scale, w_up_a, w_up_b, w_up_c, w_o, ln1_g, ln1_b, xa_wq, xa_wk, xa_wv, xa_wo, ln2_g, ln2_b, router_w, router_b, exp_w_gu, exp_b_gu, exp_w_down, exp_b_down, ln3_g, ln3_b):
    B, S, D = x.shape
    N = B * S
    p = _prep(w_in, b_in, gla_wg2, gla_bg, gla_norm_g, sgu_ln_g, sgu_ln_b, sgu_ws, sgu_bs, pool_w, pool_scale,
              w_up_a, w_up_b, w_up_c, w_o, ln1_g, ln1_b, xa_wq, xa_wk, xa_wv, xa_wo, ln2_g, ln2_b,
              router_w, router_b, ln3_g, ln3_b)
    memt = jnp.swapaxes(mem, 1, 2)
    for l in range(DEPTH):
        x1 = _mixer(x, p, l)
        kt, v = _memkv(mem, memt, p, l)
        x2, x2p, route, counts = _xattn(x1, kt, v, p, l)
        gate, dest, step_expert, step_halves, n_used, next_expert = _route(route, counts)
        n_slots = N * TOP_K + N_EXPERTS * MOE_STEP_BLOCKS * MOE_BLOCK
        xs = _sc_scatter_rows(x2p.reshape(N, D // 2), dest, n_slots)
        ys = _experts(step_expert, step_halves, n_used, next_expert, xs, exp_w_gu, exp_b_gu, exp_w_down, exp_b_down, l)
        dest_km = dest.reshape(TOP_K, N // CMB_TILE, CMB_TILE).transpose(1, 0, 2).reshape(-1)
        rows = dest_km.shape[0] // COMBINE_PARTS
        acc = None
        for part in range(COMBINE_PARTS):
            yg = _sc_gather_rows(ys, dest_km[part * rows:(part + 1) * rows])
            acc = _combine(x2.reshape(N, D), yg, gate, p, l, part, acc)
        x = acc.reshape(B, S, D)
    return x
```

```python
import functools

import jax
import jax.numpy as jnp
from jax import lax
from jax.experimental import pallas as pl
from jax.experimental.pallas import tpu as pltpu
from jax.experimental.pallas import tpu_sc as plsc

F32 = jnp.float32
BF16 = jnp.bfloat16

D_MODEL = 1024
DEPTH = 2
GLA_HEADS = 4
GLA_KEY = 256
GLA_VAL = 512
GLA_DK = 64
GLA_DV = 128
GLA_RANK = 16
GLA_TAU = 16.0
GLA_CHUNK = 64
SGU_GROUPS = 4
SGU_WIDTH = 256
SGU_GD = 64
SGU_CHUNK = 128
POOL_WINDOWS = (2, 4, 8, 16)
POOL_WIDTH = 256
POOL_GD = 64
POOL_CARRY = 32
MEM_LEN = 256
XA_HEADS = 4
XA_DH = 256
N_EXPERTS = 32
TOP_K = 4
EXPERT_FF = 1024
SWIGLU_LIMIT = 7.0
SWIGLU_ALPHA = 1.702
DEEPNORM_ALPHA = (2 * DEPTH) ** 0.25
LN_EPS = 1e-5
LANES = 128
VMEM_LIMIT = 56 * 1024 * 1024

MIX_TILE = 512
GLA_SUB = 256
XA_TILE = 1024
MOE_BLOCK = 256
MOE_STEP_BLOCKS = 4
CMB_TILE = 256
COMBINE_PARTS = 4
ROUTE_ROWS = 16
SC_GATHER_WINDOW = 128

O_QKVR = 0
O_GLOW = 2 * GLA_KEY + 2 * GLA_VAL
O_UV = O_GLOW + LANES
O_XC = O_UV + 2 * SGU_WIDTH
O_GATE = O_XC + POOL_WIDTH
N_PACK = O_GATE + 3 * D_MODEL


def _dot(a, b):
    return jnp.dot(a, b, preferred_element_type=F32)


def _dot_t0(a, b):
    return lax.dot_general(a, b, (((0,), (0,)), ((), ())), preferred_element_type=F32)


def _dot_t1(a, b):
    return lax.dot_general(a, b, (((1,), (1,)), ((), ())), preferred_element_type=F32)


def _split_bf16(x):
    hi = x.astype(BF16)
    lo = (x - hi.astype(F32)).astype(BF16)
    return hi, lo


def _layer_norm(x, g, b):
    mu = jnp.mean(x, axis=-1, keepdims=True)
    xc = x - mu
    var = jnp.mean(xc * xc, axis=-1, keepdims=True)
    return xc * lax.rsqrt(var + LN_EPS) * g + b


def _sigmoid(x):
    return 1.0 / (1.0 + jnp.exp(-x))


def _pack_bf16_pairs(x):
    H = x.shape[1] // 2
    bits = lax.bitcast_convert_type(x.astype(BF16).astype(F32), jnp.uint32)
    return (bits[:, :H] >> 16) | (bits[:, H:] & jnp.uint32(0xFFFF0000))


def _unpack_bf16_pairs(w):
    lo = lax.bitcast_convert_type(w << 16, F32)
    hi = lax.bitcast_convert_type(w & jnp.uint32(0xFFFF0000), F32)
    return jnp.concatenate([lo, hi], axis=1)


def _layer_spec(arr, l):
    nd = arr.ndim - 1
    return pl.BlockSpec((None,) + arr.shape[1:], lambda *_: (l,) + (0,) * nd, pipeline_mode=pl.Buffered(1))


def _mixer_kernel(x_ref, wcat_ref, bcat_ref, wg2_ref, bg_ref, gnorm_ref,
                  slng_ref, slnb_ref, wtril_ref, sbias_ref, poolw_ref, pscale_ref,
                  wup_ref, wo_ref, ln1g_ref, ln1b_ref,
                  out_ref,
                  state_ref, qkvr_ref, ya_ref, vln_ref, e_ref, s2_ref, s4_ref, s8_ref):
    T = MIX_TILE
    D = D_MODEL
    j = pl.program_id(1)
    x = x_ref[0]
    xb = x.astype(BF16)

    def proj(lo, hi):
        return _dot(xb, wcat_ref[:, lo:hi]) + bcat_ref[:, lo:hi]

    @pl.when(j == 0)
    def _():
        state_ref[...] = jnp.zeros_like(state_ref)
        e_ref[0:POOL_CARRY, :] = jnp.zeros((POOL_CARRY, POOL_WIDTH), F32)

    @pl.when(j > 0)
    def _():
        e_ref[0:POOL_CARRY, :] = e_ref[T:T + POOL_CARRY, :]

    qkvr_ref[...] = proj(O_QKVR, O_GLOW)
    glow = proj(O_GLOW, O_UV)
    z = _dot(glow.astype(BF16), wg2_ref[...]) + bg_ref[...]
    la = (jnp.minimum(z, 0.0) - jnp.log1p(jnp.exp(-jnp.abs(z)))) * (1.0 / GLA_TAU)
    la_hi, la_lo = _split_bf16(la)
    gate_a = _sigmoid(proj(O_GATE, O_GATE + D))

    C = GLA_CHUNK
    G = GLA_SUB
    NC = G // C
    CSH = C.bit_length() - 1
    row = lax.broadcasted_iota(jnp.int32, (G, G), 0)
    col = lax.broadcasted_iota(jnp.int32, (G, G), 1)
    same_chunk = (row >> CSH) == (col >> CSH)
    causal = same_chunk & (row >= col)
    causal_bf = jnp.where(causal, 1.0, 0.0).astype(BF16)
    chunk_bf = jnp.where(same_chunk, 1.0, 0.0).astype(BF16)
    lane = lax.broadcasted_iota(jnp.int32, (G, GLA_KEY), 1)
    ind = jnp.where((lax.broadcasted_iota(jnp.int32, (G, NC * LANES), 0) >> CSH)
                    == (lax.broadcasted_iota(jnp.int32, (G, NC * LANES), 1) >> 7), 1.0, 0.0).astype(BF16)
    gate_b = None
    for g0 in range(0, T, G):
        gr = slice(g0, g0 + G)
        lh, ll = la_hi[gr], la_lo[gr]
        b = _dot(causal_bf, lh) + _dot(causal_bf, ll)
        b_end = _dot(chunk_bf, lh) + _dot(chunk_bf, ll)
        q = qkvr_ref[gr, 0:GLA_KEY]
        k = qkvr_ref[gr, GLA_KEY:2 * GLA_KEY]
        v = qkvr_ref[gr, 2 * GLA_KEY:2 * GLA_KEY + GLA_VAL].astype(BF16)
        q_dec = q * (GLA_DK ** -0.5) * jnp.exp(b)
        k_dec = (k * jnp.exp(-b)).astype(BF16)
        k_tail = (k * jnp.exp(b_end - b)).astype(BF16)
        q_dec_bf = q_dec.astype(BF16)
        o_heads = []
        for h in range(GLA_HEADS):
            q_h = jnp.where((lane >= h * GLA_DK) & (lane < (h + 1) * GLA_DK), q_dec, 0.0).astype(BF16)
            scores = jnp.where(causal, _dot_t1(q_h, k_dec), 0.0).astype(BF16)
            o_heads.append(_dot(scores, v[:, h * GLA_DV:(h + 1) * GLA_DV]))
        o_intra = jnp.concatenate(o_heads, axis=1)
        if gate_b is None:
            gate_b = _sigmoid(proj(O_GATE + D, O_GATE + 2 * D))
        dec_all = jnp.exp(_dot_t0(lh, ind) + _dot_t0(ll, ind))
        o_inter = []
        for c in range(NC):
            rows = slice(c * C, (c + 1) * C)
            o_inter.append(_dot(q_dec_bf[rows], state_ref[...].astype(BF16)))
            kv = _dot_t0(k_tail[rows], v[rows])
            for h in range(GLA_HEADS):
                rs = slice(h * GLA_DK, (h + 1) * GLA_DK)
                cs = slice(h * GLA_DV, (h + 1) * GLA_DV)
                state_ref[rs, cs] = dec_all[rs, c * LANES:(c + 1) * LANES] * state_ref[rs, cs] + kv[rs, cs]
        o = o_intra + jnp.concatenate(o_inter, axis=0)
        for h in range(GLA_HEADS):
            cs = slice(h * GLA_DV, (h + 1) * GLA_DV)
            o_h = o[:, cs]
            ms = jnp.mean(o_h * o_h, axis=-1, keepdims=True)
            o_h = o_h * lax.rsqrt(ms + LN_EPS) * gnorm_ref[:, cs]
            r_h = qkvr_ref[gr, 2 * GLA_KEY + GLA_VAL + h * GLA_DV:2 * GLA_KEY + GLA_VAL + (h + 1) * GLA_DV]
            ya_ref[gr, cs] = (o_h * (r_h * _sigmoid(r_h))).astype(BF16)

    gate_c = _sigmoid(proj(O_GATE + 2 * D, O_GATE + 3 * D))
    uv = proj(O_UV, O_XC)
    zg = 0.5 * uv * (1.0 + lax.erf(uv * (2.0 ** -0.5)))
    u = zg[:, :SGU_WIDTH]
    vln_ref[...] = _layer_norm(zg[:, SGU_WIDTH:], slng_ref[...], slnb_ref[...])
    lane_s = lax.broadcasted_iota(jnp.int32, (SGU_CHUNK, SGU_WIDTH), 1)
    s_parts = []
    for n in range(T // SGU_CHUNK):
        vc = vln_ref[n * SGU_CHUNK:(n + 1) * SGU_CHUNK, :]
        s = sbias_ref[...]
        for g in range(SGU_GROUPS):
            vg = jnp.where((lane_s >= g * SGU_GD) & (lane_s < (g + 1) * SGU_GD), vc, 0.0).astype(BF16)
            s = s + _dot(wtril_ref[g], vg)
        s_parts.append(s)
    y_b = (u * jnp.concatenate(s_parts, axis=0)).astype(BF16)

    P = POOL_CARRY
    xc = proj(O_XC, O_GATE)
    e_ref[P:P + T, :] = xc
    s2_ref[8:P + T, :] = e_ref[8:P + T, :] + e_ref[7:P + T - 1, :]
    s4_ref[16:P + T, :] = s2_ref[16:P + T, :] + s2_ref[14:P + T - 2, :]
    s8_ref[24:P + T, :] = s4_ref[24:P + T, :] + s4_ref[20:P + T - 4, :]
    s16 = s8_ref[P:P + T, :] + s8_ref[P - 8:P + T - 8, :]
    lane_p = lax.broadcasted_iota(jnp.int32, (T, POOL_WIDTH), 1)
    tpos = lax.broadcasted_iota(jnp.int32, (T, POOL_WIDTH), 0) + (j * T + 1)
    grp = lane_p >> 6
    win = jnp.where(grp == 0, POOL_WINDOWS[0], jnp.where(grp == 1, POOL_WINDOWS[1],
                    jnp.where(grp == 2, POOL_WINDOWS[2], POOL_WINDOWS[3])))
    wsum = jnp.where(grp == 0, s2_ref[P:P + T, :], jnp.where(grp == 1, s4_ref[P:P + T, :],
                     jnp.where(grp == 2, s8_ref[P:P + T, :], s16)))
    count = jnp.minimum(tpos, win).astype(F32)
    pooled = wsum / count - xc
    y_c = (_dot(pooled.astype(BF16), poolw_ref[...]) * pscale_ref[...]).astype(BF16)

    ra, rb = GLA_VAL, GLA_VAL + SGU_WIDTH
    merged = gate_a * _dot(ya_ref[...], wup_ref[0:ra, :])
    merged += gate_b * _dot(y_b, wup_ref[ra:rb, :])
    merged += gate_c * _dot(y_c, wup_ref[rb:, :])
    h = _dot(merged.astype(BF16), wo_ref[...])
    out_ref[0] = _layer_norm(DEEPNORM_ALPHA * x + h, ln1g_ref[...], ln1b_ref[...])


_MIXER_WEIGHTS = ('wcat', 'bcat', 'wg2', 'bg', 'gnorm', 'slng', 'slnb', 'wtril', 'sbias', 'poolw', 'pscale',
                  'wup', 'wo', 'ln1g', 'ln1b')


def _mixer(x, p, l):
    B, S, D = x.shape
    T = MIX_TILE
    weights = [p[n] for n in _MIXER_WEIGHTS]
    return pl.pallas_call(
        _mixer_kernel,
        out_shape=jax.ShapeDtypeStruct((B, S, D), F32),
        grid=(B, S // T),
        in_specs=[pl.BlockSpec((1, T, D), lambda b, j: (b, j, 0))] + [_layer_spec(w, l) for w in weights],
        out_specs=pl.BlockSpec((1, T, D), lambda b, j: (b, j, 0)),
        scratch_shapes=[
            pltpu.VMEM((GLA_KEY, GLA_VAL), F32),
            pltpu.VMEM((T, 2 * GLA_KEY + 2 * GLA_VAL), F32),
            pltpu.VMEM((T, GLA_VAL), BF16),
            pltpu.VMEM((T, SGU_WIDTH), F32),
            pltpu.VMEM((T + POOL_CARRY, POOL_WIDTH), F32),
            pltpu.VMEM((T + POOL_CARRY, POOL_WIDTH), F32),
            pltpu.VMEM((T + POOL_CARRY, POOL_WIDTH), F32),
            pltpu.VMEM((T + POOL_CARRY, POOL_WIDTH), F32),
        ],
        compiler_params=pltpu.CompilerParams(dimension_semantics=("arbitrary", "arbitrary"),
                                             vmem_limit_bytes=VMEM_LIMIT),
        name="mixer",
    )(x, *weights)


def _memkv_kernel(memt_ref, mem_ref, wkt_ref, wv_ref, kt_ref, v_ref):
    kt_ref[0] = _dot(wkt_ref[...], memt_ref[0].astype(BF16)).astype(BF16)
    v_ref[0] = _dot(mem_ref[0].astype(BF16), wv_ref[...]).astype(BF16)


def _memkv(mem, memt, p, l):
    B, M, D = mem.shape
    return pl.pallas_call(
        _memkv_kernel,
        out_shape=(jax.ShapeDtypeStruct((B, D, M), BF16), jax.ShapeDtypeStruct((B, M, D), BF16)),
        grid=(B,),
        in_specs=[pl.BlockSpec((1, D, M), lambda b: (b, 0, 0)), pl.BlockSpec((1, M, D), lambda b: (b, 0, 0)),
                  _layer_spec(p['wkt'], l), _layer_spec(p['wv'], l)],
        out_specs=(pl.BlockSpec((1, D, M), lambda b: (b, 0, 0)), pl.BlockSpec((1, M, D), lambda b: (b, 0, 0))),
        compiler_params=pltpu.CompilerParams(dimension_semantics=("arbitrary",), vmem_limit_bytes=VMEM_LIMIT),
        name="memkv",
    )(memt, mem, p['wkt'], p['wv'])


def _xattn_kernel(x_ref, kt_ref, v_ref, wq_ref, wo_ref, ln2g_ref, ln2b_ref, rwt_ref, rbt_ref,
                  x2_ref, x2p_ref, route_ref, counts_ref, carry_ref):
    T = XA_TILE

    @pl.when((pl.program_id(0) == 0) & (pl.program_id(1) == 0))
    def _():
        carry_ref[...] = jnp.zeros_like(carry_ref)

    x = x_ref[0]
    q = (_dot(x.astype(BF16), wq_ref[...]) * (XA_DH ** -0.5)).astype(BF16)
    h = jnp.zeros_like(x)
    for hd in range(XA_HEADS):
        cs = slice(hd * XA_DH, (hd + 1) * XA_DH)
        s = _dot(q[:, cs], kt_ref[0, cs, :])
        e = jnp.exp(s - jnp.max(s, axis=-1, keepdims=True))
        o = _dot(e.astype(BF16), v_ref[0, :, cs]) / jnp.sum(e, axis=-1, keepdims=True)
        h = h + _dot(o.astype(BF16), wo_ref[cs, :])
    x2 = _layer_norm(DEEPNORM_ALPHA * x + h, ln2g_ref[...], ln2b_ref[...])
    x2_ref[0] = x2
    x2p_ref[0] = _pack_bf16_pairs(x2)

    E = N_EXPERTS
    hi, lo = _split_bf16(x2)
    lt = _dot_t1(rwt_ref[...], hi)
    logits = lt[0:E] + (lt[E:2 * E] + _dot_t1(rwt_ref[0:E, :], lo)) + rbt_ref[...]

    eid = lax.broadcasted_iota(jnp.int32, (E, T), 0)
    neg_inf = jnp.float32(-jnp.inf)
    rest = logits
    tops, picks = [], []
    for _ in range(TOP_K):
        m = jnp.max(rest, axis=0, keepdims=True)
        idx = jnp.min(jnp.where(rest == m, eid, E), axis=0, keepdims=True)
        pick = eid == idx
        rest = jnp.where(pick, neg_inf, rest)
        tops.append((m, idx))
        picks.append(pick)
    exps = [jnp.exp(m - tops[0][0]) for m, _ in tops]
    denom = exps[0]
    for e in exps[1:]:
        denom = denom + e

    chosen = jnp.zeros((E, T), F32)
    for pick in picks:
        chosen = chosen + jnp.where(pick, 1.0, 0.0)
    chosen_bf = chosen.astype(BF16)
    earlier = (lax.broadcasted_iota(jnp.int32, (T, T), 0) < lax.broadcasted_iota(jnp.int32, (T, T), 1))
    carry = carry_ref[...]
    before = _dot(chosen_bf, jnp.where(earlier, 1.0, 0.0).astype(BF16)) + jnp.concatenate([carry] * (T // LANES), axis=1)
    carry = carry + _dot(chosen_bf, jnp.ones((T, LANES), BF16))
    carry_ref[...] = carry
    counts_ref[...] = carry

    rid = lax.broadcasted_iota(jnp.int32, (ROUTE_ROWS, T), 0)
    route = jnp.zeros((ROUTE_ROWS, T), F32)
    for k in range(TOP_K):
        rank = jnp.sum(jnp.where(picks[k], before, 0.0), axis=0, keepdims=True)
        route = jnp.where(rid == k, tops[k][1].astype(F32), route)
        route = jnp.where(rid == TOP_K + k, exps[k] / denom, route)
        route = jnp.where(rid == 2 * TOP_K + k, rank, route)
    route_ref[...] = route


def _xattn(x, kt, v, p, l):
    B, S, D = x.shape
    T = XA_TILE
    M = MEM_LEN
    weights = [p[n] for n in ('wq', 'wxo', 'ln2g', 'ln2b', 'rwt', 'rbt')]
    return pl.pallas_call(
        _xattn_kernel,
        out_shape=(jax.ShapeDtypeStruct((B, S, D), F32), jax.ShapeDtypeStruct((B, S, D // 2), jnp.uint32),
                   jax.ShapeDtypeStruct((ROUTE_ROWS, B * S), F32), jax.ShapeDtypeStruct((N_EXPERTS, LANES), F32)),
        grid=(B, S // T),
        in_specs=[pl.BlockSpec((1, T, D), lambda b, j: (b, j, 0)),
                  pl.BlockSpec((1, D, M), lambda b, j: (b, 0, 0)),
                  pl.BlockSpec((1, M, D), lambda b, j: (b, 0, 0))] + [_layer_spec(w, l) for w in weights],
        out_specs=(pl.BlockSpec((1, T, D), lambda b, j: (b, j, 0)),
                   pl.BlockSpec((1, T, D // 2), lambda b, j: (b, j, 0)),
                   pl.BlockSpec((ROUTE_ROWS, T), lambda b, j: (0, b * (S // T) + j)),
                   pl.BlockSpec((N_EXPERTS, LANES), lambda b, j: (0, 0))),
        scratch_shapes=[pltpu.VMEM((N_EXPERTS, LANES), F32)],
        compiler_params=pltpu.CompilerParams(dimension_semantics=("arbitrary", "arbitrary"),
                                             vmem_limit_bytes=VMEM_LIMIT),
        name="xattn",
    )(x, kt, v, *weights)


def _expert_kernel(layer, se_ref, sh_ref, nu_ref, nxt_ref, xs_ref, wgu_hbm, bgu_ref, wd_hbm, bd_ref, out_ref,
                   wgu_st, wd_st, wgu_bf, wd_bf, sem):
    i = pl.program_id(0)
    F = EXPERT_FF
    halves = sh_ref[i]

    def weight_copies(e):
        return (pltpu.make_async_copy(wgu_hbm.at[layer, e], wgu_st, sem.at[0]),
                pltpu.make_async_copy(wd_hbm.at[layer, e], wd_st, sem.at[1]))

    def ffn(rows):
        xsb = _unpack_bf16_pairs(xs_ref[rows, :]).astype(BF16)
        hh = _dot(xsb, wgu_bf[...]) + bgu_ref[...]
        h_glu = jnp.minimum(hh[:, :F], SWIGLU_LIMIT)
        h_lin = jnp.clip(hh[:, F:], -SWIGLU_LIMIT, SWIGLU_LIMIT)
        a = h_glu * _sigmoid(SWIGLU_ALPHA * h_glu) * (h_lin + 1.0)
        out_ref[rows, :] = _pack_bf16_pairs(_dot(a.astype(BF16), wd_bf[...]) + bd_ref[...])

    @pl.when(halves > 0)
    def _():
        e = se_ref[i]
        prev = se_ref[jnp.maximum(i - 1, 0)]

        @pl.when(i == 0)
        def _():
            for cp in weight_copies(e):
                cp.start()

        @pl.when((i == 0) | (e != prev))
        def _():
            for cp in weight_copies(e):
                cp.wait()
            wgu_bf[...] = wgu_st[...].astype(BF16)
            wd_bf[...] = wd_st[...].astype(BF16)
            nxt = nxt_ref[e]

            @pl.when(nxt != e)
            def _():
                for cp in weight_copies(nxt):
                    cp.start()

    for n in range(1, MOE_STEP_BLOCKS + 1):
        @pl.when(halves == n)
        def _(n=n):
            ffn(slice(0, n * MOE_BLOCK))


def _experts(step_expert, step_halves, n_used, next_expert, xs, w_gu, b_gu, w_down, b_down, l):
    P, DH = xs.shape
    D = 2 * DH
    R = MOE_STEP_BLOCKS * MOE_BLOCK
    NS = P // R
    F2 = 2 * EXPERT_FF

    def row_map(i, se, sh, nu, nxt):
        return (jnp.minimum(i, nu[0] - 1), 0)

    def exp_map(i, se, sh, nu, nxt):
        return (l, se[jnp.minimum(i, nu[0] - 1)], 0, 0)

    grid_spec = pltpu.PrefetchScalarGridSpec(
        num_scalar_prefetch=4,
        grid=(NS,),
        in_specs=[pl.BlockSpec((R, DH), row_map),
                  pl.BlockSpec(memory_space=pl.ANY),
                  pl.BlockSpec((None, None, 1, F2), exp_map),
                  pl.BlockSpec(memory_space=pl.ANY),
                  pl.BlockSpec((None, None, 1, D), exp_map)],
        out_specs=pl.BlockSpec((R, DH), row_map),
        scratch_shapes=[pltpu.VMEM((D, F2), F32), pltpu.VMEM((EXPERT_FF, D), F32),
                        pltpu.VMEM((D, F2), BF16), pltpu.VMEM((EXPERT_FF, D), BF16),
                        pltpu.SemaphoreType.DMA((2,))],
    )
    return pl.pallas_call(
        functools.partial(_expert_kernel, l),
        out_shape=jax.ShapeDtypeStruct((P, DH), jnp.uint32),
        grid_spec=grid_spec,
        compiler_params=pltpu.CompilerParams(dimension_semantics=("arbitrary",), vmem_limit_bytes=VMEM_LIMIT),
        name="experts",
    )(step_expert, step_halves, n_used, next_expert, xs, w_gu, b_gu.reshape(DEPTH, N_EXPERTS, 1, F2), w_down,
      b_down.reshape(DEPTH, N_EXPERTS, 1, D))


def _sc_gather_rows(x, idx):
    M = idx.shape[0]
    D = x.shape[1]
    W = SC_GATHER_WINDOW
    mesh = plsc.VectorSubcoreMesh(core_axis_name="core", subcore_axis_name="subcore")
    n_workers = mesh.num_cores * mesh.num_subcores
    rows_per = M // n_workers
    assert rows_per * n_workers == M and rows_per % W == 0

    @pl.kernel(out_type=jax.ShapeDtypeStruct((M, D), x.dtype), mesh=mesh, name="sc_gather_rows",
               scratch_types=[pltpu.VMEM((rows_per,), jnp.int32), pltpu.VMEM((W, D), x.dtype)])
    def gather_kernel(x_hbm, i_hbm, o_hbm, idx_vmem, buf):
        wid = lax.axis_index("core") * mesh.num_subcores + lax.axis_index("subcore")
        base = wid * rows_per
        pltpu.sync_copy(i_hbm.at[pl.ds(base, rows_per)], idx_vmem)

        @pl.loop(0, rows_per // W)
        def _(j):
            pltpu.sync_copy(x_hbm.at[idx_vmem.at[pl.ds(j * W, W)]], buf)
            pltpu.sync_copy(buf, o_hbm.at[pl.ds(base + j * W, W)])

    return gather_kernel(x, idx)


def _sc_scatter_rows(x, idx, n_out):
    K, N = idx.shape
    D = x.shape[1]
    W = SC_GATHER_WINDOW
    mesh = plsc.VectorSubcoreMesh(core_axis_name="core", subcore_axis_name="subcore")
    n_workers = mesh.num_cores * mesh.num_subcores
    rows_per = N // n_workers
    assert rows_per * n_workers == N and rows_per % W == 0

    @pl.kernel(out_type=jax.ShapeDtypeStruct((n_out, D), x.dtype), mesh=mesh, name="sc_scatter_rows",
               scratch_types=[pltpu.VMEM((K * rows_per,), jnp.int32), pltpu.VMEM((W, D), x.dtype),
                              pltpu.SemaphoreType.DMA((K,))])
    def scatter_kernel(x_hbm, i_hbm, o_hbm, idx_vmem, buf, sem):
        wid = lax.axis_index("core") * mesh.num_subcores + lax.axis_index("subcore")
        base = wid * rows_per
        for k in range(K):
            pltpu.sync_copy(i_hbm.at[pl.ds(k * N + base, rows_per)], idx_vmem.at[pl.ds(k * rows_per, rows_per)])

        @pl.loop(0, rows_per // W)
        def _(j):
            pltpu.sync_copy(x_hbm.at[pl.ds(base + j * W, W)], buf)
            copies = [pltpu.make_async_copy(buf, o_hbm.at[idx_vmem.at[pl.ds(k * rows_per + j * W, W)]], sem.at[k])
                      for k in range(K)]
            for cp in copies:
                cp.start()
            for cp in copies:
                cp.wait()

    return scatter_kernel(x, idx.reshape(K * N))


def _combine_kernel(x_ref, yg_ref, gate_ref, ln3g_ref, ln3b_ref, *rest):
    out_ref = rest[-1]
    x = x_ref[...]
    g = gate_ref[...]
    y = jnp.zeros_like(x)
    for k in range(TOP_K):
        y = y + g[:, k:k + 1] * _unpack_bf16_pairs(yg_ref[k * CMB_TILE:(k + 1) * CMB_TILE, :])
    out_ref[...] = _layer_norm(DEEPNORM_ALPHA * x + y, ln3g_ref[...], ln3b_ref[...])


def _combine(x2, yg, gate, p, l, part, acc):
    N, D = x2.shape
    T = CMB_TILE
    tiles = yg.shape[0] // (TOP_K * T)
    first = part * tiles
    in_specs = [pl.BlockSpec((T, D), lambda i: (first + i, 0)), pl.BlockSpec((TOP_K * T, D // 2), lambda i: (i, 0)),
                pl.BlockSpec((T, TOP_K), lambda i: (first + i, 0)), _layer_spec(p['ln3g'], l), _layer_spec(p['ln3b'], l)]
    args = [x2, yg, gate, p['ln3g'], p['ln3b']]
    aliases = {}
    if acc is not None:
        in_specs.append(pl.BlockSpec(memory_space=pl.ANY))
        args.append(acc)
        aliases = {len(args) - 1: 0}
    return pl.pallas_call(
        _combine_kernel,
        out_shape=jax.ShapeDtypeStruct((N, D), F32),
        grid=(tiles,),
        in_specs=in_specs,
        out_specs=pl.BlockSpec((T, D), lambda i: (first + i, 0)),
        input_output_aliases=aliases,
        compiler_params=pltpu.CompilerParams(dimension_semantics=("arbitrary",), vmem_limit_bytes=VMEM_LIMIT),
        name="combine",
    )(*args)


def _prep(w_in, b_in, gla_wg2, gla_bg, gla_norm_g, sgu_ln_g, sgu_ln_b, sgu_ws, sgu_bs, pool_w, pool_scale,
          w_up_a, w_up_b, w_up_c, w_o, ln1_g, ln1_b, xa_wq, xa_wk, xa_wv, xa_wo, ln2_g, ln2_b,
          router_w, router_b, ln3_g, ln3_b):
    L = w_in.shape[0]
    o_glow = O_GLOW
    o_uv = o_glow + GLA_RANK
    row = lambda a: a.reshape(L, 1, -1).astype(F32)
    pad_last = lambda a, n: jnp.pad(a, [(0, 0)] * (a.ndim - 1) + [(0, n - a.shape[-1])])
    p = {}
    p['wcat'] = jnp.concatenate([w_in[..., :o_glow], pad_last(w_in[..., o_glow:o_uv], LANES), w_in[..., o_uv:]],
                                axis=-1).astype(BF16)
    p['bcat'] = row(jnp.concatenate([b_in[..., :o_glow], pad_last(b_in[..., o_glow:o_uv], LANES), b_in[..., o_uv:]],
                                    axis=-1))
    p['wg2'] = jnp.pad(gla_wg2, ((0, 0), (0, LANES - GLA_RANK), (0, 0))).astype(BF16)
    p['bg'] = row(gla_bg)
    p['gnorm'] = row(gla_norm_g)
    p['slng'] = row(sgu_ln_g)
    p['slnb'] = row(sgu_ln_b)
    p['wtril'] = jnp.tril(sgu_ws).astype(BF16)
    p['sbias'] = jnp.repeat(jnp.swapaxes(sgu_bs, 1, 2), SGU_GD, axis=2).astype(F32)
    G = len(POOL_WINDOWS)
    eye = jnp.eye(G, dtype=F32)
    p['poolw'] = jnp.einsum('lgcd,gh->lgchd', pool_w, eye).reshape(L, POOL_WIDTH, POOL_WIDTH).astype(BF16)
    p['pscale'] = row(pool_scale)
    p['wup'] = jnp.concatenate([w_up_a, w_up_b, w_up_c], axis=1).astype(BF16)
    p['wo'] = w_o.astype(BF16)
    p['ln1g'], p['ln1b'] = row(ln1_g), row(ln1_b)
    p['wq'] = xa_wq.astype(BF16)
    p['wkt'] = jnp.swapaxes(xa_wk, 1, 2).astype(BF16)
    p['wv'] = xa_wv.astype(BF16)
    p['wxo'] = xa_wo.astype(BF16)
    p['ln2g'], p['ln2b'] = row(ln2_g), row(ln2_b)
    rwt = jnp.swapaxes(router_w, 1, 2)
    rwt_hi = rwt.astype(BF16)
    p['rwt'] = jnp.concatenate([rwt_hi, (rwt - rwt_hi.astype(F32)).astype(BF16)], axis=1)
    p['rbt'] = jnp.broadcast_to(router_b[:, :, None], router_b.shape + (XA_TILE,)).astype(F32)
    p['ln3g'], p['ln3b'] = row(ln3_g), row(ln3_b)
    return p


def _route(route, counts):
    N = route.shape[1]
    top_idx = route[0:TOP_K].astype(jnp.int32)
    gate = route[TOP_K:2 * TOP_K].T
    rank = route[2 * TOP_K:3 * TOP_K].astype(jnp.int32)
    counts = counts[:, 0].astype(jnp.int32)
    R = MOE_STEP_BLOCKS * MOE_BLOCK
    blocks = (counts + MOE_BLOCK - 1) // MOE_BLOCK
    padded = ((counts + R - 1) // R) * R
    pad_end = jnp.cumsum(padded)
    pad_start = pad_end - padded
    ids = jnp.arange(N_EXPERTS, dtype=jnp.int32)
    start_of = jnp.sum(jnp.where(top_idx[:, :, None] == ids[None, None, :], pad_start[None, None, :], 0), axis=-1)
    dest = start_of + rank
    n_steps = N * TOP_K // R + N_EXPERTS
    step_start = jnp.arange(n_steps, dtype=jnp.int32) * R
    step_expert = jnp.minimum(jnp.sum((pad_end[None, :] <= step_start[:, None]).astype(jnp.int32), axis=1),
                              N_EXPERTS - 1)
    mine = step_expert[:, None] == ids[None, :]
    blocks_before = (step_start - jnp.sum(jnp.where(mine, pad_start[None, :], 0), axis=1)) // MOE_BLOCK
    step_halves = jnp.clip(jnp.sum(jnp.where(mine, blocks[None, :], 0), axis=1) - blocks_before, 0, MOE_STEP_BLOCKS)
    step_halves = jnp.where(step_start < pad_end[-1], step_halves, 0).astype(jnp.int32)
    n_used = (pad_end[-1] // R).astype(jnp.int32).reshape(1)
    later = jnp.where((ids[None, :] > ids[:, None]) & (counts[None, :] > 0), ids[None, :], N_EXPERTS)
    first_later = jnp.min(later, axis=1)
    next_expert = jnp.where(first_later < N_EXPERTS, first_later, ids).astype(jnp.int32)
    return gate, dest, step_expert, step_halves, n_used, next_expert


def kernel(x, mem, w_in, b_in, gla_wg2, gla_bg, gla_norm_g, sgu_ln_g, sgu_ln_b, sgu_ws, sgu_bs, pool_w, pool_scale, w_up_a, w_up_b, w_up_c, w_o, ln1_g, ln1_b, xa_wq, xa_wk, xa_wv, xa_wo, ln2_g, ln2_b, router_w, router_b, exp_w_gu, exp_b_gu, exp_w_down, exp_b_down, ln3_g, ln3_b):
    B, S, D = x.shape
    N = B * S
    dense = (w_in, b_in, gla_wg2, gla_bg, gla_norm_g, sgu_ln_g, sgu_ln_b, sgu_ws, sgu_bs, pool_w, pool_scale,
             w_up_a, w_up_b, w_up_c, w_o, ln1_g, ln1_b, xa_wq, xa_wk, xa_wv, xa_wo, ln2_g, ln2_b,
             router_w, router_b, ln3_g, ln3_b)
    memt = jnp.swapaxes(mem, 1, 2)
    p_next = _prep(*[a[0:1] for a in dense])
    for l in range(DEPTH):
        p = p_next
        x1 = _mixer(x, p, 0)
        kt, v = _memkv(mem, memt, p, 0)
        x2, x2p, route, counts = _xattn(x1, kt, v, p, 0)
        if l + 1 < DEPTH:
            tied, _ = lax.optimization_barrier(([a[l + 1:l + 2] for a in dense], route))
            p_next = _prep(*tied)
        gate, dest, step_expert, step_halves, n_used, next_expert = _route(route, counts)
        n_slots = N * TOP_K + N_EXPERTS * MOE_STEP_BLOCKS * MOE_BLOCK
        xs = _sc_scatter_rows(x2p.reshape(N, D // 2), dest, n_slots)
        ys = _experts(step_expert, step_halves, n_used, next_expert, xs, exp_w_gu, exp_b_gu, exp_w_down, exp_b_down, l)
        dest_km = dest.reshape(TOP_K, N // CMB_TILE, CMB_TILE).transpose(1, 0, 2).reshape(-1)
        rows = dest_km.shape[0] // COMBINE_PARTS
        acc = None
        for part in range(COMBINE_PARTS):
            yg = _sc_gather_rows(ys, dest_km[part * rows:(part + 1) * rows])
            acc = _combine(x2.reshape(N, D), yg, gate, p, 0, part, acc)
        x = acc.reshape(B, S, D)
    return x
```

```python
import functools

import jax
import jax.numpy as jnp
from jax import lax
from jax.experimental import pallas as pl
from jax.experimental.pallas import tpu as pltpu
from jax.experimental.pallas import tpu_sc as plsc

F32 = jnp.float32
BF16 = jnp.bfloat16

D_MODEL = 1024
DEPTH = 2
GLA_HEADS = 4
GLA_KEY = 256
GLA_VAL = 512
GLA_DK = 64
GLA_DV = 128
GLA_RANK = 16
GLA_TAU = 16.0
GLA_CHUNK = 64
SGU_GROUPS = 4
SGU_WIDTH = 256
SGU_GD = 64
SGU_CHUNK = 128
POOL_WINDOWS = (2, 4, 8, 16)
POOL_WIDTH = 256
POOL_GD = 64
POOL_CARRY = 32
MEM_LEN = 256
XA_HEADS = 4
XA_DH = 256
N_EXPERTS = 32
TOP_K = 4
EXPERT_FF = 1024
SWIGLU_LIMIT = 7.0
SWIGLU_ALPHA = 1.702
DEEPNORM_ALPHA = (2 * DEPTH) ** 0.25
LN_EPS = 1e-5
LANES = 128
VMEM_LIMIT = 56 * 1024 * 1024

MIX_TILE = 512
GLA_SUB = 256
XA_TILE = 1024
MOE_BLOCK = 256
MOE_STEP_BLOCKS = 4
CMB_TILE = 512
COMBINE_PARTS = 4
ROUTE_ROWS = 16
SC_GATHER_WINDOW = 128

O_QKVR = 0
O_GLOW = 2 * GLA_KEY + 2 * GLA_VAL
O_UV = O_GLOW + LANES
O_XC = O_UV + 2 * SGU_WIDTH
O_GATE = O_XC + POOL_WIDTH
N_PACK = O_GATE + 3 * D_MODEL


def _dot(a, b):
    return jnp.dot(a, b, preferred_element_type=F32)


def _dot_t0(a, b):
    return lax.dot_general(a, b, (((0,), (0,)), ((), ())), preferred_element_type=F32)


def _dot_t1(a, b):
    return lax.dot_general(a, b, (((1,), (1,)), ((), ())), preferred_element_type=F32)


def _split_bf16(x):
    hi = x.astype(BF16)
    lo = (x - hi.astype(F32)).astype(BF16)
    return hi, lo


def _layer_norm(x, g, b):
    mu = jnp.mean(x, axis=-1, keepdims=True)
    xc = x - mu
    var = jnp.mean(xc * xc, axis=-1, keepdims=True)
    return xc * lax.rsqrt(var + LN_EPS) * g + b


def _sigmoid(x):
    return 1.0 / (1.0 + jnp.exp(-x))


def _pack_bf16_pairs(x):
    H = x.shape[1] // 2
    bits = lax.bitcast_convert_type(x.astype(BF16).astype(F32), jnp.uint32)
    return (bits[:, :H] >> 16) | (bits[:, H:] & jnp.uint32(0xFFFF0000))


def _unpack_bf16_pairs(w):
    lo = lax.bitcast_convert_type(w << 16, F32)
    hi = lax.bitcast_convert_type(w & jnp.uint32(0xFFFF0000), F32)
    return jnp.concatenate([lo, hi], axis=1)


def _layer_spec(arr, l):
    nd = arr.ndim - 1
    return pl.BlockSpec((None,) + arr.shape[1:], lambda *_: (l,) + (0,) * nd, pipeline_mode=pl.Buffered(1))


def _mixer_kernel(x_ref, wcat_ref, bcat_ref, wg2_ref, bg_ref, gnorm_ref,
                  slng_ref, slnb_ref, wtril_ref, sbias_ref, poolw_ref, pscale_ref,
                  wup_ref, wo_ref, ln1g_ref, ln1b_ref,
                  out_ref,
                  state_ref, qkvr_ref, ya_ref, vln_ref, e_ref, s2_ref, s4_ref, s8_ref):
    T = MIX_TILE
    D = D_MODEL
    j = pl.program_id(1)
    x = x_ref[0]
    xb = x.astype(BF16)

    def proj(lo, hi):
        return _dot(xb, wcat_ref[:, lo:hi]) + bcat_ref[:, lo:hi]

    @pl.when(j == 0)
    def _():
        state_ref[...] = jnp.zeros_like(state_ref)
        e_ref[0:POOL_CARRY, :] = jnp.zeros((POOL_CARRY, POOL_WIDTH), F32)

    @pl.when(j > 0)
    def _():
        e_ref[0:POOL_CARRY, :] = e_ref[T:T + POOL_CARRY, :]

    qkvr_ref[...] = proj(O_QKVR, O_GLOW)
    glow = proj(O_GLOW, O_UV)
    z = _dot(glow.astype(BF16), wg2_ref[...]) + bg_ref[...]
    la = (jnp.minimum(z, 0.0) - jnp.log1p(jnp.exp(-jnp.abs(z)))) * (1.0 / GLA_TAU)
    la_hi, la_lo = _split_bf16(la)
    gate_a = _sigmoid(proj(O_GATE, O_GATE + D))

    C = GLA_CHUNK
    G = GLA_SUB
    NC = G // C
    CSH = C.bit_length() - 1
    row = lax.broadcasted_iota(jnp.int32, (G, G), 0)
    col = lax.broadcasted_iota(jnp.int32, (G, G), 1)
    same_chunk = (row >> CSH) == (col >> CSH)
    causal = same_chunk & (row >= col)
    causal_bf = jnp.where(causal, 1.0, 0.0).astype(BF16)
    lane = lax.broadcasted_iota(jnp.int32, (G, GLA_KEY), 1)
    gate_b = None
    for g0 in range(0, T, G):
        gr = slice(g0, g0 + G)
        lh, ll = la_hi[gr], la_lo[gr]
        b = _dot(causal_bf, lh) + _dot(causal_bf, ll)
        b_last = [b[(c + 1) * C - 1:(c + 1) * C, :] for c in range(NC)]
        b_end = jnp.concatenate([jnp.broadcast_to(r, (C, GLA_KEY)) for r in b_last], axis=0)
        q = qkvr_ref[gr, 0:GLA_KEY]
        k = qkvr_ref[gr, GLA_KEY:2 * GLA_KEY]
        v = qkvr_ref[gr, 2 * GLA_KEY:2 * GLA_KEY + GLA_VAL].astype(BF16)
        q_dec = q * (GLA_DK ** -0.5) * jnp.exp(b)
        k_dec = (k * jnp.exp(-b)).astype(BF16)
        k_tail = (k * jnp.exp(b_end - b)).astype(BF16)
        q_dec_bf = q_dec.astype(BF16)
        o_heads = []
        for h in range(GLA_HEADS):
            q_h = jnp.where((lane >= h * GLA_DK) & (lane < (h + 1) * GLA_DK), q_dec, 0.0).astype(BF16)
            scores = jnp.where(causal, _dot_t1(q_h, k_dec), 0.0).astype(BF16)
            o_heads.append(_dot(scores, v[:, h * GLA_DV:(h + 1) * GLA_DV]))
        o_intra = jnp.concatenate(o_heads, axis=1)
        if gate_b is None:
            gate_b = _sigmoid(proj(O_GATE + D, O_GATE + 2 * D))
        dec_cols = jnp.exp(jnp.concatenate(b_last + [jnp.zeros((8 - NC, GLA_KEY), F32)], axis=0)).T
        dec_all = jnp.concatenate([jnp.broadcast_to(dec_cols[:, c:c + 1], (GLA_KEY, LANES)) for c in range(NC)], axis=1)
        o_inter = []
        for c in range(NC):
            rows = slice(c * C, (c + 1) * C)
            o_inter.append(_dot(q_dec_bf[rows], state_ref[...].astype(BF16)))
            kv = _dot_t0(k_tail[rows], v[rows])
            for h in range(GLA_HEADS):
                rs = slice(h * GLA_DK, (h + 1) * GLA_DK)
                cs = slice(h * GLA_DV, (h + 1) * GLA_DV)
                state_ref[rs, cs] = dec_all[rs, c * LANES:(c + 1) * LANES] * state_ref[rs, cs] + kv[rs, cs]
        o = o_intra + jnp.concatenate(o_inter, axis=0)
        for h in range(GLA_HEADS):
            cs = slice(h * GLA_DV, (h + 1) * GLA_DV)
            o_h = o[:, cs]
            ms = jnp.mean(o_h * o_h, axis=-1, keepdims=True)
            o_h = o_h * lax.rsqrt(ms + LN_EPS) * gnorm_ref[:, cs]
            r_h = qkvr_ref[gr, 2 * GLA_KEY + GLA_VAL + h * GLA_DV:2 * GLA_KEY + GLA_VAL + (h + 1) * GLA_DV]
            ya_ref[gr, cs] = (o_h * (r_h * _sigmoid(r_h))).astype(BF16)

    gate_c = _sigmoid(proj(O_GATE + 2 * D, O_GATE + 3 * D))
    uv = proj(O_UV, O_XC)
    zg = 0.5 * uv * (1.0 + lax.erf(uv * (2.0 ** -0.5)))
    u = zg[:, :SGU_WIDTH]
    vln_ref[...] = _layer_norm(zg[:, SGU_WIDTH:], slng_ref[...], slnb_ref[...])
    lane_s = lax.broadcasted_iota(jnp.int32, (SGU_CHUNK, SGU_WIDTH), 1)
    s_parts = []
    for n in range(T // SGU_CHUNK):
        vc = vln_ref[n * SGU_CHUNK:(n + 1) * SGU_CHUNK, :]
        s = sbias_ref[...]
        for g in range(SGU_GROUPS):
            vg = jnp.where((lane_s >= g * SGU_GD) & (lane_s < (g + 1) * SGU_GD), vc, 0.0).astype(BF16)
            s = s + _dot(wtril_ref[g], vg)
        s_parts.append(s)
    y_b = (u * jnp.concatenate(s_parts, axis=0)).astype(BF16)

    P = POOL_CARRY
    xc = proj(O_XC, O_GATE)
    e_ref[P:P + T, :] = xc
    s2_ref[8:P + T, :] = e_ref[8:P + T, :] + e_ref[7:P + T - 1, :]
    s4_ref[16:P + T, :] = s2_ref[16:P + T, :] + s2_ref[14:P + T - 2, :]
    s8_ref[24:P + T, :] = s4_ref[24:P + T, :] + s4_ref[20:P + T - 4, :]
    s16 = s8_ref[P:P + T, :] + s8_ref[P - 8:P + T - 8, :]
    lane_p = lax.broadcasted_iota(jnp.int32, (T, POOL_WIDTH), 1)
    tpos = lax.broadcasted_iota(jnp.int32, (T, POOL_WIDTH), 0) + (j * T + 1)
    grp = lane_p >> 6
    win = jnp.where(grp == 0, POOL_WINDOWS[0], jnp.where(grp == 1, POOL_WINDOWS[1],
                    jnp.where(grp == 2, POOL_WINDOWS[2], POOL_WINDOWS[3])))
    wsum = jnp.where(grp == 0, s2_ref[P:P + T, :], jnp.where(grp == 1, s4_ref[P:P + T, :],
                     jnp.where(grp == 2, s8_ref[P:P + T, :], s16)))
    count = jnp.minimum(tpos, win).astype(F32)
    pooled = wsum / count - xc
    y_c = (_dot(pooled.astype(BF16), poolw_ref[...]) * pscale_ref[...]).astype(BF16)

    ra, rb = GLA_VAL, GLA_VAL + SGU_WIDTH
    merged = gate_a * _dot(ya_ref[...], wup_ref[0:ra, :])
    merged += gate_b * _dot(y_b, wup_ref[ra:rb, :])
    merged += gate_c * _dot(y_c, wup_ref[rb:, :])
    h = _dot(merged.astype(BF16), wo_ref[...])
    out_ref[0] = _layer_norm(DEEPNORM_ALPHA * x + h, ln1g_ref[...], ln1b_ref[...])


_MIXER_WEIGHTS = ('wcat', 'bcat', 'wg2', 'bg', 'gnorm', 'slng', 'slnb', 'wtril', 'sbias', 'poolw', 'pscale',
                  'wup', 'wo', 'ln1g', 'ln1b')


def _mixer(x, p, l):
    B, S, D = x.shape
    T = MIX_TILE
    weights = [p[n] for n in _MIXER_WEIGHTS]
    return pl.pallas_call(
        _mixer_kernel,
        out_shape=jax.ShapeDtypeStruct((B, S, D), F32),
        grid=(B, S // T),
        in_specs=[pl.BlockSpec((1, T, D), lambda b, j: (b, j, 0))] + [_layer_spec(w, l) for w in weights],
        out_specs=pl.BlockSpec((1, T, D), lambda b, j: (b, j, 0)),
        scratch_shapes=[
            pltpu.VMEM((GLA_KEY, GLA_VAL), F32),
            pltpu.VMEM((T, 2 * GLA_KEY + 2 * GLA_VAL), F32),
            pltpu.VMEM((T, GLA_VAL), BF16),
            pltpu.VMEM((T, SGU_WIDTH), F32),
            pltpu.VMEM((T + POOL_CARRY, POOL_WIDTH), F32),
            pltpu.VMEM((T + POOL_CARRY, POOL_WIDTH), F32),
            pltpu.VMEM((T + POOL_CARRY, POOL_WIDTH), F32),
            pltpu.VMEM((T + POOL_CARRY, POOL_WIDTH), F32),
        ],
        compiler_params=pltpu.CompilerParams(dimension_semantics=("arbitrary", "arbitrary"),
                                             vmem_limit_bytes=VMEM_LIMIT),
        name="mixer",
    )(x, *weights)


def _memkv_kernel(memt_ref, mem_ref, wkt_ref, wv_ref, kt_ref, v_ref):
    kt_ref[0] = _dot(wkt_ref[...], memt_ref[0].astype(BF16)).astype(BF16)
    v_ref[0] = _dot(mem_ref[0].astype(BF16), wv_ref[...]).astype(BF16)


def _memkv(mem, memt, p, l):
    B, M, D = mem.shape
    return pl.pallas_call(
        _memkv_kernel,
        out_shape=(jax.ShapeDtypeStruct((B, D, M), BF16), jax.ShapeDtypeStruct((B, M, D), BF16)),
        grid=(B,),
        in_specs=[pl.BlockSpec((1, D, M), lambda b: (b, 0, 0)), pl.BlockSpec((1, M, D), lambda b: (b, 0, 0)),
                  _layer_spec(p['wkt'], l), _layer_spec(p['wv'], l)],
        out_specs=(pl.BlockSpec((1, D, M), lambda b: (b, 0, 0)), pl.BlockSpec((1, M, D), lambda b: (b, 0, 0))),
        compiler_params=pltpu.CompilerParams(dimension_semantics=("arbitrary",), vmem_limit_bytes=VMEM_LIMIT),
        name="memkv",
    )(memt, mem, p['wkt'], p['wv'])


def _xattn_kernel(x_ref, kt_ref, v_ref, wq_ref, wo_ref, ln2g_ref, ln2b_ref, rwt_ref, rbt_ref,
                  x2_ref, x2p_ref, route_ref, counts_ref, carry_ref):
    T = XA_TILE

    @pl.when((pl.program_id(0) == 0) & (pl.program_id(1) == 0))
    def _():
        carry_ref[...] = jnp.zeros_like(carry_ref)

    x = x_ref[0]
    q = (_dot(x.astype(BF16), wq_ref[...]) * (XA_DH ** -0.5)).astype(BF16)
    h = jnp.zeros_like(x)
    for hd in range(XA_HEADS):
        cs = slice(hd * XA_DH, (hd + 1) * XA_DH)
        s = _dot(q[:, cs], kt_ref[0, cs, :])
        e = jnp.exp(s - jnp.max(s, axis=-1, keepdims=True))
        o = _dot(e.astype(BF16), v_ref[0, :, cs]) / jnp.sum(e, axis=-1, keepdims=True)
        h = h + _dot(o.astype(BF16), wo_ref[cs, :])
    x2 = _layer_norm(DEEPNORM_ALPHA * x + h, ln2g_ref[...], ln2b_ref[...])
    x2_ref[0] = x2
    x2p_ref[0] = _pack_bf16_pairs(x2)

    E = N_EXPERTS
    hi, lo = _split_bf16(x2)
    lt = _dot_t1(rwt_ref[...], hi)
    logits = lt[0:E] + (lt[E:2 * E] + _dot_t1(rwt_ref[0:E, :], lo)) + rbt_ref[...]

    eid = lax.broadcasted_iota(jnp.int32, (E, T), 0)
    neg_inf = jnp.float32(-jnp.inf)
    rest = logits
    tops, picks = [], []
    for _ in range(TOP_K):
        m = jnp.max(rest, axis=0, keepdims=True)
        idx = jnp.min(jnp.where(rest == m, eid, E), axis=0, keepdims=True)
        pick = eid == idx
        rest = jnp.where(pick, neg_inf, rest)
        tops.append((m, idx))
        picks.append(pick)
    exps = [jnp.exp(m - tops[0][0]) for m, _ in tops]
    denom = exps[0]
    for e in exps[1:]:
        denom = denom + e

    chosen = jnp.zeros((E, T), F32)
    for pick in picks:
        chosen = chosen + jnp.where(pick, 1.0, 0.0)
    chosen_bf = chosen.astype(BF16)
    earlier = (lax.broadcasted_iota(jnp.int32, (T, T), 0) < lax.broadcasted_iota(jnp.int32, (T, T), 1))
    carry = carry_ref[...]
    before = _dot(chosen_bf, jnp.where(earlier, 1.0, 0.0).astype(BF16)) + jnp.concatenate([carry] * (T // LANES), axis=1)
    carry = carry + _dot(chosen_bf, jnp.ones((T, LANES), BF16))
    carry_ref[...] = carry
    counts_ref[...] = carry

    rid = lax.broadcasted_iota(jnp.int32, (ROUTE_ROWS, T), 0)
    route = jnp.zeros((ROUTE_ROWS, T), F32)
    for k in range(TOP_K):
        rank = jnp.sum(jnp.where(picks[k], before, 0.0), axis=0, keepdims=True)
        route = jnp.where(rid == k, tops[k][1].astype(F32), route)
        route = jnp.where(rid == TOP_K + k, exps[k] / denom, route)
        route = jnp.where(rid == 2 * TOP_K + k, rank, route)
    route_ref[...] = route


def _xattn(x, kt, v, p, l):
    B, S, D = x.shape
    T = XA_TILE
    M = MEM_LEN
    weights = [p[n] for n in ('wq', 'wxo', 'ln2g', 'ln2b', 'rwt', 'rbt')]
    return pl.pallas_call(
        _xattn_kernel,
        out_shape=(jax.ShapeDtypeStruct((B, S, D), F32), jax.ShapeDtypeStruct((B, S, D // 2), jnp.uint32),
                   jax.ShapeDtypeStruct((ROUTE_ROWS, B * S), F32), jax.ShapeDtypeStruct((N_EXPERTS, LANES), F32)),
        grid=(B, S // T),
        in_specs=[pl.BlockSpec((1, T, D), lambda b, j: (b, j, 0)),
                  pl.BlockSpec((1, D, M), lambda b, j: (b, 0, 0)),
                  pl.BlockSpec((1, M, D), lambda b, j: (b, 0, 0))] + [_layer_spec(w, l) for w in weights],
        out_specs=(pl.BlockSpec((1, T, D), lambda b, j: (b, j, 0)),
                   pl.BlockSpec((1, T, D // 2), lambda b, j: (b, j, 0)),
                   pl.BlockSpec((ROUTE_ROWS, T), lambda b, j: (0, b * (S // T) + j)),
                   pl.BlockSpec((N_EXPERTS, LANES), lambda b, j: (0, 0))),
        scratch_shapes=[pltpu.VMEM((N_EXPERTS, LANES), F32)],
        compiler_params=pltpu.CompilerParams(dimension_semantics=("arbitrary", "arbitrary"),
                                             vmem_limit_bytes=VMEM_LIMIT),
        name="xattn",
    )(x, kt, v, *weights)


def _expert_kernel(layer, se_ref, sh_ref, nu_ref, nxt_ref, xs_ref, wgu_hbm, bgu_ref, wd_hbm, bd_ref, out_ref,
                   wgu_st, wd_st, wgu_bf, wd_bf, sem):
    i = pl.program_id(0)
    F = EXPERT_FF
    halves = sh_ref[i]

    def weight_copies(e):
        return (pltpu.make_async_copy(wgu_hbm.at[layer, e], wgu_st, sem.at[0]),
                pltpu.make_async_copy(wd_hbm.at[layer, e], wd_st, sem.at[1]))

    def ffn(rows):
        xsb = _unpack_bf16_pairs(xs_ref[rows, :]).astype(BF16)
        hh = _dot(xsb, wgu_bf[...]) + bgu_ref[...]
        h_glu = jnp.minimum(hh[:, :F], SWIGLU_LIMIT)
        h_lin = jnp.clip(hh[:, F:], -SWIGLU_LIMIT, SWIGLU_LIMIT)
        a = h_glu * _sigmoid(SWIGLU_ALPHA * h_glu) * (h_lin + 1.0)
        out_ref[rows, :] = _pack_bf16_pairs(_dot(a.astype(BF16), wd_bf[...]) + bd_ref[...])

    @pl.when(halves > 0)
    def _():
        e = se_ref[i]
        prev = se_ref[jnp.maximum(i - 1, 0)]

        @pl.when(i == 0)
        def _():
            for cp in weight_copies(e):
                cp.start()

        @pl.when((i == 0) | (e != prev))
        def _():
            for cp in weight_copies(e):
                cp.wait()
            wgu_bf[...] = wgu_st[...].astype(BF16)
            wd_bf[...] = wd_st[...].astype(BF16)
            nxt = nxt_ref[e]

            @pl.when(nxt != e)
            def _():
                for cp in weight_copies(nxt):
                    cp.start()

    for n in range(1, MOE_STEP_BLOCKS + 1):
        @pl.when(halves == n)
        def _(n=n):
            ffn(slice(0, n * MOE_BLOCK))


def _experts(step_expert, step_halves, n_used, next_expert, xs, w_gu, b_gu, w_down, b_down, l):
    P, DH = xs.shape
    D = 2 * DH
    R = MOE_STEP_BLOCKS * MOE_BLOCK
    NS = P // R
    F2 = 2 * EXPERT_FF

    def row_map(i, se, sh, nu, nxt):
        return (jnp.minimum(i, nu[0] - 1), 0)

    def exp_map(i, se, sh, nu, nxt):
        return (l, se[jnp.minimum(i, nu[0] - 1)], 0, 0)

    grid_spec = pltpu.PrefetchScalarGridSpec(
        num_scalar_prefetch=4,
        grid=(NS,),
        in_specs=[pl.BlockSpec((R, DH), row_map),
                  pl.BlockSpec(memory_space=pl.ANY),
                  pl.BlockSpec((None, None, 1, F2), exp_map),
                  pl.BlockSpec(memory_space=pl.ANY),
                  pl.BlockSpec((None, None, 1, D), exp_map)],
        out_specs=pl.BlockSpec((R, DH), row_map),
        scratch_shapes=[pltpu.VMEM((D, F2), F32), pltpu.VMEM((EXPERT_FF, D), F32),
                        pltpu.VMEM((D, F2), BF16), pltpu.VMEM((EXPERT_FF, D), BF16),
                        pltpu.SemaphoreType.DMA((2,))],
    )
    return pl.pallas_call(
        functools.partial(_expert_kernel, l),
        out_shape=jax.ShapeDtypeStruct((P, DH), jnp.uint32),
        grid_spec=grid_spec,
        compiler_params=pltpu.CompilerParams(dimension_semantics=("arbitrary",), vmem_limit_bytes=VMEM_LIMIT),
        name="experts",
    )(step_expert, step_halves, n_used, next_expert, xs, w_gu, b_gu.reshape(DEPTH, N_EXPERTS, 1, F2), w_down,
      b_down.reshape(DEPTH, N_EXPERTS, 1, D))


def _sc_gather_rows(x, idx):
    M = idx.shape[0]
    D = x.shape[1]
    W = SC_GATHER_WINDOW
    mesh = plsc.VectorSubcoreMesh(core_axis_name="core", subcore_axis_name="subcore")
    n_workers = mesh.num_cores * mesh.num_subcores
    rows_per = M // n_workers
    assert rows_per * n_workers == M and rows_per % W == 0

    @pl.kernel(out_type=jax.ShapeDtypeStruct((M, D), x.dtype), mesh=mesh, name="sc_gather_rows",
               scratch_types=[pltpu.VMEM((rows_per,), jnp.int32), pltpu.VMEM((W, D), x.dtype)])
    def gather_kernel(x_hbm, i_hbm, o_hbm, idx_vmem, buf):
        wid = lax.axis_index("core") * mesh.num_subcores + lax.axis_index("subcore")
        base = wid * rows_per
        pltpu.sync_copy(i_hbm.at[pl.ds(base, rows_per)], idx_vmem)

        @pl.loop(0, rows_per // W)
        def _(j):
            pltpu.sync_copy(x_hbm.at[idx_vmem.at[pl.ds(j * W, W)]], buf)
            pltpu.sync_copy(buf, o_hbm.at[pl.ds(base + j * W, W)])

    return gather_kernel(x, idx)


def _sc_scatter_rows(x, idx, n_out):
    K, N = idx.shape
    D = x.shape[1]
    W = SC_GATHER_WINDOW
    mesh = plsc.VectorSubcoreMesh(core_axis_name="core", subcore_axis_name="subcore")
    n_workers = mesh.num_cores * mesh.num_subcores
    rows_per = N // n_workers
    assert rows_per * n_workers == N and rows_per % W == 0

    @pl.kernel(out_type=jax.ShapeDtypeStruct((n_out, D), x.dtype), mesh=mesh, name="sc_scatter_rows",
               scratch_types=[pltpu.VMEM((K * rows_per,), jnp.int32), pltpu.VMEM((W, D), x.dtype),
                              pltpu.SemaphoreType.DMA((K,))])
    def scatter_kernel(x_hbm, i_hbm, o_hbm, idx_vmem, buf, sem):
        wid = lax.axis_index("core") * mesh.num_subcores + lax.axis_index("subcore")
        base = wid * rows_per
        for k in range(K):
            pltpu.sync_copy(i_hbm.at[pl.ds(k * N + base, rows_per)], idx_vmem.at[pl.ds(k * rows_per, rows_per)])

        @pl.loop(0, rows_per // W)
        def _(j):
            pltpu.sync_copy(x_hbm.at[pl.ds(base + j * W, W)], buf)
            copies = [pltpu.make_async_copy(buf, o_hbm.at[idx_vmem.at[pl.ds(k * rows_per + j * W, W)]], sem.at[k])
                      for k in range(K)]
            for cp in copies:
                cp.start()
            for cp in copies:
                cp.wait()

    return scatter_kernel(x, idx.reshape(K * N))


def _combine_kernel(x_ref, yg_ref, gate_ref, ln3g_ref, ln3b_ref, *rest):
    out_ref = rest[-1]
    x = x_ref[...]
    g = gate_ref[...]
    y = jnp.zeros_like(x)
    for k in range(TOP_K):
        y = y + g[:, k:k + 1] * _unpack_bf16_pairs(yg_ref[k * CMB_TILE:(k + 1) * CMB_TILE, :])
    out_ref[...] = _layer_norm(DEEPNORM_ALPHA * x + y, ln3g_ref[...], ln3b_ref[...])


def _combine(x2, yg, gate, p, l, part, acc):
    N, D = x2.shape
    T = CMB_TILE
    tiles = yg.shape[0] // (TOP_K * T)
    first = part * tiles
    in_specs = [pl.BlockSpec((T, D), lambda i: (first + i, 0)), pl.BlockSpec((TOP_K * T, D // 2), lambda i: (i, 0)),
                pl.BlockSpec((T, TOP_K), lambda i: (first + i, 0)), _layer_spec(p['ln3g'], l), _layer_spec(p['ln3b'], l)]
    args = [x2, yg, gate, p['ln3g'], p['ln3b']]
    aliases = {}
    if acc is not None:
        in_specs.append(pl.BlockSpec(memory_space=pl.ANY))
        args.append(acc)
        aliases = {len(args) - 1: 0}
    return pl.pallas_call(
        _combine_kernel,
        out_shape=jax.ShapeDtypeStruct((N, D), F32),
        grid=(tiles,),
        in_specs=in_specs,
        out_specs=pl.BlockSpec((T, D), lambda i: (first + i, 0)),
        input_output_aliases=aliases,
        compiler_params=pltpu.CompilerParams(dimension_semantics=("arbitrary",), vmem_limit_bytes=VMEM_LIMIT),
        name="combine",
    )(*args)


def _prep(w_in, b_in, gla_wg2, gla_bg, gla_norm_g, sgu_ln_g, sgu_ln_b, sgu_ws, sgu_bs, pool_w, pool_scale,
          w_up_a, w_up_b, w_up_c, w_o, ln1_g, ln1_b, xa_wq, xa_wk, xa_wv, xa_wo, ln2_g, ln2_b,
          router_w, router_b, ln3_g, ln3_b):
    L = w_in.shape[0]
    o_glow = O_GLOW
    o_uv = o_glow + GLA_RANK
    row = lambda a: a.reshape(L, 1, -1).astype(F32)
    pad_last = lambda a, n: jnp.pad(a, [(0, 0)] * (a.ndim - 1) + [(0, n - a.shape[-1])])
    p = {}
    w_bf = w_in.astype(BF16)
    p['wcat'] = jnp.concatenate([w_bf[..., :o_glow], pad_last(w_bf[..., o_glow:o_uv], LANES), w_bf[..., o_uv:]],
                                axis=-1)
    p['bcat'] = row(jnp.concatenate([b_in[..., :o_glow], pad_last(b_in[..., o_glow:o_uv], LANES), b_in[..., o_uv:]],
                                    axis=-1))
    p['wg2'] = jnp.pad(gla_wg2, ((0, 0), (0, LANES - GLA_RANK), (0, 0))).astype(BF16)
    p['bg'] = row(gla_bg)
    p['gnorm'] = row(gla_norm_g)
    p['slng'] = row(sgu_ln_g)
    p['slnb'] = row(sgu_ln_b)
    p['wtril'] = jnp.tril(sgu_ws).astype(BF16)
    p['sbias'] = jnp.repeat(jnp.swapaxes(sgu_bs, 1, 2), SGU_GD, axis=2).astype(F32)
    G = len(POOL_WINDOWS)
    eye = jnp.eye(G, dtype=F32)
    p['poolw'] = jnp.einsum('lgcd,gh->lgchd', pool_w, eye).reshape(L, POOL_WIDTH, POOL_WIDTH).astype(BF16)
    p['pscale'] = row(pool_scale)
    p['wup'] = jnp.concatenate([w_up_a, w_up_b, w_up_c], axis=1).astype(BF16)
    p['wo'] = w_o.astype(BF16)
    p['ln1g'], p['ln1b'] = row(ln1_g), row(ln1_b)
    p['wq'] = xa_wq.astype(BF16)
    p['wkt'] = jnp.swapaxes(xa_wk, 1, 2).astype(BF16)
    p['wv'] = xa_wv.astype(BF16)
    p['wxo'] = xa_wo.astype(BF16)
    p['ln2g'], p['ln2b'] = row(ln2_g), row(ln2_b)
    rwt = jnp.swapaxes(router_w, 1, 2)
    rwt_hi = rwt.astype(BF16)
    p['rwt'] = jnp.concatenate([rwt_hi, (rwt - rwt_hi.astype(F32)).astype(BF16)], axis=1)
    p['rbt'] = jnp.broadcast_to(router_b[:, :, None], router_b.shape + (XA_TILE,)).astype(F32)
    p['ln3g'], p['ln3b'] = row(ln3_g), row(ln3_b)
    return p


def _route(route, counts):
    N = route.shape[1]
    top_idx = route[0:TOP_K].astype(jnp.int32)
    gate = route[TOP_K:2 * TOP_K].T
    rank = route[2 * TOP_K:3 * TOP_K].astype(jnp.int32)
    counts = counts[:, 0].astype(jnp.int32)
    R = MOE_STEP_BLOCKS * MOE_BLOCK
    blocks = (counts + MOE_BLOCK - 1) // MOE_BLOCK
    padded = ((counts + R - 1) // R) * R
    pad_end = jnp.cumsum(padded)
    pad_start = pad_end - padded
    ids = jnp.arange(N_EXPERTS, dtype=jnp.int32)
    start_of = jnp.sum(jnp.where(top_idx[:, :, None] == ids[None, None, :], pad_start[None, None, :], 0), axis=-1)
    dest = start_of + rank
    n_steps = N * TOP_K // R + N_EXPERTS
    step_start = jnp.arange(n_steps, dtype=jnp.int32) * R
    step_expert = jnp.minimum(jnp.sum((pad_end[None, :] <= step_start[:, None]).astype(jnp.int32), axis=1),
                              N_EXPERTS - 1)
    mine = step_expert[:, None] == ids[None, :]
    blocks_before = (step_start - jnp.sum(jnp.where(mine, pad_start[None, :], 0), axis=1)) // MOE_BLOCK
    step_halves = jnp.clip(jnp.sum(jnp.where(mine, blocks[None, :], 0), axis=1) - blocks_before, 0, MOE_STEP_BLOCKS)
    step_halves = jnp.where(step_start < pad_end[-1], step_halves, 0).astype(jnp.int32)
    n_used = (pad_end[-1] // R).astype(jnp.int32).reshape(1)
    later = jnp.where((ids[None, :] > ids[:, None]) & (counts[None, :] > 0), ids[None, :], N_EXPERTS)
    first_later = jnp.min(later, axis=1)
    next_expert = jnp.where(first_later < N_EXPERTS, first_later, ids).astype(jnp.int32)
    return gate, dest, step_expert, step_halves, n_used, next_expert


def kernel(x, mem, w_in, b_in, gla_wg2, gla_bg, gla_norm_g, sgu_ln_g, sgu_ln_b, sgu_ws, sgu_bs, pool_w, pool_scale, w_up_a, w_up_b, w_up_c, w_o, ln1_g, ln1_b, xa_wq, xa_wk, xa_wv, xa_wo, ln2_g, ln2_b, router_w, router_b, exp_w_gu, exp_b_gu, exp_w_down, exp_b_down, ln3_g, ln3_b):
    B, S, D = x.shape
    N = B * S
    p = _prep(w_in, b_in, gla_wg2, gla_bg, gla_norm_g, sgu_ln_g, sgu_ln_b, sgu_ws, sgu_bs, pool_w, pool_scale,
              w_up_a, w_up_b, w_up_c, w_o, ln1_g, ln1_b, xa_wq, xa_wk, xa_wv, xa_wo, ln2_g, ln2_b,
              router_w, router_b, ln3_g, ln3_b)
    memt = jnp.swapaxes(mem, 1, 2)
    for l in range(DEPTH):
        x1 = _mixer(x, p, l)
        kt, v = _memkv(mem, memt, p, l)
        x2, x2p, route, counts = _xattn(x1, kt, v, p, l)
        gate, dest, step_expert, step_halves, n_used, next_expert = _route(route, counts)
        n_slots = N * TOP_K + N_EXPERTS * MOE_STEP_BLOCKS * MOE_BLOCK
        xs = _sc_scatter_rows(x2p.reshape(N, D // 2), dest, n_slots)
        ys = _experts(step_expert, step_halves, n_used, next_expert, xs, exp_w_gu, exp_b_gu, exp_w_down, exp_b_down, l)
        dest_km = dest.reshape(TOP_K, N // CMB_TILE, CMB_TILE).transpose(1, 0, 2).reshape(-1)
        rows = dest_km.shape[0] // COMBINE_PARTS
        acc = None
        for part in range(COMBINE_PARTS):
            yg = _sc_gather_rows(ys, dest_km[part * rows:(part + 1) * rows])
            acc = _combine(x2.reshape(N, D), yg, gate, p, l, part, acc)
        x = acc.reshape(B, S, D)
    return x
```

```python
import functools

import jax
import jax.numpy as jnp
from jax import lax
from jax.experimental import pallas as pl
from jax.experimental.pallas import tpu as pltpu
from jax.experimental.pallas import tpu_sc as plsc

F32 = jnp.float32
BF16 = jnp.bfloat16

D_MODEL = 1024
DEPTH = 2
GLA_HEADS = 4
GLA_KEY = 256
GLA_VAL = 512
GLA_DK = 64
GLA_DV = 128
GLA_RANK = 16
GLA_TAU = 16.0
GLA_CHUNK = 64
SGU_GROUPS = 4
SGU_WIDTH = 256
SGU_GD = 64
SGU_CHUNK = 128
POOL_WINDOWS = (2, 4, 8, 16)
POOL_WIDTH = 256
POOL_GD = 64
POOL_CARRY = 32
MEM_LEN = 256
XA_HEADS = 4
XA_DH = 256
N_EXPERTS = 32
TOP_K = 4
EXPERT_FF = 1024
SWIGLU_LIMIT = 7.0
SWIGLU_ALPHA = 1.702
DEEPNORM_ALPHA = (2 * DEPTH) ** 0.25
LN_EPS = 1e-5
LANES = 128
VMEM_LIMIT = 56 * 1024 * 1024

MIX_TILE = 512
GLA_SUB = 256
XA_TILE = 1024
MOE_BLOCK = 256
MOE_STEP_BLOCKS = 4
CMB_TILE = 512
COMBINE_PARTS = 4
ROUTE_ROWS = 16
SC_GATHER_WINDOW = 128

O_QKVR = 0
O_GLOW = 2 * GLA_KEY + 2 * GLA_VAL
O_UV = O_GLOW + LANES
O_XC = O_UV + 2 * SGU_WIDTH
O_GATE = O_XC + POOL_WIDTH
N_PACK = O_GATE + 3 * D_MODEL


def _dot(a, b):
    return jnp.dot(a, b, preferred_element_type=F32)


def _dot_t0(a, b):
    return lax.dot_general(a, b, (((0,), (0,)), ((), ())), preferred_element_type=F32)


def _dot_t1(a, b):
    return lax.dot_general(a, b, (((1,), (1,)), ((), ())), preferred_element_type=F32)


def _split_bf16(x):
    hi = x.astype(BF16)
    lo = (x - hi.astype(F32)).astype(BF16)
    return hi, lo


def _layer_norm(x, g, b):
    mu = jnp.mean(x, axis=-1, keepdims=True)
    xc = x - mu
    var = jnp.mean(xc * xc, axis=-1, keepdims=True)
    return xc * lax.rsqrt(var + LN_EPS) * g + b


def _sigmoid(x):
    return 1.0 / (1.0 + jnp.exp(-x))


def _pack_bf16_pairs(x):
    H = x.shape[1] // 2
    bits = lax.bitcast_convert_type(x.astype(BF16).astype(F32), jnp.uint32)
    return (bits[:, :H] >> 16) | (bits[:, H:] & jnp.uint32(0xFFFF0000))


def _unpack_bf16_pairs(w):
    lo = lax.bitcast_convert_type(w << 16, F32)
    hi = lax.bitcast_convert_type(w & jnp.uint32(0xFFFF0000), F32)
    return jnp.concatenate([lo, hi], axis=1)


def _layer_spec(arr, l):
    nd = arr.ndim - 1
    return pl.BlockSpec((None,) + arr.shape[1:], lambda *_: (l,) + (0,) * nd, pipeline_mode=pl.Buffered(1))


def _mixer_kernel(x_ref, wcat_ref, bcat_ref, wg2_ref, bg_ref, gnorm_ref,
                  slng_ref, slnb_ref, wtril_ref, sbias_ref, poolw_ref, pscale_ref,
                  wup_ref, wo_ref, ln1g_ref, ln1b_ref,
                  out_ref,
                  state_ref, qkvr_ref, ya_ref, vln_ref, e_ref, s2_ref, s4_ref, s8_ref):
    T = MIX_TILE
    D = D_MODEL
    j = pl.program_id(1)
    x = x_ref[0]
    xb = x.astype(BF16)

    def proj(lo, hi):
        return _dot(xb, wcat_ref[:, lo:hi]) + bcat_ref[:, lo:hi]

    @pl.when(j == 0)
    def _():
        state_ref[...] = jnp.zeros_like(state_ref)
        e_ref[0:POOL_CARRY, :] = jnp.zeros((POOL_CARRY, POOL_WIDTH), F32)

    @pl.when(j > 0)
    def _():
        e_ref[0:POOL_CARRY, :] = e_ref[T:T + POOL_CARRY, :]

    qkvr_ref[...] = proj(O_QKVR, O_GLOW)
    glow = proj(O_GLOW, O_UV)
    z = _dot(glow.astype(BF16), wg2_ref[...]) + bg_ref[...]
    la = (jnp.minimum(z, 0.0) - jnp.log1p(jnp.exp(-jnp.abs(z)))) * (1.0 / GLA_TAU)
    la_hi, la_lo = _split_bf16(la)
    gate_a = _sigmoid(proj(O_GATE, O_GATE + D))

    C = GLA_CHUNK
    G = GLA_SUB
    NC = G // C
    CSH = C.bit_length() - 1
    row = lax.broadcasted_iota(jnp.int32, (G, G), 0)
    col = lax.broadcasted_iota(jnp.int32, (G, G), 1)
    same_chunk = (row >> CSH) == (col >> CSH)
    causal = same_chunk & (row >= col)
    causal_bf = jnp.where(causal, 1.0, 0.0).astype(BF16)
    lane = lax.broadcasted_iota(jnp.int32, (G, GLA_KEY), 1)
    gate_b = None
    for g0 in range(0, T, G):
        gr = slice(g0, g0 + G)
        lh, ll = la_hi[gr], la_lo[gr]
        b = _dot(causal_bf, lh) + _dot(causal_bf, ll)
        b_last = [b[(c + 1) * C - 1:(c + 1) * C, :] for c in range(NC)]
        b_end = jnp.concatenate([jnp.broadcast_to(r, (C, GLA_KEY)) for r in b_last], axis=0)
        q = qkvr_ref[gr, 0:GLA_KEY]
        k = qkvr_ref[gr, GLA_KEY:2 * GLA_KEY]
        v = qkvr_ref[gr, 2 * GLA_KEY:2 * GLA_KEY + GLA_VAL].astype(BF16)
        q_dec = q * (GLA_DK ** -0.5) * jnp.exp(b)
        k_dec = (k * jnp.exp(-b)).astype(BF16)
        k_tail = (k * jnp.exp(b_end - b)).astype(BF16)
        q_dec_bf = q_dec.astype(BF16)
        o_heads = []
        for h in range(GLA_HEADS):
            q_h = jnp.where((lane >= h * GLA_DK) & (lane < (h + 1) * GLA_DK), q_dec, 0.0).astype(BF16)
            scores = jnp.where(causal, _dot_t1(q_h, k_dec), 0.0).astype(BF16)
            o_heads.append(_dot(scores, v[:, h * GLA_DV:(h + 1) * GLA_DV]))
        o_intra = jnp.concatenate(o_heads, axis=1)
        if gate_b is None:
            gate_b = _sigmoid(proj(O_GATE + D, O_GATE + 2 * D))
        dec_cols = jnp.exp(jnp.concatenate(b_last + [jnp.zeros((8 - NC, GLA_KEY), F32)], axis=0)).T
        dec_all = jnp.concatenate([jnp.broadcast_to(dec_cols[:, c:c + 1], (GLA_KEY, LANES)) for c in range(NC)], axis=1)
        o_inter = []
        for c in range(NC):
            rows = slice(c * C, (c + 1) * C)
            o_inter.append(_dot(q_dec_bf[rows], state_ref[...].astype(BF16)))
            kv = _dot_t0(k_tail[rows], v[rows])
            for h in range(GLA_HEADS):
                rs = slice(h * GLA_DK, (h + 1) * GLA_DK)
                cs = slice(h * GLA_DV, (h + 1) * GLA_DV)
                state_ref[rs, cs] = dec_all[rs, c * LANES:(c + 1) * LANES] * state_ref[rs, cs] + kv[rs, cs]
        o = o_intra + jnp.concatenate(o_inter, axis=0)
        for h in range(GLA_HEADS):
            cs = slice(h * GLA_DV, (h + 1) * GLA_DV)
            o_h = o[:, cs]
            ms = jnp.mean(o_h * o_h, axis=-1, keepdims=True)
            o_h = o_h * lax.rsqrt(ms + LN_EPS) * gnorm_ref[:, cs]
            r_h = qkvr_ref[gr, 2 * GLA_KEY + GLA_VAL + h * GLA_DV:2 * GLA_KEY + GLA_VAL + (h + 1) * GLA_DV]
            ya_ref[gr, cs] = (o_h * (r_h * _sigmoid(r_h))).astype(BF16)

    gate_c = _sigmoid(proj(O_GATE + 2 * D, O_GATE + 3 * D))
    uv = proj(O_UV, O_XC)
    zg = 0.5 * uv * (1.0 + lax.erf(uv * (2.0 ** -0.5)))
    u = zg[:, :SGU_WIDTH]
    vln_ref[...] = _layer_norm(zg[:, SGU_WIDTH:], slng_ref[...], slnb_ref[...])
    lane_s = lax.broadcasted_iota(jnp.int32, (SGU_CHUNK, SGU_WIDTH), 1)
    s_parts = []
    for n in range(T // SGU_CHUNK):
        vc = vln_ref[n * SGU_CHUNK:(n + 1) * SGU_CHUNK, :]
        s = sbias_ref[...]
        for g in range(SGU_GROUPS):
            vg = jnp.where((lane_s >= g * SGU_GD) & (lane_s < (g + 1) * SGU_GD), vc, 0.0).astype(BF16)
            s = s + _dot(wtril_ref[g], vg)
        s_parts.append(s)
    y_b = (u * jnp.concatenate(s_parts, axis=0)).astype(BF16)

    P = POOL_CARRY
    xc = proj(O_XC, O_GATE)
    e_ref[P:P + T, :] = xc
    s2_ref[8:P + T, :] = e_ref[8:P + T, :] + e_ref[7:P + T - 1, :]
    s4_ref[16:P + T, :] = s2_ref[16:P + T, :] + s2_ref[14:P + T - 2, :]
    s8_ref[24:P + T, :] = s4_ref[24:P + T, :] + s4_ref[20:P + T - 4, :]
    s16 = s8_ref[P:P + T, :] + s8_ref[P - 8:P + T - 8, :]
    lane_p = lax.broadcasted_iota(jnp.int32, (T, POOL_WIDTH), 1)
    tpos = lax.broadcasted_iota(jnp.int32, (T, POOL_WIDTH), 0) + (j * T + 1)
    grp = lane_p >> 6
    win = jnp.where(grp == 0, POOL_WINDOWS[0], jnp.where(grp == 1, POOL_WINDOWS[1],
                    jnp.where(grp == 2, POOL_WINDOWS[2], POOL_WINDOWS[3])))
    wsum = jnp.where(grp == 0, s2_ref[P:P + T, :], jnp.where(grp == 1, s4_ref[P:P + T, :],
                     jnp.where(grp == 2, s8_ref[P:P + T, :], s16)))
    count = jnp.minimum(tpos, win).astype(F32)
    pooled = wsum / count - xc
    y_c = (_dot(pooled.astype(BF16), poolw_ref[...]) * pscale_ref[...]).astype(BF16)

    ra, rb = GLA_VAL, GLA_VAL + SGU_WIDTH
    merged = gate_a * _dot(ya_ref[...], wup_ref[0:ra, :])
    merged += gate_b * _dot(y_b, wup_ref[ra:rb, :])
    merged += gate_c * _dot(y_c, wup_ref[rb:, :])
    h = _dot(merged.astype(BF16), wo_ref[...])
    out_ref[0] = _layer_norm(DEEPNORM_ALPHA * x + h, ln1g_ref[...], ln1b_ref[...])


_MIXER_WEIGHTS = ('wcat', 'bcat', 'wg2', 'bg', 'gnorm', 'slng', 'slnb', 'wtril', 'sbias', 'poolw', 'pscale',
                  'wup', 'wo', 'ln1g', 'ln1b')


def _mixer(x, p, l):
    B, S, D = x.shape
    T = MIX_TILE
    weights = [p[n] for n in _MIXER_WEIGHTS]
    return pl.pallas_call(
        _mixer_kernel,
        out_shape=jax.ShapeDtypeStruct((B, S, D), F32),
        grid=(B, S // T),
        in_specs=[pl.BlockSpec((1, T, D), lambda b, j: (b, j, 0))] + [_layer_spec(w, l) for w in weights],
        out_specs=pl.BlockSpec((1, T, D), lambda b, j: (b, j, 0)),
        scratch_shapes=[
            pltpu.VMEM((GLA_KEY, GLA_VAL), F32),
            pltpu.VMEM((T, 2 * GLA_KEY + 2 * GLA_VAL), F32),
            pltpu.VMEM((T, GLA_VAL), BF16),
            pltpu.VMEM((T, SGU_WIDTH), F32),
            pltpu.VMEM((T + POOL_CARRY, POOL_WIDTH), F32),
            pltpu.VMEM((T + POOL_CARRY, POOL_WIDTH), F32),
            pltpu.VMEM((T + POOL_CARRY, POOL_WIDTH), F32),
            pltpu.VMEM((T + POOL_CARRY, POOL_WIDTH), F32),
        ],
        compiler_params=pltpu.CompilerParams(dimension_semantics=("arbitrary", "arbitrary"),
                                             vmem_limit_bytes=VMEM_LIMIT),
        name="mixer",
    )(x, *weights)


def _memkv_kernel(memt_ref, mem_ref, wkt_ref, wv_ref, kt_ref, v_ref):
    kt_ref[0] = _dot(wkt_ref[...], memt_ref[0].astype(BF16)).astype(BF16)
    v_ref[0] = _dot(mem_ref[0].astype(BF16), wv_ref[...]).astype(BF16)


def _memkv(mem, memt, p, l):
    B, M, D = mem.shape
    return pl.pallas_call(
        _memkv_kernel,
        out_shape=(jax.ShapeDtypeStruct((B, D, M), BF16), jax.ShapeDtypeStruct((B, M, D), BF16)),
        grid=(B,),
        in_specs=[pl.BlockSpec((1, D, M), lambda b: (b, 0, 0)), pl.BlockSpec((1, M, D), lambda b: (b, 0, 0)),
                  _layer_spec(p['wkt'], l), _layer_spec(p['wv'], l)],
        out_specs=(pl.BlockSpec((1, D, M), lambda b: (b, 0, 0)), pl.BlockSpec((1, M, D), lambda b: (b, 0, 0))),
        compiler_params=pltpu.CompilerParams(dimension_semantics=("arbitrary",), vmem_limit_bytes=VMEM_LIMIT),
        name="memkv",
    )(memt, mem, p['wkt'], p['wv'])


def _xattn_kernel(x_ref, kt_ref, v_ref, wq_ref, wo_ref, ln2g_ref, ln2b_ref, rwt_ref, rbt_ref,
                  x2_ref, x2p_ref, route_ref, counts_ref, carry_ref):
    T = XA_TILE

    @pl.when((pl.program_id(0) == 0) & (pl.program_id(1) == 0))
    def _():
        carry_ref[...] = jnp.zeros_like(carry_ref)

    x = x_ref[0]
    q = (_dot(x.astype(BF16), wq_ref[...]) * (XA_DH ** -0.5)).astype(BF16)
    h = jnp.zeros_like(x)
    for hd in range(XA_HEADS):
        cs = slice(hd * XA_DH, (hd + 1) * XA_DH)
        s = _dot(q[:, cs], kt_ref[0, cs, :])
        e = jnp.exp(s - jnp.max(s, axis=-1, keepdims=True))
        o = _dot(e.astype(BF16), v_ref[0, :, cs]) / jnp.sum(e, axis=-1, keepdims=True)
        h = h + _dot(o.astype(BF16), wo_ref[cs, :])
    x2 = _layer_norm(DEEPNORM_ALPHA * x + h, ln2g_ref[...], ln2b_ref[...])
    x2_ref[0] = x2
    x2p_ref[0] = _pack_bf16_pairs(x2)

    E = N_EXPERTS
    hi, lo = _split_bf16(x2)
    lt = _dot_t1(rwt_ref[...], hi)
    logits = lt[0:E] + (lt[E:2 * E] + _dot_t1(rwt_ref[0:E, :], lo)) + rbt_ref[...]

    eid = lax.broadcasted_iota(jnp.int32, (E, T), 0)
    neg_inf = jnp.float32(-jnp.inf)
    rest = logits
    tops, picks = [], []
    for _ in range(TOP_K):
        m = jnp.max(rest, axis=0, keepdims=True)
        idx = jnp.min(jnp.where(rest == m, eid, E), axis=0, keepdims=True)
        pick = eid == idx
        rest = jnp.where(pick, neg_inf, rest)
        tops.append((m, idx))
        picks.append(pick)
    exps = [jnp.exp(m - tops[0][0]) for m, _ in tops]
    denom = exps[0]
    for e in exps[1:]:
        denom = denom + e

    chosen = jnp.zeros((E, T), F32)
    for pick in picks:
        chosen = chosen + jnp.where(pick, 1.0, 0.0)
    chosen_bf = chosen.astype(BF16)
    earlier = (lax.broadcasted_iota(jnp.int32, (T, T), 0) < lax.broadcasted_iota(jnp.int32, (T, T), 1))
    carry = carry_ref[...]
    before = _dot(chosen_bf, jnp.where(earlier, 1.0, 0.0).astype(BF16)) + jnp.concatenate([carry] * (T // LANES), axis=1)
    carry = carry + _dot(chosen_bf, jnp.ones((T, LANES), BF16))
    carry_ref[...] = carry
    counts_ref[...] = carry

    rid = lax.broadcasted_iota(jnp.int32, (ROUTE_ROWS, T), 0)
    route = jnp.zeros((ROUTE_ROWS, T), F32)
    for k in range(TOP_K):
        rank = jnp.sum(jnp.where(picks[k], before, 0.0), axis=0, keepdims=True)
        route = jnp.where(rid == k, tops[k][1].astype(F32), route)
        route = jnp.where(rid == TOP_K + k, exps[k] / denom, route)
        route = jnp.where(rid == 2 * TOP_K + k, rank, route)
    route_ref[...] = route


def _xattn(x, kt, v, p, l):
    B, S, D = x.shape
    T = XA_TILE
    M = MEM_LEN
    weights = [p[n] for n in ('wq', 'wxo', 'ln2g', 'ln2b', 'rwt', 'rbt')]
    return pl.pallas_call(
        _xattn_kernel,
        out_shape=(jax.ShapeDtypeStruct((B, S, D), F32), jax.ShapeDtypeStruct((B, S, D // 2), jnp.uint32),
                   jax.ShapeDtypeStruct((ROUTE_ROWS, B * S), F32), jax.ShapeDtypeStruct((N_EXPERTS, LANES), F32)),
        grid=(B, S // T),
        in_specs=[pl.BlockSpec((1, T, D), lambda b, j: (b, j, 0)),
                  pl.BlockSpec((1, D, M), lambda b, j: (b, 0, 0)),
                  pl.BlockSpec((1, M, D), lambda b, j: (b, 0, 0))] + [_layer_spec(w, l) for w in weights],
        out_specs=(pl.BlockSpec((1, T, D), lambda b, j: (b, j, 0)),
                   pl.BlockSpec((1, T, D // 2), lambda b, j: (b, j, 0)),
                   pl.BlockSpec((ROUTE_ROWS, T), lambda b, j: (0, b * (S // T) + j)),
                   pl.BlockSpec((N_EXPERTS, LANES), lambda b, j: (0, 0))),
        scratch_shapes=[pltpu.VMEM((N_EXPERTS, LANES), F32)],
        compiler_params=pltpu.CompilerParams(dimension_semantics=("arbitrary", "arbitrary"),
                                             vmem_limit_bytes=VMEM_LIMIT),
        name="xattn",
    )(x, kt, v, *weights)


def _expert_kernel(layer, se_ref, sh_ref, nu_ref, nxt_ref, xs_ref, wgu_hbm, bgu_ref, wd_hbm, bd_ref, out_ref,
                   wgu_st, wd_st, slot_ref, sem):
    i = pl.program_id(0)
    F = EXPERT_FF
    halves = sh_ref[i]

    def weight_copies(e, slot):
        return (pltpu.make_async_copy(wgu_hbm.at[layer, e], wgu_st.at[slot], sem.at[slot, 0]),
                pltpu.make_async_copy(wd_hbm.at[layer, e], wd_st.at[slot], sem.at[slot, 1]))

    def ffn(rows):
        slot = slot_ref[0]
        xsb = _unpack_bf16_pairs(xs_ref[rows, :]).astype(BF16)
        hh = _dot(xsb, wgu_st[slot].astype(BF16)) + bgu_ref[...]
        h_glu = jnp.minimum(hh[:, :F], SWIGLU_LIMIT)
        h_lin = jnp.clip(hh[:, F:], -SWIGLU_LIMIT, SWIGLU_LIMIT)
        a = h_glu * _sigmoid(SWIGLU_ALPHA * h_glu) * (h_lin + 1.0)
        out_ref[rows, :] = _pack_bf16_pairs(_dot(a.astype(BF16), wd_st[slot].astype(BF16)) + bd_ref[...])

    @pl.when(halves > 0)
    def _():
        e = se_ref[i]
        prev = se_ref[jnp.maximum(i - 1, 0)]

        @pl.when(i == 0)
        def _():
            slot_ref[0] = 1
            for cp in weight_copies(e, 0):
                cp.start()

        @pl.when((i == 0) | (e != prev))
        def _():
            slot = 1 - slot_ref[0]
            slot_ref[0] = slot
            for cp in weight_copies(e, slot):
                cp.wait()
            nxt = nxt_ref[e]

            @pl.when(nxt != e)
            def _():
                for cp in weight_copies(nxt, 1 - slot):
                    cp.start()

    for n in range(1, MOE_STEP_BLOCKS + 1):
        @pl.when(halves == n)
        def _(n=n):
            ffn(slice(0, n * MOE_BLOCK))


def _experts(step_expert, step_halves, n_used, next_expert, xs, w_gu, b_gu, w_down, b_down, l):
    P, DH = xs.shape
    D = 2 * DH
    R = MOE_STEP_BLOCKS * MOE_BLOCK
    NS = P // R
    F2 = 2 * EXPERT_FF

    def row_map(i, se, sh, nu, nxt):
        return (jnp.minimum(i, nu[0] - 1), 0)

    def exp_map(i, se, sh, nu, nxt):
        return (l, se[jnp.minimum(i, nu[0] - 1)], 0, 0)

    grid_spec = pltpu.PrefetchScalarGridSpec(
        num_scalar_prefetch=4,
        grid=(NS,),
        in_specs=[pl.BlockSpec((R, DH), row_map),
                  pl.BlockSpec(memory_space=pl.ANY),
                  pl.BlockSpec((None, None, 1, F2), exp_map),
                  pl.BlockSpec(memory_space=pl.ANY),
                  pl.BlockSpec((None, None, 1, D), exp_map)],
        out_specs=pl.BlockSpec((R, DH), row_map),
        scratch_shapes=[pltpu.VMEM((2, D, F2), F32), pltpu.VMEM((2, EXPERT_FF, D), F32),
                        pltpu.SMEM((1,), jnp.int32),
                        pltpu.SemaphoreType.DMA((2, 2))],
    )
    return pl.pallas_call(
        functools.partial(_expert_kernel, l),
        out_shape=jax.ShapeDtypeStruct((P, DH), jnp.uint32),
        grid_spec=grid_spec,
        compiler_params=pltpu.CompilerParams(dimension_semantics=("arbitrary",), vmem_limit_bytes=VMEM_LIMIT),
        name="experts",
    )(step_expert, step_halves, n_used, next_expert, xs, w_gu, b_gu.reshape(DEPTH, N_EXPERTS, 1, F2), w_down,
      b_down.reshape(DEPTH, N_EXPERTS, 1, D))


def _sc_gather_rows(x, idx):
    M = idx.shape[0]
    D = x.shape[1]
    W = SC_GATHER_WINDOW
    mesh = plsc.VectorSubcoreMesh(core_axis_name="core", subcore_axis_name="subcore")
    n_workers = mesh.num_cores * mesh.num_subcores
    rows_per = M // n_workers
    assert rows_per * n_workers == M and rows_per % W == 0

    @pl.kernel(out_type=jax.ShapeDtypeStruct((M, D), x.dtype), mesh=mesh, name="sc_gather_rows",
               scratch_types=[pltpu.VMEM((rows_per,), jnp.int32), pltpu.VMEM((W, D), x.dtype)])
    def gather_kernel(x_hbm, i_hbm, o_hbm, idx_vmem, buf):
        wid = lax.axis_index("core") * mesh.num_subcores + lax.axis_index("subcore")
        base = wid * rows_per
        pltpu.sync_copy(i_hbm.at[pl.ds(base, rows_per)], idx_vmem)

        @pl.loop(0, rows_per // W)
        def _(j):
            pltpu.sync_copy(x_hbm.at[idx_vmem.at[pl.ds(j * W, W)]], buf)
            pltpu.sync_copy(buf, o_hbm.at[pl.ds(base + j * W, W)])

    return gather_kernel(x, idx)


def _sc_scatter_rows(x, idx, n_out):
    K, N = idx.shape
    D = x.shape[1]
    W = SC_GATHER_WINDOW
    mesh = plsc.VectorSubcoreMesh(core_axis_name="core", subcore_axis_name="subcore")
    n_workers = mesh.num_cores * mesh.num_subcores
    rows_per = N // n_workers
    assert rows_per * n_workers == N and rows_per % W == 0

    @pl.kernel(out_type=jax.ShapeDtypeStruct((n_out, D), x.dtype), mesh=mesh, name="sc_scatter_rows",
               scratch_types=[pltpu.VMEM((K * rows_per,), jnp.int32), pltpu.VMEM((W, D), x.dtype),
                              pltpu.SemaphoreType.DMA((K,))])
    def scatter_kernel(x_hbm, i_hbm, o_hbm, idx_vmem, buf, sem):
        wid = lax.axis_index("core") * mesh.num_subcores + lax.axis_index("subcore")
        base = wid * rows_per
        for k in range(K):
            pltpu.sync_copy(i_hbm.at[pl.ds(k * N + base, rows_per)], idx_vmem.at[pl.ds(k * rows_per, rows_per)])

        @pl.loop(0, rows_per // W)
        def _(j):
            pltpu.sync_copy(x_hbm.at[pl.ds(base + j * W, W)], buf)
            copies = [pltpu.make_async_copy(buf, o_hbm.at[idx_vmem.at[pl.ds(k * rows_per + j * W, W)]], sem.at[k])
                      for k in range(K)]
            for cp in copies:
                cp.start()
            for cp in copies:
                cp.wait()

    return scatter_kernel(x, idx.reshape(K * N))


def _combine_kernel(x_ref, yg_ref, gate_ref, ln3g_ref, ln3b_ref, *rest):
    out_ref = rest[-1]
    x = x_ref[...]
    g = gate_ref[...]
    y = jnp.zeros_like(x)
    for k in range(TOP_K):
        y = y + g[:, k:k + 1] * _unpack_bf16_pairs(yg_ref[k * CMB_TILE:(k + 1) * CMB_TILE, :])
    out_ref[...] = _layer_norm(DEEPNORM_ALPHA * x + y, ln3g_ref[...], ln3b_ref[...])


def _combine(x2, yg, gate, p, l, part, acc):
    N, D = x2.shape
    T = CMB_TILE
    tiles = yg.shape[0] // (TOP_K * T)
    first = part * tiles
    in_specs = [pl.BlockSpec((T, D), lambda i: (first + i, 0)), pl.BlockSpec((TOP_K * T, D // 2), lambda i: (i, 0)),
                pl.BlockSpec((T, TOP_K), lambda i: (first + i, 0)), _layer_spec(p['ln3g'], l), _layer_spec(p['ln3b'], l)]
    args = [x2, yg, gate, p['ln3g'], p['ln3b']]
    aliases = {}
    if acc is not None:
        in_specs.append(pl.BlockSpec(memory_space=pl.ANY))
        args.append(acc)
        aliases = {len(args) - 1: 0}
    return pl.pallas_call(
        _combine_kernel,
        out_shape=jax.ShapeDtypeStruct((N, D), F32),
        grid=(tiles,),
        in_specs=in_specs,
        out_specs=pl.BlockSpec((T, D), lambda i: (first + i, 0)),
        input_output_aliases=aliases,
        compiler_params=pltpu.CompilerParams(dimension_semantics=("arbitrary",), vmem_limit_bytes=VMEM_LIMIT),
        name="combine",
    )(*args)


def _prep(w_in, b_in, gla_wg2, gla_bg, gla_norm_g, sgu_ln_g, sgu_ln_b, sgu_ws, sgu_bs, pool_w, pool_scale,
          w_up_a, w_up_b, w_up_c, w_o, ln1_g, ln1_b, xa_wq, xa_wk, xa_wv, xa_wo, ln2_g, ln2_b,
          router_w, router_b, ln3_g, ln3_b):
    L = w_in.shape[0]
    o_glow = O_GLOW
    o_uv = o_glow + GLA_RANK
    row = lambda a: a.reshape(L, 1, -1).astype(F32)
    pad_last = lambda a, n: jnp.pad(a, [(0, 0)] * (a.ndim - 1) + [(0, n - a.shape[-1])])
    p = {}
    w_bf = w_in.astype(BF16)
    p['wcat'] = jnp.concatenate([w_bf[..., :o_glow], pad_last(w_bf[..., o_glow:o_uv], LANES), w_bf[..., o_uv:]],
                                axis=-1)
    p['bcat'] = row(jnp.concatenate([b_in[..., :o_glow], pad_last(b_in[..., o_glow:o_uv], LANES), b_in[..., o_uv:]],
                                    axis=-1))
    p['wg2'] = jnp.pad(gla_wg2, ((0, 0), (0, LANES - GLA_RANK), (0, 0))).astype(BF16)
    p['bg'] = row(gla_bg)
    p['gnorm'] = row(gla_norm_g)
    p['slng'] = row(sgu_ln_g)
    p['slnb'] = row(sgu_ln_b)
    p['wtril'] = jnp.tril(sgu_ws).astype(BF16)
    p['sbias'] = jnp.repeat(jnp.swapaxes(sgu_bs, 1, 2), SGU_GD, axis=2).astype(F32)
    G = len(POOL_WINDOWS)
    eye = jnp.eye(G, dtype=F32)
    p['poolw'] = jnp.einsum('lgcd,gh->lgchd', pool_w, eye).reshape(L, POOL_WIDTH, POOL_WIDTH).astype(BF16)
    p['pscale'] = row(pool_scale)
    p['wup'] = jnp.concatenate([w_up_a, w_up_b, w_up_c], axis=1).astype(BF16)
    p['wo'] = w_o.astype(BF16)
    p['ln1g'], p['ln1b'] = row(ln1_g), row(ln1_b)
    p['wq'] = xa_wq.astype(BF16)
    p['wkt'] = jnp.swapaxes(xa_wk, 1, 2).astype(BF16)
    p['wv'] = xa_wv.astype(BF16)
    p['wxo'] = xa_wo.astype(BF16)
    p['ln2g'], p['ln2b'] = row(ln2_g), row(ln2_b)
    rwt = jnp.swapaxes(router_w, 1, 2)
    rwt_hi = rwt.astype(BF16)
    p['rwt'] = jnp.concatenate([rwt_hi, (rwt - rwt_hi.astype(F32)).astype(BF16)], axis=1)
    p['rbt'] = jnp.broadcast_to(router_b[:, :, None], router_b.shape + (XA_TILE,)).astype(F32)
    p['ln3g'], p['ln3b'] = row(ln3_g), row(ln3_b)
    return p


def _route(route, counts):
    N = route.shape[1]
    top_idx = route[0:TOP_K].astype(jnp.int32)
    gate = route[TOP_K:2 * TOP_K].T
    rank = route[2 * TOP_K:3 * TOP_K].astype(jnp.int32)
    counts = counts[:, 0].astype(jnp.int32)
    R = MOE_STEP_BLOCKS * MOE_BLOCK
    blocks = (counts + MOE_BLOCK - 1) // MOE_BLOCK
    padded = ((counts + R - 1) // R) * R
    pad_end = jnp.cumsum(padded)
    pad_start = pad_end - padded
    ids = jnp.arange(N_EXPERTS, dtype=jnp.int32)
    start_of = jnp.sum(jnp.where(top_idx[:, :, None] == ids[None, None, :], pad_start[None, None, :], 0), axis=-1)
    dest = start_of + rank
    n_steps = N * TOP_K // R + N_EXPERTS
    step_start = jnp.arange(n_steps, dtype=jnp.int32) * R
    step_expert = jnp.minimum(jnp.sum((pad_end[None, :] <= step_start[:, None]).astype(jnp.int32), axis=1),
                              N_EXPERTS - 1)
    mine = step_expert[:, None] == ids[None, :]
    blocks_before = (step_start - jnp.sum(jnp.where(mine, pad_start[None, :], 0), axis=1)) // MOE_BLOCK
    step_halves = jnp.clip(jnp.sum(jnp.where(mine, blocks[None, :], 0), axis=1) - blocks_before, 0, MOE_STEP_BLOCKS)
    step_halves = jnp.where(step_start < pad_end[-1], step_halves, 0).astype(jnp.int32)
    n_used = (pad_end[-1] // R).astype(jnp.int32).reshape(1)
    later = jnp.where((ids[None, :] > ids[:, None]) & (counts[None, :] > 0), ids[None, :], N_EXPERTS)
    first_later = jnp.min(later, axis=1)
    next_expert = jnp.where(first_later < N_EXPERTS, first_later, ids).astype(jnp.int32)
    return gate, dest, step_expert, step_halves, n_used, next_expert


def kernel(x, mem, w_in, b_in, gla_wg2, gla_bg, gla_norm_g, sgu_ln_g, sgu_ln_b, sgu_ws, sgu_bs, pool_w, pool_scale, w_up_a, w_up_b, w_up_c, w_o, ln1_g, ln1_b, xa_wq, xa_wk, xa_wv, xa_wo, ln2_g, ln2_b, router_w, router_b, exp_w_gu, exp_b_gu, exp_w_down, exp_b_down, ln3_g, ln3_b):
    B, S, D = x.shape
    N = B * S
    p = _prep(w_in, b_in, gla_wg2, gla_bg, gla_norm_g, sgu_ln_g, sgu_ln_b, sgu_ws, sgu_bs, pool_w, pool_scale,
              w_up_a, w_up_b, w_up_c, w_o, ln1_g, ln1_b, xa_wq, xa_wk, xa_wv, xa_wo, ln2_g, ln2_b,
              router_w, router_b, ln3_g, ln3_b)
    memt = jnp.swapaxes(mem, 1, 2)
    for l in range(DEPTH):
        x1 = _mixer(x, p, l)
        kt, v = _memkv(mem, memt, p, l)
        x2, x2p, route, counts = _xattn(x1, kt, v, p, l)
        gate, dest, step_expert, step_halves, n_used, next_expert = _route(route, counts)
        n_slots = N * TOP_K + N_EXPERTS * MOE_STEP_BLOCKS * MOE_BLOCK
        xs = _sc_scatter_rows(x2p.reshape(N, D // 2), dest, n_slots)
        ys = _experts(step_expert, step_halves, n_used, next_expert, xs, exp_w_gu, exp_b_gu, exp_w_down, exp_b_down, l)
        dest_km = dest.reshape(TOP_K, N // CMB_TILE, CMB_TILE).transpose(1, 0, 2).reshape(-1)
        rows = dest_km.shape[0] // COMBINE_PARTS
        acc = None
        for part in range(COMBINE_PARTS):
            yg = _sc_gather_rows(ys, dest_km[part * rows:(part + 1) * rows])
            acc = _combine(x2.reshape(N, D), yg, gate, p, l, part, acc)
        x = acc.reshape(B, S, D)
    return x
```

```python
import functools

import jax
import jax.numpy as jnp
from jax import lax
from jax.experimental import pallas as pl
from jax.experimental.pallas import tpu as pltpu
from jax.experimental.pallas import tpu_sc as plsc

F32 = jnp.float32
BF16 = jnp.bfloat16

D_MODEL = 1024
DEPTH = 2
GLA_HEADS = 4
GLA_KEY = 256
GLA_VAL = 512
GLA_DK = 64
GLA_DV = 128
GLA_RANK = 16
GLA_TAU = 16.0
GLA_CHUNK = 64
SGU_GROUPS = 4
SGU_WIDTH = 256
SGU_GD = 64
SGU_CHUNK = 128
POOL_WINDOWS = (2, 4, 8, 16)
POOL_WIDTH = 256
POOL_GD = 64
POOL_CARRY = 32
MEM_LEN = 256
XA_HEADS = 4
XA_DH = 256
N_EXPERTS = 32
TOP_K = 4
EXPERT_FF = 1024
SWIGLU_LIMIT = 7.0
SWIGLU_ALPHA = 1.702
DEEPNORM_ALPHA = (2 * DEPTH) ** 0.25
LN_EPS = 1e-5
LANES = 128
VMEM_LIMIT = 56 * 1024 * 1024

MIX_TILE = 512
GLA_SUB = 256
XA_TILE = 1024
MOE_BLOCK = 256
MOE_STEP_BLOCKS = 4
CMB_TILE = 512
COMBINE_PARTS = 4
ROUTE_ROWS = 16
SC_GATHER_WINDOW = 128

O_QKVR = 0
O_GLOW = 2 * GLA_KEY + 2 * GLA_VAL
O_UV = O_GLOW + LANES
O_XC = O_UV + 2 * SGU_WIDTH
O_GATE = O_XC + POOL_WIDTH
N_PACK = O_GATE + 3 * D_MODEL


def _dot(a, b):
    return jnp.dot(a, b, preferred_element_type=F32)


def _dot_t0(a, b):
    return lax.dot_general(a, b, (((0,), (0,)), ((), ())), preferred_element_type=F32)


def _dot_t1(a, b):
    return lax.dot_general(a, b, (((1,), (1,)), ((), ())), preferred_element_type=F32)


def _split_bf16(x):
    hi = x.astype(BF16)
    lo = (x - hi.astype(F32)).astype(BF16)
    return hi, lo


def _layer_norm(x, g, b):
    mu = jnp.mean(x, axis=-1, keepdims=True)
    xc = x - mu
    var = jnp.mean(xc * xc, axis=-1, keepdims=True)
    return xc * lax.rsqrt(var + LN_EPS) * g + b


def _sigmoid(x):
    return 1.0 / (1.0 + jnp.exp(-x))


def _pack_bf16_pairs(x):
    H = x.shape[1] // 2
    bits = lax.bitcast_convert_type(x.astype(BF16).astype(F32), jnp.uint32)
    return (bits[:, :H] >> 16) | (bits[:, H:] & jnp.uint32(0xFFFF0000))


def _unpack_bf16_pairs(w):
    lo = lax.bitcast_convert_type(w << 16, F32)
    hi = lax.bitcast_convert_type(w & jnp.uint32(0xFFFF0000), F32)
    return jnp.concatenate([lo, hi], axis=1)


def _layer_spec(arr, l):
    nd = arr.ndim - 1
    return pl.BlockSpec((None,) + arr.shape[1:], lambda *_: (l,) + (0,) * nd, pipeline_mode=pl.Buffered(1))


def _mixer_kernel(x_ref, wcat_ref, bcat_ref, wg2_ref, bg_ref, gnorm_ref,
                  slng_ref, slnb_ref, wtril_ref, sbias_ref, poolw_ref, pscale_ref,
                  wupa_ref, wupb_ref, wupc_ref, wo_ref, ln1g_ref, ln1b_ref,
                  out_ref,
                  state_ref, qkvr_ref, ya_ref, vln_ref, e_ref, s2_ref, s4_ref, s8_ref):
    T = MIX_TILE
    D = D_MODEL
    j = pl.program_id(1)
    x = x_ref[0]
    xb = x.astype(BF16)

    def proj(lo, hi):
        return _dot(xb, wcat_ref[:, lo:hi]) + bcat_ref[:, lo:hi]

    @pl.when(j == 0)
    def _():
        state_ref[...] = jnp.zeros_like(state_ref)
        e_ref[0:POOL_CARRY, :] = jnp.zeros((POOL_CARRY, POOL_WIDTH), F32)

    @pl.when(j > 0)
    def _():
        e_ref[0:POOL_CARRY, :] = e_ref[T:T + POOL_CARRY, :]

    qkvr_ref[...] = proj(O_QKVR, O_GLOW)
    glow = proj(O_GLOW, O_UV)
    z = _dot(glow.astype(BF16), wg2_ref[...]) + bg_ref[...]
    la = (jnp.minimum(z, 0.0) - jnp.log1p(jnp.exp(-jnp.abs(z)))) * (1.0 / GLA_TAU)
    la_hi, la_lo = _split_bf16(la)
    gate_a = _sigmoid(proj(O_GATE, O_GATE + D))

    C = GLA_CHUNK
    G = GLA_SUB
    NC = G // C
    CSH = C.bit_length() - 1
    row = lax.broadcasted_iota(jnp.int32, (G, G), 0)
    col = lax.broadcasted_iota(jnp.int32, (G, G), 1)
    same_chunk = (row >> CSH) == (col >> CSH)
    causal = same_chunk & (row >= col)
    causal_bf = jnp.where(causal, 1.0, 0.0).astype(BF16)
    lane = lax.broadcasted_iota(jnp.int32, (G, GLA_KEY), 1)
    gate_b = None
    for g0 in range(0, T, G):
        gr = slice(g0, g0 + G)
        lh, ll = la_hi[gr], la_lo[gr]
        b = _dot(causal_bf, lh) + _dot(causal_bf, ll)
        b_last = [b[(c + 1) * C - 1:(c + 1) * C, :] for c in range(NC)]
        b_end = jnp.concatenate([jnp.broadcast_to(r, (C, GLA_KEY)) for r in b_last], axis=0)
        q = qkvr_ref[gr, 0:GLA_KEY]
        k = qkvr_ref[gr, GLA_KEY:2 * GLA_KEY]
        v = qkvr_ref[gr, 2 * GLA_KEY:2 * GLA_KEY + GLA_VAL].astype(BF16)
        q_dec = q * (GLA_DK ** -0.5) * jnp.exp(b)
        k_dec = (k * jnp.exp(-b)).astype(BF16)
        k_tail = (k * jnp.exp(b_end - b)).astype(BF16)
        q_dec_bf = q_dec.astype(BF16)
        o_heads = []
        for h in range(GLA_HEADS):
            q_h = jnp.where((lane >= h * GLA_DK) & (lane < (h + 1) * GLA_DK), q_dec, 0.0).astype(BF16)
            scores = jnp.where(causal, _dot_t1(q_h, k_dec), 0.0).astype(BF16)
            o_heads.append(_dot(scores, v[:, h * GLA_DV:(h + 1) * GLA_DV]))
        o_intra = jnp.concatenate(o_heads, axis=1)
        if gate_b is None:
            gate_b = _sigmoid(proj(O_GATE + D, O_GATE + 2 * D))
        dec_cols = jnp.exp(jnp.concatenate(b_last + [jnp.zeros((8 - NC, GLA_KEY), F32)], axis=0)).T
        dec_all = jnp.concatenate([jnp.broadcast_to(dec_cols[:, c:c + 1], (GLA_KEY, LANES)) for c in range(NC)], axis=1)
        o_inter = []
        for c in range(NC):
            rows = slice(c * C, (c + 1) * C)
            o_inter.append(_dot(q_dec_bf[rows], state_ref[...].astype(BF16)))
            kv = _dot_t0(k_tail[rows], v[rows])
            for h in range(GLA_HEADS):
                rs = slice(h * GLA_DK, (h + 1) * GLA_DK)
                cs = slice(h * GLA_DV, (h + 1) * GLA_DV)
                state_ref[rs, cs] = dec_all[rs, c * LANES:(c + 1) * LANES] * state_ref[rs, cs] + kv[rs, cs]
        o = o_intra + jnp.concatenate(o_inter, axis=0)
        for h in range(GLA_HEADS):
            cs = slice(h * GLA_DV, (h + 1) * GLA_DV)
            o_h = o[:, cs]
            ms = jnp.mean(o_h * o_h, axis=-1, keepdims=True)
            o_h = o_h * lax.rsqrt(ms + LN_EPS) * gnorm_ref[:, cs]
            r_h = qkvr_ref[gr, 2 * GLA_KEY + GLA_VAL + h * GLA_DV:2 * GLA_KEY + GLA_VAL + (h + 1) * GLA_DV]
            ya_ref[gr, cs] = (o_h * (r_h * _sigmoid(r_h))).astype(BF16)

    gate_c = _sigmoid(proj(O_GATE + 2 * D, O_GATE + 3 * D))
    uv = proj(O_UV, O_XC)
    zg = 0.5 * uv * (1.0 + lax.erf(uv * (2.0 ** -0.5)))
    u = zg[:, :SGU_WIDTH]
    vln_ref[...] = _layer_norm(zg[:, SGU_WIDTH:], slng_ref[...], slnb_ref[...])
    lane_s = lax.broadcasted_iota(jnp.int32, (SGU_CHUNK, SGU_WIDTH), 1)
    s_parts = []
    for n in range(T // SGU_CHUNK):
        vc = vln_ref[n * SGU_CHUNK:(n + 1) * SGU_CHUNK, :]
        s = sbias_ref[...]
        for g in range(SGU_GROUPS):
            vg = jnp.where((lane_s >= g * SGU_GD) & (lane_s < (g + 1) * SGU_GD), vc, 0.0).astype(BF16)
            s = s + _dot(wtril_ref[g], vg)
        s_parts.append(s)
    y_b = (u * jnp.concatenate(s_parts, axis=0)).astype(BF16)

    P = POOL_CARRY
    xc = proj(O_XC, O_GATE)
    e_ref[P:P + T, :] = xc
    s2_ref[8:P + T, :] = e_ref[8:P + T, :] + e_ref[7:P + T - 1, :]
    s4_ref[16:P + T, :] = s2_ref[16:P + T, :] + s2_ref[14:P + T - 2, :]
    s8_ref[24:P + T, :] = s4_ref[24:P + T, :] + s4_ref[20:P + T - 4, :]
    s16 = s8_ref[P:P + T, :] + s8_ref[P - 8:P + T - 8, :]
    lane_p = lax.broadcasted_iota(jnp.int32, (T, POOL_WIDTH), 1)
    tpos = lax.broadcasted_iota(jnp.int32, (T, POOL_WIDTH), 0) + (j * T + 1)
    grp = lane_p >> 6
    win = jnp.where(grp == 0, POOL_WINDOWS[0], jnp.where(grp == 1, POOL_WINDOWS[1],
                    jnp.where(grp == 2, POOL_WINDOWS[2], POOL_WINDOWS[3])))
    wsum = jnp.where(grp == 0, s2_ref[P:P + T, :], jnp.where(grp == 1, s4_ref[P:P + T, :],
                     jnp.where(grp == 2, s8_ref[P:P + T, :], s16)))
    count = jnp.minimum(tpos, win).astype(F32)
    pooled = wsum / count - xc
    y_c = (_dot(pooled.astype(BF16), poolw_ref[...]) * pscale_ref[...]).astype(BF16)

    merged = gate_a * _dot(ya_ref[...], wupa_ref[...].astype(BF16))
    merged += gate_b * _dot(y_b, wupb_ref[...].astype(BF16))
    merged += gate_c * _dot(y_c, wupc_ref[...].astype(BF16))
    h = _dot(merged.astype(BF16), wo_ref[...].astype(BF16))
    out_ref[0] = _layer_norm(DEEPNORM_ALPHA * x + h, ln1g_ref[...], ln1b_ref[...])


_MIXER_WEIGHTS = ('wcat', 'bcat', 'wg2', 'bg', 'gnorm', 'slng', 'slnb', 'wtril', 'sbias', 'poolw', 'pscale',
                  'wupa', 'wupb', 'wupc', 'wo', 'ln1g', 'ln1b')


def _mixer(x, p, l):
    B, S, D = x.shape
    T = MIX_TILE
    weights = [p[n] for n in _MIXER_WEIGHTS]
    return pl.pallas_call(
        _mixer_kernel,
        out_shape=jax.ShapeDtypeStruct((B, S, D), F32),
        grid=(B, S // T),
        in_specs=[pl.BlockSpec((1, T, D), lambda b, j: (b, j, 0))] + [_layer_spec(w, l) for w in weights],
        out_specs=pl.BlockSpec((1, T, D), lambda b, j: (b, j, 0)),
        scratch_shapes=[
            pltpu.VMEM((GLA_KEY, GLA_VAL), F32),
            pltpu.VMEM((T, 2 * GLA_KEY + 2 * GLA_VAL), F32),
            pltpu.VMEM((T, GLA_VAL), BF16),
            pltpu.VMEM((T, SGU_WIDTH), F32),
            pltpu.VMEM((T + POOL_CARRY, POOL_WIDTH), F32),
            pltpu.VMEM((T + POOL_CARRY, POOL_WIDTH), F32),
            pltpu.VMEM((T + POOL_CARRY, POOL_WIDTH), F32),
            pltpu.VMEM((T + POOL_CARRY, POOL_WIDTH), F32),
        ],
        compiler_params=pltpu.CompilerParams(dimension_semantics=("arbitrary", "arbitrary"),
                                             vmem_limit_bytes=VMEM_LIMIT),
        name="mixer",
    )(x, *weights)


def _memkv_kernel(memt_ref, mem_ref, wk_ref, wv_ref, kt_ref, v_ref):
    kt_ref[0] = _dot_t0(wk_ref[...].astype(BF16), memt_ref[0].astype(BF16)).astype(BF16)
    v_ref[0] = _dot(mem_ref[0].astype(BF16), wv_ref[...].astype(BF16)).astype(BF16)


def _memkv(mem, memt, p, l):
    B, M, D = mem.shape
    return pl.pallas_call(
        _memkv_kernel,
        out_shape=(jax.ShapeDtypeStruct((B, D, M), BF16), jax.ShapeDtypeStruct((B, M, D), BF16)),
        grid=(B,),
        in_specs=[pl.BlockSpec((1, D, M), lambda b: (b, 0, 0)), pl.BlockSpec((1, M, D), lambda b: (b, 0, 0)),
                  _layer_spec(p['wk'], l), _layer_spec(p['wv'], l)],
        out_specs=(pl.BlockSpec((1, D, M), lambda b: (b, 0, 0)), pl.BlockSpec((1, M, D), lambda b: (b, 0, 0))),
        compiler_params=pltpu.CompilerParams(dimension_semantics=("arbitrary",), vmem_limit_bytes=VMEM_LIMIT),
        name="memkv",
    )(memt, mem, p['wk'], p['wv'])


def _xattn_kernel(x_ref, kt_ref, v_ref, wq_ref, wo_ref, ln2g_ref, ln2b_ref, rwt_ref, rbt_ref,
                  x2_ref, x2p_ref, route_ref, counts_ref, carry_ref):
    T = XA_TILE

    @pl.when((pl.program_id(0) == 0) & (pl.program_id(1) == 0))
    def _():
        carry_ref[...] = jnp.zeros_like(carry_ref)

    x = x_ref[0]
    q = (_dot(x.astype(BF16), wq_ref[...].astype(BF16)) * (XA_DH ** -0.5)).astype(BF16)
    h = jnp.zeros_like(x)
    for hd in range(XA_HEADS):
        cs = slice(hd * XA_DH, (hd + 1) * XA_DH)
        s = _dot(q[:, cs], kt_ref[0, cs, :])
        e = jnp.exp(s - jnp.max(s, axis=-1, keepdims=True))
        o = _dot(e.astype(BF16), v_ref[0, :, cs]) / jnp.sum(e, axis=-1, keepdims=True)
        h = h + _dot(o.astype(BF16), wo_ref[cs, :].astype(BF16))
    x2 = _layer_norm(DEEPNORM_ALPHA * x + h, ln2g_ref[...], ln2b_ref[...])
    x2_ref[0] = x2
    x2p_ref[0] = _pack_bf16_pairs(x2)

    E = N_EXPERTS
    hi, lo = _split_bf16(x2)
    lt = _dot_t1(rwt_ref[...], hi)
    logits = lt[0:E] + (lt[E:2 * E] + _dot_t1(rwt_ref[0:E, :], lo)) + rbt_ref[...]

    eid = lax.broadcasted_iota(jnp.int32, (E, T), 0)
    neg_inf = jnp.float32(-jnp.inf)
    rest = logits
    tops, picks = [], []
    for _ in range(TOP_K):
        m = jnp.max(rest, axis=0, keepdims=True)
        idx = jnp.min(jnp.where(rest == m, eid, E), axis=0, keepdims=True)
        pick = eid == idx
        rest = jnp.where(pick, neg_inf, rest)
        tops.append((m, idx))
        picks.append(pick)
    exps = [jnp.exp(m - tops[0][0]) for m, _ in tops]
    denom = exps[0]
    for e in exps[1:]:
        denom = denom + e

    chosen = jnp.zeros((E, T), F32)
    for pick in picks:
        chosen = chosen + jnp.where(pick, 1.0, 0.0)
    chosen_bf = chosen.astype(BF16)
    earlier = (lax.broadcasted_iota(jnp.int32, (T, T), 0) < lax.broadcasted_iota(jnp.int32, (T, T), 1))
    carry = carry_ref[...]
    before = _dot(chosen_bf, jnp.where(earlier, 1.0, 0.0).astype(BF16)) + jnp.concatenate([carry] * (T // LANES), axis=1)
    carry = carry + _dot(chosen_bf, jnp.ones((T, LANES), BF16))
    carry_ref[...] = carry
    counts_ref[...] = carry

    rid = lax.broadcasted_iota(jnp.int32, (ROUTE_ROWS, T), 0)
    route = jnp.zeros((ROUTE_ROWS, T), F32)
    for k in range(TOP_K):
        rank = jnp.sum(jnp.where(picks[k], before, 0.0), axis=0, keepdims=True)
        route = jnp.where(rid == k, tops[k][1].astype(F32), route)
        route = jnp.where(rid == TOP_K + k, exps[k] / denom, route)
        route = jnp.where(rid == 2 * TOP_K + k, rank, route)
    route_ref[...] = route


def _xattn(x, kt, v, p, l):
    B, S, D = x.shape
    T = XA_TILE
    M = MEM_LEN
    weights = [p[n] for n in ('wq', 'wxo', 'ln2g', 'ln2b', 'rwt', 'rbt')]
    return pl.pallas_call(
        _xattn_kernel,
        out_shape=(jax.ShapeDtypeStruct((B, S, D), F32), jax.ShapeDtypeStruct((B, S, D // 2), jnp.uint32),
                   jax.ShapeDtypeStruct((ROUTE_ROWS, B * S), F32), jax.ShapeDtypeStruct((N_EXPERTS, LANES), F32)),
        grid=(B, S // T),
        in_specs=[pl.BlockSpec((1, T, D), lambda b, j: (b, j, 0)),
                  pl.BlockSpec((1, D, M), lambda b, j: (b, 0, 0)),
                  pl.BlockSpec((1, M, D), lambda b, j: (b, 0, 0))] + [_layer_spec(w, l) for w in weights],
        out_specs=(pl.BlockSpec((1, T, D), lambda b, j: (b, j, 0)),
                   pl.BlockSpec((1, T, D // 2), lambda b, j: (b, j, 0)),
                   pl.BlockSpec((ROUTE_ROWS, T), lambda b, j: (0, b * (S // T) + j)),
                   pl.BlockSpec((N_EXPERTS, LANES), lambda b, j: (0, 0))),
        scratch_shapes=[pltpu.VMEM((N_EXPERTS, LANES), F32)],
        compiler_params=pltpu.CompilerParams(dimension_semantics=("arbitrary", "arbitrary"),
                                             vmem_limit_bytes=VMEM_LIMIT),
        name="xattn",
    )(x, kt, v, *weights)


def _expert_kernel(layer, se_ref, sh_ref, nu_ref, nxt_ref, xs_ref, wgu_hbm, bgu_ref, wd_hbm, bd_ref, out_ref,
                   wgu_st, wd_st, slot_ref, sem):
    i = pl.program_id(0)
    F = EXPERT_FF
    halves = sh_ref[i]

    def weight_copies(e, slot):
        return (pltpu.make_async_copy(wgu_hbm.at[layer, e], wgu_st.at[slot], sem.at[slot, 0]),
                pltpu.make_async_copy(wd_hbm.at[layer, e], wd_st.at[slot], sem.at[slot, 1]))

    def ffn(rows):
        slot = slot_ref[0]
        xsb = _unpack_bf16_pairs(xs_ref[rows, :]).astype(BF16)
        hh = _dot(xsb, wgu_st[slot].astype(BF16)) + bgu_ref[...]
        h_glu = jnp.minimum(hh[:, :F], SWIGLU_LIMIT)
        h_lin = jnp.clip(hh[:, F:], -SWIGLU_LIMIT, SWIGLU_LIMIT)
        a = h_glu * _sigmoid(SWIGLU_ALPHA * h_glu) * (h_lin + 1.0)
        out_ref[rows, :] = _pack_bf16_pairs(_dot(a.astype(BF16), wd_st[slot].astype(BF16)) + bd_ref[...])

    @pl.when(halves > 0)
    def _():
        e = se_ref[i]
        prev = se_ref[jnp.maximum(i - 1, 0)]

        @pl.when(i == 0)
        def _():
            slot_ref[0] = 1
            for cp in weight_copies(e, 0):
                cp.start()

        @pl.when((i == 0) | (e != prev))
        def _():
            slot = 1 - slot_ref[0]
            slot_ref[0] = slot
            for cp in weight_copies(e, slot):
                cp.wait()
            nxt = nxt_ref[e]

            @pl.when(nxt != e)
            def _():
                for cp in weight_copies(nxt, 1 - slot):
                    cp.start()

    for n in range(1, MOE_STEP_BLOCKS + 1):
        @pl.when(halves == n)
        def _(n=n):
            ffn(slice(0, n * MOE_BLOCK))


def _experts(step_expert, step_halves, n_used, next_expert, xs, w_gu, b_gu, w_down, b_down, l):
    P, DH = xs.shape
    D = 2 * DH
    R = MOE_STEP_BLOCKS * MOE_BLOCK
    NS = P // R
    F2 = 2 * EXPERT_FF

    def row_map(i, se, sh, nu, nxt):
        return (jnp.minimum(i, nu[0] - 1), 0)

    def exp_map(i, se, sh, nu, nxt):
        return (l, se[jnp.minimum(i, nu[0] - 1)], 0, 0)

    grid_spec = pltpu.PrefetchScalarGridSpec(
        num_scalar_prefetch=4,
        grid=(NS,),
        in_specs=[pl.BlockSpec((R, DH), row_map),
                  pl.BlockSpec(memory_space=pl.ANY),
                  pl.BlockSpec((None, None, 1, F2), exp_map),
                  pl.BlockSpec(memory_space=pl.ANY),
                  pl.BlockSpec((None, None, 1, D), exp_map)],
        out_specs=pl.BlockSpec((R, DH), row_map),
        scratch_shapes=[pltpu.VMEM((2, D, F2), F32), pltpu.VMEM((2, EXPERT_FF, D), F32),
                        pltpu.SMEM((1,), jnp.int32),
                        pltpu.SemaphoreType.DMA((2, 2))],
    )
    return pl.pallas_call(
        functools.partial(_expert_kernel, l),
        out_shape=jax.ShapeDtypeStruct((P, DH), jnp.uint32),
        grid_spec=grid_spec,
        compiler_params=pltpu.CompilerParams(dimension_semantics=("arbitrary",), vmem_limit_bytes=VMEM_LIMIT),
        name="experts",
    )(step_expert, step_halves, n_used, next_expert, xs, w_gu, b_gu.reshape(DEPTH, N_EXPERTS, 1, F2), w_down,
      b_down.reshape(DEPTH, N_EXPERTS, 1, D))


def _sc_gather_rows(x, idx):
    M = idx.shape[0]
    D = x.shape[1]
    W = SC_GATHER_WINDOW
    mesh = plsc.VectorSubcoreMesh(core_axis_name="core", subcore_axis_name="subcore")
    n_workers = mesh.num_cores * mesh.num_subcores
    rows_per = M // n_workers
    assert rows_per * n_workers == M and rows_per % W == 0

    @pl.kernel(out_type=jax.ShapeDtypeStruct((M, D), x.dtype), mesh=mesh, name="sc_gather_rows",
               scratch_types=[pltpu.VMEM((rows_per,), jnp.int32), pltpu.VMEM((W, D), x.dtype)])
    def gather_kernel(x_hbm, i_hbm, o_hbm, idx_vmem, buf):
        wid = lax.axis_index("core") * mesh.num_subcores + lax.axis_index("subcore")
        base = wid * rows_per
        pltpu.sync_copy(i_hbm.at[pl.ds(base, rows_per)], idx_vmem)

        @pl.loop(0, rows_per // W)
        def _(j):
            pltpu.sync_copy(x_hbm.at[idx_vmem.at[pl.ds(j * W, W)]], buf)
            pltpu.sync_copy(buf, o_hbm.at[pl.ds(base + j * W, W)])

    return gather_kernel(x, idx)


def _sc_scatter_rows(x, idx, n_out):
    K, N = idx.shape
    D = x.shape[1]
    W = SC_GATHER_WINDOW
    mesh = plsc.VectorSubcoreMesh(core_axis_name="core", subcore_axis_name="subcore")
    n_workers = mesh.num_cores * mesh.num_subcores
    rows_per = N // n_workers
    assert rows_per * n_workers == N and rows_per % W == 0

    @pl.kernel(out_type=jax.ShapeDtypeStruct((n_out, D), x.dtype), mesh=mesh, name="sc_scatter_rows",
               scratch_types=[pltpu.VMEM((K * rows_per,), jnp.int32), pltpu.VMEM((W, D), x.dtype),
                              pltpu.SemaphoreType.DMA((K,))])
    def scatter_kernel(x_hbm, i_hbm, o_hbm, idx_vmem, buf, sem):
        wid = lax.axis_index("core") * mesh.num_subcores + lax.axis_index("subcore")
        base = wid * rows_per
        for k in range(K):
            pltpu.sync_copy(i_hbm.at[pl.ds(k * N + base, rows_per)], idx_vmem.at[pl.ds(k * rows_per, rows_per)])

        @pl.loop(0, rows_per // W)
        def _(j):
            pltpu.sync_copy(x_hbm.at[pl.ds(base + j * W, W)], buf)
            copies = [pltpu.make_async_copy(buf, o_hbm.at[idx_vmem.at[pl.ds(k * rows_per + j * W, W)]], sem.at[k])
                      for k in range(K)]
            for cp in copies:
                cp.start()
            for cp in copies:
                cp.wait()

    return scatter_kernel(x, idx.reshape(K * N))


def _combine_kernel(x_ref, yg_ref, gate_ref, ln3g_ref, ln3b_ref, *rest):
    out_ref = rest[-1]
    x = x_ref[...]
    g = gate_ref[...]
    y = jnp.zeros_like(x)
    for k in range(TOP_K):
        y = y + g[:, k:k + 1] * _unpack_bf16_pairs(yg_ref[k * CMB_TILE:(k + 1) * CMB_TILE, :])
    out_ref[...] = _layer_norm(DEEPNORM_ALPHA * x + y, ln3g_ref[...], ln3b_ref[...])


def _combine(x2, yg, gate, p, l, part, acc):
    N, D = x2.shape
    T = CMB_TILE
    tiles = yg.shape[0] // (TOP_K * T)
    first = part * tiles
    in_specs = [pl.BlockSpec((T, D), lambda i: (first + i, 0)), pl.BlockSpec((TOP_K * T, D // 2), lambda i: (i, 0)),
                pl.BlockSpec((T, TOP_K), lambda i: (first + i, 0)), _layer_spec(p['ln3g'], l), _layer_spec(p['ln3b'], l)]
    args = [x2, yg, gate, p['ln3g'], p['ln3b']]
    aliases = {}
    if acc is not None:
        in_specs.append(pl.BlockSpec(memory_space=pl.ANY))
        args.append(acc)
        aliases = {len(args) - 1: 0}
    return pl.pallas_call(
        _combine_kernel,
        out_shape=jax.ShapeDtypeStruct((N, D), F32),
        grid=(tiles,),
        in_specs=in_specs,
        out_specs=pl.BlockSpec((T, D), lambda i: (first + i, 0)),
        input_output_aliases=aliases,
        compiler_params=pltpu.CompilerParams(dimension_semantics=("arbitrary",), vmem_limit_bytes=VMEM_LIMIT),
        name="combine",
    )(*args)


def _prep(w_in, b_in, gla_wg2, gla_bg, gla_norm_g, sgu_ln_g, sgu_ln_b, sgu_ws, sgu_bs, pool_w, pool_scale,
          w_up_a, w_up_b, w_up_c, w_o, ln1_g, ln1_b, xa_wq, xa_wk, xa_wv, xa_wo, ln2_g, ln2_b,
          router_w, router_b, ln3_g, ln3_b):
    L = w_in.shape[0]
    o_glow = O_GLOW
    o_uv = o_glow + GLA_RANK
    row = lambda a: a.reshape(L, 1, -1).astype(F32)
    pad_last = lambda a, n: jnp.pad(a, [(0, 0)] * (a.ndim - 1) + [(0, n - a.shape[-1])])
    p = {}
    w_bf = w_in.astype(BF16)
    p['wcat'] = jnp.concatenate([w_bf[..., :o_glow], pad_last(w_bf[..., o_glow:o_uv], LANES), w_bf[..., o_uv:]],
                                axis=-1)
    p['bcat'] = row(jnp.concatenate([b_in[..., :o_glow], pad_last(b_in[..., o_glow:o_uv], LANES), b_in[..., o_uv:]],
                                    axis=-1))
    p['wg2'] = jnp.pad(gla_wg2, ((0, 0), (0, LANES - GLA_RANK), (0, 0))).astype(BF16)
    p['bg'] = row(gla_bg)
    p['gnorm'] = row(gla_norm_g)
    p['slng'] = row(sgu_ln_g)
    p['slnb'] = row(sgu_ln_b)
    p['wtril'] = jnp.tril(sgu_ws).astype(BF16)
    p['sbias'] = jnp.repeat(jnp.swapaxes(sgu_bs, 1, 2), SGU_GD, axis=2).astype(F32)
    G = len(POOL_WINDOWS)
    eye = jnp.eye(G, dtype=F32)
    p['poolw'] = jnp.einsum('lgcd,gh->lgchd', pool_w, eye).reshape(L, POOL_WIDTH, POOL_WIDTH).astype(BF16)
    p['pscale'] = row(pool_scale)
    p['wupa'], p['wupb'], p['wupc'], p['wo'] = w_up_a, w_up_b, w_up_c, w_o
    p['ln1g'], p['ln1b'] = row(ln1_g), row(ln1_b)
    p['wq'], p['wk'], p['wv'], p['wxo'] = xa_wq, xa_wk, xa_wv, xa_wo
    p['ln2g'], p['ln2b'] = row(ln2_g), row(ln2_b)
    rwt = jnp.swapaxes(router_w, 1, 2)
    rwt_hi = rwt.astype(BF16)
    p['rwt'] = jnp.concatenate([rwt_hi, (rwt - rwt_hi.astype(F32)).astype(BF16)], axis=1)
    p['rbt'] = jnp.broadcast_to(router_b[:, :, None], router_b.shape + (XA_TILE,)).astype(F32)
    p['ln3g'], p['ln3b'] = row(ln3_g), row(ln3_b)
    return p


def _route(route, counts):
    N = route.shape[1]
    top_idx = route[0:TOP_K].astype(jnp.int32)
    gate = route[TOP_K:2 * TOP_K].T
    rank = route[2 * TOP_K:3 * TOP_K].astype(jnp.int32)
    counts = counts[:, 0].astype(jnp.int32)
    R = MOE_STEP_BLOCKS * MOE_BLOCK
    blocks = (counts + MOE_BLOCK - 1) // MOE_BLOCK
    padded = ((counts + R - 1) // R) * R
    pad_end = jnp.cumsum(padded)
    pad_start = pad_end - padded
    ids = jnp.arange(N_EXPERTS, dtype=jnp.int32)
    start_of = jnp.sum(jnp.where(top_idx[:, :, None] == ids[None, None, :], pad_start[None, None, :], 0), axis=-1)
    dest = start_of + rank
    n_steps = N * TOP_K // R + N_EXPERTS
    step_start = jnp.arange(n_steps, dtype=jnp.int32) * R
    step_expert = jnp.minimum(jnp.sum((pad_end[None, :] <= step_start[:, None]).astype(jnp.int32), axis=1),
                              N_EXPERTS - 1)
    mine = step_expert[:, None] == ids[None, :]
    blocks_before = (step_start - jnp.sum(jnp.where(mine, pad_start[None, :], 0), axis=1)) // MOE_BLOCK
    step_halves = jnp.clip(jnp.sum(jnp.where(mine, blocks[None, :], 0), axis=1) - blocks_before, 0, MOE_STEP_BLOCKS)
    step_halves = jnp.where(step_start < pad_end[-1], step_halves, 0).astype(jnp.int32)
    n_used = (pad_end[-1] // R).astype(jnp.int32).reshape(1)
    later = jnp.where((ids[None, :] > ids[:, None]) & (counts[None, :] > 0), ids[None, :], N_EXPERTS)
    first_later = jnp.min(later, axis=1)
    next_expert = jnp.where(first_later < N_EXPERTS, first_later, ids).astype(jnp.int32)
    return gate, dest, step_expert, step_halves, n_used, next_expert


def kernel(x, mem, w_in, b_in, gla_wg2, gla_bg, gla_norm_g, sgu_ln_g, sgu_ln_b, sgu_ws, sgu_bs, pool_w, pool_scale, w_up_a, w_up_b, w_up_c, w_o, ln1_g, ln1_b, xa_wq, xa_wk, xa_wv, xa_wo, ln2_g, ln2_b, router_w, router_b, exp_w_gu, exp_b_gu, exp_w_down, exp_b_down, ln3_g, ln3_b):
    B, S, D = x.shape
    N = B * S
    p = _prep(w_in, b_in, gla_wg2, gla_bg, gla_norm_g, sgu_ln_g, sgu_ln_b, sgu_ws, sgu_bs, pool_w, pool_scale,
              w_up_a, w_up_b, w_up_c, w_o, ln1_g, ln1_b, xa_wq, xa_wk, xa_wv, xa_wo, ln2_g, ln2_b,
              router_w, router_b, ln3_g, ln3_b)
    memt = jnp.swapaxes(mem, 1, 2)
    for l in range(DEPTH):
        x1 = _mixer(x, p, l)
        kt, v = _memkv(mem, memt, p, l)
        x2, x2p, route, counts = _xattn(x1, kt, v, p, l)
        gate, dest, step_expert, step_halves, n_used, next_expert = _route(route, counts)
        n_slots = N * TOP_K + N_EXPERTS * MOE_STEP_BLOCKS * MOE_BLOCK
        xs = _sc_scatter_rows(x2p.reshape(N, D // 2), dest, n_slots)
        ys = _experts(step_expert, step_halves, n_used, next_expert, xs, exp_w_gu, exp_b_gu, exp_w_down, exp_b_down, l)
        dest_km = dest.reshape(TOP_K, N // CMB_TILE, CMB_TILE).transpose(1, 0, 2).reshape(-1)
        rows = dest_km.shape[0] // COMBINE_PARTS
        acc = None
        for part in range(COMBINE_PARTS):
            yg = _sc_gather_rows(ys, dest_km[part * rows:(part + 1) * rows])
            acc = _combine(x2.reshape(N, D), yg, gate, p, l, part, acc)
        x = acc.reshape(B, S, D)
    return x
```

```python
import functools

import jax
import jax.numpy as jnp
from jax import lax
from jax.experimental import pallas as pl
from jax.experimental.pallas import tpu as pltpu
from jax.experimental.pallas import tpu_sc as plsc

F32 = jnp.float32
BF16 = jnp.bfloat16

D_MODEL = 1024
DEPTH = 2
GLA_HEADS = 4
GLA_KEY = 256
GLA_VAL = 512
GLA_DK = 64
GLA_DV = 128
GLA_RANK = 16
GLA_TAU = 16.0
GLA_CHUNK = 64
SGU_GROUPS = 4
SGU_WIDTH = 256
SGU_GD = 64
SGU_CHUNK = 128
POOL_WINDOWS = (2, 4, 8, 16)
POOL_WIDTH = 256
POOL_GD = 64
POOL_CARRY = 32
MEM_LEN = 256
XA_HEADS = 4
XA_DH = 256
N_EXPERTS = 32
TOP_K = 4
EXPERT_FF = 1024
SWIGLU_LIMIT = 7.0
SWIGLU_ALPHA = 1.702
DEEPNORM_ALPHA = (2 * DEPTH) ** 0.25
LN_EPS = 1e-5
LANES = 128
VMEM_LIMIT = 56 * 1024 * 1024

MIX_TILE = 512
GLA_SUB = 256
XA_TILE = 1024
MOE_BLOCK = 256
MOE_STEP_BLOCKS = 4
CMB_TILE = 512
COMBINE_PARTS = 4
ROUTE_ROWS = 16
SC_GATHER_WINDOW = 128

O_QKVR = 0
O_GLOW = 2 * GLA_KEY + 2 * GLA_VAL
O_UV = O_GLOW + LANES
O_XC = O_UV + 2 * SGU_WIDTH
O_GATE = O_XC + POOL_WIDTH
N_PACK = O_GATE + 3 * D_MODEL


def _dot(a, b):
    return jnp.dot(a, b, preferred_element_type=F32)


def _dot_t0(a, b):
    return lax.dot_general(a, b, (((0,), (0,)), ((), ())), preferred_element_type=F32)


def _dot_t1(a, b):
    return lax.dot_general(a, b, (((1,), (1,)), ((), ())), preferred_element_type=F32)


def _split_bf16(x):
    hi = x.astype(BF16)
    lo = (x - hi.astype(F32)).astype(BF16)
    return hi, lo


def _layer_norm(x, g, b):
    mu = jnp.mean(x, axis=-1, keepdims=True)
    xc = x - mu
    var = jnp.mean(xc * xc, axis=-1, keepdims=True)
    return xc * lax.rsqrt(var + LN_EPS) * g + b


def _sigmoid(x):
    return 1.0 / (1.0 + jnp.exp(-x))


def _pack_bf16_pairs(x):
    H = x.shape[1] // 2
    bits = lax.bitcast_convert_type(x.astype(BF16).astype(F32), jnp.uint32)
    return (bits[:, :H] >> 16) | (bits[:, H:] & jnp.uint32(0xFFFF0000))


def _unpack_bf16_pairs(w):
    lo = lax.bitcast_convert_type(w << 16, F32)
    hi = lax.bitcast_convert_type(w & jnp.uint32(0xFFFF0000), F32)
    return jnp.concatenate([lo, hi], axis=1)


def _layer_spec(arr, l):
    nd = arr.ndim - 1
    return pl.BlockSpec((None,) + arr.shape[1:], lambda *_: (l,) + (0,) * nd, pipeline_mode=pl.Buffered(1))


def _mixer_kernel(x_ref, whead_ref, wglow_ref, wtail_ref, bcat_ref, wg2_ref, bg_ref, gnorm_ref,
                  slng_ref, slnb_ref, wtril_ref, sbias_ref, poolw_ref, pscale_ref,
                  wupa_ref, wupb_ref, wupc_ref, wo_ref, ln1g_ref, ln1b_ref,
                  out_ref,
                  state_ref, qkvr_ref, ya_ref, vln_ref, e_ref, s2_ref, s4_ref, s8_ref):
    T = MIX_TILE
    D = D_MODEL
    j = pl.program_id(1)
    x = x_ref[0]
    xb = x.astype(BF16)

    def proj(lo, hi):
        if hi <= O_GLOW:
            w = whead_ref[:, lo:hi]
        elif lo == O_GLOW and hi == O_UV:
            w = wglow_ref[...]
        else:
            w = wtail_ref[:, lo - O_UV:hi - O_UV]
        return _dot(xb, w) + bcat_ref[:, lo:hi]

    @pl.when(j == 0)
    def _():
        state_ref[...] = jnp.zeros_like(state_ref)
        e_ref[0:POOL_CARRY, :] = jnp.zeros((POOL_CARRY, POOL_WIDTH), F32)

    @pl.when(j > 0)
    def _():
        e_ref[0:POOL_CARRY, :] = e_ref[T:T + POOL_CARRY, :]

    qkvr_ref[...] = proj(O_QKVR, O_GLOW)
    glow = proj(O_GLOW, O_UV)
    z = _dot(glow.astype(BF16), wg2_ref[...]) + bg_ref[...]
    la = (jnp.minimum(z, 0.0) - jnp.log1p(jnp.exp(-jnp.abs(z)))) * (1.0 / GLA_TAU)
    la_hi, la_lo = _split_bf16(la)
    gate_a = _sigmoid(proj(O_GATE, O_GATE + D))

    C = GLA_CHUNK
    G = GLA_SUB
    NC = G // C
    CSH = C.bit_length() - 1
    row = lax.broadcasted_iota(jnp.int32, (G, G), 0)
    col = lax.broadcasted_iota(jnp.int32, (G, G), 1)
    same_chunk = (row >> CSH) == (col >> CSH)
    causal = same_chunk & (row >= col)
    causal_bf = jnp.where(causal, 1.0, 0.0).astype(BF16)
    lane = lax.broadcasted_iota(jnp.int32, (G, GLA_KEY), 1)
    gate_b = None
    for g0 in range(0, T, G):
        gr = slice(g0, g0 + G)
        lh, ll = la_hi[gr], la_lo[gr]
        b = _dot(causal_bf, lh) + _dot(causal_bf, ll)
        b_last = [b[(c + 1) * C - 1:(c + 1) * C, :] for c in range(NC)]
        b_end = jnp.concatenate([jnp.broadcast_to(r, (C, GLA_KEY)) for r in b_last], axis=0)
        q = qkvr_ref[gr, 0:GLA_KEY]
        k = qkvr_ref[gr, GLA_KEY:2 * GLA_KEY]
        v = qkvr_ref[gr, 2 * GLA_KEY:2 * GLA_KEY + GLA_VAL].astype(BF16)
        q_dec = q * (GLA_DK ** -0.5) * jnp.exp(b)
        k_dec = (k * jnp.exp(-b)).astype(BF16)
        k_tail = (k * jnp.exp(b_end - b)).astype(BF16)
        q_dec_bf = q_dec.astype(BF16)
        o_heads = []
        for h in range(GLA_HEADS):
            q_h = jnp.where((lane >= h * GLA_DK) & (lane < (h + 1) * GLA_DK), q_dec, 0.0).astype(BF16)
            scores = jnp.where(causal, _dot_t1(q_h, k_dec), 0.0).astype(BF16)
            o_heads.append(_dot(scores, v[:, h * GLA_DV:(h + 1) * GLA_DV]))
        o_intra = jnp.concatenate(o_heads, axis=1)
        if gate_b is None:
            gate_b = _sigmoid(proj(O_GATE + D, O_GATE + 2 * D))
        dec_cols = jnp.exp(jnp.concatenate(b_last + [jnp.zeros((8 - NC, GLA_KEY), F32)], axis=0)).T
        dec_all = jnp.concatenate([jnp.broadcast_to(dec_cols[:, c:c + 1], (GLA_KEY, LANES)) for c in range(NC)], axis=1)
        o_inter = []
        for c in range(NC):
            rows = slice(c * C, (c + 1) * C)
            o_inter.append(_dot(q_dec_bf[rows], state_ref[...].astype(BF16)))
            kv = _dot_t0(k_tail[rows], v[rows])
            for h in range(GLA_HEADS):
                rs = slice(h * GLA_DK, (h + 1) * GLA_DK)
                cs = slice(h * GLA_DV, (h + 1) * GLA_DV)
                state_ref[rs, cs] = dec_all[rs, c * LANES:(c + 1) * LANES] * state_ref[rs, cs] + kv[rs, cs]
        o = o_intra + jnp.concatenate(o_inter, axis=0)
        for h in range(GLA_HEADS):
            cs = slice(h * GLA_DV, (h + 1) * GLA_DV)
            o_h = o[:, cs]
            ms = jnp.mean(o_h * o_h, axis=-1, keepdims=True)
            o_h = o_h * lax.rsqrt(ms + LN_EPS) * gnorm_ref[:, cs]
            r_h = qkvr_ref[gr, 2 * GLA_KEY + GLA_VAL + h * GLA_DV:2 * GLA_KEY + GLA_VAL + (h + 1) * GLA_DV]
            ya_ref[gr, cs] = (o_h * (r_h * _sigmoid(r_h))).astype(BF16)

    gate_c = _sigmoid(proj(O_GATE + 2 * D, O_GATE + 3 * D))
    uv = proj(O_UV, O_XC)
    zg = 0.5 * uv * (1.0 + lax.erf(uv * (2.0 ** -0.5)))
    u = zg[:, :SGU_WIDTH]
    vln_ref[...] = _layer_norm(zg[:, SGU_WIDTH:], slng_ref[...], slnb_ref[...])
    lane_s = lax.broadcasted_iota(jnp.int32, (SGU_CHUNK, SGU_WIDTH), 1)
    s_parts = []
    for n in range(T // SGU_CHUNK):
        vc = vln_ref[n * SGU_CHUNK:(n + 1) * SGU_CHUNK, :]
        s = sbias_ref[...]
        for g in range(SGU_GROUPS):
            vg = jnp.where((lane_s >= g * SGU_GD) & (lane_s < (g + 1) * SGU_GD), vc, 0.0).astype(BF16)
            s = s + _dot(wtril_ref[g], vg)
        s_parts.append(s)
    y_b = (u * jnp.concatenate(s_parts, axis=0)).astype(BF16)

    P = POOL_CARRY
    xc = proj(O_XC, O_GATE)
    e_ref[P:P + T, :] = xc
    s2_ref[8:P + T, :] = e_ref[8:P + T, :] + e_ref[7:P + T - 1, :]
    s4_ref[16:P + T, :] = s2_ref[16:P + T, :] + s2_ref[14:P + T - 2, :]
    s8_ref[24:P + T, :] = s4_ref[24:P + T, :] + s4_ref[20:P + T - 4, :]
    s16 = s8_ref[P:P + T, :] + s8_ref[P - 8:P + T - 8, :]
    lane_p = lax.broadcasted_iota(jnp.int32, (T, POOL_WIDTH), 1)
    tpos = lax.broadcasted_iota(jnp.int32, (T, POOL_WIDTH), 0) + (j * T + 1)
    grp = lane_p >> 6
    win = jnp.where(grp == 0, POOL_WINDOWS[0], jnp.where(grp == 1, POOL_WINDOWS[1],
                    jnp.where(grp == 2, POOL_WINDOWS[2], POOL_WINDOWS[3])))
    wsum = jnp.where(grp == 0, s2_ref[P:P + T, :], jnp.where(grp == 1, s4_ref[P:P + T, :],
                     jnp.where(grp == 2, s8_ref[P:P + T, :], s16)))
    count = jnp.minimum(tpos, win).astype(F32)
    pooled = wsum / count - xc
    y_c = (_dot(pooled.astype(BF16), poolw_ref[...]) * pscale_ref[...]).astype(BF16)

    merged = gate_a * _dot(ya_ref[...], wupa_ref[...].astype(BF16))
    merged += gate_b * _dot(y_b, wupb_ref[...].astype(BF16))
    merged += gate_c * _dot(y_c, wupc_ref[...].astype(BF16))
    h = _dot(merged.astype(BF16), wo_ref[...].astype(BF16))
    out_ref[0] = _layer_norm(DEEPNORM_ALPHA * x + h, ln1g_ref[...], ln1b_ref[...])


_MIXER_WEIGHTS = ('whead', 'wglow', 'wtail', 'bcat', 'wg2', 'bg', 'gnorm', 'slng', 'slnb', 'wtril', 'sbias', 'poolw', 'pscale',
                  'wupa', 'wupb', 'wupc', 'wo', 'ln1g', 'ln1b')


def _mixer(x, p, l):
    B, S, D = x.shape
    T = MIX_TILE
    weights = [p[n] for n in _MIXER_WEIGHTS]
    return pl.pallas_call(
        _mixer_kernel,
        out_shape=jax.ShapeDtypeStruct((B, S, D), F32),
        grid=(B, S // T),
        in_specs=[pl.BlockSpec((1, T, D), lambda b, j: (b, j, 0))] + [_layer_spec(w, l) for w in weights],
        out_specs=pl.BlockSpec((1, T, D), lambda b, j: (b, j, 0)),
        scratch_shapes=[
            pltpu.VMEM((GLA_KEY, GLA_VAL), F32),
            pltpu.VMEM((T, 2 * GLA_KEY + 2 * GLA_VAL), F32),
            pltpu.VMEM((T, GLA_VAL), BF16),
            pltpu.VMEM((T, SGU_WIDTH), F32),
            pltpu.VMEM((T + POOL_CARRY, POOL_WIDTH), F32),
            pltpu.VMEM((T + POOL_CARRY, POOL_WIDTH), F32),
            pltpu.VMEM((T + POOL_CARRY, POOL_WIDTH), F32),
            pltpu.VMEM((T + POOL_CARRY, POOL_WIDTH), F32),
        ],
        compiler_params=pltpu.CompilerParams(dimension_semantics=("arbitrary", "arbitrary"),
                                             vmem_limit_bytes=VMEM_LIMIT),
        name="mixer",
    )(x, *weights)


def _memkv_kernel(memt_ref, mem_ref, wk_ref, wv_ref, kt_ref, v_ref):
    kt_ref[0] = _dot_t0(wk_ref[...].astype(BF16), memt_ref[0].astype(BF16)).astype(BF16)
    v_ref[0] = _dot(mem_ref[0].astype(BF16), wv_ref[...].astype(BF16)).astype(BF16)


def _memkv(mem, memt, p, l):
    B, M, D = mem.shape
    return pl.pallas_call(
        _memkv_kernel,
        out_shape=(jax.ShapeDtypeStruct((B, D, M), BF16), jax.ShapeDtypeStruct((B, M, D), BF16)),
        grid=(B,),
        in_specs=[pl.BlockSpec((1, D, M), lambda b: (b, 0, 0)), pl.BlockSpec((1, M, D), lambda b: (b, 0, 0)),
                  _layer_spec(p['wk'], l), _layer_spec(p['wv'], l)],
        out_specs=(pl.BlockSpec((1, D, M), lambda b: (b, 0, 0)), pl.BlockSpec((1, M, D), lambda b: (b, 0, 0))),
        compiler_params=pltpu.CompilerParams(dimension_semantics=("arbitrary",), vmem_limit_bytes=VMEM_LIMIT),
        name="memkv",
    )(memt, mem, p['wk'], p['wv'])


def _xattn_kernel(x_ref, kt_ref, v_ref, wq_ref, wo_ref, ln2g_ref, ln2b_ref, rwt_ref, rbt_ref,
                  x2_ref, x2p_ref, route_ref, counts_ref, carry_ref):
    T = XA_TILE

    @pl.when((pl.program_id(0) == 0) & (pl.program_id(1) == 0))
    def _():
        carry_ref[...] = jnp.zeros_like(carry_ref)

    x = x_ref[0]
    q = (_dot(x.astype(BF16), wq_ref[...].astype(BF16)) * (XA_DH ** -0.5)).astype(BF16)
    h = jnp.zeros_like(x)
    for hd in range(XA_HEADS):
        cs = slice(hd * XA_DH, (hd + 1) * XA_DH)
        s = _dot(q[:, cs], kt_ref[0, cs, :])
        e = jnp.exp(s - jnp.max(s, axis=-1, keepdims=True))
        o = _dot(e.astype(BF16), v_ref[0, :, cs]) / jnp.sum(e, axis=-1, keepdims=True)
        h = h + _dot(o.astype(BF16), wo_ref[cs, :].astype(BF16))
    x2 = _layer_norm(DEEPNORM_ALPHA * x + h, ln2g_ref[...], ln2b_ref[...])
    x2_ref[0] = x2
    x2p_ref[0] = _pack_bf16_pairs(x2)

    E = N_EXPERTS
    hi, lo = _split_bf16(x2)
    lt = _dot_t1(rwt_ref[...], hi)
    logits = lt[0:E] + (lt[E:2 * E] + _dot_t1(rwt_ref[0:E, :], lo)) + rbt_ref[...]

    eid = lax.broadcasted_iota(jnp.int32, (E, T), 0)
    neg_inf = jnp.float32(-jnp.inf)
    rest = logits
    tops, picks = [], []
    for _ in range(TOP_K):
        m = jnp.max(rest, axis=0, keepdims=True)
        idx = jnp.min(jnp.where(rest == m, eid, E), axis=0, keepdims=True)
        pick = eid == idx
        rest = jnp.where(pick, neg_inf, rest)
        tops.append((m, idx))
        picks.append(pick)
    exps = [jnp.exp(m - tops[0][0]) for m, _ in tops]
    denom = exps[0]
    for e in exps[1:]:
        denom = denom + e

    chosen = jnp.zeros((E, T), F32)
    for pick in picks:
        chosen = chosen + jnp.where(pick, 1.0, 0.0)
    chosen_bf = chosen.astype(BF16)
    earlier = (lax.broadcasted_iota(jnp.int32, (T, T), 0) < lax.broadcasted_iota(jnp.int32, (T, T), 1))
    carry = carry_ref[...]
    before = _dot(chosen_bf, jnp.where(earlier, 1.0, 0.0).astype(BF16)) + jnp.concatenate([carry] * (T // LANES), axis=1)
    carry = carry + _dot(chosen_bf, jnp.ones((T, LANES), BF16))
    carry_ref[...] = carry
    counts_ref[...] = carry

    rid = lax.broadcasted_iota(jnp.int32, (ROUTE_ROWS, T), 0)
    route = jnp.zeros((ROUTE_ROWS, T), F32)
    for k in range(TOP_K):
        rank = jnp.sum(jnp.where(picks[k], before, 0.0), axis=0, keepdims=True)
        route = jnp.where(rid == k, tops[k][1].astype(F32), route)
        route = jnp.where(rid == TOP_K + k, exps[k] / denom, route)
        route = jnp.where(rid == 2 * TOP_K + k, rank, route)
    route_ref[...] = route


def _xattn(x, kt, v, p, l):
    B, S, D = x.shape
    T = XA_TILE
    M = MEM_LEN
    weights = [p[n] for n in ('wq', 'wxo', 'ln2g', 'ln2b', 'rwt', 'rbt')]
    return pl.pallas_call(
        _xattn_kernel,
        out_shape=(jax.ShapeDtypeStruct((B, S, D), F32), jax.ShapeDtypeStruct((B, S, D // 2), jnp.uint32),
                   jax.ShapeDtypeStruct((ROUTE_ROWS, B * S), F32), jax.ShapeDtypeStruct((N_EXPERTS, LANES), F32)),
        grid=(B, S // T),
        in_specs=[pl.BlockSpec((1, T, D), lambda b, j: (b, j, 0)),
                  pl.BlockSpec((1, D, M), lambda b, j: (b, 0, 0)),
                  pl.BlockSpec((1, M, D), lambda b, j: (b, 0, 0))] + [_layer_spec(w, l) for w in weights],
        out_specs=(pl.BlockSpec((1, T, D), lambda b, j: (b, j, 0)),
                   pl.BlockSpec((1, T, D // 2), lambda b, j: (b, j, 0)),
                   pl.BlockSpec((ROUTE_ROWS, T), lambda b, j: (0, b * (S // T) + j)),
                   pl.BlockSpec((N_EXPERTS, LANES), lambda b, j: (0, 0))),
        scratch_shapes=[pltpu.VMEM((N_EXPERTS, LANES), F32)],
        compiler_params=pltpu.CompilerParams(dimension_semantics=("arbitrary", "arbitrary"),
                                             vmem_limit_bytes=VMEM_LIMIT),
        name="xattn",
    )(x, kt, v, *weights)


def _expert_kernel(layer, se_ref, sh_ref, nu_ref, nxt_ref, xs_ref, wgu_hbm, bgu_ref, wd_hbm, bd_ref, out_ref,
                   wgu_st, wd_st, slot_ref, sem):
    i = pl.program_id(0)
    F = EXPERT_FF
    halves = sh_ref[i]

    def weight_copies(e, slot):
        return (pltpu.make_async_copy(wgu_hbm.at[layer, e], wgu_st.at[slot], sem.at[slot, 0]),
                pltpu.make_async_copy(wd_hbm.at[layer, e], wd_st.at[slot], sem.at[slot, 1]))

    def ffn(rows):
        slot = slot_ref[0]
        xsb = _unpack_bf16_pairs(xs_ref[rows, :]).astype(BF16)
        hh = _dot(xsb, wgu_st[slot].astype(BF16)) + bgu_ref[...]
        h_glu = jnp.minimum(hh[:, :F], SWIGLU_LIMIT)
        h_lin = jnp.clip(hh[:, F:], -SWIGLU_LIMIT, SWIGLU_LIMIT)
        a = h_glu * _sigmoid(SWIGLU_ALPHA * h_glu) * (h_lin + 1.0)
        out_ref[rows, :] = _pack_bf16_pairs(_dot(a.astype(BF16), wd_st[slot].astype(BF16)) + bd_ref[...])

    @pl.when(halves > 0)
    def _():
        e = se_ref[i]
        prev = se_ref[jnp.maximum(i - 1, 0)]

        @pl.when(i == 0)
        def _():
            slot_ref[0] = 1
            for cp in weight_copies(e, 0):
                cp.start()

        @pl.when((i == 0) | (e != prev))
        def _():
            slot = 1 - slot_ref[0]
            slot_ref[0] = slot
            for cp in weight_copies(e, slot):
                cp.wait()
            nxt = nxt_ref[e]

            @pl.when(nxt != e)
            def _():
                for cp in weight_copies(nxt, 1 - slot):
                    cp.start()

    for n in range(1, MOE_STEP_BLOCKS + 1):
        @pl.when(halves == n)
        def _(n=n):
            ffn(slice(0, n * MOE_BLOCK))


def _experts(step_expert, step_halves, n_used, next_expert, xs, w_gu, b_gu, w_down, b_down, l):
    P, DH = xs.shape
    D = 2 * DH
    R = MOE_STEP_BLOCKS * MOE_BLOCK
    NS = P // R
    F2 = 2 * EXPERT_FF

    def row_map(i, se, sh, nu, nxt):
        return (jnp.minimum(i, nu[0] - 1), 0)

    def exp_map(i, se, sh, nu, nxt):
        return (l, se[jnp.minimum(i, nu[0] - 1)], 0, 0)

    grid_spec = pltpu.PrefetchScalarGridSpec(
        num_scalar_prefetch=4,
        grid=(NS,),
        in_specs=[pl.BlockSpec((R, DH), row_map),
                  pl.BlockSpec(memory_space=pl.ANY),
                  pl.BlockSpec((None, None, 1, F2), exp_map),
                  pl.BlockSpec(memory_space=pl.ANY),
                  pl.BlockSpec((None, None, 1, D), exp_map)],
        out_specs=pl.BlockSpec((R, DH), row_map),
        scratch_shapes=[pltpu.VMEM((2, D, F2), F32), pltpu.VMEM((2, EXPERT_FF, D), F32),
                        pltpu.SMEM((1,), jnp.int32),
                        pltpu.SemaphoreType.DMA((2, 2))],
    )
    return pl.pallas_call(
        functools.partial(_expert_kernel, l),
        out_shape=jax.ShapeDtypeStruct((P, DH), jnp.uint32),
        grid_spec=grid_spec,
        compiler_params=pltpu.CompilerParams(dimension_semantics=("arbitrary",), vmem_limit_bytes=VMEM_LIMIT),
        name="experts",
    )(step_expert, step_halves, n_used, next_expert, xs, w_gu, b_gu.reshape(DEPTH, N_EXPERTS, 1, F2), w_down,
      b_down.reshape(DEPTH, N_EXPERTS, 1, D))


def _sc_gather_rows(x, idx):
    M = idx.shape[0]
    D = x.shape[1]
    W = SC_GATHER_WINDOW
    mesh = plsc.VectorSubcoreMesh(core_axis_name="core", subcore_axis_name="subcore")
    n_workers = mesh.num_cores * mesh.num_subcores
    rows_per = M // n_workers
    assert rows_per * n_workers == M and rows_per % W == 0

    @pl.kernel(out_type=jax.ShapeDtypeStruct((M, D), x.dtype), mesh=mesh, name="sc_gather_rows",
               scratch_types=[pltpu.VMEM((rows_per,), jnp.int32), pltpu.VMEM((W, D), x.dtype)])
    def gather_kernel(x_hbm, i_hbm, o_hbm, idx_vmem, buf):
        wid = lax.axis_index("core") * mesh.num_subcores + lax.axis_index("subcore")
        base = wid * rows_per
        pltpu.sync_copy(i_hbm.at[pl.ds(base, rows_per)], idx_vmem)

        @pl.loop(0, rows_per // W)
        def _(j):
            pltpu.sync_copy(x_hbm.at[idx_vmem.at[pl.ds(j * W, W)]], buf)
            pltpu.sync_copy(buf, o_hbm.at[pl.ds(base + j * W, W)])

    return gather_kernel(x, idx)


def _sc_scatter_rows(x, idx, n_out):
    K, N = idx.shape
    D = x.shape[1]
    W = SC_GATHER_WINDOW
    mesh = plsc.VectorSubcoreMesh(core_axis_name="core", subcore_axis_name="subcore")
    n_workers = mesh.num_cores * mesh.num_subcores
    rows_per = N // n_workers
    assert rows_per * n_workers == N and rows_per % W == 0

    @pl.kernel(out_type=jax.ShapeDtypeStruct((n_out, D), x.dtype), mesh=mesh, name="sc_scatter_rows",
               scratch_types=[pltpu.VMEM((K * rows_per,), jnp.int32), pltpu.VMEM((W, D), x.dtype),
                              pltpu.SemaphoreType.DMA((K,))])
    def scatter_kernel(x_hbm, i_hbm, o_hbm, idx_vmem, buf, sem):
        wid = lax.axis_index("core") * mesh.num_subcores + lax.axis_index("subcore")
        base = wid * rows_per
        for k in range(K):
            pltpu.sync_copy(i_hbm.at[pl.ds(k * N + base, rows_per)], idx_vmem.at[pl.ds(k * rows_per, rows_per)])

        @pl.loop(0, rows_per // W)
        def _(j):
            pltpu.sync_copy(x_hbm.at[pl.ds(base + j * W, W)], buf)
            copies = [pltpu.make_async_copy(buf, o_hbm.at[idx_vmem.at[pl.ds(k * rows_per + j * W, W)]], sem.at[k])
                      for k in range(K)]
            for cp in copies:
                cp.start()
            for cp in copies:
                cp.wait()

    return scatter_kernel(x, idx.reshape(K * N))


def _combine_kernel(x_ref, yg_ref, gate_ref, ln3g_ref, ln3b_ref, *rest):
    out_ref = rest[-1]
    x = x_ref[...]
    g = gate_ref[...]
    y = jnp.zeros_like(x)
    for k in range(TOP_K):
        y = y + g[:, k:k + 1] * _unpack_bf16_pairs(yg_ref[k * CMB_TILE:(k + 1) * CMB_TILE, :])
    out_ref[...] = _layer_norm(DEEPNORM_ALPHA * x + y, ln3g_ref[...], ln3b_ref[...])


def _combine(x2, yg, gate, p, l, part, acc):
    N, D = x2.shape
    T = CMB_TILE
    tiles = yg.shape[0] // (TOP_K * T)
    first = part * tiles
    in_specs = [pl.BlockSpec((T, D), lambda i: (first + i, 0)), pl.BlockSpec((TOP_K * T, D // 2), lambda i: (i, 0)),
                pl.BlockSpec((T, TOP_K), lambda i: (first + i, 0)), _layer_spec(p['ln3g'], l), _layer_spec(p['ln3b'], l)]
    args = [x2, yg, gate, p['ln3g'], p['ln3b']]
    aliases = {}
    if acc is not None:
        in_specs.append(pl.BlockSpec(memory_space=pl.ANY))
        args.append(acc)
        aliases = {len(args) - 1: 0}
    return pl.pallas_call(
        _combine_kernel,
        out_shape=jax.ShapeDtypeStruct((N, D), F32),
        grid=(tiles,),
        in_specs=in_specs,
        out_specs=pl.BlockSpec((T, D), lambda i: (first + i, 0)),
        input_output_aliases=aliases,
        compiler_params=pltpu.CompilerParams(dimension_semantics=("arbitrary",), vmem_limit_bytes=VMEM_LIMIT),
        name="combine",
    )(*args)


def _prep(w_in, b_in, gla_wg2, gla_bg, gla_norm_g, sgu_ln_g, sgu_ln_b, sgu_ws, sgu_bs, pool_w, pool_scale,
          w_up_a, w_up_b, w_up_c, w_o, ln1_g, ln1_b, xa_wq, xa_wk, xa_wv, xa_wo, ln2_g, ln2_b,
          router_w, router_b, ln3_g, ln3_b):
    L = w_in.shape[0]
    o_glow = O_GLOW
    o_uv = o_glow + GLA_RANK
    row = lambda a: a.reshape(L, 1, -1).astype(F32)
    pad_last = lambda a, n: jnp.pad(a, [(0, 0)] * (a.ndim - 1) + [(0, n - a.shape[-1])])
    p = {}
    p['whead'] = w_in[..., :o_glow].astype(BF16)
    p['wglow'] = pad_last(w_in[..., o_glow:o_uv], LANES).astype(BF16)
    p['wtail'] = w_in[..., o_uv:].astype(BF16)
    p['bcat'] = row(jnp.concatenate([b_in[..., :o_glow], pad_last(b_in[..., o_glow:o_uv], LANES), b_in[..., o_uv:]],
                                    axis=-1))
    p['wg2'] = jnp.pad(gla_wg2, ((0, 0), (0, LANES - GLA_RANK), (0, 0))).astype(BF16)
    p['bg'] = row(gla_bg)
    p['gnorm'] = row(gla_norm_g)
    p['slng'] = row(sgu_ln_g)
    p['slnb'] = row(sgu_ln_b)
    p['wtril'] = jnp.tril(sgu_ws).astype(BF16)
    p['sbias'] = jnp.repeat(jnp.swapaxes(sgu_bs, 1, 2), SGU_GD, axis=2).astype(F32)
    G = len(POOL_WINDOWS)
    eye = jnp.eye(G, dtype=F32)
    p['poolw'] = jnp.einsum('lgcd,gh->lgchd', pool_w, eye).reshape(L, POOL_WIDTH, POOL_WIDTH).astype(BF16)
    p['pscale'] = row(pool_scale)
    p['wupa'], p['wupb'], p['wupc'], p['wo'] = w_up_a, w_up_b, w_up_c, w_o
    p['ln1g'], p['ln1b'] = row(ln1_g), row(ln1_b)
    p['wq'], p['wk'], p['wv'], p['wxo'] = xa_wq, xa_wk, xa_wv, xa_wo
    p['ln2g'], p['ln2b'] = row(ln2_g), row(ln2_b)
    rwt = jnp.swapaxes(router_w, 1, 2)
    rwt_hi = rwt.astype(BF16)
    p['rwt'] = jnp.concatenate([rwt_hi, (rwt - rwt_hi.astype(F32)).astype(BF16)], axis=1)
    p['rbt'] = jnp.broadcast_to(router_b[:, :, None], router_b.shape + (XA_TILE,)).astype(F32)
    p['ln3g'], p['ln3b'] = row(ln3_g), row(ln3_b)
    return p


def _route(route, counts):
    N = route.shape[1]
    top_idx = route[0:TOP_K].astype(jnp.int32)
    gate = route[TOP_K:2 * TOP_K].T
    rank = route[2 * TOP_K:3 * TOP_K].astype(jnp.int32)
    counts = counts[:, 0].astype(jnp.int32)
    R = MOE_STEP_BLOCKS * MOE_BLOCK
    blocks = (counts + MOE_BLOCK - 1) // MOE_BLOCK
    padded = ((counts + R - 1) // R) * R
    pad_end = jnp.cumsum(padded)
    pad_start = pad_end - padded
    ids = jnp.arange(N_EXPERTS, dtype=jnp.int32)
    start_of = jnp.sum(jnp.where(top_idx[:, :, None] == ids[None, None, :], pad_start[None, None, :], 0), axis=-1)
    dest = start_of + rank
    n_steps = N * TOP_K // R + N_EXPERTS
    step_start = jnp.arange(n_steps, dtype=jnp.int32) * R
    step_expert = jnp.minimum(jnp.sum((pad_end[None, :] <= step_start[:, None]).astype(jnp.int32), axis=1),
                              N_EXPERTS - 1)
    mine = step_expert[:, None] == ids[None, :]
    blocks_before = (step_start - jnp.sum(jnp.where(mine, pad_start[None, :], 0), axis=1)) // MOE_BLOCK
    step_halves = jnp.clip(jnp.sum(jnp.where(mine, blocks[None, :], 0), axis=1) - blocks_before, 0, MOE_STEP_BLOCKS)
    step_halves = jnp.where(step_start < pad_end[-1], step_halves, 0).astype(jnp.int32)
    n_used = (pad_end[-1] // R).astype(jnp.int32).reshape(1)
    later = jnp.where((ids[None, :] > ids[:, None]) & (counts[None, :] > 0), ids[None, :], N_EXPERTS)
    first_later = jnp.min(later, axis=1)
    next_expert = jnp.where(first_later < N_EXPERTS, first_later, ids).astype(jnp.int32)
    return gate, dest, step_expert, step_halves, n_used, next_expert


def kernel(x, mem, w_in, b_in, gla_wg2, gla_bg, gla_norm_g, sgu_ln_g, sgu_ln_b, sgu_ws, sgu_bs, pool_w, pool_scale, w_up_a, w_up_b, w_up_c, w_o, ln1_g, ln1_b, xa_wq, xa_wk, xa_wv, xa_wo, ln2_g, ln2_b, router_w, router_b, exp_w_gu, exp_b_gu, exp_w_down, exp_b_down, ln3_g, ln3_b):
    B, S, D = x.shape
    N = B * S
    p = _prep(w_in, b_in, gla_wg2, gla_bg, gla_norm_g, sgu_ln_g, sgu_ln_b, sgu_ws, sgu_bs, pool_w, pool_scale,
              w_up_a, w_up_b, w_up_c, w_o, ln1_g, ln1_b, xa_wq, xa_wk, xa_wv, xa_wo, ln2_g, ln2_b,
              router_w, router_b, ln3_g, ln3_b)
    memt = jnp.swapaxes(mem, 1, 2)
    for l in range(DEPTH):
        x1 = _mixer(x, p, l)
        kt, v = _memkv(mem, memt, p, l)
        x2, x2p, route, counts = _xattn(x1, kt, v, p, l)
        gate, dest, step_expert, step_halves, n_used, next_expert = _route(route, counts)
        n_slots = N * TOP_K + N_EXPERTS * MOE_STEP_BLOCKS * MOE_BLOCK
        xs = _sc_scatter_rows(x2p.reshape(N, D // 2), dest, n_slots)
        ys = _experts(step_expert, step_halves, n_used, next_expert, xs, exp_w_gu, exp_b_gu, exp_w_down, exp_b_down, l)
        dest_km = dest.reshape(TOP_K, N // CMB_TILE, CMB_TILE).transpose(1, 0, 2).reshape(-1)
        rows = dest_km.shape[0] // COMBINE_PARTS
        acc = None
        for part in range(COMBINE_PARTS):
            yg = _sc_gather_rows(ys, dest_km[part * rows:(part + 1) * rows])
            acc = _combine(x2.reshape(N, D), yg, gate, p, l, part, acc)
        x = acc.reshape(B, S, D)
    return x
```

```python
import functools

import jax
import jax.numpy as jnp
from jax import lax
from jax.experimental import pallas as pl
from jax.experimental.pallas import tpu as pltpu
from jax.experimental.pallas import tpu_sc as plsc

F32 = jnp.float32
BF16 = jnp.bfloat16

D_MODEL = 1024
DEPTH = 2
GLA_HEADS = 4
GLA_KEY = 256
GLA_VAL = 512
GLA_DK = 64
GLA_DV = 128
GLA_RANK = 16
GLA_TAU = 16.0
GLA_CHUNK = 64
SGU_GROUPS = 4
SGU_WIDTH = 256
SGU_GD = 64
SGU_CHUNK = 128
POOL_WINDOWS = (2, 4, 8, 16)
POOL_WIDTH = 256
POOL_GD = 64
POOL_CARRY = 32
MEM_LEN = 256
XA_HEADS = 4
XA_DH = 256
N_EXPERTS = 32
TOP_K = 4
EXPERT_FF = 1024
SWIGLU_LIMIT = 7.0
SWIGLU_ALPHA = 1.702
DEEPNORM_ALPHA = (2 * DEPTH) ** 0.25
LN_EPS = 1e-5
LANES = 128
VMEM_LIMIT = 56 * 1024 * 1024

MIX_TILE = 512
GLA_SUB = 256
XA_TILE = 1024
MOE_BLOCK = 256
MOE_STEP_BLOCKS = 4
CMB_TILE = 512
COMBINE_PARTS = 4
ROUTE_ROWS = 16
SC_GATHER_WINDOW = 128

O_QKVR = 0
O_GLOW = 2 * GLA_KEY + 2 * GLA_VAL
O_UV = O_GLOW + LANES
O_XC = O_UV + 2 * SGU_WIDTH
O_GATE = O_XC + POOL_WIDTH
N_PACK = O_GATE + 3 * D_MODEL


def _dot(a, b):
    return jnp.dot(a, b, preferred_element_type=F32)


def _dot_t0(a, b):
    return lax.dot_general(a, b, (((0,), (0,)), ((), ())), preferred_element_type=F32)


def _dot_t1(a, b):
    return lax.dot_general(a, b, (((1,), (1,)), ((), ())), preferred_element_type=F32)


def _split_bf16(x):
    hi = x.astype(BF16)
    lo = (x - hi.astype(F32)).astype(BF16)
    return hi, lo


def _layer_norm(x, g, b):
    mu = jnp.mean(x, axis=-1, keepdims=True)
    xc = x - mu
    var = jnp.mean(xc * xc, axis=-1, keepdims=True)
    return xc * lax.rsqrt(var + LN_EPS) * g + b


def _sigmoid(x):
    return 1.0 / (1.0 + jnp.exp(-x))


def _pack_bf16_pairs(x):
    H = x.shape[1] // 2
    bits = lax.bitcast_convert_type(x.astype(BF16).astype(F32), jnp.uint32)
    return (bits[:, :H] >> 16) | (bits[:, H:] & jnp.uint32(0xFFFF0000))


def _unpack_bf16_pairs(w):
    lo = lax.bitcast_convert_type(w << 16, F32)
    hi = lax.bitcast_convert_type(w & jnp.uint32(0xFFFF0000), F32)
    return jnp.concatenate([lo, hi], axis=1)


def _layer_spec(arr, l):
    nd = arr.ndim - 1
    return pl.BlockSpec((None,) + arr.shape[1:], lambda *_: (l,) + (0,) * nd, pipeline_mode=pl.Buffered(1))


def _mixer_kernel(x_ref, whead_ref, wglow_ref, wtail_ref, bcat_ref, wg2_ref, bg_ref, gnorm_ref,
                  slng_ref, slnb_ref, wtril_ref, sbias_ref, poolw_ref, pscale_ref,
                  wupa_ref, wupb_ref, wupc_ref, wo_ref, ln1g_ref, ln1b_ref,
                  out_ref,
                  state_ref, qkvr_ref, ya_ref, vln_ref, e_ref, s2_ref, s4_ref, s8_ref):
    T = MIX_TILE
    D = D_MODEL
    j = pl.program_id(1)
    x = x_ref[0]
    xb = x.astype(BF16)

    def proj(lo, hi):
        if hi <= O_GLOW:
            w = whead_ref[:, lo:hi]
        elif lo == O_GLOW and hi == O_UV:
            w = wglow_ref[...]
        else:
            w = wtail_ref[:, lo - O_UV:hi - O_UV]
        return _dot(xb, w) + bcat_ref[:, lo:hi]

    @pl.when(j == 0)
    def _():
        state_ref[...] = jnp.zeros_like(state_ref)
        e_ref[0:POOL_CARRY, :] = jnp.zeros((POOL_CARRY, POOL_WIDTH), F32)

    @pl.when(j > 0)
    def _():
        e_ref[0:POOL_CARRY, :] = e_ref[T:T + POOL_CARRY, :]

    qkvr_ref[...] = proj(O_QKVR, O_GLOW)
    glow = proj(O_GLOW, O_UV)
    z = _dot(glow.astype(BF16), wg2_ref[...]) + bg_ref[...]
    la = (jnp.minimum(z, 0.0) - jnp.log1p(jnp.exp(-jnp.abs(z)))) * (1.0 / GLA_TAU)
    la_hi, la_lo = _split_bf16(la)
    gate_a = _sigmoid(proj(O_GATE, O_GATE + D))

    C = GLA_CHUNK
    G = GLA_SUB
    NC = G // C
    CSH = C.bit_length() - 1
    row = lax.broadcasted_iota(jnp.int32, (G, G), 0)
    col = lax.broadcasted_iota(jnp.int32, (G, G), 1)
    same_chunk = (row >> CSH) == (col >> CSH)
    causal = same_chunk & (row >= col)
    causal_bf = jnp.where(causal, 1.0, 0.0).astype(BF16)
    lane = lax.broadcasted_iota(jnp.int32, (G, GLA_KEY), 1)
    gate_b = None
    for g0 in range(0, T, G):
        gr = slice(g0, g0 + G)
        lh, ll = la_hi[gr], la_lo[gr]
        b = _dot(causal_bf, lh) + _dot(causal_bf, ll)
        b_last = [b[(c + 1) * C - 1:(c + 1) * C, :] for c in range(NC)]
        b_end = jnp.concatenate([jnp.broadcast_to(r, (C, GLA_KEY)) for r in b_last], axis=0)
        q = qkvr_ref[gr, 0:GLA_KEY]
        k = qkvr_ref[gr, GLA_KEY:2 * GLA_KEY]
        v = qkvr_ref[gr, 2 * GLA_KEY:2 * GLA_KEY + GLA_VAL].astype(BF16)
        q_dec = q * (GLA_DK ** -0.5) * jnp.exp(b)
        k_dec = (k * jnp.exp(-b)).astype(BF16)
        k_tail = (k * jnp.exp(b_end - b)).astype(BF16)
        q_dec_bf = q_dec.astype(BF16)
        o_heads = []
        for h in range(GLA_HEADS):
            q_h = jnp.where((lane >= h * GLA_DK) & (lane < (h + 1) * GLA_DK), q_dec, 0.0).astype(BF16)
            scores = jnp.where(causal, _dot_t1(q_h, k_dec), 0.0).astype(BF16)
            o_heads.append(_dot(scores, v[:, h * GLA_DV:(h + 1) * GLA_DV]))
        o_intra = jnp.concatenate(o_heads, axis=1)
        if gate_b is None:
            gate_b = _sigmoid(proj(O_GATE + D, O_GATE + 2 * D))
        dec_cols = jnp.exp(jnp.concatenate(b_last + [jnp.zeros((8 - NC, GLA_KEY), F32)], axis=0)).T
        dec_all = jnp.concatenate([jnp.broadcast_to(dec_cols[:, c:c + 1], (GLA_KEY, LANES)) for c in range(NC)], axis=1)
        o_inter = []
        for c in range(NC):
            rows = slice(c * C, (c + 1) * C)
            o_inter.append(_dot(q_dec_bf[rows], state_ref[...].astype(BF16)))
            kv = _dot_t0(k_tail[rows], v[rows])
            for h in range(GLA_HEADS):
                rs = slice(h * GLA_DK, (h + 1) * GLA_DK)
                cs = slice(h * GLA_DV, (h + 1) * GLA_DV)
                state_ref[rs, cs] = dec_all[rs, c * LANES:(c + 1) * LANES] * state_ref[rs, cs] + kv[rs, cs]
        o = o_intra + jnp.concatenate(o_inter, axis=0)
        for h in range(GLA_HEADS):
            cs = slice(h * GLA_DV, (h + 1) * GLA_DV)
            o_h = o[:, cs]
            ms = jnp.mean(o_h * o_h, axis=-1, keepdims=True)
            o_h = o_h * lax.rsqrt(ms + LN_EPS) * gnorm_ref[:, cs]
            r_h = qkvr_ref[gr, 2 * GLA_KEY + GLA_VAL + h * GLA_DV:2 * GLA_KEY + GLA_VAL + (h + 1) * GLA_DV]
            ya_ref[gr, cs] = (o_h * (r_h * _sigmoid(r_h))).astype(BF16)

    gate_c = _sigmoid(proj(O_GATE + 2 * D, O_GATE + 3 * D))
    uv = proj(O_UV, O_XC)
    zg = 0.5 * uv * (1.0 + lax.erf(uv * (2.0 ** -0.5)))
    u = zg[:, :SGU_WIDTH]
    vln_ref[...] = _layer_norm(zg[:, SGU_WIDTH:], slng_ref[...], slnb_ref[...])
    lane_s = lax.broadcasted_iota(jnp.int32, (SGU_CHUNK, SGU_WIDTH), 1)
    s_parts = []
    for n in range(T // SGU_CHUNK):
        vc = vln_ref[n * SGU_CHUNK:(n + 1) * SGU_CHUNK, :]
        s = sbias_ref[...]
        for g in range(SGU_GROUPS):
            vg = jnp.where((lane_s >= g * SGU_GD) & (lane_s < (g + 1) * SGU_GD), vc, 0.0).astype(BF16)
            s = s + _dot(wtril_ref[g], vg)
        s_parts.append(s)
    y_b = (u * jnp.concatenate(s_parts, axis=0)).astype(BF16)

    P = POOL_CARRY
    xc = proj(O_XC, O_GATE)
    e_ref[P:P + T, :] = xc
    s2_ref[8:P + T, :] = e_ref[8:P + T, :] + e_ref[7:P + T - 1, :]
    s4_ref[16:P + T, :] = s2_ref[16:P + T, :] + s2_ref[14:P + T - 2, :]
    s8_ref[24:P + T, :] = s4_ref[24:P + T, :] + s4_ref[20:P + T - 4, :]
    s16 = s8_ref[P:P + T, :] + s8_ref[P - 8:P + T - 8, :]
    lane_p = lax.broadcasted_iota(jnp.int32, (T, POOL_WIDTH), 1)
    tpos = lax.broadcasted_iota(jnp.int32, (T, POOL_WIDTH), 0) + (j * T + 1)
    grp = lane_p >> 6
    win = jnp.where(grp == 0, POOL_WINDOWS[0], jnp.where(grp == 1, POOL_WINDOWS[1],
                    jnp.where(grp == 2, POOL_WINDOWS[2], POOL_WINDOWS[3])))
    wsum = jnp.where(grp == 0, s2_ref[P:P + T, :], jnp.where(grp == 1, s4_ref[P:P + T, :],
                     jnp.where(grp == 2, s8_ref[P:P + T, :], s16)))
    count = jnp.minimum(tpos, win).astype(F32)
    pooled = wsum / count - xc
    y_c = (_dot(pooled.astype(BF16), poolw_ref[...]) * pscale_ref[...]).astype(BF16)

    merged = gate_a * _dot(ya_ref[...], wupa_ref[...].astype(BF16))
    merged += gate_b * _dot(y_b, wupb_ref[...].astype(BF16))
    merged += gate_c * _dot(y_c, wupc_ref[...].astype(BF16))
    h = _dot(merged.astype(BF16), wo_ref[...].astype(BF16))
    out_ref[0] = _layer_norm(DEEPNORM_ALPHA * x + h, ln1g_ref[...], ln1b_ref[...])


_MIXER_WEIGHTS = ('whead', 'wglow', 'wtail', 'bcat', 'wg2', 'bg', 'gnorm', 'slng', 'slnb', 'wtril', 'sbias', 'poolw', 'pscale',
                  'wupa', 'wupb', 'wupc', 'wo', 'ln1g', 'ln1b')


def _mixer(x, p, l):
    B, S, D = x.shape
    T = MIX_TILE
    weights = [p[n] for n in _MIXER_WEIGHTS]
    return pl.pallas_call(
        _mixer_kernel,
        out_shape=jax.ShapeDtypeStruct((B, S, D), F32),
        grid=(B, S // T),
        in_specs=[pl.BlockSpec((1, T, D), lambda b, j: (b, j, 0))] + [_layer_spec(w, l) for w in weights],
        out_specs=pl.BlockSpec((1, T, D), lambda b, j: (b, j, 0)),
        scratch_shapes=[
            pltpu.VMEM((GLA_KEY, GLA_VAL), F32),
            pltpu.VMEM((T, 2 * GLA_KEY + 2 * GLA_VAL), F32),
            pltpu.VMEM((T, GLA_VAL), BF16),
            pltpu.VMEM((T, SGU_WIDTH), F32),
            pltpu.VMEM((T + POOL_CARRY, POOL_WIDTH), F32),
            pltpu.VMEM((T + POOL_CARRY, POOL_WIDTH), F32),
            pltpu.VMEM((T + POOL_CARRY, POOL_WIDTH), F32),
            pltpu.VMEM((T + POOL_CARRY, POOL_WIDTH), F32),
        ],
        compiler_params=pltpu.CompilerParams(dimension_semantics=("arbitrary", "arbitrary"),
                                             vmem_limit_bytes=VMEM_LIMIT),
        name="mixer",
    )(x, *weights)


def _memkv_kernel(memt_ref, mem_ref, wk_ref, wv_ref, kt_ref, v_ref):
    kt_ref[0] = _dot_t0(wk_ref[...].astype(BF16), memt_ref[0].astype(BF16)).astype(BF16)
    v_ref[0] = _dot(mem_ref[0].astype(BF16), wv_ref[...].astype(BF16)).astype(BF16)


def _memkv(mem, memt, p, l):
    B, M, D = mem.shape
    return pl.pallas_call(
        _memkv_kernel,
        out_shape=(jax.ShapeDtypeStruct((B, D, M), BF16), jax.ShapeDtypeStruct((B, M, D), BF16)),
        grid=(B,),
        in_specs=[pl.BlockSpec((1, D, M), lambda b: (b, 0, 0)), pl.BlockSpec((1, M, D), lambda b: (b, 0, 0)),
                  _layer_spec(p['wk'], l), _layer_spec(p['wv'], l)],
        out_specs=(pl.BlockSpec((1, D, M), lambda b: (b, 0, 0)), pl.BlockSpec((1, M, D), lambda b: (b, 0, 0))),
        compiler_params=pltpu.CompilerParams(dimension_semantics=("arbitrary",), vmem_limit_bytes=VMEM_LIMIT),
        name="memkv",
    )(memt, mem, p['wk'], p['wv'])


def _xattn_kernel(x_ref, kt_ref, v_ref, wq_ref, wo_ref, ln2g_ref, ln2b_ref, rwt_ref, rbt_ref,
                  x2_ref, x2p_ref, route_ref, counts_ref, carry_ref):
    T = XA_TILE

    @pl.when((pl.program_id(0) == 0) & (pl.program_id(1) == 0))
    def _():
        carry_ref[...] = jnp.zeros_like(carry_ref)

    x = x_ref[0]
    q = (_dot(x.astype(BF16), wq_ref[...].astype(BF16)) * (XA_DH ** -0.5)).astype(BF16)
    h = jnp.zeros_like(x)
    for hd in range(XA_HEADS):
        cs = slice(hd * XA_DH, (hd + 1) * XA_DH)
        s = _dot(q[:, cs], kt_ref[0, cs, :])
        e = jnp.exp(s - jnp.max(s, axis=-1, keepdims=True))
        o = _dot(e.astype(BF16), v_ref[0, :, cs]) / jnp.sum(e, axis=-1, keepdims=True)
        h = h + _dot(o.astype(BF16), wo_ref[cs, :].astype(BF16))
    x2 = _layer_norm(DEEPNORM_ALPHA * x + h, ln2g_ref[...], ln2b_ref[...])
    x2_ref[0] = x2
    x2p_ref[0] = _pack_bf16_pairs(x2)

    E = N_EXPERTS
    hi, lo = _split_bf16(x2)
    lt = _dot_t1(rwt_ref[...], hi)
    logits = lt[0:E] + (lt[E:2 * E] + _dot_t1(rwt_ref[0:E, :], lo)) + rbt_ref[...]

    eid = lax.broadcasted_iota(jnp.int32, (E, T), 0)
    neg_inf = jnp.float32(-jnp.inf)
    rest = logits
    tops, picks = [], []
    for _ in range(TOP_K):
        m = jnp.max(rest, axis=0, keepdims=True)
        idx = jnp.min(jnp.where(rest == m, eid, E), axis=0, keepdims=True)
        pick = eid == idx
        rest = jnp.where(pick, neg_inf, rest)
        tops.append((m, idx))
        picks.append(pick)
    exps = [jnp.exp(m - tops[0][0]) for m, _ in tops]
    denom = exps[0]
    for e in exps[1:]:
        denom = denom + e

    chosen = jnp.zeros((E, T), F32)
    for pick in picks:
        chosen = chosen + jnp.where(pick, 1.0, 0.0)
    chosen_bf = chosen.astype(BF16)
    earlier = (lax.broadcasted_iota(jnp.int32, (T, T), 0) < lax.broadcasted_iota(jnp.int32, (T, T), 1))
    carry = carry_ref[...]
    before = _dot(chosen_bf, jnp.where(earlier, 1.0, 0.0).astype(BF16)) + jnp.concatenate([carry] * (T // LANES), axis=1)
    carry = carry + _dot(chosen_bf, jnp.ones((T, LANES), BF16))
    carry_ref[...] = carry
    counts_ref[...] = carry

    rid = lax.broadcasted_iota(jnp.int32, (ROUTE_ROWS, T), 0)
    route = jnp.zeros((ROUTE_ROWS, T), F32)
    for k in range(TOP_K):
        rank = jnp.sum(jnp.where(picks[k], before, 0.0), axis=0, keepdims=True)
        route = jnp.where(rid == k, tops[k][1].astype(F32), route)
        route = jnp.where(rid == TOP_K + k, exps[k] / denom, route)
        route = jnp.where(rid == 2 * TOP_K + k, rank, route)
    route_ref[...] = route


def _xattn(x, kt, v, p, l):
    B, S, D = x.shape
    T = XA_TILE
    M = MEM_LEN
    weights = [p[n] for n in ('wq', 'wxo', 'ln2g', 'ln2b', 'rwt', 'rbt')]
    return pl.pallas_call(
        _xattn_kernel,
        out_shape=(jax.ShapeDtypeStruct((B, S, D), F32), jax.ShapeDtypeStruct((B, S, D // 2), jnp.uint32),
                   jax.ShapeDtypeStruct((ROUTE_ROWS, B * S), F32), jax.ShapeDtypeStruct((N_EXPERTS, LANES), F32)),
        grid=(B, S // T),
        in_specs=[pl.BlockSpec((1, T, D), lambda b, j: (b, j, 0)),
                  pl.BlockSpec((1, D, M), lambda b, j: (b, 0, 0)),
                  pl.BlockSpec((1, M, D), lambda b, j: (b, 0, 0))] + [_layer_spec(w, l) for w in weights],
        out_specs=(pl.BlockSpec((1, T, D), lambda b, j: (b, j, 0)),
                   pl.BlockSpec((1, T, D // 2), lambda b, j: (b, j, 0)),
                   pl.BlockSpec((ROUTE_ROWS, T), lambda b, j: (0, b * (S // T) + j)),
                   pl.BlockSpec((N_EXPERTS, LANES), lambda b, j: (0, 0))),
        scratch_shapes=[pltpu.VMEM((N_EXPERTS, LANES), F32)],
        compiler_params=pltpu.CompilerParams(dimension_semantics=("arbitrary", "arbitrary"),
                                             vmem_limit_bytes=VMEM_LIMIT),
        name="xattn",
    )(x, kt, v, *weights)


def _expert_kernel(layer, se_ref, sh_ref, nu_ref, nxt_ref, xs_ref, wgu_hbm, bgu_ref, wd_hbm, bd_ref, out_ref,
                   wgu_st, wd_st, slot_ref, sem):
    i = pl.program_id(0)
    F = EXPERT_FF
    halves = sh_ref[i]

    def weight_copies(e, slot):
        return (pltpu.make_async_copy(wgu_hbm.at[layer, e], wgu_st.at[slot], sem.at[slot, 0]),
                pltpu.make_async_copy(wd_hbm.at[layer, e], wd_st.at[slot], sem.at[slot, 1]))

    def ffn(rows):
        slot = slot_ref[0]
        xsb = _unpack_bf16_pairs(xs_ref[rows, :]).astype(BF16)
        hh = _dot(xsb, wgu_st[slot].astype(BF16)) + bgu_ref[...]
        h_glu = jnp.minimum(hh[:, :F], SWIGLU_LIMIT)
        h_lin = jnp.clip(hh[:, F:], -SWIGLU_LIMIT, SWIGLU_LIMIT)
        a = h_glu * _sigmoid(SWIGLU_ALPHA * h_glu) * (h_lin + 1.0)
        out_ref[rows, :] = _pack_bf16_pairs(_dot(a.astype(BF16), wd_st[slot].astype(BF16)) + bd_ref[...])

    @pl.when(halves > 0)
    def _():
        e = se_ref[i]
        prev = se_ref[jnp.maximum(i - 1, 0)]

        @pl.when(i == 0)
        def _():
            slot_ref[0] = 1
            for cp in weight_copies(e, 0):
                cp.start()

        @pl.when((i == 0) | (e != prev))
        def _():
            slot = 1 - slot_ref[0]
            slot_ref[0] = slot
            for cp in weight_copies(e, slot):
                cp.wait()
            nxt = nxt_ref[e]

            @pl.when(nxt != e)
            def _():
                for cp in weight_copies(nxt, 1 - slot):
                    cp.start()

    for n in range(1, MOE_STEP_BLOCKS + 1):
        @pl.when(halves == n)
        def _(n=n):
            ffn(slice(0, n * MOE_BLOCK))


def _experts(step_expert, step_halves, n_used, next_expert, xs, w_gu, b_gu, w_down, b_down, l):
    P, DH = xs.shape
    D = 2 * DH
    R = MOE_STEP_BLOCKS * MOE_BLOCK
    NS = P // R
    F2 = 2 * EXPERT_FF

    def row_map(i, se, sh, nu, nxt):
        return (jnp.minimum(i, nu[0] - 1), 0)

    def exp_map(i, se, sh, nu, nxt):
        return (l, se[jnp.minimum(i, nu[0] - 1)], 0, 0)

    grid_spec = pltpu.PrefetchScalarGridSpec(
        num_scalar_prefetch=4,
        grid=(NS,),
        in_specs=[pl.BlockSpec((R, DH), row_map),
                  pl.BlockSpec(memory_space=pl.ANY),
                  pl.BlockSpec((None, None, 1, F2), exp_map),
                  pl.BlockSpec(memory_space=pl.ANY),
                  pl.BlockSpec((None, None, 1, D), exp_map)],
        out_specs=pl.BlockSpec((R, DH), row_map),
        scratch_shapes=[pltpu.VMEM((2, D, F2), F32), pltpu.VMEM((2, EXPERT_FF, D), F32),
                        pltpu.SMEM((1,), jnp.int32),
                        pltpu.SemaphoreType.DMA((2, 2))],
    )
    return pl.pallas_call(
        functools.partial(_expert_kernel, l),
        out_shape=jax.ShapeDtypeStruct((P, DH), jnp.uint32),
        grid_spec=grid_spec,
        compiler_params=pltpu.CompilerParams(dimension_semantics=("arbitrary",), vmem_limit_bytes=VMEM_LIMIT),
        name="experts",
    )(step_expert, step_halves, n_used, next_expert, xs, w_gu, b_gu.reshape(DEPTH, N_EXPERTS, 1, F2), w_down,
      b_down.reshape(DEPTH, N_EXPERTS, 1, D))


def _sc_gather_rows(x, idx):
    M = idx.shape[0]
    D = x.shape[1]
    W = SC_GATHER_WINDOW
    mesh = plsc.VectorSubcoreMesh(core_axis_name="core", subcore_axis_name="subcore")
    n_workers = mesh.num_cores * mesh.num_subcores
    rows_per = M // n_workers
    assert rows_per * n_workers == M and rows_per % W == 0

    @pl.kernel(out_type=jax.ShapeDtypeStruct((M, D), x.dtype), mesh=mesh, name="sc_gather_rows",
               scratch_types=[pltpu.VMEM((rows_per,), jnp.int32), pltpu.VMEM((W, D), x.dtype)])
    def gather_kernel(x_hbm, i_hbm, o_hbm, idx_vmem, buf):
        wid = lax.axis_index("core") * mesh.num_subcores + lax.axis_index("subcore")
        base = wid * rows_per
        pltpu.sync_copy(i_hbm.at[pl.ds(base, rows_per)], idx_vmem)

        @pl.loop(0, rows_per // W)
        def _(j):
            pltpu.sync_copy(x_hbm.at[idx_vmem.at[pl.ds(j * W, W)]], buf)
            pltpu.sync_copy(buf, o_hbm.at[pl.ds(base + j * W, W)])

    return gather_kernel(x, idx)


def _sc_scatter_rows(x, idx, n_out):
    K, N = idx.shape
    D = x.shape[1]
    W = SC_GATHER_WINDOW
    mesh = plsc.VectorSubcoreMesh(core_axis_name="core", subcore_axis_name="subcore")
    n_workers = mesh.num_cores * mesh.num_subcores
    rows_per = N // n_workers
    assert rows_per * n_workers == N and rows_per % W == 0

    @pl.kernel(out_type=jax.ShapeDtypeStruct((n_out, D), x.dtype), mesh=mesh, name="sc_scatter_rows",
               scratch_types=[pltpu.VMEM((K * rows_per,), jnp.int32), pltpu.VMEM((W, D), x.dtype),
                              pltpu.SemaphoreType.DMA((K,))])
    def scatter_kernel(x_hbm, i_hbm, o_hbm, idx_vmem, buf, sem):
        wid = lax.axis_index("core") * mesh.num_subcores + lax.axis_index("subcore")
        base = wid * rows_per
        for k in range(K):
            pltpu.sync_copy(i_hbm.at[pl.ds(k * N + base, rows_per)], idx_vmem.at[pl.ds(k * rows_per, rows_per)])

        @pl.loop(0, rows_per // W)
        def _(j):
            pltpu.sync_copy(x_hbm.at[pl.ds(base + j * W, W)], buf)
            copies = [pltpu.make_async_copy(buf, o_hbm.at[idx_vmem.at[pl.ds(k * rows_per + j * W, W)]], sem.at[k])
                      for k in range(K)]
            for cp in copies:
                cp.start()
            for cp in copies:
                cp.wait()

    return scatter_kernel(x, idx.reshape(K * N))


def _combine_kernel(x_ref, yg_ref, route_ref, ln3g_ref, ln3b_ref, *rest):
    out_ref = rest[-1]
    x = x_ref[...]
    g = jnp.transpose(route_ref[...])
    y = jnp.zeros_like(x)
    for k in range(TOP_K):
        y = y + g[:, TOP_K + k:TOP_K + k + 1] * _unpack_bf16_pairs(yg_ref[k * CMB_TILE:(k + 1) * CMB_TILE, :])
    out_ref[...] = _layer_norm(DEEPNORM_ALPHA * x + y, ln3g_ref[...], ln3b_ref[...])


def _combine(x2, yg, route, p, l, part, acc):
    N, D = x2.shape
    T = CMB_TILE
    tiles = yg.shape[0] // (TOP_K * T)
    first = part * tiles
    in_specs = [pl.BlockSpec((T, D), lambda i: (first + i, 0)), pl.BlockSpec((TOP_K * T, D // 2), lambda i: (i, 0)),
                pl.BlockSpec((ROUTE_ROWS, T), lambda i: (0, first + i)), _layer_spec(p['ln3g'], l), _layer_spec(p['ln3b'], l)]
    args = [x2, yg, route, p['ln3g'], p['ln3b']]
    aliases = {}
    if acc is not None:
        in_specs.append(pl.BlockSpec(memory_space=pl.ANY))
        args.append(acc)
        aliases = {len(args) - 1: 0}
    return pl.pallas_call(
        _combine_kernel,
        out_shape=jax.ShapeDtypeStruct((N, D), F32),
        grid=(tiles,),
        in_specs=in_specs,
        out_specs=pl.BlockSpec((T, D), lambda i: (first + i, 0)),
        input_output_aliases=aliases,
        compiler_params=pltpu.CompilerParams(dimension_semantics=("arbitrary",), vmem_limit_bytes=VMEM_LIMIT),
        name="combine",
    )(*args)


def _prep(w_in, b_in, gla_wg2, gla_bg, gla_norm_g, sgu_ln_g, sgu_ln_b, sgu_ws, sgu_bs, pool_w, pool_scale,
          w_up_a, w_up_b, w_up_c, w_o, ln1_g, ln1_b, xa_wq, xa_wk, xa_wv, xa_wo, ln2_g, ln2_b,
          router_w, router_b, ln3_g, ln3_b):
    L = w_in.shape[0]
    o_glow = O_GLOW
    o_uv = o_glow + GLA_RANK
    row = lambda a: a.reshape(L, 1, -1).astype(F32)
    pad_last = lambda a, n: jnp.pad(a, [(0, 0)] * (a.ndim - 1) + [(0, n - a.shape[-1])])
    p = {}
    p['whead'] = w_in[..., :o_glow].astype(BF16)
    p['wglow'] = pad_last(w_in[..., o_glow:o_uv], LANES).astype(BF16)
    p['wtail'] = w_in[..., o_uv:].astype(BF16)
    p['bcat'] = row(jnp.concatenate([b_in[..., :o_glow], pad_last(b_in[..., o_glow:o_uv], LANES), b_in[..., o_uv:]],
                                    axis=-1))
    p['wg2'] = jnp.pad(gla_wg2, ((0, 0), (0, LANES - GLA_RANK), (0, 0))).astype(BF16)
    p['bg'] = row(gla_bg)
    p['gnorm'] = row(gla_norm_g)
    p['slng'] = row(sgu_ln_g)
    p['slnb'] = row(sgu_ln_b)
    p['wtril'] = jnp.tril(sgu_ws).astype(BF16)
    p['sbias'] = jnp.repeat(jnp.swapaxes(sgu_bs, 1, 2), SGU_GD, axis=2).astype(F32)
    G = len(POOL_WINDOWS)
    eye = jnp.eye(G, dtype=F32)
    p['poolw'] = jnp.einsum('lgcd,gh->lgchd', pool_w, eye).reshape(L, POOL_WIDTH, POOL_WIDTH).astype(BF16)
    p['pscale'] = row(pool_scale)
    p['wupa'], p['wupb'], p['wupc'], p['wo'] = w_up_a, w_up_b, w_up_c, w_o
    p['ln1g'], p['ln1b'] = row(ln1_g), row(ln1_b)
    p['wq'], p['wk'], p['wv'], p['wxo'] = xa_wq, xa_wk, xa_wv, xa_wo
    p['ln2g'], p['ln2b'] = row(ln2_g), row(ln2_b)
    rwt = jnp.swapaxes(router_w, 1, 2)
    rwt_hi = rwt.astype(BF16)
    p['rwt'] = jnp.concatenate([rwt_hi, (rwt - rwt_hi.astype(F32)).astype(BF16)], axis=1)
    p['rbt'] = jnp.broadcast_to(router_b[:, :, None], router_b.shape + (XA_TILE,)).astype(F32)
    p['ln3g'], p['ln3b'] = row(ln3_g), row(ln3_b)
    return p


def _route(route, counts):
    N = route.shape[1]
    top_idx = route[0:TOP_K].astype(jnp.int32)
    rank = route[2 * TOP_K:3 * TOP_K].astype(jnp.int32)
    counts = counts[:, 0].astype(jnp.int32)
    R = MOE_STEP_BLOCKS * MOE_BLOCK
    blocks = (counts + MOE_BLOCK - 1) // MOE_BLOCK
    padded = ((counts + R - 1) // R) * R
    pad_end = jnp.cumsum(padded)
    pad_start = pad_end - padded
    ids = jnp.arange(N_EXPERTS, dtype=jnp.int32)
    start_of = jnp.sum(jnp.where(top_idx[:, :, None] == ids[None, None, :], pad_start[None, None, :], 0), axis=-1)
    dest = start_of + rank
    n_steps = N * TOP_K // R + N_EXPERTS
    step_start = jnp.arange(n_steps, dtype=jnp.int32) * R
    step_expert = jnp.minimum(jnp.sum((pad_end[None, :] <= step_start[:, None]).astype(jnp.int32), axis=1),
                              N_EXPERTS - 1)
    mine = step_expert[:, None] == ids[None, :]
    blocks_before = (step_start - jnp.sum(jnp.where(mine, pad_start[None, :], 0), axis=1)) // MOE_BLOCK
    step_halves = jnp.clip(jnp.sum(jnp.where(mine, blocks[None, :], 0), axis=1) - blocks_before, 0, MOE_STEP_BLOCKS)
    step_halves = jnp.where(step_start < pad_end[-1], step_halves, 0).astype(jnp.int32)
    n_used = (pad_end[-1] // R).astype(jnp.int32).reshape(1)
    later = jnp.where((ids[None, :] > ids[:, None]) & (counts[None, :] > 0), ids[None, :], N_EXPERTS)
    first_later = jnp.min(later, axis=1)
    next_expert = jnp.where(first_later < N_EXPERTS, first_later, ids).astype(jnp.int32)
    return dest, step_expert, step_halves, n_used, next_expert


def kernel(x, mem, w_in, b_in, gla_wg2, gla_bg, gla_norm_g, sgu_ln_g, sgu_ln_b, sgu_ws, sgu_bs, pool_w, pool_scale, w_up_a, w_up_b, w_up_c, w_o, ln1_g, ln1_b, xa_wq, xa_wk, xa_wv, xa_wo, ln2_g, ln2_b, router_w, router_b, exp_w_gu, exp_b_gu, exp_w_down, exp_b_down, ln3_g, ln3_b):
    B, S, D = x.shape
    N = B * S
    p = _prep(w_in, b_in, gla_wg2, gla_bg, gla_norm_g, sgu_ln_g, sgu_ln_b, sgu_ws, sgu_bs, pool_w, pool_scale,
              w_up_a, w_up_b, w_up_c, w_o, ln1_g, ln1_b, xa_wq, xa_wk, xa_wv, xa_wo, ln2_g, ln2_b,
              router_w, router_b, ln3_g, ln3_b)
    memt = jnp.swapaxes(mem, 1, 2)
    for l in range(DEPTH):
        x1 = _mixer(x, p, l)
        kt, v = _memkv(mem, memt, p, l)
        x2, x2p, route, counts = _xattn(x1, kt, v, p, l)
        dest, step_expert, step_halves, n_used, next_expert = _route(route, counts)
        n_slots = N * TOP_K + N_EXPERTS * MOE_STEP_BLOCKS * MOE_BLOCK
        xs = _sc_scatter_rows(x2p.reshape(N, D // 2), dest, n_slots)
        ys = _experts(step_expert, step_halves, n_used, next_expert, xs, exp_w_gu, exp_b_gu, exp_w_down, exp_b_down, l)
        dest_km = dest.reshape(TOP_K, N // CMB_TILE, CMB_TILE).transpose(1, 0, 2).reshape(-1)
        rows = dest_km.shape[0] // COMBINE_PARTS
        acc = None
        for part in range(COMBINE_PARTS):
            yg = _sc_gather_rows(ys, dest_km[part * rows:(part + 1) * rows])
            acc = _combine(x2.reshape(N, D), yg, route, p, l, part, acc)
        x = acc.reshape(B, S, D)
    return x
```

```python
import functools

import jax
import jax.numpy as jnp
from jax import lax
from jax.experimental import pallas as pl
from jax.experimental.pallas import tpu as pltpu
from jax.experimental.pallas import tpu_sc as plsc

F32 = jnp.float32
BF16 = jnp.bfloat16

D_MODEL = 1024
DEPTH = 2
GLA_HEADS = 4
GLA_KEY = 256
GLA_VAL = 512
GLA_DK = 64
GLA_DV = 128
GLA_RANK = 16
GLA_TAU = 16.0
GLA_CHUNK = 64
SGU_GROUPS = 4
SGU_WIDTH = 256
SGU_GD = 64
SGU_CHUNK = 128
POOL_WINDOWS = (2, 4, 8, 16)
POOL_WIDTH = 256
POOL_GD = 64
POOL_CARRY = 32
MEM_LEN = 256
XA_HEADS = 4
XA_DH = 256
N_EXPERTS = 32
TOP_K = 4
EXPERT_FF = 1024
SWIGLU_LIMIT = 7.0
SWIGLU_ALPHA = 1.702
DEEPNORM_ALPHA = (2 * DEPTH) ** 0.25
LN_EPS = 1e-5
LANES = 128
VMEM_LIMIT = 56 * 1024 * 1024

MIX_TILE = 512
GLA_SUB = 256
XA_TILE = 1024
MOE_BLOCK = 256
MOE_STEP_BLOCKS = 4
CMB_TILE = 512
COMBINE_PARTS = 4
REPACK_ROWS = 128
ROUTE_ROWS = 16
SC_GATHER_WINDOW = 128

O_QKVR = 0
O_GLOW = 2 * GLA_KEY + 2 * GLA_VAL
O_UV = O_GLOW + LANES
O_XC = O_UV + 2 * SGU_WIDTH
O_GATE = O_XC + POOL_WIDTH
N_PACK = O_GATE + 3 * D_MODEL


def _dot(a, b):
    return jnp.dot(a, b, preferred_element_type=F32)


def _dot_t0(a, b):
    return lax.dot_general(a, b, (((0,), (0,)), ((), ())), preferred_element_type=F32)


def _dot_t1(a, b):
    return lax.dot_general(a, b, (((1,), (1,)), ((), ())), preferred_element_type=F32)


def _split_bf16(x):
    hi = x.astype(BF16)
    lo = (x - hi.astype(F32)).astype(BF16)
    return hi, lo


def _layer_norm(x, g, b):
    mu = jnp.mean(x, axis=-1, keepdims=True)
    xc = x - mu
    var = jnp.mean(xc * xc, axis=-1, keepdims=True)
    return xc * lax.rsqrt(var + LN_EPS) * g + b


def _sigmoid(x):
    return 1.0 / (1.0 + jnp.exp(-x))


def _pack_bf16_pairs(x):
    H = x.shape[1] // 2
    bits = lax.bitcast_convert_type(x.astype(BF16).astype(F32), jnp.uint32)
    return (bits[:, :H] >> 16) | (bits[:, H:] & jnp.uint32(0xFFFF0000))


def _unpack_bf16_pairs(w):
    lo = lax.bitcast_convert_type(w << 16, F32)
    hi = lax.bitcast_convert_type(w & jnp.uint32(0xFFFF0000), F32)
    return jnp.concatenate([lo, hi], axis=1)


def _layer_spec(arr, l):
    nd = arr.ndim - 1
    return pl.BlockSpec((None,) + arr.shape[1:], lambda *_: (l,) + (0,) * nd, pipeline_mode=pl.Buffered(1))


def _mixer_kernel(x_ref, whead_ref, wglow_ref, wtail_ref, bcat_ref, wg2_ref, bg_ref, gnorm_ref,
                  slng_ref, slnb_ref, wtril_ref, sbias_ref, poolw_ref, pscale_ref,
                  wupa_ref, wupb_ref, wupc_ref, wo_ref, ln1g_ref, ln1b_ref,
                  out_ref,
                  state_ref, qkvr_ref, ya_ref, vln_ref, e_ref, s2_ref, s4_ref, s8_ref):
    T = MIX_TILE
    D = D_MODEL
    j = pl.program_id(1)
    x = x_ref[0]
    xb = x.astype(BF16)

    def proj(lo, hi):
        if hi <= O_GLOW:
            w = whead_ref[:, lo:hi]
        elif lo == O_GLOW and hi == O_UV:
            w = wglow_ref[...]
        else:
            w = wtail_ref[:, lo - O_UV:hi - O_UV]
        return _dot(xb, w) + bcat_ref[:, lo:hi]

    @pl.when(j == 0)
    def _():
        state_ref[...] = jnp.zeros_like(state_ref)
        e_ref[0:POOL_CARRY, :] = jnp.zeros((POOL_CARRY, POOL_WIDTH), F32)

    @pl.when(j > 0)
    def _():
        e_ref[0:POOL_CARRY, :] = e_ref[T:T + POOL_CARRY, :]

    qkvr_ref[...] = proj(O_QKVR, O_GLOW)
    glow = proj(O_GLOW, O_UV)
    z = _dot(glow.astype(BF16), wg2_ref[...]) + bg_ref[...]
    la = (jnp.minimum(z, 0.0) - jnp.log1p(jnp.exp(-jnp.abs(z)))) * (1.0 / GLA_TAU)
    la_hi, la_lo = _split_bf16(la)
    gate_a = _sigmoid(proj(O_GATE, O_GATE + D))

    C = GLA_CHUNK
    G = GLA_SUB
    NC = G // C
    CSH = C.bit_length() - 1
    row = lax.broadcasted_iota(jnp.int32, (G, G), 0)
    col = lax.broadcasted_iota(jnp.int32, (G, G), 1)
    same_chunk = (row >> CSH) == (col >> CSH)
    causal = same_chunk & (row >= col)
    causal_bf = jnp.where(causal, 1.0, 0.0).astype(BF16)
    lane = lax.broadcasted_iota(jnp.int32, (G, GLA_KEY), 1)
    gate_b = None
    for g0 in range(0, T, G):
        gr = slice(g0, g0 + G)
        lh, ll = la_hi[gr], la_lo[gr]
        b = _dot(causal_bf, lh) + _dot(causal_bf, ll)
        b_last = [b[(c + 1) * C - 1:(c + 1) * C, :] for c in range(NC)]
        b_end = jnp.concatenate([jnp.broadcast_to(r, (C, GLA_KEY)) for r in b_last], axis=0)
        q = qkvr_ref[gr, 0:GLA_KEY]
        k = qkvr_ref[gr, GLA_KEY:2 * GLA_KEY]
        v = qkvr_ref[gr, 2 * GLA_KEY:2 * GLA_KEY + GLA_VAL].astype(BF16)
        q_dec = q * (GLA_DK ** -0.5) * jnp.exp(b)
        k_dec = (k * jnp.exp(-b)).astype(BF16)
        k_tail = (k * jnp.exp(b_end - b)).astype(BF16)
        q_dec_bf = q_dec.astype(BF16)
        o_heads = []
        for h in range(GLA_HEADS):
            q_h = jnp.where((lane >= h * GLA_DK) & (lane < (h + 1) * GLA_DK), q_dec, 0.0).astype(BF16)
            scores = jnp.where(causal, _dot_t1(q_h, k_dec), 0.0).astype(BF16)
            o_heads.append(_dot(scores, v[:, h * GLA_DV:(h + 1) * GLA_DV]))
        o_intra = jnp.concatenate(o_heads, axis=1)
        if gate_b is None:
            gate_b = _sigmoid(proj(O_GATE + D, O_GATE + 2 * D))
        dec_cols = jnp.exp(jnp.concatenate(b_last + [jnp.zeros((8 - NC, GLA_KEY), F32)], axis=0)).T
        dec_all = jnp.concatenate([jnp.broadcast_to(dec_cols[:, c:c + 1], (GLA_KEY, LANES)) for c in range(NC)], axis=1)
        o_inter = []
        for c in range(NC):
            rows = slice(c * C, (c + 1) * C)
            o_inter.append(_dot(q_dec_bf[rows], state_ref[...].astype(BF16)))
            kv = _dot_t0(k_tail[rows], v[rows])
            for h in range(GLA_HEADS):
                rs = slice(h * GLA_DK, (h + 1) * GLA_DK)
                cs = slice(h * GLA_DV, (h + 1) * GLA_DV)
                state_ref[rs, cs] = dec_all[rs, c * LANES:(c + 1) * LANES] * state_ref[rs, cs] + kv[rs, cs]
        o = o_intra + jnp.concatenate(o_inter, axis=0)
        for h in range(GLA_HEADS):
            cs = slice(h * GLA_DV, (h + 1) * GLA_DV)
            o_h = o[:, cs]
            ms = jnp.mean(o_h * o_h, axis=-1, keepdims=True)
            o_h = o_h * lax.rsqrt(ms + LN_EPS) * gnorm_ref[:, cs]
            r_h = qkvr_ref[gr, 2 * GLA_KEY + GLA_VAL + h * GLA_DV:2 * GLA_KEY + GLA_VAL + (h + 1) * GLA_DV]
            ya_ref[gr, cs] = (o_h * (r_h * _sigmoid(r_h))).astype(BF16)

    gate_c = _sigmoid(proj(O_GATE + 2 * D, O_GATE + 3 * D))
    uv = proj(O_UV, O_XC)
    zg = 0.5 * uv * (1.0 + lax.erf(uv * (2.0 ** -0.5)))
    u = zg[:, :SGU_WIDTH]
    vln_ref[...] = _layer_norm(zg[:, SGU_WIDTH:], slng_ref[...], slnb_ref[...])
    lane_s = lax.broadcasted_iota(jnp.int32, (SGU_CHUNK, SGU_WIDTH), 1)
    s_parts = []
    for n in range(T // SGU_CHUNK):
        vc = vln_ref[n * SGU_CHUNK:(n + 1) * SGU_CHUNK, :]
        s = sbias_ref[...]
        for g in range(SGU_GROUPS):
            vg = jnp.where((lane_s >= g * SGU_GD) & (lane_s < (g + 1) * SGU_GD), vc, 0.0).astype(BF16)
            s = s + _dot(wtril_ref[g], vg)
        s_parts.append(s)
    y_b = (u * jnp.concatenate(s_parts, axis=0)).astype(BF16)

    P = POOL_CARRY
    xc = proj(O_XC, O_GATE)
    e_ref[P:P + T, :] = xc
    s2_ref[8:P + T, :] = e_ref[8:P + T, :] + e_ref[7:P + T - 1, :]
    s4_ref[16:P + T, :] = s2_ref[16:P + T, :] + s2_ref[14:P + T - 2, :]
    s8_ref[24:P + T, :] = s4_ref[24:P + T, :] + s4_ref[20:P + T - 4, :]
    s16 = s8_ref[P:P + T, :] + s8_ref[P - 8:P + T - 8, :]
    lane_p = lax.broadcasted_iota(jnp.int32, (T, POOL_WIDTH), 1)
    tpos = lax.broadcasted_iota(jnp.int32, (T, POOL_WIDTH), 0) + (j * T + 1)
    grp = lane_p >> 6
    win = jnp.where(grp == 0, POOL_WINDOWS[0], jnp.where(grp == 1, POOL_WINDOWS[1],
                    jnp.where(grp == 2, POOL_WINDOWS[2], POOL_WINDOWS[3])))
    wsum = jnp.where(grp == 0, s2_ref[P:P + T, :], jnp.where(grp == 1, s4_ref[P:P + T, :],
                     jnp.where(grp == 2, s8_ref[P:P + T, :], s16)))
    count = jnp.minimum(tpos, win).astype(F32)
    pooled = wsum / count - xc
    y_c = (_dot(pooled.astype(BF16), poolw_ref[...]) * pscale_ref[...]).astype(BF16)

    merged = gate_a * _dot(ya_ref[...], wupa_ref[...].astype(BF16))
    merged += gate_b * _dot(y_b, wupb_ref[...].astype(BF16))
    merged += gate_c * _dot(y_c, wupc_ref[...].astype(BF16))
    h = _dot(merged.astype(BF16), wo_ref[...].astype(BF16))
    out_ref[0] = _layer_norm(DEEPNORM_ALPHA * x + h, ln1g_ref[...], ln1b_ref[...])


_MIXER_WEIGHTS = ('whead', 'wglow', 'wtail', 'bcat', 'wg2', 'bg', 'gnorm', 'slng', 'slnb', 'wtril', 'sbias', 'poolw', 'pscale',
                  'wupa', 'wupb', 'wupc', 'wo', 'ln1g', 'ln1b')


def _mixer(x, p, l):
    B, S, D = x.shape
    T = MIX_TILE
    weights = [p[n] for n in _MIXER_WEIGHTS]
    return pl.pallas_call(
        _mixer_kernel,
        out_shape=jax.ShapeDtypeStruct((B, S, D), F32),
        grid=(B, S // T),
        in_specs=[pl.BlockSpec((1, T, D), lambda b, j: (b, j, 0))] + [_layer_spec(w, l) for w in weights],
        out_specs=pl.BlockSpec((1, T, D), lambda b, j: (b, j, 0)),
        scratch_shapes=[
            pltpu.VMEM((GLA_KEY, GLA_VAL), F32),
            pltpu.VMEM((T, 2 * GLA_KEY + 2 * GLA_VAL), F32),
            pltpu.VMEM((T, GLA_VAL), BF16),
            pltpu.VMEM((T, SGU_WIDTH), F32),
            pltpu.VMEM((T + POOL_CARRY, POOL_WIDTH), F32),
            pltpu.VMEM((T + POOL_CARRY, POOL_WIDTH), F32),
            pltpu.VMEM((T + POOL_CARRY, POOL_WIDTH), F32),
            pltpu.VMEM((T + POOL_CARRY, POOL_WIDTH), F32),
        ],
        compiler_params=pltpu.CompilerParams(dimension_semantics=("arbitrary", "arbitrary"),
                                             vmem_limit_bytes=VMEM_LIMIT),
        name="mixer",
    )(x, *weights)


def _repack_kernel(w_ref, head_ref, glow_ref, tail_ref):
    w = w_ref[...]
    head_ref[...] = w[:, :O_GLOW].astype(BF16)
    lane = lax.broadcasted_iota(jnp.int32, (w.shape[0], LANES), 1)
    glow_ref[...] = jnp.where(lane < GLA_RANK, w[:, O_GLOW:O_GLOW + LANES], 0.0).astype(BF16)
    tail_ref[...] = w[:, O_GLOW + GLA_RANK:].astype(BF16)


def _repack_w_in(w_in):
    L, D, n_in = w_in.shape
    n_tail = n_in - O_GLOW - GLA_RANK
    rows = REPACK_ROWS
    return pl.pallas_call(
        _repack_kernel,
        out_shape=(jax.ShapeDtypeStruct((L, D, O_GLOW), BF16), jax.ShapeDtypeStruct((L, D, LANES), BF16),
                   jax.ShapeDtypeStruct((L, D, n_tail), BF16)),
        grid=(L, D // rows),
        in_specs=[pl.BlockSpec((None, rows, n_in), lambda l, i: (l, i, 0))],
        out_specs=(pl.BlockSpec((None, rows, O_GLOW), lambda l, i: (l, i, 0)),
                   pl.BlockSpec((None, rows, LANES), lambda l, i: (l, i, 0)),
                   pl.BlockSpec((None, rows, n_tail), lambda l, i: (l, i, 0))),
        compiler_params=pltpu.CompilerParams(dimension_semantics=("arbitrary", "arbitrary"),
                                             vmem_limit_bytes=VMEM_LIMIT),
        name="repack_w_in",
    )(w_in)


def _memkv_kernel(memt_ref, mem_ref, wk_ref, wv_ref, kt_ref, v_ref):
    kt_ref[0] = _dot_t0(wk_ref[...].astype(BF16), memt_ref[0].astype(BF16)).astype(BF16)
    v_ref[0] = _dot(mem_ref[0].astype(BF16), wv_ref[...].astype(BF16)).astype(BF16)


def _memkv(mem, memt, p, l):
    B, M, D = mem.shape
    return pl.pallas_call(
        _memkv_kernel,
        out_shape=(jax.ShapeDtypeStruct((B, D, M), BF16), jax.ShapeDtypeStruct((B, M, D), BF16)),
        grid=(B,),
        in_specs=[pl.BlockSpec((1, D, M), lambda b: (b, 0, 0)), pl.BlockSpec((1, M, D), lambda b: (b, 0, 0)),
                  _layer_spec(p['wk'], l), _layer_spec(p['wv'], l)],
        out_specs=(pl.BlockSpec((1, D, M), lambda b: (b, 0, 0)), pl.BlockSpec((1, M, D), lambda b: (b, 0, 0))),
        compiler_params=pltpu.CompilerParams(dimension_semantics=("arbitrary",), vmem_limit_bytes=VMEM_LIMIT),
        name="memkv",
    )(memt, mem, p['wk'], p['wv'])


def _xattn_kernel(x_ref, kt_ref, v_ref, wq_ref, wo_ref, ln2g_ref, ln2b_ref, rwt_ref, rbt_ref,
                  x2_ref, x2p_ref, route_ref, counts_ref, carry_ref):
    T = XA_TILE

    @pl.when((pl.program_id(0) == 0) & (pl.program_id(1) == 0))
    def _():
        carry_ref[...] = jnp.zeros_like(carry_ref)

    x = x_ref[0]
    q = (_dot(x.astype(BF16), wq_ref[...].astype(BF16)) * (XA_DH ** -0.5)).astype(BF16)
    h = jnp.zeros_like(x)
    for hd in range(XA_HEADS):
        cs = slice(hd * XA_DH, (hd + 1) * XA_DH)
        s = _dot(q[:, cs], kt_ref[0, cs, :])
        e = jnp.exp(s - jnp.max(s, axis=-1, keepdims=True))
        o = _dot(e.astype(BF16), v_ref[0, :, cs]) / jnp.sum(e, axis=-1, keepdims=True)
        h = h + _dot(o.astype(BF16), wo_ref[cs, :].astype(BF16))
    x2 = _layer_norm(DEEPNORM_ALPHA * x + h, ln2g_ref[...], ln2b_ref[...])
    x2_ref[0] = x2
    x2p_ref[0] = _pack_bf16_pairs(x2)

    E = N_EXPERTS
    hi, lo = _split_bf16(x2)
    lt = _dot_t1(rwt_ref[...], hi)
    logits = lt[0:E] + (lt[E:2 * E] + _dot_t1(rwt_ref[0:E, :], lo)) + rbt_ref[...]

    eid = lax.broadcasted_iota(jnp.int32, (E, T), 0)
    neg_inf = jnp.float32(-jnp.inf)
    rest = logits
    tops, picks = [], []
    for _ in range(TOP_K):
        m = jnp.max(rest, axis=0, keepdims=True)
        idx = jnp.min(jnp.where(rest == m, eid, E), axis=0, keepdims=True)
        pick = eid == idx
        rest = jnp.where(pick, neg_inf, rest)
        tops.append((m, idx))
        picks.append(pick)
    exps = [jnp.exp(m - tops[0][0]) for m, _ in tops]
    denom = exps[0]
    for e in exps[1:]:
        denom = denom + e

    chosen = jnp.zeros((E, T), F32)
    for pick in picks:
        chosen = chosen + jnp.where(pick, 1.0, 0.0)
    chosen_bf = chosen.astype(BF16)
    earlier = (lax.broadcasted_iota(jnp.int32, (T, T), 0) < lax.broadcasted_iota(jnp.int32, (T, T), 1))
    carry = carry_ref[...]
    before = _dot(chosen_bf, jnp.where(earlier, 1.0, 0.0).astype(BF16)) + jnp.concatenate([carry] * (T // LANES), axis=1)
    carry = carry + _dot(chosen_bf, jnp.ones((T, LANES), BF16))
    carry_ref[...] = carry
    counts_ref[...] = carry

    rid = lax.broadcasted_iota(jnp.int32, (ROUTE_ROWS, T), 0)
    route = jnp.zeros((ROUTE_ROWS, T), F32)
    for k in range(TOP_K):
        rank = jnp.sum(jnp.where(picks[k], before, 0.0), axis=0, keepdims=True)
        route = jnp.where(rid == k, tops[k][1].astype(F32), route)
        route = jnp.where(rid == TOP_K + k, exps[k] / denom, route)
        route = jnp.where(rid == 2 * TOP_K + k, rank, route)
    route_ref[...] = route


def _xattn(x, kt, v, p, l):
    B, S, D = x.shape
    T = XA_TILE
    M = MEM_LEN
    weights = [p[n] for n in ('wq', 'wxo', 'ln2g', 'ln2b', 'rwt', 'rbt')]
    return pl.pallas_call(
        _xattn_kernel,
        out_shape=(jax.ShapeDtypeStruct((B, S, D), F32), jax.ShapeDtypeStruct((B, S, D // 2), jnp.uint32),
                   jax.ShapeDtypeStruct((ROUTE_ROWS, B * S), F32), jax.ShapeDtypeStruct((N_EXPERTS, LANES), F32)),
        grid=(B, S // T),
        in_specs=[pl.BlockSpec((1, T, D), lambda b, j: (b, j, 0)),
                  pl.BlockSpec((1, D, M), lambda b, j: (b, 0, 0)),
                  pl.BlockSpec((1, M, D), lambda b, j: (b, 0, 0))] + [_layer_spec(w, l) for w in weights],
        out_specs=(pl.BlockSpec((1, T, D), lambda b, j: (b, j, 0)),
                   pl.BlockSpec((1, T, D // 2), lambda b, j: (b, j, 0)),
                   pl.BlockSpec((ROUTE_ROWS, T), lambda b, j: (0, b * (S // T) + j)),
                   pl.BlockSpec((N_EXPERTS, LANES), lambda b, j: (0, 0))),
        scratch_shapes=[pltpu.VMEM((N_EXPERTS, LANES), F32)],
        compiler_params=pltpu.CompilerParams(dimension_semantics=("arbitrary", "arbitrary"),
                                             vmem_limit_bytes=VMEM_LIMIT),
        name="xattn",
    )(x, kt, v, *weights)


def _expert_kernel(layer, se_ref, sh_ref, nu_ref, nxt_ref, xs_ref, wgu_hbm, bgu_ref, wd_hbm, bd_ref, out_ref,
                   wgu_st, wd_st, slot_ref, sem):
    i = pl.program_id(0)
    F = EXPERT_FF
    halves = sh_ref[i]

    def weight_copies(e, slot):
        return (pltpu.make_async_copy(wgu_hbm.at[layer, e], wgu_st.at[slot], sem.at[slot, 0]),
                pltpu.make_async_copy(wd_hbm.at[layer, e], wd_st.at[slot], sem.at[slot, 1]))

    def ffn(rows):
        slot = slot_ref[0]
        xsb = _unpack_bf16_pairs(xs_ref[rows, :]).astype(BF16)
        hh = _dot(xsb, wgu_st[slot].astype(BF16)) + bgu_ref[...]
        h_glu = jnp.minimum(hh[:, :F], SWIGLU_LIMIT)
        h_lin = jnp.clip(hh[:, F:], -SWIGLU_LIMIT, SWIGLU_LIMIT)
        a = h_glu * _sigmoid(SWIGLU_ALPHA * h_glu) * (h_lin + 1.0)
        out_ref[rows, :] = _pack_bf16_pairs(_dot(a.astype(BF16), wd_st[slot].astype(BF16)) + bd_ref[...])

    @pl.when(halves > 0)
    def _():
        e = se_ref[i]
        prev = se_ref[jnp.maximum(i - 1, 0)]

        @pl.when(i == 0)
        def _():
            slot_ref[0] = 1
            for cp in weight_copies(e, 0):
                cp.start()

        @pl.when((i == 0) | (e != prev))
        def _():
            slot = 1 - slot_ref[0]
            slot_ref[0] = slot
            for cp in weight_copies(e, slot):
                cp.wait()
            nxt = nxt_ref[e]

            @pl.when(nxt != e)
            def _():
                for cp in weight_copies(nxt, 1 - slot):
                    cp.start()

    for n in range(1, MOE_STEP_BLOCKS + 1):
        @pl.when(halves == n)
        def _(n=n):
            ffn(slice(0, n * MOE_BLOCK))


def _experts(step_expert, step_halves, n_used, next_expert, xs, w_gu, b_gu, w_down, b_down, l):
    P, DH = xs.shape
    D = 2 * DH
    R = MOE_STEP_BLOCKS * MOE_BLOCK
    NS = P // R
    F2 = 2 * EXPERT_FF

    def row_map(i, se, sh, nu, nxt):
        return (jnp.minimum(i, nu[0] - 1), 0)

    def exp_map(i, se, sh, nu, nxt):
        return (l, se[jnp.minimum(i, nu[0] - 1)], 0, 0)

    grid_spec = pltpu.PrefetchScalarGridSpec(
        num_scalar_prefetch=4,
        grid=(NS,),
        in_specs=[pl.BlockSpec((R, DH), row_map),
                  pl.BlockSpec(memory_space=pl.ANY),
                  pl.BlockSpec((None, None, 1, F2), exp_map),
                  pl.BlockSpec(memory_space=pl.ANY),
                  pl.BlockSpec((None, None, 1, D), exp_map)],
        out_specs=pl.BlockSpec((R, DH), row_map),
        scratch_shapes=[pltpu.VMEM((2, D, F2), F32), pltpu.VMEM((2, EXPERT_FF, D), F32),
                        pltpu.SMEM((1,), jnp.int32),
                        pltpu.SemaphoreType.DMA((2, 2))],
    )
    return pl.pallas_call(
        functools.partial(_expert_kernel, l),
        out_shape=jax.ShapeDtypeStruct((P, DH), jnp.uint32),
        grid_spec=grid_spec,
        compiler_params=pltpu.CompilerParams(dimension_semantics=("arbitrary",), vmem_limit_bytes=VMEM_LIMIT),
        name="experts",
    )(step_expert, step_halves, n_used, next_expert, xs, w_gu, b_gu.reshape(DEPTH, N_EXPERTS, 1, F2), w_down,
      b_down.reshape(DEPTH, N_EXPERTS, 1, D))


def _sc_gather_rows(x, idx):
    M = idx.shape[0]
    D = x.shape[1]
    W = SC_GATHER_WINDOW
    mesh = plsc.VectorSubcoreMesh(core_axis_name="core", subcore_axis_name="subcore")
    n_workers = mesh.num_cores * mesh.num_subcores
    rows_per = M // n_workers
    assert rows_per * n_workers == M and rows_per % W == 0

    @pl.kernel(out_type=jax.ShapeDtypeStruct((M, D), x.dtype), mesh=mesh, name="sc_gather_rows",
               scratch_types=[pltpu.VMEM((rows_per,), jnp.int32), pltpu.VMEM((W, D), x.dtype)])
    def gather_kernel(x_hbm, i_hbm, o_hbm, idx_vmem, buf):
        wid = lax.axis_index("core") * mesh.num_subcores + lax.axis_index("subcore")
        base = wid * rows_per
        pltpu.sync_copy(i_hbm.at[pl.ds(base, rows_per)], idx_vmem)

        @pl.loop(0, rows_per // W)
        def _(j):
            pltpu.sync_copy(x_hbm.at[idx_vmem.at[pl.ds(j * W, W)]], buf)
            pltpu.sync_copy(buf, o_hbm.at[pl.ds(base + j * W, W)])

    return gather_kernel(x, idx)


def _sc_scatter_rows(x, idx, n_out):
    K, N = idx.shape
    D = x.shape[1]
    W = SC_GATHER_WINDOW
    mesh = plsc.VectorSubcoreMesh(core_axis_name="core", subcore_axis_name="subcore")
    n_workers = mesh.num_cores * mesh.num_subcores
    rows_per = N // n_workers
    assert rows_per * n_workers == N and rows_per % W == 0

    @pl.kernel(out_type=jax.ShapeDtypeStruct((n_out, D), x.dtype), mesh=mesh, name="sc_scatter_rows",
               scratch_types=[pltpu.VMEM((K * rows_per,), jnp.int32), pltpu.VMEM((W, D), x.dtype),
                              pltpu.SemaphoreType.DMA((K,))])
    def scatter_kernel(x_hbm, i_hbm, o_hbm, idx_vmem, buf, sem):
        wid = lax.axis_index("core") * mesh.num_subcores + lax.axis_index("subcore")
        base = wid * rows_per
        for k in range(K):
            pltpu.sync_copy(i_hbm.at[pl.ds(k * N + base, rows_per)], idx_vmem.at[pl.ds(k * rows_per, rows_per)])

        @pl.loop(0, rows_per // W)
        def _(j):
            pltpu.sync_copy(x_hbm.at[pl.ds(base + j * W, W)], buf)
            copies = [pltpu.make_async_copy(buf, o_hbm.at[idx_vmem.at[pl.ds(k * rows_per + j * W, W)]], sem.at[k])
                      for k in range(K)]
            for cp in copies:
                cp.start()
            for cp in copies:
                cp.wait()

    return scatter_kernel(x, idx.reshape(K * N))


def _combine_kernel(x_ref, yg_ref, route_ref, ln3g_ref, ln3b_ref, *rest):
    out_ref = rest[-1]
    x = x_ref[...]
    g = jnp.transpose(route_ref[...])
    y = jnp.zeros_like(x)
    for k in range(TOP_K):
        y = y + g[:, TOP_K + k:TOP_K + k + 1] * _unpack_bf16_pairs(yg_ref[k * CMB_TILE:(k + 1) * CMB_TILE, :])
    out_ref[...] = _layer_norm(DEEPNORM_ALPHA * x + y, ln3g_ref[...], ln3b_ref[...])


def _combine(x2, yg, route, p, l, part, acc):
    N, D = x2.shape
    T = CMB_TILE
    tiles = yg.shape[0] // (TOP_K * T)
    first = part * tiles
    in_specs = [pl.BlockSpec((T, D), lambda i: (first + i, 0)), pl.BlockSpec((TOP_K * T, D // 2), lambda i: (i, 0)),
                pl.BlockSpec((ROUTE_ROWS, T), lambda i: (0, first + i)), _layer_spec(p['ln3g'], l), _layer_spec(p['ln3b'], l)]
    args = [x2, yg, route, p['ln3g'], p['ln3b']]
    aliases = {}
    if acc is not None:
        in_specs.append(pl.BlockSpec(memory_space=pl.ANY))
        args.append(acc)
        aliases = {len(args) - 1: 0}
    return pl.pallas_call(
        _combine_kernel,
        out_shape=jax.ShapeDtypeStruct((N, D), F32),
        grid=(tiles,),
        in_specs=in_specs,
        out_specs=pl.BlockSpec((T, D), lambda i: (first + i, 0)),
        input_output_aliases=aliases,
        compiler_params=pltpu.CompilerParams(dimension_semantics=("arbitrary",), vmem_limit_bytes=VMEM_LIMIT),
        name="combine",
    )(*args)


def _prep(w_in, b_in, gla_wg2, gla_bg, gla_norm_g, sgu_ln_g, sgu_ln_b, sgu_ws, sgu_bs, pool_w, pool_scale,
          w_up_a, w_up_b, w_up_c, w_o, ln1_g, ln1_b, xa_wq, xa_wk, xa_wv, xa_wo, ln2_g, ln2_b,
          router_w, router_b, ln3_g, ln3_b):
    L = w_in.shape[0]
    o_glow = O_GLOW
    o_uv = o_glow + GLA_RANK
    row = lambda a: a.reshape(L, 1, -1).astype(F32)
    pad_last = lambda a, n: jnp.pad(a, [(0, 0)] * (a.ndim - 1) + [(0, n - a.shape[-1])])
    p = {}
    p['whead'], p['wglow'], p['wtail'] = _repack_w_in(w_in)
    p['bcat'] = row(jnp.concatenate([b_in[..., :o_glow], pad_last(b_in[..., o_glow:o_uv], LANES), b_in[..., o_uv:]],
                                    axis=-1))
    p['wg2'] = jnp.pad(gla_wg2, ((0, 0), (0, LANES - GLA_RANK), (0, 0))).astype(BF16)
    p['bg'] = row(gla_bg)
    p['gnorm'] = row(gla_norm_g)
    p['slng'] = row(sgu_ln_g)
    p['slnb'] = row(sgu_ln_b)
    p['wtril'] = jnp.tril(sgu_ws).astype(BF16)
    p['sbias'] = jnp.repeat(jnp.swapaxes(sgu_bs, 1, 2), SGU_GD, axis=2).astype(F32)
    G = len(POOL_WINDOWS)
    eye = jnp.eye(G, dtype=F32)
    p['poolw'] = jnp.einsum('lgcd,gh->lgchd', pool_w, eye).reshape(L, POOL_WIDTH, POOL_WIDTH).astype(BF16)
    p['pscale'] = row(pool_scale)
    p['wupa'], p['wupb'], p['wupc'], p['wo'] = w_up_a, w_up_b, w_up_c, w_o
    p['ln1g'], p['ln1b'] = row(ln1_g), row(ln1_b)
    p['wq'], p['wk'], p['wv'], p['wxo'] = xa_wq, xa_wk, xa_wv, xa_wo
    p['ln2g'], p['ln2b'] = row(ln2_g), row(ln2_b)
    rwt = jnp.swapaxes(router_w, 1, 2)
    rwt_hi = rwt.astype(BF16)
    p['rwt'] = jnp.concatenate([rwt_hi, (rwt - rwt_hi.astype(F32)).astype(BF16)], axis=1)
    p['rbt'] = jnp.broadcast_to(router_b[:, :, None], router_b.shape + (XA_TILE,)).astype(F32)
    p['ln3g'], p['ln3b'] = row(ln3_g), row(ln3_b)
    return p


def _route(route, counts):
    N = route.shape[1]
    top_idx = route[0:TOP_K].astype(jnp.int32)
    rank = route[2 * TOP_K:3 * TOP_K].astype(jnp.int32)
    counts = counts[:, 0].astype(jnp.int32)
    R = MOE_STEP_BLOCKS * MOE_BLOCK
    blocks = (counts + MOE_BLOCK - 1) // MOE_BLOCK
    padded = ((counts + R - 1) // R) * R
    pad_end = jnp.cumsum(padded)
    pad_start = pad_end - padded
    ids = jnp.arange(N_EXPERTS, dtype=jnp.int32)
    start_of = jnp.sum(jnp.where(top_idx[:, :, None] == ids[None, None, :], pad_start[None, None, :], 0), axis=-1)
    dest = start_of + rank
    n_steps = N * TOP_K // R + N_EXPERTS
    step_start = jnp.arange(n_steps, dtype=jnp.int32) * R
    step_expert = jnp.minimum(jnp.sum((pad_end[None, :] <= step_start[:, None]).astype(jnp.int32), axis=1),
                              N_EXPERTS - 1)
    mine = step_expert[:, None] == ids[None, :]
    blocks_before = (step_start - jnp.sum(jnp.where(mine, pad_start[None, :], 0), axis=1)) // MOE_BLOCK
    step_halves = jnp.clip(jnp.sum(jnp.where(mine, blocks[None, :], 0), axis=1) - blocks_before, 0, MOE_STEP_BLOCKS)
    step_halves = jnp.where(step_start < pad_end[-1], step_halves, 0).astype(jnp.int32)
    n_used = (pad_end[-1] // R).astype(jnp.int32).reshape(1)
    later = jnp.where((ids[None, :] > ids[:, None]) & (counts[None, :] > 0), ids[None, :], N_EXPERTS)
    first_later = jnp.min(later, axis=1)
    next_expert = jnp.where(first_later < N_EXPERTS, first_later, ids).astype(jnp.int32)
    return dest, step_expert, step_halves, n_used, next_expert


def kernel(x, mem, w_in, b_in, gla_wg2, gla_bg, gla_norm_g, sgu_ln_g, sgu_ln_b, sgu_ws, sgu_bs, pool_w, pool_scale, w_up_a, w_up_b, w_up_c, w_o, ln1_g, ln1_b, xa_wq, xa_wk, xa_wv, xa_wo, ln2_g, ln2_b, router_w, router_b, exp_w_gu, exp_b_gu, exp_w_down, exp_b_down, ln3_g, ln3_b):
    B, S, D = x.shape
    N = B * S
    p = _prep(w_in, b_in, gla_wg2, gla_bg, gla_norm_g, sgu_ln_g, sgu_ln_b, sgu_ws, sgu_bs, pool_w, pool_scale,
              w_up_a, w_up_b, w_up_c, w_o, ln1_g, ln1_b, xa_wq, xa_wk, xa_wv, xa_wo, ln2_g, ln2_b,
              router_w, router_b, ln3_g, ln3_b)
    memt = jnp.swapaxes(mem, 1, 2)
    for l in range(DEPTH):
        x1 = _mixer(x, p, l)
        kt, v = _memkv(mem, memt, p, l)
        x2, x2p, route, counts = _xattn(x1, kt, v, p, l)
        dest, step_expert, step_halves, n_used, next_expert = _route(route, counts)
        n_slots = N * TOP_K + N_EXPERTS * MOE_STEP_BLOCKS * MOE_BLOCK
        xs = _sc_scatter_rows(x2p.reshape(N, D // 2), dest, n_slots)
        ys = _experts(step_expert, step_halves, n_used, next_expert, xs, exp_w_gu, exp_b_gu, exp_w_down, exp_b_down, l)
        dest_km = dest.reshape(TOP_K, N // CMB_TILE, CMB_TILE).transpose(1, 0, 2).reshape(-1)
        rows = dest_km.shape[0] // COMBINE_PARTS
        acc = None
        for part in range(COMBINE_PARTS):
            yg = _sc_gather_rows(ys, dest_km[part * rows:(part + 1) * rows])
            acc = _combine(x2.reshape(N, D), yg, route, p, l, part, acc)
        x = acc.reshape(B, S, D)
    return x
```

```python
import functools

import jax
import jax.numpy as jnp
from jax import lax
from jax.experimental import pallas as pl
from jax.experimental.pallas import tpu as pltpu
from jax.experimental.pallas import tpu_sc as plsc

F32 = jnp.float32
BF16 = jnp.bfloat16

D_MODEL = 1024
DEPTH = 2
GLA_HEADS = 4
GLA_KEY = 256
GLA_VAL = 512
GLA_DK = 64
GLA_DV = 128
GLA_RANK = 16
GLA_TAU = 16.0
GLA_CHUNK = 64
SGU_GROUPS = 4
SGU_WIDTH = 256
SGU_GD = 64
SGU_CHUNK = 128
POOL_WINDOWS = (2, 4, 8, 16)
POOL_WIDTH = 256
POOL_GD = 64
POOL_CARRY = 32
MEM_LEN = 256
XA_HEADS = 4
XA_DH = 256
N_EXPERTS = 32
TOP_K = 4
EXPERT_FF = 1024
SWIGLU_LIMIT = 7.0
SWIGLU_ALPHA = 1.702
DEEPNORM_ALPHA = (2 * DEPTH) ** 0.25
LN_EPS = 1e-5
LANES = 128
VMEM_LIMIT = 56 * 1024 * 1024

MIX_TILE = 512
GLA_SUB = 256
XA_TILE = 1024
MOE_BLOCK = 256
MOE_STEP_BLOCKS = 4
CMB_TILE = 512
COMBINE_PARTS = 4
ROUTE_ROWS = 16
SC_GATHER_WINDOW = 128

O_QKVR = 0
O_GLOW = 2 * GLA_KEY + 2 * GLA_VAL
O_UV = O_GLOW + LANES
O_XC = O_UV + 2 * SGU_WIDTH
O_GATE = O_XC + POOL_WIDTH


def _dot(a, b):
    return jnp.dot(a, b, preferred_element_type=F32)


def _dot_t0(a, b):
    return lax.dot_general(a, b, (((0,), (0,)), ((), ())), preferred_element_type=F32)


def _dot_t1(a, b):
    return lax.dot_general(a, b, (((1,), (1,)), ((), ())), preferred_element_type=F32)


def _split_bf16(x):
    hi = x.astype(BF16)
    lo = (x - hi.astype(F32)).astype(BF16)
    return hi, lo


def _layer_norm(x, g, b):
    mu = jnp.mean(x, axis=-1, keepdims=True)
    xc = x - mu
    var = jnp.mean(xc * xc, axis=-1, keepdims=True)
    return xc * lax.rsqrt(var + LN_EPS) * g + b


def _sigmoid(x):
    return 1.0 / (1.0 + jnp.exp(-x))


def _pack_bf16_pairs(x):
    H = x.shape[1] // 2
    bits = lax.bitcast_convert_type(x.astype(BF16).astype(F32), jnp.uint32)
    return (bits[:, :H] >> 16) | (bits[:, H:] & jnp.uint32(0xFFFF0000))


def _unpack_bf16_pairs(w):
    lo = lax.bitcast_convert_type(w << 16, F32)
    hi = lax.bitcast_convert_type(w & jnp.uint32(0xFFFF0000), F32)
    return jnp.concatenate([lo, hi], axis=1)


def _layer_spec(arr, l):
    nd = arr.ndim - 1
    return pl.BlockSpec((None,) + arr.shape[1:], lambda *_: (l,) + (0,) * nd, pipeline_mode=pl.Buffered(1))


def _mixer_kernel(x_ref, whead_ref, wglow_ref, wtail_ref, bcat_ref, wg2_ref, bg_ref, gnorm_ref,
                  slng_ref, slnb_ref, wtril_ref, sbias_ref, poolw_ref, pscale_ref,
                  wupa_ref, wupb_ref, wupc_ref, wo_ref, ln1g_ref, ln1b_ref,
                  out_ref,
                  state_ref, qkvr_ref, ya_ref, vln_ref, e_ref, s2_ref, s4_ref, s8_ref):
    T = MIX_TILE
    D = D_MODEL
    j = pl.program_id(1)
    x = x_ref[0]
    xb = x.astype(BF16)

    def proj(lo, hi):
        if hi <= O_GLOW:
            w = whead_ref[:, lo:hi]
        elif lo == O_GLOW and hi == O_UV:
            w = wglow_ref[...]
        else:
            w = wtail_ref[:, lo - O_UV:hi - O_UV]
        return _dot(xb, w) + bcat_ref[:, lo:hi]

    @pl.when(j == 0)
    def _():
        state_ref[...] = jnp.zeros_like(state_ref)
        e_ref[0:POOL_CARRY, :] = jnp.zeros((POOL_CARRY, POOL_WIDTH), F32)

    @pl.when(j > 0)
    def _():
        e_ref[0:POOL_CARRY, :] = e_ref[T:T + POOL_CARRY, :]

    qkvr_ref[...] = proj(O_QKVR, O_GLOW)
    glow = proj(O_GLOW, O_UV)
    z = _dot(glow.astype(BF16), wg2_ref[...]) + bg_ref[...]
    la = (jnp.minimum(z, 0.0) - jnp.log1p(jnp.exp(-jnp.abs(z)))) * (1.0 / GLA_TAU)
    la_hi, la_lo = _split_bf16(la)
    gate_a = _sigmoid(proj(O_GATE, O_GATE + D))

    C = GLA_CHUNK
    G = GLA_SUB
    NC = G // C
    CSH = C.bit_length() - 1
    row = lax.broadcasted_iota(jnp.int32, (G, G), 0)
    col = lax.broadcasted_iota(jnp.int32, (G, G), 1)
    same_chunk = (row >> CSH) == (col >> CSH)
    causal = same_chunk & (row >= col)
    causal_bf = jnp.where(causal, 1.0, 0.0).astype(BF16)
    lane = lax.broadcasted_iota(jnp.int32, (G, GLA_KEY), 1)
    gate_b = None
    for g0 in range(0, T, G):
        gr = slice(g0, g0 + G)
        lh, ll = la_hi[gr], la_lo[gr]
        b = _dot(causal_bf, lh) + _dot(causal_bf, ll)
        b_last = [b[(c + 1) * C - 1:(c + 1) * C, :] for c in range(NC)]
        b_end = jnp.concatenate([jnp.broadcast_to(r, (C, GLA_KEY)) for r in b_last], axis=0)
        q = qkvr_ref[gr, 0:GLA_KEY]
        k = qkvr_ref[gr, GLA_KEY:2 * GLA_KEY]
        v = qkvr_ref[gr, 2 * GLA_KEY:2 * GLA_KEY + GLA_VAL].astype(BF16)
        q_dec = q * (GLA_DK ** -0.5) * jnp.exp(b)
        k_dec = (k * jnp.exp(-b)).astype(BF16)
        k_tail = (k * jnp.exp(b_end - b)).astype(BF16)
        q_dec_bf = q_dec.astype(BF16)
        o_heads = []
        for h in range(GLA_HEADS):
            q_h = jnp.where((lane >= h * GLA_DK) & (lane < (h + 1) * GLA_DK), q_dec, 0.0).astype(BF16)
            scores = jnp.where(causal, _dot_t1(q_h, k_dec), 0.0).astype(BF16)
            o_heads.append(_dot(scores, v[:, h * GLA_DV:(h + 1) * GLA_DV]))
        o_intra = jnp.concatenate(o_heads, axis=1)
        if gate_b is None:
            gate_b = _sigmoid(proj(O_GATE + D, O_GATE + 2 * D))
        dec_cols = jnp.exp(jnp.concatenate(b_last + [jnp.zeros((8 - NC, GLA_KEY), F32)], axis=0)).T
        dec_all = jnp.concatenate([jnp.broadcast_to(dec_cols[:, c:c + 1], (GLA_KEY, LANES)) for c in range(NC)], axis=1)
        o_inter = []
        for c in range(NC):
            rows = slice(c * C, (c + 1) * C)
            o_inter.append(_dot(q_dec_bf[rows], state_ref[...].astype(BF16)))
            kv = _dot_t0(k_tail[rows], v[rows])
            for h in range(GLA_HEADS):
                rs = slice(h * GLA_DK, (h + 1) * GLA_DK)
                cs = slice(h * GLA_DV, (h + 1) * GLA_DV)
                state_ref[rs, cs] = dec_all[rs, c * LANES:(c + 1) * LANES] * state_ref[rs, cs] + kv[rs, cs]
        o = o_intra + jnp.concatenate(o_inter, axis=0)
        for h in range(GLA_HEADS):
            cs = slice(h * GLA_DV, (h + 1) * GLA_DV)
            o_h = o[:, cs]
            ms = jnp.mean(o_h * o_h, axis=-1, keepdims=True)
            o_h = o_h * lax.rsqrt(ms + LN_EPS) * gnorm_ref[:, cs]
            r_h = qkvr_ref[gr, 2 * GLA_KEY + GLA_VAL + h * GLA_DV:2 * GLA_KEY + GLA_VAL + (h + 1) * GLA_DV]
            ya_ref[gr, cs] = (o_h * (r_h * _sigmoid(r_h))).astype(BF16)

    gate_c = _sigmoid(proj(O_GATE + 2 * D, O_GATE + 3 * D))
    uv = proj(O_UV, O_XC)
    zg = 0.5 * uv * (1.0 + lax.erf(uv * (2.0 ** -0.5)))
    u = zg[:, :SGU_WIDTH]
    vln_ref[...] = _layer_norm(zg[:, SGU_WIDTH:], slng_ref[...], slnb_ref[...])
    lane_s = lax.broadcasted_iota(jnp.int32, (SGU_CHUNK, SGU_WIDTH), 1)
    s_parts = []
    for n in range(T // SGU_CHUNK):
        vc = vln_ref[n * SGU_CHUNK:(n + 1) * SGU_CHUNK, :]
        s = sbias_ref[...]
        for g in range(SGU_GROUPS):
            vg = jnp.where((lane_s >= g * SGU_GD) & (lane_s < (g + 1) * SGU_GD), vc, 0.0).astype(BF16)
            s = s + _dot(wtril_ref[g], vg)
        s_parts.append(s)
    y_b = (u * jnp.concatenate(s_parts, axis=0)).astype(BF16)

    P = POOL_CARRY
    xc = proj(O_XC, O_GATE)
    e_ref[P:P + T, :] = xc
    s2_ref[8:P + T, :] = e_ref[8:P + T, :] + e_ref[7:P + T - 1, :]
    s4_ref[16:P + T, :] = s2_ref[16:P + T, :] + s2_ref[14:P + T - 2, :]
    s8_ref[24:P + T, :] = s4_ref[24:P + T, :] + s4_ref[20:P + T - 4, :]
    s16 = s8_ref[P:P + T, :] + s8_ref[P - 8:P + T - 8, :]
    lane_p = lax.broadcasted_iota(jnp.int32, (T, POOL_WIDTH), 1)
    tpos = lax.broadcasted_iota(jnp.int32, (T, POOL_WIDTH), 0) + (j * T + 1)
    grp = lane_p >> (POOL_GD.bit_length() - 1)
    win = jnp.where(grp == 0, POOL_WINDOWS[0], jnp.where(grp == 1, POOL_WINDOWS[1],
                    jnp.where(grp == 2, POOL_WINDOWS[2], POOL_WINDOWS[3])))
    wsum = jnp.where(grp == 0, s2_ref[P:P + T, :], jnp.where(grp == 1, s4_ref[P:P + T, :],
                     jnp.where(grp == 2, s8_ref[P:P + T, :], s16)))
    count = jnp.minimum(tpos, win).astype(F32)
    pooled = wsum / count - xc
    y_c = (_dot(pooled.astype(BF16), poolw_ref[...]) * pscale_ref[...]).astype(BF16)

    merged = gate_a * _dot(ya_ref[...], wupa_ref[...].astype(BF16))
    merged += gate_b * _dot(y_b, wupb_ref[...].astype(BF16))
    merged += gate_c * _dot(y_c, wupc_ref[...].astype(BF16))
    h = _dot(merged.astype(BF16), wo_ref[...].astype(BF16))
    out_ref[0] = _layer_norm(DEEPNORM_ALPHA * x + h, ln1g_ref[...], ln1b_ref[...])


_MIXER_WEIGHTS = ('whead', 'wglow', 'wtail', 'bcat', 'wg2', 'bg', 'gnorm', 'slng', 'slnb', 'wtril', 'sbias', 'poolw', 'pscale',
                  'wupa', 'wupb', 'wupc', 'wo', 'ln1g', 'ln1b')


def _mixer(x, p, l):
    B, S, D = x.shape
    T = MIX_TILE
    weights = [p[n] for n in _MIXER_WEIGHTS]
    return pl.pallas_call(
        _mixer_kernel,
        out_shape=jax.ShapeDtypeStruct((B, S, D), F32),
        grid=(B, S // T),
        in_specs=[pl.BlockSpec((1, T, D), lambda b, j: (b, j, 0))] + [_layer_spec(w, l) for w in weights],
        out_specs=pl.BlockSpec((1, T, D), lambda b, j: (b, j, 0)),
        scratch_shapes=[
            pltpu.VMEM((GLA_KEY, GLA_VAL), F32),
            pltpu.VMEM((T, 2 * GLA_KEY + 2 * GLA_VAL), F32),
            pltpu.VMEM((T, GLA_VAL), BF16),
            pltpu.VMEM((T, SGU_WIDTH), F32),
            pltpu.VMEM((T + POOL_CARRY, POOL_WIDTH), F32),
            pltpu.VMEM((T + POOL_CARRY, POOL_WIDTH), F32),
            pltpu.VMEM((T + POOL_CARRY, POOL_WIDTH), F32),
            pltpu.VMEM((T + POOL_CARRY, POOL_WIDTH), F32),
        ],
        compiler_params=pltpu.CompilerParams(dimension_semantics=("arbitrary", "arbitrary"),
                                             vmem_limit_bytes=VMEM_LIMIT),
        name="mixer",
    )(x, *weights)


def _memkv_kernel(memt_ref, mem_ref, wk_ref, wv_ref, kt_ref, v_ref):
    kt_ref[0] = _dot_t0(wk_ref[...].astype(BF16), memt_ref[0].astype(BF16)).astype(BF16)
    v_ref[0] = _dot(mem_ref[0].astype(BF16), wv_ref[...].astype(BF16)).astype(BF16)


def _memkv(mem, memt, p, l):
    B, M, D = mem.shape
    return pl.pallas_call(
        _memkv_kernel,
        out_shape=(jax.ShapeDtypeStruct((B, D, M), BF16), jax.ShapeDtypeStruct((B, M, D), BF16)),
        grid=(B,),
        in_specs=[pl.BlockSpec((1, D, M), lambda b: (b, 0, 0)), pl.BlockSpec((1, M, D), lambda b: (b, 0, 0)),
                  _layer_spec(p['wk'], l), _layer_spec(p['wv'], l)],
        out_specs=(pl.BlockSpec((1, D, M), lambda b: (b, 0, 0)), pl.BlockSpec((1, M, D), lambda b: (b, 0, 0))),
        compiler_params=pltpu.CompilerParams(dimension_semantics=("arbitrary",), vmem_limit_bytes=VMEM_LIMIT),
        name="memkv",
    )(memt, mem, p['wk'], p['wv'])


def _xattn_kernel(x_ref, kt_ref, v_ref, wq_ref, wo_ref, ln2g_ref, ln2b_ref, rwt_ref, rbt_ref,
                  x2_ref, x2p_ref, route_ref, counts_ref, carry_ref):
    T = XA_TILE

    @pl.when((pl.program_id(0) == 0) & (pl.program_id(1) == 0))
    def _():
        carry_ref[...] = jnp.zeros_like(carry_ref)

    x = x_ref[0]
    q = (_dot(x.astype(BF16), wq_ref[...].astype(BF16)) * (XA_DH ** -0.5)).astype(BF16)
    h = jnp.zeros_like(x)
    for hd in range(XA_HEADS):
        cs = slice(hd * XA_DH, (hd + 1) * XA_DH)
        s = _dot(q[:, cs], kt_ref[0, cs, :])
        e = jnp.exp(s - jnp.max(s, axis=-1, keepdims=True))
        o = _dot(e.astype(BF16), v_ref[0, :, cs]) / jnp.sum(e, axis=-1, keepdims=True)
        h = h + _dot(o.astype(BF16), wo_ref[cs, :].astype(BF16))
    x2 = _layer_norm(DEEPNORM_ALPHA * x + h, ln2g_ref[...], ln2b_ref[...])
    x2_ref[0] = x2
    x2p_ref[0] = _pack_bf16_pairs(x2)

    E = N_EXPERTS
    hi, lo = _split_bf16(x2)
    lt = _dot_t1(rwt_ref[...], hi)
    logits = lt[0:E] + (lt[E:2 * E] + _dot_t1(rwt_ref[0:E, :], lo)) + rbt_ref[...]

    eid = lax.broadcasted_iota(jnp.int32, (E, T), 0)
    neg_inf = jnp.float32(-jnp.inf)
    rest = logits
    tops, picks = [], []
    for _ in range(TOP_K):
        m = jnp.max(rest, axis=0, keepdims=True)
        idx = jnp.min(jnp.where(rest == m, eid, E), axis=0, keepdims=True)
        pick = eid == idx
        rest = jnp.where(pick, neg_inf, rest)
        tops.append((m, idx))
        picks.append(pick)
    exps = [jnp.exp(m - tops[0][0]) for m, _ in tops]
    denom = exps[0]
    for e in exps[1:]:
        denom = denom + e

    chosen = jnp.zeros((E, T), F32)
    for pick in picks:
        chosen = chosen + jnp.where(pick, 1.0, 0.0)
    chosen_bf = chosen.astype(BF16)
    earlier = (lax.broadcasted_iota(jnp.int32, (T, T), 0) < lax.broadcasted_iota(jnp.int32, (T, T), 1))
    carry = carry_ref[...]
    before = _dot(chosen_bf, jnp.where(earlier, 1.0, 0.0).astype(BF16)) + jnp.concatenate([carry] * (T // LANES), axis=1)
    carry = carry + _dot(chosen_bf, jnp.ones((T, LANES), BF16))
    carry_ref[...] = carry
    counts_ref[...] = carry

    rid = lax.broadcasted_iota(jnp.int32, (ROUTE_ROWS, T), 0)
    route = jnp.zeros((ROUTE_ROWS, T), F32)
    for k in range(TOP_K):
        rank = jnp.sum(jnp.where(picks[k], before, 0.0), axis=0, keepdims=True)
        route = jnp.where(rid == k, tops[k][1].astype(F32), route)
        route = jnp.where(rid == TOP_K + k, exps[k] / denom, route)
        route = jnp.where(rid == 2 * TOP_K + k, rank, route)
    route_ref[...] = route


def _xattn(x, kt, v, p, l):
    B, S, D = x.shape
    T = XA_TILE
    M = MEM_LEN
    weights = [p[n] for n in ('wq', 'wxo', 'ln2g', 'ln2b', 'rwt', 'rbt')]
    return pl.pallas_call(
        _xattn_kernel,
        out_shape=(jax.ShapeDtypeStruct((B, S, D), F32), jax.ShapeDtypeStruct((B, S, D // 2), jnp.uint32),
                   jax.ShapeDtypeStruct((ROUTE_ROWS, B * S), F32), jax.ShapeDtypeStruct((N_EXPERTS, LANES), F32)),
        grid=(B, S // T),
        in_specs=[pl.BlockSpec((1, T, D), lambda b, j: (b, j, 0)),
                  pl.BlockSpec((1, D, M), lambda b, j: (b, 0, 0)),
                  pl.BlockSpec((1, M, D), lambda b, j: (b, 0, 0))] + [_layer_spec(w, l) for w in weights],
        out_specs=(pl.BlockSpec((1, T, D), lambda b, j: (b, j, 0)),
                   pl.BlockSpec((1, T, D // 2), lambda b, j: (b, j, 0)),
                   pl.BlockSpec((ROUTE_ROWS, T), lambda b, j: (0, b * (S // T) + j)),
                   pl.BlockSpec((N_EXPERTS, LANES), lambda b, j: (0, 0))),
        scratch_shapes=[pltpu.VMEM((N_EXPERTS, LANES), F32)],
        compiler_params=pltpu.CompilerParams(dimension_semantics=("arbitrary", "arbitrary"),
                                             vmem_limit_bytes=VMEM_LIMIT),
        name="xattn",
    )(x, kt, v, *weights)


def _expert_kernel(layer, se_ref, sh_ref, nu_ref, nxt_ref, xs_ref, wgu_hbm, bgu_ref, wd_hbm, bd_ref, out_ref,
                   wgu_st, wd_st, slot_ref, sem):
    i = pl.program_id(0)
    F = EXPERT_FF
    halves = sh_ref[i]

    def weight_copies(e, slot):
        return (pltpu.make_async_copy(wgu_hbm.at[layer, e], wgu_st.at[slot], sem.at[slot, 0]),
                pltpu.make_async_copy(wd_hbm.at[layer, e], wd_st.at[slot], sem.at[slot, 1]))

    def ffn(rows):
        slot = slot_ref[0]
        xsb = _unpack_bf16_pairs(xs_ref[rows, :]).astype(BF16)
        hh = _dot(xsb, wgu_st[slot].astype(BF16)) + bgu_ref[...]
        h_glu = jnp.minimum(hh[:, :F], SWIGLU_LIMIT)
        h_lin = jnp.clip(hh[:, F:], -SWIGLU_LIMIT, SWIGLU_LIMIT)
        a = h_glu * _sigmoid(SWIGLU_ALPHA * h_glu) * (h_lin + 1.0)
        out_ref[rows, :] = _pack_bf16_pairs(_dot(a.astype(BF16), wd_st[slot].astype(BF16)) + bd_ref[...])

    @pl.when(halves > 0)
    def _():
        e = se_ref[i]
        prev = se_ref[jnp.maximum(i - 1, 0)]

        @pl.when(i == 0)
        def _():
            slot_ref[0] = 1
            for cp in weight_copies(e, 0):
                cp.start()

        @pl.when((i == 0) | (e != prev))
        def _():
            slot = 1 - slot_ref[0]
            slot_ref[0] = slot
            for cp in weight_copies(e, slot):
                cp.wait()
            nxt = nxt_ref[e]

            @pl.when(nxt != e)
            def _():
                for cp in weight_copies(nxt, 1 - slot):
                    cp.start()

    for n in range(1, MOE_STEP_BLOCKS + 1):
        @pl.when(halves == n)
        def _(n=n):
            ffn(slice(0, n * MOE_BLOCK))


def _experts(step_expert, step_halves, n_used, next_expert, xs, w_gu, b_gu, w_down, b_down, l):
    P, DH = xs.shape
    D = 2 * DH
    R = MOE_STEP_BLOCKS * MOE_BLOCK
    NS = P // R
    F2 = 2 * EXPERT_FF

    def row_map(i, se, sh, nu, nxt):
        return (jnp.minimum(i, nu[0] - 1), 0)

    def exp_map(i, se, sh, nu, nxt):
        return (l, se[jnp.minimum(i, nu[0] - 1)], 0, 0)

    grid_spec = pltpu.PrefetchScalarGridSpec(
        num_scalar_prefetch=4,
        grid=(NS,),
        in_specs=[pl.BlockSpec((R, DH), row_map),
                  pl.BlockSpec(memory_space=pl.ANY),
                  pl.BlockSpec((None, None, 1, F2), exp_map),
                  pl.BlockSpec(memory_space=pl.ANY),
                  pl.BlockSpec((None, None, 1, D), exp_map)],
        out_specs=pl.BlockSpec((R, DH), row_map),
        scratch_shapes=[pltpu.VMEM((2, D, F2), F32), pltpu.VMEM((2, EXPERT_FF, D), F32),
                        pltpu.SMEM((1,), jnp.int32),
                        pltpu.SemaphoreType.DMA((2, 2))],
    )
    return pl.pallas_call(
        functools.partial(_expert_kernel, l),
        out_shape=jax.ShapeDtypeStruct((P, DH), jnp.uint32),
        grid_spec=grid_spec,
        compiler_params=pltpu.CompilerParams(dimension_semantics=("arbitrary",), vmem_limit_bytes=VMEM_LIMIT),
        name="experts",
    )(step_expert, step_halves, n_used, next_expert, xs, w_gu, b_gu.reshape(DEPTH, N_EXPERTS, 1, F2), w_down,
      b_down.reshape(DEPTH, N_EXPERTS, 1, D))


def _sc_gather_rows(x, idx):
    M = idx.shape[0]
    D = x.shape[1]
    W = SC_GATHER_WINDOW
    mesh = plsc.VectorSubcoreMesh(core_axis_name="core", subcore_axis_name="subcore")
    n_workers = mesh.num_cores * mesh.num_subcores
    rows_per = M // n_workers
    assert rows_per * n_workers == M and rows_per % W == 0

    @pl.kernel(out_type=jax.ShapeDtypeStruct((M, D), x.dtype), mesh=mesh, name="sc_gather_rows",
               scratch_types=[pltpu.VMEM((rows_per,), jnp.int32), pltpu.VMEM((W, D), x.dtype)])
    def gather_kernel(x_hbm, i_hbm, o_hbm, idx_vmem, buf):
        wid = lax.axis_index("core") * mesh.num_subcores + lax.axis_index("subcore")
        base = wid * rows_per
        pltpu.sync_copy(i_hbm.at[pl.ds(base, rows_per)], idx_vmem)

        @pl.loop(0, rows_per // W)
        def _(j):
            pltpu.sync_copy(x_hbm.at[idx_vmem.at[pl.ds(j * W, W)]], buf)
            pltpu.sync_copy(buf, o_hbm.at[pl.ds(base + j * W, W)])

    return gather_kernel(x, idx)


def _sc_scatter_rows(x, idx, n_out):
    K, N = idx.shape
    D = x.shape[1]
    W = SC_GATHER_WINDOW
    mesh = plsc.VectorSubcoreMesh(core_axis_name="core", subcore_axis_name="subcore")
    n_workers = mesh.num_cores * mesh.num_subcores
    rows_per = N // n_workers
    assert rows_per * n_workers == N and rows_per % W == 0

    @pl.kernel(out_type=jax.ShapeDtypeStruct((n_out, D), x.dtype), mesh=mesh, name="sc_scatter_rows",
               scratch_types=[pltpu.VMEM((K * rows_per,), jnp.int32), pltpu.VMEM((W, D), x.dtype),
                              pltpu.SemaphoreType.DMA((K,))])
    def scatter_kernel(x_hbm, i_hbm, o_hbm, idx_vmem, buf, sem):
        wid = lax.axis_index("core") * mesh.num_subcores + lax.axis_index("subcore")
        base = wid * rows_per
        for k in range(K):
            pltpu.sync_copy(i_hbm.at[pl.ds(k * N + base, rows_per)], idx_vmem.at[pl.ds(k * rows_per, rows_per)])

        @pl.loop(0, rows_per // W)
        def _(j):
            pltpu.sync_copy(x_hbm.at[pl.ds(base + j * W, W)], buf)
            copies = [pltpu.make_async_copy(buf, o_hbm.at[idx_vmem.at[pl.ds(k * rows_per + j * W, W)]], sem.at[k])
                      for k in range(K)]
            for cp in copies:
                cp.start()
            for cp in copies:
                cp.wait()

    return scatter_kernel(x, idx.reshape(K * N))


def _combine_kernel(x_ref, yg_ref, route_ref, ln3g_ref, ln3b_ref, *rest):
    out_ref = rest[-1]
    x = x_ref[...]
    g = jnp.transpose(route_ref[...])
    y = jnp.zeros_like(x)
    for k in range(TOP_K):
        y = y + g[:, TOP_K + k:TOP_K + k + 1] * _unpack_bf16_pairs(yg_ref[k * CMB_TILE:(k + 1) * CMB_TILE, :])
    out_ref[...] = _layer_norm(DEEPNORM_ALPHA * x + y, ln3g_ref[...], ln3b_ref[...])


def _combine(x2, yg, route, p, l, part, acc):
    N, D = x2.shape
    T = CMB_TILE
    tiles = yg.shape[0] // (TOP_K * T)
    first = part * tiles
    in_specs = [pl.BlockSpec((T, D), lambda i: (first + i, 0)), pl.BlockSpec((TOP_K * T, D // 2), lambda i: (i, 0)),
                pl.BlockSpec((ROUTE_ROWS, T), lambda i: (0, first + i)), _layer_spec(p['ln3g'], l), _layer_spec(p['ln3b'], l)]
    args = [x2, yg, route, p['ln3g'], p['ln3b']]
    aliases = {}
    if acc is not None:
        in_specs.append(pl.BlockSpec(memory_space=pl.ANY))
        args.append(acc)
        aliases = {len(args) - 1: 0}
    return pl.pallas_call(
        _combine_kernel,
        out_shape=jax.ShapeDtypeStruct((N, D), F32),
        grid=(tiles,),
        in_specs=in_specs,
        out_specs=pl.BlockSpec((T, D), lambda i: (first + i, 0)),
        input_output_aliases=aliases,
        compiler_params=pltpu.CompilerParams(dimension_semantics=("arbitrary",), vmem_limit_bytes=VMEM_LIMIT),
        name="combine",
    )(*args)


def _prep(w_in, b_in, gla_wg2, gla_bg, gla_norm_g, sgu_ln_g, sgu_ln_b, sgu_ws, sgu_bs, pool_w, pool_scale,
          w_up_a, w_up_b, w_up_c, w_o, ln1_g, ln1_b, xa_wq, xa_wk, xa_wv, xa_wo, ln2_g, ln2_b,
          router_w, router_b, ln3_g, ln3_b):
    L = w_in.shape[0]
    o_glow = O_GLOW
    o_uv = o_glow + GLA_RANK
    row = lambda a: a.reshape(L, 1, -1).astype(F32)
    pad_last = lambda a, n: jnp.pad(a, [(0, 0)] * (a.ndim - 1) + [(0, n - a.shape[-1])])
    p = {}
    p['whead'] = w_in[..., :o_glow].astype(BF16)
    p['wglow'] = pad_last(w_in[..., o_glow:o_uv], LANES).astype(BF16)
    p['wtail'] = w_in[..., o_uv:].astype(BF16)
    p['bcat'] = row(jnp.concatenate([b_in[..., :o_glow], pad_last(b_in[..., o_glow:o_uv], LANES), b_in[..., o_uv:]],
                                    axis=-1))
    p['wg2'] = jnp.pad(gla_wg2, ((0, 0), (0, LANES - GLA_RANK), (0, 0))).astype(BF16)
    p['bg'] = row(gla_bg)
    p['gnorm'] = row(gla_norm_g)
    p['slng'] = row(sgu_ln_g)
    p['slnb'] = row(sgu_ln_b)
    p['wtril'] = jnp.tril(sgu_ws).astype(BF16)
    p['sbias'] = jnp.repeat(jnp.swapaxes(sgu_bs, 1, 2), SGU_GD, axis=2).astype(F32)
    G = len(POOL_WINDOWS)
    eye = jnp.eye(G, dtype=F32)
    p['poolw'] = jnp.einsum('lgcd,gh->lgchd', pool_w, eye).reshape(L, POOL_WIDTH, POOL_WIDTH).astype(BF16)
    p['pscale'] = row(pool_scale)
    p['wupa'], p['wupb'], p['wupc'], p['wo'] = w_up_a, w_up_b, w_up_c, w_o
    p['ln1g'], p['ln1b'] = row(ln1_g), row(ln1_b)
    p['wq'], p['wk'], p['wv'], p['wxo'] = xa_wq, xa_wk, xa_wv, xa_wo
    p['ln2g'], p['ln2b'] = row(ln2_g), row(ln2_b)
    rwt = jnp.swapaxes(router_w, 1, 2)
    rwt_hi = rwt.astype(BF16)
    p['rwt'] = jnp.concatenate([rwt_hi, (rwt - rwt_hi.astype(F32)).astype(BF16)], axis=1)
    p['rbt'] = jnp.broadcast_to(router_b[:, :, None], router_b.shape + (XA_TILE,)).astype(F32)
    p['ln3g'], p['ln3b'] = row(ln3_g), row(ln3_b)
    return p


def _route(route, counts):
    N = route.shape[1]
    top_idx = route[0:TOP_K].astype(jnp.int32)
    rank = route[2 * TOP_K:3 * TOP_K].astype(jnp.int32)
    counts = counts[:, 0].astype(jnp.int32)
    R = MOE_STEP_BLOCKS * MOE_BLOCK
    blocks = (counts + MOE_BLOCK - 1) // MOE_BLOCK
    padded = ((counts + R - 1) // R) * R
    pad_end = jnp.cumsum(padded)
    pad_start = pad_end - padded
    ids = jnp.arange(N_EXPERTS, dtype=jnp.int32)
    start_of = jnp.sum(jnp.where(top_idx[:, :, None] == ids[None, None, :], pad_start[None, None, :], 0), axis=-1)
    dest = start_of + rank
    n_steps = N * TOP_K // R + N_EXPERTS
    step_start = jnp.arange(n_steps, dtype=jnp.int32) * R
    step_expert = jnp.minimum(jnp.sum((pad_end[None, :] <= step_start[:, None]).astype(jnp.int32), axis=1),
                              N_EXPERTS - 1)
    mine = step_expert[:, None] == ids[None, :]
    blocks_before = (step_start - jnp.sum(jnp.where(mine, pad_start[None, :], 0), axis=1)) // MOE_BLOCK
    step_halves = jnp.clip(jnp.sum(jnp.where(mine, blocks[None, :], 0), axis=1) - blocks_before, 0, MOE_STEP_BLOCKS)
    step_halves = jnp.where(step_start < pad_end[-1], step_halves, 0).astype(jnp.int32)
    n_used = (pad_end[-1] // R).astype(jnp.int32).reshape(1)
    later = jnp.where((ids[None, :] > ids[:, None]) & (counts[None, :] > 0), ids[None, :], N_EXPERTS)
    first_later = jnp.min(later, axis=1)
    next_expert = jnp.where(first_later < N_EXPERTS, first_later, ids).astype(jnp.int32)
    return dest, step_expert, step_halves, n_used, next_expert


def kernel(x, mem, w_in, b_in, gla_wg2, gla_bg, gla_norm_g, sgu_ln_g, sgu_ln_b, sgu_ws, sgu_bs, pool_w, pool_scale, w_up_a, w_up_b, w_up_c, w_o, ln1_g, ln1_b, xa_wq, xa_wk, xa_wv, xa_wo, ln2_g, ln2_b, router_w, router_b, exp_w_gu, exp_b_gu, exp_w_down, exp_b_down, ln3_g, ln3_b):
    B, S, D = x.shape
    N = B * S
    p = _prep(w_in, b_in, gla_wg2, gla_bg, gla_norm_g, sgu_ln_g, sgu_ln_b, sgu_ws, sgu_bs, pool_w, pool_scale,
              w_up_a, w_up_b, w_up_c, w_o, ln1_g, ln1_b, xa_wq, xa_wk, xa_wv, xa_wo, ln2_g, ln2_b,
              router_w, router_b, ln3_g, ln3_b)
    memt = jnp.swapaxes(mem, 1, 2)
    for l in range(DEPTH):
        x1 = _mixer(x, p, l)
        kt, v = _memkv(mem, memt, p, l)
        x2, x2p, route, counts = _xattn(x1, kt, v, p, l)
        dest, step_expert, step_halves, n_used, next_expert = _route(route, counts)
        n_slots = N * TOP_K + N_EXPERTS * MOE_STEP_BLOCKS * MOE_BLOCK
        xs = _sc_scatter_rows(x2p.reshape(N, D // 2), dest, n_slots)
        ys = _experts(step_expert, step_halves, n_used, next_expert, xs, exp_w_gu, exp_b_gu, exp_w_down, exp_b_down, l)
        dest_km = dest.reshape(TOP_K, N // CMB_TILE, CMB_TILE).transpose(1, 0, 2).reshape(-1)
        rows = dest_km.shape[0] // COMBINE_PARTS
        acc = None
        for part in range(COMBINE_PARTS):
            yg = _sc_gather_rows(ys, dest_km[part * rows:(part + 1) * rows])
            acc = _combine(x2.reshape(N, D), yg, route, p, l, part, acc)
        x = acc.reshape(B, S, D)
    return x
```

```python
import functools

import jax
import jax.numpy as jnp
from jax import lax
from jax.experimental import pallas as pl
from jax.experimental.pallas import tpu as pltpu
from jax.experimental.pallas import tpu_sc as plsc

F32 = jnp.float32
BF16 = jnp.bfloat16

D_MODEL = 1024
DEPTH = 2
GLA_HEADS = 4
GLA_KEY = 256
GLA_VAL = 512
GLA_DK = 64
GLA_DV = 128
GLA_RANK = 16
GLA_TAU = 16.0
GLA_CHUNK = 64
SGU_GROUPS = 4
SGU_WIDTH = 256
SGU_GD = 64
SGU_CHUNK = 128
POOL_WINDOWS = (2, 4, 8, 16)
POOL_WIDTH = 256
POOL_GD = 64
POOL_CARRY = 32
MEM_LEN = 256
XA_HEADS = 4
XA_DH = 256
N_EXPERTS = 32
TOP_K = 4
EXPERT_FF = 1024
SWIGLU_LIMIT = 7.0
SWIGLU_ALPHA = 1.702
DEEPNORM_ALPHA = (2 * DEPTH) ** 0.25
LN_EPS = 1e-5
LANES = 128
VMEM_LIMIT = 56 * 1024 * 1024

MIX_TILE = 512
GLA_SUB = 256
XA_TILE = 1024
MOE_BLOCK = 256
MOE_STEP_BLOCKS = 4
CMB_TILE = 512
COMBINE_PARTS = 8
ROUTE_ROWS = 16
SC_GATHER_WINDOW = 128

O_QKVR = 0
O_GLOW = 2 * GLA_KEY + 2 * GLA_VAL
O_UV = O_GLOW + LANES
O_XC = O_UV + 2 * SGU_WIDTH
O_GATE = O_XC + POOL_WIDTH


def _dot(a, b):
    return jnp.dot(a, b, preferred_element_type=F32)


def _dot_t0(a, b):
    return lax.dot_general(a, b, (((0,), (0,)), ((), ())), preferred_element_type=F32)


def _dot_t1(a, b):
    return lax.dot_general(a, b, (((1,), (1,)), ((), ())), preferred_element_type=F32)


def _split_bf16(x):
    hi = x.astype(BF16)
    lo = (x - hi.astype(F32)).astype(BF16)
    return hi, lo


def _layer_norm(x, g, b):
    mu = jnp.mean(x, axis=-1, keepdims=True)
    xc = x - mu
    var = jnp.mean(xc * xc, axis=-1, keepdims=True)
    return xc * lax.rsqrt(var + LN_EPS) * g + b


def _sigmoid(x):
    return 1.0 / (1.0 + jnp.exp(-x))


def _pack_bf16_pairs(x):
    H = x.shape[1] // 2
    bits = lax.bitcast_convert_type(x.astype(BF16).astype(F32), jnp.uint32)
    return (bits[:, :H] >> 16) | (bits[:, H:] & jnp.uint32(0xFFFF0000))


def _unpack_bf16_pairs(w):
    lo = lax.bitcast_convert_type(w << 16, F32)
    hi = lax.bitcast_convert_type(w & jnp.uint32(0xFFFF0000), F32)
    return jnp.concatenate([lo, hi], axis=1)


def _layer_spec(arr, l):
    nd = arr.ndim - 1
    return pl.BlockSpec((None,) + arr.shape[1:], lambda *_: (l,) + (0,) * nd, pipeline_mode=pl.Buffered(1))


def _mixer_kernel(x_ref, whead_ref, wglow_ref, wtail_ref, bcat_ref, wg2_ref, bg_ref, gnorm_ref,
                  slng_ref, slnb_ref, wtril_ref, sbias_ref, poolw_ref, pscale_ref,
                  wupa_ref, wupb_ref, wupc_ref, wo_ref, ln1g_ref, ln1b_ref,
                  out_ref,
                  state_ref, qkvr_ref, ya_ref, vln_ref, e_ref, s2_ref, s4_ref, s8_ref):
    T = MIX_TILE
    D = D_MODEL
    j = pl.program_id(1)
    x = x_ref[0]
    xb = x.astype(BF16)

    def proj(lo, hi):
        if hi <= O_GLOW:
            w = whead_ref[:, lo:hi]
        elif lo == O_GLOW and hi == O_UV:
            w = wglow_ref[...]
        else:
            w = wtail_ref[:, lo - O_UV:hi - O_UV]
        return _dot(xb, w) + bcat_ref[:, lo:hi]

    @pl.when(j == 0)
    def _():
        state_ref[...] = jnp.zeros_like(state_ref)
        e_ref[0:POOL_CARRY, :] = jnp.zeros((POOL_CARRY, POOL_WIDTH), F32)

    @pl.when(j > 0)
    def _():
        e_ref[0:POOL_CARRY, :] = e_ref[T:T + POOL_CARRY, :]

    qkvr_ref[...] = proj(O_QKVR, O_GLOW)
    glow = proj(O_GLOW, O_UV)
    z = _dot(glow.astype(BF16), wg2_ref[...]) + bg_ref[...]
    la = (jnp.minimum(z, 0.0) - jnp.log1p(jnp.exp(-jnp.abs(z)))) * (1.0 / GLA_TAU)
    la_hi, la_lo = _split_bf16(la)
    gate_a = _sigmoid(proj(O_GATE, O_GATE + D))

    C = GLA_CHUNK
    G = GLA_SUB
    NC = G // C
    CSH = C.bit_length() - 1
    row = lax.broadcasted_iota(jnp.int32, (G, G), 0)
    col = lax.broadcasted_iota(jnp.int32, (G, G), 1)
    same_chunk = (row >> CSH) == (col >> CSH)
    causal = same_chunk & (row >= col)
    causal_bf = jnp.where(causal, 1.0, 0.0).astype(BF16)
    lane = lax.broadcasted_iota(jnp.int32, (G, GLA_KEY), 1)
    gate_b = None
    for g0 in range(0, T, G):
        gr = slice(g0, g0 + G)
        lh, ll = la_hi[gr], la_lo[gr]
        b = _dot(causal_bf, lh) + _dot(causal_bf, ll)
        b_last = [b[(c + 1) * C - 1:(c + 1) * C, :] for c in range(NC)]
        b_end = jnp.concatenate([jnp.broadcast_to(r, (C, GLA_KEY)) for r in b_last], axis=0)
        q = qkvr_ref[gr, 0:GLA_KEY]
        k = qkvr_ref[gr, GLA_KEY:2 * GLA_KEY]
        v = qkvr_ref[gr, 2 * GLA_KEY:2 * GLA_KEY + GLA_VAL].astype(BF16)
        q_dec = q * (GLA_DK ** -0.5) * jnp.exp(b)
        k_dec = (k * jnp.exp(-b)).astype(BF16)
        k_tail = (k * jnp.exp(b_end - b)).astype(BF16)
        q_dec_bf = q_dec.astype(BF16)
        o_heads = []
        for h in range(GLA_HEADS):
            q_h = jnp.where((lane >= h * GLA_DK) & (lane < (h + 1) * GLA_DK), q_dec, 0.0).astype(BF16)
            scores = jnp.where(causal, _dot_t1(q_h, k_dec), 0.0).astype(BF16)
            o_heads.append(_dot(scores, v[:, h * GLA_DV:(h + 1) * GLA_DV]))
        o_intra = jnp.concatenate(o_heads, axis=1)
        if gate_b is None:
            gate_b = _sigmoid(proj(O_GATE + D, O_GATE + 2 * D))
        dec_cols = jnp.exp(jnp.concatenate(b_last + [jnp.zeros((8 - NC, GLA_KEY), F32)], axis=0)).T
        dec_all = jnp.concatenate([jnp.broadcast_to(dec_cols[:, c:c + 1], (GLA_KEY, LANES)) for c in range(NC)], axis=1)
        o_inter = []
        for c in range(NC):
            rows = slice(c * C, (c + 1) * C)
            o_inter.append(_dot(q_dec_bf[rows], state_ref[...].astype(BF16)))
            kv = _dot_t0(k_tail[rows], v[rows])
            for h in range(GLA_HEADS):
                rs = slice(h * GLA_DK, (h + 1) * GLA_DK)
                cs = slice(h * GLA_DV, (h + 1) * GLA_DV)
                state_ref[rs, cs] = dec_all[rs, c * LANES:(c + 1) * LANES] * state_ref[rs, cs] + kv[rs, cs]
        o = o_intra + jnp.concatenate(o_inter, axis=0)
        for h in range(GLA_HEADS):
            cs = slice(h * GLA_DV, (h + 1) * GLA_DV)
            o_h = o[:, cs]
            ms = jnp.mean(o_h * o_h, axis=-1, keepdims=True)
            o_h = o_h * lax.rsqrt(ms + LN_EPS) * gnorm_ref[:, cs]
            r_h = qkvr_ref[gr, 2 * GLA_KEY + GLA_VAL + h * GLA_DV:2 * GLA_KEY + GLA_VAL + (h + 1) * GLA_DV]
            ya_ref[gr, cs] = (o_h * (r_h * _sigmoid(r_h))).astype(BF16)

    gate_c = _sigmoid(proj(O_GATE + 2 * D, O_GATE + 3 * D))
    uv = proj(O_UV, O_XC)
    zg = 0.5 * uv * (1.0 + lax.erf(uv * (2.0 ** -0.5)))
    u = zg[:, :SGU_WIDTH]
    vln_ref[...] = _layer_norm(zg[:, SGU_WIDTH:], slng_ref[...], slnb_ref[...])
    lane_s = lax.broadcasted_iota(jnp.int32, (SGU_CHUNK, SGU_WIDTH), 1)
    s_parts = []
    for n in range(T // SGU_CHUNK):
        vc = vln_ref[n * SGU_CHUNK:(n + 1) * SGU_CHUNK, :]
        s = sbias_ref[...]
        for g in range(SGU_GROUPS):
            vg = jnp.where((lane_s >= g * SGU_GD) & (lane_s < (g + 1) * SGU_GD), vc, 0.0).astype(BF16)
            s = s + _dot(wtril_ref[g], vg)
        s_parts.append(s)
    y_b = (u * jnp.concatenate(s_parts, axis=0)).astype(BF16)

    P = POOL_CARRY
    xc = proj(O_XC, O_GATE)
    e_ref[P:P + T, :] = xc
    s2_ref[8:P + T, :] = e_ref[8:P + T, :] + e_ref[7:P + T - 1, :]
    s4_ref[16:P + T, :] = s2_ref[16:P + T, :] + s2_ref[14:P + T - 2, :]
    s8_ref[24:P + T, :] = s4_ref[24:P + T, :] + s4_ref[20:P + T - 4, :]
    s16 = s8_ref[P:P + T, :] + s8_ref[P - 8:P + T - 8, :]
    lane_p = lax.broadcasted_iota(jnp.int32, (T, POOL_WIDTH), 1)
    tpos = lax.broadcasted_iota(jnp.int32, (T, POOL_WIDTH), 0) + (j * T + 1)
    grp = lane_p >> (POOL_GD.bit_length() - 1)
    win = jnp.where(grp == 0, POOL_WINDOWS[0], jnp.where(grp == 1, POOL_WINDOWS[1],
                    jnp.where(grp == 2, POOL_WINDOWS[2], POOL_WINDOWS[3])))
    wsum = jnp.where(grp == 0, s2_ref[P:P + T, :], jnp.where(grp == 1, s4_ref[P:P + T, :],
                     jnp.where(grp == 2, s8_ref[P:P + T, :], s16)))
    count = jnp.minimum(tpos, win).astype(F32)
    pooled = wsum / count - xc
    y_c = (_dot(pooled.astype(BF16), poolw_ref[...]) * pscale_ref[...]).astype(BF16)

    merged = gate_a * _dot(ya_ref[...], wupa_ref[...].astype(BF16))
    merged += gate_b * _dot(y_b, wupb_ref[...].astype(BF16))
    merged += gate_c * _dot(y_c, wupc_ref[...].astype(BF16))
    h = _dot(merged.astype(BF16), wo_ref[...].astype(BF16))
    out_ref[0] = _layer_norm(DEEPNORM_ALPHA * x + h, ln1g_ref[...], ln1b_ref[...])


_MIXER_WEIGHTS = ('whead', 'wglow', 'wtail', 'bcat', 'wg2', 'bg', 'gnorm', 'slng', 'slnb', 'wtril', 'sbias', 'poolw', 'pscale',
                  'wupa', 'wupb', 'wupc', 'wo', 'ln1g', 'ln1b')


def _mixer(x, p, l):
    B, S, D = x.shape
    T = MIX_TILE
    weights = [p[n] for n in _MIXER_WEIGHTS]
    return pl.pallas_call(
        _mixer_kernel,
        out_shape=jax.ShapeDtypeStruct((B, S, D), F32),
        grid=(B, S // T),
        in_specs=[pl.BlockSpec((1, T, D), lambda b, j: (b, j, 0))] + [_layer_spec(w, l) for w in weights],
        out_specs=pl.BlockSpec((1, T, D), lambda b, j: (b, j, 0)),
        scratch_shapes=[
            pltpu.VMEM((GLA_KEY, GLA_VAL), F32),
            pltpu.VMEM((T, 2 * GLA_KEY + 2 * GLA_VAL), F32),
            pltpu.VMEM((T, GLA_VAL), BF16),
            pltpu.VMEM((T, SGU_WIDTH), F32),
            pltpu.VMEM((T + POOL_CARRY, POOL_WIDTH), F32),
            pltpu.VMEM((T + POOL_CARRY, POOL_WIDTH), F32),
            pltpu.VMEM((T + POOL_CARRY, POOL_WIDTH), F32),
            pltpu.VMEM((T + POOL_CARRY, POOL_WIDTH), F32),
        ],
        compiler_params=pltpu.CompilerParams(dimension_semantics=("arbitrary", "arbitrary"),
                                             vmem_limit_bytes=VMEM_LIMIT),
        name="mixer",
    )(x, *weights)


def _memkv_kernel(memt_ref, mem_ref, wk_ref, wv_ref, kt_ref, v_ref):
    kt_ref[0] = _dot_t0(wk_ref[...].astype(BF16), memt_ref[0].astype(BF16)).astype(BF16)
    v_ref[0] = _dot(mem_ref[0].astype(BF16), wv_ref[...].astype(BF16)).astype(BF16)


def _memkv(mem, memt, p, l):
    B, M, D = mem.shape
    return pl.pallas_call(
        _memkv_kernel,
        out_shape=(jax.ShapeDtypeStruct((B, D, M), BF16), jax.ShapeDtypeStruct((B, M, D), BF16)),
        grid=(B,),
        in_specs=[pl.BlockSpec((1, D, M), lambda b: (b, 0, 0)), pl.BlockSpec((1, M, D), lambda b: (b, 0, 0)),
                  _layer_spec(p['wk'], l), _layer_spec(p['wv'], l)],
        out_specs=(pl.BlockSpec((1, D, M), lambda b: (b, 0, 0)), pl.BlockSpec((1, M, D), lambda b: (b, 0, 0))),
        compiler_params=pltpu.CompilerParams(dimension_semantics=("arbitrary",), vmem_limit_bytes=VMEM_LIMIT),
        name="memkv",
    )(memt, mem, p['wk'], p['wv'])


def _xattn_kernel(x_ref, kt_ref, v_ref, wq_ref, wo_ref, ln2g_ref, ln2b_ref, rwt_ref, rbt_ref,
                  x2_ref, x2p_ref, route_ref, counts_ref, carry_ref):
    T = XA_TILE

    @pl.when((pl.program_id(0) == 0) & (pl.program_id(1) == 0))
    def _():
        carry_ref[...] = jnp.zeros_like(carry_ref)

    x = x_ref[0]
    q = (_dot(x.astype(BF16), wq_ref[...].astype(BF16)) * (XA_DH ** -0.5)).astype(BF16)
    h = jnp.zeros_like(x)
    for hd in range(XA_HEADS):
        cs = slice(hd * XA_DH, (hd + 1) * XA_DH)
        s = _dot(q[:, cs], kt_ref[0, cs, :])
        e = jnp.exp(s - jnp.max(s, axis=-1, keepdims=True))
        o = _dot(e.astype(BF16), v_ref[0, :, cs]) / jnp.sum(e, axis=-1, keepdims=True)
        h = h + _dot(o.astype(BF16), wo_ref[cs, :].astype(BF16))
    x2 = _layer_norm(DEEPNORM_ALPHA * x + h, ln2g_ref[...], ln2b_ref[...])
    x2_ref[0] = x2
    x2p_ref[0] = _pack_bf16_pairs(x2)

    E = N_EXPERTS
    hi, lo = _split_bf16(x2)
    lt = _dot_t1(rwt_ref[...], hi)
    logits = lt[0:E] + (lt[E:2 * E] + _dot_t1(rwt_ref[0:E, :], lo)) + rbt_ref[...]

    eid = lax.broadcasted_iota(jnp.int32, (E, T), 0)
    neg_inf = jnp.float32(-jnp.inf)
    rest = logits
    tops, picks = [], []
    for _ in range(TOP_K):
        m = jnp.max(rest, axis=0, keepdims=True)
        idx = jnp.min(jnp.where(rest == m, eid, E), axis=0, keepdims=True)
        pick = eid == idx
        rest = jnp.where(pick, neg_inf, rest)
        tops.append((m, idx))
        picks.append(pick)
    exps = [jnp.exp(m - tops[0][0]) for m, _ in tops]
    denom = exps[0]
    for e in exps[1:]:
        denom = denom + e

    chosen = jnp.zeros((E, T), F32)
    for pick in picks:
        chosen = chosen + jnp.where(pick, 1.0, 0.0)
    chosen_bf = chosen.astype(BF16)
    earlier = (lax.broadcasted_iota(jnp.int32, (T, T), 0) < lax.broadcasted_iota(jnp.int32, (T, T), 1))
    carry = carry_ref[...]
    before = _dot(chosen_bf, jnp.where(earlier, 1.0, 0.0).astype(BF16)) + jnp.concatenate([carry] * (T // LANES), axis=1)
    carry = carry + _dot(chosen_bf, jnp.ones((T, LANES), BF16))
    carry_ref[...] = carry
    counts_ref[...] = carry

    rid = lax.broadcasted_iota(jnp.int32, (ROUTE_ROWS, T), 0)
    route = jnp.zeros((ROUTE_ROWS, T), F32)
    for k in range(TOP_K):
        rank = jnp.sum(jnp.where(picks[k], before, 0.0), axis=0, keepdims=True)
        route = jnp.where(rid == k, tops[k][1].astype(F32), route)
        route = jnp.where(rid == TOP_K + k, exps[k] / denom, route)
        route = jnp.where(rid == 2 * TOP_K + k, rank, route)
    route_ref[...] = route


def _xattn(x, kt, v, p, l):
    B, S, D = x.shape
    T = XA_TILE
    M = MEM_LEN
    weights = [p[n] for n in ('wq', 'wxo', 'ln2g', 'ln2b', 'rwt', 'rbt')]
    return pl.pallas_call(
        _xattn_kernel,
        out_shape=(jax.ShapeDtypeStruct((B, S, D), F32), jax.ShapeDtypeStruct((B, S, D // 2), jnp.uint32),
                   jax.ShapeDtypeStruct((ROUTE_ROWS, B * S), F32), jax.ShapeDtypeStruct((N_EXPERTS, LANES), F32)),
        grid=(B, S // T),
        in_specs=[pl.BlockSpec((1, T, D), lambda b, j: (b, j, 0)),
                  pl.BlockSpec((1, D, M), lambda b, j: (b, 0, 0)),
                  pl.BlockSpec((1, M, D), lambda b, j: (b, 0, 0))] + [_layer_spec(w, l) for w in weights],
        out_specs=(pl.BlockSpec((1, T, D), lambda b, j: (b, j, 0)),
                   pl.BlockSpec((1, T, D // 2), lambda b, j: (b, j, 0)),
                   pl.BlockSpec((ROUTE_ROWS, T), lambda b, j: (0, b * (S // T) + j)),
                   pl.BlockSpec((N_EXPERTS, LANES), lambda b, j: (0, 0))),
        scratch_shapes=[pltpu.VMEM((N_EXPERTS, LANES), F32)],
        compiler_params=pltpu.CompilerParams(dimension_semantics=("arbitrary", "arbitrary"),
                                             vmem_limit_bytes=VMEM_LIMIT),
        name="xattn",
    )(x, kt, v, *weights)


def _expert_kernel(layer, se_ref, sh_ref, nu_ref, nxt_ref, xs_ref, wgu_hbm, bgu_ref, wd_hbm, bd_ref, out_ref,
                   wgu_st, wd_st, slot_ref, sem):
    i = pl.program_id(0)
    F = EXPERT_FF
    halves = sh_ref[i]

    def weight_copies(e, slot):
        return (pltpu.make_async_copy(wgu_hbm.at[layer, e], wgu_st.at[slot], sem.at[slot, 0]),
                pltpu.make_async_copy(wd_hbm.at[layer, e], wd_st.at[slot], sem.at[slot, 1]))

    def ffn(rows):
        slot = slot_ref[0]
        xsb = _unpack_bf16_pairs(xs_ref[rows, :]).astype(BF16)
        hh = _dot(xsb, wgu_st[slot].astype(BF16)) + bgu_ref[...]
        h_glu = jnp.minimum(hh[:, :F], SWIGLU_LIMIT)
        h_lin = jnp.clip(hh[:, F:], -SWIGLU_LIMIT, SWIGLU_LIMIT)
        a = h_glu * _sigmoid(SWIGLU_ALPHA * h_glu) * (h_lin + 1.0)
        out_ref[rows, :] = _pack_bf16_pairs(_dot(a.astype(BF16), wd_st[slot].astype(BF16)) + bd_ref[...])

    @pl.when(halves > 0)
    def _():
        e = se_ref[i]
        prev = se_ref[jnp.maximum(i - 1, 0)]

        @pl.when(i == 0)
        def _():
            slot_ref[0] = 1
            for cp in weight_copies(e, 0):
                cp.start()

        @pl.when((i == 0) | (e != prev))
        def _():
            slot = 1 - slot_ref[0]
            slot_ref[0] = slot
            for cp in weight_copies(e, slot):
                cp.wait()
            nxt = nxt_ref[e]

            @pl.when(nxt != e)
            def _():
                for cp in weight_copies(nxt, 1 - slot):
                    cp.start()

    for n in range(1, MOE_STEP_BLOCKS + 1):
        @pl.when(halves == n)
        def _(n=n):
            ffn(slice(0, n * MOE_BLOCK))


def _experts(step_expert, step_halves, n_used, next_expert, xs, w_gu, b_gu, w_down, b_down, l):
    P, DH = xs.shape
    D = 2 * DH
    R = MOE_STEP_BLOCKS * MOE_BLOCK
    NS = P // R
    F2 = 2 * EXPERT_FF

    def row_map(i, se, sh, nu, nxt):
        return (jnp.minimum(i, nu[0] - 1), 0)

    def exp_map(i, se, sh, nu, nxt):
        return (l, se[jnp.minimum(i, nu[0] - 1)], 0, 0)

    grid_spec = pltpu.PrefetchScalarGridSpec(
        num_scalar_prefetch=4,
        grid=(NS,),
        in_specs=[pl.BlockSpec((R, DH), row_map),
                  pl.BlockSpec(memory_space=pl.ANY),
                  pl.BlockSpec((None, None, 1, F2), exp_map),
                  pl.BlockSpec(memory_space=pl.ANY),
                  pl.BlockSpec((None, None, 1, D), exp_map)],
        out_specs=pl.BlockSpec((R, DH), row_map),
        scratch_shapes=[pltpu.VMEM((2, D, F2), F32), pltpu.VMEM((2, EXPERT_FF, D), F32),
                        pltpu.SMEM((1,), jnp.int32),
                        pltpu.SemaphoreType.DMA((2, 2))],
    )
    return pl.pallas_call(
        functools.partial(_expert_kernel, l),
        out_shape=jax.ShapeDtypeStruct((P, DH), jnp.uint32),
        grid_spec=grid_spec,
        compiler_params=pltpu.CompilerParams(dimension_semantics=("arbitrary",), vmem_limit_bytes=VMEM_LIMIT),
        name="experts",
    )(step_expert, step_halves, n_used, next_expert, xs, w_gu, b_gu.reshape(DEPTH, N_EXPERTS, 1, F2), w_down,
      b_down.reshape(DEPTH, N_EXPERTS, 1, D))


def _sc_gather_rows(x, idx):
    M = idx.shape[0]
    D = x.shape[1]
    W = SC_GATHER_WINDOW
    mesh = plsc.VectorSubcoreMesh(core_axis_name="core", subcore_axis_name="subcore")
    n_workers = mesh.num_cores * mesh.num_subcores
    rows_per = M // n_workers
    assert rows_per * n_workers == M and rows_per % W == 0

    @pl.kernel(out_type=jax.ShapeDtypeStruct((M, D), x.dtype), mesh=mesh, name="sc_gather_rows",
               scratch_types=[pltpu.VMEM((rows_per,), jnp.int32), pltpu.VMEM((W, D), x.dtype)])
    def gather_kernel(x_hbm, i_hbm, o_hbm, idx_vmem, buf):
        wid = lax.axis_index("core") * mesh.num_subcores + lax.axis_index("subcore")
        base = wid * rows_per
        pltpu.sync_copy(i_hbm.at[pl.ds(base, rows_per)], idx_vmem)

        @pl.loop(0, rows_per // W)
        def _(j):
            pltpu.sync_copy(x_hbm.at[idx_vmem.at[pl.ds(j * W, W)]], buf)
            pltpu.sync_copy(buf, o_hbm.at[pl.ds(base + j * W, W)])

    return gather_kernel(x, idx)


def _sc_scatter_rows(x, idx, n_out):
    K, N = idx.shape
    D = x.shape[1]
    W = SC_GATHER_WINDOW
    mesh = plsc.VectorSubcoreMesh(core_axis_name="core", subcore_axis_name="subcore")
    n_workers = mesh.num_cores * mesh.num_subcores
    rows_per = N // n_workers
    assert rows_per * n_workers == N and rows_per % W == 0

    @pl.kernel(out_type=jax.ShapeDtypeStruct((n_out, D), x.dtype), mesh=mesh, name="sc_scatter_rows",
               scratch_types=[pltpu.VMEM((K * rows_per,), jnp.int32), pltpu.VMEM((W, D), x.dtype),
                              pltpu.SemaphoreType.DMA((K,))])
    def scatter_kernel(x_hbm, i_hbm, o_hbm, idx_vmem, buf, sem):
        wid = lax.axis_index("core") * mesh.num_subcores + lax.axis_index("subcore")
        base = wid * rows_per
        for k in range(K):
            pltpu.sync_copy(i_hbm.at[pl.ds(k * N + base, rows_per)], idx_vmem.at[pl.ds(k * rows_per, rows_per)])

        @pl.loop(0, rows_per // W)
        def _(j):
            pltpu.sync_copy(x_hbm.at[pl.ds(base + j * W, W)], buf)
            copies = [pltpu.make_async_copy(buf, o_hbm.at[idx_vmem.at[pl.ds(k * rows_per + j * W, W)]], sem.at[k])
                      for k in range(K)]
            for cp in copies:
                cp.start()
            for cp in copies:
                cp.wait()

    return scatter_kernel(x, idx.reshape(K * N))


def _combine_kernel(x_ref, yg_ref, route_ref, ln3g_ref, ln3b_ref, *rest):
    out_ref = rest[-1]
    x = x_ref[...]
    g = jnp.transpose(route_ref[...])
    y = jnp.zeros_like(x)
    for k in range(TOP_K):
        y = y + g[:, TOP_K + k:TOP_K + k + 1] * _unpack_bf16_pairs(yg_ref[k * CMB_TILE:(k + 1) * CMB_TILE, :])
    out_ref[...] = _layer_norm(DEEPNORM_ALPHA * x + y, ln3g_ref[...], ln3b_ref[...])


def _combine(x2, yg, route, p, l, part, acc):
    N, D = x2.shape
    T = CMB_TILE
    tiles = yg.shape[0] // (TOP_K * T)
    first = part * tiles
    in_specs = [pl.BlockSpec((T, D), lambda i: (first + i, 0)), pl.BlockSpec((TOP_K * T, D // 2), lambda i: (i, 0)),
                pl.BlockSpec((ROUTE_ROWS, T), lambda i: (0, first + i)), _layer_spec(p['ln3g'], l), _layer_spec(p['ln3b'], l)]
    args = [x2, yg, route, p['ln3g'], p['ln3b']]
    aliases = {}
    if acc is not None:
        in_specs.append(pl.BlockSpec(memory_space=pl.ANY))
        args.append(acc)
        aliases = {len(args) - 1: 0}
    return pl.pallas_call(
        _combine_kernel,
        out_shape=jax.ShapeDtypeStruct((N, D), F32),
        grid=(tiles,),
        in_specs=in_specs,
        out_specs=pl.BlockSpec((T, D), lambda i: (first + i, 0)),
        input_output_aliases=aliases,
        compiler_params=pltpu.CompilerParams(dimension_semantics=("arbitrary",), vmem_limit_bytes=VMEM_LIMIT),
        name="combine",
    )(*args)


def _prep(w_in, b_in, gla_wg2, gla_bg, gla_norm_g, sgu_ln_g, sgu_ln_b, sgu_ws, sgu_bs, pool_w, pool_scale,
          w_up_a, w_up_b, w_up_c, w_o, ln1_g, ln1_b, xa_wq, xa_wk, xa_wv, xa_wo, ln2_g, ln2_b,
          router_w, router_b, ln3_g, ln3_b):
    L = w_in.shape[0]
    o_glow = O_GLOW
    o_uv = o_glow + GLA_RANK
    row = lambda a: a.reshape(L, 1, -1).astype(F32)
    pad_last = lambda a, n: jnp.pad(a, [(0, 0)] * (a.ndim - 1) + [(0, n - a.shape[-1])])
    p = {}
    p['whead'] = w_in[..., :o_glow].astype(BF16)
    p['wglow'] = pad_last(w_in[..., o_glow:o_uv], LANES).astype(BF16)
    p['wtail'] = w_in[..., o_uv:].astype(BF16)
    p['bcat'] = row(jnp.concatenate([b_in[..., :o_glow], pad_last(b_in[..., o_glow:o_uv], LANES), b_in[..., o_uv:]],
                                    axis=-1))
    p['wg2'] = jnp.pad(gla_wg2, ((0, 0), (0, LANES - GLA_RANK), (0, 0))).astype(BF16)
    p['bg'] = row(gla_bg)
    p['gnorm'] = row(gla_norm_g)
    p['slng'] = row(sgu_ln_g)
    p['slnb'] = row(sgu_ln_b)
    p['wtril'] = jnp.tril(sgu_ws).astype(BF16)
    p['sbias'] = jnp.repeat(jnp.swapaxes(sgu_bs, 1, 2), SGU_GD, axis=2).astype(F32)
    G = len(POOL_WINDOWS)
    eye = jnp.eye(G, dtype=F32)
    p['poolw'] = jnp.einsum('lgcd,gh->lgchd', pool_w, eye).reshape(L, POOL_WIDTH, POOL_WIDTH).astype(BF16)
    p['pscale'] = row(pool_scale)
    p['wupa'], p['wupb'], p['wupc'], p['wo'] = w_up_a, w_up_b, w_up_c, w_o
    p['ln1g'], p['ln1b'] = row(ln1_g), row(ln1_b)
    p['wq'], p['wk'], p['wv'], p['wxo'] = xa_wq, xa_wk, xa_wv, xa_wo
    p['ln2g'], p['ln2b'] = row(ln2_g), row(ln2_b)
    rwt = jnp.swapaxes(router_w, 1, 2)
    rwt_hi = rwt.astype(BF16)
    p['rwt'] = jnp.concatenate([rwt_hi, (rwt - rwt_hi.astype(F32)).astype(BF16)], axis=1)
    p['rbt'] = jnp.broadcast_to(router_b[:, :, None], router_b.shape + (XA_TILE,)).astype(F32)
    p['ln3g'], p['ln3b'] = row(ln3_g), row(ln3_b)
    return p


def _route(route, counts):
    N = route.shape[1]
    top_idx = route[0:TOP_K].astype(jnp.int32)
    rank = route[2 * TOP_K:3 * TOP_K].astype(jnp.int32)
    counts = counts[:, 0].astype(jnp.int32)
    R = MOE_STEP_BLOCKS * MOE_BLOCK
    blocks = (counts + MOE_BLOCK - 1) // MOE_BLOCK
    padded = ((counts + R - 1) // R) * R
    pad_end = jnp.cumsum(padded)
    pad_start = pad_end - padded
    ids = jnp.arange(N_EXPERTS, dtype=jnp.int32)
    start_of = jnp.sum(jnp.where(top_idx[:, :, None] == ids[None, None, :], pad_start[None, None, :], 0), axis=-1)
    dest = start_of + rank
    n_steps = N * TOP_K // R + N_EXPERTS
    step_start = jnp.arange(n_steps, dtype=jnp.int32) * R
    step_expert = jnp.minimum(jnp.sum((pad_end[None, :] <= step_start[:, None]).astype(jnp.int32), axis=1),
                              N_EXPERTS - 1)
    mine = step_expert[:, None] == ids[None, :]
    blocks_before = (step_start - jnp.sum(jnp.where(mine, pad_start[None, :], 0), axis=1)) // MOE_BLOCK
    step_halves = jnp.clip(jnp.sum(jnp.where(mine, blocks[None, :], 0), axis=1) - blocks_before, 0, MOE_STEP_BLOCKS)
    step_halves = jnp.where(step_start < pad_end[-1], step_halves, 0).astype(jnp.int32)
    n_used = (pad_end[-1] // R).astype(jnp.int32).reshape(1)
    later = jnp.where((ids[None, :] > ids[:, None]) & (counts[None, :] > 0), ids[None, :], N_EXPERTS)
    first_later = jnp.min(later, axis=1)
    next_expert = jnp.where(first_later < N_EXPERTS, first_later, ids).astype(jnp.int32)
    return dest, step_expert, step_halves, n_used, next_expert


def kernel(x, mem, w_in, b_in, gla_wg2, gla_bg, gla_norm_g, sgu_ln_g, sgu_ln_b, sgu_ws, sgu_bs, pool_w, pool_scale, w_up_a, w_up_b, w_up_c, w_o, ln1_g, ln1_b, xa_wq, xa_wk, xa_wv, xa_wo, ln2_g, ln2_b, router_w, router_b, exp_w_gu, exp_b_gu, exp_w_down, exp_b_down, ln3_g, ln3_b):
    B, S, D = x.shape
    N = B * S
    p = _prep(w_in, b_in, gla_wg2, gla_bg, gla_norm_g, sgu_ln_g, sgu_ln_b, sgu_ws, sgu_bs, pool_w, pool_scale,
              w_up_a, w_up_b, w_up_c, w_o, ln1_g, ln1_b, xa_wq, xa_wk, xa_wv, xa_wo, ln2_g, ln2_b,
              router_w, router_b, ln3_g, ln3_b)
    memt = jnp.swapaxes(mem, 1, 2)
    for l in range(DEPTH):
        x1 = _mixer(x, p, l)
        kt, v = _memkv(mem, memt, p, l)
        x2, x2p, route, counts = _xattn(x1, kt, v, p, l)
        dest, step_expert, step_halves, n_used, next_expert = _route(route, counts)
        n_slots = N * TOP_K + N_EXPERTS * MOE_STEP_BLOCKS * MOE_BLOCK
        xs = _sc_scatter_rows(x2p.reshape(N, D // 2), dest, n_slots)
        ys = _experts(step_expert, step_halves, n_used, next_expert, xs, exp_w_gu, exp_b_gu, exp_w_down, exp_b_down, l)
        dest_km = dest.reshape(TOP_K, N // CMB_TILE, CMB_TILE).transpose(1, 0, 2).reshape(-1)
        rows = dest_km.shape[0] // COMBINE_PARTS
        acc = None
        for part in range(COMBINE_PARTS):
            yg = _sc_gather_rows(ys, dest_km[part * rows:(part + 1) * rows])
            acc = _combine(x2.reshape(N, D), yg, route, p, l, part, acc)
        x = acc.reshape(B, S, D)
    return x
```

```python
import functools

import jax
import jax.numpy as jnp
from jax import lax
from jax.experimental import pallas as pl
from jax.experimental.pallas import tpu as pltpu
from jax.experimental.pallas import tpu_sc as plsc

F32 = jnp.float32
BF16 = jnp.bfloat16

D_MODEL = 1024
DEPTH = 2
GLA_HEADS = 4
GLA_KEY = 256
GLA_VAL = 512
GLA_DK = 64
GLA_DV = 128
GLA_RANK = 16
GLA_TAU = 16.0
GLA_CHUNK = 64
SGU_GROUPS = 4
SGU_WIDTH = 256
SGU_GD = 64
SGU_CHUNK = 128
POOL_WINDOWS = (2, 4, 8, 16)
POOL_WIDTH = 256
POOL_GD = 64
POOL_CARRY = 32
MEM_LEN = 256
XA_HEADS = 4
XA_DH = 256
N_EXPERTS = 32
TOP_K = 4
EXPERT_FF = 1024
SWIGLU_LIMIT = 7.0
SWIGLU_ALPHA = 1.702
DEEPNORM_ALPHA = (2 * DEPTH) ** 0.25
LN_EPS = 1e-5
LANES = 128
VMEM_LIMIT = 56 * 1024 * 1024

MIX_TILE = 512
GLA_SUB = 256
XA_TILE = 1024
MOE_BLOCK = 256
MOE_STEP_BLOCKS = 4
CMB_TILE = 512
COMBINE_SHARES = (1, 1, 2, 2, 2)
ROUTE_ROWS = 16
SC_GATHER_WINDOW = 128

O_QKVR = 0
O_GLOW = 2 * GLA_KEY + 2 * GLA_VAL
O_UV = O_GLOW + LANES
O_XC = O_UV + 2 * SGU_WIDTH
O_GATE = O_XC + POOL_WIDTH


def _dot(a, b):
    return jnp.dot(a, b, preferred_element_type=F32)


def _dot_t0(a, b):
    return lax.dot_general(a, b, (((0,), (0,)), ((), ())), preferred_element_type=F32)


def _dot_t1(a, b):
    return lax.dot_general(a, b, (((1,), (1,)), ((), ())), preferred_element_type=F32)


def _split_bf16(x):
    hi = x.astype(BF16)
    lo = (x - hi.astype(F32)).astype(BF16)
    return hi, lo


def _layer_norm(x, g, b):
    mu = jnp.mean(x, axis=-1, keepdims=True)
    xc = x - mu
    var = jnp.mean(xc * xc, axis=-1, keepdims=True)
    return xc * lax.rsqrt(var + LN_EPS) * g + b


def _sigmoid(x):
    return 1.0 / (1.0 + jnp.exp(-x))


def _pack_bf16_pairs(x):
    H = x.shape[1] // 2
    bits = lax.bitcast_convert_type(x.astype(BF16).astype(F32), jnp.uint32)
    return (bits[:, :H] >> 16) | (bits[:, H:] & jnp.uint32(0xFFFF0000))


def _unpack_bf16_pairs(w):
    lo = lax.bitcast_convert_type(w << 16, F32)
    hi = lax.bitcast_convert_type(w & jnp.uint32(0xFFFF0000), F32)
    return jnp.concatenate([lo, hi], axis=1)


def _layer_spec(arr, l):
    nd = arr.ndim - 1
    return pl.BlockSpec((None,) + arr.shape[1:], lambda *_: (l,) + (0,) * nd, pipeline_mode=pl.Buffered(1))


def _mixer_kernel(x_ref, whead_ref, wglow_ref, wtail_ref, bcat_ref, wg2_ref, bg_ref, gnorm_ref,
                  slng_ref, slnb_ref, wtril_ref, sbias_ref, poolw_ref, pscale_ref,
                  wupa_ref, wupb_ref, wupc_ref, wo_ref, ln1g_ref, ln1b_ref,
                  out_ref,
                  state_ref, qkvr_ref, ya_ref, vln_ref, e_ref, s2_ref, s4_ref, s8_ref):
    T = MIX_TILE
    D = D_MODEL
    j = pl.program_id(1)
    x = x_ref[0]
    xb = x.astype(BF16)

    def proj(lo, hi):
        if hi <= O_GLOW:
            w = whead_ref[:, lo:hi]
        elif lo == O_GLOW and hi == O_UV:
            w = wglow_ref[...]
        else:
            w = wtail_ref[:, lo - O_UV:hi - O_UV]
        return _dot(xb, w) + bcat_ref[:, lo:hi]

    @pl.when(j == 0)
    def _():
        state_ref[...] = jnp.zeros_like(state_ref)
        e_ref[0:POOL_CARRY, :] = jnp.zeros((POOL_CARRY, POOL_WIDTH), F32)

    @pl.when(j > 0)
    def _():
        e_ref[0:POOL_CARRY, :] = e_ref[T:T + POOL_CARRY, :]

    qkvr_ref[...] = proj(O_QKVR, O_GLOW)
    glow = proj(O_GLOW, O_UV)
    z = _dot(glow.astype(BF16), wg2_ref[...]) + bg_ref[...]
    la = (jnp.minimum(z, 0.0) - jnp.log1p(jnp.exp(-jnp.abs(z)))) * (1.0 / GLA_TAU)
    la_hi, la_lo = _split_bf16(la)
    gate_a = _sigmoid(proj(O_GATE, O_GATE + D))

    C = GLA_CHUNK
    G = GLA_SUB
    NC = G // C
    CSH = C.bit_length() - 1
    row = lax.broadcasted_iota(jnp.int32, (G, G), 0)
    col = lax.broadcasted_iota(jnp.int32, (G, G), 1)
    same_chunk = (row >> CSH) == (col >> CSH)
    causal = same_chunk & (row >= col)
    causal_bf = jnp.where(causal, 1.0, 0.0).astype(BF16)
    lane = lax.broadcasted_iota(jnp.int32, (G, GLA_KEY), 1)
    gate_b = None
    for g0 in range(0, T, G):
        gr = slice(g0, g0 + G)
        lh, ll = la_hi[gr], la_lo[gr]
        b = _dot(causal_bf, lh) + _dot(causal_bf, ll)
        b_last = [b[(c + 1) * C - 1:(c + 1) * C, :] for c in range(NC)]
        b_end = jnp.concatenate([jnp.broadcast_to(r, (C, GLA_KEY)) for r in b_last], axis=0)
        q = qkvr_ref[gr, 0:GLA_KEY]
        k = qkvr_ref[gr, GLA_KEY:2 * GLA_KEY]
        v = qkvr_ref[gr, 2 * GLA_KEY:2 * GLA_KEY + GLA_VAL].astype(BF16)
        q_dec = q * (GLA_DK ** -0.5) * jnp.exp(b)
        k_dec = (k * jnp.exp(-b)).astype(BF16)
        k_tail = (k * jnp.exp(b_end - b)).astype(BF16)
        q_dec_bf = q_dec.astype(BF16)
        o_heads = []
        for h in range(GLA_HEADS):
            q_h = jnp.where((lane >= h * GLA_DK) & (lane < (h + 1) * GLA_DK), q_dec, 0.0).astype(BF16)
            scores = jnp.where(causal, _dot_t1(q_h, k_dec), 0.0).astype(BF16)
            o_heads.append(_dot(scores, v[:, h * GLA_DV:(h + 1) * GLA_DV]))
        o_intra = jnp.concatenate(o_heads, axis=1)
        if gate_b is None:
            gate_b = _sigmoid(proj(O_GATE + D, O_GATE + 2 * D))
        dec_cols = jnp.exp(jnp.concatenate(b_last + [jnp.zeros((8 - NC, GLA_KEY), F32)], axis=0)).T
        dec_all = jnp.concatenate([jnp.broadcast_to(dec_cols[:, c:c + 1], (GLA_KEY, LANES)) for c in range(NC)], axis=1)
        o_inter = []
        for c in range(NC):
            rows = slice(c * C, (c + 1) * C)
            o_inter.append(_dot(q_dec_bf[rows], state_ref[...].astype(BF16)))
            kv = _dot_t0(k_tail[rows], v[rows])
            for h in range(GLA_HEADS):
                rs = slice(h * GLA_DK, (h + 1) * GLA_DK)
                cs = slice(h * GLA_DV, (h + 1) * GLA_DV)
                state_ref[rs, cs] = dec_all[rs, c * LANES:(c + 1) * LANES] * state_ref[rs, cs] + kv[rs, cs]
        o = o_intra + jnp.concatenate(o_inter, axis=0)
        for h in range(GLA_HEADS):
            cs = slice(h * GLA_DV, (h + 1) * GLA_DV)
            o_h = o[:, cs]
            ms = jnp.mean(o_h * o_h, axis=-1, keepdims=True)
            o_h = o_h * lax.rsqrt(ms + LN_EPS) * gnorm_ref[:, cs]
            r_h = qkvr_ref[gr, 2 * GLA_KEY + GLA_VAL + h * GLA_DV:2 * GLA_KEY + GLA_VAL + (h + 1) * GLA_DV]
            ya_ref[gr, cs] = (o_h * (r_h * _sigmoid(r_h))).astype(BF16)

    gate_c = _sigmoid(proj(O_GATE + 2 * D, O_GATE + 3 * D))
    uv = proj(O_UV, O_XC)
    zg = 0.5 * uv * (1.0 + lax.erf(uv * (2.0 ** -0.5)))
    u = zg[:, :SGU_WIDTH]
    vln_ref[...] = _layer_norm(zg[:, SGU_WIDTH:], slng_ref[...], slnb_ref[...])
    lane_s = lax.broadcasted_iota(jnp.int32, (SGU_CHUNK, SGU_WIDTH), 1)
    s_parts = []
    for n in range(T // SGU_CHUNK):
        vc = vln_ref[n * SGU_CHUNK:(n + 1) * SGU_CHUNK, :]
        s = sbias_ref[...]
        for g in range(SGU_GROUPS):
            vg = jnp.where((lane_s >= g * SGU_GD) & (lane_s < (g + 1) * SGU_GD), vc, 0.0).astype(BF16)
            s = s + _dot(wtril_ref[g], vg)
        s_parts.append(s)
    y_b = (u * jnp.concatenate(s_parts, axis=0)).astype(BF16)

    P = POOL_CARRY
    xc = proj(O_XC, O_GATE)
    e_ref[P:P + T, :] = xc
    s2_ref[8:P + T, :] = e_ref[8:P + T, :] + e_ref[7:P + T - 1, :]
    s4_ref[16:P + T, :] = s2_ref[16:P + T, :] + s2_ref[14:P + T - 2, :]
    s8_ref[24:P + T, :] = s4_ref[24:P + T, :] + s4_ref[20:P + T - 4, :]
    s16 = s8_ref[P:P + T, :] + s8_ref[P - 8:P + T - 8, :]
    lane_p = lax.broadcasted_iota(jnp.int32, (T, POOL_WIDTH), 1)
    tpos = lax.broadcasted_iota(jnp.int32, (T, POOL_WIDTH), 0) + (j * T + 1)
    grp = lane_p >> (POOL_GD.bit_length() - 1)
    win = jnp.where(grp == 0, POOL_WINDOWS[0], jnp.where(grp == 1, POOL_WINDOWS[1],
                    jnp.where(grp == 2, POOL_WINDOWS[2], POOL_WINDOWS[3])))
    wsum = jnp.where(grp == 0, s2_ref[P:P + T, :], jnp.where(grp == 1, s4_ref[P:P + T, :],
                     jnp.where(grp == 2, s8_ref[P:P + T, :], s16)))
    count = jnp.minimum(tpos, win).astype(F32)
    pooled = wsum / count - xc
    y_c = (_dot(pooled.astype(BF16), poolw_ref[...]) * pscale_ref[...]).astype(BF16)

    merged = gate_a * _dot(ya_ref[...], wupa_ref[...].astype(BF16))
    merged += gate_b * _dot(y_b, wupb_ref[...].astype(BF16))
    merged += gate_c * _dot(y_c, wupc_ref[...].astype(BF16))
    h = _dot(merged.astype(BF16), wo_ref[...].astype(BF16))
    out_ref[0] = _layer_norm(DEEPNORM_ALPHA * x + h, ln1g_ref[...], ln1b_ref[...])


_MIXER_WEIGHTS = ('whead', 'wglow', 'wtail', 'bcat', 'wg2', 'bg', 'gnorm', 'slng', 'slnb', 'wtril', 'sbias', 'poolw', 'pscale',
                  'wupa', 'wupb', 'wupc', 'wo', 'ln1g', 'ln1b')


def _mixer(x, p, l):
    B, S, D = x.shape
    T = MIX_TILE
    weights = [p[n] for n in _MIXER_WEIGHTS]
    return pl.pallas_call(
        _mixer_kernel,
        out_shape=jax.ShapeDtypeStruct((B, S, D), F32),
        grid=(B, S // T),
        in_specs=[pl.BlockSpec((1, T, D), lambda b, j: (b, j, 0))] + [_layer_spec(w, l) for w in weights],
        out_specs=pl.BlockSpec((1, T, D), lambda b, j: (b, j, 0)),
        scratch_shapes=[
            pltpu.VMEM((GLA_KEY, GLA_VAL), F32),
            pltpu.VMEM((T, 2 * GLA_KEY + 2 * GLA_VAL), F32),
            pltpu.VMEM((T, GLA_VAL), BF16),
            pltpu.VMEM((T, SGU_WIDTH), F32),
            pltpu.VMEM((T + POOL_CARRY, POOL_WIDTH), F32),
            pltpu.VMEM((T + POOL_CARRY, POOL_WIDTH), F32),
            pltpu.VMEM((T + POOL_CARRY, POOL_WIDTH), F32),
            pltpu.VMEM((T + POOL_CARRY, POOL_WIDTH), F32),
        ],
        compiler_params=pltpu.CompilerParams(dimension_semantics=("arbitrary", "arbitrary"),
                                             vmem_limit_bytes=VMEM_LIMIT),
        name="mixer",
    )(x, *weights)


def _memkv_kernel(memt_ref, mem_ref, wk_ref, wv_ref, kt_ref, v_ref):
    kt_ref[0] = _dot_t0(wk_ref[...].astype(BF16), memt_ref[0].astype(BF16)).astype(BF16)
    v_ref[0] = _dot(mem_ref[0].astype(BF16), wv_ref[...].astype(BF16)).astype(BF16)


def _memkv(mem, memt, p, l):
    B, M, D = mem.shape
    return pl.pallas_call(
        _memkv_kernel,
        out_shape=(jax.ShapeDtypeStruct((B, D, M), BF16), jax.ShapeDtypeStruct((B, M, D), BF16)),
        grid=(B,),
        in_specs=[pl.BlockSpec((1, D, M), lambda b: (b, 0, 0)), pl.BlockSpec((1, M, D), lambda b: (b, 0, 0)),
                  _layer_spec(p['wk'], l), _layer_spec(p['wv'], l)],
        out_specs=(pl.BlockSpec((1, D, M), lambda b: (b, 0, 0)), pl.BlockSpec((1, M, D), lambda b: (b, 0, 0))),
        compiler_params=pltpu.CompilerParams(dimension_semantics=("arbitrary",), vmem_limit_bytes=VMEM_LIMIT),
        name="memkv",
    )(memt, mem, p['wk'], p['wv'])


def _xattn_kernel(x_ref, kt_ref, v_ref, wq_ref, wo_ref, ln2g_ref, ln2b_ref, rwt_ref, rbt_ref,
                  x2_ref, x2p_ref, route_ref, counts_ref, carry_ref):
    T = XA_TILE

    @pl.when((pl.program_id(0) == 0) & (pl.program_id(1) == 0))
    def _():
        carry_ref[...] = jnp.zeros_like(carry_ref)

    x = x_ref[0]
    q = (_dot(x.astype(BF16), wq_ref[...].astype(BF16)) * (XA_DH ** -0.5)).astype(BF16)
    h = jnp.zeros_like(x)
    for hd in range(XA_HEADS):
        cs = slice(hd * XA_DH, (hd + 1) * XA_DH)
        s = _dot(q[:, cs], kt_ref[0, cs, :])
        e = jnp.exp(s - jnp.max(s, axis=-1, keepdims=True))
        o = _dot(e.astype(BF16), v_ref[0, :, cs]) / jnp.sum(e, axis=-1, keepdims=True)
        h = h + _dot(o.astype(BF16), wo_ref[cs, :].astype(BF16))
    x2 = _layer_norm(DEEPNORM_ALPHA * x + h, ln2g_ref[...], ln2b_ref[...])
    x2_ref[0] = x2
    x2p_ref[0] = _pack_bf16_pairs(x2)

    E = N_EXPERTS
    hi, lo = _split_bf16(x2)
    lt = _dot_t1(rwt_ref[...], hi)
    logits = lt[0:E] + (lt[E:2 * E] + _dot_t1(rwt_ref[0:E, :], lo)) + rbt_ref[...]

    eid = lax.broadcasted_iota(jnp.int32, (E, T), 0)
    neg_inf = jnp.float32(-jnp.inf)
    rest = logits
    tops, picks = [], []
    for _ in range(TOP_K):
        m = jnp.max(rest, axis=0, keepdims=True)
        idx = jnp.min(jnp.where(rest == m, eid, E), axis=0, keepdims=True)
        pick = eid == idx
        rest = jnp.where(pick, neg_inf, rest)
        tops.append((m, idx))
        picks.append(pick)
    exps = [jnp.exp(m - tops[0][0]) for m, _ in tops]
    denom = exps[0]
    for e in exps[1:]:
        denom = denom + e

    chosen = jnp.zeros((E, T), F32)
    for pick in picks:
        chosen = chosen + jnp.where(pick, 1.0, 0.0)
    chosen_bf = chosen.astype(BF16)
    earlier = (lax.broadcasted_iota(jnp.int32, (T, T), 0) < lax.broadcasted_iota(jnp.int32, (T, T), 1))
    carry = carry_ref[...]
    before = _dot(chosen_bf, jnp.where(earlier, 1.0, 0.0).astype(BF16)) + jnp.concatenate([carry] * (T // LANES), axis=1)
    carry = carry + _dot(chosen_bf, jnp.ones((T, LANES), BF16))
    carry_ref[...] = carry
    counts_ref[...] = carry

    rid = lax.broadcasted_iota(jnp.int32, (ROUTE_ROWS, T), 0)
    route = jnp.zeros((ROUTE_ROWS, T), F32)
    for k in range(TOP_K):
        rank = jnp.sum(jnp.where(picks[k], before, 0.0), axis=0, keepdims=True)
        route = jnp.where(rid == k, tops[k][1].astype(F32), route)
        route = jnp.where(rid == TOP_K + k, exps[k] / denom, route)
        route = jnp.where(rid == 2 * TOP_K + k, rank, route)
    route_ref[...] = route


def _xattn(x, kt, v, p, l):
    B, S, D = x.shape
    T = XA_TILE
    M = MEM_LEN
    weights = [p[n] for n in ('wq', 'wxo', 'ln2g', 'ln2b', 'rwt', 'rbt')]
    return pl.pallas_call(
        _xattn_kernel,
        out_shape=(jax.ShapeDtypeStruct((B, S, D), F32), jax.ShapeDtypeStruct((B, S, D // 2), jnp.uint32),
                   jax.ShapeDtypeStruct((ROUTE_ROWS, B * S), F32), jax.ShapeDtypeStruct((N_EXPERTS, LANES), F32)),
        grid=(B, S // T),
        in_specs=[pl.BlockSpec((1, T, D), lambda b, j: (b, j, 0)),
                  pl.BlockSpec((1, D, M), lambda b, j: (b, 0, 0)),
                  pl.BlockSpec((1, M, D), lambda b, j: (b, 0, 0))] + [_layer_spec(w, l) for w in weights],
        out_specs=(pl.BlockSpec((1, T, D), lambda b, j: (b, j, 0)),
                   pl.BlockSpec((1, T, D // 2), lambda b, j: (b, j, 0)),
                   pl.BlockSpec((ROUTE_ROWS, T), lambda b, j: (0, b * (S // T) + j)),
                   pl.BlockSpec((N_EXPERTS, LANES), lambda b, j: (0, 0))),
        scratch_shapes=[pltpu.VMEM((N_EXPERTS, LANES), F32)],
        compiler_params=pltpu.CompilerParams(dimension_semantics=("arbitrary", "arbitrary"),
                                             vmem_limit_bytes=VMEM_LIMIT),
        name="xattn",
    )(x, kt, v, *weights)


def _expert_kernel(layer, se_ref, sh_ref, nu_ref, nxt_ref, xs_ref, wgu_hbm, bgu_ref, wd_hbm, bd_ref, out_ref,
                   wgu_st, wd_st, slot_ref, sem):
    i = pl.program_id(0)
    F = EXPERT_FF
    halves = sh_ref[i]

    def weight_copies(e, slot):
        return (pltpu.make_async_copy(wgu_hbm.at[layer, e], wgu_st.at[slot], sem.at[slot, 0]),
                pltpu.make_async_copy(wd_hbm.at[layer, e], wd_st.at[slot], sem.at[slot, 1]))

    def ffn(rows):
        slot = slot_ref[0]
        xsb = _unpack_bf16_pairs(xs_ref[rows, :]).astype(BF16)
        hh = _dot(xsb, wgu_st[slot].astype(BF16)) + bgu_ref[...]
        h_glu = jnp.minimum(hh[:, :F], SWIGLU_LIMIT)
        h_lin = jnp.clip(hh[:, F:], -SWIGLU_LIMIT, SWIGLU_LIMIT)
        a = h_glu * _sigmoid(SWIGLU_ALPHA * h_glu) * (h_lin + 1.0)
        out_ref[rows, :] = _pack_bf16_pairs(_dot(a.astype(BF16), wd_st[slot].astype(BF16)) + bd_ref[...])

    @pl.when(halves > 0)
    def _():
        e = se_ref[i]
        prev = se_ref[jnp.maximum(i - 1, 0)]

        @pl.when(i == 0)
        def _():
            slot_ref[0] = 1
            for cp in weight_copies(e, 0):
                cp.start()

        @pl.when((i == 0) | (e != prev))
        def _():
            slot = 1 - slot_ref[0]
            slot_ref[0] = slot
            for cp in weight_copies(e, slot):
                cp.wait()
            nxt = nxt_ref[e]

            @pl.when(nxt != e)
            def _():
                for cp in weight_copies(nxt, 1 - slot):
                    cp.start()

    for n in range(1, MOE_STEP_BLOCKS + 1):
        @pl.when(halves == n)
        def _(n=n):
            ffn(slice(0, n * MOE_BLOCK))


def _experts(step_expert, step_halves, n_used, next_expert, xs, w_gu, b_gu, w_down, b_down, l):
    P, DH = xs.shape
    D = 2 * DH
    R = MOE_STEP_BLOCKS * MOE_BLOCK
    NS = P // R
    F2 = 2 * EXPERT_FF

    def row_map(i, se, sh, nu, nxt):
        return (jnp.minimum(i, nu[0] - 1), 0)

    def exp_map(i, se, sh, nu, nxt):
        return (l, se[jnp.minimum(i, nu[0] - 1)], 0, 0)

    grid_spec = pltpu.PrefetchScalarGridSpec(
        num_scalar_prefetch=4,
        grid=(NS,),
        in_specs=[pl.BlockSpec((R, DH), row_map),
                  pl.BlockSpec(memory_space=pl.ANY),
                  pl.BlockSpec((None, None, 1, F2), exp_map),
                  pl.BlockSpec(memory_space=pl.ANY),
                  pl.BlockSpec((None, None, 1, D), exp_map)],
        out_specs=pl.BlockSpec((R, DH), row_map),
        scratch_shapes=[pltpu.VMEM((2, D, F2), F32), pltpu.VMEM((2, EXPERT_FF, D), F32),
                        pltpu.SMEM((1,), jnp.int32),
                        pltpu.SemaphoreType.DMA((2, 2))],
    )
    return pl.pallas_call(
        functools.partial(_expert_kernel, l),
        out_shape=jax.ShapeDtypeStruct((P, DH), jnp.uint32),
        grid_spec=grid_spec,
        compiler_params=pltpu.CompilerParams(dimension_semantics=("arbitrary",), vmem_limit_bytes=VMEM_LIMIT),
        name="experts",
    )(step_expert, step_halves, n_used, next_expert, xs, w_gu, b_gu.reshape(DEPTH, N_EXPERTS, 1, F2), w_down,
      b_down.reshape(DEPTH, N_EXPERTS, 1, D))


def _sc_gather_rows(x, idx):
    M = idx.shape[0]
    D = x.shape[1]
    W = SC_GATHER_WINDOW
    mesh = plsc.VectorSubcoreMesh(core_axis_name="core", subcore_axis_name="subcore")
    n_workers = mesh.num_cores * mesh.num_subcores
    rows_per = M // n_workers
    assert rows_per * n_workers == M and rows_per % W == 0

    @pl.kernel(out_type=jax.ShapeDtypeStruct((M, D), x.dtype), mesh=mesh, name="sc_gather_rows",
               scratch_types=[pltpu.VMEM((rows_per,), jnp.int32), pltpu.VMEM((W, D), x.dtype)])
    def gather_kernel(x_hbm, i_hbm, o_hbm, idx_vmem, buf):
        wid = lax.axis_index("core") * mesh.num_subcores + lax.axis_index("subcore")
        base = wid * rows_per
        pltpu.sync_copy(i_hbm.at[pl.ds(base, rows_per)], idx_vmem)

        @pl.loop(0, rows_per // W)
        def _(j):
            pltpu.sync_copy(x_hbm.at[idx_vmem.at[pl.ds(j * W, W)]], buf)
            pltpu.sync_copy(buf, o_hbm.at[pl.ds(base + j * W, W)])

    return gather_kernel(x, idx)


def _sc_scatter_rows(x, idx, n_out):
    K, N = idx.shape
    D = x.shape[1]
    W = SC_GATHER_WINDOW
    mesh = plsc.VectorSubcoreMesh(core_axis_name="core", subcore_axis_name="subcore")
    n_workers = mesh.num_cores * mesh.num_subcores
    rows_per = N // n_workers
    assert rows_per * n_workers == N and rows_per % W == 0

    @pl.kernel(out_type=jax.ShapeDtypeStruct((n_out, D), x.dtype), mesh=mesh, name="sc_scatter_rows",
               scratch_types=[pltpu.VMEM((K * rows_per,), jnp.int32), pltpu.VMEM((W, D), x.dtype),
                              pltpu.SemaphoreType.DMA((K,))])
    def scatter_kernel(x_hbm, i_hbm, o_hbm, idx_vmem, buf, sem):
        wid = lax.axis_index("core") * mesh.num_subcores + lax.axis_index("subcore")
        base = wid * rows_per
        for k in range(K):
            pltpu.sync_copy(i_hbm.at[pl.ds(k * N + base, rows_per)], idx_vmem.at[pl.ds(k * rows_per, rows_per)])

        @pl.loop(0, rows_per // W)
        def _(j):
            pltpu.sync_copy(x_hbm.at[pl.ds(base + j * W, W)], buf)
            copies = [pltpu.make_async_copy(buf, o_hbm.at[idx_vmem.at[pl.ds(k * rows_per + j * W, W)]], sem.at[k])
                      for k in range(K)]
            for cp in copies:
                cp.start()
            for cp in copies:
                cp.wait()

    return scatter_kernel(x, idx.reshape(K * N))


def _combine_kernel(x_ref, yg_ref, route_ref, ln3g_ref, ln3b_ref, *rest):
    out_ref = rest[-1]
    x = x_ref[...]
    g = jnp.transpose(route_ref[...])
    y = jnp.zeros_like(x)
    for k in range(TOP_K):
        y = y + g[:, TOP_K + k:TOP_K + k + 1] * _unpack_bf16_pairs(yg_ref[k * CMB_TILE:(k + 1) * CMB_TILE, :])
    out_ref[...] = _layer_norm(DEEPNORM_ALPHA * x + y, ln3g_ref[...], ln3b_ref[...])


def _combine(x2, yg, route, p, l, first, acc):
    N, D = x2.shape
    T = CMB_TILE
    tiles = yg.shape[0] // (TOP_K * T)
    in_specs = [pl.BlockSpec((T, D), lambda i: (first + i, 0)), pl.BlockSpec((TOP_K * T, D // 2), lambda i: (i, 0)),
                pl.BlockSpec((ROUTE_ROWS, T), lambda i: (0, first + i)), _layer_spec(p['ln3g'], l), _layer_spec(p['ln3b'], l)]
    args = [x2, yg, route, p['ln3g'], p['ln3b']]
    aliases = {}
    if acc is not None:
        in_specs.append(pl.BlockSpec(memory_space=pl.ANY))
        args.append(acc)
        aliases = {len(args) - 1: 0}
    return pl.pallas_call(
        _combine_kernel,
        out_shape=jax.ShapeDtypeStruct((N, D), F32),
        grid=(tiles,),
        in_specs=in_specs,
        out_specs=pl.BlockSpec((T, D), lambda i: (first + i, 0)),
        input_output_aliases=aliases,
        compiler_params=pltpu.CompilerParams(dimension_semantics=("arbitrary",), vmem_limit_bytes=VMEM_LIMIT),
        name="combine",
    )(*args)


def _prep(w_in, b_in, gla_wg2, gla_bg, gla_norm_g, sgu_ln_g, sgu_ln_b, sgu_ws, sgu_bs, pool_w, pool_scale,
          w_up_a, w_up_b, w_up_c, w_o, ln1_g, ln1_b, xa_wq, xa_wk, xa_wv, xa_wo, ln2_g, ln2_b,
          router_w, router_b, ln3_g, ln3_b):
    L = w_in.shape[0]
    o_glow = O_GLOW
    o_uv = o_glow + GLA_RANK
    row = lambda a: a.reshape(L, 1, -1).astype(F32)
    pad_last = lambda a, n: jnp.pad(a, [(0, 0)] * (a.ndim - 1) + [(0, n - a.shape[-1])])
    p = {}
    p['whead'] = w_in[..., :o_glow].astype(BF16)
    p['wglow'] = pad_last(w_in[..., o_glow:o_uv], LANES).astype(BF16)
    p['wtail'] = w_in[..., o_uv:].astype(BF16)
    p['bcat'] = row(jnp.concatenate([b_in[..., :o_glow], pad_last(b_in[..., o_glow:o_uv], LANES), b_in[..., o_uv:]],
                                    axis=-1))
    p['wg2'] = jnp.pad(gla_wg2, ((0, 0), (0, LANES - GLA_RANK), (0, 0))).astype(BF16)
    p['bg'] = row(gla_bg)
    p['gnorm'] = row(gla_norm_g)
    p['slng'] = row(sgu_ln_g)
    p['slnb'] = row(sgu_ln_b)
    p['wtril'] = jnp.tril(sgu_ws).astype(BF16)
    p['sbias'] = jnp.repeat(jnp.swapaxes(sgu_bs, 1, 2), SGU_GD, axis=2).astype(F32)
    G = len(POOL_WINDOWS)
    eye = jnp.eye(G, dtype=F32)
    p['poolw'] = jnp.einsum('lgcd,gh->lgchd', pool_w, eye).reshape(L, POOL_WIDTH, POOL_WIDTH).astype(BF16)
    p['pscale'] = row(pool_scale)
    p['wupa'], p['wupb'], p['wupc'], p['wo'] = w_up_a, w_up_b, w_up_c, w_o
    p['ln1g'], p['ln1b'] = row(ln1_g), row(ln1_b)
    p['wq'], p['wk'], p['wv'], p['wxo'] = xa_wq, xa_wk, xa_wv, xa_wo
    p['ln2g'], p['ln2b'] = row(ln2_g), row(ln2_b)
    rwt = jnp.swapaxes(router_w, 1, 2)
    rwt_hi = rwt.astype(BF16)
    p['rwt'] = jnp.concatenate([rwt_hi, (rwt - rwt_hi.astype(F32)).astype(BF16)], axis=1)
    p['rbt'] = jnp.broadcast_to(router_b[:, :, None], router_b.shape + (XA_TILE,)).astype(F32)
    p['ln3g'], p['ln3b'] = row(ln3_g), row(ln3_b)
    return p


def _route(route, counts):
    N = route.shape[1]
    top_idx = route[0:TOP_K].astype(jnp.int32)
    rank = route[2 * TOP_K:3 * TOP_K].astype(jnp.int32)
    counts = counts[:, 0].astype(jnp.int32)
    R = MOE_STEP_BLOCKS * MOE_BLOCK
    blocks = (counts + MOE_BLOCK - 1) // MOE_BLOCK
    padded = ((counts + R - 1) // R) * R
    pad_end = jnp.cumsum(padded)
    pad_start = pad_end - padded
    ids = jnp.arange(N_EXPERTS, dtype=jnp.int32)
    start_of = jnp.sum(jnp.where(top_idx[:, :, None] == ids[None, None, :], pad_start[None, None, :], 0), axis=-1)
    dest = start_of + rank
    n_steps = N * TOP_K // R + N_EXPERTS
    step_start = jnp.arange(n_steps, dtype=jnp.int32) * R
    step_expert = jnp.minimum(jnp.sum((pad_end[None, :] <= step_start[:, None]).astype(jnp.int32), axis=1),
                              N_EXPERTS - 1)
    mine = step_expert[:, None] == ids[None, :]
    blocks_before = (step_start - jnp.sum(jnp.where(mine, pad_start[None, :], 0), axis=1)) // MOE_BLOCK
    step_halves = jnp.clip(jnp.sum(jnp.where(mine, blocks[None, :], 0), axis=1) - blocks_before, 0, MOE_STEP_BLOCKS)
    step_halves = jnp.where(step_start < pad_end[-1], step_halves, 0).astype(jnp.int32)
    n_used = (pad_end[-1] // R).astype(jnp.int32).reshape(1)
    later = jnp.where((ids[None, :] > ids[:, None]) & (counts[None, :] > 0), ids[None, :], N_EXPERTS)
    first_later = jnp.min(later, axis=1)
    next_expert = jnp.where(first_later < N_EXPERTS, first_later, ids).astype(jnp.int32)
    return dest, step_expert, step_halves, n_used, next_expert


def kernel(x, mem, w_in, b_in, gla_wg2, gla_bg, gla_norm_g, sgu_ln_g, sgu_ln_b, sgu_ws, sgu_bs, pool_w, pool_scale, w_up_a, w_up_b, w_up_c, w_o, ln1_g, ln1_b, xa_wq, xa_wk, xa_wv, xa_wo, ln2_g, ln2_b, router_w, router_b, exp_w_gu, exp_b_gu, exp_w_down, exp_b_down, ln3_g, ln3_b):
    B, S, D = x.shape
    N = B * S
    p = _prep(w_in, b_in, gla_wg2, gla_bg, gla_norm_g, sgu_ln_g, sgu_ln_b, sgu_ws, sgu_bs, pool_w, pool_scale,
              w_up_a, w_up_b, w_up_c, w_o, ln1_g, ln1_b, xa_wq, xa_wk, xa_wv, xa_wo, ln2_g, ln2_b,
              router_w, router_b, ln3_g, ln3_b)
    memt = jnp.swapaxes(mem, 1, 2)
    for l in range(DEPTH):
        x1 = _mixer(x, p, l)
        kt, v = _memkv(mem, memt, p, l)
        x2, x2p, route, counts = _xattn(x1, kt, v, p, l)
        dest, step_expert, step_halves, n_used, next_expert = _route(route, counts)
        n_slots = N * TOP_K + N_EXPERTS * MOE_STEP_BLOCKS * MOE_BLOCK
        xs = _sc_scatter_rows(x2p.reshape(N, D // 2), dest, n_slots)
        ys = _experts(step_expert, step_halves, n_used, next_expert, xs, exp_w_gu, exp_b_gu, exp_w_down, exp_b_down, l)
        dest_km = dest.reshape(TOP_K, N // CMB_TILE, CMB_TILE).transpose(1, 0, 2).reshape(-1)
        tile_rows = TOP_K * CMB_TILE
        acc, first = None, 0
        for share in COMBINE_SHARES:
            tiles = (N // CMB_TILE) * share // sum(COMBINE_SHARES)
            yg = _sc_gather_rows(ys, dest_km[first * tile_rows:(first + tiles) * tile_rows])
            acc = _combine(x2.reshape(N, D), yg, route, p, l, first, acc)
            first += tiles
        x = acc.reshape(B, S, D)
    return x
```

```python
import functools

import jax
import jax.numpy as jnp
from jax import lax
from jax.experimental import pallas as pl
from jax.experimental.pallas import tpu as pltpu
from jax.experimental.pallas import tpu_sc as plsc

F32 = jnp.float32
BF16 = jnp.bfloat16

D_MODEL = 1024
DEPTH = 2
GLA_HEADS = 4
GLA_KEY = 256
GLA_VAL = 512
GLA_DK = 64
GLA_DV = 128
GLA_RANK = 16
GLA_TAU = 16.0
GLA_CHUNK = 64
SGU_GROUPS = 4
SGU_WIDTH = 256
SGU_GD = 64
SGU_CHUNK = 128
POOL_WINDOWS = (2, 4, 8, 16)
POOL_WIDTH = 256
POOL_GD = 64
POOL_CARRY = 32
MEM_LEN = 256
XA_HEADS = 4
XA_DH = 256
N_EXPERTS = 32
TOP_K = 4
EXPERT_FF = 1024
SWIGLU_LIMIT = 7.0
SWIGLU_ALPHA = 1.702
DEEPNORM_ALPHA = (2 * DEPTH) ** 0.25
LN_EPS = 1e-5
LANES = 128
VMEM_LIMIT = 56 * 1024 * 1024

MIX_TILE = 512
GLA_SUB = 256
XA_TILE = 1024
MOE_BLOCK = 128
MOE_STEP_BLOCKS = 8
CMB_TILE = 512
COMBINE_SHARES = (1, 1, 2, 2, 2)
ROUTE_ROWS = 16
SC_GATHER_WINDOW = 128

O_QKVR = 0
O_GLOW = 2 * GLA_KEY + 2 * GLA_VAL
O_UV = O_GLOW + LANES
O_XC = O_UV + 2 * SGU_WIDTH
O_GATE = O_XC + POOL_WIDTH


def _dot(a, b):
    return jnp.dot(a, b, preferred_element_type=F32)


def _dot_t0(a, b):
    return lax.dot_general(a, b, (((0,), (0,)), ((), ())), preferred_element_type=F32)


def _dot_t1(a, b):
    return lax.dot_general(a, b, (((1,), (1,)), ((), ())), preferred_element_type=F32)


def _split_bf16(x):
    hi = x.astype(BF16)
    lo = (x - hi.astype(F32)).astype(BF16)
    return hi, lo


def _layer_norm(x, g, b):
    mu = jnp.mean(x, axis=-1, keepdims=True)
    xc = x - mu
    var = jnp.mean(xc * xc, axis=-1, keepdims=True)
    return xc * lax.rsqrt(var + LN_EPS) * g + b


def _sigmoid(x):
    return 1.0 / (1.0 + jnp.exp(-x))


def _pack_bf16_pairs(x):
    H = x.shape[1] // 2
    bits = lax.bitcast_convert_type(x.astype(BF16).astype(F32), jnp.uint32)
    return (bits[:, :H] >> 16) | (bits[:, H:] & jnp.uint32(0xFFFF0000))


def _unpack_bf16_pairs(w):
    lo = lax.bitcast_convert_type(w << 16, F32)
    hi = lax.bitcast_convert_type(w & jnp.uint32(0xFFFF0000), F32)
    return jnp.concatenate([lo, hi], axis=1)


def _layer_spec(arr, l):
    nd = arr.ndim - 1
    return pl.BlockSpec((None,) + arr.shape[1:], lambda *_: (l,) + (0,) * nd, pipeline_mode=pl.Buffered(1))


def _mixer_kernel(x_ref, whead_ref, wglow_ref, wtail_ref, bcat_ref, wg2_ref, bg_ref, gnorm_ref,
                  slng_ref, slnb_ref, wtril_ref, sbias_ref, poolw_ref, pscale_ref,
                  wupa_ref, wupb_ref, wupc_ref, wo_ref, ln1g_ref, ln1b_ref,
                  out_ref,
                  state_ref, qkvr_ref, ya_ref, vln_ref, e_ref, s2_ref, s4_ref, s8_ref):
    T = MIX_TILE
    D = D_MODEL
    j = pl.program_id(1)
    x = x_ref[0]
    xb = x.astype(BF16)

    def proj(lo, hi):
        if hi <= O_GLOW:
            w = whead_ref[:, lo:hi]
        elif lo == O_GLOW and hi == O_UV:
            w = wglow_ref[...]
        else:
            w = wtail_ref[:, lo - O_UV:hi - O_UV]
        return _dot(xb, w) + bcat_ref[:, lo:hi]

    @pl.when(j == 0)
    def _():
        state_ref[...] = jnp.zeros_like(state_ref)
        e_ref[0:POOL_CARRY, :] = jnp.zeros((POOL_CARRY, POOL_WIDTH), F32)

    @pl.when(j > 0)
    def _():
        e_ref[0:POOL_CARRY, :] = e_ref[T:T + POOL_CARRY, :]

    qkvr_ref[...] = proj(O_QKVR, O_GLOW)
    glow = proj(O_GLOW, O_UV)
    z = _dot(glow.astype(BF16), wg2_ref[...]) + bg_ref[...]
    la = (jnp.minimum(z, 0.0) - jnp.log1p(jnp.exp(-jnp.abs(z)))) * (1.0 / GLA_TAU)
    la_hi, la_lo = _split_bf16(la)
    gate_a = _sigmoid(proj(O_GATE, O_GATE + D))

    C = GLA_CHUNK
    G = GLA_SUB
    NC = G // C
    CSH = C.bit_length() - 1
    row = lax.broadcasted_iota(jnp.int32, (G, G), 0)
    col = lax.broadcasted_iota(jnp.int32, (G, G), 1)
    same_chunk = (row >> CSH) == (col >> CSH)
    causal = same_chunk & (row >= col)
    causal_bf = jnp.where(causal, 1.0, 0.0).astype(BF16)
    lane = lax.broadcasted_iota(jnp.int32, (G, GLA_KEY), 1)
    gate_b = None
    for g0 in range(0, T, G):
        gr = slice(g0, g0 + G)
        lh, ll = la_hi[gr], la_lo[gr]
        b = _dot(causal_bf, lh) + _dot(causal_bf, ll)
        b_last = [b[(c + 1) * C - 1:(c + 1) * C, :] for c in range(NC)]
        b_end = jnp.concatenate([jnp.broadcast_to(r, (C, GLA_KEY)) for r in b_last], axis=0)
        q = qkvr_ref[gr, 0:GLA_KEY]
        k = qkvr_ref[gr, GLA_KEY:2 * GLA_KEY]
        v = qkvr_ref[gr, 2 * GLA_KEY:2 * GLA_KEY + GLA_VAL].astype(BF16)
        q_dec = q * (GLA_DK ** -0.5) * jnp.exp(b)
        k_dec = (k * jnp.exp(-b)).astype(BF16)
        k_tail = (k * jnp.exp(b_end - b)).astype(BF16)
        q_dec_bf = q_dec.astype(BF16)
        o_heads = []
        for h in range(GLA_HEADS):
            q_h = jnp.where((lane >= h * GLA_DK) & (lane < (h + 1) * GLA_DK), q_dec, 0.0).astype(BF16)
            scores = jnp.where(causal, _dot_t1(q_h, k_dec), 0.0).astype(BF16)
            o_heads.append(_dot(scores, v[:, h * GLA_DV:(h + 1) * GLA_DV]))
        o_intra = jnp.concatenate(o_heads, axis=1)
        if gate_b is None:
            gate_b = _sigmoid(proj(O_GATE + D, O_GATE + 2 * D))
        dec_cols = jnp.exp(jnp.concatenate(b_last + [jnp.zeros((8 - NC, GLA_KEY), F32)], axis=0)).T
        dec_all = jnp.concatenate([jnp.broadcast_to(dec_cols[:, c:c + 1], (GLA_KEY, LANES)) for c in range(NC)], axis=1)
        o_inter = []
        for c in range(NC):
            rows = slice(c * C, (c + 1) * C)
            o_inter.append(_dot(q_dec_bf[rows], state_ref[...].astype(BF16)))
            kv = _dot_t0(k_tail[rows], v[rows])
            for h in range(GLA_HEADS):
                rs = slice(h * GLA_DK, (h + 1) * GLA_DK)
                cs = slice(h * GLA_DV, (h + 1) * GLA_DV)
                state_ref[rs, cs] = dec_all[rs, c * LANES:(c + 1) * LANES] * state_ref[rs, cs] + kv[rs, cs]
        o = o_intra + jnp.concatenate(o_inter, axis=0)
        for h in range(GLA_HEADS):
            cs = slice(h * GLA_DV, (h + 1) * GLA_DV)
            o_h = o[:, cs]
            ms = jnp.mean(o_h * o_h, axis=-1, keepdims=True)
            o_h = o_h * lax.rsqrt(ms + LN_EPS) * gnorm_ref[:, cs]
            r_h = qkvr_ref[gr, 2 * GLA_KEY + GLA_VAL + h * GLA_DV:2 * GLA_KEY + GLA_VAL + (h + 1) * GLA_DV]
            ya_ref[gr, cs] = (o_h * (r_h * _sigmoid(r_h))).astype(BF16)

    gate_c = _sigmoid(proj(O_GATE + 2 * D, O_GATE + 3 * D))
    uv = proj(O_UV, O_XC)
    zg = 0.5 * uv * (1.0 + lax.erf(uv * (2.0 ** -0.5)))
    u = zg[:, :SGU_WIDTH]
    vln_ref[...] = _layer_norm(zg[:, SGU_WIDTH:], slng_ref[...], slnb_ref[...])
    lane_s = lax.broadcasted_iota(jnp.int32, (SGU_CHUNK, SGU_WIDTH), 1)
    s_parts = []
    for n in range(T // SGU_CHUNK):
        vc = vln_ref[n * SGU_CHUNK:(n + 1) * SGU_CHUNK, :]
        s = sbias_ref[...]
        for g in range(SGU_GROUPS):
            vg = jnp.where((lane_s >= g * SGU_GD) & (lane_s < (g + 1) * SGU_GD), vc, 0.0).astype(BF16)
            s = s + _dot(wtril_ref[g], vg)
        s_parts.append(s)
    y_b = (u * jnp.concatenate(s_parts, axis=0)).astype(BF16)

    P = POOL_CARRY
    xc = proj(O_XC, O_GATE)
    e_ref[P:P + T, :] = xc
    s2_ref[8:P + T, :] = e_ref[8:P + T, :] + e_ref[7:P + T - 1, :]
    s4_ref[16:P + T, :] = s2_ref[16:P + T, :] + s2_ref[14:P + T - 2, :]
    s8_ref[24:P + T, :] = s4_ref[24:P + T, :] + s4_ref[20:P + T - 4, :]
    s16 = s8_ref[P:P + T, :] + s8_ref[P - 8:P + T - 8, :]
    lane_p = lax.broadcasted_iota(jnp.int32, (T, POOL_WIDTH), 1)
    tpos = lax.broadcasted_iota(jnp.int32, (T, POOL_WIDTH), 0) + (j * T + 1)
    grp = lane_p >> (POOL_GD.bit_length() - 1)
    win = jnp.where(grp == 0, POOL_WINDOWS[0], jnp.where(grp == 1, POOL_WINDOWS[1],
                    jnp.where(grp == 2, POOL_WINDOWS[2], POOL_WINDOWS[3])))
    wsum = jnp.where(grp == 0, s2_ref[P:P + T, :], jnp.where(grp == 1, s4_ref[P:P + T, :],
                     jnp.where(grp == 2, s8_ref[P:P + T, :], s16)))
    count = jnp.minimum(tpos, win).astype(F32)
    pooled = wsum / count - xc
    y_c = (_dot(pooled.astype(BF16), poolw_ref[...]) * pscale_ref[...]).astype(BF16)

    merged = gate_a * _dot(ya_ref[...], wupa_ref[...].astype(BF16))
    merged += gate_b * _dot(y_b, wupb_ref[...].astype(BF16))
    merged += gate_c * _dot(y_c, wupc_ref[...].astype(BF16))
    h = _dot(merged.astype(BF16), wo_ref[...].astype(BF16))
    out_ref[0] = _layer_norm(DEEPNORM_ALPHA * x + h, ln1g_ref[...], ln1b_ref[...])


_MIXER_WEIGHTS = ('whead', 'wglow', 'wtail', 'bcat', 'wg2', 'bg', 'gnorm', 'slng', 'slnb', 'wtril', 'sbias', 'poolw', 'pscale',
                  'wupa', 'wupb', 'wupc', 'wo', 'ln1g', 'ln1b')


def _mixer(x, p, l):
    B, S, D = x.shape
    T = MIX_TILE
    weights = [p[n] for n in _MIXER_WEIGHTS]
    return pl.pallas_call(
        _mixer_kernel,
        out_shape=jax.ShapeDtypeStruct((B, S, D), F32),
        grid=(B, S // T),
        in_specs=[pl.BlockSpec((1, T, D), lambda b, j: (b, j, 0))] + [_layer_spec(w, l) for w in weights],
        out_specs=pl.BlockSpec((1, T, D), lambda b, j: (b, j, 0)),
        scratch_shapes=[
            pltpu.VMEM((GLA_KEY, GLA_VAL), F32),
            pltpu.VMEM((T, 2 * GLA_KEY + 2 * GLA_VAL), F32),
            pltpu.VMEM((T, GLA_VAL), BF16),
            pltpu.VMEM((T, SGU_WIDTH), F32),
            pltpu.VMEM((T + POOL_CARRY, POOL_WIDTH), F32),
            pltpu.VMEM((T + POOL_CARRY, POOL_WIDTH), F32),
            pltpu.VMEM((T + POOL_CARRY, POOL_WIDTH), F32),
            pltpu.VMEM((T + POOL_CARRY, POOL_WIDTH), F32),
        ],
        compiler_params=pltpu.CompilerParams(dimension_semantics=("arbitrary", "arbitrary"),
                                             vmem_limit_bytes=VMEM_LIMIT),
        name="mixer",
    )(x, *weights)


def _memkv_kernel(memt_ref, mem_ref, wk_ref, wv_ref, kt_ref, v_ref):
    kt_ref[0] = _dot_t0(wk_ref[...].astype(BF16), memt_ref[0].astype(BF16)).astype(BF16)
    v_ref[0] = _dot(mem_ref[0].astype(BF16), wv_ref[...].astype(BF16)).astype(BF16)


def _memkv(mem, memt, p, l):
    B, M, D = mem.shape
    return pl.pallas_call(
        _memkv_kernel,
        out_shape=(jax.ShapeDtypeStruct((B, D, M), BF16), jax.ShapeDtypeStruct((B, M, D), BF16)),
        grid=(B,),
        in_specs=[pl.BlockSpec((1, D, M), lambda b: (b, 0, 0)), pl.BlockSpec((1, M, D), lambda b: (b, 0, 0)),
                  _layer_spec(p['wk'], l), _layer_spec(p['wv'], l)],
        out_specs=(pl.BlockSpec((1, D, M), lambda b: (b, 0, 0)), pl.BlockSpec((1, M, D), lambda b: (b, 0, 0))),
        compiler_params=pltpu.CompilerParams(dimension_semantics=("arbitrary",), vmem_limit_bytes=VMEM_LIMIT),
        name="memkv",
    )(memt, mem, p['wk'], p['wv'])


def _xattn_kernel(x_ref, kt_ref, v_ref, wq_ref, wo_ref, ln2g_ref, ln2b_ref, rwt_ref, rbt_ref,
                  x2_ref, x2p_ref, route_ref, counts_ref, carry_ref):
    T = XA_TILE

    @pl.when((pl.program_id(0) == 0) & (pl.program_id(1) == 0))
    def _():
        carry_ref[...] = jnp.zeros_like(carry_ref)

    x = x_ref[0]
    q = (_dot(x.astype(BF16), wq_ref[...].astype(BF16)) * (XA_DH ** -0.5)).astype(BF16)
    h = jnp.zeros_like(x)
    for hd in range(XA_HEADS):
        cs = slice(hd * XA_DH, (hd + 1) * XA_DH)
        s = _dot(q[:, cs], kt_ref[0, cs, :])
        e = jnp.exp(s - jnp.max(s, axis=-1, keepdims=True))
        o = _dot(e.astype(BF16), v_ref[0, :, cs]) / jnp.sum(e, axis=-1, keepdims=True)
        h = h + _dot(o.astype(BF16), wo_ref[cs, :].astype(BF16))
    x2 = _layer_norm(DEEPNORM_ALPHA * x + h, ln2g_ref[...], ln2b_ref[...])
    x2_ref[0] = x2
    x2p_ref[0] = _pack_bf16_pairs(x2)

    E = N_EXPERTS
    hi, lo = _split_bf16(x2)
    lt = _dot_t1(rwt_ref[...], hi)
    logits = lt[0:E] + (lt[E:2 * E] + _dot_t1(rwt_ref[0:E, :], lo)) + rbt_ref[...]

    eid = lax.broadcasted_iota(jnp.int32, (E, T), 0)
    neg_inf = jnp.float32(-jnp.inf)
    rest = logits
    tops, picks = [], []
    for _ in range(TOP_K):
        m = jnp.max(rest, axis=0, keepdims=True)
        idx = jnp.min(jnp.where(rest == m, eid, E), axis=0, keepdims=True)
        pick = eid == idx
        rest = jnp.where(pick, neg_inf, rest)
        tops.append((m, idx))
        picks.append(pick)
    exps = [jnp.exp(m - tops[0][0]) for m, _ in tops]
    denom = exps[0]
    for e in exps[1:]:
        denom = denom + e

    chosen = jnp.zeros((E, T), F32)
    for pick in picks:
        chosen = chosen + jnp.where(pick, 1.0, 0.0)
    chosen_bf = chosen.astype(BF16)
    earlier = (lax.broadcasted_iota(jnp.int32, (T, T), 0) < lax.broadcasted_iota(jnp.int32, (T, T), 1))
    carry = carry_ref[...]
    before = _dot(chosen_bf, jnp.where(earlier, 1.0, 0.0).astype(BF16)) + jnp.concatenate([carry] * (T // LANES), axis=1)
    carry = carry + _dot(chosen_bf, jnp.ones((T, LANES), BF16))
    carry_ref[...] = carry
    counts_ref[...] = carry

    rid = lax.broadcasted_iota(jnp.int32, (ROUTE_ROWS, T), 0)
    route = jnp.zeros((ROUTE_ROWS, T), F32)
    for k in range(TOP_K):
        rank = jnp.sum(jnp.where(picks[k], before, 0.0), axis=0, keepdims=True)
        route = jnp.where(rid == k, tops[k][1].astype(F32), route)
        route = jnp.where(rid == TOP_K + k, exps[k] / denom, route)
        route = jnp.where(rid == 2 * TOP_K + k, rank, route)
    route_ref[...] = route


def _xattn(x, kt, v, p, l):
    B, S, D = x.shape
    T = XA_TILE
    M = MEM_LEN
    weights = [p[n] for n in ('wq', 'wxo', 'ln2g', 'ln2b', 'rwt', 'rbt')]
    return pl.pallas_call(
        _xattn_kernel,
        out_shape=(jax.ShapeDtypeStruct((B, S, D), F32), jax.ShapeDtypeStruct((B, S, D // 2), jnp.uint32),
                   jax.ShapeDtypeStruct((ROUTE_ROWS, B * S), F32), jax.ShapeDtypeStruct((N_EXPERTS, LANES), F32)),
        grid=(B, S // T),
        in_specs=[pl.BlockSpec((1, T, D), lambda b, j: (b, j, 0)),
                  pl.BlockSpec((1, D, M), lambda b, j: (b, 0, 0)),
                  pl.BlockSpec((1, M, D), lambda b, j: (b, 0, 0))] + [_layer_spec(w, l) for w in weights],
        out_specs=(pl.BlockSpec((1, T, D), lambda b, j: (b, j, 0)),
                   pl.BlockSpec((1, T, D // 2), lambda b, j: (b, j, 0)),
                   pl.BlockSpec((ROUTE_ROWS, T), lambda b, j: (0, b * (S // T) + j)),
                   pl.BlockSpec((N_EXPERTS, LANES), lambda b, j: (0, 0))),
        scratch_shapes=[pltpu.VMEM((N_EXPERTS, LANES), F32)],
        compiler_params=pltpu.CompilerParams(dimension_semantics=("arbitrary", "arbitrary"),
                                             vmem_limit_bytes=VMEM_LIMIT),
        name="xattn",
    )(x, kt, v, *weights)


def _expert_kernel(layer, se_ref, sh_ref, nu_ref, nxt_ref, xs_ref, wgu_hbm, bgu_ref, wd_hbm, bd_ref, out_ref,
                   wgu_st, wd_st, slot_ref, sem):
    i = pl.program_id(0)
    F = EXPERT_FF
    halves = sh_ref[i]

    def weight_copies(e, slot):
        return (pltpu.make_async_copy(wgu_hbm.at[layer, e], wgu_st.at[slot], sem.at[slot, 0]),
                pltpu.make_async_copy(wd_hbm.at[layer, e], wd_st.at[slot], sem.at[slot, 1]))

    def ffn(rows):
        slot = slot_ref[0]
        xsb = _unpack_bf16_pairs(xs_ref[rows, :]).astype(BF16)
        hh = _dot(xsb, wgu_st[slot].astype(BF16)) + bgu_ref[...]
        h_glu = jnp.minimum(hh[:, :F], SWIGLU_LIMIT)
        h_lin = jnp.clip(hh[:, F:], -SWIGLU_LIMIT, SWIGLU_LIMIT)
        a = h_glu * _sigmoid(SWIGLU_ALPHA * h_glu) * (h_lin + 1.0)
        out_ref[rows, :] = _pack_bf16_pairs(_dot(a.astype(BF16), wd_st[slot].astype(BF16)) + bd_ref[...])

    @pl.when(halves > 0)
    def _():
        e = se_ref[i]
        prev = se_ref[jnp.maximum(i - 1, 0)]

        @pl.when(i == 0)
        def _():
            slot_ref[0] = 1
            for cp in weight_copies(e, 0):
                cp.start()

        @pl.when((i == 0) | (e != prev))
        def _():
            slot = 1 - slot_ref[0]
            slot_ref[0] = slot
            for cp in weight_copies(e, slot):
                cp.wait()
            nxt = nxt_ref[e]

            @pl.when(nxt != e)
            def _():
                for cp in weight_copies(nxt, 1 - slot):
                    cp.start()

    for n in range(1, MOE_STEP_BLOCKS + 1):
        @pl.when(halves == n)
        def _(n=n):
            ffn(slice(0, n * MOE_BLOCK))


def _experts(step_expert, step_halves, n_used, next_expert, xs, w_gu, b_gu, w_down, b_down, l):
    P, DH = xs.shape
    D = 2 * DH
    R = MOE_STEP_BLOCKS * MOE_BLOCK
    NS = P // R
    F2 = 2 * EXPERT_FF

    def row_map(i, se, sh, nu, nxt):
        return (jnp.minimum(i, nu[0] - 1), 0)

    def exp_map(i, se, sh, nu, nxt):
        return (l, se[jnp.minimum(i, nu[0] - 1)], 0, 0)

    grid_spec = pltpu.PrefetchScalarGridSpec(
        num_scalar_prefetch=4,
        grid=(NS,),
        in_specs=[pl.BlockSpec((R, DH), row_map),
                  pl.BlockSpec(memory_space=pl.ANY),
                  pl.BlockSpec((None, None, 1, F2), exp_map),
                  pl.BlockSpec(memory_space=pl.ANY),
                  pl.BlockSpec((None, None, 1, D), exp_map)],
        out_specs=pl.BlockSpec((R, DH), row_map),
        scratch_shapes=[pltpu.VMEM((2, D, F2), F32), pltpu.VMEM((2, EXPERT_FF, D), F32),
                        pltpu.SMEM((1,), jnp.int32),
                        pltpu.SemaphoreType.DMA((2, 2))],
    )
    return pl.pallas_call(
        functools.partial(_expert_kernel, l),
        out_shape=jax.ShapeDtypeStruct((P, DH), jnp.uint32),
        grid_spec=grid_spec,
        compiler_params=pltpu.CompilerParams(dimension_semantics=("arbitrary",), vmem_limit_bytes=VMEM_LIMIT),
        name="experts",
    )(step_expert, step_halves, n_used, next_expert, xs, w_gu, b_gu.reshape(DEPTH, N_EXPERTS, 1, F2), w_down,
      b_down.reshape(DEPTH, N_EXPERTS, 1, D))


def _sc_gather_rows(x, idx):
    M = idx.shape[0]
    D = x.shape[1]
    W = SC_GATHER_WINDOW
    mesh = plsc.VectorSubcoreMesh(core_axis_name="core", subcore_axis_name="subcore")
    n_workers = mesh.num_cores * mesh.num_subcores
    rows_per = M // n_workers
    assert rows_per * n_workers == M and rows_per % W == 0

    @pl.kernel(out_type=jax.ShapeDtypeStruct((M, D), x.dtype), mesh=mesh, name="sc_gather_rows",
               scratch_types=[pltpu.VMEM((rows_per,), jnp.int32), pltpu.VMEM((W, D), x.dtype)])
    def gather_kernel(x_hbm, i_hbm, o_hbm, idx_vmem, buf):
        wid = lax.axis_index("core") * mesh.num_subcores + lax.axis_index("subcore")
        base = wid * rows_per
        pltpu.sync_copy(i_hbm.at[pl.ds(base, rows_per)], idx_vmem)

        @pl.loop(0, rows_per // W)
        def _(j):
            pltpu.sync_copy(x_hbm.at[idx_vmem.at[pl.ds(j * W, W)]], buf)
            pltpu.sync_copy(buf, o_hbm.at[pl.ds(base + j * W, W)])

    return gather_kernel(x, idx)


def _sc_scatter_rows(x, idx, n_out):
    K, N = idx.shape
    D = x.shape[1]
    W = SC_GATHER_WINDOW
    mesh = plsc.VectorSubcoreMesh(core_axis_name="core", subcore_axis_name="subcore")
    n_workers = mesh.num_cores * mesh.num_subcores
    rows_per = N // n_workers
    assert rows_per * n_workers == N and rows_per % W == 0

    @pl.kernel(out_type=jax.ShapeDtypeStruct((n_out, D), x.dtype), mesh=mesh, name="sc_scatter_rows",
               scratch_types=[pltpu.VMEM((K * rows_per,), jnp.int32), pltpu.VMEM((W, D), x.dtype),
                              pltpu.SemaphoreType.DMA((K,))])
    def scatter_kernel(x_hbm, i_hbm, o_hbm, idx_vmem, buf, sem):
        wid = lax.axis_index("core") * mesh.num_subcores + lax.axis_index("subcore")
        base = wid * rows_per
        for k in range(K):
            pltpu.sync_copy(i_hbm.at[pl.ds(k * N + base, rows_per)], idx_vmem.at[pl.ds(k * rows_per, rows_per)])

        @pl.loop(0, rows_per // W)
        def _(j):
            pltpu.sync_copy(x_hbm.at[pl.ds(base + j * W, W)], buf)
            copies = [pltpu.make_async_copy(buf, o_hbm.at[idx_vmem.at[pl.ds(k * rows_per + j * W, W)]], sem.at[k])
                      for k in range(K)]
            for cp in copies:
                cp.start()
            for cp in copies:
                cp.wait()

    return scatter_kernel(x, idx.reshape(K * N))


def _combine_kernel(x_ref, yg_ref, route_ref, ln3g_ref, ln3b_ref, *rest):
    out_ref = rest[-1]
    x = x_ref[...]
    g = jnp.transpose(route_ref[...])
    y = jnp.zeros_like(x)
    for k in range(TOP_K):
        y = y + g[:, TOP_K + k:TOP_K + k + 1] * _unpack_bf16_pairs(yg_ref[k * CMB_TILE:(k + 1) * CMB_TILE, :])
    out_ref[...] = _layer_norm(DEEPNORM_ALPHA * x + y, ln3g_ref[...], ln3b_ref[...])


def _combine(x2, yg, route, p, l, first, acc):
    N, D = x2.shape
    T = CMB_TILE
    tiles = yg.shape[0] // (TOP_K * T)
    in_specs = [pl.BlockSpec((T, D), lambda i: (first + i, 0)), pl.BlockSpec((TOP_K * T, D // 2), lambda i: (i, 0)),
                pl.BlockSpec((ROUTE_ROWS, T), lambda i: (0, first + i)), _layer_spec(p['ln3g'], l), _layer_spec(p['ln3b'], l)]
    args = [x2, yg, route, p['ln3g'], p['ln3b']]
    aliases = {}
    if acc is not None:
        in_specs.append(pl.BlockSpec(memory_space=pl.ANY))
        args.append(acc)
        aliases = {len(args) - 1: 0}
    return pl.pallas_call(
        _combine_kernel,
        out_shape=jax.ShapeDtypeStruct((N, D), F32),
        grid=(tiles,),
        in_specs=in_specs,
        out_specs=pl.BlockSpec((T, D), lambda i: (first + i, 0)),
        input_output_aliases=aliases,
        compiler_params=pltpu.CompilerParams(dimension_semantics=("arbitrary",), vmem_limit_bytes=VMEM_LIMIT),
        name="combine",
    )(*args)


def _prep(w_in, b_in, gla_wg2, gla_bg, gla_norm_g, sgu_ln_g, sgu_ln_b, sgu_ws, sgu_bs, pool_w, pool_scale,
          w_up_a, w_up_b, w_up_c, w_o, ln1_g, ln1_b, xa_wq, xa_wk, xa_wv, xa_wo, ln2_g, ln2_b,
          router_w, router_b, ln3_g, ln3_b):
    L = w_in.shape[0]
    o_glow = O_GLOW
    o_uv = o_glow + GLA_RANK
    row = lambda a: a.reshape(L, 1, -1).astype(F32)
    pad_last = lambda a, n: jnp.pad(a, [(0, 0)] * (a.ndim - 1) + [(0, n - a.shape[-1])])
    p = {}
    p['whead'] = w_in[..., :o_glow].astype(BF16)
    p['wglow'] = pad_last(w_in[..., o_glow:o_uv], LANES).astype(BF16)
    p['wtail'] = w_in[..., o_uv:].astype(BF16)
    p['bcat'] = row(jnp.concatenate([b_in[..., :o_glow], pad_last(b_in[..., o_glow:o_uv], LANES), b_in[..., o_uv:]],
                                    axis=-1))
    p['wg2'] = jnp.pad(gla_wg2, ((0, 0), (0, LANES - GLA_RANK), (0, 0))).astype(BF16)
    p['bg'] = row(gla_bg)
    p['gnorm'] = row(gla_norm_g)
    p['slng'] = row(sgu_ln_g)
    p['slnb'] = row(sgu_ln_b)
    p['wtril'] = jnp.tril(sgu_ws).astype(BF16)
    p['sbias'] = jnp.repeat(jnp.swapaxes(sgu_bs, 1, 2), SGU_GD, axis=2).astype(F32)
    G = len(POOL_WINDOWS)
    eye = jnp.eye(G, dtype=F32)
    p['poolw'] = jnp.einsum('lgcd,gh->lgchd', pool_w, eye).reshape(L, POOL_WIDTH, POOL_WIDTH).astype(BF16)
    p['pscale'] = row(pool_scale)
    p['wupa'], p['wupb'], p['wupc'], p['wo'] = w_up_a, w_up_b, w_up_c, w_o
    p['ln1g'], p['ln1b'] = row(ln1_g), row(ln1_b)
    p['wq'], p['wk'], p['wv'], p['wxo'] = xa_wq, xa_wk, xa_wv, xa_wo
    p['ln2g'], p['ln2b'] = row(ln2_g), row(ln2_b)
    rwt = jnp.swapaxes(router_w, 1, 2)
    rwt_hi = rwt.astype(BF16)
    p['rwt'] = jnp.concatenate([rwt_hi, (rwt - rwt_hi.astype(F32)).astype(BF16)], axis=1)
    p['rbt'] = jnp.broadcast_to(router_b[:, :, None], router_b.shape + (XA_TILE,)).astype(F32)
    p['ln3g'], p['ln3b'] = row(ln3_g), row(ln3_b)
    return p


def _route(route, counts):
    N = route.shape[1]
    top_idx = route[0:TOP_K].astype(jnp.int32)
    rank = route[2 * TOP_K:3 * TOP_K].astype(jnp.int32)
    counts = counts[:, 0].astype(jnp.int32)
    R = MOE_STEP_BLOCKS * MOE_BLOCK
    blocks = (counts + MOE_BLOCK - 1) // MOE_BLOCK
    padded = ((counts + R - 1) // R) * R
    pad_end = jnp.cumsum(padded)
    pad_start = pad_end - padded
    ids = jnp.arange(N_EXPERTS, dtype=jnp.int32)
    start_of = jnp.sum(jnp.where(top_idx[:, :, None] == ids[None, None, :], pad_start[None, None, :], 0), axis=-1)
    dest = start_of + rank
    n_steps = N * TOP_K // R + N_EXPERTS
    step_start = jnp.arange(n_steps, dtype=jnp.int32) * R
    step_expert = jnp.minimum(jnp.sum((pad_end[None, :] <= step_start[:, None]).astype(jnp.int32), axis=1),
                              N_EXPERTS - 1)
    mine = step_expert[:, None] == ids[None, :]
    blocks_before = (step_start - jnp.sum(jnp.where(mine, pad_start[None, :], 0), axis=1)) // MOE_BLOCK
    step_halves = jnp.clip(jnp.sum(jnp.where(mine, blocks[None, :], 0), axis=1) - blocks_before, 0, MOE_STEP_BLOCKS)
    step_halves = jnp.where(step_start < pad_end[-1], step_halves, 0).astype(jnp.int32)
    n_used = (pad_end[-1] // R).astype(jnp.int32).reshape(1)
    later = jnp.where((ids[None, :] > ids[:, None]) & (counts[None, :] > 0), ids[None, :], N_EXPERTS)
    first_later = jnp.min(later, axis=1)
    next_expert = jnp.where(first_later < N_EXPERTS, first_later, ids).astype(jnp.int32)
    return dest, step_expert, step_halves, n_used, next_expert


def kernel(x, mem, w_in, b_in, gla_wg2, gla_bg, gla_norm_g, sgu_ln_g, sgu_ln_b, sgu_ws, sgu_bs, pool_w, pool_scale, w_up_a, w_up_b, w_up_c, w_o, ln1_g, ln1_b, xa_wq, xa_wk, xa_wv, xa_wo, ln2_g, ln2_b, router_w, router_b, exp_w_gu, exp_b_gu, exp_w_down, exp_b_down, ln3_g, ln3_b):
    B, S, D = x.shape
    N = B * S
    p = _prep(w_in, b_in, gla_wg2, gla_bg, gla_norm_g, sgu_ln_g, sgu_ln_b, sgu_ws, sgu_bs, pool_w, pool_scale,
              w_up_a, w_up_b, w_up_c, w_o, ln1_g, ln1_b, xa_wq, xa_wk, xa_wv, xa_wo, ln2_g, ln2_b,
              router_w, router_b, ln3_g, ln3_b)
    memt = jnp.swapaxes(mem, 1, 2)
    for l in range(DEPTH):
        x1 = _mixer(x, p, l)
        kt, v = _memkv(mem, memt, p, l)
        x2, x2p, route, counts = _xattn(x1, kt, v, p, l)
        dest, step_expert, step_halves, n_used, next_expert = _route(route, counts)
        n_slots = N * TOP_K + N_EXPERTS * MOE_STEP_BLOCKS * MOE_BLOCK
        xs = _sc_scatter_rows(x2p.reshape(N, D // 2), dest, n_slots)
        ys = _experts(step_expert, step_halves, n_used, next_expert, xs, exp_w_gu, exp_b_gu, exp_w_down, exp_b_down, l)
        dest_km = dest.reshape(TOP_K, N // CMB_TILE, CMB_TILE).transpose(1, 0, 2).reshape(-1)
        tile_rows = TOP_K * CMB_TILE
        acc, first = None, 0
        for share in COMBINE_SHARES:
            tiles = (N // CMB_TILE) * share // sum(COMBINE_SHARES)
            yg = _sc_gather_rows(ys, dest_km[first * tile_rows:(first + tiles) * tile_rows])
            acc = _combine(x2.reshape(N, D), yg, route, p, l, first, acc)
            first += tiles
        x = acc.reshape(B, S, D)
    return x
```

```python
import functools

import jax
import jax.numpy as jnp
from jax import lax
from jax.experimental import pallas as pl
from jax.experimental.pallas import tpu as pltpu
from jax.experimental.pallas import tpu_sc as plsc

F32 = jnp.float32
BF16 = jnp.bfloat16

D_MODEL = 1024
DEPTH = 2
GLA_HEADS = 4
GLA_KEY = 256
GLA_VAL = 512
GLA_DK = 64
GLA_DV = 128
GLA_RANK = 16
GLA_TAU = 16.0
GLA_CHUNK = 64
SGU_GROUPS = 4
SGU_WIDTH = 256
SGU_GD = 64
SGU_CHUNK = 128
POOL_WINDOWS = (2, 4, 8, 16)
POOL_WIDTH = 256
POOL_GD = 64
POOL_CARRY = 32
MEM_LEN = 256
XA_HEADS = 4
XA_DH = 256
N_EXPERTS = 32
TOP_K = 4
EXPERT_FF = 1024
SWIGLU_LIMIT = 7.0
SWIGLU_ALPHA = 1.702
DEEPNORM_ALPHA = (2 * DEPTH) ** 0.25
LN_EPS = 1e-5
LANES = 128
VMEM_LIMIT = 56 * 1024 * 1024

MIX_TILE = 512
GLA_SUB = 256
XA_TILE = 1024
MOE_BLOCK = 256
MOE_STEP_BLOCKS = 3
CMB_TILE = 512
COMBINE_SHARES = (1, 1, 2, 2, 2)
ROUTE_ROWS = 16
SC_GATHER_WINDOW = 128

O_QKVR = 0
O_GLOW = 2 * GLA_KEY + 2 * GLA_VAL
O_UV = O_GLOW + LANES
O_XC = O_UV + 2 * SGU_WIDTH
O_GATE = O_XC + POOL_WIDTH


def _dot(a, b):
    return jnp.dot(a, b, preferred_element_type=F32)


def _dot_t0(a, b):
    return lax.dot_general(a, b, (((0,), (0,)), ((), ())), preferred_element_type=F32)


def _dot_t1(a, b):
    return lax.dot_general(a, b, (((1,), (1,)), ((), ())), preferred_element_type=F32)


def _split_bf16(x):
    hi = x.astype(BF16)
    lo = (x - hi.astype(F32)).astype(BF16)
    return hi, lo


def _layer_norm(x, g, b):
    mu = jnp.mean(x, axis=-1, keepdims=True)
    xc = x - mu
    var = jnp.mean(xc * xc, axis=-1, keepdims=True)
    return xc * lax.rsqrt(var + LN_EPS) * g + b


def _sigmoid(x):
    return 1.0 / (1.0 + jnp.exp(-x))


def _pack_bf16_pairs(x):
    H = x.shape[1] // 2
    bits = lax.bitcast_convert_type(x.astype(BF16).astype(F32), jnp.uint32)
    return (bits[:, :H] >> 16) | (bits[:, H:] & jnp.uint32(0xFFFF0000))


def _unpack_bf16_pairs(w):
    lo = lax.bitcast_convert_type(w << 16, F32)
    hi = lax.bitcast_convert_type(w & jnp.uint32(0xFFFF0000), F32)
    return jnp.concatenate([lo, hi], axis=1)


def _layer_spec(arr, l):
    nd = arr.ndim - 1
    return pl.BlockSpec((None,) + arr.shape[1:], lambda *_: (l,) + (0,) * nd, pipeline_mode=pl.Buffered(1))


def _mixer_kernel(x_ref, whead_ref, wglow_ref, wtail_ref, bcat_ref, wg2_ref, bg_ref, gnorm_ref,
                  slng_ref, slnb_ref, wtril_ref, sbias_ref, poolw_ref, pscale_ref,
                  wupa_ref, wupb_ref, wupc_ref, wo_ref, ln1g_ref, ln1b_ref,
                  out_ref,
                  state_ref, qkvr_ref, ya_ref, vln_ref, e_ref, s2_ref, s4_ref, s8_ref):
    T = MIX_TILE
    D = D_MODEL
    j = pl.program_id(1)
    x = x_ref[0]
    xb = x.astype(BF16)

    def proj(lo, hi):
        if hi <= O_GLOW:
            w = whead_ref[:, lo:hi]
        elif lo == O_GLOW and hi == O_UV:
            w = wglow_ref[...]
        else:
            w = wtail_ref[:, lo - O_UV:hi - O_UV]
        return _dot(xb, w) + bcat_ref[:, lo:hi]

    @pl.when(j == 0)
    def _():
        state_ref[...] = jnp.zeros_like(state_ref)
        e_ref[0:POOL_CARRY, :] = jnp.zeros((POOL_CARRY, POOL_WIDTH), F32)

    @pl.when(j > 0)
    def _():
        e_ref[0:POOL_CARRY, :] = e_ref[T:T + POOL_CARRY, :]

    qkvr_ref[...] = proj(O_QKVR, O_GLOW)
    glow = proj(O_GLOW, O_UV)
    z = _dot(glow.astype(BF16), wg2_ref[...]) + bg_ref[...]
    la = (jnp.minimum(z, 0.0) - jnp.log1p(jnp.exp(-jnp.abs(z)))) * (1.0 / GLA_TAU)
    la_hi, la_lo = _split_bf16(la)
    gate_a = _sigmoid(proj(O_GATE, O_GATE + D))

    C = GLA_CHUNK
    G = GLA_SUB
    NC = G // C
    CSH = C.bit_length() - 1
    row = lax.broadcasted_iota(jnp.int32, (G, G), 0)
    col = lax.broadcasted_iota(jnp.int32, (G, G), 1)
    same_chunk = (row >> CSH) == (col >> CSH)
    causal = same_chunk & (row >= col)
    causal_bf = jnp.where(causal, 1.0, 0.0).astype(BF16)
    lane = lax.broadcasted_iota(jnp.int32, (G, GLA_KEY), 1)
    gate_b = None
    for g0 in range(0, T, G):
        gr = slice(g0, g0 + G)
        lh, ll = la_hi[gr], la_lo[gr]
        b = _dot(causal_bf, lh) + _dot(causal_bf, ll)
        b_last = [b[(c + 1) * C - 1:(c + 1) * C, :] for c in range(NC)]
        b_end = jnp.concatenate([jnp.broadcast_to(r, (C, GLA_KEY)) for r in b_last], axis=0)
        q = qkvr_ref[gr, 0:GLA_KEY]
        k = qkvr_ref[gr, GLA_KEY:2 * GLA_KEY]
        v = qkvr_ref[gr, 2 * GLA_KEY:2 * GLA_KEY + GLA_VAL].astype(BF16)
        q_dec = q * (GLA_DK ** -0.5) * jnp.exp(b)
        k_dec = (k * jnp.exp(-b)).astype(BF16)
        k_tail = (k * jnp.exp(b_end - b)).astype(BF16)
        q_dec_bf = q_dec.astype(BF16)
        o_heads = []
        for h in range(GLA_HEADS):
            q_h = jnp.where((lane >= h * GLA_DK) & (lane < (h + 1) * GLA_DK), q_dec, 0.0).astype(BF16)
            scores = jnp.where(causal, _dot_t1(q_h, k_dec), 0.0).astype(BF16)
            o_heads.append(_dot(scores, v[:, h * GLA_DV:(h + 1) * GLA_DV]))
        o_intra = jnp.concatenate(o_heads, axis=1)
        if gate_b is None:
            gate_b = _sigmoid(proj(O_GATE + D, O_GATE + 2 * D))
        dec_cols = jnp.exp(jnp.concatenate(b_last + [jnp.zeros((8 - NC, GLA_KEY), F32)], axis=0)).T
        dec_all = jnp.concatenate([jnp.broadcast_to(dec_cols[:, c:c + 1], (GLA_KEY, LANES)) for c in range(NC)], axis=1)
        o_inter = []
        for c in range(NC):
            rows = slice(c * C, (c + 1) * C)
            o_inter.append(_dot(q_dec_bf[rows], state_ref[...].astype(BF16)))
            kv = _dot_t0(k_tail[rows], v[rows])
            for h in range(GLA_HEADS):
                rs = slice(h * GLA_DK, (h + 1) * GLA_DK)
                cs = slice(h * GLA_DV, (h + 1) * GLA_DV)
                state_ref[rs, cs] = dec_all[rs, c * LANES:(c + 1) * LANES] * state_ref[rs, cs] + kv[rs, cs]
        o = o_intra + jnp.concatenate(o_inter, axis=0)
        for h in range(GLA_HEADS):
            cs = slice(h * GLA_DV, (h + 1) * GLA_DV)
            o_h = o[:, cs]
            ms = jnp.mean(o_h * o_h, axis=-1, keepdims=True)
            o_h = o_h * lax.rsqrt(ms + LN_EPS) * gnorm_ref[:, cs]
            r_h = qkvr_ref[gr, 2 * GLA_KEY + GLA_VAL + h * GLA_DV:2 * GLA_KEY + GLA_VAL + (h + 1) * GLA_DV]
            ya_ref[gr, cs] = (o_h * (r_h * _sigmoid(r_h))).astype(BF16)

    gate_c = _sigmoid(proj(O_GATE + 2 * D, O_GATE + 3 * D))
    uv = proj(O_UV, O_XC)
    zg = 0.5 * uv * (1.0 + lax.erf(uv * (2.0 ** -0.5)))
    u = zg[:, :SGU_WIDTH]
    vln_ref[...] = _layer_norm(zg[:, SGU_WIDTH:], slng_ref[...], slnb_ref[...])
    lane_s = lax.broadcasted_iota(jnp.int32, (SGU_CHUNK, SGU_WIDTH), 1)
    s_parts = []
    for n in range(T // SGU_CHUNK):
        vc = vln_ref[n * SGU_CHUNK:(n + 1) * SGU_CHUNK, :]
        s = sbias_ref[...]
        for g in range(SGU_GROUPS):
            vg = jnp.where((lane_s >= g * SGU_GD) & (lane_s < (g + 1) * SGU_GD), vc, 0.0).astype(BF16)
            s = s + _dot(wtril_ref[g], vg)
        s_parts.append(s)
    y_b = (u * jnp.concatenate(s_parts, axis=0)).astype(BF16)

    P = POOL_CARRY
    xc = proj(O_XC, O_GATE)
    e_ref[P:P + T, :] = xc
    s2_ref[8:P + T, :] = e_ref[8:P + T, :] + e_ref[7:P + T - 1, :]
    s4_ref[16:P + T, :] = s2_ref[16:P + T, :] + s2_ref[14:P + T - 2, :]
    s8_ref[24:P + T, :] = s4_ref[24:P + T, :] + s4_ref[20:P + T - 4, :]
    s16 = s8_ref[P:P + T, :] + s8_ref[P - 8:P + T - 8, :]
    lane_p = lax.broadcasted_iota(jnp.int32, (T, POOL_WIDTH), 1)
    tpos = lax.broadcasted_iota(jnp.int32, (T, POOL_WIDTH), 0) + (j * T + 1)
    grp = lane_p >> (POOL_GD.bit_length() - 1)
    win = jnp.where(grp == 0, POOL_WINDOWS[0], jnp.where(grp == 1, POOL_WINDOWS[1],
                    jnp.where(grp == 2, POOL_WINDOWS[2], POOL_WINDOWS[3])))
    wsum = jnp.where(grp == 0, s2_ref[P:P + T, :], jnp.where(grp == 1, s4_ref[P:P + T, :],
                     jnp.where(grp == 2, s8_ref[P:P + T, :], s16)))
    count = jnp.minimum(tpos, win).astype(F32)
    pooled = wsum / count - xc
    y_c = (_dot(pooled.astype(BF16), poolw_ref[...]) * pscale_ref[...]).astype(BF16)

    merged = gate_a * _dot(ya_ref[...], wupa_ref[...].astype(BF16))
    merged += gate_b * _dot(y_b, wupb_ref[...].astype(BF16))
    merged += gate_c * _dot(y_c, wupc_ref[...].astype(BF16))
    h = _dot(merged.astype(BF16), wo_ref[...].astype(BF16))
    out_ref[0] = _layer_norm(DEEPNORM_ALPHA * x + h, ln1g_ref[...], ln1b_ref[...])


_MIXER_WEIGHTS = ('whead', 'wglow', 'wtail', 'bcat', 'wg2', 'bg', 'gnorm', 'slng', 'slnb', 'wtril', 'sbias', 'poolw', 'pscale',
                  'wupa', 'wupb', 'wupc', 'wo', 'ln1g', 'ln1b')


def _mixer(x, p, l):
    B, S, D = x.shape
    T = MIX_TILE
    weights = [p[n] for n in _MIXER_WEIGHTS]
    return pl.pallas_call(
        _mixer_kernel,
        out_shape=jax.ShapeDtypeStruct((B, S, D), F32),
        grid=(B, S // T),
        in_specs=[pl.BlockSpec((1, T, D), lambda b, j: (b, j, 0))] + [_layer_spec(w, l) for w in weights],
        out_specs=pl.BlockSpec((1, T, D), lambda b, j: (b, j, 0)),
        scratch_shapes=[
            pltpu.VMEM((GLA_KEY, GLA_VAL), F32),
            pltpu.VMEM((T, 2 * GLA_KEY + 2 * GLA_VAL), F32),
            pltpu.VMEM((T, GLA_VAL), BF16),
            pltpu.VMEM((T, SGU_WIDTH), F32),
            pltpu.VMEM((T + POOL_CARRY, POOL_WIDTH), F32),
            pltpu.VMEM((T + POOL_CARRY, POOL_WIDTH), F32),
            pltpu.VMEM((T + POOL_CARRY, POOL_WIDTH), F32),
            pltpu.VMEM((T + POOL_CARRY, POOL_WIDTH), F32),
        ],
        compiler_params=pltpu.CompilerParams(dimension_semantics=("arbitrary", "arbitrary"),
                                             vmem_limit_bytes=VMEM_LIMIT),
        name="mixer",
    )(x, *weights)


def _memkv_kernel(memt_ref, mem_ref, wk_ref, wv_ref, kt_ref, v_ref):
    kt_ref[0] = _dot_t0(wk_ref[...].astype(BF16), memt_ref[0].astype(BF16)).astype(BF16)
    v_ref[0] = _dot(mem_ref[0].astype(BF16), wv_ref[...].astype(BF16)).astype(BF16)


def _memkv(mem, memt, p, l):
    B, M, D = mem.shape
    return pl.pallas_call(
        _memkv_kernel,
        out_shape=(jax.ShapeDtypeStruct((B, D, M), BF16), jax.ShapeDtypeStruct((B, M, D), BF16)),
        grid=(B,),
        in_specs=[pl.BlockSpec((1, D, M), lambda b: (b, 0, 0)), pl.BlockSpec((1, M, D), lambda b: (b, 0, 0)),
                  _layer_spec(p['wk'], l), _layer_spec(p['wv'], l)],
        out_specs=(pl.BlockSpec((1, D, M), lambda b: (b, 0, 0)), pl.BlockSpec((1, M, D), lambda b: (b, 0, 0))),
        compiler_params=pltpu.CompilerParams(dimension_semantics=("arbitrary",), vmem_limit_bytes=VMEM_LIMIT),
        name="memkv",
    )(memt, mem, p['wk'], p['wv'])


def _xattn_kernel(x_ref, kt_ref, v_ref, wq_ref, wo_ref, ln2g_ref, ln2b_ref, rwt_ref, rbt_ref,
                  x2_ref, x2p_ref, route_ref, counts_ref, carry_ref):
    T = XA_TILE

    @pl.when((pl.program_id(0) == 0) & (pl.program_id(1) == 0))
    def _():
        carry_ref[...] = jnp.zeros_like(carry_ref)

    x = x_ref[0]
    q = (_dot(x.astype(BF16), wq_ref[...].astype(BF16)) * (XA_DH ** -0.5)).astype(BF16)
    h = jnp.zeros_like(x)
    for hd in range(XA_HEADS):
        cs = slice(hd * XA_DH, (hd + 1) * XA_DH)
        s = _dot(q[:, cs], kt_ref[0, cs, :])
        e = jnp.exp(s - jnp.max(s, axis=-1, keepdims=True))
        o = _dot(e.astype(BF16), v_ref[0, :, cs]) / jnp.sum(e, axis=-1, keepdims=True)
        h = h + _dot(o.astype(BF16), wo_ref[cs, :].astype(BF16))
    x2 = _layer_norm(DEEPNORM_ALPHA * x + h, ln2g_ref[...], ln2b_ref[...])
    x2_ref[0] = x2
    x2p_ref[0] = _pack_bf16_pairs(x2)

    E = N_EXPERTS
    hi, lo = _split_bf16(x2)
    lt = _dot_t1(rwt_ref[...], hi)
    logits = lt[0:E] + (lt[E:2 * E] + _dot_t1(rwt_ref[0:E, :], lo)) + rbt_ref[...]

    eid = lax.broadcasted_iota(jnp.int32, (E, T), 0)
    neg_inf = jnp.float32(-jnp.inf)
    rest = logits
    tops, picks = [], []
    for _ in range(TOP_K):
        m = jnp.max(rest, axis=0, keepdims=True)
        idx = jnp.min(jnp.where(rest == m, eid, E), axis=0, keepdims=True)
        pick = eid == idx
        rest = jnp.where(pick, neg_inf, rest)
        tops.append((m, idx))
        picks.append(pick)
    exps = [jnp.exp(m - tops[0][0]) for m, _ in tops]
    denom = exps[0]
    for e in exps[1:]:
        denom = denom + e

    chosen = jnp.zeros((E, T), F32)
    for pick in picks:
        chosen = chosen + jnp.where(pick, 1.0, 0.0)
    chosen_bf = chosen.astype(BF16)
    earlier = (lax.broadcasted_iota(jnp.int32, (T, T), 0) < lax.broadcasted_iota(jnp.int32, (T, T), 1))
    carry = carry_ref[...]
    before = _dot(chosen_bf, jnp.where(earlier, 1.0, 0.0).astype(BF16)) + jnp.concatenate([carry] * (T // LANES), axis=1)
    carry = carry + _dot(chosen_bf, jnp.ones((T, LANES), BF16))
    carry_ref[...] = carry
    counts_ref[...] = carry

    rid = lax.broadcasted_iota(jnp.int32, (ROUTE_ROWS, T), 0)
    route = jnp.zeros((ROUTE_ROWS, T), F32)
    for k in range(TOP_K):
        rank = jnp.sum(jnp.where(picks[k], before, 0.0), axis=0, keepdims=True)
        route = jnp.where(rid == k, tops[k][1].astype(F32), route)
        route = jnp.where(rid == TOP_K + k, exps[k] / denom, route)
        route = jnp.where(rid == 2 * TOP_K + k, rank, route)
    route_ref[...] = route


def _xattn(x, kt, v, p, l):
    B, S, D = x.shape
    T = XA_TILE
    M = MEM_LEN
    weights = [p[n] for n in ('wq', 'wxo', 'ln2g', 'ln2b', 'rwt', 'rbt')]
    return pl.pallas_call(
        _xattn_kernel,
        out_shape=(jax.ShapeDtypeStruct((B, S, D), F32), jax.ShapeDtypeStruct((B, S, D // 2), jnp.uint32),
                   jax.ShapeDtypeStruct((ROUTE_ROWS, B * S), F32), jax.ShapeDtypeStruct((N_EXPERTS, LANES), F32)),
        grid=(B, S // T),
        in_specs=[pl.BlockSpec((1, T, D), lambda b, j: (b, j, 0)),
                  pl.BlockSpec((1, D, M), lambda b, j: (b, 0, 0)),
                  pl.BlockSpec((1, M, D), lambda b, j: (b, 0, 0))] + [_layer_spec(w, l) for w in weights],
        out_specs=(pl.BlockSpec((1, T, D), lambda b, j: (b, j, 0)),
                   pl.BlockSpec((1, T, D // 2), lambda b, j: (b, j, 0)),
                   pl.BlockSpec((ROUTE_ROWS, T), lambda b, j: (0, b * (S // T) + j)),
                   pl.BlockSpec((N_EXPERTS, LANES), lambda b, j: (0, 0))),
        scratch_shapes=[pltpu.VMEM((N_EXPERTS, LANES), F32)],
        compiler_params=pltpu.CompilerParams(dimension_semantics=("arbitrary", "arbitrary"),
                                             vmem_limit_bytes=VMEM_LIMIT),
        name="xattn",
    )(x, kt, v, *weights)


def _expert_kernel(layer, se_ref, sh_ref, nu_ref, nxt_ref, xs_ref, wgu_hbm, bgu_ref, wd_hbm, bd_ref, out_ref,
                   wgu_st, wd_st, slot_ref, sem):
    i = pl.program_id(0)
    F = EXPERT_FF
    halves = sh_ref[i]

    def weight_copies(e, slot):
        return (pltpu.make_async_copy(wgu_hbm.at[layer, e], wgu_st.at[slot], sem.at[slot, 0]),
                pltpu.make_async_copy(wd_hbm.at[layer, e], wd_st.at[slot], sem.at[slot, 1]))

    def ffn(rows):
        slot = slot_ref[0]
        xsb = _unpack_bf16_pairs(xs_ref[rows, :]).astype(BF16)
        hh = _dot(xsb, wgu_st[slot].astype(BF16)) + bgu_ref[...]
        h_glu = jnp.minimum(hh[:, :F], SWIGLU_LIMIT)
        h_lin = jnp.clip(hh[:, F:], -SWIGLU_LIMIT, SWIGLU_LIMIT)
        a = h_glu * _sigmoid(SWIGLU_ALPHA * h_glu) * (h_lin + 1.0)
        out_ref[rows, :] = _pack_bf16_pairs(_dot(a.astype(BF16), wd_st[slot].astype(BF16)) + bd_ref[...])

    @pl.when(halves > 0)
    def _():
        e = se_ref[i]
        prev = se_ref[jnp.maximum(i - 1, 0)]

        @pl.when(i == 0)
        def _():
            slot_ref[0] = 1
            for cp in weight_copies(e, 0):
                cp.start()

        @pl.when((i == 0) | (e != prev))
        def _():
            slot = 1 - slot_ref[0]
            slot_ref[0] = slot
            for cp in weight_copies(e, slot):
                cp.wait()
            nxt = nxt_ref[e]

            @pl.when(nxt != e)
            def _():
                for cp in weight_copies(nxt, 1 - slot):
                    cp.start()

    for n in range(1, MOE_STEP_BLOCKS + 1):
        @pl.when(halves == n)
        def _(n=n):
            ffn(slice(0, n * MOE_BLOCK))


def _experts(step_expert, step_halves, n_used, next_expert, xs, w_gu, b_gu, w_down, b_down, l):
    P, DH = xs.shape
    D = 2 * DH
    R = MOE_STEP_BLOCKS * MOE_BLOCK
    NS = P // R
    F2 = 2 * EXPERT_FF

    def row_map(i, se, sh, nu, nxt):
        return (jnp.minimum(i, nu[0] - 1), 0)

    def exp_map(i, se, sh, nu, nxt):
        return (l, se[jnp.minimum(i, nu[0] - 1)], 0, 0)

    grid_spec = pltpu.PrefetchScalarGridSpec(
        num_scalar_prefetch=4,
        grid=(NS,),
        in_specs=[pl.BlockSpec((R, DH), row_map),
                  pl.BlockSpec(memory_space=pl.ANY),
                  pl.BlockSpec((None, None, 1, F2), exp_map),
                  pl.BlockSpec(memory_space=pl.ANY),
                  pl.BlockSpec((None, None, 1, D), exp_map)],
        out_specs=pl.BlockSpec((R, DH), row_map),
        scratch_shapes=[pltpu.VMEM((2, D, F2), F32), pltpu.VMEM((2, EXPERT_FF, D), F32),
                        pltpu.SMEM((1,), jnp.int32),
                        pltpu.SemaphoreType.DMA((2, 2))],
    )
    return pl.pallas_call(
        functools.partial(_expert_kernel, l),
        out_shape=jax.ShapeDtypeStruct((P, DH), jnp.uint32),
        grid_spec=grid_spec,
        compiler_params=pltpu.CompilerParams(dimension_semantics=("arbitrary",), vmem_limit_bytes=VMEM_LIMIT),
        name="experts",
    )(step_expert, step_halves, n_used, next_expert, xs, w_gu, b_gu.reshape(DEPTH, N_EXPERTS, 1, F2), w_down,
      b_down.reshape(DEPTH, N_EXPERTS, 1, D))


def _sc_gather_rows(x, idx):
    M = idx.shape[0]
    D = x.shape[1]
    W = SC_GATHER_WINDOW
    mesh = plsc.VectorSubcoreMesh(core_axis_name="core", subcore_axis_name="subcore")
    n_workers = mesh.num_cores * mesh.num_subcores
    rows_per = M // n_workers
    assert rows_per * n_workers == M and rows_per % W == 0

    @pl.kernel(out_type=jax.ShapeDtypeStruct((M, D), x.dtype), mesh=mesh, name="sc_gather_rows",
               scratch_types=[pltpu.VMEM((rows_per,), jnp.int32), pltpu.VMEM((W, D), x.dtype)])
    def gather_kernel(x_hbm, i_hbm, o_hbm, idx_vmem, buf):
        wid = lax.axis_index("core") * mesh.num_subcores + lax.axis_index("subcore")
        base = wid * rows_per
        pltpu.sync_copy(i_hbm.at[pl.ds(base, rows_per)], idx_vmem)

        @pl.loop(0, rows_per // W)
        def _(j):
            pltpu.sync_copy(x_hbm.at[idx_vmem.at[pl.ds(j * W, W)]], buf)
            pltpu.sync_copy(buf, o_hbm.at[pl.ds(base + j * W, W)])

    return gather_kernel(x, idx)


def _sc_scatter_rows(x, idx, n_out):
    K, N = idx.shape
    D = x.shape[1]
    W = SC_GATHER_WINDOW
    mesh = plsc.VectorSubcoreMesh(core_axis_name="core", subcore_axis_name="subcore")
    n_workers = mesh.num_cores * mesh.num_subcores
    rows_per = N // n_workers
    assert rows_per * n_workers == N and rows_per % W == 0

    @pl.kernel(out_type=jax.ShapeDtypeStruct((n_out, D), x.dtype), mesh=mesh, name="sc_scatter_rows",
               scratch_types=[pltpu.VMEM((K * rows_per,), jnp.int32), pltpu.VMEM((W, D), x.dtype),
                              pltpu.SemaphoreType.DMA((K,))])
    def scatter_kernel(x_hbm, i_hbm, o_hbm, idx_vmem, buf, sem):
        wid = lax.axis_index("core") * mesh.num_subcores + lax.axis_index("subcore")
        base = wid * rows_per
        for k in range(K):
            pltpu.sync_copy(i_hbm.at[pl.ds(k * N + base, rows_per)], idx_vmem.at[pl.ds(k * rows_per, rows_per)])

        @pl.loop(0, rows_per // W)
        def _(j):
            pltpu.sync_copy(x_hbm.at[pl.ds(base + j * W, W)], buf)
            copies = [pltpu.make_async_copy(buf, o_hbm.at[idx_vmem.at[pl.ds(k * rows_per + j * W, W)]], sem.at[k])
                      for k in range(K)]
            for cp in copies:
                cp.start()
            for cp in copies:
                cp.wait()

    return scatter_kernel(x, idx.reshape(K * N))


def _combine_kernel(x_ref, yg_ref, route_ref, ln3g_ref, ln3b_ref, *rest):
    out_ref = rest[-1]
    x = x_ref[...]
    g = jnp.transpose(route_ref[...])
    y = jnp.zeros_like(x)
    for k in range(TOP_K):
        y = y + g[:, TOP_K + k:TOP_K + k + 1] * _unpack_bf16_pairs(yg_ref[k * CMB_TILE:(k + 1) * CMB_TILE, :])
    out_ref[...] = _layer_norm(DEEPNORM_ALPHA * x + y, ln3g_ref[...], ln3b_ref[...])


def _combine(x2, yg, route, p, l, first, acc):
    N, D = x2.shape
    T = CMB_TILE
    tiles = yg.shape[0] // (TOP_K * T)
    in_specs = [pl.BlockSpec((T, D), lambda i: (first + i, 0)), pl.BlockSpec((TOP_K * T, D // 2), lambda i: (i, 0)),
                pl.BlockSpec((ROUTE_ROWS, T), lambda i: (0, first + i)), _layer_spec(p['ln3g'], l), _layer_spec(p['ln3b'], l)]
    args = [x2, yg, route, p['ln3g'], p['ln3b']]
    aliases = {}
    if acc is not None:
        in_specs.append(pl.BlockSpec(memory_space=pl.ANY))
        args.append(acc)
        aliases = {len(args) - 1: 0}
    return pl.pallas_call(
        _combine_kernel,
        out_shape=jax.ShapeDtypeStruct((N, D), F32),
        grid=(tiles,),
        in_specs=in_specs,
        out_specs=pl.BlockSpec((T, D), lambda i: (first + i, 0)),
        input_output_aliases=aliases,
        compiler_params=pltpu.CompilerParams(dimension_semantics=("arbitrary",), vmem_limit_bytes=VMEM_LIMIT),
        name="combine",
    )(*args)


def _prep(w_in, b_in, gla_wg2, gla_bg, gla_norm_g, sgu_ln_g, sgu_ln_b, sgu_ws, sgu_bs, pool_w, pool_scale,
          w_up_a, w_up_b, w_up_c, w_o, ln1_g, ln1_b, xa_wq, xa_wk, xa_wv, xa_wo, ln2_g, ln2_b,
          router_w, router_b, ln3_g, ln3_b):
    L = w_in.shape[0]
    o_glow = O_GLOW
    o_uv = o_glow + GLA_RANK
    row = lambda a: a.reshape(L, 1, -1).astype(F32)
    pad_last = lambda a, n: jnp.pad(a, [(0, 0)] * (a.ndim - 1) + [(0, n - a.shape[-1])])
    p = {}
    p['whead'] = w_in[..., :o_glow].astype(BF16)
    p['wglow'] = pad_last(w_in[..., o_glow:o_uv], LANES).astype(BF16)
    p['wtail'] = w_in[..., o_uv:].astype(BF16)
    p['bcat'] = row(jnp.concatenate([b_in[..., :o_glow], pad_last(b_in[..., o_glow:o_uv], LANES), b_in[..., o_uv:]],
                                    axis=-1))
    p['wg2'] = jnp.pad(gla_wg2, ((0, 0), (0, LANES - GLA_RANK), (0, 0))).astype(BF16)
    p['bg'] = row(gla_bg)
    p['gnorm'] = row(gla_norm_g)
    p['slng'] = row(sgu_ln_g)
    p['slnb'] = row(sgu_ln_b)
    p['wtril'] = jnp.tril(sgu_ws).astype(BF16)
    p['sbias'] = jnp.repeat(jnp.swapaxes(sgu_bs, 1, 2), SGU_GD, axis=2).astype(F32)
    G = len(POOL_WINDOWS)
    eye = jnp.eye(G, dtype=F32)
    p['poolw'] = jnp.einsum('lgcd,gh->lgchd', pool_w, eye).reshape(L, POOL_WIDTH, POOL_WIDTH).astype(BF16)
    p['pscale'] = row(pool_scale)
    p['wupa'], p['wupb'], p['wupc'], p['wo'] = w_up_a, w_up_b, w_up_c, w_o
    p['ln1g'], p['ln1b'] = row(ln1_g), row(ln1_b)
    p['wq'], p['wk'], p['wv'], p['wxo'] = xa_wq, xa_wk, xa_wv, xa_wo
    p['ln2g'], p['ln2b'] = row(ln2_g), row(ln2_b)
    rwt = jnp.swapaxes(router_w, 1, 2)
    rwt_hi = rwt.astype(BF16)
    p['rwt'] = jnp.concatenate([rwt_hi, (rwt - rwt_hi.astype(F32)).astype(BF16)], axis=1)
    p['rbt'] = jnp.broadcast_to(router_b[:, :, None], router_b.shape + (XA_TILE,)).astype(F32)
    p['ln3g'], p['ln3b'] = row(ln3_g), row(ln3_b)
    return p


def _route(route, counts):
    N = route.shape[1]
    top_idx = route[0:TOP_K].astype(jnp.int32)
    rank = route[2 * TOP_K:3 * TOP_K].astype(jnp.int32)
    counts = counts[:, 0].astype(jnp.int32)
    R = MOE_STEP_BLOCKS * MOE_BLOCK
    blocks = (counts + MOE_BLOCK - 1) // MOE_BLOCK
    padded = ((counts + R - 1) // R) * R
    pad_end = jnp.cumsum(padded)
    pad_start = pad_end - padded
    ids = jnp.arange(N_EXPERTS, dtype=jnp.int32)
    start_of = jnp.sum(jnp.where(top_idx[:, :, None] == ids[None, None, :], pad_start[None, None, :], 0), axis=-1)
    dest = start_of + rank
    n_steps = -(-N * TOP_K // R) + N_EXPERTS
    step_start = jnp.arange(n_steps, dtype=jnp.int32) * R
    step_expert = jnp.minimum(jnp.sum((pad_end[None, :] <= step_start[:, None]).astype(jnp.int32), axis=1),
                              N_EXPERTS - 1)
    mine = step_expert[:, None] == ids[None, :]
    blocks_before = (step_start - jnp.sum(jnp.where(mine, pad_start[None, :], 0), axis=1)) // MOE_BLOCK
    step_halves = jnp.clip(jnp.sum(jnp.where(mine, blocks[None, :], 0), axis=1) - blocks_before, 0, MOE_STEP_BLOCKS)
    step_halves = jnp.where(step_start < pad_end[-1], step_halves, 0).astype(jnp.int32)
    n_used = (pad_end[-1] // R).astype(jnp.int32).reshape(1)
    later = jnp.where((ids[None, :] > ids[:, None]) & (counts[None, :] > 0), ids[None, :], N_EXPERTS)
    first_later = jnp.min(later, axis=1)
    next_expert = jnp.where(first_later < N_EXPERTS, first_later, ids).astype(jnp.int32)
    return dest, step_expert, step_halves, n_used, next_expert


def kernel(x, mem, w_in, b_in, gla_wg2, gla_bg, gla_norm_g, sgu_ln_g, sgu_ln_b, sgu_ws, sgu_bs, pool_w, pool_scale, w_up_a, w_up_b, w_up_c, w_o, ln1_g, ln1_b, xa_wq, xa_wk, xa_wv, xa_wo, ln2_g, ln2_b, router_w, router_b, exp_w_gu, exp_b_gu, exp_w_down, exp_b_down, ln3_g, ln3_b):
    B, S, D = x.shape
    N = B * S
    p = _prep(w_in, b_in, gla_wg2, gla_bg, gla_norm_g, sgu_ln_g, sgu_ln_b, sgu_ws, sgu_bs, pool_w, pool_scale,
              w_up_a, w_up_b, w_up_c, w_o, ln1_g, ln1_b, xa_wq, xa_wk, xa_wv, xa_wo, ln2_g, ln2_b,
              router_w, router_b, ln3_g, ln3_b)
    memt = jnp.swapaxes(mem, 1, 2)
    for l in range(DEPTH):
        x1 = _mixer(x, p, l)
        kt, v = _memkv(mem, memt, p, l)
        x2, x2p, route, counts = _xattn(x1, kt, v, p, l)
        dest, step_expert, step_halves, n_used, next_expert = _route(route, counts)
        n_slots = step_expert.shape[0] * MOE_STEP_BLOCKS * MOE_BLOCK
        xs = _sc_scatter_rows(x2p.reshape(N, D // 2), dest, n_slots)
        ys = _experts(step_expert, step_halves, n_used, next_expert, xs, exp_w_gu, exp_b_gu, exp_w_down, exp_b_down, l)
        dest_km = dest.reshape(TOP_K, N // CMB_TILE, CMB_TILE).transpose(1, 0, 2).reshape(-1)
        tile_rows = TOP_K * CMB_TILE
        acc, first = None, 0
        for share in COMBINE_SHARES:
            tiles = (N // CMB_TILE) * share // sum(COMBINE_SHARES)
            yg = _sc_gather_rows(ys, dest_km[first * tile_rows:(first + tiles) * tile_rows])
            acc = _combine(x2.reshape(N, D), yg, route, p, l, first, acc)
            first += tiles
        x = acc.reshape(B, S, D)
    return x
```

```python
import functools

import jax
import jax.numpy as jnp
from jax import lax
from jax.experimental import pallas as pl
from jax.experimental.pallas import tpu as pltpu
from jax.experimental.pallas import tpu_sc as plsc

F32 = jnp.float32
BF16 = jnp.bfloat16

D_MODEL = 1024
DEPTH = 2
GLA_HEADS = 4
GLA_KEY = 256
GLA_VAL = 512
GLA_DK = 64
GLA_DV = 128
GLA_RANK = 16
GLA_TAU = 16.0
GLA_CHUNK = 64
SGU_GROUPS = 4
SGU_WIDTH = 256
SGU_GD = 64
SGU_CHUNK = 128
POOL_WINDOWS = (2, 4, 8, 16)
POOL_WIDTH = 256
POOL_GD = 64
POOL_CARRY = 32
MEM_LEN = 256
XA_HEADS = 4
XA_DH = 256
N_EXPERTS = 32
TOP_K = 4
EXPERT_FF = 1024
SWIGLU_LIMIT = 7.0
SWIGLU_ALPHA = 1.702
DEEPNORM_ALPHA = (2 * DEPTH) ** 0.25
LN_EPS = 1e-5
LANES = 128
VMEM_LIMIT = 56 * 1024 * 1024

MIX_TILE = 512
GLA_SUB = 256
XA_TILE = 1024
MOE_BLOCK = 256
MOE_STEP_BLOCKS = 4
CMB_TILE = 512
TOKEN_GROUPS = 2
COMBINE_SHARES = (1, 1, 2, 2, 2)
ROUTE_ROWS = 16
SC_GATHER_WINDOW = 128

O_QKVR = 0
O_GLOW = 2 * GLA_KEY + 2 * GLA_VAL
O_UV = O_GLOW + LANES
O_XC = O_UV + 2 * SGU_WIDTH
O_GATE = O_XC + POOL_WIDTH


def _dot(a, b):
    return jnp.dot(a, b, preferred_element_type=F32)


def _dot_t0(a, b):
    return lax.dot_general(a, b, (((0,), (0,)), ((), ())), preferred_element_type=F32)


def _dot_t1(a, b):
    return lax.dot_general(a, b, (((1,), (1,)), ((), ())), preferred_element_type=F32)


def _split_bf16(x):
    hi = x.astype(BF16)
    lo = (x - hi.astype(F32)).astype(BF16)
    return hi, lo


def _layer_norm(x, g, b):
    mu = jnp.mean(x, axis=-1, keepdims=True)
    xc = x - mu
    var = jnp.mean(xc * xc, axis=-1, keepdims=True)
    return xc * lax.rsqrt(var + LN_EPS) * g + b


def _sigmoid(x):
    return 1.0 / (1.0 + jnp.exp(-x))


def _pack_bf16_pairs(x):
    H = x.shape[1] // 2
    bits = lax.bitcast_convert_type(x.astype(BF16).astype(F32), jnp.uint32)
    return (bits[:, :H] >> 16) | (bits[:, H:] & jnp.uint32(0xFFFF0000))


def _unpack_bf16_pairs(w):
    lo = lax.bitcast_convert_type(w << 16, F32)
    hi = lax.bitcast_convert_type(w & jnp.uint32(0xFFFF0000), F32)
    return jnp.concatenate([lo, hi], axis=1)


def _layer_spec(arr, l):
    nd = arr.ndim - 1
    return pl.BlockSpec((None,) + arr.shape[1:], lambda *_: (l,) + (0,) * nd, pipeline_mode=pl.Buffered(1))


def _mixer_kernel(x_ref, whead_ref, wglow_ref, wtail_ref, bcat_ref, wg2_ref, bg_ref, gnorm_ref,
                  slng_ref, slnb_ref, wtril_ref, sbias_ref, poolw_ref, pscale_ref,
                  wupa_ref, wupb_ref, wupc_ref, wo_ref, ln1g_ref, ln1b_ref,
                  out_ref,
                  state_ref, qkvr_ref, ya_ref, vln_ref, e_ref, s2_ref, s4_ref, s8_ref):
    T = MIX_TILE
    D = D_MODEL
    j = pl.program_id(1)
    x = x_ref[0]
    xb = x.astype(BF16)

    def proj(lo, hi):
        if hi <= O_GLOW:
            w = whead_ref[:, lo:hi]
        elif lo == O_GLOW and hi == O_UV:
            w = wglow_ref[...]
        else:
            w = wtail_ref[:, lo - O_UV:hi - O_UV]
        return _dot(xb, w) + bcat_ref[:, lo:hi]

    @pl.when(j == 0)
    def _():
        state_ref[...] = jnp.zeros_like(state_ref)
        e_ref[0:POOL_CARRY, :] = jnp.zeros((POOL_CARRY, POOL_WIDTH), F32)

    @pl.when(j > 0)
    def _():
        e_ref[0:POOL_CARRY, :] = e_ref[T:T + POOL_CARRY, :]

    qkvr_ref[...] = proj(O_QKVR, O_GLOW)
    glow = proj(O_GLOW, O_UV)
    z = _dot(glow.astype(BF16), wg2_ref[...]) + bg_ref[...]
    la = (jnp.minimum(z, 0.0) - jnp.log1p(jnp.exp(-jnp.abs(z)))) * (1.0 / GLA_TAU)
    la_hi, la_lo = _split_bf16(la)
    gate_a = _sigmoid(proj(O_GATE, O_GATE + D))

    C = GLA_CHUNK
    G = GLA_SUB
    NC = G // C
    CSH = C.bit_length() - 1
    row = lax.broadcasted_iota(jnp.int32, (G, G), 0)
    col = lax.broadcasted_iota(jnp.int32, (G, G), 1)
    same_chunk = (row >> CSH) == (col >> CSH)
    causal = same_chunk & (row >= col)
    causal_bf = jnp.where(causal, 1.0, 0.0).astype(BF16)
    lane = lax.broadcasted_iota(jnp.int32, (G, GLA_KEY), 1)
    gate_b = None
    for g0 in range(0, T, G):
        gr = slice(g0, g0 + G)
        lh, ll = la_hi[gr], la_lo[gr]
        b = _dot(causal_bf, lh) + _dot(causal_bf, ll)
        b_last = [b[(c + 1) * C - 1:(c + 1) * C, :] for c in range(NC)]
        b_end = jnp.concatenate([jnp.broadcast_to(r, (C, GLA_KEY)) for r in b_last], axis=0)
        q = qkvr_ref[gr, 0:GLA_KEY]
        k = qkvr_ref[gr, GLA_KEY:2 * GLA_KEY]
        v = qkvr_ref[gr, 2 * GLA_KEY:2 * GLA_KEY + GLA_VAL].astype(BF16)
        q_dec = q * (GLA_DK ** -0.5) * jnp.exp(b)
        k_dec = (k * jnp.exp(-b)).astype(BF16)
        k_tail = (k * jnp.exp(b_end - b)).astype(BF16)
        q_dec_bf = q_dec.astype(BF16)
        o_heads = []
        for h in range(GLA_HEADS):
            q_h = jnp.where((lane >= h * GLA_DK) & (lane < (h + 1) * GLA_DK), q_dec, 0.0).astype(BF16)
            scores = jnp.where(causal, _dot_t1(q_h, k_dec), 0.0).astype(BF16)
            o_heads.append(_dot(scores, v[:, h * GLA_DV:(h + 1) * GLA_DV]))
        o_intra = jnp.concatenate(o_heads, axis=1)
        if gate_b is None:
            gate_b = _sigmoid(proj(O_GATE + D, O_GATE + 2 * D))
        dec_cols = jnp.exp(jnp.concatenate(b_last + [jnp.zeros((8 - NC, GLA_KEY), F32)], axis=0)).T
        dec_all = jnp.concatenate([jnp.broadcast_to(dec_cols[:, c:c + 1], (GLA_KEY, LANES)) for c in range(NC)], axis=1)
        o_inter = []
        for c in range(NC):
            rows = slice(c * C, (c + 1) * C)
            o_inter.append(_dot(q_dec_bf[rows], state_ref[...].astype(BF16)))
            kv = _dot_t0(k_tail[rows], v[rows])
            for h in range(GLA_HEADS):
                rs = slice(h * GLA_DK, (h + 1) * GLA_DK)
                cs = slice(h * GLA_DV, (h + 1) * GLA_DV)
                state_ref[rs, cs] = dec_all[rs, c * LANES:(c + 1) * LANES] * state_ref[rs, cs] + kv[rs, cs]
        o = o_intra + jnp.concatenate(o_inter, axis=0)
        for h in range(GLA_HEADS):
            cs = slice(h * GLA_DV, (h + 1) * GLA_DV)
            o_h = o[:, cs]
            ms = jnp.mean(o_h * o_h, axis=-1, keepdims=True)
            o_h = o_h * lax.rsqrt(ms + LN_EPS) * gnorm_ref[:, cs]
            r_h = qkvr_ref[gr, 2 * GLA_KEY + GLA_VAL + h * GLA_DV:2 * GLA_KEY + GLA_VAL + (h + 1) * GLA_DV]
            ya_ref[gr, cs] = (o_h * (r_h * _sigmoid(r_h))).astype(BF16)

    gate_c = _sigmoid(proj(O_GATE + 2 * D, O_GATE + 3 * D))
    uv = proj(O_UV, O_XC)
    zg = 0.5 * uv * (1.0 + lax.erf(uv * (2.0 ** -0.5)))
    u = zg[:, :SGU_WIDTH]
    vln_ref[...] = _layer_norm(zg[:, SGU_WIDTH:], slng_ref[...], slnb_ref[...])
    lane_s = lax.broadcasted_iota(jnp.int32, (SGU_CHUNK, SGU_WIDTH), 1)
    s_parts = []
    for n in range(T // SGU_CHUNK):
        vc = vln_ref[n * SGU_CHUNK:(n + 1) * SGU_CHUNK, :]
        s = sbias_ref[...]
        for g in range(SGU_GROUPS):
            vg = jnp.where((lane_s >= g * SGU_GD) & (lane_s < (g + 1) * SGU_GD), vc, 0.0).astype(BF16)
            s = s + _dot(wtril_ref[g], vg)
        s_parts.append(s)
    y_b = (u * jnp.concatenate(s_parts, axis=0)).astype(BF16)

    P = POOL_CARRY
    xc = proj(O_XC, O_GATE)
    e_ref[P:P + T, :] = xc
    s2_ref[8:P + T, :] = e_ref[8:P + T, :] + e_ref[7:P + T - 1, :]
    s4_ref[16:P + T, :] = s2_ref[16:P + T, :] + s2_ref[14:P + T - 2, :]
    s8_ref[24:P + T, :] = s4_ref[24:P + T, :] + s4_ref[20:P + T - 4, :]
    s16 = s8_ref[P:P + T, :] + s8_ref[P - 8:P + T - 8, :]
    lane_p = lax.broadcasted_iota(jnp.int32, (T, POOL_WIDTH), 1)
    tpos = lax.broadcasted_iota(jnp.int32, (T, POOL_WIDTH), 0) + (j * T + 1)
    grp = lane_p >> (POOL_GD.bit_length() - 1)
    win = jnp.where(grp == 0, POOL_WINDOWS[0], jnp.where(grp == 1, POOL_WINDOWS[1],
                    jnp.where(grp == 2, POOL_WINDOWS[2], POOL_WINDOWS[3])))
    wsum = jnp.where(grp == 0, s2_ref[P:P + T, :], jnp.where(grp == 1, s4_ref[P:P + T, :],
                     jnp.where(grp == 2, s8_ref[P:P + T, :], s16)))
    count = jnp.minimum(tpos, win).astype(F32)
    pooled = wsum / count - xc
    y_c = (_dot(pooled.astype(BF16), poolw_ref[...]) * pscale_ref[...]).astype(BF16)

    merged = gate_a * _dot(ya_ref[...], wupa_ref[...].astype(BF16))
    merged += gate_b * _dot(y_b, wupb_ref[...].astype(BF16))
    merged += gate_c * _dot(y_c, wupc_ref[...].astype(BF16))
    h = _dot(merged.astype(BF16), wo_ref[...].astype(BF16))
    out_ref[0] = _layer_norm(DEEPNORM_ALPHA * x + h, ln1g_ref[...], ln1b_ref[...])


_MIXER_WEIGHTS = ('whead', 'wglow', 'wtail', 'bcat', 'wg2', 'bg', 'gnorm', 'slng', 'slnb', 'wtril', 'sbias', 'poolw', 'pscale',
                  'wupa', 'wupb', 'wupc', 'wo', 'ln1g', 'ln1b')


def _mixer(x, p, l, b0, nb):
    _, S, D = x.shape
    T = MIX_TILE
    weights = [p[n] for n in _MIXER_WEIGHTS]
    return pl.pallas_call(
        _mixer_kernel,
        out_shape=jax.ShapeDtypeStruct((nb, S, D), F32),
        grid=(nb, S // T),
        in_specs=[pl.BlockSpec((1, T, D), lambda b, j: (b0 + b, j, 0))] + [_layer_spec(w, l) for w in weights],
        out_specs=pl.BlockSpec((1, T, D), lambda b, j: (b, j, 0)),
        scratch_shapes=[
            pltpu.VMEM((GLA_KEY, GLA_VAL), F32),
            pltpu.VMEM((T, 2 * GLA_KEY + 2 * GLA_VAL), F32),
            pltpu.VMEM((T, GLA_VAL), BF16),
            pltpu.VMEM((T, SGU_WIDTH), F32),
            pltpu.VMEM((T + POOL_CARRY, POOL_WIDTH), F32),
            pltpu.VMEM((T + POOL_CARRY, POOL_WIDTH), F32),
            pltpu.VMEM((T + POOL_CARRY, POOL_WIDTH), F32),
            pltpu.VMEM((T + POOL_CARRY, POOL_WIDTH), F32),
        ],
        compiler_params=pltpu.CompilerParams(dimension_semantics=("arbitrary", "arbitrary"),
                                             vmem_limit_bytes=VMEM_LIMIT),
        name="mixer",
    )(x, *weights)


def _memkv_kernel(memt_ref, mem_ref, wk_ref, wv_ref, kt_ref, v_ref):
    kt_ref[0] = _dot_t0(wk_ref[...].astype(BF16), memt_ref[0].astype(BF16)).astype(BF16)
    v_ref[0] = _dot(mem_ref[0].astype(BF16), wv_ref[...].astype(BF16)).astype(BF16)


def _memkv(mem, memt, p, l, b0, nb):
    _, M, D = mem.shape
    return pl.pallas_call(
        _memkv_kernel,
        out_shape=(jax.ShapeDtypeStruct((nb, D, M), BF16), jax.ShapeDtypeStruct((nb, M, D), BF16)),
        grid=(nb,),
        in_specs=[pl.BlockSpec((1, D, M), lambda b: (b0 + b, 0, 0)), pl.BlockSpec((1, M, D), lambda b: (b0 + b, 0, 0)),
                  _layer_spec(p['wk'], l), _layer_spec(p['wv'], l)],
        out_specs=(pl.BlockSpec((1, D, M), lambda b: (b, 0, 0)), pl.BlockSpec((1, M, D), lambda b: (b, 0, 0))),
        compiler_params=pltpu.CompilerParams(dimension_semantics=("arbitrary",), vmem_limit_bytes=VMEM_LIMIT),
        name="memkv",
    )(memt, mem, p['wk'], p['wv'])


def _xattn_kernel(x_ref, kt_ref, v_ref, wq_ref, wo_ref, ln2g_ref, ln2b_ref, rwt_ref, rbt_ref,
                  x2_ref, x2p_ref, route_ref, counts_ref, carry_ref):
    T = XA_TILE

    @pl.when((pl.program_id(0) == 0) & (pl.program_id(1) == 0))
    def _():
        carry_ref[...] = jnp.zeros_like(carry_ref)

    x = x_ref[0]
    q = (_dot(x.astype(BF16), wq_ref[...].astype(BF16)) * (XA_DH ** -0.5)).astype(BF16)
    h = jnp.zeros_like(x)
    for hd in range(XA_HEADS):
        cs = slice(hd * XA_DH, (hd + 1) * XA_DH)
        s = _dot(q[:, cs], kt_ref[0, cs, :])
        e = jnp.exp(s - jnp.max(s, axis=-1, keepdims=True))
        o = _dot(e.astype(BF16), v_ref[0, :, cs]) / jnp.sum(e, axis=-1, keepdims=True)
        h = h + _dot(o.astype(BF16), wo_ref[cs, :].astype(BF16))
    x2 = _layer_norm(DEEPNORM_ALPHA * x + h, ln2g_ref[...], ln2b_ref[...])
    x2_ref[0] = x2
    x2p_ref[0] = _pack_bf16_pairs(x2)

    E = N_EXPERTS
    hi, lo = _split_bf16(x2)
    lt = _dot_t1(rwt_ref[...], hi)
    logits = lt[0:E] + (lt[E:2 * E] + _dot_t1(rwt_ref[0:E, :], lo)) + rbt_ref[...]

    eid = lax.broadcasted_iota(jnp.int32, (E, T), 0)
    neg_inf = jnp.float32(-jnp.inf)
    rest = logits
    tops, picks = [], []
    for _ in range(TOP_K):
        m = jnp.max(rest, axis=0, keepdims=True)
        idx = jnp.min(jnp.where(rest == m, eid, E), axis=0, keepdims=True)
        pick = eid == idx
        rest = jnp.where(pick, neg_inf, rest)
        tops.append((m, idx))
        picks.append(pick)
    exps = [jnp.exp(m - tops[0][0]) for m, _ in tops]
    denom = exps[0]
    for e in exps[1:]:
        denom = denom + e

    chosen = jnp.zeros((E, T), F32)
    for pick in picks:
        chosen = chosen + jnp.where(pick, 1.0, 0.0)
    chosen_bf = chosen.astype(BF16)
    earlier = (lax.broadcasted_iota(jnp.int32, (T, T), 0) < lax.broadcasted_iota(jnp.int32, (T, T), 1))
    carry = carry_ref[...]
    before = _dot(chosen_bf, jnp.where(earlier, 1.0, 0.0).astype(BF16)) + jnp.concatenate([carry] * (T // LANES), axis=1)
    carry = carry + _dot(chosen_bf, jnp.ones((T, LANES), BF16))
    carry_ref[...] = carry
    counts_ref[...] = carry

    rid = lax.broadcasted_iota(jnp.int32, (ROUTE_ROWS, T), 0)
    route = jnp.zeros((ROUTE_ROWS, T), F32)
    for k in range(TOP_K):
        rank = jnp.sum(jnp.where(picks[k], before, 0.0), axis=0, keepdims=True)
        route = jnp.where(rid == k, tops[k][1].astype(F32), route)
        route = jnp.where(rid == TOP_K + k, exps[k] / denom, route)
        route = jnp.where(rid == 2 * TOP_K + k, rank, route)
    route_ref[...] = route


def _xattn(x, kt, v, p, l):
    B, S, D = x.shape
    T = XA_TILE
    M = MEM_LEN
    weights = [p[n] for n in ('wq', 'wxo', 'ln2g', 'ln2b', 'rwt', 'rbt')]
    return pl.pallas_call(
        _xattn_kernel,
        out_shape=(jax.ShapeDtypeStruct((B, S, D), F32), jax.ShapeDtypeStruct((B, S, D // 2), jnp.uint32),
                   jax.ShapeDtypeStruct((ROUTE_ROWS, B * S), F32), jax.ShapeDtypeStruct((N_EXPERTS, LANES), F32)),
        grid=(B, S // T),
        in_specs=[pl.BlockSpec((1, T, D), lambda b, j: (b, j, 0)),
                  pl.BlockSpec((1, D, M), lambda b, j: (b, 0, 0)),
                  pl.BlockSpec((1, M, D), lambda b, j: (b, 0, 0))] + [_layer_spec(w, l) for w in weights],
        out_specs=(pl.BlockSpec((1, T, D), lambda b, j: (b, j, 0)),
                   pl.BlockSpec((1, T, D // 2), lambda b, j: (b, j, 0)),
                   pl.BlockSpec((ROUTE_ROWS, T), lambda b, j: (0, b * (S // T) + j)),
                   pl.BlockSpec((N_EXPERTS, LANES), lambda b, j: (0, 0))),
        scratch_shapes=[pltpu.VMEM((N_EXPERTS, LANES), F32)],
        compiler_params=pltpu.CompilerParams(dimension_semantics=("arbitrary", "arbitrary"),
                                             vmem_limit_bytes=VMEM_LIMIT),
        name="xattn",
    )(x, kt, v, *weights)


def _expert_kernel(layer, se_ref, sh_ref, nu_ref, nxt_ref, xs_ref, wgu_hbm, bgu_ref, wd_hbm, bd_ref, out_ref,
                   wgu_st, wd_st, slot_ref, sem):
    i = pl.program_id(0)
    F = EXPERT_FF
    halves = sh_ref[i]

    def weight_copies(e, slot):
        return (pltpu.make_async_copy(wgu_hbm.at[layer, e], wgu_st.at[slot], sem.at[slot, 0]),
                pltpu.make_async_copy(wd_hbm.at[layer, e], wd_st.at[slot], sem.at[slot, 1]))

    def ffn(rows):
        slot = slot_ref[0]
        xsb = _unpack_bf16_pairs(xs_ref[rows, :]).astype(BF16)
        hh = _dot(xsb, wgu_st[slot].astype(BF16)) + bgu_ref[...]
        h_glu = jnp.minimum(hh[:, :F], SWIGLU_LIMIT)
        h_lin = jnp.clip(hh[:, F:], -SWIGLU_LIMIT, SWIGLU_LIMIT)
        a = h_glu * _sigmoid(SWIGLU_ALPHA * h_glu) * (h_lin + 1.0)
        out_ref[rows, :] = _pack_bf16_pairs(_dot(a.astype(BF16), wd_st[slot].astype(BF16)) + bd_ref[...])

    @pl.when(halves > 0)
    def _():
        e = se_ref[i]
        prev = se_ref[jnp.maximum(i - 1, 0)]

        @pl.when(i == 0)
        def _():
            slot_ref[0] = 1
            for cp in weight_copies(e, 0):
                cp.start()

        @pl.when((i == 0) | (e != prev))
        def _():
            slot = 1 - slot_ref[0]
            slot_ref[0] = slot
            for cp in weight_copies(e, slot):
                cp.wait()
            nxt = nxt_ref[e]

            @pl.when(nxt != e)
            def _():
                for cp in weight_copies(nxt, 1 - slot):
                    cp.start()

    for n in range(1, MOE_STEP_BLOCKS + 1):
        @pl.when(halves == n)
        def _(n=n):
            ffn(slice(0, n * MOE_BLOCK))


def _experts(step_expert, step_halves, n_used, next_expert, xs, w_gu, b_gu, w_down, b_down, l):
    P, DH = xs.shape
    D = 2 * DH
    R = MOE_STEP_BLOCKS * MOE_BLOCK
    NS = P // R
    F2 = 2 * EXPERT_FF

    def row_map(i, se, sh, nu, nxt):
        return (jnp.minimum(i, nu[0] - 1), 0)

    def exp_map(i, se, sh, nu, nxt):
        return (l, se[jnp.minimum(i, nu[0] - 1)], 0, 0)

    grid_spec = pltpu.PrefetchScalarGridSpec(
        num_scalar_prefetch=4,
        grid=(NS,),
        in_specs=[pl.BlockSpec((R, DH), row_map),
                  pl.BlockSpec(memory_space=pl.ANY),
                  pl.BlockSpec((None, None, 1, F2), exp_map),
                  pl.BlockSpec(memory_space=pl.ANY),
                  pl.BlockSpec((None, None, 1, D), exp_map)],
        out_specs=pl.BlockSpec((R, DH), row_map),
        scratch_shapes=[pltpu.VMEM((2, D, F2), F32), pltpu.VMEM((2, EXPERT_FF, D), F32),
                        pltpu.SMEM((1,), jnp.int32),
                        pltpu.SemaphoreType.DMA((2, 2))],
    )
    return pl.pallas_call(
        functools.partial(_expert_kernel, l),
        out_shape=jax.ShapeDtypeStruct((P, DH), jnp.uint32),
        grid_spec=grid_spec,
        compiler_params=pltpu.CompilerParams(dimension_semantics=("arbitrary",), vmem_limit_bytes=VMEM_LIMIT),
        name="experts",
    )(step_expert, step_halves, n_used, next_expert, xs, w_gu, b_gu.reshape(DEPTH, N_EXPERTS, 1, F2), w_down,
      b_down.reshape(DEPTH, N_EXPERTS, 1, D))


def _sc_gather_rows(x, idx):
    M = idx.shape[0]
    D = x.shape[1]
    W = SC_GATHER_WINDOW
    mesh = plsc.VectorSubcoreMesh(core_axis_name="core", subcore_axis_name="subcore")
    n_workers = mesh.num_cores * mesh.num_subcores
    rows_per = M // n_workers
    assert rows_per * n_workers == M and rows_per % W == 0

    @pl.kernel(out_type=jax.ShapeDtypeStruct((M, D), x.dtype), mesh=mesh, name="sc_gather_rows",
               scratch_types=[pltpu.VMEM((rows_per,), jnp.int32), pltpu.VMEM((W, D), x.dtype)])
    def gather_kernel(x_hbm, i_hbm, o_hbm, idx_vmem, buf):
        wid = lax.axis_index("core") * mesh.num_subcores + lax.axis_index("subcore")
        base = wid * rows_per
        pltpu.sync_copy(i_hbm.at[pl.ds(base, rows_per)], idx_vmem)

        @pl.loop(0, rows_per // W)
        def _(j):
            pltpu.sync_copy(x_hbm.at[idx_vmem.at[pl.ds(j * W, W)]], buf)
            pltpu.sync_copy(buf, o_hbm.at[pl.ds(base + j * W, W)])

    return gather_kernel(x, idx)


def _sc_scatter_rows(x, idx, n_out):
    K, N = idx.shape
    D = x.shape[1]
    W = SC_GATHER_WINDOW
    mesh = plsc.VectorSubcoreMesh(core_axis_name="core", subcore_axis_name="subcore")
    n_workers = mesh.num_cores * mesh.num_subcores
    rows_per = N // n_workers
    assert rows_per * n_workers == N and rows_per % W == 0

    @pl.kernel(out_type=jax.ShapeDtypeStruct((n_out, D), x.dtype), mesh=mesh, name="sc_scatter_rows",
               scratch_types=[pltpu.VMEM((K * rows_per,), jnp.int32), pltpu.VMEM((W, D), x.dtype),
                              pltpu.SemaphoreType.DMA((K,))])
    def scatter_kernel(x_hbm, i_hbm, o_hbm, idx_vmem, buf, sem):
        wid = lax.axis_index("core") * mesh.num_subcores + lax.axis_index("subcore")
        base = wid * rows_per
        for k in range(K):
            pltpu.sync_copy(i_hbm.at[pl.ds(k * N + base, rows_per)], idx_vmem.at[pl.ds(k * rows_per, rows_per)])

        @pl.loop(0, rows_per // W)
        def _(j):
            pltpu.sync_copy(x_hbm.at[pl.ds(base + j * W, W)], buf)
            copies = [pltpu.make_async_copy(buf, o_hbm.at[idx_vmem.at[pl.ds(k * rows_per + j * W, W)]], sem.at[k])
                      for k in range(K)]
            for cp in copies:
                cp.start()
            for cp in copies:
                cp.wait()

    return scatter_kernel(x, idx.reshape(K * N))


def _combine_kernel(x_ref, yg_ref, route_ref, ln3g_ref, ln3b_ref, *rest):
    out_ref = rest[-1]
    x = x_ref[...]
    g = jnp.transpose(route_ref[...])
    y = jnp.zeros_like(x)
    for k in range(TOP_K):
        y = y + g[:, TOP_K + k:TOP_K + k + 1] * _unpack_bf16_pairs(yg_ref[k * CMB_TILE:(k + 1) * CMB_TILE, :])
    out_ref[...] = _layer_norm(DEEPNORM_ALPHA * x + y, ln3g_ref[...], ln3b_ref[...])


def _combine(x2, yg, route, p, l, first, acc, out_first, n_out):
    _, D = x2.shape
    T = CMB_TILE
    tiles = yg.shape[0] // (TOP_K * T)
    in_specs = [pl.BlockSpec((T, D), lambda i: (first + i, 0)), pl.BlockSpec((TOP_K * T, D // 2), lambda i: (i, 0)),
                pl.BlockSpec((ROUTE_ROWS, T), lambda i: (0, first + i)), _layer_spec(p['ln3g'], l), _layer_spec(p['ln3b'], l)]
    args = [x2, yg, route, p['ln3g'], p['ln3b']]
    aliases = {}
    if acc is not None:
        in_specs.append(pl.BlockSpec(memory_space=pl.ANY))
        args.append(acc)
        aliases = {len(args) - 1: 0}
    return pl.pallas_call(
        _combine_kernel,
        out_shape=jax.ShapeDtypeStruct((n_out, D), F32),
        grid=(tiles,),
        in_specs=in_specs,
        out_specs=pl.BlockSpec((T, D), lambda i: (out_first + first + i, 0)),
        input_output_aliases=aliases,
        compiler_params=pltpu.CompilerParams(dimension_semantics=("arbitrary",), vmem_limit_bytes=VMEM_LIMIT),
        name="combine",
    )(*args)


def _prep(w_in, b_in, gla_wg2, gla_bg, gla_norm_g, sgu_ln_g, sgu_ln_b, sgu_ws, sgu_bs, pool_w, pool_scale,
          w_up_a, w_up_b, w_up_c, w_o, ln1_g, ln1_b, xa_wq, xa_wk, xa_wv, xa_wo, ln2_g, ln2_b,
          router_w, router_b, ln3_g, ln3_b):
    L = w_in.shape[0]
    o_glow = O_GLOW
    o_uv = o_glow + GLA_RANK
    row = lambda a: a.reshape(L, 1, -1).astype(F32)
    pad_last = lambda a, n: jnp.pad(a, [(0, 0)] * (a.ndim - 1) + [(0, n - a.shape[-1])])
    p = {}
    p['whead'] = w_in[..., :o_glow].astype(BF16)
    p['wglow'] = pad_last(w_in[..., o_glow:o_uv], LANES).astype(BF16)
    p['wtail'] = w_in[..., o_uv:].astype(BF16)
    p['bcat'] = row(jnp.concatenate([b_in[..., :o_glow], pad_last(b_in[..., o_glow:o_uv], LANES), b_in[..., o_uv:]],
                                    axis=-1))
    p['wg2'] = jnp.pad(gla_wg2, ((0, 0), (0, LANES - GLA_RANK), (0, 0))).astype(BF16)
    p['bg'] = row(gla_bg)
    p['gnorm'] = row(gla_norm_g)
    p['slng'] = row(sgu_ln_g)
    p['slnb'] = row(sgu_ln_b)
    p['wtril'] = jnp.tril(sgu_ws).astype(BF16)
    p['sbias'] = jnp.repeat(jnp.swapaxes(sgu_bs, 1, 2), SGU_GD, axis=2).astype(F32)
    G = len(POOL_WINDOWS)
    eye = jnp.eye(G, dtype=F32)
    p['poolw'] = jnp.einsum('lgcd,gh->lgchd', pool_w, eye).reshape(L, POOL_WIDTH, POOL_WIDTH).astype(BF16)
    p['pscale'] = row(pool_scale)
    p['wupa'], p['wupb'], p['wupc'], p['wo'] = w_up_a, w_up_b, w_up_c, w_o
    p['ln1g'], p['ln1b'] = row(ln1_g), row(ln1_b)
    p['wq'], p['wk'], p['wv'], p['wxo'] = xa_wq, xa_wk, xa_wv, xa_wo
    p['ln2g'], p['ln2b'] = row(ln2_g), row(ln2_b)
    rwt = jnp.swapaxes(router_w, 1, 2)
    rwt_hi = rwt.astype(BF16)
    p['rwt'] = jnp.concatenate([rwt_hi, (rwt - rwt_hi.astype(F32)).astype(BF16)], axis=1)
    p['rbt'] = jnp.broadcast_to(router_b[:, :, None], router_b.shape + (XA_TILE,)).astype(F32)
    p['ln3g'], p['ln3b'] = row(ln3_g), row(ln3_b)
    return p


def _route(route, counts):
    N = route.shape[1]
    top_idx = route[0:TOP_K].astype(jnp.int32)
    rank = route[2 * TOP_K:3 * TOP_K].astype(jnp.int32)
    counts = counts[:, 0].astype(jnp.int32)
    R = MOE_STEP_BLOCKS * MOE_BLOCK
    blocks = (counts + MOE_BLOCK - 1) // MOE_BLOCK
    padded = ((counts + R - 1) // R) * R
    pad_end = jnp.cumsum(padded)
    pad_start = pad_end - padded
    ids = jnp.arange(N_EXPERTS, dtype=jnp.int32)
    start_of = jnp.sum(jnp.where(top_idx[:, :, None] == ids[None, None, :], pad_start[None, None, :], 0), axis=-1)
    dest = start_of + rank
    n_steps = N * TOP_K // R + N_EXPERTS
    step_start = jnp.arange(n_steps, dtype=jnp.int32) * R
    step_expert = jnp.minimum(jnp.sum((pad_end[None, :] <= step_start[:, None]).astype(jnp.int32), axis=1),
                              N_EXPERTS - 1)
    mine = step_expert[:, None] == ids[None, :]
    blocks_before = (step_start - jnp.sum(jnp.where(mine, pad_start[None, :], 0), axis=1)) // MOE_BLOCK
    step_halves = jnp.clip(jnp.sum(jnp.where(mine, blocks[None, :], 0), axis=1) - blocks_before, 0, MOE_STEP_BLOCKS)
    step_halves = jnp.where(step_start < pad_end[-1], step_halves, 0).astype(jnp.int32)
    n_used = (pad_end[-1] // R).astype(jnp.int32).reshape(1)
    later = jnp.where((ids[None, :] > ids[:, None]) & (counts[None, :] > 0), ids[None, :], N_EXPERTS)
    first_later = jnp.min(later, axis=1)
    next_expert = jnp.where(first_later < N_EXPERTS, first_later, ids).astype(jnp.int32)
    return dest, step_expert, step_halves, n_used, next_expert


def kernel(x, mem, w_in, b_in, gla_wg2, gla_bg, gla_norm_g, sgu_ln_g, sgu_ln_b, sgu_ws, sgu_bs, pool_w, pool_scale, w_up_a, w_up_b, w_up_c, w_o, ln1_g, ln1_b, xa_wq, xa_wk, xa_wv, xa_wo, ln2_g, ln2_b, router_w, router_b, exp_w_gu, exp_b_gu, exp_w_down, exp_b_down, ln3_g, ln3_b):
    B, S, D = x.shape
    N = B * S
    p = _prep(w_in, b_in, gla_wg2, gla_bg, gla_norm_g, sgu_ln_g, sgu_ln_b, sgu_ws, sgu_bs, pool_w, pool_scale,
              w_up_a, w_up_b, w_up_c, w_o, ln1_g, ln1_b, xa_wq, xa_wk, xa_wv, xa_wo, ln2_g, ln2_b,
              router_w, router_b, ln3_g, ln3_b)
    memt = jnp.swapaxes(mem, 1, 2)
    nb = B // TOKEN_GROUPS
    Ng = nb * S
    tile_rows = TOP_K * CMB_TILE
    xg = [None] * TOKEN_GROUPS
    out = None
    for l in range(DEPTH):
        last = l == DEPTH - 1
        for g in range(TOKEN_GROUPS):
            x1 = _mixer(x, p, l, g * nb, nb) if l == 0 else _mixer(xg[g], p, l, 0, nb)
            kt, v = _memkv(mem, memt, p, l, g * nb, nb)
            x2, x2p, route, counts = _xattn(x1, kt, v, p, l)
            dest, step_expert, step_halves, n_used, next_expert = _route(route, counts)
            n_slots = Ng * TOP_K + N_EXPERTS * MOE_STEP_BLOCKS * MOE_BLOCK
            xs = _sc_scatter_rows(x2p.reshape(Ng, D // 2), dest, n_slots)
            ys = _experts(step_expert, step_halves, n_used, next_expert, xs, exp_w_gu, exp_b_gu, exp_w_down,
                          exp_b_down, l)
            dest_km = dest.reshape(TOP_K, Ng // CMB_TILE, CMB_TILE).transpose(1, 0, 2).reshape(-1)
            acc = out if last else None
            out_first = g * (Ng // CMB_TILE) if last else 0
            first = 0
            for share in COMBINE_SHARES:
                tiles = (Ng // CMB_TILE) * share // sum(COMBINE_SHARES)
                yg = _sc_gather_rows(ys, dest_km[first * tile_rows:(first + tiles) * tile_rows])
                acc = _combine(x2.reshape(Ng, D), yg, route, p, l, first, acc, out_first, N if last else Ng)
                first += tiles
            if last:
                out = acc
            else:
                xg[g] = acc.reshape(nb, S, D)
    return out.reshape(B, S, D)
```

```python
import functools

import jax
import jax.numpy as jnp
from jax import lax
from jax.experimental import pallas as pl
from jax.experimental.pallas import tpu as pltpu
from jax.experimental.pallas import tpu_sc as plsc

F32 = jnp.float32
BF16 = jnp.bfloat16

D_MODEL = 1024
DEPTH = 2
GLA_HEADS = 4
GLA_KEY = 256
GLA_VAL = 512
GLA_DK = 64
GLA_DV = 128
GLA_RANK = 16
GLA_TAU = 16.0
GLA_CHUNK = 64
SGU_GROUPS = 4
SGU_WIDTH = 256
SGU_GD = 64
SGU_CHUNK = 128
POOL_WINDOWS = (2, 4, 8, 16)
POOL_WIDTH = 256
POOL_GD = 64
POOL_CARRY = 32
MEM_LEN = 256
XA_HEADS = 4
XA_DH = 256
N_EXPERTS = 32
TOP_K = 4
EXPERT_FF = 1024
SWIGLU_LIMIT = 7.0
SWIGLU_ALPHA = 1.702
DEEPNORM_ALPHA = (2 * DEPTH) ** 0.25
LN_EPS = 1e-5
LANES = 128
VMEM_LIMIT = 56 * 1024 * 1024

MIX_TILE = 512
GLA_SUB = 256
XA_TILE = 1024
MOE_BLOCK = 128
MOE_STEP_BLOCKS = 4
CMB_TILE = 512
TOKEN_GROUPS = 2
COMBINE_SHARES = (1, 1, 2, 2, 2)
ROUTE_ROWS = 16
SC_GATHER_WINDOW = 128

O_QKVR = 0
O_GLOW = 2 * GLA_KEY + 2 * GLA_VAL
O_UV = O_GLOW + LANES
O_XC = O_UV + 2 * SGU_WIDTH
O_GATE = O_XC + POOL_WIDTH


def _dot(a, b):
    return jnp.dot(a, b, preferred_element_type=F32)


def _dot_t0(a, b):
    return lax.dot_general(a, b, (((0,), (0,)), ((), ())), preferred_element_type=F32)


def _dot_t1(a, b):
    return lax.dot_general(a, b, (((1,), (1,)), ((), ())), preferred_element_type=F32)


def _split_bf16(x):
    hi = x.astype(BF16)
    lo = (x - hi.astype(F32)).astype(BF16)
    return hi, lo


def _layer_norm(x, g, b):
    mu = jnp.mean(x, axis=-1, keepdims=True)
    xc = x - mu
    var = jnp.mean(xc * xc, axis=-1, keepdims=True)
    return xc * lax.rsqrt(var + LN_EPS) * g + b


def _sigmoid(x):
    return 1.0 / (1.0 + jnp.exp(-x))


def _pack_bf16_pairs(x):
    H = x.shape[1] // 2
    bits = lax.bitcast_convert_type(x.astype(BF16).astype(F32), jnp.uint32)
    return (bits[:, :H] >> 16) | (bits[:, H:] & jnp.uint32(0xFFFF0000))


def _unpack_bf16_pairs(w):
    lo = lax.bitcast_convert_type(w << 16, F32)
    hi = lax.bitcast_convert_type(w & jnp.uint32(0xFFFF0000), F32)
    return jnp.concatenate([lo, hi], axis=1)


def _layer_spec(arr, l):
    nd = arr.ndim - 1
    return pl.BlockSpec((None,) + arr.shape[1:], lambda *_: (l,) + (0,) * nd, pipeline_mode=pl.Buffered(1))


def _mixer_kernel(x_ref, whead_ref, wglow_ref, wtail_ref, bcat_ref, wg2_ref, bg_ref, gnorm_ref,
                  slng_ref, slnb_ref, wtril_ref, sbias_ref, poolw_ref, pscale_ref,
                  wupa_ref, wupb_ref, wupc_ref, wo_ref, ln1g_ref, ln1b_ref,
                  out_ref,
                  state_ref, qkvr_ref, ya_ref, vln_ref, e_ref, s2_ref, s4_ref, s8_ref):
    T = MIX_TILE
    D = D_MODEL
    j = pl.program_id(1)
    x = x_ref[0]
    xb = x.astype(BF16)

    def proj(lo, hi):
        if hi <= O_GLOW:
            w = whead_ref[:, lo:hi]
        elif lo == O_GLOW and hi == O_UV:
            w = wglow_ref[...]
        else:
            w = wtail_ref[:, lo - O_UV:hi - O_UV]
        return _dot(xb, w) + bcat_ref[:, lo:hi]

    @pl.when(j == 0)
    def _():
        state_ref[...] = jnp.zeros_like(state_ref)
        e_ref[0:POOL_CARRY, :] = jnp.zeros((POOL_CARRY, POOL_WIDTH), F32)

    @pl.when(j > 0)
    def _():
        e_ref[0:POOL_CARRY, :] = e_ref[T:T + POOL_CARRY, :]

    qkvr_ref[...] = proj(O_QKVR, O_GLOW)
    glow = proj(O_GLOW, O_UV)
    z = _dot(glow.astype(BF16), wg2_ref[...]) + bg_ref[...]
    la = (jnp.minimum(z, 0.0) - jnp.log1p(jnp.exp(-jnp.abs(z)))) * (1.0 / GLA_TAU)
    la_hi, la_lo = _split_bf16(la)
    gate_a = _sigmoid(proj(O_GATE, O_GATE + D))

    C = GLA_CHUNK
    G = GLA_SUB
    NC = G // C
    CSH = C.bit_length() - 1
    row = lax.broadcasted_iota(jnp.int32, (G, G), 0)
    col = lax.broadcasted_iota(jnp.int32, (G, G), 1)
    same_chunk = (row >> CSH) == (col >> CSH)
    causal = same_chunk & (row >= col)
    causal_bf = jnp.where(causal, 1.0, 0.0).astype(BF16)
    lane = lax.broadcasted_iota(jnp.int32, (G, GLA_KEY), 1)
    gate_b = None
    for g0 in range(0, T, G):
        gr = slice(g0, g0 + G)
        lh, ll = la_hi[gr], la_lo[gr]
        b = _dot(causal_bf, lh) + _dot(causal_bf, ll)
        b_last = [b[(c + 1) * C - 1:(c + 1) * C, :] for c in range(NC)]
        b_end = jnp.concatenate([jnp.broadcast_to(r, (C, GLA_KEY)) for r in b_last], axis=0)
        q = qkvr_ref[gr, 0:GLA_KEY]
        k = qkvr_ref[gr, GLA_KEY:2 * GLA_KEY]
        v = qkvr_ref[gr, 2 * GLA_KEY:2 * GLA_KEY + GLA_VAL].astype(BF16)
        q_dec = q * (GLA_DK ** -0.5) * jnp.exp(b)
        k_dec = (k * jnp.exp(-b)).astype(BF16)
        k_tail = (k * jnp.exp(b_end - b)).astype(BF16)
        q_dec_bf = q_dec.astype(BF16)
        o_heads = []
        for h in range(GLA_HEADS):
            q_h = jnp.where((lane >= h * GLA_DK) & (lane < (h + 1) * GLA_DK), q_dec, 0.0).astype(BF16)
            scores = jnp.where(causal, _dot_t1(q_h, k_dec), 0.0).astype(BF16)
            o_heads.append(_dot(scores, v[:, h * GLA_DV:(h + 1) * GLA_DV]))
        o_intra = jnp.concatenate(o_heads, axis=1)
        if gate_b is None:
            gate_b = _sigmoid(proj(O_GATE + D, O_GATE + 2 * D))
        dec_cols = jnp.exp(jnp.concatenate(b_last + [jnp.zeros((8 - NC, GLA_KEY), F32)], axis=0)).T
        dec_all = jnp.concatenate([jnp.broadcast_to(dec_cols[:, c:c + 1], (GLA_KEY, LANES)) for c in range(NC)], axis=1)
        o_inter = []
        for c in range(NC):
            rows = slice(c * C, (c + 1) * C)
            o_inter.append(_dot(q_dec_bf[rows], state_ref[...].astype(BF16)))
            kv = _dot_t0(k_tail[rows], v[rows])
            for h in range(GLA_HEADS):
                rs = slice(h * GLA_DK, (h + 1) * GLA_DK)
                cs = slice(h * GLA_DV, (h + 1) * GLA_DV)
                state_ref[rs, cs] = dec_all[rs, c * LANES:(c + 1) * LANES] * state_ref[rs, cs] + kv[rs, cs]
        o = o_intra + jnp.concatenate(o_inter, axis=0)
        for h in range(GLA_HEADS):
            cs = slice(h * GLA_DV, (h + 1) * GLA_DV)
            o_h = o[:, cs]
            ms = jnp.mean(o_h * o_h, axis=-1, keepdims=True)
            o_h = o_h * lax.rsqrt(ms + LN_EPS) * gnorm_ref[:, cs]
            r_h = qkvr_ref[gr, 2 * GLA_KEY + GLA_VAL + h * GLA_DV:2 * GLA_KEY + GLA_VAL + (h + 1) * GLA_DV]
            ya_ref[gr, cs] = (o_h * (r_h * _sigmoid(r_h))).astype(BF16)

    gate_c = _sigmoid(proj(O_GATE + 2 * D, O_GATE + 3 * D))
    uv = proj(O_UV, O_XC)
    zg = 0.5 * uv * (1.0 + lax.erf(uv * (2.0 ** -0.5)))
    u = zg[:, :SGU_WIDTH]
    vln_ref[...] = _layer_norm(zg[:, SGU_WIDTH:], slng_ref[...], slnb_ref[...])
    lane_s = lax.broadcasted_iota(jnp.int32, (SGU_CHUNK, SGU_WIDTH), 1)
    s_parts = []
    for n in range(T // SGU_CHUNK):
        vc = vln_ref[n * SGU_CHUNK:(n + 1) * SGU_CHUNK, :]
        s = sbias_ref[...]
        for g in range(SGU_GROUPS):
            vg = jnp.where((lane_s >= g * SGU_GD) & (lane_s < (g + 1) * SGU_GD), vc, 0.0).astype(BF16)
            s = s + _dot(wtril_ref[g], vg)
        s_parts.append(s)
    y_b = (u * jnp.concatenate(s_parts, axis=0)).astype(BF16)

    P = POOL_CARRY
    xc = proj(O_XC, O_GATE)
    e_ref[P:P + T, :] = xc
    s2_ref[8:P + T, :] = e_ref[8:P + T, :] + e_ref[7:P + T - 1, :]
    s4_ref[16:P + T, :] = s2_ref[16:P + T, :] + s2_ref[14:P + T - 2, :]
    s8_ref[24:P + T, :] = s4_ref[24:P + T, :] + s4_ref[20:P + T - 4, :]
    s16 = s8_ref[P:P + T, :] + s8_ref[P - 8:P + T - 8, :]
    lane_p = lax.broadcasted_iota(jnp.int32, (T, POOL_WIDTH), 1)
    tpos = lax.broadcasted_iota(jnp.int32, (T, POOL_WIDTH), 0) + (j * T + 1)
    grp = lane_p >> (POOL_GD.bit_length() - 1)
    win = jnp.where(grp == 0, POOL_WINDOWS[0], jnp.where(grp == 1, POOL_WINDOWS[1],
                    jnp.where(grp == 2, POOL_WINDOWS[2], POOL_WINDOWS[3])))
    wsum = jnp.where(grp == 0, s2_ref[P:P + T, :], jnp.where(grp == 1, s4_ref[P:P + T, :],
                     jnp.where(grp == 2, s8_ref[P:P + T, :], s16)))
    count = jnp.minimum(tpos, win).astype(F32)
    pooled = wsum / count - xc
    y_c = (_dot(pooled.astype(BF16), poolw_ref[...]) * pscale_ref[...]).astype(BF16)

    merged = gate_a * _dot(ya_ref[...], wupa_ref[...].astype(BF16))
    merged += gate_b * _dot(y_b, wupb_ref[...].astype(BF16))
    merged += gate_c * _dot(y_c, wupc_ref[...].astype(BF16))
    h = _dot(merged.astype(BF16), wo_ref[...].astype(BF16))
    out_ref[0] = _layer_norm(DEEPNORM_ALPHA * x + h, ln1g_ref[...], ln1b_ref[...])


_MIXER_WEIGHTS = ('whead', 'wglow', 'wtail', 'bcat', 'wg2', 'bg', 'gnorm', 'slng', 'slnb', 'wtril', 'sbias', 'poolw', 'pscale',
                  'wupa', 'wupb', 'wupc', 'wo', 'ln1g', 'ln1b')


def _mixer(x, p, l, b0, nb):
    _, S, D = x.shape
    T = MIX_TILE
    weights = [p[n] for n in _MIXER_WEIGHTS]
    return pl.pallas_call(
        _mixer_kernel,
        out_shape=jax.ShapeDtypeStruct((nb, S, D), F32),
        grid=(nb, S // T),
        in_specs=[pl.BlockSpec((1, T, D), lambda b, j: (b0 + b, j, 0))] + [_layer_spec(w, l) for w in weights],
        out_specs=pl.BlockSpec((1, T, D), lambda b, j: (b, j, 0)),
        scratch_shapes=[
            pltpu.VMEM((GLA_KEY, GLA_VAL), F32),
            pltpu.VMEM((T, 2 * GLA_KEY + 2 * GLA_VAL), F32),
            pltpu.VMEM((T, GLA_VAL), BF16),
            pltpu.VMEM((T, SGU_WIDTH), F32),
            pltpu.VMEM((T + POOL_CARRY, POOL_WIDTH), F32),
            pltpu.VMEM((T + POOL_CARRY, POOL_WIDTH), F32),
            pltpu.VMEM((T + POOL_CARRY, POOL_WIDTH), F32),
            pltpu.VMEM((T + POOL_CARRY, POOL_WIDTH), F32),
        ],
        compiler_params=pltpu.CompilerParams(dimension_semantics=("arbitrary", "arbitrary"),
                                             vmem_limit_bytes=VMEM_LIMIT),
        name="mixer",
    )(x, *weights)


def _memkv_kernel(memt_ref, mem_ref, wk_ref, wv_ref, kt_ref, v_ref):
    kt_ref[0] = _dot_t0(wk_ref[...].astype(BF16), memt_ref[0].astype(BF16)).astype(BF16)
    v_ref[0] = _dot(mem_ref[0].astype(BF16), wv_ref[...].astype(BF16)).astype(BF16)


def _memkv(mem, memt, p, l, b0, nb):
    _, M, D = mem.shape
    return pl.pallas_call(
        _memkv_kernel,
        out_shape=(jax.ShapeDtypeStruct((nb, D, M), BF16), jax.ShapeDtypeStruct((nb, M, D), BF16)),
        grid=(nb,),
        in_specs=[pl.BlockSpec((1, D, M), lambda b: (b0 + b, 0, 0)), pl.BlockSpec((1, M, D), lambda b: (b0 + b, 0, 0)),
                  _layer_spec(p['wk'], l), _layer_spec(p['wv'], l)],
        out_specs=(pl.BlockSpec((1, D, M), lambda b: (b, 0, 0)), pl.BlockSpec((1, M, D), lambda b: (b, 0, 0))),
        compiler_params=pltpu.CompilerParams(dimension_semantics=("arbitrary",), vmem_limit_bytes=VMEM_LIMIT),
        name="memkv",
    )(memt, mem, p['wk'], p['wv'])


def _xattn_kernel(x_ref, kt_ref, v_ref, wq_ref, wo_ref, ln2g_ref, ln2b_ref, rwt_ref, rbt_ref,
                  x2_ref, x2p_ref, route_ref, counts_ref, carry_ref):
    T = XA_TILE

    @pl.when((pl.program_id(0) == 0) & (pl.program_id(1) == 0))
    def _():
        carry_ref[...] = jnp.zeros_like(carry_ref)

    x = x_ref[0]
    q = (_dot(x.astype(BF16), wq_ref[...].astype(BF16)) * (XA_DH ** -0.5)).astype(BF16)
    h = jnp.zeros_like(x)
    for hd in range(XA_HEADS):
        cs = slice(hd * XA_DH, (hd + 1) * XA_DH)
        s = _dot(q[:, cs], kt_ref[0, cs, :])
        e = jnp.exp(s - jnp.max(s, axis=-1, keepdims=True))
        o = _dot(e.astype(BF16), v_ref[0, :, cs]) / jnp.sum(e, axis=-1, keepdims=True)
        h = h + _dot(o.astype(BF16), wo_ref[cs, :].astype(BF16))
    x2 = _layer_norm(DEEPNORM_ALPHA * x + h, ln2g_ref[...], ln2b_ref[...])
    x2_ref[0] = x2
    x2p_ref[0] = _pack_bf16_pairs(x2)

    E = N_EXPERTS
    hi, lo = _split_bf16(x2)
    lt = _dot_t1(rwt_ref[...], hi)
    logits = lt[0:E] + (lt[E:2 * E] + _dot_t1(rwt_ref[0:E, :], lo)) + rbt_ref[...]

    eid = lax.broadcasted_iota(jnp.int32, (E, T), 0)
    neg_inf = jnp.float32(-jnp.inf)
    rest = logits
    tops, picks = [], []
    for _ in range(TOP_K):
        m = jnp.max(rest, axis=0, keepdims=True)
        idx = jnp.min(jnp.where(rest == m, eid, E), axis=0, keepdims=True)
        pick = eid == idx
        rest = jnp.where(pick, neg_inf, rest)
        tops.append((m, idx))
        picks.append(pick)
    exps = [jnp.exp(m - tops[0][0]) for m, _ in tops]
    denom = exps[0]
    for e in exps[1:]:
        denom = denom + e

    chosen = jnp.zeros((E, T), F32)
    for pick in picks:
        chosen = chosen + jnp.where(pick, 1.0, 0.0)
    chosen_bf = chosen.astype(BF16)
    earlier = (lax.broadcasted_iota(jnp.int32, (T, T), 0) < lax.broadcasted_iota(jnp.int32, (T, T), 1))
    carry = carry_ref[...]
    before = _dot(chosen_bf, jnp.where(earlier, 1.0, 0.0).astype(BF16)) + jnp.concatenate([carry] * (T // LANES), axis=1)
    carry = carry + _dot(chosen_bf, jnp.ones((T, LANES), BF16))
    carry_ref[...] = carry
    counts_ref[...] = carry

    rid = lax.broadcasted_iota(jnp.int32, (ROUTE_ROWS, T), 0)
    route = jnp.zeros((ROUTE_ROWS, T), F32)
    for k in range(TOP_K):
        rank = jnp.sum(jnp.where(picks[k], before, 0.0), axis=0, keepdims=True)
        route = jnp.where(rid == k, tops[k][1].astype(F32), route)
        route = jnp.where(rid == TOP_K + k, exps[k] / denom, route)
        route = jnp.where(rid == 2 * TOP_K + k, rank, route)
    route_ref[...] = route


def _xattn(x, kt, v, p, l):
    B, S, D = x.shape
    T = XA_TILE
    M = MEM_LEN
    weights = [p[n] for n in ('wq', 'wxo', 'ln2g', 'ln2b', 'rwt', 'rbt')]
    return pl.pallas_call(
        _xattn_kernel,
        out_shape=(jax.ShapeDtypeStruct((B, S, D), F32), jax.ShapeDtypeStruct((B, S, D // 2), jnp.uint32),
                   jax.ShapeDtypeStruct((ROUTE_ROWS, B * S), F32), jax.ShapeDtypeStruct((N_EXPERTS, LANES), F32)),
        grid=(B, S // T),
        in_specs=[pl.BlockSpec((1, T, D), lambda b, j: (b, j, 0)),
                  pl.BlockSpec((1, D, M), lambda b, j: (b, 0, 0)),
                  pl.BlockSpec((1, M, D), lambda b, j: (b, 0, 0))] + [_layer_spec(w, l) for w in weights],
        out_specs=(pl.BlockSpec((1, T, D), lambda b, j: (b, j, 0)),
                   pl.BlockSpec((1, T, D // 2), lambda b, j: (b, j, 0)),
                   pl.BlockSpec((ROUTE_ROWS, T), lambda b, j: (0, b * (S // T) + j)),
                   pl.BlockSpec((N_EXPERTS, LANES), lambda b, j: (0, 0))),
        scratch_shapes=[pltpu.VMEM((N_EXPERTS, LANES), F32)],
        compiler_params=pltpu.CompilerParams(dimension_semantics=("arbitrary", "arbitrary"),
                                             vmem_limit_bytes=VMEM_LIMIT),
        name="xattn",
    )(x, kt, v, *weights)


def _expert_kernel(layer, se_ref, sh_ref, nu_ref, nxt_ref, xs_ref, wgu_hbm, bgu_ref, wd_hbm, bd_ref, out_ref,
                   wgu_st, wd_st, slot_ref, sem):
    i = pl.program_id(0)
    F = EXPERT_FF
    halves = sh_ref[i]

    def weight_copies(e, slot):
        return (pltpu.make_async_copy(wgu_hbm.at[layer, e], wgu_st.at[slot], sem.at[slot, 0]),
                pltpu.make_async_copy(wd_hbm.at[layer, e], wd_st.at[slot], sem.at[slot, 1]))

    def ffn(rows):
        slot = slot_ref[0]
        xsb = _unpack_bf16_pairs(xs_ref[rows, :]).astype(BF16)
        hh = _dot(xsb, wgu_st[slot].astype(BF16)) + bgu_ref[...]
        h_glu = jnp.minimum(hh[:, :F], SWIGLU_LIMIT)
        h_lin = jnp.clip(hh[:, F:], -SWIGLU_LIMIT, SWIGLU_LIMIT)
        a = h_glu * _sigmoid(SWIGLU_ALPHA * h_glu) * (h_lin + 1.0)
        out_ref[rows, :] = _pack_bf16_pairs(_dot(a.astype(BF16), wd_st[slot].astype(BF16)) + bd_ref[...])

    @pl.when(halves > 0)
    def _():
        e = se_ref[i]
        prev = se_ref[jnp.maximum(i - 1, 0)]

        @pl.when(i == 0)
        def _():
            slot_ref[0] = 1
            for cp in weight_copies(e, 0):
                cp.start()

        @pl.when((i == 0) | (e != prev))
        def _():
            slot = 1 - slot_ref[0]
            slot_ref[0] = slot
            for cp in weight_copies(e, slot):
                cp.wait()
            nxt = nxt_ref[e]

            @pl.when(nxt != e)
            def _():
                for cp in weight_copies(nxt, 1 - slot):
                    cp.start()

    for n in range(1, MOE_STEP_BLOCKS + 1):
        @pl.when(halves == n)
        def _(n=n):
            ffn(slice(0, n * MOE_BLOCK))


def _experts(step_expert, step_halves, n_used, next_expert, xs, w_gu, b_gu, w_down, b_down, l):
    P, DH = xs.shape
    D = 2 * DH
    R = MOE_STEP_BLOCKS * MOE_BLOCK
    NS = P // R
    F2 = 2 * EXPERT_FF

    def row_map(i, se, sh, nu, nxt):
        return (jnp.minimum(i, nu[0] - 1), 0)

    def exp_map(i, se, sh, nu, nxt):
        return (l, se[jnp.minimum(i, nu[0] - 1)], 0, 0)

    grid_spec = pltpu.PrefetchScalarGridSpec(
        num_scalar_prefetch=4,
        grid=(NS,),
        in_specs=[pl.BlockSpec((R, DH), row_map),
                  pl.BlockSpec(memory_space=pl.ANY),
                  pl.BlockSpec((None, None, 1, F2), exp_map),
                  pl.BlockSpec(memory_space=pl.ANY),
                  pl.BlockSpec((None, None, 1, D), exp_map)],
        out_specs=pl.BlockSpec((R, DH), row_map),
        scratch_shapes=[pltpu.VMEM((2, D, F2), F32), pltpu.VMEM((2, EXPERT_FF, D), F32),
                        pltpu.SMEM((1,), jnp.int32),
                        pltpu.SemaphoreType.DMA((2, 2))],
    )
    return pl.pallas_call(
        functools.partial(_expert_kernel, l),
        out_shape=jax.ShapeDtypeStruct((P, DH), jnp.uint32),
        grid_spec=grid_spec,
        compiler_params=pltpu.CompilerParams(dimension_semantics=("arbitrary",), vmem_limit_bytes=VMEM_LIMIT),
        name="experts",
    )(step_expert, step_halves, n_used, next_expert, xs, w_gu, b_gu.reshape(DEPTH, N_EXPERTS, 1, F2), w_down,
      b_down.reshape(DEPTH, N_EXPERTS, 1, D))


def _sc_gather_rows(x, idx):
    M = idx.shape[0]
    D = x.shape[1]
    W = SC_GATHER_WINDOW
    mesh = plsc.VectorSubcoreMesh(core_axis_name="core", subcore_axis_name="subcore")
    n_workers = mesh.num_cores * mesh.num_subcores
    rows_per = M // n_workers
    assert rows_per * n_workers == M and rows_per % W == 0

    @pl.kernel(out_type=jax.ShapeDtypeStruct((M, D), x.dtype), mesh=mesh, name="sc_gather_rows",
               scratch_types=[pltpu.VMEM((rows_per,), jnp.int32), pltpu.VMEM((W, D), x.dtype)])
    def gather_kernel(x_hbm, i_hbm, o_hbm, idx_vmem, buf):
        wid = lax.axis_index("core") * mesh.num_subcores + lax.axis_index("subcore")
        base = wid * rows_per
        pltpu.sync_copy(i_hbm.at[pl.ds(base, rows_per)], idx_vmem)

        @pl.loop(0, rows_per // W)
        def _(j):
            pltpu.sync_copy(x_hbm.at[idx_vmem.at[pl.ds(j * W, W)]], buf)
            pltpu.sync_copy(buf, o_hbm.at[pl.ds(base + j * W, W)])

    return gather_kernel(x, idx)


def _sc_scatter_rows(x, idx, n_out):
    K, N = idx.shape
    D = x.shape[1]
    W = SC_GATHER_WINDOW
    mesh = plsc.VectorSubcoreMesh(core_axis_name="core", subcore_axis_name="subcore")
    n_workers = mesh.num_cores * mesh.num_subcores
    rows_per = N // n_workers
    assert rows_per * n_workers == N and rows_per % W == 0

    @pl.kernel(out_type=jax.ShapeDtypeStruct((n_out, D), x.dtype), mesh=mesh, name="sc_scatter_rows",
               scratch_types=[pltpu.VMEM((K * rows_per,), jnp.int32), pltpu.VMEM((W, D), x.dtype),
                              pltpu.SemaphoreType.DMA((K,))])
    def scatter_kernel(x_hbm, i_hbm, o_hbm, idx_vmem, buf, sem):
        wid = lax.axis_index("core") * mesh.num_subcores + lax.axis_index("subcore")
        base = wid * rows_per
        for k in range(K):
            pltpu.sync_copy(i_hbm.at[pl.ds(k * N + base, rows_per)], idx_vmem.at[pl.ds(k * rows_per, rows_per)])

        @pl.loop(0, rows_per // W)
        def _(j):
            pltpu.sync_copy(x_hbm.at[pl.ds(base + j * W, W)], buf)
            copies = [pltpu.make_async_copy(buf, o_hbm.at[idx_vmem.at[pl.ds(k * rows_per + j * W, W)]], sem.at[k])
                      for k in range(K)]
            for cp in copies:
                cp.start()
            for cp in copies:
                cp.wait()

    return scatter_kernel(x, idx.reshape(K * N))


def _combine_kernel(x_ref, yg_ref, route_ref, ln3g_ref, ln3b_ref, *rest):
    out_ref = rest[-1]
    x = x_ref[...]
    g = jnp.transpose(route_ref[...])
    y = jnp.zeros_like(x)
    for k in range(TOP_K):
        y = y + g[:, TOP_K + k:TOP_K + k + 1] * _unpack_bf16_pairs(yg_ref[k * CMB_TILE:(k + 1) * CMB_TILE, :])
    out_ref[...] = _layer_norm(DEEPNORM_ALPHA * x + y, ln3g_ref[...], ln3b_ref[...])


def _combine(x2, yg, route, p, l, first, acc, out_first, n_out):
    _, D = x2.shape
    T = CMB_TILE
    tiles = yg.shape[0] // (TOP_K * T)
    in_specs = [pl.BlockSpec((T, D), lambda i: (first + i, 0)), pl.BlockSpec((TOP_K * T, D // 2), lambda i: (i, 0)),
                pl.BlockSpec((ROUTE_ROWS, T), lambda i: (0, first + i)), _layer_spec(p['ln3g'], l), _layer_spec(p['ln3b'], l)]
    args = [x2, yg, route, p['ln3g'], p['ln3b']]
    aliases = {}
    if acc is not None:
        in_specs.append(pl.BlockSpec(memory_space=pl.ANY))
        args.append(acc)
        aliases = {len(args) - 1: 0}
    return pl.pallas_call(
        _combine_kernel,
        out_shape=jax.ShapeDtypeStruct((n_out, D), F32),
        grid=(tiles,),
        in_specs=in_specs,
        out_specs=pl.BlockSpec((T, D), lambda i: (out_first + first + i, 0)),
        input_output_aliases=aliases,
        compiler_params=pltpu.CompilerParams(dimension_semantics=("arbitrary",), vmem_limit_bytes=VMEM_LIMIT),
        name="combine",
    )(*args)


def _prep(w_in, b_in, gla_wg2, gla_bg, gla_norm_g, sgu_ln_g, sgu_ln_b, sgu_ws, sgu_bs, pool_w, pool_scale,
          w_up_a, w_up_b, w_up_c, w_o, ln1_g, ln1_b, xa_wq, xa_wk, xa_wv, xa_wo, ln2_g, ln2_b,
          router_w, router_b, ln3_g, ln3_b):
    L = w_in.shape[0]
    o_glow = O_GLOW
    o_uv = o_glow + GLA_RANK
    row = lambda a: a.reshape(L, 1, -1).astype(F32)
    pad_last = lambda a, n: jnp.pad(a, [(0, 0)] * (a.ndim - 1) + [(0, n - a.shape[-1])])
    p = {}
    p['whead'] = w_in[..., :o_glow].astype(BF16)
    p['wglow'] = pad_last(w_in[..., o_glow:o_uv], LANES).astype(BF16)
    p['wtail'] = w_in[..., o_uv:].astype(BF16)
    p['bcat'] = row(jnp.concatenate([b_in[..., :o_glow], pad_last(b_in[..., o_glow:o_uv], LANES), b_in[..., o_uv:]],
                                    axis=-1))
    p['wg2'] = jnp.pad(gla_wg2, ((0, 0), (0, LANES - GLA_RANK), (0, 0))).astype(BF16)
    p['bg'] = row(gla_bg)
    p['gnorm'] = row(gla_norm_g)
    p['slng'] = row(sgu_ln_g)
    p['slnb'] = row(sgu_ln_b)
    p['wtril'] = jnp.tril(sgu_ws).astype(BF16)
    p['sbias'] = jnp.repeat(jnp.swapaxes(sgu_bs, 1, 2), SGU_GD, axis=2).astype(F32)
    G = len(POOL_WINDOWS)
    eye = jnp.eye(G, dtype=F32)
    p['poolw'] = jnp.einsum('lgcd,gh->lgchd', pool_w, eye).reshape(L, POOL_WIDTH, POOL_WIDTH).astype(BF16)
    p['pscale'] = row(pool_scale)
    p['wupa'], p['wupb'], p['wupc'], p['wo'] = w_up_a, w_up_b, w_up_c, w_o
    p['ln1g'], p['ln1b'] = row(ln1_g), row(ln1_b)
    p['wq'], p['wk'], p['wv'], p['wxo'] = xa_wq, xa_wk, xa_wv, xa_wo
    p['ln2g'], p['ln2b'] = row(ln2_g), row(ln2_b)
    rwt = jnp.swapaxes(router_w, 1, 2)
    rwt_hi = rwt.astype(BF16)
    p['rwt'] = jnp.concatenate([rwt_hi, (rwt - rwt_hi.astype(F32)).astype(BF16)], axis=1)
    p['rbt'] = jnp.broadcast_to(router_b[:, :, None], router_b.shape + (XA_TILE,)).astype(F32)
    p['ln3g'], p['ln3b'] = row(ln3_g), row(ln3_b)
    return p


def _route(route, counts):
    N = route.shape[1]
    top_idx = route[0:TOP_K].astype(jnp.int32)
    rank = route[2 * TOP_K:3 * TOP_K].astype(jnp.int32)
    counts = counts[:, 0].astype(jnp.int32)
    R = MOE_STEP_BLOCKS * MOE_BLOCK
    blocks = (counts + MOE_BLOCK - 1) // MOE_BLOCK
    padded = ((counts + R - 1) // R) * R
    pad_end = jnp.cumsum(padded)
    pad_start = pad_end - padded
    ids = jnp.arange(N_EXPERTS, dtype=jnp.int32)
    start_of = jnp.sum(jnp.where(top_idx[:, :, None] == ids[None, None, :], pad_start[None, None, :], 0), axis=-1)
    dest = start_of + rank
    n_steps = N * TOP_K // R + N_EXPERTS
    step_start = jnp.arange(n_steps, dtype=jnp.int32) * R
    step_expert = jnp.minimum(jnp.sum((pad_end[None, :] <= step_start[:, None]).astype(jnp.int32), axis=1),
                              N_EXPERTS - 1)
    mine = step_expert[:, None] == ids[None, :]
    blocks_before = (step_start - jnp.sum(jnp.where(mine, pad_start[None, :], 0), axis=1)) // MOE_BLOCK
    step_halves = jnp.clip(jnp.sum(jnp.where(mine, blocks[None, :], 0), axis=1) - blocks_before, 0, MOE_STEP_BLOCKS)
    step_halves = jnp.where(step_start < pad_end[-1], step_halves, 0).astype(jnp.int32)
    n_used = (pad_end[-1] // R).astype(jnp.int32).reshape(1)
    later = jnp.where((ids[None, :] > ids[:, None]) & (counts[None, :] > 0), ids[None, :], N_EXPERTS)
    first_later = jnp.min(later, axis=1)
    next_expert = jnp.where(first_later < N_EXPERTS, first_later, ids).astype(jnp.int32)
    return dest, step_expert, step_halves, n_used, next_expert


def kernel(x, mem, w_in, b_in, gla_wg2, gla_bg, gla_norm_g, sgu_ln_g, sgu_ln_b, sgu_ws, sgu_bs, pool_w, pool_scale, w_up_a, w_up_b, w_up_c, w_o, ln1_g, ln1_b, xa_wq, xa_wk, xa_wv, xa_wo, ln2_g, ln2_b, router_w, router_b, exp_w_gu, exp_b_gu, exp_w_down, exp_b_down, ln3_g, ln3_b):
    B, S, D = x.shape
    N = B * S
    p = _prep(w_in, b_in, gla_wg2, gla_bg, gla_norm_g, sgu_ln_g, sgu_ln_b, sgu_ws, sgu_bs, pool_w, pool_scale,
              w_up_a, w_up_b, w_up_c, w_o, ln1_g, ln1_b, xa_wq, xa_wk, xa_wv, xa_wo, ln2_g, ln2_b,
              router_w, router_b, ln3_g, ln3_b)
    memt = jnp.swapaxes(mem, 1, 2)
    nb = B // TOKEN_GROUPS
    Ng = nb * S
    tile_rows = TOP_K * CMB_TILE
    xg = [None] * TOKEN_GROUPS
    out = None
    for l in range(DEPTH):
        last = l == DEPTH - 1
        for g in range(TOKEN_GROUPS):
            x1 = _mixer(x, p, l, g * nb, nb) if l == 0 else _mixer(xg[g], p, l, 0, nb)
            kt, v = _memkv(mem, memt, p, l, g * nb, nb)
            x2, x2p, route, counts = _xattn(x1, kt, v, p, l)
            dest, step_expert, step_halves, n_used, next_expert = _route(route, counts)
            n_slots = Ng * TOP_K + N_EXPERTS * MOE_STEP_BLOCKS * MOE_BLOCK
            xs = _sc_scatter_rows(x2p.reshape(Ng, D // 2), dest, n_slots)
            ys = _experts(step_expert, step_halves, n_used, next_expert, xs, exp_w_gu, exp_b_gu, exp_w_down,
                          exp_b_down, l)
            dest_km = dest.reshape(TOP_K, Ng // CMB_TILE, CMB_TILE).transpose(1, 0, 2).reshape(-1)
            acc = out if last else None
            out_first = g * (Ng // CMB_TILE) if last else 0
            first = 0
            for share in COMBINE_SHARES:
                tiles = (Ng // CMB_TILE) * share // sum(COMBINE_SHARES)
                yg = _sc_gather_rows(ys, dest_km[first * tile_rows:(first + tiles) * tile_rows])
                acc = _combine(x2.reshape(Ng, D), yg, route, p, l, first, acc, out_first, N if last else Ng)
                first += tiles
            if last:
                out = acc
            else:
                xg[g] = acc.reshape(nb, S, D)
    return out.reshape(B, S, D)
```

```python
import functools

import jax
import jax.numpy as jnp
from jax import lax
from jax.experimental import pallas as pl
from jax.experimental.pallas import tpu as pltpu
from jax.experimental.pallas import tpu_sc as plsc

F32 = jnp.float32
BF16 = jnp.bfloat16

D_MODEL = 1024
DEPTH = 2
GLA_HEADS = 4
GLA_KEY = 256
GLA_VAL = 512
GLA_DK = 64
GLA_DV = 128
GLA_RANK = 16
GLA_TAU = 16.0
GLA_CHUNK = 64
SGU_GROUPS = 4
SGU_WIDTH = 256
SGU_GD = 64
SGU_CHUNK = 128
POOL_WINDOWS = (2, 4, 8, 16)
POOL_WIDTH = 256
POOL_GD = 64
POOL_CARRY = 32
MEM_LEN = 256
XA_HEADS = 4
XA_DH = 256
N_EXPERTS = 32
TOP_K = 4
EXPERT_FF = 1024
SWIGLU_LIMIT = 7.0
SWIGLU_ALPHA = 1.702
DEEPNORM_ALPHA = (2 * DEPTH) ** 0.25
LN_EPS = 1e-5
LANES = 128
VMEM_LIMIT = 56 * 1024 * 1024

MIX_TILE = 512
GLA_SUB = 256
XA_TILE = 1024
MOE_BLOCK = 256
MOE_STEP_BLOCKS = 4
CMB_TILE = 512
COMBINE_SHARES = (1, 1, 2, 2, 2)
ROUTE_ROWS = 16
SC_GATHER_WINDOW = 128

O_QKVR = 0
O_GLOW = 2 * GLA_KEY + 2 * GLA_VAL
O_UV = O_GLOW + LANES
O_XC = O_UV + 2 * SGU_WIDTH
O_GATE = O_XC + POOL_WIDTH


def _dot(a, b):
    return jnp.dot(a, b, preferred_element_type=F32)


def _dot_t0(a, b):
    return lax.dot_general(a, b, (((0,), (0,)), ((), ())), preferred_element_type=F32)


def _dot_t1(a, b):
    return lax.dot_general(a, b, (((1,), (1,)), ((), ())), preferred_element_type=F32)


def _split_bf16(x):
    hi = x.astype(BF16)
    lo = (x - hi.astype(F32)).astype(BF16)
    return hi, lo


def _layer_norm(x, g, b):
    mu = jnp.mean(x, axis=-1, keepdims=True)
    xc = x - mu
    var = jnp.mean(xc * xc, axis=-1, keepdims=True)
    return xc * lax.rsqrt(var + LN_EPS) * g + b


def _sigmoid(x):
    return 1.0 / (1.0 + jnp.exp(-x))


def _pack_bf16_pairs(x):
    H = x.shape[1] // 2
    bits = lax.bitcast_convert_type(x.astype(BF16).astype(F32), jnp.uint32)
    return (bits[:, :H] >> 16) | (bits[:, H:] & jnp.uint32(0xFFFF0000))


def _unpack_bf16_pairs(w):
    lo = lax.bitcast_convert_type(w << 16, F32)
    hi = lax.bitcast_convert_type(w & jnp.uint32(0xFFFF0000), F32)
    return jnp.concatenate([lo, hi], axis=1)


def _layer_spec(arr, l):
    nd = arr.ndim - 1
    return pl.BlockSpec((None,) + arr.shape[1:], lambda *_: (l,) + (0,) * nd, pipeline_mode=pl.Buffered(1))


def _mixer_kernel(x_ref, whead_ref, wglow_ref, wtail_ref, bcat_ref, wg2_ref, bg_ref, gnorm_ref,
                  slng_ref, slnb_ref, wtril_ref, sbias_ref, poolw_ref, pscale_ref,
                  wupa_ref, wupb_ref, wupc_ref, wo_ref, ln1g_ref, ln1b_ref,
                  out_ref,
                  state_ref, qkvr_ref, ya_ref, vln_ref, e_ref, s2_ref, s4_ref, s8_ref):
    T = MIX_TILE
    D = D_MODEL
    j = pl.program_id(1)
    x = x_ref[0]
    xb = x.astype(BF16)

    def proj(lo, hi):
        if hi <= O_GLOW:
            w = whead_ref[:, lo:hi]
        elif lo == O_GLOW and hi == O_UV:
            w = wglow_ref[...]
        else:
            w = wtail_ref[:, lo - O_UV:hi - O_UV]
        return _dot(xb, w) + bcat_ref[:, lo:hi]

    @pl.when(j == 0)
    def _():
        state_ref[...] = jnp.zeros_like(state_ref)
        e_ref[0:POOL_CARRY, :] = jnp.zeros((POOL_CARRY, POOL_WIDTH), F32)

    @pl.when(j > 0)
    def _():
        e_ref[0:POOL_CARRY, :] = e_ref[T:T + POOL_CARRY, :]

    qkvr_ref[...] = proj(O_QKVR, O_GLOW)
    glow = proj(O_GLOW, O_UV)
    z = _dot(glow.astype(BF16), wg2_ref[...]) + bg_ref[...]
    la = (jnp.minimum(z, 0.0) - jnp.log1p(jnp.exp(-jnp.abs(z)))) * (1.0 / GLA_TAU)
    la_hi, la_lo = _split_bf16(la)
    gate_a = _sigmoid(proj(O_GATE, O_GATE + D))

    C = GLA_CHUNK
    G = GLA_SUB
    NC = G // C
    CSH = C.bit_length() - 1
    row = lax.broadcasted_iota(jnp.int32, (G, G), 0)
    col = lax.broadcasted_iota(jnp.int32, (G, G), 1)
    same_chunk = (row >> CSH) == (col >> CSH)
    causal = same_chunk & (row >= col)
    causal_bf = jnp.where(causal, 1.0, 0.0).astype(BF16)
    lane = lax.broadcasted_iota(jnp.int32, (G, GLA_KEY), 1)
    gate_b = None
    for g0 in range(0, T, G):
        gr = slice(g0, g0 + G)
        lh, ll = la_hi[gr], la_lo[gr]
        b = _dot(causal_bf, lh) + _dot(causal_bf, ll)
        b_last = [b[(c + 1) * C - 1:(c + 1) * C, :] for c in range(NC)]
        b_end = jnp.concatenate([jnp.broadcast_to(r, (C, GLA_KEY)) for r in b_last], axis=0)
        q = qkvr_ref[gr, 0:GLA_KEY]
        k = qkvr_ref[gr, GLA_KEY:2 * GLA_KEY]
        v = qkvr_ref[gr, 2 * GLA_KEY:2 * GLA_KEY + GLA_VAL].astype(BF16)
        q_dec = q * (GLA_DK ** -0.5) * jnp.exp(b)
        k_dec = (k * jnp.exp(-b)).astype(BF16)
        k_tail = (k * jnp.exp(b_end - b)).astype(BF16)
        q_dec_bf = q_dec.astype(BF16)
        o_heads = []
        for h in range(GLA_HEADS):
            q_h = jnp.where((lane >= h * GLA_DK) & (lane < (h + 1) * GLA_DK), q_dec, 0.0).astype(BF16)
            scores = jnp.where(causal, _dot_t1(q_h, k_dec), 0.0).astype(BF16)
            o_heads.append(_dot(scores, v[:, h * GLA_DV:(h + 1) * GLA_DV]))
        o_intra = jnp.concatenate(o_heads, axis=1)
        if gate_b is None:
            gate_b = _sigmoid(proj(O_GATE + D, O_GATE + 2 * D))
        dec_cols = jnp.exp(jnp.concatenate(b_last + [jnp.zeros((8 - NC, GLA_KEY), F32)], axis=0)).T
        dec_all = jnp.concatenate([jnp.broadcast_to(dec_cols[:, c:c + 1], (GLA_KEY, LANES)) for c in range(NC)], axis=1)
        o_inter = []
        for c in range(NC):
            rows = slice(c * C, (c + 1) * C)
            o_inter.append(_dot(q_dec_bf[rows], state_ref[...].astype(BF16)))
            kv = _dot_t0(k_tail[rows], v[rows])
            for h in range(GLA_HEADS):
                rs = slice(h * GLA_DK, (h + 1) * GLA_DK)
                cs = slice(h * GLA_DV, (h + 1) * GLA_DV)
                state_ref[rs, cs] = dec_all[rs, c * LANES:(c + 1) * LANES] * state_ref[rs, cs] + kv[rs, cs]
        o = o_intra + jnp.concatenate(o_inter, axis=0)
        for h in range(GLA_HEADS):
            cs = slice(h * GLA_DV, (h + 1) * GLA_DV)
            o_h = o[:, cs]
            ms = jnp.mean(o_h * o_h, axis=-1, keepdims=True)
            o_h = o_h * lax.rsqrt(ms + LN_EPS) * gnorm_ref[:, cs]
            r_h = qkvr_ref[gr, 2 * GLA_KEY + GLA_VAL + h * GLA_DV:2 * GLA_KEY + GLA_VAL + (h + 1) * GLA_DV]
            ya_ref[gr, cs] = (o_h * (r_h * _sigmoid(r_h))).astype(BF16)

    gate_c = _sigmoid(proj(O_GATE + 2 * D, O_GATE + 3 * D))
    uv = proj(O_UV, O_XC)
    zg = 0.5 * uv * (1.0 + lax.erf(uv * (2.0 ** -0.5)))
    u = zg[:, :SGU_WIDTH]
    vln_ref[...] = _layer_norm(zg[:, SGU_WIDTH:], slng_ref[...], slnb_ref[...])
    lane_s = lax.broadcasted_iota(jnp.int32, (SGU_CHUNK, SGU_WIDTH), 1)
    s_parts = []
    for n in range(T // SGU_CHUNK):
        vc = vln_ref[n * SGU_CHUNK:(n + 1) * SGU_CHUNK, :]
        s = sbias_ref[...]
        for g in range(SGU_GROUPS):
            vg = jnp.where((lane_s >= g * SGU_GD) & (lane_s < (g + 1) * SGU_GD), vc, 0.0).astype(BF16)
            s = s + _dot(wtril_ref[g], vg)
        s_parts.append(s)
    y_b = (u * jnp.concatenate(s_parts, axis=0)).astype(BF16)

    P = POOL_CARRY
    xc = proj(O_XC, O_GATE)
    e_ref[P:P + T, :] = xc
    s2_ref[8:P + T, :] = e_ref[8:P + T, :] + e_ref[7:P + T - 1, :]
    s4_ref[16:P + T, :] = s2_ref[16:P + T, :] + s2_ref[14:P + T - 2, :]
    s8_ref[24:P + T, :] = s4_ref[24:P + T, :] + s4_ref[20:P + T - 4, :]
    s16 = s8_ref[P:P + T, :] + s8_ref[P - 8:P + T - 8, :]
    lane_p = lax.broadcasted_iota(jnp.int32, (T, POOL_WIDTH), 1)
    tpos = lax.broadcasted_iota(jnp.int32, (T, POOL_WIDTH), 0) + (j * T + 1)
    grp = lane_p >> (POOL_GD.bit_length() - 1)
    win = jnp.where(grp == 0, POOL_WINDOWS[0], jnp.where(grp == 1, POOL_WINDOWS[1],
                    jnp.where(grp == 2, POOL_WINDOWS[2], POOL_WINDOWS[3])))
    wsum = jnp.where(grp == 0, s2_ref[P:P + T, :], jnp.where(grp == 1, s4_ref[P:P + T, :],
                     jnp.where(grp == 2, s8_ref[P:P + T, :], s16)))
    count = jnp.minimum(tpos, win).astype(F32)
    pooled = wsum / count - xc
    y_c = (_dot(pooled.astype(BF16), poolw_ref[...]) * pscale_ref[...]).astype(BF16)

    merged = gate_a * _dot(ya_ref[...], wupa_ref[...].astype(BF16))
    merged += gate_b * _dot(y_b, wupb_ref[...].astype(BF16))
    merged += gate_c * _dot(y_c, wupc_ref[...].astype(BF16))
    h = _dot(merged.astype(BF16), wo_ref[...].astype(BF16))
    out_ref[0] = _layer_norm(DEEPNORM_ALPHA * x + h, ln1g_ref[...], ln1b_ref[...])


_MIXER_WEIGHTS = ('whead', 'wglow', 'wtail', 'bcat', 'wg2', 'bg', 'gnorm', 'slng', 'slnb', 'wtril', 'sbias', 'poolw', 'pscale',
                  'wupa', 'wupb', 'wupc', 'wo', 'ln1g', 'ln1b')


def _mixer(x, p, l):
    B, S, D = x.shape
    T = MIX_TILE
    weights = [p[n] for n in _MIXER_WEIGHTS]
    return pl.pallas_call(
        _mixer_kernel,
        out_shape=jax.ShapeDtypeStruct((B, S, D), F32),
        grid=(B, S // T),
        in_specs=[pl.BlockSpec((1, T, D), lambda b, j: (b, j, 0))] + [_layer_spec(w, l) for w in weights],
        out_specs=pl.BlockSpec((1, T, D), lambda b, j: (b, j, 0)),
        scratch_shapes=[
            pltpu.VMEM((GLA_KEY, GLA_VAL), F32),
            pltpu.VMEM((T, 2 * GLA_KEY + 2 * GLA_VAL), F32),
            pltpu.VMEM((T, GLA_VAL), BF16),
            pltpu.VMEM((T, SGU_WIDTH), F32),
            pltpu.VMEM((T + POOL_CARRY, POOL_WIDTH), F32),
            pltpu.VMEM((T + POOL_CARRY, POOL_WIDTH), F32),
            pltpu.VMEM((T + POOL_CARRY, POOL_WIDTH), F32),
            pltpu.VMEM((T + POOL_CARRY, POOL_WIDTH), F32),
        ],
        compiler_params=pltpu.CompilerParams(dimension_semantics=("arbitrary", "arbitrary"),
                                             vmem_limit_bytes=VMEM_LIMIT),
        name="mixer",
    )(x, *weights)


def _xattn_kernel(x_ref, memt_ref, mem_ref, wk_ref, wv_ref, wq_ref, wo_ref, ln2g_ref, ln2b_ref, rwt_ref, rbt_ref,
                  x2_ref, x2p_ref, route_ref, counts_ref, carry_ref, kt_ref, v_ref):
    T = XA_TILE

    @pl.when(pl.program_id(1) == 0)
    def _():
        kt_ref[...] = _dot_t0(wk_ref[...].astype(BF16), memt_ref[0].astype(BF16)).astype(BF16)
        v_ref[...] = _dot(mem_ref[0].astype(BF16), wv_ref[...].astype(BF16)).astype(BF16)

    @pl.when((pl.program_id(0) == 0) & (pl.program_id(1) == 0))
    def _():
        carry_ref[...] = jnp.zeros_like(carry_ref)

    x = x_ref[0]
    q = (_dot(x.astype(BF16), wq_ref[...].astype(BF16)) * (XA_DH ** -0.5)).astype(BF16)
    h = jnp.zeros_like(x)
    for hd in range(XA_HEADS):
        cs = slice(hd * XA_DH, (hd + 1) * XA_DH)
        s = _dot(q[:, cs], kt_ref[cs, :])
        e = jnp.exp(s - jnp.max(s, axis=-1, keepdims=True))
        o = _dot(e.astype(BF16), v_ref[:, cs]) / jnp.sum(e, axis=-1, keepdims=True)
        h = h + _dot(o.astype(BF16), wo_ref[cs, :].astype(BF16))
    x2 = _layer_norm(DEEPNORM_ALPHA * x + h, ln2g_ref[...], ln2b_ref[...])
    x2_ref[0] = x2
    x2p_ref[0] = _pack_bf16_pairs(x2)

    E = N_EXPERTS
    hi, lo = _split_bf16(x2)
    lt = _dot_t1(rwt_ref[...], hi)
    logits = lt[0:E] + (lt[E:2 * E] + _dot_t1(rwt_ref[0:E, :], lo)) + rbt_ref[...]

    eid = lax.broadcasted_iota(jnp.int32, (E, T), 0)
    neg_inf = jnp.float32(-jnp.inf)
    rest = logits
    tops, picks = [], []
    for _ in range(TOP_K):
        m = jnp.max(rest, axis=0, keepdims=True)
        idx = jnp.min(jnp.where(rest == m, eid, E), axis=0, keepdims=True)
        pick = eid == idx
        rest = jnp.where(pick, neg_inf, rest)
        tops.append((m, idx))
        picks.append(pick)
    exps = [jnp.exp(m - tops[0][0]) for m, _ in tops]
    denom = exps[0]
    for e in exps[1:]:
        denom = denom + e

    chosen = jnp.zeros((E, T), F32)
    for pick in picks:
        chosen = chosen + jnp.where(pick, 1.0, 0.0)
    chosen_bf = chosen.astype(BF16)
    earlier = (lax.broadcasted_iota(jnp.int32, (T, T), 0) < lax.broadcasted_iota(jnp.int32, (T, T), 1))
    carry = carry_ref[...]
    before = _dot(chosen_bf, jnp.where(earlier, 1.0, 0.0).astype(BF16)) + jnp.concatenate([carry] * (T // LANES), axis=1)
    carry = carry + _dot(chosen_bf, jnp.ones((T, LANES), BF16))
    carry_ref[...] = carry
    counts_ref[...] = carry

    rid = lax.broadcasted_iota(jnp.int32, (ROUTE_ROWS, T), 0)
    route = jnp.zeros((ROUTE_ROWS, T), F32)
    for k in range(TOP_K):
        rank = jnp.sum(jnp.where(picks[k], before, 0.0), axis=0, keepdims=True)
        route = jnp.where(rid == k, tops[k][1].astype(F32), route)
        route = jnp.where(rid == TOP_K + k, exps[k] / denom, route)
        route = jnp.where(rid == 2 * TOP_K + k, rank, route)
    route_ref[...] = route


def _xattn(x, memt, mem, p, l):
    B, S, D = x.shape
    T = XA_TILE
    M = MEM_LEN
    weights = [p[n] for n in ('wk', 'wv', 'wq', 'wxo', 'ln2g', 'ln2b', 'rwt', 'rbt')]
    return pl.pallas_call(
        _xattn_kernel,
        out_shape=(jax.ShapeDtypeStruct((B, S, D), F32), jax.ShapeDtypeStruct((B, S, D // 2), jnp.uint32),
                   jax.ShapeDtypeStruct((ROUTE_ROWS, B * S), F32), jax.ShapeDtypeStruct((N_EXPERTS, LANES), F32)),
        grid=(B, S // T),
        in_specs=[pl.BlockSpec((1, T, D), lambda b, j: (b, j, 0)),
                  pl.BlockSpec((1, D, M), lambda b, j: (b, 0, 0)),
                  pl.BlockSpec((1, M, D), lambda b, j: (b, 0, 0))] + [_layer_spec(w, l) for w in weights],
        out_specs=(pl.BlockSpec((1, T, D), lambda b, j: (b, j, 0)),
                   pl.BlockSpec((1, T, D // 2), lambda b, j: (b, j, 0)),
                   pl.BlockSpec((ROUTE_ROWS, T), lambda b, j: (0, b * (S // T) + j)),
                   pl.BlockSpec((N_EXPERTS, LANES), lambda b, j: (0, 0))),
        scratch_shapes=[pltpu.VMEM((N_EXPERTS, LANES), F32),
                        pltpu.VMEM((D, M), BF16), pltpu.VMEM((M, D), BF16)],
        compiler_params=pltpu.CompilerParams(dimension_semantics=("arbitrary", "arbitrary"),
                                             vmem_limit_bytes=VMEM_LIMIT),
        name="xattn",
    )(x, memt, mem, *weights)


def _expert_kernel(layer, se_ref, sh_ref, nu_ref, nxt_ref, xs_ref, wgu_hbm, bgu_ref, wd_hbm, bd_ref, out_ref,
                   wgu_st, wd_st, slot_ref, sem):
    i = pl.program_id(0)
    F = EXPERT_FF
    halves = sh_ref[i]

    def weight_copies(e, slot):
        return (pltpu.make_async_copy(wgu_hbm.at[layer, e], wgu_st.at[slot], sem.at[slot, 0]),
                pltpu.make_async_copy(wd_hbm.at[layer, e], wd_st.at[slot], sem.at[slot, 1]))

    def ffn(rows):
        slot = slot_ref[0]
        xsb = _unpack_bf16_pairs(xs_ref[rows, :]).astype(BF16)
        hh = _dot(xsb, wgu_st[slot].astype(BF16)) + bgu_ref[...]
        h_glu = jnp.minimum(hh[:, :F], SWIGLU_LIMIT)
        h_lin = jnp.clip(hh[:, F:], -SWIGLU_LIMIT, SWIGLU_LIMIT)
        a = h_glu * _sigmoid(SWIGLU_ALPHA * h_glu) * (h_lin + 1.0)
        out_ref[rows, :] = _pack_bf16_pairs(_dot(a.astype(BF16), wd_st[slot].astype(BF16)) + bd_ref[...])

    @pl.when(halves > 0)
    def _():
        e = se_ref[i]
        prev = se_ref[jnp.maximum(i - 1, 0)]

        @pl.when(i == 0)
        def _():
            slot_ref[0] = 1
            for cp in weight_copies(e, 0):
                cp.start()

        @pl.when((i == 0) | (e != prev))
        def _():
            slot = 1 - slot_ref[0]
            slot_ref[0] = slot
            for cp in weight_copies(e, slot):
                cp.wait()
            nxt = nxt_ref[e]

            @pl.when(nxt != e)
            def _():
                for cp in weight_copies(nxt, 1 - slot):
                    cp.start()

    for n in range(1, MOE_STEP_BLOCKS + 1):
        @pl.when(halves == n)
        def _(n=n):
            ffn(slice(0, n * MOE_BLOCK))


def _experts(step_expert, step_halves, n_used, next_expert, xs, w_gu, b_gu, w_down, b_down, l):
    P, DH = xs.shape
    D = 2 * DH
    R = MOE_STEP_BLOCKS * MOE_BLOCK
    NS = P // R
    F2 = 2 * EXPERT_FF

    def row_map(i, se, sh, nu, nxt):
        return (jnp.minimum(i, nu[0] - 1), 0)

    def exp_map(i, se, sh, nu, nxt):
        return (l, se[jnp.minimum(i, nu[0] - 1)], 0, 0)

    grid_spec = pltpu.PrefetchScalarGridSpec(
        num_scalar_prefetch=4,
        grid=(NS,),
        in_specs=[pl.BlockSpec((R, DH), row_map),
                  pl.BlockSpec(memory_space=pl.ANY),
                  pl.BlockSpec((None, None, 1, F2), exp_map),
                  pl.BlockSpec(memory_space=pl.ANY),
                  pl.BlockSpec((None, None, 1, D), exp_map)],
        out_specs=pl.BlockSpec((R, DH), row_map),
        scratch_shapes=[pltpu.VMEM((2, D, F2), F32), pltpu.VMEM((2, EXPERT_FF, D), F32),
                        pltpu.SMEM((1,), jnp.int32),
                        pltpu.SemaphoreType.DMA((2, 2))],
    )
    return pl.pallas_call(
        functools.partial(_expert_kernel, l),
        out_shape=jax.ShapeDtypeStruct((P, DH), jnp.uint32),
        grid_spec=grid_spec,
        compiler_params=pltpu.CompilerParams(dimension_semantics=("arbitrary",), vmem_limit_bytes=VMEM_LIMIT),
        name="experts",
    )(step_expert, step_halves, n_used, next_expert, xs, w_gu, b_gu.reshape(DEPTH, N_EXPERTS, 1, F2), w_down,
      b_down.reshape(DEPTH, N_EXPERTS, 1, D))


def _sc_gather_rows(x, idx):
    M = idx.shape[0]
    D = x.shape[1]
    W = SC_GATHER_WINDOW
    mesh = plsc.VectorSubcoreMesh(core_axis_name="core", subcore_axis_name="subcore")
    n_workers = mesh.num_cores * mesh.num_subcores
    rows_per = M // n_workers
    assert rows_per * n_workers == M and rows_per % W == 0

    @pl.kernel(out_type=jax.ShapeDtypeStruct((M, D), x.dtype), mesh=mesh, name="sc_gather_rows",
               scratch_types=[pltpu.VMEM((rows_per,), jnp.int32), pltpu.VMEM((W, D), x.dtype)])
    def gather_kernel(x_hbm, i_hbm, o_hbm, idx_vmem, buf):
        wid = lax.axis_index("core") * mesh.num_subcores + lax.axis_index("subcore")
        base = wid * rows_per
        pltpu.sync_copy(i_hbm.at[pl.ds(base, rows_per)], idx_vmem)

        @pl.loop(0, rows_per // W)
        def _(j):
            pltpu.sync_copy(x_hbm.at[idx_vmem.at[pl.ds(j * W, W)]], buf)
            pltpu.sync_copy(buf, o_hbm.at[pl.ds(base + j * W, W)])

    return gather_kernel(x, idx)


def _sc_scatter_rows(x, idx, n_out):
    K, N = idx.shape
    D = x.shape[1]
    W = SC_GATHER_WINDOW
    mesh = plsc.VectorSubcoreMesh(core_axis_name="core", subcore_axis_name="subcore")
    n_workers = mesh.num_cores * mesh.num_subcores
    rows_per = N // n_workers
    assert rows_per * n_workers == N and rows_per % W == 0

    @pl.kernel(out_type=jax.ShapeDtypeStruct((n_out, D), x.dtype), mesh=mesh, name="sc_scatter_rows",
               scratch_types=[pltpu.VMEM((K * rows_per,), jnp.int32), pltpu.VMEM((W, D), x.dtype),
                              pltpu.SemaphoreType.DMA((K,))])
    def scatter_kernel(x_hbm, i_hbm, o_hbm, idx_vmem, buf, sem):
        wid = lax.axis_index("core") * mesh.num_subcores + lax.axis_index("subcore")
        base = wid * rows_per
        for k in range(K):
            pltpu.sync_copy(i_hbm.at[pl.ds(k * N + base, rows_per)], idx_vmem.at[pl.ds(k * rows_per, rows_per)])

        @pl.loop(0, rows_per // W)
        def _(j):
            pltpu.sync_copy(x_hbm.at[pl.ds(base + j * W, W)], buf)
            copies = [pltpu.make_async_copy(buf, o_hbm.at[idx_vmem.at[pl.ds(k * rows_per + j * W, W)]], sem.at[k])
                      for k in range(K)]
            for cp in copies:
                cp.start()
            for cp in copies:
                cp.wait()

    return scatter_kernel(x, idx.reshape(K * N))


def _combine_kernel(x_ref, yg_ref, route_ref, ln3g_ref, ln3b_ref, *rest):
    out_ref = rest[-1]
    x = x_ref[...]
    g = jnp.transpose(route_ref[...])
    y = jnp.zeros_like(x)
    for k in range(TOP_K):
        y = y + g[:, TOP_K + k:TOP_K + k + 1] * _unpack_bf16_pairs(yg_ref[k * CMB_TILE:(k + 1) * CMB_TILE, :])
    out_ref[...] = _layer_norm(DEEPNORM_ALPHA * x + y, ln3g_ref[...], ln3b_ref[...])


def _combine(x2, yg, route, p, l, first, acc):
    N, D = x2.shape
    T = CMB_TILE
    tiles = yg.shape[0] // (TOP_K * T)
    in_specs = [pl.BlockSpec((T, D), lambda i: (first + i, 0)), pl.BlockSpec((TOP_K * T, D // 2), lambda i: (i, 0)),
                pl.BlockSpec((ROUTE_ROWS, T), lambda i: (0, first + i)), _layer_spec(p['ln3g'], l), _layer_spec(p['ln3b'], l)]
    args = [x2, yg, route, p['ln3g'], p['ln3b']]
    aliases = {}
    if acc is not None:
        in_specs.append(pl.BlockSpec(memory_space=pl.ANY))
        args.append(acc)
        aliases = {len(args) - 1: 0}
    return pl.pallas_call(
        _combine_kernel,
        out_shape=jax.ShapeDtypeStruct((N, D), F32),
        grid=(tiles,),
        in_specs=in_specs,
        out_specs=pl.BlockSpec((T, D), lambda i: (first + i, 0)),
        input_output_aliases=aliases,
        compiler_params=pltpu.CompilerParams(dimension_semantics=("arbitrary",), vmem_limit_bytes=VMEM_LIMIT),
        name="combine",
    )(*args)


def _prep(w_in, b_in, gla_wg2, gla_bg, gla_norm_g, sgu_ln_g, sgu_ln_b, sgu_ws, sgu_bs, pool_w, pool_scale,
          w_up_a, w_up_b, w_up_c, w_o, ln1_g, ln1_b, xa_wq, xa_wk, xa_wv, xa_wo, ln2_g, ln2_b,
          router_w, router_b, ln3_g, ln3_b):
    L = w_in.shape[0]
    o_glow = O_GLOW
    o_uv = o_glow + GLA_RANK
    row = lambda a: a.reshape(L, 1, -1).astype(F32)
    pad_last = lambda a, n: jnp.pad(a, [(0, 0)] * (a.ndim - 1) + [(0, n - a.shape[-1])])
    p = {}
    p['whead'] = w_in[..., :o_glow].astype(BF16)
    p['wglow'] = pad_last(w_in[..., o_glow:o_uv], LANES).astype(BF16)
    p['wtail'] = w_in[..., o_uv:].astype(BF16)
    p['bcat'] = row(jnp.concatenate([b_in[..., :o_glow], pad_last(b_in[..., o_glow:o_uv], LANES), b_in[..., o_uv:]],
                                    axis=-1))
    p['wg2'] = jnp.pad(gla_wg2, ((0, 0), (0, LANES - GLA_RANK), (0, 0))).astype(BF16)
    p['bg'] = row(gla_bg)
    p['gnorm'] = row(gla_norm_g)
    p['slng'] = row(sgu_ln_g)
    p['slnb'] = row(sgu_ln_b)
    p['wtril'] = jnp.tril(sgu_ws).astype(BF16)
    p['sbias'] = jnp.repeat(jnp.swapaxes(sgu_bs, 1, 2), SGU_GD, axis=2).astype(F32)
    G = len(POOL_WINDOWS)
    eye = jnp.eye(G, dtype=F32)
    p['poolw'] = jnp.einsum('lgcd,gh->lgchd', pool_w, eye).reshape(L, POOL_WIDTH, POOL_WIDTH).astype(BF16)
    p['pscale'] = row(pool_scale)
    p['wupa'], p['wupb'], p['wupc'], p['wo'] = w_up_a, w_up_b, w_up_c, w_o
    p['ln1g'], p['ln1b'] = row(ln1_g), row(ln1_b)
    p['wq'], p['wk'], p['wv'], p['wxo'] = xa_wq, xa_wk, xa_wv, xa_wo
    p['ln2g'], p['ln2b'] = row(ln2_g), row(ln2_b)
    rwt = jnp.swapaxes(router_w, 1, 2)
    rwt_hi = rwt.astype(BF16)
    p['rwt'] = jnp.concatenate([rwt_hi, (rwt - rwt_hi.astype(F32)).astype(BF16)], axis=1)
    p['rbt'] = jnp.broadcast_to(router_b[:, :, None], router_b.shape + (XA_TILE,)).astype(F32)
    p['ln3g'], p['ln3b'] = row(ln3_g), row(ln3_b)
    return p


def _route(route, counts):
    N = route.shape[1]
    top_idx = route[0:TOP_K].astype(jnp.int32)
    rank = route[2 * TOP_K:3 * TOP_K].astype(jnp.int32)
    counts = counts[:, 0].astype(jnp.int32)
    R = MOE_STEP_BLOCKS * MOE_BLOCK
    blocks = (counts + MOE_BLOCK - 1) // MOE_BLOCK
    padded = ((counts + R - 1) // R) * R
    pad_end = jnp.cumsum(padded)
    pad_start = pad_end - padded
    ids = jnp.arange(N_EXPERTS, dtype=jnp.int32)
    start_of = jnp.sum(jnp.where(top_idx[:, :, None] == ids[None, None, :], pad_start[None, None, :], 0), axis=-1)
    dest = start_of + rank
    n_steps = N * TOP_K // R + N_EXPERTS
    step_start = jnp.arange(n_steps, dtype=jnp.int32) * R
    step_expert = jnp.minimum(jnp.sum((pad_end[None, :] <= step_start[:, None]).astype(jnp.int32), axis=1),
                              N_EXPERTS - 1)
    mine = step_expert[:, None] == ids[None, :]
    blocks_before = (step_start - jnp.sum(jnp.where(mine, pad_start[None, :], 0), axis=1)) // MOE_BLOCK
    step_halves = jnp.clip(jnp.sum(jnp.where(mine, blocks[None, :], 0), axis=1) - blocks_before, 0, MOE_STEP_BLOCKS)
    step_halves = jnp.where(step_start < pad_end[-1], step_halves, 0).astype(jnp.int32)
    n_used = (pad_end[-1] // R).astype(jnp.int32).reshape(1)
    later = jnp.where((ids[None, :] > ids[:, None]) & (counts[None, :] > 0), ids[None, :], N_EXPERTS)
    first_later = jnp.min(later, axis=1)
    next_expert = jnp.where(first_later < N_EXPERTS, first_later, ids).astype(jnp.int32)
    return dest, step_expert, step_halves, n_used, next_expert


def kernel(x, mem, w_in, b_in, gla_wg2, gla_bg, gla_norm_g, sgu_ln_g, sgu_ln_b, sgu_ws, sgu_bs, pool_w, pool_scale, w_up_a, w_up_b, w_up_c, w_o, ln1_g, ln1_b, xa_wq, xa_wk, xa_wv, xa_wo, ln2_g, ln2_b, router_w, router_b, exp_w_gu, exp_b_gu, exp_w_down, exp_b_down, ln3_g, ln3_b):
    B, S, D = x.shape
    N = B * S
    p = _prep(w_in, b_in, gla_wg2, gla_bg, gla_norm_g, sgu_ln_g, sgu_ln_b, sgu_ws, sgu_bs, pool_w, pool_scale,
              w_up_a, w_up_b, w_up_c, w_o, ln1_g, ln1_b, xa_wq, xa_wk, xa_wv, xa_wo, ln2_g, ln2_b,
              router_w, router_b, ln3_g, ln3_b)
    memt = jnp.swapaxes(mem, 1, 2)
    for l in range(DEPTH):
        x1 = _mixer(x, p, l)
        x2, x2p, route, counts = _xattn(x1, memt, mem, p, l)
        dest, step_expert, step_halves, n_used, next_expert = _route(route, counts)
        n_slots = N * TOP_K + N_EXPERTS * MOE_STEP_BLOCKS * MOE_BLOCK
        xs = _sc_scatter_rows(x2p.reshape(N, D // 2), dest, n_slots)
        ys = _experts(step_expert, step_halves, n_used, next_expert, xs, exp_w_gu, exp_b_gu, exp_w_down, exp_b_down, l)
        dest_km = dest.reshape(TOP_K, N // CMB_TILE, CMB_TILE).transpose(1, 0, 2).reshape(-1)
        tile_rows = TOP_K * CMB_TILE
        acc, first = None, 0
        for share in COMBINE_SHARES:
            tiles = (N // CMB_TILE) * share // sum(COMBINE_SHARES)
            yg = _sc_gather_rows(ys, dest_km[first * tile_rows:(first + tiles) * tile_rows])
            acc = _combine(x2.reshape(N, D), yg, route, p, l, first, acc)
            first += tiles
        x = acc.reshape(B, S, D)
    return x
```

```python
import functools

import jax
import jax.numpy as jnp
from jax import lax
from jax.experimental import pallas as pl
from jax.experimental.pallas import tpu as pltpu
from jax.experimental.pallas import tpu_sc as plsc

F32 = jnp.float32
BF16 = jnp.bfloat16

D_MODEL = 1024
DEPTH = 2
GLA_HEADS = 4
GLA_KEY = 256
GLA_VAL = 512
GLA_DK = 64
GLA_DV = 128
GLA_RANK = 16
GLA_TAU = 16.0
GLA_CHUNK = 64
SGU_GROUPS = 4
SGU_WIDTH = 256
SGU_GD = 64
SGU_CHUNK = 128
POOL_WINDOWS = (2, 4, 8, 16)
POOL_WIDTH = 256
POOL_GD = 64
POOL_CARRY = 32
MEM_LEN = 256
XA_HEADS = 4
XA_DH = 256
N_EXPERTS = 32
TOP_K = 4
EXPERT_FF = 1024
SWIGLU_LIMIT = 7.0
SWIGLU_ALPHA = 1.702
DEEPNORM_ALPHA = (2 * DEPTH) ** 0.25
LN_EPS = 1e-5
LANES = 128
VMEM_LIMIT = 56 * 1024 * 1024

MIX_TILE = 512
GLA_SUB = 256
XA_TILE = 1024
MOE_BLOCK = 256
MOE_STEP_BLOCKS = 4
CMB_TILE = 512
COMBINE_SHARES = (1, 1, 2, 2, 2)
ROUTE_ROWS = 16
SC_GATHER_WINDOW = 128

O_QKVR = 0
O_GLOW = 2 * GLA_KEY + 2 * GLA_VAL
O_UV = O_GLOW + LANES
O_XC = O_UV + 2 * SGU_WIDTH
O_GATE = O_XC + POOL_WIDTH


def _dot(a, b):
    return jnp.dot(a, b, preferred_element_type=F32)


def _dot_t0(a, b):
    return lax.dot_general(a, b, (((0,), (0,)), ((), ())), preferred_element_type=F32)


def _dot_t1(a, b):
    return lax.dot_general(a, b, (((1,), (1,)), ((), ())), preferred_element_type=F32)


def _split_bf16(x):
    hi = x.astype(BF16)
    lo = (x - hi.astype(F32)).astype(BF16)
    return hi, lo


def _layer_norm(x, g, b):
    mu = jnp.mean(x, axis=-1, keepdims=True)
    xc = x - mu
    var = jnp.mean(xc * xc, axis=-1, keepdims=True)
    return xc * lax.rsqrt(var + LN_EPS) * g + b


def _sigmoid(x):
    return 1.0 / (1.0 + jnp.exp(-x))


def _pack_bf16_pairs(x):
    H = x.shape[1] // 2
    bits = lax.bitcast_convert_type(x.astype(BF16).astype(F32), jnp.uint32)
    return (bits[:, :H] >> 16) | (bits[:, H:] & jnp.uint32(0xFFFF0000))


def _unpack_bf16_pairs(w):
    lo = lax.bitcast_convert_type(w << 16, F32)
    hi = lax.bitcast_convert_type(w & jnp.uint32(0xFFFF0000), F32)
    return jnp.concatenate([lo, hi], axis=1)


def _layer_spec(arr, l):
    nd = arr.ndim - 1
    return pl.BlockSpec((None,) + arr.shape[1:], lambda *_: (l,) + (0,) * nd, pipeline_mode=pl.Buffered(1))


def _mixer_kernel(x_ref, whead_ref, wglow_ref, wtail_ref, bcat_ref, wg2_ref, bg_ref, gnorm_ref,
                  slng_ref, slnb_ref, wtril_ref, sbias_ref, poolw_ref, pscale_ref,
                  wupa_ref, wupb_ref, wupc_ref, wo_ref, ln1g_ref, ln1b_ref,
                  out_ref,
                  state_ref, qkvr_ref, ya_ref, vln_ref, e_ref, s2_ref, s4_ref, s8_ref):
    T = MIX_TILE
    D = D_MODEL
    j = pl.program_id(1)
    x = x_ref[0]
    xb = x.astype(BF16)

    def proj(lo, hi):
        if hi <= O_GLOW:
            w = whead_ref[:, lo:hi]
        elif lo == O_GLOW and hi == O_UV:
            w = wglow_ref[...]
        else:
            w = wtail_ref[:, lo - O_UV:hi - O_UV]
        return _dot(xb, w) + bcat_ref[:, lo:hi]

    @pl.when(j == 0)
    def _():
        state_ref[...] = jnp.zeros_like(state_ref)
        e_ref[0:POOL_CARRY, :] = jnp.zeros((POOL_CARRY, POOL_WIDTH), F32)

    @pl.when(j > 0)
    def _():
        e_ref[0:POOL_CARRY, :] = e_ref[T:T + POOL_CARRY, :]

    qkvr_ref[...] = proj(O_QKVR, O_GLOW)
    glow = proj(O_GLOW, O_UV)
    z = _dot(glow.astype(BF16), wg2_ref[...]) + bg_ref[...]
    la = (jnp.minimum(z, 0.0) - jnp.log1p(jnp.exp(-jnp.abs(z)))) * (1.0 / GLA_TAU)
    la_hi, la_lo = _split_bf16(la)
    gate_a = _sigmoid(proj(O_GATE, O_GATE + D))

    C = GLA_CHUNK
    G = GLA_SUB
    NC = G // C
    CSH = C.bit_length() - 1
    row = lax.broadcasted_iota(jnp.int32, (G, G), 0)
    col = lax.broadcasted_iota(jnp.int32, (G, G), 1)
    same_chunk = (row >> CSH) == (col >> CSH)
    causal = same_chunk & (row >= col)
    causal_bf = jnp.where(causal, 1.0, 0.0).astype(BF16)
    lane = lax.broadcasted_iota(jnp.int32, (G, GLA_KEY), 1)
    gate_b = None
    for g0 in range(0, T, G):
        gr = slice(g0, g0 + G)
        lh, ll = la_hi[gr], la_lo[gr]
        b = _dot(causal_bf, lh) + _dot(causal_bf, ll)
        b_last = [b[(c + 1) * C - 1:(c + 1) * C, :] for c in range(NC)]
        b_end = jnp.concatenate([jnp.broadcast_to(r, (C, GLA_KEY)) for r in b_last], axis=0)
        q = qkvr_ref[gr, 0:GLA_KEY]
        k = qkvr_ref[gr, GLA_KEY:2 * GLA_KEY]
        v = qkvr_ref[gr, 2 * GLA_KEY:2 * GLA_KEY + GLA_VAL].astype(BF16)
        q_dec = q * (GLA_DK ** -0.5) * jnp.exp(b)
        k_dec = (k * jnp.exp(-b)).astype(BF16)
        k_tail = (k * jnp.exp(b_end - b)).astype(BF16)
        q_dec_bf = q_dec.astype(BF16)
        o_heads = []
        for h in range(GLA_HEADS):
            q_h = jnp.where((lane >= h * GLA_DK) & (lane < (h + 1) * GLA_DK), q_dec, 0.0).astype(BF16)
            scores = jnp.where(causal, _dot_t1(q_h, k_dec), 0.0).astype(BF16)
            o_heads.append(_dot(scores, v[:, h * GLA_DV:(h + 1) * GLA_DV]))
        o_intra = jnp.concatenate(o_heads, axis=1)
        if gate_b is None:
            gate_b = _sigmoid(proj(O_GATE + D, O_GATE + 2 * D))
        dec_cols = jnp.exp(jnp.concatenate(b_last + [jnp.zeros((8 - NC, GLA_KEY), F32)], axis=0)).T
        dec_all = jnp.concatenate([jnp.broadcast_to(dec_cols[:, c:c + 1], (GLA_KEY, LANES)) for c in range(NC)], axis=1)
        o_inter = []
        for c in range(NC):
            rows = slice(c * C, (c + 1) * C)
            o_inter.append(_dot(q_dec_bf[rows], state_ref[...].astype(BF16)))
            kv = _dot_t0(k_tail[rows], v[rows])
            for h in range(GLA_HEADS):
                rs = slice(h * GLA_DK, (h + 1) * GLA_DK)
                cs = slice(h * GLA_DV, (h + 1) * GLA_DV)
                state_ref[rs, cs] = dec_all[rs, c * LANES:(c + 1) * LANES] * state_ref[rs, cs] + kv[rs, cs]
        o = o_intra + jnp.concatenate(o_inter, axis=0)
        for h in range(GLA_HEADS):
            cs = slice(h * GLA_DV, (h + 1) * GLA_DV)
            o_h = o[:, cs]
            ms = jnp.mean(o_h * o_h, axis=-1, keepdims=True)
            o_h = o_h * lax.rsqrt(ms + LN_EPS) * gnorm_ref[:, cs]
            r_h = qkvr_ref[gr, 2 * GLA_KEY + GLA_VAL + h * GLA_DV:2 * GLA_KEY + GLA_VAL + (h + 1) * GLA_DV]
            ya_ref[gr, cs] = (o_h * (r_h * _sigmoid(r_h))).astype(BF16)

    gate_c = _sigmoid(proj(O_GATE + 2 * D, O_GATE + 3 * D))
    uv = proj(O_UV, O_XC)
    zg = 0.5 * uv * (1.0 + lax.erf(uv * (2.0 ** -0.5)))
    u = zg[:, :SGU_WIDTH]
    vln_ref[...] = _layer_norm(zg[:, SGU_WIDTH:], slng_ref[...], slnb_ref[...])
    lane_s = lax.broadcasted_iota(jnp.int32, (SGU_CHUNK, SGU_WIDTH), 1)
    s_parts = []
    for n in range(T // SGU_CHUNK):
        vc = vln_ref[n * SGU_CHUNK:(n + 1) * SGU_CHUNK, :]
        s = sbias_ref[...]
        for g in range(SGU_GROUPS):
            vg = jnp.where((lane_s >= g * SGU_GD) & (lane_s < (g + 1) * SGU_GD), vc, 0.0).astype(BF16)
            s = s + _dot(wtril_ref[g], vg)
        s_parts.append(s)
    y_b = (u * jnp.concatenate(s_parts, axis=0)).astype(BF16)

    P = POOL_CARRY
    xc = proj(O_XC, O_GATE)
    e_ref[P:P + T, :] = xc
    s2_ref[8:P + T, :] = e_ref[8:P + T, :] + e_ref[7:P + T - 1, :]
    s4_ref[16:P + T, :] = s2_ref[16:P + T, :] + s2_ref[14:P + T - 2, :]
    s8_ref[24:P + T, :] = s4_ref[24:P + T, :] + s4_ref[20:P + T - 4, :]
    s16 = s8_ref[P:P + T, :] + s8_ref[P - 8:P + T - 8, :]
    lane_p = lax.broadcasted_iota(jnp.int32, (T, POOL_WIDTH), 1)
    tpos = lax.broadcasted_iota(jnp.int32, (T, POOL_WIDTH), 0) + (j * T + 1)
    grp = lane_p >> (POOL_GD.bit_length() - 1)
    win = jnp.where(grp == 0, POOL_WINDOWS[0], jnp.where(grp == 1, POOL_WINDOWS[1],
                    jnp.where(grp == 2, POOL_WINDOWS[2], POOL_WINDOWS[3])))
    wsum = jnp.where(grp == 0, s2_ref[P:P + T, :], jnp.where(grp == 1, s4_ref[P:P + T, :],
                     jnp.where(grp == 2, s8_ref[P:P + T, :], s16)))
    count = jnp.minimum(tpos, win).astype(F32)
    pooled = wsum / count - xc
    y_c = (_dot(pooled.astype(BF16), poolw_ref[...]) * pscale_ref[...]).astype(BF16)

    merged = gate_a * _dot(ya_ref[...], wupa_ref[...].astype(BF16))
    merged += gate_b * _dot(y_b, wupb_ref[...].astype(BF16))
    merged += gate_c * _dot(y_c, wupc_ref[...].astype(BF16))
    h = _dot(merged.astype(BF16), wo_ref[...].astype(BF16))
    out_ref[0] = _layer_norm(DEEPNORM_ALPHA * x + h, ln1g_ref[...], ln1b_ref[...])


_MIXER_WEIGHTS = ('whead', 'wglow', 'wtail', 'bcat', 'wg2', 'bg', 'gnorm', 'slng', 'slnb', 'wtril', 'sbias', 'poolw', 'pscale',
                  'wupa', 'wupb', 'wupc', 'wo', 'ln1g', 'ln1b')


def _mixer(x, p, l):
    B, S, D = x.shape
    T = MIX_TILE
    weights = [p[n] for n in _MIXER_WEIGHTS]
    return pl.pallas_call(
        _mixer_kernel,
        out_shape=jax.ShapeDtypeStruct((B, S, D), F32),
        grid=(B, S // T),
        in_specs=[pl.BlockSpec((1, T, D), lambda b, j: (b, j, 0))] + [_layer_spec(w, l) for w in weights],
        out_specs=pl.BlockSpec((1, T, D), lambda b, j: (b, j, 0)),
        scratch_shapes=[
            pltpu.VMEM((GLA_KEY, GLA_VAL), F32),
            pltpu.VMEM((T, 2 * GLA_KEY + 2 * GLA_VAL), F32),
            pltpu.VMEM((T, GLA_VAL), BF16),
            pltpu.VMEM((T, SGU_WIDTH), F32),
            pltpu.VMEM((T + POOL_CARRY, POOL_WIDTH), F32),
            pltpu.VMEM((T + POOL_CARRY, POOL_WIDTH), F32),
            pltpu.VMEM((T + POOL_CARRY, POOL_WIDTH), F32),
            pltpu.VMEM((T + POOL_CARRY, POOL_WIDTH), F32),
        ],
        compiler_params=pltpu.CompilerParams(dimension_semantics=("parallel", "arbitrary"),
                                             vmem_limit_bytes=VMEM_LIMIT),
        name="mixer",
    )(x, *weights)


def _memkv_kernel(memt_ref, mem_ref, wk_ref, wv_ref, kt_ref, v_ref):
    kt_ref[0] = _dot_t0(wk_ref[...].astype(BF16), memt_ref[0].astype(BF16)).astype(BF16)
    v_ref[0] = _dot(mem_ref[0].astype(BF16), wv_ref[...].astype(BF16)).astype(BF16)


def _memkv(mem, memt, p, l):
    B, M, D = mem.shape
    return pl.pallas_call(
        _memkv_kernel,
        out_shape=(jax.ShapeDtypeStruct((B, D, M), BF16), jax.ShapeDtypeStruct((B, M, D), BF16)),
        grid=(B,),
        in_specs=[pl.BlockSpec((1, D, M), lambda b: (b, 0, 0)), pl.BlockSpec((1, M, D), lambda b: (b, 0, 0)),
                  _layer_spec(p['wk'], l), _layer_spec(p['wv'], l)],
        out_specs=(pl.BlockSpec((1, D, M), lambda b: (b, 0, 0)), pl.BlockSpec((1, M, D), lambda b: (b, 0, 0))),
        compiler_params=pltpu.CompilerParams(dimension_semantics=("arbitrary",), vmem_limit_bytes=VMEM_LIMIT),
        name="memkv",
    )(memt, mem, p['wk'], p['wv'])


def _xattn_kernel(x_ref, kt_ref, v_ref, wq_ref, wo_ref, ln2g_ref, ln2b_ref, rwt_ref, rbt_ref,
                  x2_ref, x2p_ref, route_ref, counts_ref, carry_ref):
    T = XA_TILE

    @pl.when((pl.program_id(0) == 0) & (pl.program_id(1) == 0))
    def _():
        carry_ref[...] = jnp.zeros_like(carry_ref)

    x = x_ref[0]
    q = (_dot(x.astype(BF16), wq_ref[...].astype(BF16)) * (XA_DH ** -0.5)).astype(BF16)
    h = jnp.zeros_like(x)
    for hd in range(XA_HEADS):
        cs = slice(hd * XA_DH, (hd + 1) * XA_DH)
        s = _dot(q[:, cs], kt_ref[0, cs, :])
        e = jnp.exp(s - jnp.max(s, axis=-1, keepdims=True))
        o = _dot(e.astype(BF16), v_ref[0, :, cs]) / jnp.sum(e, axis=-1, keepdims=True)
        h = h + _dot(o.astype(BF16), wo_ref[cs, :].astype(BF16))
    x2 = _layer_norm(DEEPNORM_ALPHA * x + h, ln2g_ref[...], ln2b_ref[...])
    x2_ref[0] = x2
    x2p_ref[0] = _pack_bf16_pairs(x2)

    E = N_EXPERTS
    hi, lo = _split_bf16(x2)
    lt = _dot_t1(rwt_ref[...], hi)
    logits = lt[0:E] + (lt[E:2 * E] + _dot_t1(rwt_ref[0:E, :], lo)) + rbt_ref[...]

    eid = lax.broadcasted_iota(jnp.int32, (E, T), 0)
    neg_inf = jnp.float32(-jnp.inf)
    rest = logits
    tops, picks = [], []
    for _ in range(TOP_K):
        m = jnp.max(rest, axis=0, keepdims=True)
        idx = jnp.min(jnp.where(rest == m, eid, E), axis=0, keepdims=True)
        pick = eid == idx
        rest = jnp.where(pick, neg_inf, rest)
        tops.append((m, idx))
        picks.append(pick)
    exps = [jnp.exp(m - tops[0][0]) for m, _ in tops]
    denom = exps[0]
    for e in exps[1:]:
        denom = denom + e

    chosen = jnp.zeros((E, T), F32)
    for pick in picks:
        chosen = chosen + jnp.where(pick, 1.0, 0.0)
    chosen_bf = chosen.astype(BF16)
    earlier = (lax.broadcasted_iota(jnp.int32, (T, T), 0) < lax.broadcasted_iota(jnp.int32, (T, T), 1))
    carry = carry_ref[...]
    before = _dot(chosen_bf, jnp.where(earlier, 1.0, 0.0).astype(BF16)) + jnp.concatenate([carry] * (T // LANES), axis=1)
    carry = carry + _dot(chosen_bf, jnp.ones((T, LANES), BF16))
    carry_ref[...] = carry
    counts_ref[...] = carry

    rid = lax.broadcasted_iota(jnp.int32, (ROUTE_ROWS, T), 0)
    route = jnp.zeros((ROUTE_ROWS, T), F32)
    for k in range(TOP_K):
        rank = jnp.sum(jnp.where(picks[k], before, 0.0), axis=0, keepdims=True)
        route = jnp.where(rid == k, tops[k][1].astype(F32), route)
        route = jnp.where(rid == TOP_K + k, exps[k] / denom, route)
        route = jnp.where(rid == 2 * TOP_K + k, rank, route)
    route_ref[...] = route


def _xattn(x, kt, v, p, l):
    B, S, D = x.shape
    T = XA_TILE
    M = MEM_LEN
    weights = [p[n] for n in ('wq', 'wxo', 'ln2g', 'ln2b', 'rwt', 'rbt')]
    return pl.pallas_call(
        _xattn_kernel,
        out_shape=(jax.ShapeDtypeStruct((B, S, D), F32), jax.ShapeDtypeStruct((B, S, D // 2), jnp.uint32),
                   jax.ShapeDtypeStruct((ROUTE_ROWS, B * S), F32), jax.ShapeDtypeStruct((N_EXPERTS, LANES), F32)),
        grid=(B, S // T),
        in_specs=[pl.BlockSpec((1, T, D), lambda b, j: (b, j, 0)),
                  pl.BlockSpec((1, D, M), lambda b, j: (b, 0, 0)),
                  pl.BlockSpec((1, M, D), lambda b, j: (b, 0, 0))] + [_layer_spec(w, l) for w in weights],
        out_specs=(pl.BlockSpec((1, T, D), lambda b, j: (b, j, 0)),
                   pl.BlockSpec((1, T, D // 2), lambda b, j: (b, j, 0)),
                   pl.BlockSpec((ROUTE_ROWS, T), lambda b, j: (0, b * (S // T) + j)),
                   pl.BlockSpec((N_EXPERTS, LANES), lambda b, j: (0, 0))),
        scratch_shapes=[pltpu.VMEM((N_EXPERTS, LANES), F32)],
        compiler_params=pltpu.CompilerParams(dimension_semantics=("arbitrary", "arbitrary"),
                                             vmem_limit_bytes=VMEM_LIMIT),
        name="xattn",
    )(x, kt, v, *weights)


def _expert_kernel(layer, se_ref, sh_ref, nu_ref, nxt_ref, xs_ref, wgu_hbm, bgu_ref, wd_hbm, bd_ref, out_ref,
                   wgu_st, wd_st, slot_ref, sem):
    i = pl.program_id(0)
    F = EXPERT_FF
    halves = sh_ref[i]

    def weight_copies(e, slot):
        return (pltpu.make_async_copy(wgu_hbm.at[layer, e], wgu_st.at[slot], sem.at[slot, 0]),
                pltpu.make_async_copy(wd_hbm.at[layer, e], wd_st.at[slot], sem.at[slot, 1]))

    def ffn(rows):
        slot = slot_ref[0]
        xsb = _unpack_bf16_pairs(xs_ref[rows, :]).astype(BF16)
        hh = _dot(xsb, wgu_st[slot].astype(BF16)) + bgu_ref[...]
        h_glu = jnp.minimum(hh[:, :F], SWIGLU_LIMIT)
        h_lin = jnp.clip(hh[:, F:], -SWIGLU_LIMIT, SWIGLU_LIMIT)
        a = h_glu * _sigmoid(SWIGLU_ALPHA * h_glu) * (h_lin + 1.0)
        out_ref[rows, :] = _pack_bf16_pairs(_dot(a.astype(BF16), wd_st[slot].astype(BF16)) + bd_ref[...])

    @pl.when(halves > 0)
    def _():
        e = se_ref[i]
        prev = se_ref[jnp.maximum(i - 1, 0)]

        @pl.when(i == 0)
        def _():
            slot_ref[0] = 1
            for cp in weight_copies(e, 0):
                cp.start()

        @pl.when((i == 0) | (e != prev))
        def _():
            slot = 1 - slot_ref[0]
            slot_ref[0] = slot
            for cp in weight_copies(e, slot):
                cp.wait()
            nxt = nxt_ref[e]

            @pl.when(nxt != e)
            def _():
                for cp in weight_copies(nxt, 1 - slot):
                    cp.start()

    for n in range(1, MOE_STEP_BLOCKS + 1):
        @pl.when(halves == n)
        def _(n=n):
            ffn(slice(0, n * MOE_BLOCK))


def _experts(step_expert, step_halves, n_used, next_expert, xs, w_gu, b_gu, w_down, b_down, l):
    P, DH = xs.shape
    D = 2 * DH
    R = MOE_STEP_BLOCKS * MOE_BLOCK
    NS = P // R
    F2 = 2 * EXPERT_FF

    def row_map(i, se, sh, nu, nxt):
        return (jnp.minimum(i, nu[0] - 1), 0)

    def exp_map(i, se, sh, nu, nxt):
        return (l, se[jnp.minimum(i, nu[0] - 1)], 0, 0)

    grid_spec = pltpu.PrefetchScalarGridSpec(
        num_scalar_prefetch=4,
        grid=(NS,),
        in_specs=[pl.BlockSpec((R, DH), row_map),
                  pl.BlockSpec(memory_space=pl.ANY),
                  pl.BlockSpec((None, None, 1, F2), exp_map),
                  pl.BlockSpec(memory_space=pl.ANY),
                  pl.BlockSpec((None, None, 1, D), exp_map)],
        out_specs=pl.BlockSpec((R, DH), row_map),
        scratch_shapes=[pltpu.VMEM((2, D, F2), F32), pltpu.VMEM((2, EXPERT_FF, D), F32),
                        pltpu.SMEM((1,), jnp.int32),
                        pltpu.SemaphoreType.DMA((2, 2))],
    )
    return pl.pallas_call(
        functools.partial(_expert_kernel, l),
        out_shape=jax.ShapeDtypeStruct((P, DH), jnp.uint32),
        grid_spec=grid_spec,
        compiler_params=pltpu.CompilerParams(dimension_semantics=("arbitrary",), vmem_limit_bytes=VMEM_LIMIT),
        name="experts",
    )(step_expert, step_halves, n_used, next_expert, xs, w_gu, b_gu.reshape(DEPTH, N_EXPERTS, 1, F2), w_down,
      b_down.reshape(DEPTH, N_EXPERTS, 1, D))


def _sc_gather_rows(x, idx):
    M = idx.shape[0]
    D = x.shape[1]
    W = SC_GATHER_WINDOW
    mesh = plsc.VectorSubcoreMesh(core_axis_name="core", subcore_axis_name="subcore")
    n_workers = mesh.num_cores * mesh.num_subcores
    rows_per = M // n_workers
    assert rows_per * n_workers == M and rows_per % W == 0

    @pl.kernel(out_type=jax.ShapeDtypeStruct((M, D), x.dtype), mesh=mesh, name="sc_gather_rows",
               scratch_types=[pltpu.VMEM((rows_per,), jnp.int32), pltpu.VMEM((W, D), x.dtype)])
    def gather_kernel(x_hbm, i_hbm, o_hbm, idx_vmem, buf):
        wid = lax.axis_index("core") * mesh.num_subcores + lax.axis_index("subcore")
        base = wid * rows_per
        pltpu.sync_copy(i_hbm.at[pl.ds(base, rows_per)], idx_vmem)

        @pl.loop(0, rows_per // W)
        def _(j):
            pltpu.sync_copy(x_hbm.at[idx_vmem.at[pl.ds(j * W, W)]], buf)
            pltpu.sync_copy(buf, o_hbm.at[pl.ds(base + j * W, W)])

    return gather_kernel(x, idx)


def _sc_scatter_rows(x, idx, n_out):
    K, N = idx.shape
    D = x.shape[1]
    W = SC_GATHER_WINDOW
    mesh = plsc.VectorSubcoreMesh(core_axis_name="core", subcore_axis_name="subcore")
    n_workers = mesh.num_cores * mesh.num_subcores
    rows_per = N // n_workers
    assert rows_per * n_workers == N and rows_per % W == 0

    @pl.kernel(out_type=jax.ShapeDtypeStruct((n_out, D), x.dtype), mesh=mesh, name="sc_scatter_rows",
               scratch_types=[pltpu.VMEM((K * rows_per,), jnp.int32), pltpu.VMEM((W, D), x.dtype),
                              pltpu.SemaphoreType.DMA((K,))])
    def scatter_kernel(x_hbm, i_hbm, o_hbm, idx_vmem, buf, sem):
        wid = lax.axis_index("core") * mesh.num_subcores + lax.axis_index("subcore")
        base = wid * rows_per
        for k in range(K):
            pltpu.sync_copy(i_hbm.at[pl.ds(k * N + base, rows_per)], idx_vmem.at[pl.ds(k * rows_per, rows_per)])

        @pl.loop(0, rows_per // W)
        def _(j):
            pltpu.sync_copy(x_hbm.at[pl.ds(base + j * W, W)], buf)
            copies = [pltpu.make_async_copy(buf, o_hbm.at[idx_vmem.at[pl.ds(k * rows_per + j * W, W)]], sem.at[k])
                      for k in range(K)]
            for cp in copies:
                cp.start()
            for cp in copies:
                cp.wait()

    return scatter_kernel(x, idx.reshape(K * N))


def _combine_kernel(x_ref, yg_ref, route_ref, ln3g_ref, ln3b_ref, *rest):
    out_ref = rest[-1]
    x = x_ref[...]
    g = jnp.transpose(route_ref[...])
    y = jnp.zeros_like(x)
    for k in range(TOP_K):
        y = y + g[:, TOP_K + k:TOP_K + k + 1] * _unpack_bf16_pairs(yg_ref[k * CMB_TILE:(k + 1) * CMB_TILE, :])
    out_ref[...] = _layer_norm(DEEPNORM_ALPHA * x + y, ln3g_ref[...], ln3b_ref[...])


def _combine(x2, yg, route, p, l, first, acc):
    N, D = x2.shape
    T = CMB_TILE
    tiles = yg.shape[0] // (TOP_K * T)
    in_specs = [pl.BlockSpec((T, D), lambda i: (first + i, 0)), pl.BlockSpec((TOP_K * T, D // 2), lambda i: (i, 0)),
                pl.BlockSpec((ROUTE_ROWS, T), lambda i: (0, first + i)), _layer_spec(p['ln3g'], l), _layer_spec(p['ln3b'], l)]
    args = [x2, yg, route, p['ln3g'], p['ln3b']]
    aliases = {}
    if acc is not None:
        in_specs.append(pl.BlockSpec(memory_space=pl.ANY))
        args.append(acc)
        aliases = {len(args) - 1: 0}
    return pl.pallas_call(
        _combine_kernel,
        out_shape=jax.ShapeDtypeStruct((N, D), F32),
        grid=(tiles,),
        in_specs=in_specs,
        out_specs=pl.BlockSpec((T, D), lambda i: (first + i, 0)),
        input_output_aliases=aliases,
        compiler_params=pltpu.CompilerParams(dimension_semantics=("arbitrary",), vmem_limit_bytes=VMEM_LIMIT),
        name="combine",
    )(*args)


def _prep(w_in, b_in, gla_wg2, gla_bg, gla_norm_g, sgu_ln_g, sgu_ln_b, sgu_ws, sgu_bs, pool_w, pool_scale,
          w_up_a, w_up_b, w_up_c, w_o, ln1_g, ln1_b, xa_wq, xa_wk, xa_wv, xa_wo, ln2_g, ln2_b,
          router_w, router_b, ln3_g, ln3_b):
    L = w_in.shape[0]
    o_glow = O_GLOW
    o_uv = o_glow + GLA_RANK
    row = lambda a: a.reshape(L, 1, -1).astype(F32)
    pad_last = lambda a, n: jnp.pad(a, [(0, 0)] * (a.ndim - 1) + [(0, n - a.shape[-1])])
    p = {}
    p['whead'] = w_in[..., :o_glow].astype(BF16)
    p['wglow'] = pad_last(w_in[..., o_glow:o_uv], LANES).astype(BF16)
    p['wtail'] = w_in[..., o_uv:].astype(BF16)
    p['bcat'] = row(jnp.concatenate([b_in[..., :o_glow], pad_last(b_in[..., o_glow:o_uv], LANES), b_in[..., o_uv:]],
                                    axis=-1))
    p['wg2'] = jnp.pad(gla_wg2, ((0, 0), (0, LANES - GLA_RANK), (0, 0))).astype(BF16)
    p['bg'] = row(gla_bg)
    p['gnorm'] = row(gla_norm_g)
    p['slng'] = row(sgu_ln_g)
    p['slnb'] = row(sgu_ln_b)
    p['wtril'] = jnp.tril(sgu_ws).astype(BF16)
    p['sbias'] = jnp.repeat(jnp.swapaxes(sgu_bs, 1, 2), SGU_GD, axis=2).astype(F32)
    G = len(POOL_WINDOWS)
    eye = jnp.eye(G, dtype=F32)
    p['poolw'] = jnp.einsum('lgcd,gh->lgchd', pool_w, eye).reshape(L, POOL_WIDTH, POOL_WIDTH).astype(BF16)
    p['pscale'] = row(pool_scale)
    p['wupa'], p['wupb'], p['wupc'], p['wo'] = w_up_a, w_up_b, w_up_c, w_o
    p['ln1g'], p['ln1b'] = row(ln1_g), row(ln1_b)
    p['wq'], p['wk'], p['wv'], p['wxo'] = xa_wq, xa_wk, xa_wv, xa_wo
    p['ln2g'], p['ln2b'] = row(ln2_g), row(ln2_b)
    rwt = jnp.swapaxes(router_w, 1, 2)
    rwt_hi = rwt.astype(BF16)
    p['rwt'] = jnp.concatenate([rwt_hi, (rwt - rwt_hi.astype(F32)).astype(BF16)], axis=1)
    p['rbt'] = jnp.broadcast_to(router_b[:, :, None], router_b.shape + (XA_TILE,)).astype(F32)
    p['ln3g'], p['ln3b'] = row(ln3_g), row(ln3_b)
    return p


def _route(route, counts):
    N = route.shape[1]
    top_idx = route[0:TOP_K].astype(jnp.int32)
    rank = route[2 * TOP_K:3 * TOP_K].astype(jnp.int32)
    counts = counts[:, 0].astype(jnp.int32)
    R = MOE_STEP_BLOCKS * MOE_BLOCK
    blocks = (counts + MOE_BLOCK - 1) // MOE_BLOCK
    padded = ((counts + R - 1) // R) * R
    pad_end = jnp.cumsum(padded)
    pad_start = pad_end - padded
    ids = jnp.arange(N_EXPERTS, dtype=jnp.int32)
    start_of = jnp.sum(jnp.where(top_idx[:, :, None] == ids[None, None, :], pad_start[None, None, :], 0), axis=-1)
    dest = start_of + rank
    n_steps = N * TOP_K // R + N_EXPERTS
    step_start = jnp.arange(n_steps, dtype=jnp.int32) * R
    step_expert = jnp.minimum(jnp.sum((pad_end[None, :] <= step_start[:, None]).astype(jnp.int32), axis=1),
                              N_EXPERTS - 1)
    mine = step_expert[:, None] == ids[None, :]
    blocks_before = (step_start - jnp.sum(jnp.where(mine, pad_start[None, :], 0), axis=1)) // MOE_BLOCK
    step_halves = jnp.clip(jnp.sum(jnp.where(mine, blocks[None, :], 0), axis=1) - blocks_before, 0, MOE_STEP_BLOCKS)
    step_halves = jnp.where(step_start < pad_end[-1], step_halves, 0).astype(jnp.int32)
    n_used = (pad_end[-1] // R).astype(jnp.int32).reshape(1)
    later = jnp.where((ids[None, :] > ids[:, None]) & (counts[None, :] > 0), ids[None, :], N_EXPERTS)
    first_later = jnp.min(later, axis=1)
    next_expert = jnp.where(first_later < N_EXPERTS, first_later, ids).astype(jnp.int32)
    return dest, step_expert, step_halves, n_used, next_expert


def kernel(x, mem, w_in, b_in, gla_wg2, gla_bg, gla_norm_g, sgu_ln_g, sgu_ln_b, sgu_ws, sgu_bs, pool_w, pool_scale, w_up_a, w_up_b, w_up_c, w_o, ln1_g, ln1_b, xa_wq, xa_wk, xa_wv, xa_wo, ln2_g, ln2_b, router_w, router_b, exp_w_gu, exp_b_gu, exp_w_down, exp_b_down, ln3_g, ln3_b):
    B, S, D = x.shape
    N = B * S
    p = _prep(w_in, b_in, gla_wg2, gla_bg, gla_norm_g, sgu_ln_g, sgu_ln_b, sgu_ws, sgu_bs, pool_w, pool_scale,
              w_up_a, w_up_b, w_up_c, w_o, ln1_g, ln1_b, xa_wq, xa_wk, xa_wv, xa_wo, ln2_g, ln2_b,
              router_w, router_b, ln3_g, ln3_b)
    memt = jnp.swapaxes(mem, 1, 2)
    for l in range(DEPTH):
        x1 = _mixer(x, p, l)
        kt, v = _memkv(mem, memt, p, l)
        x2, x2p, route, counts = _xattn(x1, kt, v, p, l)
        dest, step_expert, step_halves, n_used, next_expert = _route(route, counts)
        n_slots = N * TOP_K + N_EXPERTS * MOE_STEP_BLOCKS * MOE_BLOCK
        xs = _sc_scatter_rows(x2p.reshape(N, D // 2), dest, n_slots)
        ys = _experts(step_expert, step_halves, n_used, next_expert, xs, exp_w_gu, exp_b_gu, exp_w_down, exp_b_down, l)
        dest_km = dest.reshape(TOP_K, N // CMB_TILE, CMB_TILE).transpose(1, 0, 2).reshape(-1)
        tile_rows = TOP_K * CMB_TILE
        acc, first = None, 0
        for share in COMBINE_SHARES:
            tiles = (N // CMB_TILE) * share // sum(COMBINE_SHARES)
            yg = _sc_gather_rows(ys, dest_km[first * tile_rows:(first + tiles) * tile_rows])
            acc = _combine(x2.reshape(N, D), yg, route, p, l, first, acc)
            first += tiles
        x = acc.reshape(B, S, D)
    return x
```

```python
import functools

import jax
import jax.numpy as jnp
from jax import lax
from jax.experimental import pallas as pl
from jax.experimental.pallas import tpu as pltpu
from jax.experimental.pallas import tpu_sc as plsc

F32 = jnp.float32
BF16 = jnp.bfloat16

D_MODEL = 1024
DEPTH = 2
GLA_HEADS = 4
GLA_KEY = 256
GLA_VAL = 512
GLA_DK = 64
GLA_DV = 128
GLA_RANK = 16
GLA_TAU = 16.0
GLA_CHUNK = 64
SGU_GROUPS = 4
SGU_WIDTH = 256
SGU_GD = 64
SGU_CHUNK = 128
POOL_WINDOWS = (2, 4, 8, 16)
POOL_WIDTH = 256
POOL_GD = 64
POOL_CARRY = 32
MEM_LEN = 256
XA_HEADS = 4
XA_DH = 256
N_EXPERTS = 32
TOP_K = 4
EXPERT_FF = 1024
SWIGLU_LIMIT = 7.0
SWIGLU_ALPHA = 1.702
DEEPNORM_ALPHA = (2 * DEPTH) ** 0.25
LN_EPS = 1e-5
LANES = 128
VMEM_LIMIT = 56 * 1024 * 1024

MIX_TILE = 512
GLA_SUB = 256
XA_TILE = 1024
MOE_BLOCK = 256
MOE_STEP_BLOCKS = 4
CMB_TILE = 512
COMBINE_SHARES = (1, 1, 2, 2, 2)
ROUTE_ROWS = 16
SC_GATHER_WINDOW = 128

O_QKVR = 0
O_GLOW = 2 * GLA_KEY + 2 * GLA_VAL
O_UV = O_GLOW + LANES
O_XC = O_UV + 2 * SGU_WIDTH
O_GATE = O_XC + POOL_WIDTH


def _dot(a, b):
    return jnp.dot(a, b, preferred_element_type=F32)


def _dot_t0(a, b):
    return lax.dot_general(a, b, (((0,), (0,)), ((), ())), preferred_element_type=F32)


def _dot_t1(a, b):
    return lax.dot_general(a, b, (((1,), (1,)), ((), ())), preferred_element_type=F32)


def _split_bf16(x):
    hi = x.astype(BF16)
    lo = (x - hi.astype(F32)).astype(BF16)
    return hi, lo


def _layer_norm(x, g, b):
    mu = jnp.mean(x, axis=-1, keepdims=True)
    xc = x - mu
    var = jnp.mean(xc * xc, axis=-1, keepdims=True)
    return xc * lax.rsqrt(var + LN_EPS) * g + b


def _sigmoid(x):
    return 1.0 / (1.0 + jnp.exp(-x))


def _pack_bf16_pairs(x):
    H = x.shape[1] // 2
    bits = lax.bitcast_convert_type(x.astype(BF16).astype(F32), jnp.uint32)
    return (bits[:, :H] >> 16) | (bits[:, H:] & jnp.uint32(0xFFFF0000))


def _unpack_bf16_pairs(w):
    lo = lax.bitcast_convert_type(w << 16, F32)
    hi = lax.bitcast_convert_type(w & jnp.uint32(0xFFFF0000), F32)
    return jnp.concatenate([lo, hi], axis=1)


def _layer_spec(arr, l):
    nd = arr.ndim - 1
    return pl.BlockSpec((None,) + arr.shape[1:], lambda *_: (l,) + (0,) * nd, pipeline_mode=pl.Buffered(1))


def _mixer_kernel(x_ref, whead_ref, wglow_ref, wtail_ref, bcat_ref, wg2_ref, bg_ref, gnorm_ref,
                  slng_ref, slnb_ref, wtril_ref, sbias_ref, poolw_ref, pscale_ref,
                  wupa_ref, wupb_ref, wupc_ref, wo_ref, ln1g_ref, ln1b_ref,
                  out_ref,
                  state_ref, qkvr_ref, ya_ref, vln_ref, e_ref, s2_ref, s4_ref, s8_ref):
    T = MIX_TILE
    D = D_MODEL
    j = pl.program_id(1)
    x = x_ref[0]
    xb = x.astype(BF16)

    def proj(lo, hi):
        if hi <= O_GLOW:
            w = whead_ref[:, lo:hi]
        elif lo == O_GLOW and hi == O_UV:
            w = wglow_ref[...]
        else:
            w = wtail_ref[:, lo - O_UV:hi - O_UV]
        return _dot(xb, w) + bcat_ref[:, lo:hi]

    @pl.when(j == 0)
    def _():
        state_ref[...] = jnp.zeros_like(state_ref)
        e_ref[0:POOL_CARRY, :] = jnp.zeros((POOL_CARRY, POOL_WIDTH), F32)

    @pl.when(j > 0)
    def _():
        e_ref[0:POOL_CARRY, :] = e_ref[T:T + POOL_CARRY, :]

    qkvr_ref[...] = proj(O_QKVR, O_GLOW)
    glow = proj(O_GLOW, O_UV)
    z = _dot(glow.astype(BF16), wg2_ref[...]) + bg_ref[...]
    la = (jnp.minimum(z, 0.0) - jnp.log1p(jnp.exp(-jnp.abs(z)))) * (1.0 / GLA_TAU)
    la_hi, la_lo = _split_bf16(la)
    gate_a = _sigmoid(proj(O_GATE, O_GATE + D))

    C = GLA_CHUNK
    G = GLA_SUB
    NC = G // C
    CSH = C.bit_length() - 1
    row = lax.broadcasted_iota(jnp.int32, (G, G), 0)
    col = lax.broadcasted_iota(jnp.int32, (G, G), 1)
    same_chunk = (row >> CSH) == (col >> CSH)
    causal = same_chunk & (row >= col)
    causal_bf = jnp.where(causal, 1.0, 0.0).astype(BF16)
    lane = lax.broadcasted_iota(jnp.int32, (G, GLA_KEY), 1)
    gate_b = None
    for g0 in range(0, T, G):
        gr = slice(g0, g0 + G)
        lh, ll = la_hi[gr], la_lo[gr]
        b = _dot(causal_bf, lh) + _dot(causal_bf, ll)
        b_last = [b[(c + 1) * C - 1:(c + 1) * C, :] for c in range(NC)]
        b_end = jnp.concatenate([jnp.broadcast_to(r, (C, GLA_KEY)) for r in b_last], axis=0)
        q = qkvr_ref[gr, 0:GLA_KEY]
        k = qkvr_ref[gr, GLA_KEY:2 * GLA_KEY]
        v = qkvr_ref[gr, 2 * GLA_KEY:2 * GLA_KEY + GLA_VAL].astype(BF16)
        q_dec = q * (GLA_DK ** -0.5) * jnp.exp(b)
        k_dec = (k * jnp.exp(-b)).astype(BF16)
        k_tail = (k * jnp.exp(b_end - b)).astype(BF16)
        q_dec_bf = q_dec.astype(BF16)
        o_heads = []
        for h in range(GLA_HEADS):
            q_h = jnp.where((lane >= h * GLA_DK) & (lane < (h + 1) * GLA_DK), q_dec, 0.0).astype(BF16)
            scores = jnp.where(causal, _dot_t1(q_h, k_dec), 0.0).astype(BF16)
            o_heads.append(_dot(scores, v[:, h * GLA_DV:(h + 1) * GLA_DV]))
        o_intra = jnp.concatenate(o_heads, axis=1)
        if gate_b is None:
            gate_b = _sigmoid(proj(O_GATE + D, O_GATE + 2 * D))
        dec_cols = jnp.exp(jnp.concatenate(b_last + [jnp.zeros((8 - NC, GLA_KEY), F32)], axis=0)).T
        dec_all = jnp.concatenate([jnp.broadcast_to(dec_cols[:, c:c + 1], (GLA_KEY, LANES)) for c in range(NC)], axis=1)
        o_inter = []
        for c in range(NC):
            rows = slice(c * C, (c + 1) * C)
            o_inter.append(_dot(q_dec_bf[rows], state_ref[...].astype(BF16)))
            kv = _dot_t0(k_tail[rows], v[rows])
            for h in range(GLA_HEADS):
                rs = slice(h * GLA_DK, (h + 1) * GLA_DK)
                cs = slice(h * GLA_DV, (h + 1) * GLA_DV)
                state_ref[rs, cs] = dec_all[rs, c * LANES:(c + 1) * LANES] * state_ref[rs, cs] + kv[rs, cs]
        o = o_intra + jnp.concatenate(o_inter, axis=0)
        for h in range(GLA_HEADS):
            cs = slice(h * GLA_DV, (h + 1) * GLA_DV)
            o_h = o[:, cs]
            ms = jnp.mean(o_h * o_h, axis=-1, keepdims=True)
            o_h = o_h * lax.rsqrt(ms + LN_EPS) * gnorm_ref[:, cs]
            r_h = qkvr_ref[gr, 2 * GLA_KEY + GLA_VAL + h * GLA_DV:2 * GLA_KEY + GLA_VAL + (h + 1) * GLA_DV]
            ya_ref[gr, cs] = (o_h * (r_h * _sigmoid(r_h))).astype(BF16)

    gate_c = _sigmoid(proj(O_GATE + 2 * D, O_GATE + 3 * D))
    uv = proj(O_UV, O_XC)
    zg = 0.5 * uv * (1.0 + lax.erf(uv * (2.0 ** -0.5)))
    u = zg[:, :SGU_WIDTH]
    vln_ref[...] = _layer_norm(zg[:, SGU_WIDTH:], slng_ref[...], slnb_ref[...])
    lane_s = lax.broadcasted_iota(jnp.int32, (SGU_CHUNK, SGU_WIDTH), 1)
    s_parts = []
    for n in range(T // SGU_CHUNK):
        vc = vln_ref[n * SGU_CHUNK:(n + 1) * SGU_CHUNK, :]
        s = sbias_ref[...]
        for g in range(SGU_GROUPS):
            vg = jnp.where((lane_s >= g * SGU_GD) & (lane_s < (g + 1) * SGU_GD), vc, 0.0).astype(BF16)
            s = s + _dot(wtril_ref[g], vg)
        s_parts.append(s)
    y_b = (u * jnp.concatenate(s_parts, axis=0)).astype(BF16)

    P = POOL_CARRY
    xc = proj(O_XC, O_GATE)
    e_ref[P:P + T, :] = xc
    s2_ref[8:P + T, :] = e_ref[8:P + T, :] + e_ref[7:P + T - 1, :]
    s4_ref[16:P + T, :] = s2_ref[16:P + T, :] + s2_ref[14:P + T - 2, :]
    s8_ref[24:P + T, :] = s4_ref[24:P + T, :] + s4_ref[20:P + T - 4, :]
    s16 = s8_ref[P:P + T, :] + s8_ref[P - 8:P + T - 8, :]
    lane_p = lax.broadcasted_iota(jnp.int32, (T, POOL_WIDTH), 1)
    tpos = lax.broadcasted_iota(jnp.int32, (T, POOL_WIDTH), 0) + (j * T + 1)
    grp = lane_p >> (POOL_GD.bit_length() - 1)
    win = jnp.where(grp == 0, POOL_WINDOWS[0], jnp.where(grp == 1, POOL_WINDOWS[1],
                    jnp.where(grp == 2, POOL_WINDOWS[2], POOL_WINDOWS[3])))
    wsum = jnp.where(grp == 0, s2_ref[P:P + T, :], jnp.where(grp == 1, s4_ref[P:P + T, :],
                     jnp.where(grp == 2, s8_ref[P:P + T, :], s16)))
    count = jnp.minimum(tpos, win).astype(F32)
    pooled = wsum / count - xc
    y_c = (_dot(pooled.astype(BF16), poolw_ref[...]) * pscale_ref[...]).astype(BF16)

    merged = gate_a * _dot(ya_ref[...], wupa_ref[...].astype(BF16))
    merged += gate_b * _dot(y_b, wupb_ref[...].astype(BF16))
    merged += gate_c * _dot(y_c, wupc_ref[...].astype(BF16))
    h = _dot(merged.astype(BF16), wo_ref[...].astype(BF16))
    out_ref[0] = _layer_norm(DEEPNORM_ALPHA * x + h, ln1g_ref[...], ln1b_ref[...])


_MIXER_WEIGHTS = ('whead', 'wglow', 'wtail', 'bcat', 'wg2', 'bg', 'gnorm', 'slng', 'slnb', 'wtril', 'sbias', 'poolw', 'pscale',
                  'wupa', 'wupb', 'wupc', 'wo', 'ln1g', 'ln1b')


def _mixer(x, p, l):
    B, S, D = x.shape
    T = MIX_TILE
    weights = [p[n] for n in _MIXER_WEIGHTS]
    return pl.pallas_call(
        _mixer_kernel,
        out_shape=jax.ShapeDtypeStruct((B, S, D), F32),
        grid=(B, S // T),
        in_specs=[pl.BlockSpec((1, T, D), lambda b, j: (b, j, 0))] + [_layer_spec(w, l) for w in weights],
        out_specs=pl.BlockSpec((1, T, D), lambda b, j: (b, j, 0)),
        scratch_shapes=[
            pltpu.VMEM((GLA_KEY, GLA_VAL), F32),
            pltpu.VMEM((T, 2 * GLA_KEY + 2 * GLA_VAL), F32),
            pltpu.VMEM((T, GLA_VAL), BF16),
            pltpu.VMEM((T, SGU_WIDTH), F32),
            pltpu.VMEM((T + POOL_CARRY, POOL_WIDTH), F32),
            pltpu.VMEM((T + POOL_CARRY, POOL_WIDTH), F32),
            pltpu.VMEM((T + POOL_CARRY, POOL_WIDTH), F32),
            pltpu.VMEM((T + POOL_CARRY, POOL_WIDTH), F32),
        ],
        compiler_params=pltpu.CompilerParams(dimension_semantics=("parallel", "arbitrary"),
                                             vmem_limit_bytes=VMEM_LIMIT),
        name="mixer",
    )(x, *weights)


def _memkv_kernel(memt_ref, mem_ref, wk_ref, wv_ref, kt_ref, v_ref):
    kt_ref[0] = _dot_t0(wk_ref[...].astype(BF16), memt_ref[0].astype(BF16)).astype(BF16)
    v_ref[0] = _dot(mem_ref[0].astype(BF16), wv_ref[...].astype(BF16)).astype(BF16)


def _memkv(mem, memt, p, l):
    B, M, D = mem.shape
    return pl.pallas_call(
        _memkv_kernel,
        out_shape=(jax.ShapeDtypeStruct((B, D, M), BF16), jax.ShapeDtypeStruct((B, M, D), BF16)),
        grid=(B,),
        in_specs=[pl.BlockSpec((1, D, M), lambda b: (b, 0, 0)), pl.BlockSpec((1, M, D), lambda b: (b, 0, 0)),
                  _layer_spec(p['wk'], l), _layer_spec(p['wv'], l)],
        out_specs=(pl.BlockSpec((1, D, M), lambda b: (b, 0, 0)), pl.BlockSpec((1, M, D), lambda b: (b, 0, 0))),
        compiler_params=pltpu.CompilerParams(dimension_semantics=("arbitrary",), vmem_limit_bytes=VMEM_LIMIT),
        name="memkv",
    )(memt, mem, p['wk'], p['wv'])


def _xattn_kernel(x_ref, kt_ref, v_ref, wq_ref, wo_ref, ln2g_ref, ln2b_ref, rwt_ref, rbt_ref,
                  x2_ref, x2p_ref, route_ref, counts_ref, carry_ref):
    T = XA_TILE

    @pl.when((pl.program_id(0) == 0) & (pl.program_id(1) == 0))
    def _():
        carry_ref[...] = jnp.zeros_like(carry_ref)

    x = x_ref[0]
    q = (_dot(x.astype(BF16), wq_ref[...].astype(BF16)) * (XA_DH ** -0.5)).astype(BF16)
    h = jnp.zeros_like(x)
    for hd in range(XA_HEADS):
        cs = slice(hd * XA_DH, (hd + 1) * XA_DH)
        s = _dot(q[:, cs], kt_ref[0, cs, :])
        e = jnp.exp(s - jnp.max(s, axis=-1, keepdims=True))
        o = _dot(e.astype(BF16), v_ref[0, :, cs]) / jnp.sum(e, axis=-1, keepdims=True)
        h = h + _dot(o.astype(BF16), wo_ref[cs, :].astype(BF16))
    x2 = _layer_norm(DEEPNORM_ALPHA * x + h, ln2g_ref[...], ln2b_ref[...])
    x2_ref[0] = x2
    x2p_ref[0] = _pack_bf16_pairs(x2)

    E = N_EXPERTS
    hi, lo = _split_bf16(x2)
    lt = _dot_t1(rwt_ref[...], hi)
    logits = lt[0:E] + (lt[E:2 * E] + _dot_t1(rwt_ref[0:E, :], lo)) + rbt_ref[...]

    eid = lax.broadcasted_iota(jnp.int32, (E, T), 0)
    neg_inf = jnp.float32(-jnp.inf)
    rest = logits
    tops, picks = [], []
    for _ in range(TOP_K):
        m = jnp.max(rest, axis=0, keepdims=True)
        idx = jnp.min(jnp.where(rest == m, eid, E), axis=0, keepdims=True)
        pick = eid == idx
        rest = jnp.where(pick, neg_inf, rest)
        tops.append((m, idx))
        picks.append(pick)
    exps = [jnp.exp(m - tops[0][0]) for m, _ in tops]
    denom = exps[0]
    for e in exps[1:]:
        denom = denom + e

    chosen = jnp.zeros((E, T), F32)
    for pick in picks:
        chosen = chosen + jnp.where(pick, 1.0, 0.0)
    chosen_bf = chosen.astype(BF16)
    earlier = (lax.broadcasted_iota(jnp.int32, (T, T), 0) < lax.broadcasted_iota(jnp.int32, (T, T), 1))
    carry = carry_ref[...]
    before = _dot(chosen_bf, jnp.where(earlier, 1.0, 0.0).astype(BF16)) + jnp.concatenate([carry] * (T // LANES), axis=1)
    carry = carry + _dot(chosen_bf, jnp.ones((T, LANES), BF16))
    carry_ref[...] = carry
    counts_ref[...] = carry

    rid = lax.broadcasted_iota(jnp.int32, (ROUTE_ROWS, T), 0)
    route = jnp.zeros((ROUTE_ROWS, T), F32)
    for k in range(TOP_K):
        rank = jnp.sum(jnp.where(picks[k], before, 0.0), axis=0, keepdims=True)
        route = jnp.where(rid == k, tops[k][1].astype(F32), route)
        route = jnp.where(rid == TOP_K + k, exps[k] / denom, route)
        route = jnp.where(rid == 2 * TOP_K + k, rank, route)
    route_ref[...] = route


def _xattn(x, kt, v, p, l):
    B, S, D = x.shape
    T = XA_TILE
    M = MEM_LEN
    weights = [p[n] for n in ('wq', 'wxo', 'ln2g', 'ln2b', 'rwt', 'rbt')]
    return pl.pallas_call(
        _xattn_kernel,
        out_shape=(jax.ShapeDtypeStruct((B, S, D), F32), jax.ShapeDtypeStruct((B, S, D // 2), jnp.uint32),
                   jax.ShapeDtypeStruct((ROUTE_ROWS, B * S), F32), jax.ShapeDtypeStruct((N_EXPERTS, LANES), F32)),
        grid=(B, S // T),
        in_specs=[pl.BlockSpec((1, T, D), lambda b, j: (b, j, 0)),
                  pl.BlockSpec((1, D, M), lambda b, j: (b, 0, 0)),
                  pl.BlockSpec((1, M, D), lambda b, j: (b, 0, 0))] + [_layer_spec(w, l) for w in weights],
        out_specs=(pl.BlockSpec((1, T, D), lambda b, j: (b, j, 0)),
                   pl.BlockSpec((1, T, D // 2), lambda b, j: (b, j, 0)),
                   pl.BlockSpec((ROUTE_ROWS, T), lambda b, j: (0, b * (S // T) + j)),
                   pl.BlockSpec((N_EXPERTS, LANES), lambda b, j: (0, 0))),
        scratch_shapes=[pltpu.VMEM((N_EXPERTS, LANES), F32)],
        compiler_params=pltpu.CompilerParams(dimension_semantics=("arbitrary", "arbitrary"),
                                             vmem_limit_bytes=VMEM_LIMIT),
        name="xattn",
    )(x, kt, v, *weights)


def _expert_kernel(layer, se_ref, sh_ref, nu_ref, nxt_ref, xs_ref, wgu_hbm, bgu_ref, wd_hbm, bd_ref, out_ref,
                   wgu_st, wd_st, slot_ref, sem):
    i = pl.program_id(0)
    F = EXPERT_FF
    halves = sh_ref[i]

    def weight_copies(e, slot):
        return (pltpu.make_async_copy(wgu_hbm.at[layer, e], wgu_st.at[slot], sem.at[slot, 0]),
                pltpu.make_async_copy(wd_hbm.at[layer, e], wd_st.at[slot], sem.at[slot, 1]))

    def ffn(rows):
        slot = slot_ref[0]
        xsb = _unpack_bf16_pairs(xs_ref[rows, :]).astype(BF16)
        hh = _dot(xsb, wgu_st[slot].astype(BF16)) + bgu_ref[...]
        h_glu = jnp.minimum(hh[:, :F], SWIGLU_LIMIT)
        h_lin = jnp.clip(hh[:, F:], -SWIGLU_LIMIT, SWIGLU_LIMIT)
        a = h_glu * _sigmoid(SWIGLU_ALPHA * h_glu) * (h_lin + 1.0)
        out_ref[rows, :] = _pack_bf16_pairs(_dot(a.astype(BF16), wd_st[slot].astype(BF16)) + bd_ref[...])

    @pl.when(halves > 0)
    def _():
        e = se_ref[i]
        prev = se_ref[jnp.maximum(i - 1, 0)]

        @pl.when(i == 0)
        def _():
            slot_ref[0] = 1
            for cp in weight_copies(e, 0):
                cp.start()

        @pl.when((i == 0) | (e != prev))
        def _():
            slot = 1 - slot_ref[0]
            slot_ref[0] = slot
            for cp in weight_copies(e, slot):
                cp.wait()
            nxt = nxt_ref[e]

            @pl.when(nxt != e)
            def _():
                for cp in weight_copies(nxt, 1 - slot):
                    cp.start()

    for n in range(1, MOE_STEP_BLOCKS + 1):
        @pl.when(halves == n)
        def _(n=n):
            ffn(slice(0, n * MOE_BLOCK))


def _experts(step_expert, step_halves, n_used, next_expert, xs, w_gu, b_gu, w_down, b_down, l):
    P, DH = xs.shape
    D = 2 * DH
    R = MOE_STEP_BLOCKS * MOE_BLOCK
    NS = P // R
    F2 = 2 * EXPERT_FF

    def row_map(i, se, sh, nu, nxt):
        return (jnp.minimum(i, nu[0] - 1), 0)

    def exp_map(i, se, sh, nu, nxt):
        return (l, se[jnp.minimum(i, nu[0] - 1)], 0, 0)

    grid_spec = pltpu.PrefetchScalarGridSpec(
        num_scalar_prefetch=4,
        grid=(NS,),
        in_specs=[pl.BlockSpec((R, DH), row_map),
                  pl.BlockSpec(memory_space=pl.ANY),
                  pl.BlockSpec((None, None, 1, F2), exp_map),
                  pl.BlockSpec(memory_space=pl.ANY),
                  pl.BlockSpec((None, None, 1, D), exp_map)],
        out_specs=pl.BlockSpec((R, DH), row_map),
        scratch_shapes=[pltpu.VMEM((2, D, F2), F32), pltpu.VMEM((2, EXPERT_FF, D), F32),
                        pltpu.SMEM((1,), jnp.int32),
                        pltpu.SemaphoreType.DMA((2, 2))],
    )
    return pl.pallas_call(
        functools.partial(_expert_kernel, l),
        out_shape=jax.ShapeDtypeStruct((P, DH), jnp.uint32),
        grid_spec=grid_spec,
        compiler_params=pltpu.CompilerParams(dimension_semantics=("arbitrary",), vmem_limit_bytes=VMEM_LIMIT),
        name="experts",
    )(step_expert, step_halves, n_used, next_expert, xs, w_gu, b_gu.reshape(DEPTH, N_EXPERTS, 1, F2), w_down,
      b_down.reshape(DEPTH, N_EXPERTS, 1, D))


def _sc_gather_rows(x, idx):
    M = idx.shape[0]
    D = x.shape[1]
    W = SC_GATHER_WINDOW
    mesh = plsc.VectorSubcoreMesh(core_axis_name="core", subcore_axis_name="subcore")
    n_workers = mesh.num_cores * mesh.num_subcores
    rows_per = M // n_workers
    assert rows_per * n_workers == M and rows_per % W == 0

    @pl.kernel(out_type=jax.ShapeDtypeStruct((M, D), x.dtype), mesh=mesh, name="sc_gather_rows",
               scratch_types=[pltpu.VMEM((rows_per,), jnp.int32), pltpu.VMEM((W, D), x.dtype)])
    def gather_kernel(x_hbm, i_hbm, o_hbm, idx_vmem, buf):
        wid = lax.axis_index("core") * mesh.num_subcores + lax.axis_index("subcore")
        base = wid * rows_per
        pltpu.sync_copy(i_hbm.at[pl.ds(base, rows_per)], idx_vmem)

        @pl.loop(0, rows_per // W)
        def _(j):
            pltpu.sync_copy(x_hbm.at[idx_vmem.at[pl.ds(j * W, W)]], buf)
            pltpu.sync_copy(buf, o_hbm.at[pl.ds(base + j * W, W)])

    return gather_kernel(x, idx)


def _sc_scatter_rows(x, idx, n_out):
    K, N = idx.shape
    D = x.shape[1]
    W = SC_GATHER_WINDOW
    mesh = plsc.VectorSubcoreMesh(core_axis_name="core", subcore_axis_name="subcore")
    n_workers = mesh.num_cores * mesh.num_subcores
    rows_per = N // n_workers
    assert rows_per * n_workers == N and rows_per % W == 0

    @pl.kernel(out_type=jax.ShapeDtypeStruct((n_out, D), x.dtype), mesh=mesh, name="sc_scatter_rows",
               scratch_types=[pltpu.VMEM((K * rows_per,), jnp.int32), pltpu.VMEM((W, D), x.dtype),
                              pltpu.SemaphoreType.DMA((K,))])
    def scatter_kernel(x_hbm, i_hbm, o_hbm, idx_vmem, buf, sem):
        wid = lax.axis_index("core") * mesh.num_subcores + lax.axis_index("subcore")
        base = wid * rows_per
        for k in range(K):
            pltpu.sync_copy(i_hbm.at[pl.ds(k * N + base, rows_per)], idx_vmem.at[pl.ds(k * rows_per, rows_per)])

        @pl.loop(0, rows_per // W)
        def _(j):
            pltpu.sync_copy(x_hbm.at[pl.ds(base + j * W, W)], buf)
            copies = [pltpu.make_async_copy(buf, o_hbm.at[idx_vmem.at[pl.ds(k * rows_per + j * W, W)]], sem.at[k])
                      for k in range(K)]
            for cp in copies:
                cp.start()
            for cp in copies:
                cp.wait()

    return scatter_kernel(x, idx.reshape(K * N))


def _combine_kernel(x_ref, yg_ref, route_ref, ln3g_ref, ln3b_ref, *rest):
    out_ref = rest[-1]
    x = x_ref[...]
    g = jnp.transpose(route_ref[...])
    y = jnp.zeros_like(x)
    for k in range(TOP_K):
        y = y + g[:, TOP_K + k:TOP_K + k + 1] * _unpack_bf16_pairs(yg_ref[k * CMB_TILE:(k + 1) * CMB_TILE, :])
    out_ref[...] = _layer_norm(DEEPNORM_ALPHA * x + y, ln3g_ref[...], ln3b_ref[...])


def _combine(x2, yg, route, p, l, first, acc):
    N, D = x2.shape
    T = CMB_TILE
    tiles = yg.shape[0] // (TOP_K * T)
    in_specs = [pl.BlockSpec((T, D), lambda i: (first + i, 0)), pl.BlockSpec((TOP_K * T, D // 2), lambda i: (i, 0)),
                pl.BlockSpec((ROUTE_ROWS, T), lambda i: (0, first + i)), _layer_spec(p['ln3g'], l), _layer_spec(p['ln3b'], l)]
    args = [x2, yg, route, p['ln3g'], p['ln3b']]
    aliases = {}
    if acc is not None:
        in_specs.append(pl.BlockSpec(memory_space=pl.ANY))
        args.append(acc)
        aliases = {len(args) - 1: 0}
    return pl.pallas_call(
        _combine_kernel,
        out_shape=jax.ShapeDtypeStruct((N, D), F32),
        grid=(tiles,),
        in_specs=in_specs,
        out_specs=pl.BlockSpec((T, D), lambda i: (first + i, 0)),
        input_output_aliases=aliases,
        compiler_params=pltpu.CompilerParams(dimension_semantics=("parallel",), vmem_limit_bytes=VMEM_LIMIT),
        name="combine",
    )(*args)


def _prep(w_in, b_in, gla_wg2, gla_bg, gla_norm_g, sgu_ln_g, sgu_ln_b, sgu_ws, sgu_bs, pool_w, pool_scale,
          w_up_a, w_up_b, w_up_c, w_o, ln1_g, ln1_b, xa_wq, xa_wk, xa_wv, xa_wo, ln2_g, ln2_b,
          router_w, router_b, ln3_g, ln3_b):
    L = w_in.shape[0]
    o_glow = O_GLOW
    o_uv = o_glow + GLA_RANK
    row = lambda a: a.reshape(L, 1, -1).astype(F32)
    pad_last = lambda a, n: jnp.pad(a, [(0, 0)] * (a.ndim - 1) + [(0, n - a.shape[-1])])
    p = {}
    p['whead'] = w_in[..., :o_glow].astype(BF16)
    p['wglow'] = pad_last(w_in[..., o_glow:o_uv], LANES).astype(BF16)
    p['wtail'] = w_in[..., o_uv:].astype(BF16)
    p['bcat'] = row(jnp.concatenate([b_in[..., :o_glow], pad_last(b_in[..., o_glow:o_uv], LANES), b_in[..., o_uv:]],
                                    axis=-1))
    p['wg2'] = jnp.pad(gla_wg2, ((0, 0), (0, LANES - GLA_RANK), (0, 0))).astype(BF16)
    p['bg'] = row(gla_bg)
    p['gnorm'] = row(gla_norm_g)
    p['slng'] = row(sgu_ln_g)
    p['slnb'] = row(sgu_ln_b)
    p['wtril'] = jnp.tril(sgu_ws).astype(BF16)
    p['sbias'] = jnp.repeat(jnp.swapaxes(sgu_bs, 1, 2), SGU_GD, axis=2).astype(F32)
    G = len(POOL_WINDOWS)
    eye = jnp.eye(G, dtype=F32)
    p['poolw'] = jnp.einsum('lgcd,gh->lgchd', pool_w, eye).reshape(L, POOL_WIDTH, POOL_WIDTH).astype(BF16)
    p['pscale'] = row(pool_scale)
    p['wupa'], p['wupb'], p['wupc'], p['wo'] = w_up_a, w_up_b, w_up_c, w_o
    p['ln1g'], p['ln1b'] = row(ln1_g), row(ln1_b)
    p['wq'], p['wk'], p['wv'], p['wxo'] = xa_wq, xa_wk, xa_wv, xa_wo
    p['ln2g'], p['ln2b'] = row(ln2_g), row(ln2_b)
    rwt = jnp.swapaxes(router_w, 1, 2)
    rwt_hi = rwt.astype(BF16)
    p['rwt'] = jnp.concatenate([rwt_hi, (rwt - rwt_hi.astype(F32)).astype(BF16)], axis=1)
    p['rbt'] = jnp.broadcast_to(router_b[:, :, None], router_b.shape + (XA_TILE,)).astype(F32)
    p['ln3g'], p['ln3b'] = row(ln3_g), row(ln3_b)
    return p


def _route(route, counts):
    N = route.shape[1]
    top_idx = route[0:TOP_K].astype(jnp.int32)
    rank = route[2 * TOP_K:3 * TOP_K].astype(jnp.int32)
    counts = counts[:, 0].astype(jnp.int32)
    R = MOE_STEP_BLOCKS * MOE_BLOCK
    blocks = (counts + MOE_BLOCK - 1) // MOE_BLOCK
    padded = ((counts + R - 1) // R) * R
    pad_end = jnp.cumsum(padded)
    pad_start = pad_end - padded
    ids = jnp.arange(N_EXPERTS, dtype=jnp.int32)
    start_of = jnp.sum(jnp.where(top_idx[:, :, None] == ids[None, None, :], pad_start[None, None, :], 0), axis=-1)
    dest = start_of + rank
    n_steps = N * TOP_K // R + N_EXPERTS
    step_start = jnp.arange(n_steps, dtype=jnp.int32) * R
    step_expert = jnp.minimum(jnp.sum((pad_end[None, :] <= step_start[:, None]).astype(jnp.int32), axis=1),
                              N_EXPERTS - 1)
    mine = step_expert[:, None] == ids[None, :]
    blocks_before = (step_start - jnp.sum(jnp.where(mine, pad_start[None, :], 0), axis=1)) // MOE_BLOCK
    step_halves = jnp.clip(jnp.sum(jnp.where(mine, blocks[None, :], 0), axis=1) - blocks_before, 0, MOE_STEP_BLOCKS)
    step_halves = jnp.where(step_start < pad_end[-1], step_halves, 0).astype(jnp.int32)
    n_used = (pad_end[-1] // R).astype(jnp.int32).reshape(1)
    later = jnp.where((ids[None, :] > ids[:, None]) & (counts[None, :] > 0), ids[None, :], N_EXPERTS)
    first_later = jnp.min(later, axis=1)
    next_expert = jnp.where(first_later < N_EXPERTS, first_later, ids).astype(jnp.int32)
    return dest, step_expert, step_halves, n_used, next_expert


def kernel(x, mem, w_in, b_in, gla_wg2, gla_bg, gla_norm_g, sgu_ln_g, sgu_ln_b, sgu_ws, sgu_bs, pool_w, pool_scale, w_up_a, w_up_b, w_up_c, w_o, ln1_g, ln1_b, xa_wq, xa_wk, xa_wv, xa_wo, ln2_g, ln2_b, router_w, router_b, exp_w_gu, exp_b_gu, exp_w_down, exp_b_down, ln3_g, ln3_b):
    B, S, D = x.shape
    N = B * S
    p = _prep(w_in, b_in, gla_wg2, gla_bg, gla_norm_g, sgu_ln_g, sgu_ln_b, sgu_ws, sgu_bs, pool_w, pool_scale,
              w_up_a, w_up_b, w_up_c, w_o, ln1_g, ln1_b, xa_wq, xa_wk, xa_wv, xa_wo, ln2_g, ln2_b,
              router_w, router_b, ln3_g, ln3_b)
    memt = jnp.swapaxes(mem, 1, 2)
    for l in range(DEPTH):
        x1 = _mixer(x, p, l)
        kt, v = _memkv(mem, memt, p, l)
        x2, x2p, route, counts = _xattn(x1, kt, v, p, l)
        dest, step_expert, step_halves, n_used, next_expert = _route(route, counts)
        n_slots = N * TOP_K + N_EXPERTS * MOE_STEP_BLOCKS * MOE_BLOCK
        xs = _sc_scatter_rows(x2p.reshape(N, D // 2), dest, n_slots)
        ys = _experts(step_expert, step_halves, n_used, next_expert, xs, exp_w_gu, exp_b_gu, exp_w_down, exp_b_down, l)
        dest_km = dest.reshape(TOP_K, N // CMB_TILE, CMB_TILE).transpose(1, 0, 2).reshape(-1)
        tile_rows = TOP_K * CMB_TILE
        acc, first = None, 0
        for share in COMBINE_SHARES:
            tiles = (N // CMB_TILE) * share // sum(COMBINE_SHARES)
            yg = _sc_gather_rows(ys, dest_km[first * tile_rows:(first + tiles) * tile_rows])
            acc = _combine(x2.reshape(N, D), yg, route, p, l, first, acc)
            first += tiles
        x = acc.reshape(B, S, D)
    return x
```
